```python
import math, functools
import jax, jax.numpy as jnp
from jax import lax
import numpy as np

D_MODEL = 1024
BATCH = 8
SEQ = 8192
DEPTH = 2

ATT_HEADS = 8
ATT_HEAD_DIM = 64
ATT_WIDTH = ATT_HEADS * ATT_HEAD_DIM
ATT_PATTERNS = ((128, 1), (512, 4), (2048, 16))
ATT_BLOCK = 128
SSD_HEADS = 8
SSD_HEAD_DIM = 64
SSD_WIDTH = SSD_HEADS * SSD_HEAD_DIM
SSD_GROUPS = 2
SSD_STATE = 128
SSD_CHUNK = 128
SSD_CONV_CH = SSD_WIDTH + 2 * SSD_GROUPS * SSD_STATE
DT_MIN = 0.001
DT_MAX = 0.1
LRU_WIDTH = 512
LRU_BLOCKS = 8
LRU_BLOCK_W = LRU_WIDTH // LRU_BLOCKS
LRU_C = 8.0
CONV_WIDTH = 4
D_MIX = ATT_WIDTH + SSD_WIDTH + LRU_WIDTH
IN_COLS = 3 * ATT_WIDTH + SSD_WIDTH + SSD_CONV_CH + SSD_HEADS + 2 * LRU_WIDTH
D_FF = ((8 * D_MODEL + 3 * 256 - 1) // (3 * 256)) * 256
NORM_EPS = 1e-6
SSD_NORM_EPS = 1e-5

kernel_name = 'hybrid_dilated_attn_ssd_rglru_block'


def rmsnorm(x, g, eps=NORM_EPS):
    xf = x.astype(jnp.float32)
    y = xf * lax.rsqrt(jnp.mean(xf * xf, axis=-1, keepdims=True) + eps)
    return (y * g.astype(jnp.float32)).astype(x.dtype)


def causal_depthwise_conv(x, w, b):
    k_width, s = w.shape[0], x.shape[1]
    xp = jnp.pad(x, ((0, 0), (k_width - 1, 0), (0, 0)))
    y = b + xp[:, k_width - 1:] * w[k_width - 1]
    for k in range(k_width - 1):
        y = y + xp[:, k:k + s] * w[k]
    return y


def split_cols(proj):
    sizes = (ATT_WIDTH, ATT_WIDTH, ATT_WIDTH, SSD_WIDTH, SSD_CONV_CH, SSD_HEADS, LRU_WIDTH, LRU_WIDTH)
    offsets = np.cumsum(sizes)[:-1].tolist()
    return jnp.split(proj, offsets, axis=-1)


def dilated_branch(q, k, v, window, dilation, slopes):
    b, s, h, dh = q.shape
    length = s // dilation
    nb = -(-length // ATT_BLOCK)
    pad = nb * ATT_BLOCK - length

    def to_blocks(t):
        t = t.reshape(b, length, dilation, h, dh).transpose(0, 2, 3, 1, 4)
        t = jnp.pad(t, ((0, 0), (0, 0), (0, 0), (0, pad), (0, 0)))
        return t.reshape(b, dilation, h, nb, ATT_BLOCK, dh)

    def with_prev(t):
        prev = jnp.pad(t[:, :, :, :-1], ((0, 0), (0, 0), (0, 0), (1, 0), (0, 0), (0, 0)))
        return jnp.concatenate([prev, t], axis=4)

    qb, kb, vb = to_blocks(q), to_blocks(k), to_blocks(v)
    kk, vv = with_prev(kb), with_prev(vb)
    scores = jnp.einsum('brhnqd,brhnkd->brhnqk', qb, kk,
                        preferred_element_type=jnp.float32) * (ATT_HEAD_DIM ** -0.5)
    qi = jnp.arange(ATT_BLOCK)[:, None]
    ki = jnp.arange(2 * ATT_BLOCK)[None, :]
    dist = ATT_BLOCK + qi - ki
    band = (dist >= 0) & (dist <= window // dilation)
    valid = band[None] & ((jnp.arange(nb)[:, None, None] > 0) | (ki[None] >= ATT_BLOCK))
    alibi = -slopes[:, None, None] * (dilation * dist).astype(jnp.float32)
    scores = scores + alibi[None, None, :, None]
    scores = jnp.where(valid[None, None, None], scores, -jnp.inf)
    m = jnp.max(scores, axis=-1)
    p = jnp.exp(scores - m[..., None])
    den = jnp.sum(p, axis=-1)
    num = jnp.einsum('brhnqk,brhnkd->brhnqd', p, vv.astype(jnp.float32))

    def from_blocks(t):
        t = t.reshape(b, dilation, h, nb * ATT_BLOCK, *t.shape[5:])[:, :, :, :length]
        t = jnp.moveaxis(t, 3, 1)
        return t.reshape(b, s, h, *t.shape[4:])

    return from_blocks(num), from_blocks(m), from_blocks(den)


def dilated_attention(q, k, v):
    b, s, _ = q.shape
    shp = (b, s, ATT_HEADS, ATT_HEAD_DIM)
    q, k, v = q.reshape(shp), k.reshape(shp), v.reshape(shp)
    slopes = jnp.exp2(-8.0 * jnp.arange(1, ATT_HEADS + 1, dtype=jnp.float32) / ATT_HEADS)
    branches = [dilated_branch(q, k, v, w, d, slopes) for (w, d) in ATT_PATTERNS]
    m_all = functools.reduce(jnp.maximum, [br[1] for br in branches])
    num = jnp.zeros(shp, jnp.float32)
    den = jnp.zeros(shp[:3], jnp.float32)
    for n_g, m_g, d_g in branches:
        e = jnp.exp(m_g - m_all)
        num = num + n_g * e[..., None]
        den = den + d_g * e
    return (num / den[..., None]).reshape(b, s, ATT_WIDTH)


def ssd_chunked_scan(x, dt, a, bm, cm):
    b, s, h, p = x.shape
    n = bm.shape[-1]
    q = SSD_CHUNK
    nc = s // q
    rep = h // bm.shape[2]
    x = x.reshape(b, nc, q, h, p)
    dt = dt.reshape(b, nc, q, h)
    bh = jnp.repeat(bm, rep, axis=2).reshape(b, nc, q, h, n)
    ch = jnp.repeat(cm, rep, axis=2).reshape(b, nc, q, h, n)
    acs = jnp.cumsum(dt * a, axis=2)
    seg = acs[:, :, :, None, :] - acs[:, :, None, :, :]
    causal = jnp.tril(jnp.ones((q, q), dtype=bool))[None, None, :, :, None]
    lmat = jnp.exp(jnp.where(causal, seg, -jnp.inf))
    scores = jnp.einsum('bcihn,bcjhn->bcijh', ch, bh) * lmat * dt[:, :, None, :, :]
    y_diag = jnp.einsum('bcijh,bcjhp->bcihp', scores, x)
    decay_to_end = jnp.exp(acs[:, :, -1:, :] - acs)
    states = jnp.einsum('bcjhn,bcjh,bcjhp->bchpn', bh, decay_to_end * dt, x)
    chunk_decay = jnp.exp(acs[:, :, -1, :])

    def step(carry, inp):
        st, dc = inp
        return dc[:, :, None, None] * carry + st, carry

    _, prev = lax.scan(step, jnp.zeros((b, h, p, n), x.dtype),
                       (jnp.moveaxis(states, 1, 0), jnp.moveaxis(chunk_decay, 1, 0)))
    prev = jnp.moveaxis(prev, 0, 1)
    y_off = jnp.einsum('bcihn,bchpn,bcih->bcihp', ch, prev, jnp.exp(acs))
    return (y_diag + y_off).reshape(b, s, h, p)


def ssd_mixer(z, xbc, dt_raw, conv_w, conv_b, dt_bias, a_log, d_skip, norm_w):
    b, s, _ = z.shape
    f32 = jnp.float32
    xbc = jax.nn.silu(causal_depthwise_conv(xbc, conv_w, conv_b)).astype(f32)
    xs, bm, cm = jnp.split(xbc, [SSD_WIDTH, SSD_WIDTH + SSD_GROUPS * SSD_STATE], axis=-1)
    xs = xs.reshape(b, s, SSD_HEADS, SSD_HEAD_DIM)
    bm = bm.reshape(b, s, SSD_GROUPS, SSD_STATE)
    cm = cm.reshape(b, s, SSD_GROUPS, SSD_STATE)
    dt = jax.nn.softplus(dt_raw.astype(f32) + dt_bias.astype(f32))
    a = -jnp.exp(a_log.astype(f32))
    y = ssd_chunked_scan(xs, dt, a, bm, cm) + d_skip.astype(f32)[:, None] * xs
    y = y.reshape(b, s, SSD_WIDTH) * jax.nn.silu(z.astype(f32))
    yg = y.reshape(b, s, SSD_GROUPS, SSD_WIDTH // SSD_GROUPS)
    yg = yg * lax.rsqrt(jnp.mean(yg * yg, axis=-1, keepdims=True) + SSD_NORM_EPS)
    return yg.reshape(b, s, SSD_WIDTH) * norm_w.astype(f32)


def rglru_mixer(gate_in, x_in, conv_w, conv_b, wa, ba, wx, bx, lam):
    b, s, _ = x_in.shape
    f32 = jnp.float32
    gate = jax.nn.gelu(gate_in.astype(f32))
    xc = causal_depthwise_conv(x_in, conv_w, conv_b).astype(f32)
    xb = xc.reshape(b, s, LRU_BLOCKS, LRU_BLOCK_W)
    r = jax.nn.sigmoid(jnp.einsum('bsnc,ncd->bsnd', xb, wa.astype(f32)).reshape(b, s, LRU_WIDTH) + ba.astype(f32))
    i = jax.nn.sigmoid(jnp.einsum('bsnc,ncd->bsnd', xb, wx.astype(f32)).reshape(b, s, LRU_WIDTH) + bx.astype(f32))
    log_a = -LRU_C * r * jax.nn.softplus(-lam.astype(f32))
    a = jnp.exp(log_a)
    u = jnp.sqrt(-jnp.expm1(2.0 * log_a)) * (i * xc)

    def combine(left, right):
        a_l, h_l = left
        a_r, h_r = right
        return a_l * a_r, a_r * h_l + h_r

    _, h = lax.associative_scan(combine, (a, u), axis=1)
    return h * gate


def swiglu(h, w_gate, w_up, w_down):
    return (jax.nn.silu(h @ w_gate) * (h @ w_up)) @ w_down


def hybrid_layer(x, norm_mix, w_in, ssd_conv_w, ssd_conv_b, ssd_dt_bias, ssd_a_log, ssd_d, ssd_norm,
                 lru_conv_w, lru_conv_b, lru_wa, lru_ba, lru_wx, lru_bx, lru_lambda, w_out,
                 norm_ffn, w_gate, w_up, w_down):
    h = rmsnorm(x, norm_mix)
    q, k, v, z, xbc, dt_raw, g_lru, x_lru = split_cols(h @ w_in)
    att = dilated_attention(q, k, v).astype(x.dtype)
    ssd = ssd_mixer(z, xbc, dt_raw, ssd_conv_w, ssd_conv_b, ssd_dt_bias, ssd_a_log, ssd_d, ssd_norm).astype(x.dtype)
    lru = rglru_mixer(g_lru, x_lru, lru_conv_w, lru_conv_b, lru_wa, lru_ba, lru_wx, lru_bx, lru_lambda).astype(x.dtype)
    x = x + jnp.concatenate([att, ssd, lru], axis=-1) @ w_out
    x = x + swiglu(rmsnorm(x, norm_ffn), w_gate, w_up, w_down)
    return x


def _fwd_setup_inputs(seed: int = 0) -> dict:
    key = jax.random.key(seed)
    ks = jax.random.split(key, 24)
    f32 = jnp.float32
    L = DEPTH

    def nrm(k, shape, scale):
        return scale * jax.random.normal(k, shape, f32)

    x = nrm(ks[0], (BATCH, SEQ, D_MODEL), 1.0)
    norm_mix = 1.0 + nrm(ks[1], (L, D_MODEL), 0.05)
    w_in = nrm(ks[2], (L, D_MODEL, IN_COLS), D_MODEL ** -0.5)
    ssd_conv_w = nrm(ks[3], (L, CONV_WIDTH, SSD_CONV_CH), CONV_WIDTH ** -0.5)
    ssd_conv_b = nrm(ks[4], (L, SSD_CONV_CH), 0.02)
    dt0 = jnp.exp(jax.random.uniform(ks[5], (L, SSD_HEADS), f32, math.log(DT_MIN), math.log(DT_MAX)))
    ssd_dt_bias = dt0 + jnp.log(-jnp.expm1(-dt0))
    ssd_a_log = jnp.log(jax.random.uniform(ks[6], (L, SSD_HEADS), f32, 1.0, 16.0))
    ssd_d = 1.0 + nrm(ks[7], (L, SSD_HEADS), 0.1)
    ssd_norm = 1.0 + nrm(ks[8], (L, SSD_WIDTH), 0.05)
    lru_conv_w = nrm(ks[9], (L, CONV_WIDTH, LRU_WIDTH), CONV_WIDTH ** -0.5)
    lru_conv_b = nrm(ks[10], (L, LRU_WIDTH), 0.02)
    lru_wa = nrm(ks[11], (L, LRU_BLOCKS, LRU_BLOCK_W, LRU_BLOCK_W), LRU_BLOCK_W ** -0.5)
    lru_ba = nrm(ks[12], (L, LRU_WIDTH), 0.02)
    lru_wx = nrm(ks[13], (L, LRU_BLOCKS, LRU_BLOCK_W, LRU_BLOCK_W), LRU_BLOCK_W ** -0.5)
    lru_bx = nrm(ks[14], (L, LRU_WIDTH), 0.02)
    a_c = jax.random.uniform(ks[15], (L, LRU_WIDTH), f32, 0.9, 0.999)
    a_base = a_c ** (1.0 / LRU_C)
    lru_lambda = jnp.log(a_base) - jnp.log1p(-a_base)
    w_out = nrm(ks[16], (L, D_MIX, D_MODEL), D_MIX ** -0.5)
    norm_ffn = 1.0 + nrm(ks[17], (L, D_MODEL), 0.05)
    w_gate = nrm(ks[18], (L, D_MODEL, D_FF), D_MODEL ** -0.5)
    w_up = nrm(ks[19], (L, D_MODEL, D_FF), D_MODEL ** -0.5)
    w_down = nrm(ks[20], (L, D_FF, D_MODEL), D_FF ** -0.5)
    norm_final = 1.0 + nrm(ks[21], (D_MODEL,), 0.05)
    return {'x': x, 'norm_mix': norm_mix, 'w_in': w_in, 'ssd_conv_w': ssd_conv_w, 'ssd_conv_b': ssd_conv_b,
            'ssd_dt_bias': ssd_dt_bias, 'ssd_a_log': ssd_a_log, 'ssd_d': ssd_d, 'ssd_norm': ssd_norm,
            'lru_conv_w': lru_conv_w, 'lru_conv_b': lru_conv_b, 'lru_wa': lru_wa, 'lru_ba': lru_ba,
            'lru_wx': lru_wx, 'lru_bx': lru_bx, 'lru_lambda': lru_lambda, 'w_out': w_out,
            'norm_ffn': norm_ffn, 'w_gate': w_gate, 'w_up': w_up, 'w_down': w_down, 'norm_final': norm_final}


def _fwd_reference(x, norm_mix, w_in, ssd_conv_w, ssd_conv_b, ssd_dt_bias, ssd_a_log, ssd_d, ssd_norm,
              lru_conv_w, lru_conv_b, lru_wa, lru_ba, lru_wx, lru_bx, lru_lambda, w_out,
              norm_ffn, w_gate, w_up, w_down, norm_final):
    for l in range(DEPTH):
        x = hybrid_layer(x, norm_mix[l], w_in[l], ssd_conv_w[l], ssd_conv_b[l], ssd_dt_bias[l], ssd_a_log[l],
                         ssd_d[l], ssd_norm[l], lru_conv_w[l], lru_conv_b[l], lru_wa[l], lru_ba[l],
                         lru_wx[l], lru_bx[l], lru_lambda[l], w_out[l], norm_ffn[l], w_gate[l], w_up[l], w_down[l])
    return rmsnorm(x, norm_final)


import jax as _jax
import jax.numpy as _jnp

TWIN_FORMAT = 'train_step'
FWD_PARAMS = ['x', 'norm_mix', 'w_in', 'ssd_conv_w', 'ssd_conv_b', 'ssd_dt_bias', 'ssd_a_log', 'ssd_d', 'ssd_norm', 'lru_conv_w', 'lru_conv_b', 'lru_wa', 'lru_ba', 'lru_wx', 'lru_bx', 'lru_lambda', 'w_out', 'norm_ffn', 'w_gate', 'w_up', 'w_down', 'norm_final']
TWIN_WEIGHTS = ['norm_mix', 'w_in', 'ssd_conv_w', 'ssd_conv_b', 'ssd_dt_bias', 'ssd_a_log', 'ssd_d', 'ssd_norm', 'lru_conv_w', 'lru_conv_b', 'lru_wa', 'lru_ba', 'lru_wx', 'lru_bx', 'lru_lambda', 'w_out', 'norm_ffn', 'w_gate', 'w_up', 'w_down', 'norm_final']
TWIN_DIFF_INPUT = 'x'
TWIN_INPUTS = ['x', 'norm_mix', 'w_in', 'ssd_conv_w', 'ssd_conv_b', 'ssd_dt_bias', 'ssd_a_log', 'ssd_d', 'ssd_norm', 'lru_conv_w', 'lru_conv_b', 'lru_wa', 'lru_ba', 'lru_wx', 'lru_bx', 'lru_lambda', 'w_out', 'norm_ffn', 'w_gate', 'w_up', 'w_down', 'norm_final', 'loss_target', 'm_norm_mix', 'm_w_in', 'm_ssd_conv_w', 'm_ssd_conv_b', 'm_ssd_dt_bias', 'm_ssd_a_log', 'm_ssd_d', 'm_ssd_norm', 'm_lru_conv_w', 'm_lru_conv_b', 'm_lru_wa', 'm_lru_ba', 'm_lru_wx', 'm_lru_bx', 'm_lru_lambda', 'm_w_out', 'm_norm_ffn', 'm_w_gate', 'm_w_up', 'm_w_down', 'm_norm_final', 'v_norm_mix', 'v_w_in', 'v_ssd_conv_w', 'v_ssd_conv_b', 'v_ssd_dt_bias', 'v_ssd_a_log', 'v_ssd_d', 'v_ssd_norm', 'v_lru_conv_w', 'v_lru_conv_b', 'v_lru_wa', 'v_lru_ba', 'v_lru_wx', 'v_lru_bx', 'v_lru_lambda', 'v_w_out', 'v_norm_ffn', 'v_w_gate', 'v_w_up', 'v_w_down', 'v_norm_final']
TWIN_OUTPUTS = ['loss', 'grad_x', 'grad_norm_mix', 'grad_w_in', 'grad_ssd_conv_w', 'grad_ssd_conv_b', 'grad_ssd_dt_bias', 'grad_ssd_a_log', 'grad_ssd_d', 'grad_ssd_norm', 'grad_lru_conv_w', 'grad_lru_conv_b', 'grad_lru_wa', 'grad_lru_ba', 'grad_lru_wx', 'grad_lru_bx', 'grad_lru_lambda', 'grad_w_out', 'grad_norm_ffn', 'grad_w_gate', 'grad_w_up', 'grad_w_down', 'grad_norm_final', 'delta_norm_mix', 'delta_w_in', 'delta_ssd_conv_w', 'delta_ssd_conv_b', 'delta_ssd_dt_bias', 'delta_ssd_a_log', 'delta_ssd_d', 'delta_ssd_norm', 'delta_lru_conv_w', 'delta_lru_conv_b', 'delta_lru_wa', 'delta_lru_ba', 'delta_lru_wx', 'delta_lru_bx', 'delta_lru_lambda', 'delta_w_out', 'delta_norm_ffn', 'delta_w_gate', 'delta_w_up', 'delta_w_down', 'delta_norm_final', 'new_m_norm_mix', 'new_m_w_in', 'new_m_ssd_conv_w', 'new_m_ssd_conv_b', 'new_m_ssd_dt_bias', 'new_m_ssd_a_log', 'new_m_ssd_d', 'new_m_ssd_norm', 'new_m_lru_conv_w', 'new_m_lru_conv_b', 'new_m_lru_wa', 'new_m_lru_ba', 'new_m_lru_wx', 'new_m_lru_bx', 'new_m_lru_lambda', 'new_m_w_out', 'new_m_norm_ffn', 'new_m_w_gate', 'new_m_w_up', 'new_m_w_down', 'new_m_norm_final', 'new_v_norm_mix', 'new_v_w_in', 'new_v_ssd_conv_w', 'new_v_ssd_conv_b', 'new_v_ssd_dt_bias', 'new_v_ssd_a_log', 'new_v_ssd_d', 'new_v_ssd_norm', 'new_v_lru_conv_w', 'new_v_lru_conv_b', 'new_v_lru_wa', 'new_v_lru_ba', 'new_v_lru_wx', 'new_v_lru_bx', 'new_v_lru_lambda', 'new_v_w_out', 'new_v_norm_ffn', 'new_v_w_gate', 'new_v_w_up', 'new_v_w_down', 'new_v_norm_final']
TWIN_LEAF_KINDS = {'loss': 'loss', 'grad_x': 'grad_x', 'grad_norm_mix': 'grad_w', 'grad_w_in': 'grad_w', 'grad_ssd_conv_w': 'grad_w', 'grad_ssd_conv_b': 'grad_w', 'grad_ssd_dt_bias': 'grad_w', 'grad_ssd_a_log': 'grad_w', 'grad_ssd_d': 'grad_w', 'grad_ssd_norm': 'grad_w', 'grad_lru_conv_w': 'grad_w', 'grad_lru_conv_b': 'grad_w', 'grad_lru_wa': 'grad_w', 'grad_lru_ba': 'grad_w', 'grad_lru_wx': 'grad_w', 'grad_lru_bx': 'grad_w', 'grad_lru_lambda': 'grad_w', 'grad_w_out': 'grad_w', 'grad_norm_ffn': 'grad_w', 'grad_w_gate': 'grad_w', 'grad_w_up': 'grad_w', 'grad_w_down': 'grad_w', 'grad_norm_final': 'grad_w', 'delta_norm_mix': 'delta_w', 'delta_w_in': 'delta_w', 'delta_ssd_conv_w': 'delta_w', 'delta_ssd_conv_b': 'delta_w', 'delta_ssd_dt_bias': 'delta_w', 'delta_ssd_a_log': 'delta_w', 'delta_ssd_d': 'delta_w', 'delta_ssd_norm': 'delta_w', 'delta_lru_conv_w': 'delta_w', 'delta_lru_conv_b': 'delta_w', 'delta_lru_wa': 'delta_w', 'delta_lru_ba': 'delta_w', 'delta_lru_wx': 'delta_w', 'delta_lru_bx': 'delta_w', 'delta_lru_lambda': 'delta_w', 'delta_w_out': 'delta_w', 'delta_norm_ffn': 'delta_w', 'delta_w_gate': 'delta_w', 'delta_w_up': 'delta_w', 'delta_w_down': 'delta_w', 'delta_norm_final': 'delta_w', 'new_m_norm_mix': 'new_m', 'new_m_w_in': 'new_m', 'new_m_ssd_conv_w': 'new_m', 'new_m_ssd_conv_b': 'new_m', 'new_m_ssd_dt_bias': 'new_m', 'new_m_ssd_a_log': 'new_m', 'new_m_ssd_d': 'new_m', 'new_m_ssd_norm': 'new_m', 'new_m_lru_conv_w': 'new_m', 'new_m_lru_conv_b': 'new_m', 'new_m_lru_wa': 'new_m', 'new_m_lru_ba': 'new_m', 'new_m_lru_wx': 'new_m', 'new_m_lru_bx': 'new_m', 'new_m_lru_lambda': 'new_m', 'new_m_w_out': 'new_m', 'new_m_norm_ffn': 'new_m', 'new_m_w_gate': 'new_m', 'new_m_w_up': 'new_m', 'new_m_w_down': 'new_m', 'new_m_norm_final': 'new_m', 'new_v_norm_mix': 'new_v', 'new_v_w_in': 'new_v', 'new_v_ssd_conv_w': 'new_v', 'new_v_ssd_conv_b': 'new_v', 'new_v_ssd_dt_bias': 'new_v', 'new_v_ssd_a_log': 'new_v', 'new_v_ssd_d': 'new_v', 'new_v_ssd_norm': 'new_v', 'new_v_lru_conv_w': 'new_v', 'new_v_lru_conv_b': 'new_v', 'new_v_lru_wa': 'new_v', 'new_v_lru_ba': 'new_v', 'new_v_lru_wx': 'new_v', 'new_v_lru_bx': 'new_v', 'new_v_lru_lambda': 'new_v', 'new_v_w_out': 'new_v', 'new_v_norm_ffn': 'new_v', 'new_v_w_gate': 'new_v', 'new_v_w_up': 'new_v', 'new_v_w_down': 'new_v', 'new_v_norm_final': 'new_v'}


def _forward(args):
    return _fwd_reference(*[args[k] for k in FWD_PARAMS])


def _output_shape():
    def fwd():
        inp = _fwd_setup_inputs(0)
        return _fwd_reference(*[inp[k] for k in FWD_PARAMS])
    out = _jax.eval_shape(fwd)
    return out.shape, out.dtype

N_MICROBATCH = 1
ADAM_LR = 0.001
ADAM_B1 = 0.9
ADAM_B2 = 0.999
ADAM_EPS = 1e-08
ADAM_WD = 0.01
ADAM_STEP = 10
PER_EXAMPLE_BATCH_AXIS = {'x': 0, 'loss_target': 0}
SHARED_INPUTS = []
_WEIGHT_DTYPES = {'norm_mix': _jnp.float32, 'w_in': _jnp.float32, 'ssd_conv_w': _jnp.float32, 'ssd_conv_b': _jnp.float32, 'ssd_dt_bias': _jnp.float32, 'ssd_a_log': _jnp.float32, 'ssd_d': _jnp.float32, 'ssd_norm': _jnp.float32, 'lru_conv_w': _jnp.float32, 'lru_conv_b': _jnp.float32, 'lru_wa': _jnp.float32, 'lru_ba': _jnp.float32, 'lru_wx': _jnp.float32, 'lru_bx': _jnp.float32, 'lru_lambda': _jnp.float32, 'w_out': _jnp.float32, 'norm_ffn': _jnp.float32, 'w_gate': _jnp.float32, 'w_up': _jnp.float32, 'w_down': _jnp.float32, 'norm_final': _jnp.float32}
MOMENT_SCALE = {'norm_mix': 2.461891e-01, 'w_in': 1.189613e-01, 'ssd_conv_w': 1.510074e-01, 'ssd_conv_b': 2.113536e-01, 'ssd_dt_bias': 5.608912e-01, 'ssd_a_log': 4.320613e-01, 'ssd_d': 1.564331e+00, 'ssd_norm': 1.916497e-01, 'lru_conv_w': 1.115735e-01, 'lru_conv_b': 1.128640e+00, 'lru_wa': 4.043635e-02, 'lru_ba': 3.399353e-02, 'lru_wx': 7.398773e-02, 'lru_bx': 3.718024e-02, 'lru_lambda': 6.505491e-02, 'w_out': 1.724087e-01, 'norm_ffn': 1.681760e-01, 'w_gate': 7.189760e-02, 'w_up': 6.976359e-02, 'w_down': 1.156066e-01, 'norm_final': 6.421724e+01}


def _to_microbatches(a, axis):
    t = _jnp.moveaxis(a, axis, 0)
    t = t.reshape((N_MICROBATCH, t.shape[0] // N_MICROBATCH) + t.shape[1:])
    return _jnp.moveaxis(t, 1, axis + 1)


def setup_inputs(seed: int = 0) -> dict:
    inp = _fwd_setup_inputs(seed)
    key = _jax.random.fold_in(_jax.random.key(seed), 7919)
    shape, _ = _output_shape()
    out = dict(inp)
    out["loss_target"] = _jax.random.normal(_jax.random.fold_in(key, 0), shape, _jnp.float32)
    for i, name in enumerate(TWIN_WEIGHTS):
        w = inp[name].astype(_jnp.float32)
        if MOMENT_SCALE is None:
            s = _jnp.sqrt(_jnp.mean(_jnp.square(w)) + 1e-30)
        else:
            s = MOMENT_SCALE[name]
        km, kv = _jax.random.split(_jax.random.fold_in(key, i + 1))
        out[name] = w
        out["m_" + name] = s * _jax.random.normal(km, w.shape, _jnp.float32)
        out["v_" + name] = (s * s) * _jax.random.uniform(kv, w.shape, _jnp.float32, 0.5, 1.5)
    if N_MICROBATCH > 1:
        for name, axis in PER_EXAMPLE_BATCH_AXIS.items():
            out[name] = _to_microbatches(out[name], axis)
    return {'x': out['x'], 'norm_mix': out['norm_mix'], 'w_in': out['w_in'], 'ssd_conv_w': out['ssd_conv_w'], 'ssd_conv_b': out['ssd_conv_b'], 'ssd_dt_bias': out['ssd_dt_bias'], 'ssd_a_log': out['ssd_a_log'], 'ssd_d': out['ssd_d'], 'ssd_norm': out['ssd_norm'], 'lru_conv_w': out['lru_conv_w'], 'lru_conv_b': out['lru_conv_b'], 'lru_wa': out['lru_wa'], 'lru_ba': out['lru_ba'], 'lru_wx': out['lru_wx'], 'lru_bx': out['lru_bx'], 'lru_lambda': out['lru_lambda'], 'w_out': out['w_out'], 'norm_ffn': out['norm_ffn'], 'w_gate': out['w_gate'], 'w_up': out['w_up'], 'w_down': out['w_down'], 'norm_final': out['norm_final'], 'loss_target': out['loss_target'], 'm_norm_mix': out['m_norm_mix'], 'm_w_in': out['m_w_in'], 'm_ssd_conv_w': out['m_ssd_conv_w'], 'm_ssd_conv_b': out['m_ssd_conv_b'], 'm_ssd_dt_bias': out['m_ssd_dt_bias'], 'm_ssd_a_log': out['m_ssd_a_log'], 'm_ssd_d': out['m_ssd_d'], 'm_ssd_norm': out['m_ssd_norm'], 'm_lru_conv_w': out['m_lru_conv_w'], 'm_lru_conv_b': out['m_lru_conv_b'], 'm_lru_wa': out['m_lru_wa'], 'm_lru_ba': out['m_lru_ba'], 'm_lru_wx': out['m_lru_wx'], 'm_lru_bx': out['m_lru_bx'], 'm_lru_lambda': out['m_lru_lambda'], 'm_w_out': out['m_w_out'], 'm_norm_ffn': out['m_norm_ffn'], 'm_w_gate': out['m_w_gate'], 'm_w_up': out['m_w_up'], 'm_w_down': out['m_w_down'], 'm_norm_final': out['m_norm_final'], 'v_norm_mix': out['v_norm_mix'], 'v_w_in': out['v_w_in'], 'v_ssd_conv_w': out['v_ssd_conv_w'], 'v_ssd_conv_b': out['v_ssd_conv_b'], 'v_ssd_dt_bias': out['v_ssd_dt_bias'], 'v_ssd_a_log': out['v_ssd_a_log'], 'v_ssd_d': out['v_ssd_d'], 'v_ssd_norm': out['v_ssd_norm'], 'v_lru_conv_w': out['v_lru_conv_w'], 'v_lru_conv_b': out['v_lru_conv_b'], 'v_lru_wa': out['v_lru_wa'], 'v_lru_ba': out['v_lru_ba'], 'v_lru_wx': out['v_lru_wx'], 'v_lru_bx': out['v_lru_bx'], 'v_lru_lambda': out['v_lru_lambda'], 'v_w_out': out['v_w_out'], 'v_norm_ffn': out['v_norm_ffn'], 'v_w_gate': out['v_w_gate'], 'v_w_up': out['v_w_up'], 'v_w_down': out['v_w_down'], 'v_norm_final': out['v_norm_final']}


def _loss(weights, diff, rest, loss_target):
    with _jax.named_scope("forward"):
        args = {**rest, TWIN_DIFF_INPUT: diff, **{k: w.astype(_WEIGHT_DTYPES[k]) for k, w in weights.items()}}
        y = _forward(args)
    with _jax.named_scope("loss_head"):
        err = _jnp.square(y.astype(_jnp.float32) - loss_target)
        return 0.5 * _jnp.sum(_jnp.mean(err, axis=-1)) if err.ndim else 0.5 * err


def _adamw(w, g, m, v):
    m = ADAM_B1 * m + (1.0 - ADAM_B1) * g
    v = ADAM_B2 * v + (1.0 - ADAM_B2) * _jnp.square(g)
    m_hat = m / (1.0 - ADAM_B1 ** ADAM_STEP)
    v_hat = v / (1.0 - ADAM_B2 ** ADAM_STEP)
    delta = -ADAM_LR * (m_hat / (_jnp.sqrt(v_hat) + ADAM_EPS) + ADAM_WD * w)
    return delta, m, v


def reference(x, norm_mix, w_in, ssd_conv_w, ssd_conv_b, ssd_dt_bias, ssd_a_log, ssd_d, ssd_norm, lru_conv_w, lru_conv_b, lru_wa, lru_ba, lru_wx, lru_bx, lru_lambda, w_out, norm_ffn, w_gate, w_up, w_down, norm_final, loss_target, m_norm_mix, m_w_in, m_ssd_conv_w, m_ssd_conv_b, m_ssd_dt_bias, m_ssd_a_log, m_ssd_d, m_ssd_norm, m_lru_conv_w, m_lru_conv_b, m_lru_wa, m_lru_ba, m_lru_wx, m_lru_bx, m_lru_lambda, m_w_out, m_norm_ffn, m_w_gate, m_w_up, m_w_down, m_norm_final, v_norm_mix, v_w_in, v_ssd_conv_w, v_ssd_conv_b, v_ssd_dt_bias, v_ssd_a_log, v_ssd_d, v_ssd_norm, v_lru_conv_w, v_lru_conv_b, v_lru_wa, v_lru_ba, v_lru_wx, v_lru_bx, v_lru_lambda, v_w_out, v_norm_ffn, v_w_gate, v_w_up, v_w_down, v_norm_final):
    given = dict(x=x, norm_mix=norm_mix, w_in=w_in, ssd_conv_w=ssd_conv_w, ssd_conv_b=ssd_conv_b, ssd_dt_bias=ssd_dt_bias, ssd_a_log=ssd_a_log, ssd_d=ssd_d, ssd_norm=ssd_norm, lru_conv_w=lru_conv_w, lru_conv_b=lru_conv_b, lru_wa=lru_wa, lru_ba=lru_ba, lru_wx=lru_wx, lru_bx=lru_bx, lru_lambda=lru_lambda, w_out=w_out, norm_ffn=norm_ffn, w_gate=w_gate, w_up=w_up, w_down=w_down, norm_final=norm_final, loss_target=loss_target, m_norm_mix=m_norm_mix, m_w_in=m_w_in, m_ssd_conv_w=m_ssd_conv_w, m_ssd_conv_b=m_ssd_conv_b, m_ssd_dt_bias=m_ssd_dt_bias, m_ssd_a_log=m_ssd_a_log, m_ssd_d=m_ssd_d, m_ssd_norm=m_ssd_norm, m_lru_conv_w=m_lru_conv_w, m_lru_conv_b=m_lru_conv_b, m_lru_wa=m_lru_wa, m_lru_ba=m_lru_ba, m_lru_wx=m_lru_wx, m_lru_bx=m_lru_bx, m_lru_lambda=m_lru_lambda, m_w_out=m_w_out, m_norm_ffn=m_norm_ffn, m_w_gate=m_w_gate, m_w_up=m_w_up, m_w_down=m_w_down, m_norm_final=m_norm_final, v_norm_mix=v_norm_mix, v_w_in=v_w_in, v_ssd_conv_w=v_ssd_conv_w, v_ssd_conv_b=v_ssd_conv_b, v_ssd_dt_bias=v_ssd_dt_bias, v_ssd_a_log=v_ssd_a_log, v_ssd_d=v_ssd_d, v_ssd_norm=v_ssd_norm, v_lru_conv_w=v_lru_conv_w, v_lru_conv_b=v_lru_conv_b, v_lru_wa=v_lru_wa, v_lru_ba=v_lru_ba, v_lru_wx=v_lru_wx, v_lru_bx=v_lru_bx, v_lru_lambda=v_lru_lambda, v_w_out=v_w_out, v_norm_ffn=v_norm_ffn, v_w_gate=v_w_gate, v_w_up=v_w_up, v_w_down=v_w_down, v_norm_final=v_norm_final)
    weights = {n: given[n] for n in TWIN_WEIGHTS}
    shared = {n: given[n] for n in SHARED_INPUTS}
    per_example = {n: given[n] for n in ['x']}
    grad_fn = _jax.value_and_grad(_loss, argnums=(0, 1))

    def one_microbatch(ex, loss_target):
        ex = dict(ex)
        diff = ex.pop(TWIN_DIFF_INPUT)
        return grad_fn(weights, diff, {**shared, **ex}, loss_target)

    if N_MICROBATCH == 1:
        loss, (grad_w, grad_x) = one_microbatch(per_example, given["loss_target"])
    else:
        def body(carry, xs):
            loss_sum, grad_sum = carry
            l_k, (gw_k, gx_k) = one_microbatch(xs[0], xs[1])
            with _jax.named_scope("update"):
                return (loss_sum + l_k, _jax.tree.map(_jnp.add, grad_sum, gw_k)), gx_k

        init = (_jnp.zeros((), _jnp.float32), _jax.tree.map(_jnp.zeros_like, weights))
        (loss, grad_w), grad_x = _jax.lax.scan(body, init, (per_example, given["loss_target"]))
    with _jax.named_scope("update"):
        delta_w, new_m, new_v = {}, {}, {}
        for n in TWIN_WEIGHTS:
            delta_w[n], new_m[n], new_v[n] = _adamw(weights[n], grad_w[n], given["m_" + n], given["v_" + n])
    return (loss, grad_x, *[grad_w[n] for n in TWIN_WEIGHTS], *[delta_w[n] for n in TWIN_WEIGHTS],
            *[new_m[n] for n in TWIN_WEIGHTS], *[new_v[n] for n in TWIN_WEIGHTS])
```

```python
import functools
import math

import jax
import jax.numpy as jnp
import numpy as np
from jax import lax
from jax.experimental import pallas as pl
from jax.experimental.pallas import tpu as pltpu

F32 = jnp.float32
MXU = jnp.bfloat16
HI = lax.Precision.HIGHEST
MESH = pl.DeviceIdType.MESH

D_MODEL = 1024
DEPTH = 2
HEAD_DIM = 64
ATT_W = 512
ATT_PATTERNS = ((128, 1), (512, 4), (2048, 16))
BLK = 128
SSD_W = 512
SSD_STATE = 128
LRU_W = 512
LRU_BLOCKS = 8
LRU_C = 8.0
CONV_K = 4
D_MIX = 1536
D_FF = 2816
IN_COLS = 4104
NP = 4224
NORM_EPS = 1e-6
SSD_NORM_EPS = 1e-5
LN2 = math.log(2.0)
NEG = -1e30

ADAM_LR, ADAM_B1, ADAM_B2, ADAM_EPS, ADAM_WD, ADAM_STEP = 0.001, 0.9, 0.999, 1e-08, 0.01, 10
BC1 = 1.0 - ADAM_B1 ** ADAM_STEP
BC2 = 1.0 - ADAM_B2 ** ADAM_STEP

VMEM_LIMIT = 56 * 1024 * 1024

C_Q, C_K, C_V, C_Z, C_XBC, C_G, C_XL, C_DT = 0, 512, 1024, 1536, 2048, 3072, 3584, 4096


def _cp(*sem):
    return pltpu.CompilerParams(dimension_semantics=sem, vmem_limit_bytes=VMEM_LIMIT)


def _dot(a, b, dims, prec=None):
    return lax.dot_general(a, b, (dims, ((), ())), preferred_element_type=F32, precision=prec)


def _nn(a, b, prec=None):
    return _dot(a, b, ((1,), (0,)), prec)


def _nt(a, b, prec=None):
    return _dot(a, b, ((1,), (1,)), prec)


def _tn(a, b, prec=None):
    return _dot(a, b, ((0,), (0,)), prec)


def _sigmoid(x):
    return jax.nn.sigmoid(x)


def _silu(x):
    return x * _sigmoid(x)


def _softplus(x):
    return jnp.maximum(x, 0.0) + jnp.log(1.0 + jnp.exp(-jnp.abs(x)))


def _gelu(x):
    return 0.5 * x * (1.0 + jnp.tanh(0.7978845608028654 * (x + 0.044715 * x * x * x)))


def _mm(a, b, *, ta=False, tb=False, add=None, out_dtype=F32, tm, tn, tk, name):
    m, k = (a.shape[1], a.shape[0]) if ta else a.shape
    n = b.shape[0] if tb else b.shape[1]
    assert (b.shape[1] if tb else b.shape[0]) == k
    assert m % tm == 0 and n % tn == 0 and k % tk == 0, (name, m, n, k)
    nk = k // tk
    a_spec = pl.BlockSpec((tk, tm), lambda i, j, kk: (kk, i)) if ta else pl.BlockSpec((tm, tk), lambda i, j, kk: (i, kk))
    b_spec = pl.BlockSpec((tn, tk), lambda i, j, kk: (j, kk)) if tb else pl.BlockSpec((tk, tn), lambda i, j, kk: (kk, j))
    o_spec = pl.BlockSpec((tm, tn), lambda i, j, kk: (i, j))
    dims = ((0 if ta else 1,), (1 if tb else 0,))

    def body(*refs):
        if add is None:
            a_ref, b_ref, o_ref, acc = refs
        else:
            a_ref, b_ref, add_ref, o_ref, acc = refs
        kk = pl.program_id(2)

        @pl.when(kk == 0)
        def _():
            acc[...] = jnp.zeros_like(acc)

        acc[...] += _dot(a_ref[...].astype(MXU), b_ref[...].astype(MXU), dims)

        @pl.when(kk == nk - 1)
        def _():
            r = acc[...]
            if add is not None:
                r = r + add_ref[...]
            o_ref[...] = r.astype(out_dtype)

    ins = [a, b] + ([] if add is None else [add])
    specs = [a_spec, b_spec] + ([] if add is None else [o_spec])
    return pl.pallas_call(
        body, name=name, grid=(m // tm, n // tn, nk), in_specs=specs, out_specs=o_spec,
        out_shape=jax.ShapeDtypeStruct((m, n), out_dtype),
        scratch_shapes=[pltpu.VMEM((tm, tn), F32)],
        compiler_params=_cp("parallel", "parallel", "arbitrary"),
    )(*ins)


def _rows(fn, rows, consts=(), outs=(), accs=(), *, tile, name, halos=()):
    rows = [r if isinstance(r, tuple) else (r, r.shape[1], 0) for r in rows]
    s = rows[0][0].shape[0]
    assert s % tile == 0 and tile % 8 == 0
    n = s // tile
    t8 = tile // 8
    nr, nh, nc_, no, na = len(rows), len(halos), len(consts), len(outs), len(accs)

    def body(*refs):
        i = pl.program_id(0)
        rv = [r[...] for r in refs[:nr]]
        hv = []
        for (idx, kind), r in zip(halos, refs[nr:nr + nh]):
            edge = (i == 0) if kind == "prev" else (i == n - 1)
            hv.append(jnp.where(edge, 0.0, r[...]))
        cv = [r[...] for r in refs[nr + nh:nr + nh + nc_]]
        o_refs = refs[nr + nh + nc_:nr + nh + nc_ + no]
        a_refs = refs[nr + nh + nc_ + no:]
        ov, av = fn(rv, hv, cv)
        for r, v in zip(o_refs, ov):
            r[...] = v.astype(r.dtype)
        if na:
            @pl.when(i == 0)
            def _():
                for r in a_refs:
                    r[...] = jnp.zeros_like(r)
            for r, v in zip(a_refs, av):
                r[...] += v

    in_specs = [pl.BlockSpec((tile, w), functools.partial(lambda i, cb: (i, cb), cb=cb)) for (_, w, cb) in rows]
    for idx, kind in halos:
        _, w, cb = rows[idx]
        if kind == "prev":
            in_specs.append(pl.BlockSpec((8, w), functools.partial(lambda i, cb: (jnp.maximum(i * t8 - 1, 0), cb), cb=cb)))
        else:
            in_specs.append(pl.BlockSpec((8, w), functools.partial(lambda i, cb: (jnp.minimum((i + 1) * t8, n * t8 - 1), cb), cb=cb)))
    in_specs += [pl.BlockSpec(c.shape, functools.partial(lambda i, nd: (0,) * nd, nd=c.ndim)) for c in consts]
    out_specs = [pl.BlockSpec((tile, c), lambda i: (i, 0)) for (c, _) in outs]
    out_specs += [pl.BlockSpec((r, c), lambda i: (0, 0)) for (r, c) in accs]
    out_shape = [jax.ShapeDtypeStruct((s, c), dt) for (c, dt) in outs]
    out_shape += [jax.ShapeDtypeStruct((r, c), F32) for (r, c) in accs]
    args = [r[0] for r in rows] + [rows[idx][0] for idx, _ in halos] + list(consts)
    res = pl.pallas_call(
        body, name=name, grid=(n,), in_specs=in_specs, out_specs=out_specs, out_shape=out_shape,
        compiler_params=_cp("arbitrary"),
    )(*args)
    return list(res)


def _colsum8(v):
    t, c = v.shape
    return jnp.sum(v.reshape(t // 8, 8, c), axis=0)


def _rms(x, g):
    return x * lax.rsqrt(jnp.mean(x * x, axis=-1, keepdims=True) + NORM_EPS) * g


def _rms_fwd(x, g, name):
    def fn(rv, hv, cv):
        return [_rms(rv[0], cv[0])], []
    return _rows(fn, [x], [g.reshape(1, -1)], [(x.shape[1], MXU)], tile=512, name=name)[0]


def _rms_bwd(x, g, dh, dres, name):
    def fn(rv, hv, cv):
        xb, dhb, drb = rv
        _, vjp = jax.vjp(_rms, xb, cv[0])
        dx, _ = vjp(dhb)
        rstd = lax.rsqrt(jnp.mean(xb * xb, axis=-1, keepdims=True) + NORM_EPS)
        return [drb + dx], [_colsum8(dhb * xb * rstd)]
    d = x.shape[1]
    return _rows(fn, [x, dh, dres], [g.reshape(1, -1)], [(d, F32)], [(8, d)], tile=512, name=name)


def _slope_dist(hp, hh, dist, dil):
    hf = (2 * hp + hh + 1).astype(F32)
    slope = jnp.exp(jnp.zeros(dist.shape, F32) - hf * LN2)
    return slope * (dist.astype(F32) * float(dil))


def _att_fwd(proj, dil, name):
    s, npc = proj.shape
    l = s // dil
    nb = l // BLK
    cb = npc // BLK
    pv = proj.reshape(l, dil * npc)

    def body(q_ref, kp_ref, kc_ref, vp_ref, vc_ref, num_ref, m_ref, den_ref):
        hp, n = pl.program_id(1), pl.program_id(2)
        lane = lax.broadcasted_iota(jnp.int32, (BLK, BLK), 1)
        q = q_ref[...]
        kk = jnp.concatenate([kp_ref[...], kc_ref[...]], axis=0).astype(MXU)
        vv = jnp.concatenate([vp_ref[...], vc_ref[...]], axis=0).astype(MXU)
        qi = lax.broadcasted_iota(jnp.int32, (BLK, 2 * BLK), 0)
        ki = lax.broadcasted_iota(jnp.int32, (BLK, 2 * BLK), 1)
        dist = BLK + qi - ki
        valid = (dist >= 0) & (dist <= BLK) & ((n > 0) | (ki >= BLK))
        num = jnp.zeros((BLK, BLK), F32)
        mx = jnp.zeros((BLK, BLK), F32)
        den = jnp.zeros((BLK, BLK), F32)
        for hh in (0, 1):
            hmask = (lane < HEAD_DIM) if hh == 0 else (lane >= HEAD_DIM)
            qm = jnp.where(hmask, q, 0.0).astype(MXU)
            sc = _nt(qm, kk) * (HEAD_DIM ** -0.5) - _slope_dist(hp, hh, dist, dil)
            sc = jnp.where(valid, sc, NEG)
            m = jnp.max(sc, axis=1, keepdims=True)
            p = jnp.exp(sc - m)
            dn = jnp.sum(p, axis=1, keepdims=True)
            o = _nn(p.astype(MXU), vv)
            num = jnp.where(hmask, o, num)
            mx = jnp.where(hmask, m, mx)
            den = jnp.where(hmask, dn, den)
        num_ref[...] = num
        m_ref[...] = mx
        den_ref[...] = den

    def col(base):
        return base // BLK

    def cur(base):
        return pl.BlockSpec((BLK, BLK), lambda r, hp, n: (n, r * cb + col(base) + hp))

    def prev(base):
        return pl.BlockSpec((BLK, BLK), lambda r, hp, n: (jnp.maximum(n - 1, 0), r * cb + col(base) + hp))

    o_spec = pl.BlockSpec((BLK, BLK), lambda r, hp, n: (n, r * 4 + hp))
    o_shape = jax.ShapeDtypeStruct((l, dil * ATT_W), F32)
    num, m, den = pl.pallas_call(
        body, name=name, grid=(dil, 4, nb),
        in_specs=[cur(C_Q), prev(C_K), cur(C_K), prev(C_V), cur(C_V)],
        out_specs=[o_spec] * 3, out_shape=[o_shape] * 3,
        compiler_params=_cp("parallel", "parallel", "arbitrary"),
    )(pv, pv, pv, pv, pv)
    return [t.reshape(s, ATT_W) for t in (num, m, den)]


def _att_merge(parts, name):
    def fn(rv, hv, cv):
        ms = rv[1::3]
        m_all = functools.reduce(jnp.maximum, ms)
        num = 0.0
        den = 0.0
        for g in range(len(ms)):
            e = jnp.exp(rv[3 * g + 1] - m_all)
            num = num + rv[3 * g] * e
            den = den + rv[3 * g + 2] * e
        return [num / den, m_all + jnp.log(den)], []
    flat = [t for p in parts for t in p]
    return _rows(fn, flat, [], [(ATT_W, F32), (ATT_W, F32)], tile=512, name=name)


def _att_delta(datt, att, name):
    def fn(rv, hv, cv):
        r = lax.broadcasted_iota(jnp.int32, (ATT_W, ATT_W), 0) // HEAD_DIM
        c = lax.broadcasted_iota(jnp.int32, (ATT_W, ATT_W), 1) // HEAD_DIM
        ones = (r == c).astype(F32)
        return [_nn(rv[0] * rv[1], ones, HI)], []
    return _rows(fn, [datt, att], [], [(ATT_W, F32)], tile=512, name=name)[0]


def _att_bwd(proj, datt, lse, delta, acc, dil, name):
    s, npc = proj.shape
    l = s // dil
    nb = l // BLK
    cb = npc // BLK
    pv = proj.reshape(l, dil * npc)
    wide = [t.reshape(l, dil * ATT_W) for t in (datt, lse, delta)]
    accv = [] if acc is None else [t.reshape(l, dil * ATT_W) for t in acc]
    scale = HEAD_DIM ** -0.5

    def body(qn_ref, qx_ref, kp_ref, kc_ref, vp_ref, vc_ref, don_ref, dox_ref, lsn_ref, lsx_ref, dln_ref, dlx_ref, *rest):
        if acc is None:
            dq_ref, dk_ref, dv_ref = rest
        else:
            aq_ref, ak_ref, av_ref, dq_ref, dk_ref, dv_ref = rest
        hp, n = pl.program_id(1), pl.program_id(2)
        lane = lax.broadcasted_iota(jnp.int32, (BLK, BLK), 1)
        qi = lax.broadcasted_iota(jnp.int32, (BLK, BLK), 0)
        ki = lax.broadcasted_iota(jnp.int32, (BLK, BLK), 1)
        d_far = BLK + qi - ki
        d_near = qi - ki
        qn, qx = qn_ref[...], qx_ref[...]
        kp, kc = kp_ref[...].astype(MXU), kc_ref[...].astype(MXU)
        vp, vc = vp_ref[...].astype(MXU), vc_ref[...].astype(MXU)
        don, dox = don_ref[...], dox_ref[...]
        lsn, lsx, dln, dlx = lsn_ref[...], lsx_ref[...], dln_ref[...], dlx_ref[...]

        def tile(qm, dom, k, v, ls, dl, dist, valid, hh):
            c0 = hh * HEAD_DIM
            sc = _nt(qm, k) * scale - _slope_dist(hp, hh, dist, dil)
            p = jnp.exp(jnp.where(valid, sc - ls[:, c0:c0 + 1], NEG))
            ds = p * (_nt(dom, v) - dl[:, c0:c0 + 1])
            return p.astype(MXU), ds.astype(MXU)

        dq = jnp.zeros((BLK, BLK), F32)
        dk = jnp.zeros((BLK, BLK), F32)
        dv = jnp.zeros((BLK, BLK), F32)
        for hh in (0, 1):
            hmask = (lane < HEAD_DIM) if hh == 0 else (lane >= HEAD_DIM)
            qnm = jnp.where(hmask, qn, 0.0).astype(MXU)
            qxm = jnp.where(hmask, qx, 0.0).astype(MXU)
            donm = jnp.where(hmask, don, 0.0).astype(MXU)
            doxm = jnp.where(hmask, dox, 0.0).astype(MXU)
            _, ds_a = tile(qnm, donm, kp, vp, lsn, dln, d_far, (d_far <= BLK) & (n > 0), hh)
            p_b, ds_b = tile(qnm, donm, kc, vc, lsn, dln, d_near, d_near >= 0, hh)
            p_c, ds_c = tile(qxm, doxm, kc, vc, lsx, dlx, d_far, (d_far <= BLK) & (n < nb - 1), hh)
            dq = jnp.where(hmask, _nn(ds_a, kp) + _nn(ds_b, kc), dq)
            dk = dk + _tn(ds_b, qnm) + _tn(ds_c, qxm)
            dv = dv + _tn(p_b, donm) + _tn(p_c, doxm)
        dq, dk = dq * scale, dk * scale
        if acc is not None:
            dq, dk, dv = dq + aq_ref[...], dk + ak_ref[...], dv + av_ref[...]
        dq_ref[...] = dq
        dk_ref[...] = dk
        dv_ref[...] = dv

    def pspec(base, shift):
        def idx(r, hp, n):
            return (jnp.clip(n + shift, 0, nb - 1), r * cb + base // BLK + hp)
        return pl.BlockSpec((BLK, BLK), idx)

    def wspec(shift):
        def idx(r, hp, n):
            return (jnp.clip(n + shift, 0, nb - 1), r * 4 + hp)
        return pl.BlockSpec((BLK, BLK), idx)

    in_specs = [pspec(C_Q, 0), pspec(C_Q, 1), pspec(C_K, -1), pspec(C_K, 0), pspec(C_V, -1), pspec(C_V, 0),
                wspec(0), wspec(1), wspec(0), wspec(1), wspec(0), wspec(1)] + [wspec(0)] * len(accv)
    o_shape = jax.ShapeDtypeStruct((l, dil * ATT_W), F32)
    res = pl.pallas_call(
        body, name=name, grid=(dil, 4, nb), in_specs=in_specs,
        out_specs=[wspec(0)] * 3, out_shape=[o_shape] * 3,
        compiler_params=_cp("parallel", "parallel", "arbitrary"),
    )(pv, pv, pv, pv, pv, pv, wide[0], wide[0], wide[1], wide[1], wide[2], wide[2], *accv)
    return [t.reshape(s, ATT_W) for t in res]


def _shift_down(cur, halo, sft):
    if sft == 0:
        return cur
    t = cur.shape[0]
    rolled = pltpu.roll(cur, sft, 0)
    hr = pltpu.roll(halo, sft, 0)
    row = lax.broadcasted_iota(jnp.int32, cur.shape, 0)
    return jnp.where(row < sft, jnp.tile(hr, (t // 8, 1)), rolled)


def _shift_up(cur, halo, sft):
    if sft == 0:
        return cur
    t = cur.shape[0]
    rolled = pltpu.roll(cur, t - sft, 0)
    hr = pltpu.roll(halo, 8 - sft, 0)
    row = lax.broadcasted_iota(jnp.int32, cur.shape, 0)
    return jnp.where(row >= t - sft, jnp.tile(hr, (t // 8, 1)), rolled)


def _conv(x, xh, w, b):
    y = b + x * w[CONV_K - 1:CONV_K]
    for k in range(CONV_K - 1):
        y = y + _shift_down(x, xh, CONV_K - 1 - k) * w[k:k + 1]
    return y


def _conv_bwd(x, xh, dy, dyh, w):
    dx = dy * w[CONV_K - 1:CONV_K]
    dws = []
    for k in range(CONV_K - 1):
        sft = CONV_K - 1 - k
        dx = dx + _shift_up(dy, dyh, sft) * w[k:k + 1]
        dws.append(jnp.sum(dy * _shift_down(x, xh, sft), axis=0, keepdims=True))
    dws.append(jnp.sum(dy * x, axis=0, keepdims=True))
    c = x.shape[1]
    dw = jnp.concatenate(dws + [jnp.zeros((8 - CONV_K, c), F32)], axis=0)
    return dx, dw, jnp.sum(dy, axis=0, keepdims=True)


def _pad8(w):
    return jnp.concatenate([w, jnp.zeros((8 - w.shape[0], w.shape[1]), w.dtype)], axis=0)


def _ssd_pre(proj, conv_w, conv_b, dt_bias128, name):
    def fn(rv, hv, cv):
        xbc, dtr = rv
        return [_silu(_conv(xbc, hv[0], cv[0], cv[1])), _softplus(dtr + cv[2])], []
    return _rows(fn, [(proj, 1024, C_XBC // 1024), (proj, BLK, C_DT // BLK)],
                 [_pad8(conv_w), conv_b.reshape(1, -1), dt_bias128],
                 [(1024, F32), (BLK, F32)], tile=256, name=name, halos=[(0, "prev")])


def _ssd_pre_bwd(proj, dxc, ddt, conv_w, conv_b, dt_bias128, name):
    def fn(rv, hv, cv):
        xbc, dtr, dxcb, ddtb = rv
        xh, dxch_raw, xnext = hv
        w, b, bias = cv
        pre = _conv(xbc, xh, w, b)
        sg = _sigmoid(pre)
        dpre = dxcb * (sg * (1.0 + pre * (1.0 - sg)))
        t = xbc.shape[0]
        tail = jnp.concatenate([xbc[t - 8:], xnext], axis=0)
        pre_n = _conv(tail[8:], tail[:8], w, b)
        sgn = _sigmoid(pre_n)
        dpre_h = dxch_raw * (sgn * (1.0 + pre_n * (1.0 - sgn)))
        dx, dw, db = _conv_bwd(xbc, xh, dpre, dpre_h, w)
        ddr = ddtb * _sigmoid(dtr + bias)
        return [dx, ddr], [dw, jnp.concatenate([db, jnp.zeros((7, db.shape[1]), F32)], axis=0), _colsum8(ddr)]
    return _rows(fn, [(proj, 1024, C_XBC // 1024), (proj, BLK, C_DT // BLK), dxc, ddt],
                 [_pad8(conv_w), conv_b.reshape(1, -1), dt_bias128],
                 [(1024, F32), (BLK, F32)], [(8, 1024), (8, 1024), (8, BLK)], tile=256, name=name,
                 halos=[(0, "prev"), (2, "next"), (0, "next")])


def _head_cols(v, h0):
    lane = lax.broadcasted_iota(jnp.int32, (v.shape[0], BLK), 1)
    return jnp.where(lane < HEAD_DIM, v[:, h0:h0 + 1], v[:, h0 + 1:h0 + 2])


def _ssd_scan(xc, dt, par, name):
    s = xc.shape[0]
    nc = s // BLK

    def body(x_ref, dt_ref, par_ref, y_ref, st_ref, h_ref):
        c = pl.program_id(0)

        @pl.when(c == 0)
        def _():
            h_ref[...] = jnp.zeros_like(h_ref)

        st_ref[0] = h_ref[...]
        dt = dt_ref[...]
        a_row = -jnp.exp(par_ref[0:1, :])
        d_row = par_ref[1:2, :]
        ri = lax.broadcasted_iota(jnp.int32, (BLK, BLK), 0)
        ci = lax.broadcasted_iota(jnp.int32, (BLK, BLK), 1)
        tril = ri >= ci
        cs = _nn(tril.astype(F32), dt * a_row, HI)
        cst, dtt = cs.T, dt.T
        last = cs[BLK - 1:BLK, :]
        wcol = jnp.exp(last - cs) * dt
        ecs = jnp.exp(cs)
        elast = jnp.exp(last)
        for g in (0, 1):
            bg = x_ref[:, 512 + g * BLK:512 + (g + 1) * BLK].astype(MXU)
            cg = x_ref[:, 768 + g * BLK:768 + (g + 1) * BLK].astype(MXU)
            gm = _nt(cg, bg)
            for pp in (0, 1):
                pr = 2 * g + pp
                h0 = 2 * pr
                x2 = x_ref[:, pr * BLK:(pr + 1) * BLK]
                hprev = h_ref[pr * BLK:(pr + 1) * BLK, :]
                yp = jnp.zeros((BLK, BLK), F32)
                for hh in (0, 1):
                    h = h0 + hh
                    hmask = (ci < HEAD_DIM) if hh == 0 else (ci >= HEAD_DIM)
                    lm = jnp.exp(jnp.where(tril, cs[:, h:h + 1] - cst[h:h + 1, :], NEG))
                    mm = gm * lm * dtt[h:h + 1, :]
                    yp = yp + _nn(mm.astype(MXU), jnp.where(hmask, x2, 0.0).astype(MXU))
                y0 = _nt(cg, hprev.astype(MXU))
                y_ref[:, pr * BLK:(pr + 1) * BLK] = yp + _head_cols(ecs, h0) * y0 + _head_cols(d_row, h0) * x2
                dec = jnp.where(ri < HEAD_DIM, elast[:, h0:h0 + 1], elast[:, h0 + 1:h0 + 2])
                xw = (x2 * _head_cols(wcol, h0)).astype(MXU)
                h_ref[pr * BLK:(pr + 1) * BLK, :] = dec * hprev + _tn(xw, bg)

    return pl.pallas_call(
        body, name=name, grid=(nc,),
        in_specs=[pl.BlockSpec((BLK, 1024), lambda c: (c, 0)), pl.BlockSpec((BLK, BLK), lambda c: (c, 0)),
                  pl.BlockSpec((8, BLK), lambda c: (0, 0))],
        out_specs=[pl.BlockSpec((BLK, SSD_W), lambda c: (c, 0)), pl.BlockSpec((1, SSD_W, SSD_STATE), lambda c: (c, 0, 0))],
        out_shape=[jax.ShapeDtypeStruct((s, SSD_W), F32), jax.ShapeDtypeStruct((nc, SSD_W, SSD_STATE), F32)],
        scratch_shapes=[pltpu.VMEM((SSD_W, SSD_STATE), F32)],
        compiler_params=_cp("arbitrary"),
    )(xc, dt, par)


def _ssd_scan_bwd(xc, dt, par, st, dy, name):
    s = xc.shape[0]
    nc = s // BLK

    def body(x_ref, dt_ref, par_ref, st_ref, dy_ref, dx_ref, ddt_ref, dal_ref, dd_ref, dh_ref):
        c = pl.program_id(0)

        @pl.when(c == 0)
        def _():
            dh_ref[...] = jnp.zeros_like(dh_ref)
            dal_ref[...] = jnp.zeros_like(dal_ref)
            dd_ref[...] = jnp.zeros_like(dd_ref)

        dt = dt_ref[...]
        a_row = -jnp.exp(par_ref[0:1, :])
        d_row = par_ref[1:2, :]
        ri = lax.broadcasted_iota(jnp.int32, (BLK, BLK), 0)
        ci = lax.broadcasted_iota(jnp.int32, (BLK, BLK), 1)
        tril = ri >= ci
        cs = _nn(tril.astype(F32), dt * a_row, HI)
        cst, dtt = cs.T, dt.T
        last = cs[BLK - 1:BLK, :]
        tolast = jnp.exp(last - cs)
        wcol = tolast * dt
        ecs = jnp.exp(cs)
        elast = jnp.exp(last)
        dcs_col = jnp.zeros((BLK, BLK), F32)
        ddt_col = jnp.zeros((BLK, BLK), F32)
        dcs_row = jnp.zeros((BLK, BLK), F32)
        ddt_row = jnp.zeros((BLK, BLK), F32)
        dlast = jnp.zeros((1, BLK), F32)
        ddsk = jnp.zeros((1, BLK), F32)
        for g in (0, 1):
            bg32 = x_ref[:, 512 + g * BLK:512 + (g + 1) * BLK]
            cg32 = x_ref[:, 768 + g * BLK:768 + (g + 1) * BLK]
            bg, cg = bg32.astype(MXU), cg32.astype(MXU)
            gm = _nt(cg, bg)
            dgm = jnp.zeros((BLK, BLK), F32)
            dbg = jnp.zeros((BLK, BLK), F32)
            dcg = jnp.zeros((BLK, BLK), F32)
            for pp in (0, 1):
                pr = 2 * g + pp
                h0 = 2 * pr
                x2 = x_ref[:, pr * BLK:(pr + 1) * BLK]
                dy2 = dy_ref[:, pr * BLK:(pr + 1) * BLK]
                hprev = st_ref[0, pr * BLK:(pr + 1) * BLK, :]
                dhn = dh_ref[pr * BLK:(pr + 1) * BLK, :]
                x2m, dhnm = x2.astype(MXU), dhn.astype(MXU)
                zb = _nt(bg, dhnm)
                y0 = _nt(cg, hprev.astype(MXU))
                esel = _head_cols(ecs, h0)
                wsel = _head_cols(wcol, h0)
                dx2 = _head_cols(d_row, h0) * dy2 + wsel * zb
                r_off = dy2 * y0
                r_w = x2 * zb
                r_d = dy2 * x2
                r_h = dhn * hprev
                for hh in (0, 1):
                    h = h0 + hh
                    hmask = (ci < HEAD_DIM) if hh == 0 else (ci >= HEAD_DIM)
                    onl = (ci == h).astype(F32)
                    ons = (ri == h).astype(F32)
                    dym = jnp.where(hmask, dy2, 0.0).astype(MXU)
                    dt_r = dtt[h:h + 1, :]
                    lm = jnp.exp(jnp.where(tril, cs[:, h:h + 1] - cst[h:h + 1, :], NEG))
                    mm = gm * lm * dt_r
                    dx2 = dx2 + _tn(mm.astype(MXU), dym)
                    dm = _nt(dym, x2m)
                    t1 = dm * lm
                    dgm = dgm + t1 * dt_r
                    tt = t1 * gm
                    ddt_row = ddt_row + ons * jnp.sum(tt, axis=0, keepdims=True)
                    t = tt * dt_r
                    dcs_col = dcs_col + onl * jnp.sum(t, axis=1, keepdims=True)
                    dcs_row = dcs_row - ons * jnp.sum(t, axis=0, keepdims=True)
                    de = jnp.sum(jnp.where(hmask, r_off, 0.0), axis=1, keepdims=True)
                    dcs_col = dcs_col + onl * (ecs[:, h:h + 1] * de)
                    hrow = (ri < HEAD_DIM) if hh == 0 else (ri >= HEAD_DIM)
                    dl_h = elast[:, h:h + 1] * jnp.sum(jnp.where(hrow, r_h, 0.0), keepdims=True)
                    dw = jnp.sum(jnp.where(hmask, r_w, 0.0), axis=1, keepdims=True)
                    ddt_col = ddt_col + onl * (dw * tolast[:, h:h + 1])
                    v = dw * wcol[:, h:h + 1]
                    dcs_col = dcs_col - onl * v
                    dl_h = dl_h + jnp.sum(v, keepdims=True)
                    dlast = dlast + onl[0:1, :] * dl_h
                    ddsk = ddsk + onl[0:1, :] * jnp.sum(jnp.where(hmask, r_d, 0.0), keepdims=True)
                dx_ref[:, pr * BLK:(pr + 1) * BLK] = dx2
                edy = (esel * dy2).astype(MXU)
                dcg = dcg + _nn(edy, hprev.astype(MXU))
                dec = jnp.where(ri < HEAD_DIM, elast[:, h0:h0 + 1], elast[:, h0 + 1:h0 + 2])
                dh_ref[pr * BLK:(pr + 1) * BLK, :] = dec * dhn + _tn(edy, cg)
                dbg = dbg + _nn((x2 * wsel).astype(MXU), dhnm)
            dgmm = dgm.astype(MXU)
            dx_ref[:, 512 + g * BLK:512 + (g + 1) * BLK] = dbg + _tn(dgmm, cg)
            dx_ref[:, 768 + g * BLK:768 + (g + 1) * BLK] = dcg + _nn(dgmm, bg)
        dcs = dcs_col + dcs_row.T + jnp.where(ri == BLK - 1, dlast, 0.0)
        dda = _nn((ri <= ci).astype(F32), dcs, HI)
        ddt_ref[...] = ddt_col + ddt_row.T + a_row * dda
        da = jnp.sum(dt * dda, axis=0, keepdims=True)
        dal_ref[0:1, :] += da * a_row
        dd_ref[0:1, :] += ddsk

    rev = lambda c: (nc - 1 - c, 0)
    res = pl.pallas_call(
        body, name=name, grid=(nc,),
        in_specs=[pl.BlockSpec((BLK, 1024), rev), pl.BlockSpec((BLK, BLK), rev), pl.BlockSpec((8, BLK), lambda c: (0, 0)),
                  pl.BlockSpec((1, SSD_W, SSD_STATE), lambda c: (nc - 1 - c, 0, 0)), pl.BlockSpec((BLK, SSD_W), rev)],
        out_specs=[pl.BlockSpec((BLK, 1024), rev), pl.BlockSpec((BLK, BLK), rev),
                   pl.BlockSpec((8, BLK), lambda c: (0, 0)), pl.BlockSpec((8, BLK), lambda c: (0, 0))],
        out_shape=[jax.ShapeDtypeStruct((s, 1024), F32), jax.ShapeDtypeStruct((s, BLK), F32),
                   jax.ShapeDtypeStruct((8, BLK), F32), jax.ShapeDtypeStruct((8, BLK), F32)],
        scratch_shapes=[pltpu.VMEM((SSD_W, SSD_STATE), F32)],
        compiler_params=_cp("arbitrary"),
    )(xc, dt, par, st, dy)
    return res


def _ssd_gate(y, z, w):
    t = y * _silu(z)
    outs = []
    for g in (0, 1):
        tg = t[:, g * 256:(g + 1) * 256]
        outs.append(tg * lax.rsqrt(jnp.mean(tg * tg, axis=-1, keepdims=True) + SSD_NORM_EPS))
    return jnp.concatenate(outs, axis=1) * w


def _ssd_post(y, proj, norm_w, name):
    def fn(rv, hv, cv):
        return [_ssd_gate(rv[0], rv[1], cv[0])], []
    return _rows(fn, [y, (proj, SSD_W, C_Z // SSD_W)], [norm_w.reshape(1, -1)], [(SSD_W, F32)], tile=512, name=name)[0]


def _ssd_post_bwd(y, proj, norm_w, dout, name):
    def fn(rv, hv, cv):
        yb, zb, db = rv
        _, vjp = jax.vjp(lambda a, b: _ssd_gate(a, b, cv[0]), yb, zb)
        dy, dz = vjp(db)
        t = yb * _silu(zb)
        nrm = []
        for g in (0, 1):
            tg = t[:, g * 256:(g + 1) * 256]
            nrm.append(tg * lax.rsqrt(jnp.mean(tg * tg, axis=-1, keepdims=True) + SSD_NORM_EPS))
        return [dy, dz], [_colsum8(db * jnp.concatenate(nrm, axis=1))]
    return _rows(fn, [y, (proj, SSD_W, C_Z // SSD_W), dout], [norm_w.reshape(1, -1)],
                 [(SSD_W, F32), (SSD_W, F32)], [(8, SSD_W)], tile=512, name=name)


LRU_T = 256


def _lru_conv(proj, conv_w, conv_b, name):
    def fn(rv, hv, cv):
        return [_conv(rv[0], hv[0], cv[0], cv[1])], []
    return _rows(fn, [(proj, LRU_W, C_XL // LRU_W)], [_pad8(conv_w), conv_b.reshape(1, -1)], [(LRU_W, F32)],
                 tile=512, name=name, halos=[(0, "prev")])[0]


def _lru_conv_bwd(proj, dxc, conv_w, name):
    def fn(rv, hv, cv):
        dx, dw, db = _conv_bwd(rv[0], hv[0], rv[1], hv[1], cv[0])
        return [dx], [dw, jnp.concatenate([db, jnp.zeros((7, db.shape[1]), F32)], axis=0)]
    return _rows(fn, [(proj, LRU_W, C_XL // LRU_W), dxc], [_pad8(conv_w)], [(LRU_W, F32)], [(8, LRU_W), (8, LRU_W)],
                 tile=512, name=name, halos=[(0, "prev"), (1, "next")])


def _lru_au(pre_a, pre_x, xc, ba, bx, lam):
    r = _sigmoid(pre_a + ba)
    i = _sigmoid(pre_x + bx)
    log_a = -LRU_C * r * _softplus(-lam)
    a = jnp.exp(log_a)
    u = jnp.sqrt(1.0 - jnp.exp(2.0 * log_a)) * (i * xc)
    return a, u


def _lru_scan(pre, xc, proj, par, name):
    s = xc.shape[0]
    t = LRU_T

    def body(pre_ref, xc_ref, g_ref, par_ref, out_ref, h_ref, carry):
        c = pl.program_id(0)

        @pl.when(c == 0)
        def _():
            carry[...] = jnp.zeros_like(carry)

        a, u = _lru_au(pre_ref[:, :LRU_W], pre_ref[:, LRU_W:], xc_ref[...], par_ref[0:1, :], par_ref[1:2, :], par_ref[2:3, :])
        row = lax.broadcasted_iota(jnp.int32, (t, LRU_W), 0)
        sft = 1
        while sft < t:
            keep = row >= sft
            a_s = jnp.where(keep, pltpu.roll(a, sft, 0), 1.0)
            u_s = jnp.where(keep, pltpu.roll(u, sft, 0), 0.0)
            u = a * u_s + u
            a = a * a_s
            sft *= 2
        h = a * carry[0:1, :] + u
        h_ref[...] = h
        out_ref[...] = h * _gelu(g_ref[...])
        carry[0:1, :] = h[t - 1:t, :]

    return pl.pallas_call(
        body, name=name, grid=(s // t,),
        in_specs=[pl.BlockSpec((t, 2 * LRU_W), lambda c: (c, 0)), pl.BlockSpec((t, LRU_W), lambda c: (c, 0)),
                  pl.BlockSpec((t, LRU_W), lambda c: (c, C_G // LRU_W)), pl.BlockSpec((8, LRU_W), lambda c: (0, 0))],
        out_specs=[pl.BlockSpec((t, LRU_W), lambda c: (c, 0))] * 2,
        out_shape=[jax.ShapeDtypeStruct((s, LRU_W), F32)] * 2,
        scratch_shapes=[pltpu.VMEM((8, LRU_W), F32)],
        compiler_params=_cp("arbitrary"),
    )(pre, xc, proj, par)


def _lru_scan_bwd(pre, xc, proj, par, h, dout, name):
    s = xc.shape[0]
    t = LRU_T
    n = s // t
    t8 = t // 8

    def body(pre_ref, xc_ref, g_ref, par_ref, h_ref, hh_ref, do_ref, dpre_ref, dxc_ref, dg_ref, dpar_ref, carry):
        c = pl.program_id(0)

        @pl.when(c == 0)
        def _():
            carry[...] = jnp.zeros_like(carry)
            dpar_ref[...] = jnp.zeros_like(dpar_ref)

        pa, px, xcb = pre_ref[:, :LRU_W], pre_ref[:, LRU_W:], xc_ref[...]
        ba, bx, lam = par_ref[0:1, :], par_ref[1:2, :], par_ref[2:3, :]
        (a, u), vjp = jax.vjp(_lru_au, pa, px, xcb, ba, bx, lam)
        g = g_ref[...]
        hcur = h_ref[...]
        do = do_ref[...]
        _, gvjp = jax.vjp(_gelu, g)
        dg_ref[...] = gvjp(do * hcur)[0]
        row = lax.broadcasted_iota(jnp.int32, (t, LRU_W), 0)
        v = do * _gelu(g) + jnp.where(row == t - 1, carry[0:1, :], 0.0)
        b = jnp.where(row == t - 1, 0.0, pltpu.roll(a, t - 1, 0))
        sft = 1
        while sft < t:
            keep = row < t - sft
            b_s = jnp.where(keep, pltpu.roll(b, t - sft, 0), 1.0)
            v_s = jnp.where(keep, pltpu.roll(v, t - sft, 0), 0.0)
            v = b * v_s + v
            b = b * b_s
            sft *= 2
        dh = v
        carry[0:1, :] = a[0:1, :] * dh[0:1, :]
        hhalo = jnp.where(c == n - 1, 0.0, hh_ref[...])
        hprev = _shift_down(hcur, hhalo, 1)
        dpa, dpx, dxc, dba, dbx, dlam = vjp((dh * hprev, dh))
        dpre_ref[:, :LRU_W] = dpa
        dpre_ref[:, LRU_W:] = dpx
        dxc_ref[...] = dxc
        dpar_ref[0:1, :] += dba
        dpar_ref[1:2, :] += dbx
        dpar_ref[2:3, :] += dlam

    rev = lambda c: (n - 1 - c, 0)
    return pl.pallas_call(
        body, name=name, grid=(n,),
        in_specs=[pl.BlockSpec((t, 2 * LRU_W), rev), pl.BlockSpec((t, LRU_W), rev),
                  pl.BlockSpec((t, LRU_W), lambda c: (n - 1 - c, C_G // LRU_W)), pl.BlockSpec((8, LRU_W), lambda c: (0, 0)),
                  pl.BlockSpec((t, LRU_W), rev),
                  pl.BlockSpec((8, LRU_W), lambda c: (jnp.maximum((n - 1 - c) * t8 - 1, 0), 0)),
                  pl.BlockSpec((t, LRU_W), rev)],
        out_specs=[pl.BlockSpec((t, 2 * LRU_W), rev), pl.BlockSpec((t, LRU_W), rev), pl.BlockSpec((t, LRU_W), rev),
                   pl.BlockSpec((8, LRU_W), lambda c: (0, 0))],
        out_shape=[jax.ShapeDtypeStruct((s, 2 * LRU_W), F32), jax.ShapeDtypeStruct((s, LRU_W), F32),
                   jax.ShapeDtypeStruct((s, LRU_W), F32), jax.ShapeDtypeStruct((8, LRU_W), F32)],
        scratch_shapes=[pltpu.VMEM((8, LRU_W), F32)],
        compiler_params=_cp("arbitrary"),
    )(pre, xc, proj, par, h, h, dout)


def _swiglu_act(gu, name):
    def fn(rv, hv, cv):
        return [_silu(rv[0]) * rv[1]], []
    return _rows(fn, [(gu, D_FF, 0), (gu, D_FF, 1)], [], [(D_FF, MXU)], tile=256, name=name)[0]


def _swiglu_bwd(gu, da, name):
    def fn(rv, hv, cv):
        gt, up, dab = rv
        sg = _sigmoid(gt)
        dgate = dab * up * (sg * (1.0 + gt * (1.0 - sg)))
        dup = dab * (gt * sg)
        return [jnp.concatenate([dgate, dup], axis=1)], []
    return _rows(fn, [(gu, D_FF, 0), (gu, D_FF, 1), da], [], [(2 * D_FF, MXU)], tile=256, name=name)[0]


def _loss_head(x, g, target, name):
    d = x.shape[1]

    def fn(rv, hv, cv):
        xb, tb = rv
        y, vjp = jax.vjp(_rms, xb, cv[0])
        err = y - tb
        dy = err * (1.0 / d)
        dx, _ = vjp(dy)
        rstd = lax.rsqrt(jnp.mean(xb * xb, axis=-1, keepdims=True) + NORM_EPS)
        e2 = err * err * (0.5 / d)
        e2 = functools.reduce(lambda a, b: a + b, [e2[:, k * BLK:(k + 1) * BLK] for k in range(d // BLK)])
        return [dx], [_colsum8(dy * xb * rstd), _colsum8(e2)]
    return _rows(fn, [x, target], [g.reshape(1, -1)], [(d, F32)], [(8, d), (8, BLK)], tile=512, name=name)


ANY = pl.BlockSpec(memory_space=pl.ANY)


def _coords():
    return lax.axis_index("x"), lax.axis_index("y"), lax.axis_index("c")


def _gather_weights(big, small):
    def body(big_ref, small_ref, obig_ref, osmall_ref, send_sems, recv_sems, local_sems):
        x, y, c = _coords()
        me = 2 * x + y
        chips = [(1 - x, y), (x, 1 - y), (1 - x, 1 - y)]
        mine_b = pltpu.make_async_copy(big_ref, obig_ref.at[me], local_sems.at[0])
        mine_s = pltpu.make_async_copy(small_ref, osmall_ref.at[me], local_sems.at[1])
        mine_b.start()
        mine_s.start()
        sends = []
        for j, (px, py) in enumerate(chips):
            sends.append(pltpu.make_async_remote_copy(big_ref, obig_ref.at[me], send_sems.at[2 * j], recv_sems.at[2 * j],
                                                      device_id=(px, py, c), device_id_type=MESH))
            sends.append(pltpu.make_async_remote_copy(small_ref, osmall_ref.at[me], send_sems.at[2 * j + 1], recv_sems.at[2 * j + 1],
                                                      device_id=(px, py, c), device_id_type=MESH))
        for cp in sends:
            cp.start()
        for j, (px, py) in enumerate(chips):
            src = 2 * px + py
            pltpu.make_async_remote_copy(big_ref, obig_ref.at[src], send_sems.at[2 * j], recv_sems.at[2 * j],
                                         device_id=(px, py, c), device_id_type=MESH).wait_recv()
            pltpu.make_async_remote_copy(small_ref, osmall_ref.at[src], send_sems.at[2 * j + 1], recv_sems.at[2 * j + 1],
                                         device_id=(px, py, c), device_id_type=MESH).wait_recv()
        for cp in sends:
            cp.wait_send()
        mine_b.wait()
        mine_s.wait()

    return pl.pallas_call(
        body, name="gather_weights", in_specs=[ANY, ANY], out_specs=[ANY, ANY],
        out_shape=[jax.ShapeDtypeStruct((4,) + big.shape, big.dtype), jax.ShapeDtypeStruct((4,) + small.shape, small.dtype)],
        scratch_shapes=[pltpu.SemaphoreType.DMA((6,)), pltpu.SemaphoreType.DMA((6,)), pltpu.SemaphoreType.DMA((2,))],
        compiler_params=pltpu.CompilerParams(has_side_effects=True),
    )(big, small)


def _exchange_grads(gbig):
    def body(g_ref, o_ref, send_sems, recv_sems):
        x, y, c = _coords()
        chips = [(1 - x, y), (x, 1 - y), (1 - x, 1 - y)]
        sends = [pltpu.make_async_remote_copy(g_ref.at[2 * px + py], o_ref.at[j], send_sems.at[j], recv_sems.at[j],
                                              device_id=(px, py, c), device_id_type=MESH)
                 for j, (px, py) in enumerate(chips)]
        for cp in sends:
            cp.start()
        for cp in sends:
            cp.wait_recv()
        for cp in sends:
            cp.wait_send()

    return pl.pallas_call(
        body, name="exchange_grads", in_specs=[ANY], out_specs=ANY,
        out_shape=jax.ShapeDtypeStruct((3,) + gbig.shape[1:], gbig.dtype),
        scratch_shapes=[pltpu.SemaphoreType.DMA((3,)), pltpu.SemaphoreType.DMA((3,))],
        compiler_params=pltpu.CompilerParams(has_side_effects=True),
    )(gbig)


def _swap_sibling(p):
    def body(p_ref, o_ref, send_sem, recv_sem):
        x, y, c = _coords()
        cp = pltpu.make_async_remote_copy(p_ref, o_ref, send_sem, recv_sem, device_id=(x, y, 1 - c), device_id_type=MESH)
        cp.start()
        cp.wait_recv()
        cp.wait_send()

    return pl.pallas_call(
        body, name="swap_sibling", in_specs=[ANY], out_specs=ANY,
        out_shape=jax.ShapeDtypeStruct(p.shape, p.dtype),
        scratch_shapes=[pltpu.SemaphoreType.DMA, pltpu.SemaphoreType.DMA],
        compiler_params=pltpu.CompilerParams(has_side_effects=True),
    )(p)


def _gather_small(gs):
    def body(g_ref, o_ref, send_sems, recv_sems, local_sem):
        x, y, c = _coords()
        me = 4 * x + 2 * y + c
        mine = pltpu.make_async_copy(g_ref, o_ref.at[me], local_sem)
        mine.start()
        sends = []
        for k in range(1, 8):
            px, py, pc = x ^ (k >> 2), y ^ ((k >> 1) & 1), c ^ (k & 1)
            sends.append((pltpu.make_async_remote_copy(g_ref, o_ref.at[me], send_sems.at[k - 1], recv_sems.at[k - 1],
                                                       device_id=(px, py, pc), device_id_type=MESH), 4 * px + 2 * py + pc, k))
        for cp, _, _ in sends:
            cp.start()
        for cp, src, k in sends:
            pltpu.make_async_remote_copy(g_ref, o_ref.at[src], send_sems.at[k - 1], recv_sems.at[k - 1],
                                         device_id=(x, y, c), device_id_type=MESH).wait_recv()
        for cp, _, _ in sends:
            cp.wait_send()
        mine.wait()

    return pl.pallas_call(
        body, name="gather_small", in_specs=[ANY], out_specs=ANY,
        out_shape=jax.ShapeDtypeStruct((8,) + gs.shape, gs.dtype),
        scratch_shapes=[pltpu.SemaphoreType.DMA((7,)), pltpu.SemaphoreType.DMA((7,)), pltpu.SemaphoreType.DMA],
        compiler_params=pltpu.CompilerParams(has_side_effects=True),
    )(gs)


def _sum_slots(own, others, name, tile):
    k, r, c = others.shape

    def body(*refs):
        if own is None:
            o_ref, out_ref = refs
            acc = o_ref[0]
            first = 1
        else:
            own_ref, o_ref, out_ref = refs
            acc = own_ref[...]
            first = 0
        for j in range(first, k):
            acc = acc + o_ref[j]
        out_ref[...] = acc

    row = pl.BlockSpec((tile, c), lambda i: (i, 0))
    specs = ([] if own is None else [row]) + [pl.BlockSpec((k, tile, c), lambda i: (0, i, 0))]
    args = ([] if own is None else [own]) + [others]
    return pl.pallas_call(body, name=name, grid=(r // tile,), in_specs=specs, out_specs=row,
                          out_shape=jax.ShapeDtypeStruct((r, c), F32), compiler_params=_cp("parallel"))(*args)


def _adamw(w, m, v, ga, gb, name, tile):
    r, c = w.shape

    def body(*refs):
        if gb is None:
            w_ref, m_ref, v_ref, ga_ref, g_ref, d_ref, nm_ref, nv_ref = refs
            g = ga_ref[...]
        else:
            w_ref, m_ref, v_ref, ga_ref, gb_ref, g_ref, d_ref, nm_ref, nv_ref = refs
            g = ga_ref[...] + gb_ref[...]
        nm = ADAM_B1 * m_ref[...] + (1.0 - ADAM_B1) * g
        nv = ADAM_B2 * v_ref[...] + (1.0 - ADAM_B2) * (g * g)
        g_ref[...] = g
        nm_ref[...] = nm
        nv_ref[...] = nv
        d_ref[...] = -ADAM_LR * ((nm / BC1) / (jnp.sqrt(nv / BC2) + ADAM_EPS) + ADAM_WD * w_ref[...])

    row = pl.BlockSpec((tile, c), lambda i: (i, 0))
    args = [w, m, v, ga] + ([] if gb is None else [gb])
    return pl.pallas_call(body, name=name, grid=(r // tile,), in_specs=[row] * len(args), out_specs=[row] * 4,
                          out_shape=[jax.ShapeDtypeStruct((r, c), F32)] * 4, compiler_params=_cp("parallel"))(*args)


BIG = ("w_in", "ssd_conv_w", "lru_conv_w", "w_out", "w_gate", "w_up", "w_down")
BIG_AXIS = {"w_in": 2, "ssd_conv_w": 2, "lru_conv_w": 2, "w_out": 1, "w_gate": 2, "w_up": 2, "w_down": 1}
CONVS = ("ssd_conv_w", "lru_conv_w")
SMALL = ("norm_mix", "ssd_conv_b", "ssd_dt_bias", "ssd_a_log", "ssd_d", "ssd_norm", "lru_conv_b", "lru_wa", "lru_ba",
         "lru_wx", "lru_bx", "lru_lambda", "norm_ffn", "norm_final")
WEIGHTS = ("norm_mix", "w_in", "ssd_conv_w", "ssd_conv_b", "ssd_dt_bias", "ssd_a_log", "ssd_d", "ssd_norm", "lru_conv_w",
           "lru_conv_b", "lru_wa", "lru_ba", "lru_wx", "lru_bx", "lru_lambda", "w_out", "norm_ffn", "w_gate", "w_up",
           "w_down", "norm_final")


def _pack(arrs, width, row_mult, dtype):
    flat = jnp.concatenate([a.reshape(-1).astype(dtype) for a in arrs])
    rows = -(-flat.shape[0] // width)
    rows = -(-rows // row_mult) * row_mult
    flat = jnp.pad(flat, (0, rows * width - flat.shape[0]))
    return flat.reshape(rows, width)


def _unpack(buf, shapes):
    flat = buf.reshape(-1)
    out, off = [], 0
    for shp in shapes:
        n = int(np.prod(shp))
        out.append(flat[off:off + n].reshape(shp))
        off += n
    return out


def _perm_cols(w):
    pad = jnp.zeros(w.shape[:-1] + (NP - IN_COLS,), w.dtype)
    return jnp.concatenate([w[..., :3072], w[..., 3080:4104], w[..., 3072:3080], pad], axis=-1)


def _unperm_cols(g):
    return jnp.concatenate([g[..., :3072], g[..., C_DT:C_DT + 8], g[..., 3072:4096]], axis=-1)


def _block_diag(w):
    eye = jnp.eye(LRU_BLOCKS, dtype=w.dtype)
    return jnp.einsum("ncd,nm->ncmd", w, eye).reshape(LRU_W, LRU_W)


def _block_diag_extract(g):
    g4 = g.reshape(LRU_BLOCKS, 64, LRU_BLOCKS, 64)
    return jnp.stack([g4[n, :, n, :] for n in range(LRU_BLOCKS)], axis=0)


def _lanes128(v):
    return jnp.pad(v, (0, BLK - v.shape[0])).reshape(1, BLK)


def _layer_fwd(x, p):
    h = _rms_fwd(x, p["norm_mix"], "rms_mix")
    proj = _mm(h, p["w_in"], tm=1024, tn=1408, tk=1024, name="mm_in")
    parts = [_att_fwd(proj, d, "att_fwd_d%d" % d) for (_, d) in ATT_PATTERNS]
    att, lse = _att_merge(parts, "att_merge")
    xconv, dt = _ssd_pre(proj, p["ssd_conv_w"], p["ssd_conv_b"], _lanes128(p["ssd_dt_bias"]), "ssd_pre")
    spar = jnp.concatenate([_lanes128(p["ssd_a_log"]), _lanes128(p["ssd_d"]), jnp.zeros((6, BLK), F32)], axis=0)
    y, states = _ssd_scan(xconv, dt, spar, "ssd_scan")
    ssd = _ssd_post(y, proj, p["ssd_norm"], "ssd_post")
    xc = _lru_conv(proj, p["lru_conv_w"], p["lru_conv_b"], "lru_conv")
    wab = jnp.concatenate([_block_diag(p["lru_wa"]), _block_diag(p["lru_wx"])], axis=1).astype(MXU)
    pre = _mm(xc, wab, tm=1024, tn=1024, tk=512, name="mm_lru")
    lpar = jnp.concatenate([p["lru_ba"].reshape(1, -1), p["lru_bx"].reshape(1, -1), p["lru_lambda"].reshape(1, -1),
                            jnp.zeros((5, LRU_W), F32)], axis=0)
    lru, hs = _lru_scan(pre, xc, proj, lpar, "lru_scan")
    mix = jnp.concatenate([att, ssd, lru], axis=1).astype(MXU)
    x1 = _mm(mix, p["w_out"], add=x, tm=1024, tn=1024, tk=1536, name="mm_out")
    h2 = _rms_fwd(x1, p["norm_ffn"], "rms_ffn")
    gu = _mm(h2, p["w_gu"], tm=1024, tn=1408, tk=1024, name="mm_gu")
    act = _swiglu_act(gu, "swiglu_act")
    x2 = _mm(act, p["w_down"], add=x1, tm=1024, tn=1024, tk=2816, name="mm_down")
    saved = dict(x=x, h=h, proj=proj, att=att, lse=lse, xconv=xconv, dt=dt, spar=spar, y=y, states=states, xc=xc, wab=wab,
                 pre=pre, lpar=lpar, hs=hs, mix=mix, x1=x1, h2=h2, gu=gu, act=act)
    return x2, saved


def _layer_bwd(dx2, p, sv):
    g = {}
    da = _mm(dx2, p["w_down"], tb=True, tm=1024, tn=1408, tk=1024, name="mm_d_act")
    g["w_down"] = _mm(sv["act"], dx2, ta=True, tm=1408, tn=1024, tk=1024, name="mm_g_down")
    dgu = _swiglu_bwd(sv["gu"], da, "swiglu_bwd")
    dh2 = _mm(dgu, p["w_gu"], tb=True, tm=1024, tn=1024, tk=1408, name="mm_d_h2")
    g["w_gu"] = _mm(sv["h2"], dgu, ta=True, tm=1024, tn=1408, tk=1024, name="mm_g_gu")
    dx1, gn = _rms_bwd(sv["x1"], p["norm_ffn"], dh2, dx2, "rms_ffn_bwd")
    g["norm_ffn"] = jnp.sum(gn, axis=0)
    dmix = _mm(dx1, p["w_out"], tb=True, tm=1024, tn=1536, tk=1024, name="mm_d_mix")
    g["w_out"] = _mm(sv["mix"], dx1, ta=True, tm=1536, tn=1024, tk=1024, name="mm_g_out")
    datt, dssd, dlru = dmix[:, :ATT_W], dmix[:, ATT_W:ATT_W + SSD_W], dmix[:, ATT_W + SSD_W:]
    proj = sv["proj"]
    dpre, dxc_u, dgl, dlpar = _lru_scan_bwd(sv["pre"], sv["xc"], proj, sv["lpar"], sv["hs"], dlru, "lru_scan_bwd")
    dxc = _mm(dpre, sv["wab"], tb=True, add=dxc_u, tm=1024, tn=512, tk=1024, name="mm_d_xc")
    gwab = _mm(sv["xc"], dpre, ta=True, tm=512, tn=1024, tk=1024, name="mm_g_lru")
    g["lru_wa"], g["lru_wx"] = _block_diag_extract(gwab[:, :LRU_W]), _block_diag_extract(gwab[:, LRU_W:])
    g["lru_ba"], g["lru_bx"], g["lru_lambda"] = dlpar[0], dlpar[1], dlpar[2]
    dxl, gcw, gcb = _lru_conv_bwd(proj, dxc, p["lru_conv_w"], "lru_conv_bwd")
    g["lru_conv_w"], g["lru_conv_b"] = gcw[:CONV_K], jnp.sum(gcb, axis=0)
    dy, dz, gsn = _ssd_post_bwd(sv["y"], proj, p["ssd_norm"], dssd, "ssd_post_bwd")
    g["ssd_norm"] = jnp.sum(gsn, axis=0)
    dxconv, ddt, dal, ddk = _ssd_scan_bwd(sv["xconv"], sv["dt"], sv["spar"], sv["states"], dy, "ssd_scan_bwd")
    g["ssd_a_log"], g["ssd_d"] = dal[0, :8], ddk[0, :8]
    dxbc, ddtr, gsw, gsb, gdb = _ssd_pre_bwd(proj, dxconv, ddt, p["ssd_conv_w"], p["ssd_conv_b"],
                                             _lanes128(p["ssd_dt_bias"]), "ssd_pre_bwd")
    g["ssd_conv_w"], g["ssd_conv_b"], g["ssd_dt_bias"] = gsw[:CONV_K], jnp.sum(gsb, axis=0), jnp.sum(gdb, axis=0)[:8]
    delta = _att_delta(datt, sv["att"], "att_delta")
    acc = None
    for (_, d) in ATT_PATTERNS:
        acc = _att_bwd(proj, datt, sv["lse"], delta, acc, d, "att_bwd_d%d" % d)
    dq, dk, dv = acc
    dproj = jnp.concatenate([dq, dk, dv, dz, dxbc, dgl, dxl, ddtr], axis=1).astype(MXU)
    dh = _mm(dproj, p["w_in"], tb=True, tm=1024, tn=1024, tk=1408, name="mm_d_h")
    g["w_in"] = _mm(sv["h"], dproj, ta=True, tm=1024, tn=1408, tk=1024, name="mm_g_in")
    dx, gm = _rms_bwd(sv["x"], p["norm_mix"], dh, dx1, "rms_mix_bwd")
    g["norm_mix"] = jnp.sum(gm, axis=0)
    return dx, g


def kernel(x, norm_mix, w_in, ssd_conv_w, ssd_conv_b, ssd_dt_bias, ssd_a_log, ssd_d, ssd_norm, lru_conv_w, lru_conv_b, lru_wa, lru_ba, lru_wx, lru_bx, lru_lambda, w_out, norm_ffn, w_gate, w_up, w_down, norm_final, loss_target, m_norm_mix, m_w_in, m_ssd_conv_w, m_ssd_conv_b, m_ssd_dt_bias, m_ssd_a_log, m_ssd_d, m_ssd_norm, m_lru_conv_w, m_lru_conv_b, m_lru_wa, m_lru_ba, m_lru_wx, m_lru_bx, m_lru_lambda, m_w_out, m_norm_ffn, m_w_gate, m_w_up, m_w_down, m_norm_final, v_norm_mix, v_w_in, v_ssd_conv_w, v_ssd_conv_b, v_ssd_dt_bias, v_ssd_a_log, v_ssd_d, v_ssd_norm, v_lru_conv_w, v_lru_conv_b, v_lru_wa, v_lru_ba, v_lru_wx, v_lru_bx, v_lru_lambda, v_w_out, v_norm_ffn, v_w_gate, v_w_up, v_w_down, v_norm_final):
    loc = dict(locals())
    w = {n: loc[n] for n in WEIGHTS}
    m = {n: loc["m_" + n] for n in WEIGHTS}
    v = {n: loc["v_" + n] for n in WEIGHTS}

    mats = [n for n in BIG if n not in CONVS]
    shard_shapes = {n: w[n].shape for n in BIG}
    wbig = _pack([w[n] for n in mats], 1024, 512, MXU)
    wsmall = _pack([w[n] for n in CONVS], BLK, 8, F32)
    gbig, gsmall = _gather_weights(wbig, wsmall)
    full = {}
    per_chip_m = [_unpack(gbig[j], [shard_shapes[n] for n in mats]) for j in range(4)]
    per_chip_c = [_unpack(gsmall[j], [shard_shapes[n] for n in CONVS]) for j in range(4)]
    for i, n in enumerate(mats):
        full[n] = jnp.concatenate([per_chip_m[j][i] for j in range(4)], axis=BIG_AXIS[n])
    for i, n in enumerate(CONVS):
        full[n] = jnp.concatenate([per_chip_c[j][i] for j in range(4)], axis=BIG_AXIS[n])
    w_in_p = _perm_cols(full["w_in"])
    w_gu = jnp.concatenate([full["w_gate"], full["w_up"]], axis=-1)

    def layer_params(l):
        p = {n: w[n][l] for n in SMALL if n != "norm_final"}
        p.update(w_in=w_in_p[l], w_out=full["w_out"][l], w_gu=w_gu[l], w_down=full["w_down"][l],
                 ssd_conv_w=full["ssd_conv_w"][l], lru_conv_w=full["lru_conv_w"][l])
        return p

    xs = x[0]
    saved = []
    for l in range(DEPTH):
        xs, sv = _layer_fwd(xs, layer_params(l))
        saved.append(sv)
    dx, gnf, lsum = _loss_head(xs, norm_final, loss_target[0], "loss_head")
    loss = lax.psum(jnp.sum(lsum), ("x", "y", "c"))
    grads = [None] * DEPTH
    for l in reversed(range(DEPTH)):
        dx, grads[l] = _layer_bwd(dx, layer_params(l), saved[l])

    def stack(n, f=lambda t: t):
        return jnp.stack([f(grads[l][n]) for l in range(DEPTH)], axis=0)

    gfull = {
        "w_in": _unperm_cols(stack("w_in")),
        "w_out": stack("w_out"),
        "w_gate": stack("w_gu", lambda t: t[:, :D_FF]),
        "w_up": stack("w_gu", lambda t: t[:, D_FF:]),
        "w_down": stack("w_down"),
        "ssd_conv_w": stack("ssd_conv_w"),
        "lru_conv_w": stack("lru_conv_w"),
    }
    gsm = {n: stack(n) for n in SMALL if n != "norm_final"}
    gsm["norm_final"] = jnp.sum(gnf, axis=0)

    def shard_of(n, j):
        ax = BIG_AXIS[n]
        size = shard_shapes[n][ax]
        return lax.slice_in_dim(gfull[n], j * size, (j + 1) * size, axis=ax)

    gpack = jnp.stack([_pack([shard_of(n, j) for n in BIG], 1024, 512, F32) for j in range(4)], axis=0)
    me = 2 * lax.axis_index("x") + lax.axis_index("y")
    own = lax.dynamic_index_in_dim(gpack, me, axis=0, keepdims=False)
    others = _exchange_grads(gpack)
    part = _sum_slots(own, others, "sum_chips", 512)
    sib = _swap_sibling(part)
    wl = _pack([w[n] for n in BIG], 1024, 512, F32)
    ml = _pack([m[n] for n in BIG], 1024, 512, F32)
    vl = _pack([v[n] for n in BIG], 1024, 512, F32)
    gb, db, nmb, nvb = _adamw(wl, ml, vl, part, sib, "adamw_big", 512)
    big_shapes = [shard_shapes[n] for n in BIG]
    out_g = dict(zip(BIG, _unpack(gb, big_shapes)))
    out_d = dict(zip(BIG, _unpack(db, big_shapes)))
    out_m = dict(zip(BIG, _unpack(nmb, big_shapes)))
    out_v = dict(zip(BIG, _unpack(nvb, big_shapes)))

    small_shapes = [w[n].shape for n in SMALL]
    gs = _pack([gsm[n].reshape(w[n].shape) for n in SMALL], BLK, 8, F32)
    gall = _gather_small(gs)
    gsum = _sum_slots(None, gall, "sum_devices", gs.shape[0])
    ws = _pack([w[n] for n in SMALL], BLK, 8, F32)
    ms = _pack([m[n] for n in SMALL], BLK, 8, F32)
    vs = _pack([v[n] for n in SMALL], BLK, 8, F32)
    gsr, dsr, nms, nvs = _adamw(ws, ms, vs, gsum, None, "adamw_small", gs.shape[0])
    out_g.update(zip(SMALL, _unpack(gsr, small_shapes)))
    out_d.update(zip(SMALL, _unpack(dsr, small_shapes)))
    out_m.update(zip(SMALL, _unpack(nms, small_shapes)))
    out_v.update(zip(SMALL, _unpack(nvs, small_shapes)))

    return (loss, dx[None], *[out_g[n] for n in WEIGHTS], *[out_d[n] for n in WEIGHTS],
            *[out_m[n] for n in WEIGHTS], *[out_v[n] for n in WEIGHTS])
```

```python
import functools
import math

import jax
import jax.numpy as jnp
import numpy as np
from jax import lax
from jax.experimental import pallas as pl
from jax.experimental.pallas import tpu as pltpu

F32 = jnp.float32
MXU = jnp.bfloat16
HI = lax.Precision.HIGHEST
MESH = pl.DeviceIdType.MESH

D_MODEL = 1024
DEPTH = 2
HEAD_DIM = 64
ATT_W = 512
ATT_PATTERNS = ((128, 1), (512, 4), (2048, 16))
BLK = 128
SSD_W = 512
SSD_STATE = 128
LRU_W = 512
LRU_BLOCKS = 8
LRU_C = 8.0
CONV_K = 4
D_MIX = 1536
D_FF = 2816
IN_COLS = 4104
NP = 4224
NORM_EPS = 1e-6
SSD_NORM_EPS = 1e-5
LN2 = math.log(2.0)
NEG = -1e30

ADAM_LR, ADAM_B1, ADAM_B2, ADAM_EPS, ADAM_WD, ADAM_STEP = 0.001, 0.9, 0.999, 1e-08, 0.01, 10
BC1 = 1.0 - ADAM_B1 ** ADAM_STEP
BC2 = 1.0 - ADAM_B2 ** ADAM_STEP

VMEM_LIMIT = 56 * 1024 * 1024

C_Q, C_K, C_V, C_Z, C_XBC, C_G, C_XL, C_DT = 0, 512, 1024, 1536, 2048, 3072, 3584, 4096


def _cp(*sem):
    return pltpu.CompilerParams(dimension_semantics=sem, vmem_limit_bytes=VMEM_LIMIT)


def _dot(a, b, dims, prec=None):
    return lax.dot_general(a, b, (dims, ((), ())), preferred_element_type=F32, precision=prec)


def _nn(a, b, prec=None):
    return _dot(a, b, ((1,), (0,)), prec)


def _nt(a, b, prec=None):
    return _dot(a, b, ((1,), (1,)), prec)


def _tn(a, b, prec=None):
    return _dot(a, b, ((0,), (0,)), prec)


def _sigmoid(x):
    return jax.nn.sigmoid(x)


def _silu(x):
    return x * _sigmoid(x)


def _softplus(x):
    return jnp.maximum(x, 0.0) + jnp.log(1.0 + jnp.exp(-jnp.abs(x)))


def _gelu(x):
    return 0.5 * x * (1.0 + jnp.tanh(0.7978845608028654 * (x + 0.044715 * x * x * x)))


def _mm(a, b, *, ta=False, tb=False, add=None, out_dtype=F32, tm, tn, tk, name):
    m, k = (a.shape[1], a.shape[0]) if ta else a.shape
    n = b.shape[0] if tb else b.shape[1]
    assert (b.shape[1] if tb else b.shape[0]) == k
    assert m % tm == 0 and n % tn == 0 and k % tk == 0, (name, m, n, k)
    nk = k // tk
    a_spec = pl.BlockSpec((tk, tm), lambda i, j, kk: (kk, i)) if ta else pl.BlockSpec((tm, tk), lambda i, j, kk: (i, kk))
    b_spec = pl.BlockSpec((tn, tk), lambda i, j, kk: (j, kk)) if tb else pl.BlockSpec((tk, tn), lambda i, j, kk: (kk, j))
    o_spec = pl.BlockSpec((tm, tn), lambda i, j, kk: (i, j))
    dims = ((0 if ta else 1,), (1 if tb else 0,))

    def body(*refs):
        if add is None:
            a_ref, b_ref, o_ref, acc = refs
        else:
            a_ref, b_ref, add_ref, o_ref, acc = refs
        kk = pl.program_id(2)

        @pl.when(kk == 0)
        def _():
            acc[...] = jnp.zeros_like(acc)

        acc[...] += _dot(a_ref[...].astype(MXU), b_ref[...].astype(MXU), dims)

        @pl.when(kk == nk - 1)
        def _():
            r = acc[...]
            if add is not None:
                r = r + add_ref[...]
            o_ref[...] = r.astype(out_dtype)

    ins = [a, b] + ([] if add is None else [add])
    specs = [a_spec, b_spec] + ([] if add is None else [o_spec])
    return pl.pallas_call(
        body, name=name, grid=(m // tm, n // tn, nk), in_specs=specs, out_specs=o_spec,
        out_shape=jax.ShapeDtypeStruct((m, n), out_dtype),
        scratch_shapes=[pltpu.VMEM((tm, tn), F32)],
        compiler_params=_cp("parallel", "parallel", "arbitrary"),
    )(*ins)


def _rows(fn, rows, consts=(), outs=(), accs=(), *, tile, name, halos=()):
    rows = [r if isinstance(r, tuple) else (r, r.shape[1], 0) for r in rows]
    s = rows[0][0].shape[0]
    assert s % tile == 0 and tile % 8 == 0
    n = s // tile
    t8 = tile // 8
    nr, nh, nc_, no, na = len(rows), len(halos), len(consts), len(outs), len(accs)

    def body(*refs):
        i = pl.program_id(0)
        rv = [r[...] for r in refs[:nr]]
        hv = []
        for (idx, kind), r in zip(halos, refs[nr:nr + nh]):
            edge = (i == 0) if kind == "prev" else (i == n - 1)
            hv.append(jnp.where(edge, 0.0, r[...]))
        cv = [r[...] for r in refs[nr + nh:nr + nh + nc_]]
        o_refs = refs[nr + nh + nc_:nr + nh + nc_ + no]
        a_refs = refs[nr + nh + nc_ + no:]
        ov, av = fn(rv, hv, cv)
        for r, v in zip(o_refs, ov):
            r[...] = v.astype(r.dtype)
        if na:
            @pl.when(i == 0)
            def _():
                for r in a_refs:
                    r[...] = jnp.zeros_like(r)
            for r, v in zip(a_refs, av):
                r[...] += v

    in_specs = [pl.BlockSpec((tile, w), functools.partial(lambda i, cb: (i, cb), cb=cb)) for (_, w, cb) in rows]
    for idx, kind in halos:
        _, w, cb = rows[idx]
        if kind == "prev":
            in_specs.append(pl.BlockSpec((8, w), functools.partial(lambda i, cb: (jnp.maximum(i * t8 - 1, 0), cb), cb=cb)))
        else:
            in_specs.append(pl.BlockSpec((8, w), functools.partial(lambda i, cb: (jnp.minimum((i + 1) * t8, n * t8 - 1), cb), cb=cb)))
    in_specs += [pl.BlockSpec(c.shape, functools.partial(lambda i, nd: (0,) * nd, nd=c.ndim)) for c in consts]
    out_specs = [pl.BlockSpec((tile, c), lambda i: (i, 0)) for (c, _) in outs]
    out_specs += [pl.BlockSpec((r, c), lambda i: (0, 0)) for (r, c) in accs]
    out_shape = [jax.ShapeDtypeStruct((s, c), dt) for (c, dt) in outs]
    out_shape += [jax.ShapeDtypeStruct((r, c), F32) for (r, c) in accs]
    args = [r[0] for r in rows] + [rows[idx][0] for idx, _ in halos] + list(consts)
    res = pl.pallas_call(
        body, name=name, grid=(n,), in_specs=in_specs, out_specs=out_specs, out_shape=out_shape,
        compiler_params=_cp("arbitrary"),
    )(*args)
    return list(res)


def _colsum8(v):
    t, c = v.shape
    return jnp.sum(v.reshape(t // 8, 8, c), axis=0)


def _rms(x, g):
    return x * lax.rsqrt(jnp.mean(x * x, axis=-1, keepdims=True) + NORM_EPS) * g


def _rms_fwd(x, g, name):
    def fn(rv, hv, cv):
        return [_rms(rv[0], cv[0])], []
    return _rows(fn, [x], [g.reshape(1, -1)], [(x.shape[1], MXU)], tile=512, name=name)[0]


def _rms_bwd(x, g, dh, dres, name):
    def fn(rv, hv, cv):
        xb, dhb, drb = rv
        _, vjp = jax.vjp(_rms, xb, cv[0])
        dx, _ = vjp(dhb)
        rstd = lax.rsqrt(jnp.mean(xb * xb, axis=-1, keepdims=True) + NORM_EPS)
        return [drb + dx], [_colsum8(dhb * xb * rstd)]
    d = x.shape[1]
    return _rows(fn, [x, dh, dres], [g.reshape(1, -1)], [(d, F32)], [(8, d)], tile=512, name=name)


def _slope_dist(hp, hh, dist, dil):
    hf = (2 * hp + hh + 1).astype(F32)
    slope = jnp.exp(jnp.zeros(dist.shape, F32) - hf * LN2)
    return slope * (dist.astype(F32) * float(dil))


def _att_fwd(proj, dil, name):
    s, npc = proj.shape
    l = s // dil
    nb = l // BLK
    cb = npc // BLK
    pv = proj.reshape(l, dil * npc)

    def body(q_ref, kp_ref, kc_ref, vp_ref, vc_ref, num_ref, m_ref, den_ref):
        hp, n = pl.program_id(1), pl.program_id(2)
        lane = lax.broadcasted_iota(jnp.int32, (BLK, BLK), 1)
        q = q_ref[...]
        kk = jnp.concatenate([kp_ref[...], kc_ref[...]], axis=0).astype(MXU)
        vv = jnp.concatenate([vp_ref[...], vc_ref[...]], axis=0).astype(MXU)
        qi = lax.broadcasted_iota(jnp.int32, (BLK, 2 * BLK), 0)
        ki = lax.broadcasted_iota(jnp.int32, (BLK, 2 * BLK), 1)
        dist = BLK + qi - ki
        valid = (dist >= 0) & (dist <= BLK) & ((n > 0) | (ki >= BLK))
        num = jnp.zeros((BLK, BLK), F32)
        mx = jnp.zeros((BLK, BLK), F32)
        den = jnp.zeros((BLK, BLK), F32)
        for hh in (0, 1):
            hmask = (lane < HEAD_DIM) if hh == 0 else (lane >= HEAD_DIM)
            qm = jnp.where(hmask, q, 0.0).astype(MXU)
            sc = _nt(qm, kk) * (HEAD_DIM ** -0.5) - _slope_dist(hp, hh, dist, dil)
            sc = jnp.where(valid, sc, NEG)
            m = jnp.max(sc, axis=1, keepdims=True)
            p = jnp.exp(sc - m)
            dn = jnp.sum(p, axis=1, keepdims=True)
            o = _nn(p.astype(MXU), vv)
            num = jnp.where(hmask, o, num)
            mx = jnp.where(hmask, m, mx)
            den = jnp.where(hmask, dn, den)
        num_ref[...] = num
        m_ref[...] = mx
        den_ref[...] = den

    def col(base):
        return base // BLK

    def cur(base):
        return pl.BlockSpec((BLK, BLK), lambda r, hp, n: (n, r * cb + col(base) + hp))

    def prev(base):
        return pl.BlockSpec((BLK, BLK), lambda r, hp, n: (jnp.maximum(n - 1, 0), r * cb + col(base) + hp))

    o_spec = pl.BlockSpec((BLK, BLK), lambda r, hp, n: (n, r * 4 + hp))
    o_shape = jax.ShapeDtypeStruct((l, dil * ATT_W), F32)
    num, m, den = pl.pallas_call(
        body, name=name, grid=(dil, 4, nb),
        in_specs=[cur(C_Q), prev(C_K), cur(C_K), prev(C_V), cur(C_V)],
        out_specs=[o_spec] * 3, out_shape=[o_shape] * 3,
        compiler_params=_cp("parallel", "parallel", "arbitrary"),
    )(pv, pv, pv, pv, pv)
    return [t.reshape(s, ATT_W) for t in (num, m, den)]


def _att_merge(parts, name):
    def fn(rv, hv, cv):
        ms = rv[1::3]
        m_all = functools.reduce(jnp.maximum, ms)
        num = 0.0
        den = 0.0
        for g in range(len(ms)):
            e = jnp.exp(rv[3 * g + 1] - m_all)
            num = num + rv[3 * g] * e
            den = den + rv[3 * g + 2] * e
        return [num / den, m_all + jnp.log(den)], []
    flat = [t for p in parts for t in p]
    return _rows(fn, flat, [], [(ATT_W, F32), (ATT_W, F32)], tile=512, name=name)


def _att_delta(datt, att, name):
    def fn(rv, hv, cv):
        r = lax.broadcasted_iota(jnp.int32, (ATT_W, ATT_W), 0) // HEAD_DIM
        c = lax.broadcasted_iota(jnp.int32, (ATT_W, ATT_W), 1) // HEAD_DIM
        ones = (r == c).astype(F32)
        return [_nn(rv[0] * rv[1], ones, HI)], []
    return _rows(fn, [datt, att], [], [(ATT_W, F32)], tile=512, name=name)[0]


def _att_bwd(proj, datt, lse, delta, acc, dil, name):
    s, npc = proj.shape
    l = s // dil
    nb = l // BLK
    cb = npc // BLK
    pv = proj.reshape(l, dil * npc)
    wide = [t.reshape(l, dil * ATT_W) for t in (datt, lse, delta)]
    accv = [] if acc is None else [t.reshape(l, dil * ATT_W) for t in acc]
    scale = HEAD_DIM ** -0.5

    def body(qn_ref, qx_ref, kp_ref, kc_ref, vp_ref, vc_ref, don_ref, dox_ref, lsn_ref, lsx_ref, dln_ref, dlx_ref, *rest):
        if acc is None:
            dq_ref, dk_ref, dv_ref = rest
        else:
            aq_ref, ak_ref, av_ref, dq_ref, dk_ref, dv_ref = rest
        hp, n = pl.program_id(1), pl.program_id(2)
        lane = lax.broadcasted_iota(jnp.int32, (BLK, BLK), 1)
        qi = lax.broadcasted_iota(jnp.int32, (BLK, BLK), 0)
        ki = lax.broadcasted_iota(jnp.int32, (BLK, BLK), 1)
        d_far = BLK + qi - ki
        d_near = qi - ki
        qn, qx = qn_ref[...], qx_ref[...]
        kp, kc = kp_ref[...].astype(MXU), kc_ref[...].astype(MXU)
        vp, vc = vp_ref[...].astype(MXU), vc_ref[...].astype(MXU)
        don, dox = don_ref[...], dox_ref[...]
        lsn, lsx, dln, dlx = lsn_ref[...], lsx_ref[...], dln_ref[...], dlx_ref[...]

        def tile(qm, dom, k, v, ls, dl, dist, valid, hh):
            c0 = hh * HEAD_DIM
            sc = _nt(qm, k) * scale - _slope_dist(hp, hh, dist, dil)
            p = jnp.exp(jnp.where(valid, sc - ls[:, c0:c0 + 1], NEG))
            ds = p * (_nt(dom, v) - dl[:, c0:c0 + 1])
            return p.astype(MXU), ds.astype(MXU)

        dq = jnp.zeros((BLK, BLK), F32)
        dk = jnp.zeros((BLK, BLK), F32)
        dv = jnp.zeros((BLK, BLK), F32)
        for hh in (0, 1):
            hmask = (lane < HEAD_DIM) if hh == 0 else (lane >= HEAD_DIM)
            qnm = jnp.where(hmask, qn, 0.0).astype(MXU)
            qxm = jnp.where(hmask, qx, 0.0).astype(MXU)
            donm = jnp.where(hmask, don, 0.0).astype(MXU)
            doxm = jnp.where(hmask, dox, 0.0).astype(MXU)
            _, ds_a = tile(qnm, donm, kp, vp, lsn, dln, d_far, (d_far <= BLK) & (n > 0), hh)
            p_b, ds_b = tile(qnm, donm, kc, vc, lsn, dln, d_near, d_near >= 0, hh)
            p_c, ds_c = tile(qxm, doxm, kc, vc, lsx, dlx, d_far, (d_far <= BLK) & (n < nb - 1), hh)
            dq = jnp.where(hmask, _nn(ds_a, kp) + _nn(ds_b, kc), dq)
            dk = dk + _tn(ds_b, qnm) + _tn(ds_c, qxm)
            dv = dv + _tn(p_b, donm) + _tn(p_c, doxm)
        dq, dk = dq * scale, dk * scale
        if acc is not None:
            dq, dk, dv = dq + aq_ref[...], dk + ak_ref[...], dv + av_ref[...]
        dq_ref[...] = dq
        dk_ref[...] = dk
        dv_ref[...] = dv

    def pspec(base, shift):
        def idx(r, hp, n):
            return (jnp.clip(n + shift, 0, nb - 1), r * cb + base // BLK + hp)
        return pl.BlockSpec((BLK, BLK), idx)

    def wspec(shift):
        def idx(r, hp, n):
            return (jnp.clip(n + shift, 0, nb - 1), r * 4 + hp)
        return pl.BlockSpec((BLK, BLK), idx)

    in_specs = [pspec(C_Q, 0), pspec(C_Q, 1), pspec(C_K, -1), pspec(C_K, 0), pspec(C_V, -1), pspec(C_V, 0),
                wspec(0), wspec(1), wspec(0), wspec(1), wspec(0), wspec(1)] + [wspec(0)] * len(accv)
    o_shape = jax.ShapeDtypeStruct((l, dil * ATT_W), F32)
    res = pl.pallas_call(
        body, name=name, grid=(dil, 4, nb), in_specs=in_specs,
        out_specs=[wspec(0)] * 3, out_shape=[o_shape] * 3,
        compiler_params=_cp("parallel", "parallel", "arbitrary"),
    )(pv, pv, pv, pv, pv, pv, wide[0], wide[0], wide[1], wide[1], wide[2], wide[2], *accv)
    return [t.reshape(s, ATT_W) for t in res]


ATT_G = 2048


def _deinterleave(dst, src, dil, ld, region, offset):
    for r in range(dil):
        rows = pl.ds(r, ld, stride=dil) if dil > 1 else pl.ds(0, ld)
        dst[r * region + offset:r * region + offset + ld, :] = src[rows, :]


def _deinterleave_edge(dst, src, dil, region, offset, first_row):
    for r in range(dil):
        rows = pl.ds(first_row + r, BLK, stride=dil) if dil > 1 else pl.ds(first_row, BLK)
        dst[r * region + offset:r * region + offset + BLK, :] = src[rows, :]


def _att_fwd_fused(proj, name):
    s, npc = proj.shape
    gsz = ATT_G
    ng = s // gsz
    assert s % gsz == 0
    scale = HEAD_DIM ** -0.5

    def body(q_ref, kp_ref, kc_ref, vp_ref, vc_ref, att_ref, lse_ref, qd, kd, vd, nd, md, dd, nn, mn, dn):
        hp, g = pl.program_id(0), pl.program_id(1)
        lane = lax.broadcasted_iota(jnp.int32, (BLK, BLK), 1)
        qi = lax.broadcasted_iota(jnp.int32, (BLK, 2 * BLK), 0)
        ki = lax.broadcasted_iota(jnp.int32, (BLK, 2 * BLK), 1)
        dist = BLK + qi - ki
        band = (dist >= 0) & (dist <= BLK)
        for pi, (_, dil) in enumerate(ATT_PATTERNS):
            ld = gsz // dil
            nbg = ld // BLK
            _deinterleave(qd, q_ref, dil, ld, ld, 0)
            _deinterleave(kd, kc_ref, dil, ld, ld + BLK, BLK)
            _deinterleave(vd, vc_ref, dil, ld, ld + BLK, BLK)
            _deinterleave_edge(kd, kp_ref, dil, ld + BLK, 0, gsz - BLK * dil)
            _deinterleave_edge(vd, vp_ref, dil, ld + BLK, 0, gsz - BLK * dil)
            bias = [_slope_dist(hp, hh, dist, dil) for hh in (0, 1)]

            def tile(t, carry, ld=ld, nbg=nbg, bias=bias):
                r, b = t // nbg, t % nbg
                qo = pl.multiple_of(r * ld + b * BLK, BLK)
                ko = pl.multiple_of(r * (ld + BLK) + b * BLK, BLK)
                q = qd[pl.ds(qo, BLK), :]
                kk = kd[pl.ds(ko, 2 * BLK), :].astype(MXU)
                vv = vd[pl.ds(ko, 2 * BLK), :].astype(MXU)
                valid = band & ((g > 0) | (b > 0) | (ki >= BLK))
                num = jnp.zeros((BLK, BLK), F32)
                mx = jnp.zeros((BLK, BLK), F32)
                den = jnp.zeros((BLK, BLK), F32)
                for hh in (0, 1):
                    hmask = (lane < HEAD_DIM) if hh == 0 else (lane >= HEAD_DIM)
                    qm = jnp.where(hmask, q, 0.0).astype(MXU)
                    sc = jnp.where(valid, _nt(qm, kk) * scale - bias[hh], NEG)
                    m = jnp.max(sc, axis=1, keepdims=True)
                    p = jnp.exp(sc - m)
                    dn_ = jnp.sum(p, axis=1, keepdims=True)
                    o = _nn(p.astype(MXU), vv)
                    num = jnp.where(hmask, o, num)
                    mx = jnp.where(hmask, m, mx)
                    den = jnp.where(hmask, dn_, den)
                nd[pl.ds(qo, BLK), :] = num
                md[pl.ds(qo, BLK), :] = mx
                dd[pl.ds(qo, BLK), :] = den
                return carry

            lax.fori_loop(0, dil * nbg, tile, 0)
            for r in range(dil):
                rows = pl.ds(r, ld, stride=dil) if dil > 1 else pl.ds(0, ld)
                nn.at[pi][rows, :] = nd[r * ld:(r + 1) * ld, :]
                mn.at[pi][rows, :] = md[r * ld:(r + 1) * ld, :]
                dn.at[pi][rows, :] = dd[r * ld:(r + 1) * ld, :]

        def merge(c, carry):
            rows = pl.ds(pl.multiple_of(c * 256, 256), 256)
            ms = [mn[pi, rows, :] for pi in range(len(ATT_PATTERNS))]
            m_all = functools.reduce(jnp.maximum, ms)
            num = jnp.zeros((256, BLK), F32)
            den = jnp.zeros((256, BLK), F32)
            for pi in range(len(ATT_PATTERNS)):
                e = jnp.exp(ms[pi] - m_all)
                num = num + nn[pi, rows, :] * e
                den = den + dn[pi, rows, :] * e
            att_ref[rows, :] = num / den
            lse_ref[rows, :] = m_all + jnp.log(den)
            return carry

        lax.fori_loop(0, gsz // 256, merge, 0)

    def cur(base):
        return pl.BlockSpec((gsz, BLK), lambda hp, g: (g, base // BLK + hp))

    def prev(base):
        return pl.BlockSpec((gsz, BLK), lambda hp, g: (jnp.maximum(g - 1, 0), base // BLK + hp))

    o_spec = pl.BlockSpec((gsz, BLK), lambda hp, g: (g, hp))
    npat = len(ATT_PATTERNS)
    return pl.pallas_call(
        body, name=name, grid=(4, ng),
        in_specs=[cur(C_Q), prev(C_K), cur(C_K), prev(C_V), cur(C_V)],
        out_specs=[o_spec] * 2, out_shape=[jax.ShapeDtypeStruct((s, ATT_W), F32)] * 2,
        scratch_shapes=[pltpu.VMEM((gsz, BLK), F32), pltpu.VMEM((2 * gsz, BLK), F32), pltpu.VMEM((2 * gsz, BLK), F32)]
        + [pltpu.VMEM((gsz, BLK), F32)] * 3 + [pltpu.VMEM((npat, gsz, BLK), F32)] * 3,
        compiler_params=_cp("parallel", "arbitrary"),
    )(proj, proj, proj, proj, proj)


def _att_bwd_fused(proj, datt, lse, delta, name):
    s, npc = proj.shape
    gsz = ATT_G
    ng = s // gsz
    scale = HEAD_DIM ** -0.5

    def body(qc_ref, qn_ref, kp_ref, kc_ref, vp_ref, vc_ref, doc_ref, don_ref, lsc_ref, lsn_ref, dlc_ref, dln_ref,
             dq_ref, dk_ref, dv_ref, qd, dod, lsd, dld, kd, vd, dqd, dkd, dvd):
        hp, g = pl.program_id(0), pl.program_id(1)
        lane = lax.broadcasted_iota(jnp.int32, (BLK, BLK), 1)
        qi = lax.broadcasted_iota(jnp.int32, (BLK, BLK), 0)
        ki = lax.broadcasted_iota(jnp.int32, (BLK, BLK), 1)
        d_far = BLK + qi - ki
        d_near = qi - ki
        for pi, (_, dil) in enumerate(ATT_PATTERNS):
            ld = gsz // dil
            nbg = ld // BLK
            reg = ld + BLK
            for dst, c_ref, n_ref in ((qd, qc_ref, qn_ref), (dod, doc_ref, don_ref), (lsd, lsc_ref, lsn_ref), (dld, dlc_ref, dln_ref)):
                _deinterleave(dst, c_ref, dil, ld, reg, 0)
                _deinterleave_edge(dst, n_ref, dil, reg, ld, 0)
            for dst, p_ref, c_ref in ((kd, kp_ref, kc_ref), (vd, vp_ref, vc_ref)):
                _deinterleave(dst, c_ref, dil, ld, reg, BLK)
                _deinterleave_edge(dst, p_ref, dil, reg, 0, gsz - BLK * dil)
            b_far = [_slope_dist(hp, hh, d_far, dil) for hh in (0, 1)]
            b_near = [_slope_dist(hp, hh, d_near, dil) for hh in (0, 1)]

            def tile(t, carry, ld=ld, nbg=nbg, reg=reg, b_far=b_far, b_near=b_near):
                r, b = t // nbg, t % nbg
                oo = pl.multiple_of(r * ld + b * BLK, BLK)
                ro = pl.multiple_of(r * reg + b * BLK, BLK)
                qn, qx = qd[pl.ds(ro, BLK), :], qd[pl.ds(ro + BLK, BLK), :]
                don, dox = dod[pl.ds(ro, BLK), :], dod[pl.ds(ro + BLK, BLK), :]
                lsn, lsx = lsd[pl.ds(ro, BLK), :], lsd[pl.ds(ro + BLK, BLK), :]
                dln, dlx = dld[pl.ds(ro, BLK), :], dld[pl.ds(ro + BLK, BLK), :]
                kp, kc = kd[pl.ds(ro, BLK), :].astype(MXU), kd[pl.ds(ro + BLK, BLK), :].astype(MXU)
                vp, vc = vd[pl.ds(ro, BLK), :].astype(MXU), vd[pl.ds(ro + BLK, BLK), :].astype(MXU)
                ok_a = (d_far <= BLK) & ((g > 0) | (b > 0))
                ok_b = d_near >= 0
                ok_c = (d_far <= BLK) & ((g < ng - 1) | (b < nbg - 1))

                def grads(qm, dom, k, v, ls, dl, bias, valid, hh):
                    c0 = hh * HEAD_DIM
                    sc = _nt(qm, k) * scale - bias
                    p = jnp.exp(jnp.where(valid, sc - ls[:, c0:c0 + 1], NEG))
                    ds = p * (_nt(dom, v) - dl[:, c0:c0 + 1])
                    return p.astype(MXU), ds.astype(MXU)

                dq = jnp.zeros((BLK, BLK), F32)
                dk = jnp.zeros((BLK, BLK), F32)
                dv = jnp.zeros((BLK, BLK), F32)
                for hh in (0, 1):
                    hmask = (lane < HEAD_DIM) if hh == 0 else (lane >= HEAD_DIM)
                    qnm = jnp.where(hmask, qn, 0.0).astype(MXU)
                    qxm = jnp.where(hmask, qx, 0.0).astype(MXU)
                    donm = jnp.where(hmask, don, 0.0).astype(MXU)
                    doxm = jnp.where(hmask, dox, 0.0).astype(MXU)
                    _, ds_a = grads(qnm, donm, kp, vp, lsn, dln, b_far[hh], ok_a, hh)
                    p_b, ds_b = grads(qnm, donm, kc, vc, lsn, dln, b_near[hh], ok_b, hh)
                    p_c, ds_c = grads(qxm, doxm, kc, vc, lsx, dlx, b_far[hh], ok_c, hh)
                    dq = jnp.where(hmask, _nn(ds_a, kp) + _nn(ds_b, kc), dq)
                    dk = dk + _tn(ds_b, qnm) + _tn(ds_c, qxm)
                    dv = dv + _tn(p_b, donm) + _tn(p_c, doxm)
                dqd[pl.ds(oo, BLK), :] = dq * scale
                dkd[pl.ds(oo, BLK), :] = dk * scale
                dvd[pl.ds(oo, BLK), :] = dv
                return carry

            lax.fori_loop(0, dil * nbg, tile, 0)
            for out, src in ((dq_ref, dqd), (dk_ref, dkd), (dv_ref, dvd)):
                for r in range(dil):
                    rows = pl.ds(r, ld, stride=dil) if dil > 1 else pl.ds(0, ld)
                    if pi == 0:
                        out[rows, :] = src[r * ld:(r + 1) * ld, :]
                    else:
                        out[rows, :] = out[rows, :] + src[r * ld:(r + 1) * ld, :]

    def pspec(base, shift):
        return pl.BlockSpec((gsz, BLK), lambda hp, g: (jnp.clip(g + shift, 0, ng - 1), base // BLK + hp))

    def wspec(shift):
        return pl.BlockSpec((gsz, BLK), lambda hp, g: (jnp.clip(g + shift, 0, ng - 1), hp))

    in_specs = [pspec(C_Q, 0), pspec(C_Q, 1), pspec(C_K, -1), pspec(C_K, 0), pspec(C_V, -1), pspec(C_V, 0),
                wspec(0), wspec(1), wspec(0), wspec(1), wspec(0), wspec(1)]
    return pl.pallas_call(
        body, name=name, grid=(4, ng), in_specs=in_specs,
        out_specs=[wspec(0)] * 3, out_shape=[jax.ShapeDtypeStruct((s, ATT_W), F32)] * 3,
        scratch_shapes=[pltpu.VMEM((2 * gsz, BLK), F32)] * 6 + [pltpu.VMEM((gsz, BLK), F32)] * 3,
        compiler_params=_cp("parallel", "arbitrary"),
    )(proj, proj, proj, proj, proj, proj, datt, datt, lse, lse, delta, delta)


def _shift_down(cur, halo, sft):
    if sft == 0:
        return cur
    t = cur.shape[0]
    rolled = pltpu.roll(cur, sft, 0)
    hr = pltpu.roll(halo, sft, 0)
    row = lax.broadcasted_iota(jnp.int32, cur.shape, 0)
    return jnp.where(row < sft, jnp.tile(hr, (t // 8, 1)), rolled)


def _shift_up(cur, halo, sft):
    if sft == 0:
        return cur
    t = cur.shape[0]
    rolled = pltpu.roll(cur, t - sft, 0)
    hr = pltpu.roll(halo, 8 - sft, 0)
    row = lax.broadcasted_iota(jnp.int32, cur.shape, 0)
    return jnp.where(row >= t - sft, jnp.tile(hr, (t // 8, 1)), rolled)


def _conv(x, xh, w, b):
    y = b + x * w[CONV_K - 1:CONV_K]
    for k in range(CONV_K - 1):
        y = y + _shift_down(x, xh, CONV_K - 1 - k) * w[k:k + 1]
    return y


def _conv_bwd(x, xh, dy, dyh, w):
    dx = dy * w[CONV_K - 1:CONV_K]
    dws = []
    for k in range(CONV_K - 1):
        sft = CONV_K - 1 - k
        dx = dx + _shift_up(dy, dyh, sft) * w[k:k + 1]
        dws.append(jnp.sum(dy * _shift_down(x, xh, sft), axis=0, keepdims=True))
    dws.append(jnp.sum(dy * x, axis=0, keepdims=True))
    c = x.shape[1]
    dw = jnp.concatenate(dws + [jnp.zeros((8 - CONV_K, c), F32)], axis=0)
    return dx, dw, jnp.sum(dy, axis=0, keepdims=True)


def _pad8(w):
    return jnp.concatenate([w, jnp.zeros((8 - w.shape[0], w.shape[1]), w.dtype)], axis=0)


def _ssd_pre(proj, conv_w, conv_b, dt_bias128, name):
    def fn(rv, hv, cv):
        xbc, dtr = rv
        return [_silu(_conv(xbc, hv[0], cv[0], cv[1])), _softplus(dtr + cv[2])], []
    return _rows(fn, [(proj, 1024, C_XBC // 1024), (proj, BLK, C_DT // BLK)],
                 [_pad8(conv_w), conv_b.reshape(1, -1), dt_bias128],
                 [(1024, F32), (BLK, F32)], tile=256, name=name, halos=[(0, "prev")])


def _ssd_pre_bwd(proj, dxc, ddt, conv_w, conv_b, dt_bias128, name):
    def fn(rv, hv, cv):
        xbc, dtr, dxcb, ddtb = rv
        xh, dxch_raw, xnext = hv
        w, b, bias = cv
        pre = _conv(xbc, xh, w, b)
        sg = _sigmoid(pre)
        dpre = dxcb * (sg * (1.0 + pre * (1.0 - sg)))
        t = xbc.shape[0]
        tail = jnp.concatenate([xbc[t - 8:], xnext], axis=0)
        pre_n = _conv(tail[8:], tail[:8], w, b)
        sgn = _sigmoid(pre_n)
        dpre_h = dxch_raw * (sgn * (1.0 + pre_n * (1.0 - sgn)))
        dx, dw, db = _conv_bwd(xbc, xh, dpre, dpre_h, w)
        ddr = ddtb * _sigmoid(dtr + bias)
        return [dx, ddr], [dw, jnp.concatenate([db, jnp.zeros((7, db.shape[1]), F32)], axis=0), _colsum8(ddr)]
    return _rows(fn, [(proj, 1024, C_XBC // 1024), (proj, BLK, C_DT // BLK), dxc, ddt],
                 [_pad8(conv_w), conv_b.reshape(1, -1), dt_bias128],
                 [(1024, F32), (BLK, F32)], [(8, 1024), (8, 1024), (8, BLK)], tile=256, name=name,
                 halos=[(0, "prev"), (2, "next"), (0, "next")])


def _head_cols(v, h0):
    lane = lax.broadcasted_iota(jnp.int32, (v.shape[0], BLK), 1)
    return jnp.where(lane < HEAD_DIM, v[:, h0:h0 + 1], v[:, h0 + 1:h0 + 2])


def _ssd_scan(xc, dt, par, name):
    s = xc.shape[0]
    nc = s // BLK

    def body(x_ref, dt_ref, par_ref, y_ref, st_ref, h_ref):
        c = pl.program_id(0)

        @pl.when(c == 0)
        def _():
            h_ref[...] = jnp.zeros_like(h_ref)

        st_ref[0] = h_ref[...]
        dt = dt_ref[...]
        a_row = -jnp.exp(par_ref[0:1, :])
        d_row = par_ref[1:2, :]
        ri = lax.broadcasted_iota(jnp.int32, (BLK, BLK), 0)
        ci = lax.broadcasted_iota(jnp.int32, (BLK, BLK), 1)
        tril = ri >= ci
        cs = _nn(tril.astype(F32), dt * a_row, HI)
        cst, dtt = cs.T, dt.T
        last = cs[BLK - 1:BLK, :]
        wcol = jnp.exp(last - cs) * dt
        ecs = jnp.exp(cs)
        elast = jnp.exp(last)
        for g in (0, 1):
            bg = x_ref[:, 512 + g * BLK:512 + (g + 1) * BLK].astype(MXU)
            cg = x_ref[:, 768 + g * BLK:768 + (g + 1) * BLK].astype(MXU)
            gm = _nt(cg, bg)
            for pp in (0, 1):
                pr = 2 * g + pp
                h0 = 2 * pr
                x2 = x_ref[:, pr * BLK:(pr + 1) * BLK]
                hprev = h_ref[pr * BLK:(pr + 1) * BLK, :]
                yp = jnp.zeros((BLK, BLK), F32)
                for hh in (0, 1):
                    h = h0 + hh
                    hmask = (ci < HEAD_DIM) if hh == 0 else (ci >= HEAD_DIM)
                    lm = jnp.exp(jnp.where(tril, cs[:, h:h + 1] - cst[h:h + 1, :], NEG))
                    mm = gm * lm * dtt[h:h + 1, :]
                    yp = yp + _nn(mm.astype(MXU), jnp.where(hmask, x2, 0.0).astype(MXU))
                y0 = _nt(cg, hprev.astype(MXU))
                y_ref[:, pr * BLK:(pr + 1) * BLK] = yp + _head_cols(ecs, h0) * y0 + _head_cols(d_row, h0) * x2
                dec = jnp.where(ri < HEAD_DIM, elast[:, h0:h0 + 1], elast[:, h0 + 1:h0 + 2])
                xw = (x2 * _head_cols(wcol, h0)).astype(MXU)
                h_ref[pr * BLK:(pr + 1) * BLK, :] = dec * hprev + _tn(xw, bg)

    return pl.pallas_call(
        body, name=name, grid=(nc,),
        in_specs=[pl.BlockSpec((BLK, 1024), lambda c: (c, 0)), pl.BlockSpec((BLK, BLK), lambda c: (c, 0)),
                  pl.BlockSpec((8, BLK), lambda c: (0, 0))],
        out_specs=[pl.BlockSpec((BLK, SSD_W), lambda c: (c, 0)), pl.BlockSpec((1, SSD_W, SSD_STATE), lambda c: (c, 0, 0))],
        out_shape=[jax.ShapeDtypeStruct((s, SSD_W), F32), jax.ShapeDtypeStruct((nc, SSD_W, SSD_STATE), F32)],
        scratch_shapes=[pltpu.VMEM((SSD_W, SSD_STATE), F32)],
        compiler_params=_cp("arbitrary"),
    )(xc, dt, par)


def _ssd_scan_bwd(xc, dt, par, st, dy, name):
    s = xc.shape[0]
    nc = s // BLK

    def body(x_ref, dt_ref, par_ref, st_ref, dy_ref, dx_ref, ddt_ref, dal_ref, dd_ref, dh_ref):
        c = pl.program_id(0)

        @pl.when(c == 0)
        def _():
            dh_ref[...] = jnp.zeros_like(dh_ref)
            dal_ref[...] = jnp.zeros_like(dal_ref)
            dd_ref[...] = jnp.zeros_like(dd_ref)

        dt = dt_ref[...]
        a_row = -jnp.exp(par_ref[0:1, :])
        d_row = par_ref[1:2, :]
        ri = lax.broadcasted_iota(jnp.int32, (BLK, BLK), 0)
        ci = lax.broadcasted_iota(jnp.int32, (BLK, BLK), 1)
        tril = ri >= ci
        cs = _nn(tril.astype(F32), dt * a_row, HI)
        cst, dtt = cs.T, dt.T
        last = cs[BLK - 1:BLK, :]
        tolast = jnp.exp(last - cs)
        wcol = tolast * dt
        ecs = jnp.exp(cs)
        elast = jnp.exp(last)
        dcs_col = jnp.zeros((BLK, BLK), F32)
        ddt_col = jnp.zeros((BLK, BLK), F32)
        dcs_row = jnp.zeros((BLK, BLK), F32)
        ddt_row = jnp.zeros((BLK, BLK), F32)
        dlast = jnp.zeros((1, BLK), F32)
        ddsk = jnp.zeros((1, BLK), F32)
        for g in (0, 1):
            bg32 = x_ref[:, 512 + g * BLK:512 + (g + 1) * BLK]
            cg32 = x_ref[:, 768 + g * BLK:768 + (g + 1) * BLK]
            bg, cg = bg32.astype(MXU), cg32.astype(MXU)
            gm = _nt(cg, bg)
            dgm = jnp.zeros((BLK, BLK), F32)
            dbg = jnp.zeros((BLK, BLK), F32)
            dcg = jnp.zeros((BLK, BLK), F32)
            for pp in (0, 1):
                pr = 2 * g + pp
                h0 = 2 * pr
                x2 = x_ref[:, pr * BLK:(pr + 1) * BLK]
                dy2 = dy_ref[:, pr * BLK:(pr + 1) * BLK]
                hprev = st_ref[0, pr * BLK:(pr + 1) * BLK, :]
                dhn = dh_ref[pr * BLK:(pr + 1) * BLK, :]
                x2m, dhnm = x2.astype(MXU), dhn.astype(MXU)
                zb = _nt(bg, dhnm)
                y0 = _nt(cg, hprev.astype(MXU))
                esel = _head_cols(ecs, h0)
                wsel = _head_cols(wcol, h0)
                dx2 = _head_cols(d_row, h0) * dy2 + wsel * zb
                r_off = dy2 * y0
                r_w = x2 * zb
                r_d = dy2 * x2
                r_h = dhn * hprev
                for hh in (0, 1):
                    h = h0 + hh
                    hmask = (ci < HEAD_DIM) if hh == 0 else (ci >= HEAD_DIM)
                    onl = (ci == h).astype(F32)
                    ons = (ri == h).astype(F32)
                    dym = jnp.where(hmask, dy2, 0.0).astype(MXU)
                    dt_r = dtt[h:h + 1, :]
                    lm = jnp.exp(jnp.where(tril, cs[:, h:h + 1] - cst[h:h + 1, :], NEG))
                    mm = gm * lm * dt_r
                    dx2 = dx2 + _tn(mm.astype(MXU), dym)
                    dm = _nt(dym, x2m)
                    t1 = dm * lm
                    dgm = dgm + t1 * dt_r
                    tt = t1 * gm
                    ddt_row = ddt_row + ons * jnp.sum(tt, axis=0, keepdims=True)
                    t = tt * dt_r
                    dcs_col = dcs_col + onl * jnp.sum(t, axis=1, keepdims=True)
                    dcs_row = dcs_row - ons * jnp.sum(t, axis=0, keepdims=True)
                    de = jnp.sum(jnp.where(hmask, r_off, 0.0), axis=1, keepdims=True)
                    dcs_col = dcs_col + onl * (ecs[:, h:h + 1] * de)
                    hrow = (ri < HEAD_DIM) if hh == 0 else (ri >= HEAD_DIM)
                    dl_h = elast[:, h:h + 1] * jnp.sum(jnp.where(hrow, r_h, 0.0), keepdims=True)
                    dw = jnp.sum(jnp.where(hmask, r_w, 0.0), axis=1, keepdims=True)
                    ddt_col = ddt_col + onl * (dw * tolast[:, h:h + 1])
                    v = dw * wcol[:, h:h + 1]
                    dcs_col = dcs_col - onl * v
                    dl_h = dl_h + jnp.sum(v, keepdims=True)
                    dlast = dlast + onl[0:1, :] * dl_h
                    ddsk = ddsk + onl[0:1, :] * jnp.sum(jnp.where(hmask, r_d, 0.0), keepdims=True)
                dx_ref[:, pr * BLK:(pr + 1) * BLK] = dx2
                edy = (esel * dy2).astype(MXU)
                dcg = dcg + _nn(edy, hprev.astype(MXU))
                dec = jnp.where(ri < HEAD_DIM, elast[:, h0:h0 + 1], elast[:, h0 + 1:h0 + 2])
                dh_ref[pr * BLK:(pr + 1) * BLK, :] = dec * dhn + _tn(edy, cg)
                dbg = dbg + _nn((x2 * wsel).astype(MXU), dhnm)
            dgmm = dgm.astype(MXU)
            dx_ref[:, 512 + g * BLK:512 + (g + 1) * BLK] = dbg + _tn(dgmm, cg)
            dx_ref[:, 768 + g * BLK:768 + (g + 1) * BLK] = dcg + _nn(dgmm, bg)
        dcs = dcs_col + dcs_row.T + jnp.where(ri == BLK - 1, dlast, 0.0)
        dda = _nn((ri <= ci).astype(F32), dcs, HI)
        ddt_ref[...] = ddt_col + ddt_row.T + a_row * dda
        da = jnp.sum(dt * dda, axis=0, keepdims=True)
        dal_ref[0:1, :] += da * a_row
        dd_ref[0:1, :] += ddsk

    rev = lambda c: (nc - 1 - c, 0)
    res = pl.pallas_call(
        body, name=name, grid=(nc,),
        in_specs=[pl.BlockSpec((BLK, 1024), rev), pl.BlockSpec((BLK, BLK), rev), pl.BlockSpec((8, BLK), lambda c: (0, 0)),
                  pl.BlockSpec((1, SSD_W, SSD_STATE), lambda c: (nc - 1 - c, 0, 0)), pl.BlockSpec((BLK, SSD_W), rev)],
        out_specs=[pl.BlockSpec((BLK, 1024), rev), pl.BlockSpec((BLK, BLK), rev),
                   pl.BlockSpec((8, BLK), lambda c: (0, 0)), pl.BlockSpec((8, BLK), lambda c: (0, 0))],
        out_shape=[jax.ShapeDtypeStruct((s, 1024), F32), jax.ShapeDtypeStruct((s, BLK), F32),
                   jax.ShapeDtypeStruct((8, BLK), F32), jax.ShapeDtypeStruct((8, BLK), F32)],
        scratch_shapes=[pltpu.VMEM((SSD_W, SSD_STATE), F32)],
        compiler_params=_cp("arbitrary"),
    )(xc, dt, par, st, dy)
    return res


def _ssd_gate(y, z, w):
    t = y * _silu(z)
    outs = []
    for g in (0, 1):
        tg = t[:, g * 256:(g + 1) * 256]
        outs.append(tg * lax.rsqrt(jnp.mean(tg * tg, axis=-1, keepdims=True) + SSD_NORM_EPS))
    return jnp.concatenate(outs, axis=1) * w


def _ssd_post(y, proj, norm_w, name):
    def fn(rv, hv, cv):
        return [_ssd_gate(rv[0], rv[1], cv[0])], []
    return _rows(fn, [y, (proj, SSD_W, C_Z // SSD_W)], [norm_w.reshape(1, -1)], [(SSD_W, F32)], tile=512, name=name)[0]


def _ssd_post_bwd(y, proj, norm_w, dout, name):
    def fn(rv, hv, cv):
        yb, zb, db = rv
        _, vjp = jax.vjp(lambda a, b: _ssd_gate(a, b, cv[0]), yb, zb)
        dy, dz = vjp(db)
        t = yb * _silu(zb)
        nrm = []
        for g in (0, 1):
            tg = t[:, g * 256:(g + 1) * 256]
            nrm.append(tg * lax.rsqrt(jnp.mean(tg * tg, axis=-1, keepdims=True) + SSD_NORM_EPS))
        return [dy, dz], [_colsum8(db * jnp.concatenate(nrm, axis=1))]
    return _rows(fn, [y, (proj, SSD_W, C_Z // SSD_W), dout], [norm_w.reshape(1, -1)],
                 [(SSD_W, F32), (SSD_W, F32)], [(8, SSD_W)], tile=512, name=name)


LRU_T = 256


def _lru_conv(proj, conv_w, conv_b, name):
    def fn(rv, hv, cv):
        return [_conv(rv[0], hv[0], cv[0], cv[1])], []
    return _rows(fn, [(proj, LRU_W, C_XL // LRU_W)], [_pad8(conv_w), conv_b.reshape(1, -1)], [(LRU_W, F32)],
                 tile=512, name=name, halos=[(0, "prev")])[0]


def _lru_conv_bwd(proj, dxc, conv_w, name):
    def fn(rv, hv, cv):
        dx, dw, db = _conv_bwd(rv[0], hv[0], rv[1], hv[1], cv[0])
        return [dx], [dw, jnp.concatenate([db, jnp.zeros((7, db.shape[1]), F32)], axis=0)]
    return _rows(fn, [(proj, LRU_W, C_XL // LRU_W), dxc], [_pad8(conv_w)], [(LRU_W, F32)], [(8, LRU_W), (8, LRU_W)],
                 tile=512, name=name, halos=[(0, "prev"), (1, "next")])


def _lru_au(pre_a, pre_x, xc, ba, bx, lam):
    r = _sigmoid(pre_a + ba)
    i = _sigmoid(pre_x + bx)
    log_a = -LRU_C * r * _softplus(-lam)
    a = jnp.exp(log_a)
    u = jnp.sqrt(1.0 - jnp.exp(2.0 * log_a)) * (i * xc)
    return a, u


def _lru_scan(pre, xc, proj, par, name):
    s = xc.shape[0]
    t = LRU_T

    def body(pre_ref, xc_ref, g_ref, par_ref, out_ref, h_ref, carry):
        c = pl.program_id(0)

        @pl.when(c == 0)
        def _():
            carry[...] = jnp.zeros_like(carry)

        a, u = _lru_au(pre_ref[:, :LRU_W], pre_ref[:, LRU_W:], xc_ref[...], par_ref[0:1, :], par_ref[1:2, :], par_ref[2:3, :])
        row = lax.broadcasted_iota(jnp.int32, (t, LRU_W), 0)
        sft = 1
        while sft < t:
            keep = row >= sft
            a_s = jnp.where(keep, pltpu.roll(a, sft, 0), 1.0)
            u_s = jnp.where(keep, pltpu.roll(u, sft, 0), 0.0)
            u = a * u_s + u
            a = a * a_s
            sft *= 2
        h = a * carry[0:1, :] + u
        h_ref[...] = h
        out_ref[...] = h * _gelu(g_ref[...])
        carry[0:1, :] = h[t - 1:t, :]

    return pl.pallas_call(
        body, name=name, grid=(s // t,),
        in_specs=[pl.BlockSpec((t, 2 * LRU_W), lambda c: (c, 0)), pl.BlockSpec((t, LRU_W), lambda c: (c, 0)),
                  pl.BlockSpec((t, LRU_W), lambda c: (c, C_G // LRU_W)), pl.BlockSpec((8, LRU_W), lambda c: (0, 0))],
        out_specs=[pl.BlockSpec((t, LRU_W), lambda c: (c, 0))] * 2,
        out_shape=[jax.ShapeDtypeStruct((s, LRU_W), F32)] * 2,
        scratch_shapes=[pltpu.VMEM((8, LRU_W), F32)],
        compiler_params=_cp("arbitrary"),
    )(pre, xc, proj, par)


def _lru_scan_bwd(pre, xc, proj, par, h, dout, name):
    s = xc.shape[0]
    t = LRU_T
    n = s // t
    t8 = t // 8

    def body(pre_ref, xc_ref, g_ref, par_ref, h_ref, hh_ref, do_ref, dpre_ref, dxc_ref, dg_ref, dpar_ref, carry):
        c = pl.program_id(0)

        @pl.when(c == 0)
        def _():
            carry[...] = jnp.zeros_like(carry)
            dpar_ref[...] = jnp.zeros_like(dpar_ref)

        pa, px, xcb = pre_ref[:, :LRU_W], pre_ref[:, LRU_W:], xc_ref[...]
        ba, bx, lam = par_ref[0:1, :], par_ref[1:2, :], par_ref[2:3, :]
        (a, u), vjp = jax.vjp(_lru_au, pa, px, xcb, ba, bx, lam)
        g = g_ref[...]
        hcur = h_ref[...]
        do = do_ref[...]
        _, gvjp = jax.vjp(_gelu, g)
        dg_ref[...] = gvjp(do * hcur)[0]
        row = lax.broadcasted_iota(jnp.int32, (t, LRU_W), 0)
        v = do * _gelu(g) + jnp.where(row == t - 1, carry[0:1, :], 0.0)
        b = jnp.where(row == t - 1, 0.0, pltpu.roll(a, t - 1, 0))
        sft = 1
        while sft < t:
            keep = row < t - sft
            b_s = jnp.where(keep, pltpu.roll(b, t - sft, 0), 1.0)
            v_s = jnp.where(keep, pltpu.roll(v, t - sft, 0), 0.0)
            v = b * v_s + v
            b = b * b_s
            sft *= 2
        dh = v
        carry[0:1, :] = a[0:1, :] * dh[0:1, :]
        hhalo = jnp.where(c == n - 1, 0.0, hh_ref[...])
        hprev = _shift_down(hcur, hhalo, 1)
        dpa, dpx, dxc, dba, dbx, dlam = vjp((dh * hprev, dh))
        dpre_ref[:, :LRU_W] = dpa
        dpre_ref[:, LRU_W:] = dpx
        dxc_ref[...] = dxc
        dpar_ref[0:1, :] += dba
        dpar_ref[1:2, :] += dbx
        dpar_ref[2:3, :] += dlam

    rev = lambda c: (n - 1 - c, 0)
    return pl.pallas_call(
        body, name=name, grid=(n,),
        in_specs=[pl.BlockSpec((t, 2 * LRU_W), rev), pl.BlockSpec((t, LRU_W), rev),
                  pl.BlockSpec((t, LRU_W), lambda c: (n - 1 - c, C_G // LRU_W)), pl.BlockSpec((8, LRU_W), lambda c: (0, 0)),
                  pl.BlockSpec((t, LRU_W), rev),
                  pl.BlockSpec((8, LRU_W), lambda c: (jnp.maximum((n - 1 - c) * t8 - 1, 0), 0)),
                  pl.BlockSpec((t, LRU_W), rev)],
        out_specs=[pl.BlockSpec((t, 2 * LRU_W), rev), pl.BlockSpec((t, LRU_W), rev), pl.BlockSpec((t, LRU_W), rev),
                   pl.BlockSpec((8, LRU_W), lambda c: (0, 0))],
        out_shape=[jax.ShapeDtypeStruct((s, 2 * LRU_W), F32), jax.ShapeDtypeStruct((s, LRU_W), F32),
                   jax.ShapeDtypeStruct((s, LRU_W), F32), jax.ShapeDtypeStruct((8, LRU_W), F32)],
        scratch_shapes=[pltpu.VMEM((8, LRU_W), F32)],
        compiler_params=_cp("arbitrary"),
    )(pre, xc, proj, par, h, h, dout)


def _swiglu_act(gu, name):
    def fn(rv, hv, cv):
        return [_silu(rv[0]) * rv[1]], []
    return _rows(fn, [(gu, D_FF, 0), (gu, D_FF, 1)], [], [(D_FF, MXU)], tile=256, name=name)[0]


def _swiglu_bwd(gu, da, name):
    def fn(rv, hv, cv):
        gt, up, dab = rv
        sg = _sigmoid(gt)
        dgate = dab * up * (sg * (1.0 + gt * (1.0 - sg)))
        dup = dab * (gt * sg)
        return [jnp.concatenate([dgate, dup], axis=1)], []
    return _rows(fn, [(gu, D_FF, 0), (gu, D_FF, 1), da], [], [(2 * D_FF, MXU)], tile=256, name=name)[0]


def _loss_head(x, g, target, name):
    d = x.shape[1]

    def fn(rv, hv, cv):
        xb, tb = rv
        y, vjp = jax.vjp(_rms, xb, cv[0])
        err = y - tb
        dy = err * (1.0 / d)
        dx, _ = vjp(dy)
        rstd = lax.rsqrt(jnp.mean(xb * xb, axis=-1, keepdims=True) + NORM_EPS)
        e2 = err * err * (0.5 / d)
        e2 = functools.reduce(lambda a, b: a + b, [e2[:, k * BLK:(k + 1) * BLK] for k in range(d // BLK)])
        return [dx], [_colsum8(dy * xb * rstd), _colsum8(e2)]
    return _rows(fn, [x, target], [g.reshape(1, -1)], [(d, F32)], [(8, d), (8, BLK)], tile=512, name=name)


ANY = pl.BlockSpec(memory_space=pl.ANY)


def _coords():
    return lax.axis_index("x"), lax.axis_index("y"), lax.axis_index("c")


def _gather_weights(big, small):
    def body(big_ref, small_ref, obig_ref, osmall_ref, send_sems, recv_sems, local_sems):
        x, y, c = _coords()
        me = 2 * x + y
        chips = [(1 - x, y), (x, 1 - y), (1 - x, 1 - y)]
        mine_b = pltpu.make_async_copy(big_ref, obig_ref.at[me], local_sems.at[0])
        mine_s = pltpu.make_async_copy(small_ref, osmall_ref.at[me], local_sems.at[1])
        mine_b.start()
        mine_s.start()
        sends = []
        for j, (px, py) in enumerate(chips):
            sends.append(pltpu.make_async_remote_copy(big_ref, obig_ref.at[me], send_sems.at[2 * j], recv_sems.at[2 * j],
                                                      device_id=(px, py, c), device_id_type=MESH))
            sends.append(pltpu.make_async_remote_copy(small_ref, osmall_ref.at[me], send_sems.at[2 * j + 1], recv_sems.at[2 * j + 1],
                                                      device_id=(px, py, c), device_id_type=MESH))
        for cp in sends:
            cp.start()
        for j, (px, py) in enumerate(chips):
            src = 2 * px + py
            pltpu.make_async_remote_copy(big_ref, obig_ref.at[src], send_sems.at[2 * j], recv_sems.at[2 * j],
                                         device_id=(px, py, c), device_id_type=MESH).wait_recv()
            pltpu.make_async_remote_copy(small_ref, osmall_ref.at[src], send_sems.at[2 * j + 1], recv_sems.at[2 * j + 1],
                                         device_id=(px, py, c), device_id_type=MESH).wait_recv()
        for cp in sends:
            cp.wait_send()
        mine_b.wait()
        mine_s.wait()

    return pl.pallas_call(
        body, name="gather_weights", in_specs=[ANY, ANY], out_specs=[ANY, ANY],
        out_shape=[jax.ShapeDtypeStruct((4,) + big.shape, big.dtype), jax.ShapeDtypeStruct((4,) + small.shape, small.dtype)],
        scratch_shapes=[pltpu.SemaphoreType.DMA((6,)), pltpu.SemaphoreType.DMA((6,)), pltpu.SemaphoreType.DMA((2,))],
        compiler_params=pltpu.CompilerParams(has_side_effects=True),
    )(big, small)


def _exchange_grads(gbig):
    def body(g_ref, o_ref, send_sems, recv_sems):
        x, y, c = _coords()
        chips = [(1 - x, y), (x, 1 - y), (1 - x, 1 - y)]
        sends = [pltpu.make_async_remote_copy(g_ref.at[2 * px + py], o_ref.at[j], send_sems.at[j], recv_sems.at[j],
                                              device_id=(px, py, c), device_id_type=MESH)
                 for j, (px, py) in enumerate(chips)]
        for cp in sends:
            cp.start()
        for cp in sends:
            cp.wait_recv()
        for cp in sends:
            cp.wait_send()

    return pl.pallas_call(
        body, name="exchange_grads", in_specs=[ANY], out_specs=ANY,
        out_shape=jax.ShapeDtypeStruct((3,) + gbig.shape[1:], gbig.dtype),
        scratch_shapes=[pltpu.SemaphoreType.DMA((3,)), pltpu.SemaphoreType.DMA((3,))],
        compiler_params=pltpu.CompilerParams(has_side_effects=True),
    )(gbig)


def _swap_sibling(p):
    def body(p_ref, o_ref, send_sem, recv_sem):
        x, y, c = _coords()
        cp = pltpu.make_async_remote_copy(p_ref, o_ref, send_sem, recv_sem, device_id=(x, y, 1 - c), device_id_type=MESH)
        cp.start()
        cp.wait_recv()
        cp.wait_send()

    return pl.pallas_call(
        body, name="swap_sibling", in_specs=[ANY], out_specs=ANY,
        out_shape=jax.ShapeDtypeStruct(p.shape, p.dtype),
        scratch_shapes=[pltpu.SemaphoreType.DMA, pltpu.SemaphoreType.DMA],
        compiler_params=pltpu.CompilerParams(has_side_effects=True),
    )(p)


def _gather_small(gs):
    def body(g_ref, o_ref, send_sems, recv_sems, local_sem):
        x, y, c = _coords()
        me = 4 * x + 2 * y + c
        mine = pltpu.make_async_copy(g_ref, o_ref.at[me], local_sem)
        mine.start()
        sends = []
        for k in range(1, 8):
            px, py, pc = x ^ (k >> 2), y ^ ((k >> 1) & 1), c ^ (k & 1)
            sends.append((pltpu.make_async_remote_copy(g_ref, o_ref.at[me], send_sems.at[k - 1], recv_sems.at[k - 1],
                                                       device_id=(px, py, pc), device_id_type=MESH), 4 * px + 2 * py + pc, k))
        for cp, _, _ in sends:
            cp.start()
        for cp, src, k in sends:
            pltpu.make_async_remote_copy(g_ref, o_ref.at[src], send_sems.at[k - 1], recv_sems.at[k - 1],
                                         device_id=(x, y, c), device_id_type=MESH).wait_recv()
        for cp, _, _ in sends:
            cp.wait_send()
        mine.wait()

    return pl.pallas_call(
        body, name="gather_small", in_specs=[ANY], out_specs=ANY,
        out_shape=jax.ShapeDtypeStruct((8,) + gs.shape, gs.dtype),
        scratch_shapes=[pltpu.SemaphoreType.DMA((7,)), pltpu.SemaphoreType.DMA((7,)), pltpu.SemaphoreType.DMA],
        compiler_params=pltpu.CompilerParams(has_side_effects=True),
    )(gs)


def _sum_slots(own, others, name, tile):
    k, r, c = others.shape

    def body(*refs):
        if own is None:
            o_ref, out_ref = refs
            acc = o_ref[0]
            first = 1
        else:
            own_ref, o_ref, out_ref = refs
            acc = own_ref[...]
            first = 0
        for j in range(first, k):
            acc = acc + o_ref[j]
        out_ref[...] = acc

    row = pl.BlockSpec((tile, c), lambda i: (i, 0))
    specs = ([] if own is None else [row]) + [pl.BlockSpec((k, tile, c), lambda i: (0, i, 0))]
    args = ([] if own is None else [own]) + [others]
    return pl.pallas_call(body, name=name, grid=(r // tile,), in_specs=specs, out_specs=row,
                          out_shape=jax.ShapeDtypeStruct((r, c), F32), compiler_params=_cp("parallel"))(*args)


def _adamw(w, m, v, ga, gb, name, tile):
    r, c = w.shape

    def body(*refs):
        if gb is None:
            w_ref, m_ref, v_ref, ga_ref, g_ref, d_ref, nm_ref, nv_ref = refs
            g = ga_ref[...]
        else:
            w_ref, m_ref, v_ref, ga_ref, gb_ref, g_ref, d_ref, nm_ref, nv_ref = refs
            g = ga_ref[...] + gb_ref[...]
        nm = ADAM_B1 * m_ref[...] + (1.0 - ADAM_B1) * g
        nv = ADAM_B2 * v_ref[...] + (1.0 - ADAM_B2) * (g * g)
        g_ref[...] = g
        nm_ref[...] = nm
        nv_ref[...] = nv
        d_ref[...] = -ADAM_LR * ((nm / BC1) / (jnp.sqrt(nv / BC2) + ADAM_EPS) + ADAM_WD * w_ref[...])

    row = pl.BlockSpec((tile, c), lambda i: (i, 0))
    args = [w, m, v, ga] + ([] if gb is None else [gb])
    return pl.pallas_call(body, name=name, grid=(r // tile,), in_specs=[row] * len(args), out_specs=[row] * 4,
                          out_shape=[jax.ShapeDtypeStruct((r, c), F32)] * 4, compiler_params=_cp("parallel"))(*args)


BIG = ("w_in", "ssd_conv_w", "lru_conv_w", "w_out", "w_gate", "w_up", "w_down")
BIG_AXIS = {"w_in": 2, "ssd_conv_w": 2, "lru_conv_w": 2, "w_out": 1, "w_gate": 2, "w_up": 2, "w_down": 1}
CONVS = ("ssd_conv_w", "lru_conv_w")
SMALL = ("norm_mix", "ssd_conv_b", "ssd_dt_bias", "ssd_a_log", "ssd_d", "ssd_norm", "lru_conv_b", "lru_wa", "lru_ba",
         "lru_wx", "lru_bx", "lru_lambda", "norm_ffn", "norm_final")
WEIGHTS = ("norm_mix", "w_in", "ssd_conv_w", "ssd_conv_b", "ssd_dt_bias", "ssd_a_log", "ssd_d", "ssd_norm", "lru_conv_w",
           "lru_conv_b", "lru_wa", "lru_ba", "lru_wx", "lru_bx", "lru_lambda", "w_out", "norm_ffn", "w_gate", "w_up",
           "w_down", "norm_final")


def _pack(arrs, width, row_mult, dtype):
    flat = jnp.concatenate([a.reshape(-1).astype(dtype) for a in arrs])
    rows = -(-flat.shape[0] // width)
    rows = -(-rows // row_mult) * row_mult
    flat = jnp.pad(flat, (0, rows * width - flat.shape[0]))
    return flat.reshape(rows, width)


def _unpack(buf, shapes):
    flat = buf.reshape(-1)
    out, off = [], 0
    for shp in shapes:
        n = int(np.prod(shp))
        out.append(flat[off:off + n].reshape(shp))
        off += n
    return out


def _perm_cols(w):
    pad = jnp.zeros(w.shape[:-1] + (NP - IN_COLS,), w.dtype)
    return jnp.concatenate([w[..., :3072], w[..., 3080:4104], w[..., 3072:3080], pad], axis=-1)


def _unperm_cols(g):
    return jnp.concatenate([g[..., :3072], g[..., C_DT:C_DT + 8], g[..., 3072:4096]], axis=-1)


def _block_diag(w):
    eye = jnp.eye(LRU_BLOCKS, dtype=w.dtype)
    return jnp.einsum("ncd,nm->ncmd", w, eye).reshape(LRU_W, LRU_W)


def _block_diag_extract(g):
    g4 = g.reshape(LRU_BLOCKS, 64, LRU_BLOCKS, 64)
    return jnp.stack([g4[n, :, n, :] for n in range(LRU_BLOCKS)], axis=0)


def _lanes128(v):
    return jnp.pad(v, (0, BLK - v.shape[0])).reshape(1, BLK)


def _layer_fwd(x, p):
    h = _rms_fwd(x, p["norm_mix"], "rms_mix")
    proj = _mm(h, p["w_in"], tm=1024, tn=1408, tk=1024, name="mm_in")
    att, lse = _att_fwd_fused(proj, "att_fwd")
    xconv, dt = _ssd_pre(proj, p["ssd_conv_w"], p["ssd_conv_b"], _lanes128(p["ssd_dt_bias"]), "ssd_pre")
    spar = jnp.concatenate([_lanes128(p["ssd_a_log"]), _lanes128(p["ssd_d"]), jnp.zeros((6, BLK), F32)], axis=0)
    y, states = _ssd_scan(xconv, dt, spar, "ssd_scan")
    ssd = _ssd_post(y, proj, p["ssd_norm"], "ssd_post")
    xc = _lru_conv(proj, p["lru_conv_w"], p["lru_conv_b"], "lru_conv")
    wab = jnp.concatenate([_block_diag(p["lru_wa"]), _block_diag(p["lru_wx"])], axis=1).astype(MXU)
    pre = _mm(xc, wab, tm=1024, tn=1024, tk=512, name="mm_lru")
    lpar = jnp.concatenate([p["lru_ba"].reshape(1, -1), p["lru_bx"].reshape(1, -1), p["lru_lambda"].reshape(1, -1),
                            jnp.zeros((5, LRU_W), F32)], axis=0)
    lru, hs = _lru_scan(pre, xc, proj, lpar, "lru_scan")
    mix = jnp.concatenate([att, ssd, lru], axis=1).astype(MXU)
    x1 = _mm(mix, p["w_out"], add=x, tm=1024, tn=1024, tk=1536, name="mm_out")
    h2 = _rms_fwd(x1, p["norm_ffn"], "rms_ffn")
    gu = _mm(h2, p["w_gu"], tm=1024, tn=1408, tk=1024, name="mm_gu")
    act = _swiglu_act(gu, "swiglu_act")
    x2 = _mm(act, p["w_down"], add=x1, tm=1024, tn=1024, tk=2816, name="mm_down")
    saved = dict(x=x, h=h, proj=proj, att=att, lse=lse, xconv=xconv, dt=dt, spar=spar, y=y, states=states, xc=xc, wab=wab,
                 pre=pre, lpar=lpar, hs=hs, mix=mix, x1=x1, h2=h2, gu=gu, act=act)
    return x2, saved


def _layer_bwd(dx2, p, sv):
    g = {}
    da = _mm(dx2, p["w_down"], tb=True, tm=1024, tn=1408, tk=1024, name="mm_d_act")
    g["w_down"] = _mm(sv["act"], dx2, ta=True, tm=1408, tn=1024, tk=1024, name="mm_g_down")
    dgu = _swiglu_bwd(sv["gu"], da, "swiglu_bwd")
    dh2 = _mm(dgu, p["w_gu"], tb=True, tm=1024, tn=1024, tk=1408, name="mm_d_h2")
    g["w_gu"] = _mm(sv["h2"], dgu, ta=True, tm=1024, tn=1408, tk=1024, name="mm_g_gu")
    dx1, gn = _rms_bwd(sv["x1"], p["norm_ffn"], dh2, dx2, "rms_ffn_bwd")
    g["norm_ffn"] = jnp.sum(gn, axis=0)
    dmix = _mm(dx1, p["w_out"], tb=True, tm=1024, tn=1536, tk=1024, name="mm_d_mix")
    g["w_out"] = _mm(sv["mix"], dx1, ta=True, tm=1536, tn=1024, tk=1024, name="mm_g_out")
    datt, dssd, dlru = dmix[:, :ATT_W], dmix[:, ATT_W:ATT_W + SSD_W], dmix[:, ATT_W + SSD_W:]
    proj = sv["proj"]
    dpre, dxc_u, dgl, dlpar = _lru_scan_bwd(sv["pre"], sv["xc"], proj, sv["lpar"], sv["hs"], dlru, "lru_scan_bwd")
    dxc = _mm(dpre, sv["wab"], tb=True, add=dxc_u, tm=1024, tn=512, tk=1024, name="mm_d_xc")
    gwab = _mm(sv["xc"], dpre, ta=True, tm=512, tn=1024, tk=1024, name="mm_g_lru")
    g["lru_wa"], g["lru_wx"] = _block_diag_extract(gwab[:, :LRU_W]), _block_diag_extract(gwab[:, LRU_W:])
    g["lru_ba"], g["lru_bx"], g["lru_lambda"] = dlpar[0], dlpar[1], dlpar[2]
    dxl, gcw, gcb = _lru_conv_bwd(proj, dxc, p["lru_conv_w"], "lru_conv_bwd")
    g["lru_conv_w"], g["lru_conv_b"] = gcw[:CONV_K], jnp.sum(gcb, axis=0)
    dy, dz, gsn = _ssd_post_bwd(sv["y"], proj, p["ssd_norm"], dssd, "ssd_post_bwd")
    g["ssd_norm"] = jnp.sum(gsn, axis=0)
    dxconv, ddt, dal, ddk = _ssd_scan_bwd(sv["xconv"], sv["dt"], sv["spar"], sv["states"], dy, "ssd_scan_bwd")
    g["ssd_a_log"], g["ssd_d"] = dal[0, :8], ddk[0, :8]
    dxbc, ddtr, gsw, gsb, gdb = _ssd_pre_bwd(proj, dxconv, ddt, p["ssd_conv_w"], p["ssd_conv_b"],
                                             _lanes128(p["ssd_dt_bias"]), "ssd_pre_bwd")
    g["ssd_conv_w"], g["ssd_conv_b"], g["ssd_dt_bias"] = gsw[:CONV_K], jnp.sum(gsb, axis=0), jnp.sum(gdb, axis=0)[:8]
    delta = _att_delta(datt, sv["att"], "att_delta")
    dq, dk, dv = _att_bwd_fused(proj, datt, sv["lse"], delta, "att_bwd")
    dproj = jnp.concatenate([dq, dk, dv, dz, dxbc, dgl, dxl, ddtr], axis=1).astype(MXU)
    dh = _mm(dproj, p["w_in"], tb=True, tm=1024, tn=1024, tk=1408, name="mm_d_h")
    g["w_in"] = _mm(sv["h"], dproj, ta=True, tm=1024, tn=1408, tk=1024, name="mm_g_in")
    dx, gm = _rms_bwd(sv["x"], p["norm_mix"], dh, dx1, "rms_mix_bwd")
    g["norm_mix"] = jnp.sum(gm, axis=0)
    return dx, g


def kernel(x, norm_mix, w_in, ssd_conv_w, ssd_conv_b, ssd_dt_bias, ssd_a_log, ssd_d, ssd_norm, lru_conv_w, lru_conv_b, lru_wa, lru_ba, lru_wx, lru_bx, lru_lambda, w_out, norm_ffn, w_gate, w_up, w_down, norm_final, loss_target, m_norm_mix, m_w_in, m_ssd_conv_w, m_ssd_conv_b, m_ssd_dt_bias, m_ssd_a_log, m_ssd_d, m_ssd_norm, m_lru_conv_w, m_lru_conv_b, m_lru_wa, m_lru_ba, m_lru_wx, m_lru_bx, m_lru_lambda, m_w_out, m_norm_ffn, m_w_gate, m_w_up, m_w_down, m_norm_final, v_norm_mix, v_w_in, v_ssd_conv_w, v_ssd_conv_b, v_ssd_dt_bias, v_ssd_a_log, v_ssd_d, v_ssd_norm, v_lru_conv_w, v_lru_conv_b, v_lru_wa, v_lru_ba, v_lru_wx, v_lru_bx, v_lru_lambda, v_w_out, v_norm_ffn, v_w_gate, v_w_up, v_w_down, v_norm_final):
    loc = dict(locals())
    w = {n: loc[n] for n in WEIGHTS}
    m = {n: loc["m_" + n] for n in WEIGHTS}
    v = {n: loc["v_" + n] for n in WEIGHTS}

    mats = [n for n in BIG if n not in CONVS]
    shard_shapes = {n: w[n].shape for n in BIG}
    wbig = _pack([w[n] for n in mats], 1024, 512, MXU)
    wsmall = _pack([w[n] for n in CONVS], BLK, 8, F32)
    gbig, gsmall = _gather_weights(wbig, wsmall)
    full = {}
    per_chip_m = [_unpack(gbig[j], [shard_shapes[n] for n in mats]) for j in range(4)]
    per_chip_c = [_unpack(gsmall[j], [shard_shapes[n] for n in CONVS]) for j in range(4)]
    for i, n in enumerate(mats):
        full[n] = jnp.concatenate([per_chip_m[j][i] for j in range(4)], axis=BIG_AXIS[n])
    for i, n in enumerate(CONVS):
        full[n] = jnp.concatenate([per_chip_c[j][i] for j in range(4)], axis=BIG_AXIS[n])
    w_in_p = _perm_cols(full["w_in"])
    w_gu = jnp.concatenate([full["w_gate"], full["w_up"]], axis=-1)

    def layer_params(l):
        p = {n: w[n][l] for n in SMALL if n != "norm_final"}
        p.update(w_in=w_in_p[l], w_out=full["w_out"][l], w_gu=w_gu[l], w_down=full["w_down"][l],
                 ssd_conv_w=full["ssd_conv_w"][l], lru_conv_w=full["lru_conv_w"][l])
        return p

    xs = x[0]
    saved = []
    for l in range(DEPTH):
        xs, sv = _layer_fwd(xs, layer_params(l))
        saved.append(sv)
    dx, gnf, lsum = _loss_head(xs, norm_final, loss_target[0], "loss_head")
    loss = lax.psum(jnp.sum(lsum), ("x", "y", "c"))
    grads = [None] * DEPTH
    for l in reversed(range(DEPTH)):
        dx, grads[l] = _layer_bwd(dx, layer_params(l), saved[l])

    def stack(n, f=lambda t: t):
        return jnp.stack([f(grads[l][n]) for l in range(DEPTH)], axis=0)

    gfull = {
        "w_in": _unperm_cols(stack("w_in")),
        "w_out": stack("w_out"),
        "w_gate": stack("w_gu", lambda t: t[:, :D_FF]),
        "w_up": stack("w_gu", lambda t: t[:, D_FF:]),
        "w_down": stack("w_down"),
        "ssd_conv_w": stack("ssd_conv_w"),
        "lru_conv_w": stack("lru_conv_w"),
    }
    gsm = {n: stack(n) for n in SMALL if n != "norm_final"}
    gsm["norm_final"] = jnp.sum(gnf, axis=0)

    def shard_of(n, j):
        ax = BIG_AXIS[n]
        size = shard_shapes[n][ax]
        return lax.slice_in_dim(gfull[n], j * size, (j + 1) * size, axis=ax)

    gpack = jnp.stack([_pack([shard_of(n, j) for n in BIG], 1024, 512, F32) for j in range(4)], axis=0)
    me = 2 * lax.axis_index("x") + lax.axis_index("y")
    own = lax.dynamic_index_in_dim(gpack, me, axis=0, keepdims=False)
    others = _exchange_grads(gpack)
    part = _sum_slots(own, others, "sum_chips", 512)
    sib = _swap_sibling(part)
    wl = _pack([w[n] for n in BIG], 1024, 512, F32)
    ml = _pack([m[n] for n in BIG], 1024, 512, F32)
    vl = _pack([v[n] for n in BIG], 1024, 512, F32)
    gb, db, nmb, nvb = _adamw(wl, ml, vl, part, sib, "adamw_big", 512)
    big_shapes = [shard_shapes[n] for n in BIG]
    out_g = dict(zip(BIG, _unpack(gb, big_shapes)))
    out_d = dict(zip(BIG, _unpack(db, big_shapes)))
    out_m = dict(zip(BIG, _unpack(nmb, big_shapes)))
    out_v = dict(zip(BIG, _unpack(nvb, big_shapes)))

    small_shapes = [w[n].shape for n in SMALL]
    gs = _pack([gsm[n].reshape(w[n].shape) for n in SMALL], BLK, 8, F32)
    gall = _gather_small(gs)
    gsum = _sum_slots(None, gall, "sum_devices", gs.shape[0])
    ws = _pack([w[n] for n in SMALL], BLK, 8, F32)
    ms = _pack([m[n] for n in SMALL], BLK, 8, F32)
    vs = _pack([v[n] for n in SMALL], BLK, 8, F32)
    gsr, dsr, nms, nvs = _adamw(ws, ms, vs, gsum, None, "adamw_small", gs.shape[0])
    out_g.update(zip(SMALL, _unpack(gsr, small_shapes)))
    out_d.update(zip(SMALL, _unpack(dsr, small_shapes)))
    out_m.update(zip(SMALL, _unpack(nms, small_shapes)))
    out_v.update(zip(SMALL, _unpack(nvs, small_shapes)))

    return (loss, dx[None], *[out_g[n] for n in WEIGHTS], *[out_d[n] for n in WEIGHTS],
            *[out_m[n] for n in WEIGHTS], *[out_v[n] for n in WEIGHTS])
```

```python
import functools
import math

import jax
import jax.numpy as jnp
import numpy as np
from jax import lax
from jax.experimental import pallas as pl
from jax.experimental.pallas import tpu as pltpu

F32 = jnp.float32
MXU = jnp.bfloat16
HI = lax.Precision.HIGHEST
MESH = pl.DeviceIdType.MESH

D_MODEL = 1024
DEPTH = 2
HEAD_DIM = 64
ATT_W = 512
ATT_PATTERNS = ((128, 1), (512, 4), (2048, 16))
BLK = 128
SSD_W = 512
SSD_STATE = 128
LRU_W = 512
LRU_BLOCKS = 8
LRU_C = 8.0
CONV_K = 4
D_MIX = 1536
D_FF = 2816
IN_COLS = 4104
NP = 4224
NORM_EPS = 1e-6
SSD_NORM_EPS = 1e-5
LN2 = math.log(2.0)
NEG = -1e30

ADAM_LR, ADAM_B1, ADAM_B2, ADAM_EPS, ADAM_WD, ADAM_STEP = 0.001, 0.9, 0.999, 1e-08, 0.01, 10
BC1 = 1.0 - ADAM_B1 ** ADAM_STEP
BC2 = 1.0 - ADAM_B2 ** ADAM_STEP

VMEM_LIMIT = 56 * 1024 * 1024

C_Q, C_K, C_V, C_Z, C_XBC, C_G, C_XL, C_DT = 0, 512, 1024, 1536, 2048, 3072, 3584, 4096


def _cp(*sem):
    return pltpu.CompilerParams(dimension_semantics=sem, vmem_limit_bytes=VMEM_LIMIT)


def _dot(a, b, dims, prec=None):
    return lax.dot_general(a, b, (dims, ((), ())), preferred_element_type=F32, precision=prec)


def _nn(a, b, prec=None):
    return _dot(a, b, ((1,), (0,)), prec)


def _nt(a, b, prec=None):
    return _dot(a, b, ((1,), (1,)), prec)


def _tn(a, b, prec=None):
    return _dot(a, b, ((0,), (0,)), prec)


def _sigmoid(x):
    return jax.nn.sigmoid(x)


def _silu(x):
    return x * _sigmoid(x)


def _softplus(x):
    return jnp.maximum(x, 0.0) + jnp.log(1.0 + jnp.exp(-jnp.abs(x)))


def _gelu(x):
    return 0.5 * x * (1.0 + jnp.tanh(0.7978845608028654 * (x + 0.044715 * x * x * x)))


def _mm(a, b, *, ta=False, tb=False, add=None, out_dtype=F32, tm, tn, tk, name):
    m, k = (a.shape[1], a.shape[0]) if ta else a.shape
    n = b.shape[0] if tb else b.shape[1]
    assert (b.shape[1] if tb else b.shape[0]) == k
    assert m % tm == 0 and n % tn == 0 and k % tk == 0, (name, m, n, k)
    nk = k // tk
    a_spec = pl.BlockSpec((tk, tm), lambda i, j, kk: (kk, i)) if ta else pl.BlockSpec((tm, tk), lambda i, j, kk: (i, kk))
    b_spec = pl.BlockSpec((tn, tk), lambda i, j, kk: (j, kk)) if tb else pl.BlockSpec((tk, tn), lambda i, j, kk: (kk, j))
    o_spec = pl.BlockSpec((tm, tn), lambda i, j, kk: (i, j))
    dims = ((0 if ta else 1,), (1 if tb else 0,))

    def body(*refs):
        if add is None:
            a_ref, b_ref, o_ref, acc = refs
        else:
            a_ref, b_ref, add_ref, o_ref, acc = refs
        kk = pl.program_id(2)

        @pl.when(kk == 0)
        def _():
            acc[...] = jnp.zeros_like(acc)

        acc[...] += _dot(a_ref[...].astype(MXU), b_ref[...].astype(MXU), dims)

        @pl.when(kk == nk - 1)
        def _():
            r = acc[...]
            if add is not None:
                r = r + add_ref[...]
            o_ref[...] = r.astype(out_dtype)

    ins = [a, b] + ([] if add is None else [add])
    specs = [a_spec, b_spec] + ([] if add is None else [o_spec])
    return pl.pallas_call(
        body, name=name, grid=(m // tm, n // tn, nk), in_specs=specs, out_specs=o_spec,
        out_shape=jax.ShapeDtypeStruct((m, n), out_dtype),
        scratch_shapes=[pltpu.VMEM((tm, tn), F32)],
        compiler_params=_cp("parallel", "parallel", "arbitrary"),
    )(*ins)


def _rows(fn, rows, consts=(), outs=(), accs=(), *, tile, name, halos=()):
    rows = [r if isinstance(r, tuple) else (r, r.shape[1], 0) for r in rows]
    s = rows[0][0].shape[0]
    assert s % tile == 0 and tile % 8 == 0
    n = s // tile
    t8 = tile // 8
    nr, nh, nc_, no, na = len(rows), len(halos), len(consts), len(outs), len(accs)

    def body(*refs):
        i = pl.program_id(0)
        rv = [r[...] for r in refs[:nr]]
        hv = []
        for (idx, kind), r in zip(halos, refs[nr:nr + nh]):
            edge = (i == 0) if kind == "prev" else (i == n - 1)
            hv.append(jnp.where(edge, 0.0, r[...]))
        cv = [r[...] for r in refs[nr + nh:nr + nh + nc_]]
        o_refs = refs[nr + nh + nc_:nr + nh + nc_ + no]
        a_refs = refs[nr + nh + nc_ + no:]
        ov, av = fn(rv, hv, cv)
        for r, v in zip(o_refs, ov):
            r[...] = v.astype(r.dtype)
        if na:
            @pl.when(i == 0)
            def _():
                for r in a_refs:
                    r[...] = jnp.zeros_like(r)
            for r, v in zip(a_refs, av):
                r[...] += v

    in_specs = [pl.BlockSpec((tile, w), functools.partial(lambda i, cb: (i, cb), cb=cb)) for (_, w, cb) in rows]
    for idx, kind in halos:
        _, w, cb = rows[idx]
        if kind == "prev":
            in_specs.append(pl.BlockSpec((8, w), functools.partial(lambda i, cb: (jnp.maximum(i * t8 - 1, 0), cb), cb=cb)))
        else:
            in_specs.append(pl.BlockSpec((8, w), functools.partial(lambda i, cb: (jnp.minimum((i + 1) * t8, n * t8 - 1), cb), cb=cb)))
    in_specs += [pl.BlockSpec(c.shape, functools.partial(lambda i, nd: (0,) * nd, nd=c.ndim)) for c in consts]
    out_specs = [pl.BlockSpec((tile, c), lambda i: (i, 0)) for (c, _) in outs]
    out_specs += [pl.BlockSpec((r, c), lambda i: (0, 0)) for (r, c) in accs]
    out_shape = [jax.ShapeDtypeStruct((s, c), dt) for (c, dt) in outs]
    out_shape += [jax.ShapeDtypeStruct((r, c), F32) for (r, c) in accs]
    args = [r[0] for r in rows] + [rows[idx][0] for idx, _ in halos] + list(consts)
    res = pl.pallas_call(
        body, name=name, grid=(n,), in_specs=in_specs, out_specs=out_specs, out_shape=out_shape,
        compiler_params=_cp("arbitrary"),
    )(*args)
    return list(res)


def _colsum8(v):
    t, c = v.shape
    return jnp.sum(v.reshape(t // 8, 8, c), axis=0)


def _rms(x, g):
    return x * lax.rsqrt(jnp.mean(x * x, axis=-1, keepdims=True) + NORM_EPS) * g


def _rms_fwd(x, g, name):
    def fn(rv, hv, cv):
        return [_rms(rv[0], cv[0])], []
    return _rows(fn, [x], [g.reshape(1, -1)], [(x.shape[1], MXU)], tile=512, name=name)[0]


def _rms_bwd(x, g, dh, dres, name):
    def fn(rv, hv, cv):
        xb, dhb, drb = rv
        _, vjp = jax.vjp(_rms, xb, cv[0])
        dx, _ = vjp(dhb)
        rstd = lax.rsqrt(jnp.mean(xb * xb, axis=-1, keepdims=True) + NORM_EPS)
        return [drb + dx], [_colsum8(dhb * xb * rstd)]
    d = x.shape[1]
    return _rows(fn, [x, dh, dres], [g.reshape(1, -1)], [(d, F32)], [(8, d)], tile=512, name=name)


def _slope_dist(hp, hh, dist, dil):
    hf = (2 * hp + hh + 1).astype(F32)
    slope = jnp.exp(jnp.zeros(dist.shape, F32) - hf * LN2)
    return slope * (dist.astype(F32) * float(dil))


def _att_delta(datt, att, name):
    def fn(rv, hv, cv):
        r = lax.broadcasted_iota(jnp.int32, (ATT_W, ATT_W), 0) // HEAD_DIM
        c = lax.broadcasted_iota(jnp.int32, (ATT_W, ATT_W), 1) // HEAD_DIM
        ones = (r == c).astype(F32)
        return [_nn(rv[0] * rv[1], ones, HI)], []
    return _rows(fn, [datt, att], [], [(ATT_W, F32)], tile=512, name=name)[0]


ATT_G = 2048


def _deinterleave(dst, src, dil, ld, region, offset):
    for r in range(dil):
        rows = pl.ds(r, ld, stride=dil) if dil > 1 else pl.ds(0, ld)
        dst[r * region + offset:r * region + offset + ld, :] = src[rows, :]


def _deinterleave_edge(dst, src, dil, region, offset, first_row):
    for r in range(dil):
        rows = pl.ds(first_row + r, BLK, stride=dil) if dil > 1 else pl.ds(first_row, BLK)
        dst[r * region + offset:r * region + offset + BLK, :] = src[rows, :]


def _att_fwd_fused(proj, name, comm=None):
    s, npc = proj.shape
    gsz = ATT_G
    ng = s // gsz
    assert s % gsz == 0
    scale = HEAD_DIM ** -0.5
    comm = comm or _Comm()

    def body(*refs):
        (q_ref, kp_ref, kc_ref, vp_ref, vc_ref, att_ref, lse_ref, qd, kd, vd, nd, md, dd, nn, mn, dn), cm = comm.split(refs, 5, 2, 9)
        hp, g = pl.program_id(0), pl.program_id(1)
        comm.start_at((hp == 0) & (g == 0), cm)
        lane = lax.broadcasted_iota(jnp.int32, (BLK, BLK), 1)
        qi = lax.broadcasted_iota(jnp.int32, (BLK, 2 * BLK), 0)
        ki = lax.broadcasted_iota(jnp.int32, (BLK, 2 * BLK), 1)
        dist = BLK + qi - ki
        band = (dist >= 0) & (dist <= BLK)
        for pi, (_, dil) in enumerate(ATT_PATTERNS):
            ld = gsz // dil
            nbg = ld // BLK
            _deinterleave(qd, q_ref, dil, ld, ld, 0)
            _deinterleave(kd, kc_ref, dil, ld, ld + BLK, BLK)
            _deinterleave(vd, vc_ref, dil, ld, ld + BLK, BLK)
            _deinterleave_edge(kd, kp_ref, dil, ld + BLK, 0, gsz - BLK * dil)
            _deinterleave_edge(vd, vp_ref, dil, ld + BLK, 0, gsz - BLK * dil)
            bias = [_slope_dist(hp, hh, dist, dil) for hh in (0, 1)]

            def tile(t, carry, ld=ld, nbg=nbg, bias=bias):
                r, b = t // nbg, t % nbg
                qo = pl.multiple_of(r * ld + b * BLK, BLK)
                ko = pl.multiple_of(r * (ld + BLK) + b * BLK, BLK)
                q = qd[pl.ds(qo, BLK), :]
                kk = kd[pl.ds(ko, 2 * BLK), :].astype(MXU)
                vv = vd[pl.ds(ko, 2 * BLK), :].astype(MXU)
                valid = band & ((g > 0) | (b > 0) | (ki >= BLK))
                num = jnp.zeros((BLK, BLK), F32)
                mx = jnp.zeros((BLK, BLK), F32)
                den = jnp.zeros((BLK, BLK), F32)
                for hh in (0, 1):
                    hmask = (lane < HEAD_DIM) if hh == 0 else (lane >= HEAD_DIM)
                    qm = jnp.where(hmask, q, 0.0).astype(MXU)
                    sc = jnp.where(valid, _nt(qm, kk) * scale - bias[hh], NEG)
                    m = jnp.max(sc, axis=1, keepdims=True)
                    p = jnp.exp(sc - m)
                    dn_ = jnp.sum(p, axis=1, keepdims=True)
                    o = _nn(p.astype(MXU), vv)
                    num = jnp.where(hmask, o, num)
                    mx = jnp.where(hmask, m, mx)
                    den = jnp.where(hmask, dn_, den)
                nd[pl.ds(qo, BLK), :] = num
                md[pl.ds(qo, BLK), :] = mx
                dd[pl.ds(qo, BLK), :] = den
                return carry

            lax.fori_loop(0, dil * nbg, tile, 0)
            for r in range(dil):
                rows = pl.ds(r, ld, stride=dil) if dil > 1 else pl.ds(0, ld)
                nn.at[pi][rows, :] = nd[r * ld:(r + 1) * ld, :]
                mn.at[pi][rows, :] = md[r * ld:(r + 1) * ld, :]
                dn.at[pi][rows, :] = dd[r * ld:(r + 1) * ld, :]

        def merge(c, carry):
            rows = pl.ds(pl.multiple_of(c * 256, 256), 256)
            ms = [mn[pi, rows, :] for pi in range(len(ATT_PATTERNS))]
            m_all = functools.reduce(jnp.maximum, ms)
            num = jnp.zeros((256, BLK), F32)
            den = jnp.zeros((256, BLK), F32)
            for pi in range(len(ATT_PATTERNS)):
                e = jnp.exp(ms[pi] - m_all)
                num = num + nn[pi, rows, :] * e
                den = den + dn[pi, rows, :] * e
            att_ref[rows, :] = num / den
            lse_ref[rows, :] = m_all + jnp.log(den)
            return carry

        lax.fori_loop(0, gsz // 256, merge, 0)
        comm.wait_at((hp == 3) & (g == ng - 1), cm)

    def cur(base):
        return pl.BlockSpec((gsz, BLK), lambda hp, g: (g, base // BLK + hp))

    def prev(base):
        return pl.BlockSpec((gsz, BLK), lambda hp, g: (jnp.maximum(g - 1, 0), base // BLK + hp))

    o_spec = pl.BlockSpec((gsz, BLK), lambda hp, g: (g, hp))
    npat = len(ATT_PATTERNS)
    res = pl.pallas_call(
        body, name=name, grid=(4, ng),
        in_specs=[cur(C_Q), prev(C_K), cur(C_K), prev(C_V), cur(C_V)] + [ANY] * comm.n,
        out_specs=[o_spec] * 2 + [ANY] * comm.n,
        out_shape=[jax.ShapeDtypeStruct((s, ATT_W), F32)] * 2 + comm.out_shape(),
        scratch_shapes=[pltpu.VMEM((gsz, BLK), F32), pltpu.VMEM((2 * gsz, BLK), F32), pltpu.VMEM((2 * gsz, BLK), F32)]
        + [pltpu.VMEM((gsz, BLK), F32)] * 3 + [pltpu.VMEM((npat, gsz, BLK), F32)] * 3 + comm.scratch(),
        compiler_params=_cp("arbitrary", "arbitrary"),
    )(proj, proj, proj, proj, proj, *comm.args())
    return res[0], res[1], list(res[2:])


def _att_bwd_fused(proj, datt, lse, delta, name, comm=None):
    s, npc = proj.shape
    gsz = ATT_G
    ng = s // gsz
    scale = HEAD_DIM ** -0.5
    comm = comm or _Comm()

    def body(*refs):
        (qc_ref, qn_ref, kp_ref, kc_ref, vp_ref, vc_ref, doc_ref, don_ref, lsc_ref, lsn_ref, dlc_ref, dln_ref,
         dq_ref, dk_ref, dv_ref, qd, dod, lsd, dld, kd, vd, dqd, dkd, dvd), cm = comm.split(refs, 12, 3, 9)
        hp, g = pl.program_id(0), pl.program_id(1)
        comm.start_at((hp == 0) & (g == 0), cm)
        lane = lax.broadcasted_iota(jnp.int32, (BLK, BLK), 1)
        qi = lax.broadcasted_iota(jnp.int32, (BLK, BLK), 0)
        ki = lax.broadcasted_iota(jnp.int32, (BLK, BLK), 1)
        d_far = BLK + qi - ki
        d_near = qi - ki
        for pi, (_, dil) in enumerate(ATT_PATTERNS):
            ld = gsz // dil
            nbg = ld // BLK
            reg = ld + BLK
            for dst, c_ref, n_ref in ((qd, qc_ref, qn_ref), (dod, doc_ref, don_ref), (lsd, lsc_ref, lsn_ref), (dld, dlc_ref, dln_ref)):
                _deinterleave(dst, c_ref, dil, ld, reg, 0)
                _deinterleave_edge(dst, n_ref, dil, reg, ld, 0)
            for dst, p_ref, c_ref in ((kd, kp_ref, kc_ref), (vd, vp_ref, vc_ref)):
                _deinterleave(dst, c_ref, dil, ld, reg, BLK)
                _deinterleave_edge(dst, p_ref, dil, reg, 0, gsz - BLK * dil)
            b_far = [_slope_dist(hp, hh, d_far, dil) for hh in (0, 1)]
            b_near = [_slope_dist(hp, hh, d_near, dil) for hh in (0, 1)]

            def tile(t, carry, ld=ld, nbg=nbg, reg=reg, b_far=b_far, b_near=b_near):
                r, b = t // nbg, t % nbg
                oo = pl.multiple_of(r * ld + b * BLK, BLK)
                ro = pl.multiple_of(r * reg + b * BLK, BLK)
                qn, qx = qd[pl.ds(ro, BLK), :], qd[pl.ds(ro + BLK, BLK), :]
                don, dox = dod[pl.ds(ro, BLK), :], dod[pl.ds(ro + BLK, BLK), :]
                lsn, lsx = lsd[pl.ds(ro, BLK), :], lsd[pl.ds(ro + BLK, BLK), :]
                dln, dlx = dld[pl.ds(ro, BLK), :], dld[pl.ds(ro + BLK, BLK), :]
                kp, kc = kd[pl.ds(ro, BLK), :].astype(MXU), kd[pl.ds(ro + BLK, BLK), :].astype(MXU)
                vp, vc = vd[pl.ds(ro, BLK), :].astype(MXU), vd[pl.ds(ro + BLK, BLK), :].astype(MXU)
                ok_a = (d_far <= BLK) & ((g > 0) | (b > 0))
                ok_b = d_near >= 0
                ok_c = (d_far <= BLK) & ((g < ng - 1) | (b < nbg - 1))

                def grads(qm, dom, k, v, ls, dl, bias, valid, hh):
                    c0 = hh * HEAD_DIM
                    sc = _nt(qm, k) * scale - bias
                    p = jnp.exp(jnp.where(valid, sc - ls[:, c0:c0 + 1], NEG))
                    ds = p * (_nt(dom, v) - dl[:, c0:c0 + 1])
                    return p.astype(MXU), ds.astype(MXU)

                dq = jnp.zeros((BLK, BLK), F32)
                dk = jnp.zeros((BLK, BLK), F32)
                dv = jnp.zeros((BLK, BLK), F32)
                for hh in (0, 1):
                    hmask = (lane < HEAD_DIM) if hh == 0 else (lane >= HEAD_DIM)
                    qnm = jnp.where(hmask, qn, 0.0).astype(MXU)
                    qxm = jnp.where(hmask, qx, 0.0).astype(MXU)
                    donm = jnp.where(hmask, don, 0.0).astype(MXU)
                    doxm = jnp.where(hmask, dox, 0.0).astype(MXU)
                    _, ds_a = grads(qnm, donm, kp, vp, lsn, dln, b_far[hh], ok_a, hh)
                    p_b, ds_b = grads(qnm, donm, kc, vc, lsn, dln, b_near[hh], ok_b, hh)
                    p_c, ds_c = grads(qxm, doxm, kc, vc, lsx, dlx, b_far[hh], ok_c, hh)
                    dq = jnp.where(hmask, _nn(ds_a, kp) + _nn(ds_b, kc), dq)
                    dk = dk + _tn(ds_b, qnm) + _tn(ds_c, qxm)
                    dv = dv + _tn(p_b, donm) + _tn(p_c, doxm)
                dqd[pl.ds(oo, BLK), :] = dq * scale
                dkd[pl.ds(oo, BLK), :] = dk * scale
                dvd[pl.ds(oo, BLK), :] = dv
                return carry

            lax.fori_loop(0, dil * nbg, tile, 0)
            for out, src in ((dq_ref, dqd), (dk_ref, dkd), (dv_ref, dvd)):
                for r in range(dil):
                    rows = pl.ds(r, ld, stride=dil) if dil > 1 else pl.ds(0, ld)
                    if pi == 0:
                        out[rows, :] = src[r * ld:(r + 1) * ld, :]
                    else:
                        out[rows, :] = out[rows, :] + src[r * ld:(r + 1) * ld, :]
        comm.wait_at((hp == 3) & (g == ng - 1), cm)

    def pspec(base, shift):
        return pl.BlockSpec((gsz, BLK), lambda hp, g: (jnp.clip(g + shift, 0, ng - 1), base // BLK + hp))

    def wspec(shift):
        return pl.BlockSpec((gsz, BLK), lambda hp, g: (jnp.clip(g + shift, 0, ng - 1), hp))

    in_specs = [pspec(C_Q, 0), pspec(C_Q, 1), pspec(C_K, -1), pspec(C_K, 0), pspec(C_V, -1), pspec(C_V, 0),
                wspec(0), wspec(1), wspec(0), wspec(1), wspec(0), wspec(1)] + [ANY] * comm.n
    res = pl.pallas_call(
        body, name=name, grid=(4, ng), in_specs=in_specs,
        out_specs=[wspec(0)] * 3 + [ANY] * comm.n,
        out_shape=[jax.ShapeDtypeStruct((s, ATT_W), F32)] * 3 + comm.out_shape(),
        scratch_shapes=[pltpu.VMEM((2 * gsz, BLK), F32)] * 6 + [pltpu.VMEM((gsz, BLK), F32)] * 3 + comm.scratch(),
        compiler_params=_cp("arbitrary", "arbitrary"),
    )(proj, proj, proj, proj, proj, proj, datt, datt, lse, lse, delta, delta, *comm.args())
    return res[0], res[1], res[2], list(res[3:])


def _shift_down(cur, halo, sft):
    if sft == 0:
        return cur
    t = cur.shape[0]
    rolled = pltpu.roll(cur, sft, 0)
    hr = pltpu.roll(halo, sft, 0)
    row = lax.broadcasted_iota(jnp.int32, cur.shape, 0)
    return jnp.where(row < sft, jnp.tile(hr, (t // 8, 1)), rolled)


def _shift_up(cur, halo, sft):
    if sft == 0:
        return cur
    t = cur.shape[0]
    rolled = pltpu.roll(cur, t - sft, 0)
    hr = pltpu.roll(halo, 8 - sft, 0)
    row = lax.broadcasted_iota(jnp.int32, cur.shape, 0)
    return jnp.where(row >= t - sft, jnp.tile(hr, (t // 8, 1)), rolled)


def _conv(x, xh, w, b):
    y = b + x * w[CONV_K - 1:CONV_K]
    for k in range(CONV_K - 1):
        y = y + _shift_down(x, xh, CONV_K - 1 - k) * w[k:k + 1]
    return y


def _conv_bwd(x, xh, dy, dyh, w):
    dx = dy * w[CONV_K - 1:CONV_K]
    dws = []
    for k in range(CONV_K - 1):
        sft = CONV_K - 1 - k
        dx = dx + _shift_up(dy, dyh, sft) * w[k:k + 1]
        dws.append(jnp.sum(dy * _shift_down(x, xh, sft), axis=0, keepdims=True))
    dws.append(jnp.sum(dy * x, axis=0, keepdims=True))
    c = x.shape[1]
    dw = jnp.concatenate(dws + [jnp.zeros((8 - CONV_K, c), F32)], axis=0)
    return dx, dw, jnp.sum(dy, axis=0, keepdims=True)


def _pad8(w):
    return jnp.concatenate([w, jnp.zeros((8 - w.shape[0], w.shape[1]), w.dtype)], axis=0)


def _ssd_pre(proj, conv_w, conv_b, dt_bias128, name):
    def fn(rv, hv, cv):
        xbc, dtr = rv
        return [_silu(_conv(xbc, hv[0], cv[0], cv[1])), _softplus(dtr + cv[2])], []
    return _rows(fn, [(proj, 1024, C_XBC // 1024), (proj, BLK, C_DT // BLK)],
                 [_pad8(conv_w), conv_b.reshape(1, -1), dt_bias128],
                 [(1024, F32), (BLK, F32)], tile=256, name=name, halos=[(0, "prev")])


def _ssd_pre_bwd(proj, dxc, ddt, conv_w, conv_b, dt_bias128, name):
    def fn(rv, hv, cv):
        xbc, dtr, dxcb, ddtb = rv
        xh, dxch_raw, xnext = hv
        w, b, bias = cv
        pre = _conv(xbc, xh, w, b)
        sg = _sigmoid(pre)
        dpre = dxcb * (sg * (1.0 + pre * (1.0 - sg)))
        t = xbc.shape[0]
        tail = jnp.concatenate([xbc[t - 8:], xnext], axis=0)
        pre_n = _conv(tail[8:], tail[:8], w, b)
        sgn = _sigmoid(pre_n)
        dpre_h = dxch_raw * (sgn * (1.0 + pre_n * (1.0 - sgn)))
        dx, dw, db = _conv_bwd(xbc, xh, dpre, dpre_h, w)
        ddr = ddtb * _sigmoid(dtr + bias)
        return [dx, ddr], [dw, jnp.concatenate([db, jnp.zeros((7, db.shape[1]), F32)], axis=0), _colsum8(ddr)]
    return _rows(fn, [(proj, 1024, C_XBC // 1024), (proj, BLK, C_DT // BLK), dxc, ddt],
                 [_pad8(conv_w), conv_b.reshape(1, -1), dt_bias128],
                 [(1024, F32), (BLK, F32)], [(8, 1024), (8, 1024), (8, BLK)], tile=256, name=name,
                 halos=[(0, "prev"), (2, "next"), (0, "next")])


def _head_cols(v, h0):
    lane = lax.broadcasted_iota(jnp.int32, (v.shape[0], BLK), 1)
    return jnp.where(lane < HEAD_DIM, v[:, h0:h0 + 1], v[:, h0 + 1:h0 + 2])


def _ssd_scan(xc, dt, par, name):
    s = xc.shape[0]
    nc = s // BLK

    def body(x_ref, dt_ref, par_ref, y_ref, st_ref, h_ref):
        c = pl.program_id(0)

        @pl.when(c == 0)
        def _():
            h_ref[...] = jnp.zeros_like(h_ref)

        st_ref[0] = h_ref[...]
        dt = dt_ref[...]
        a_row = -jnp.exp(par_ref[0:1, :])
        d_row = par_ref[1:2, :]
        ri = lax.broadcasted_iota(jnp.int32, (BLK, BLK), 0)
        ci = lax.broadcasted_iota(jnp.int32, (BLK, BLK), 1)
        tril = ri >= ci
        cs = _nn(tril.astype(F32), dt * a_row, HI)
        cst, dtt = cs.T, dt.T
        last = cs[BLK - 1:BLK, :]
        wcol = jnp.exp(last - cs) * dt
        ecs = jnp.exp(cs)
        elast = jnp.exp(last)
        for g in (0, 1):
            bg = x_ref[:, 512 + g * BLK:512 + (g + 1) * BLK].astype(MXU)
            cg = x_ref[:, 768 + g * BLK:768 + (g + 1) * BLK].astype(MXU)
            gm = _nt(cg, bg)
            for pp in (0, 1):
                pr = 2 * g + pp
                h0 = 2 * pr
                x2 = x_ref[:, pr * BLK:(pr + 1) * BLK]
                hprev = h_ref[pr * BLK:(pr + 1) * BLK, :]
                yp = jnp.zeros((BLK, BLK), F32)
                for hh in (0, 1):
                    h = h0 + hh
                    hmask = (ci < HEAD_DIM) if hh == 0 else (ci >= HEAD_DIM)
                    lm = jnp.exp(jnp.where(tril, cs[:, h:h + 1] - cst[h:h + 1, :], NEG))
                    mm = gm * lm * dtt[h:h + 1, :]
                    yp = yp + _nn(mm.astype(MXU), jnp.where(hmask, x2, 0.0).astype(MXU))
                y0 = _nt(cg, hprev.astype(MXU))
                y_ref[:, pr * BLK:(pr + 1) * BLK] = yp + _head_cols(ecs, h0) * y0 + _head_cols(d_row, h0) * x2
                dec = jnp.where(ri < HEAD_DIM, elast[:, h0:h0 + 1], elast[:, h0 + 1:h0 + 2])
                xw = (x2 * _head_cols(wcol, h0)).astype(MXU)
                h_ref[pr * BLK:(pr + 1) * BLK, :] = dec * hprev + _tn(xw, bg)

    return pl.pallas_call(
        body, name=name, grid=(nc,),
        in_specs=[pl.BlockSpec((BLK, 1024), lambda c: (c, 0)), pl.BlockSpec((BLK, BLK), lambda c: (c, 0)),
                  pl.BlockSpec((8, BLK), lambda c: (0, 0))],
        out_specs=[pl.BlockSpec((BLK, SSD_W), lambda c: (c, 0)), pl.BlockSpec((1, SSD_W, SSD_STATE), lambda c: (c, 0, 0))],
        out_shape=[jax.ShapeDtypeStruct((s, SSD_W), F32), jax.ShapeDtypeStruct((nc, SSD_W, SSD_STATE), F32)],
        scratch_shapes=[pltpu.VMEM((SSD_W, SSD_STATE), F32)],
        compiler_params=_cp("arbitrary"),
    )(xc, dt, par)


def _ssd_scan_bwd(xc, dt, par, st, dy, name, comm=None):
    s = xc.shape[0]
    nc = s // BLK
    comm = comm or _Comm()

    def body(*refs):
        (x_ref, dt_ref, par_ref, st_ref, dy_ref, dx_ref, ddt_ref, dal_ref, dd_ref, dh_ref), cm = comm.split(refs, 5, 4, 1)
        c = pl.program_id(0)
        comm.start_at(c == 0, cm)

        @pl.when(c == 0)
        def _():
            dh_ref[...] = jnp.zeros_like(dh_ref)
            dal_ref[...] = jnp.zeros_like(dal_ref)
            dd_ref[...] = jnp.zeros_like(dd_ref)

        dt = dt_ref[...]
        a_row = -jnp.exp(par_ref[0:1, :])
        d_row = par_ref[1:2, :]
        ri = lax.broadcasted_iota(jnp.int32, (BLK, BLK), 0)
        ci = lax.broadcasted_iota(jnp.int32, (BLK, BLK), 1)
        tril = ri >= ci
        cs = _nn(tril.astype(F32), dt * a_row, HI)
        cst, dtt = cs.T, dt.T
        last = cs[BLK - 1:BLK, :]
        tolast = jnp.exp(last - cs)
        wcol = tolast * dt
        ecs = jnp.exp(cs)
        elast = jnp.exp(last)
        dcs_col = jnp.zeros((BLK, BLK), F32)
        ddt_col = jnp.zeros((BLK, BLK), F32)
        dcs_row = jnp.zeros((BLK, BLK), F32)
        ddt_row = jnp.zeros((BLK, BLK), F32)
        dlast = jnp.zeros((1, BLK), F32)
        ddsk = jnp.zeros((1, BLK), F32)
        for g in (0, 1):
            bg32 = x_ref[:, 512 + g * BLK:512 + (g + 1) * BLK]
            cg32 = x_ref[:, 768 + g * BLK:768 + (g + 1) * BLK]
            bg, cg = bg32.astype(MXU), cg32.astype(MXU)
            gm = _nt(cg, bg)
            dgm = jnp.zeros((BLK, BLK), F32)
            dbg = jnp.zeros((BLK, BLK), F32)
            dcg = jnp.zeros((BLK, BLK), F32)
            for pp in (0, 1):
                pr = 2 * g + pp
                h0 = 2 * pr
                x2 = x_ref[:, pr * BLK:(pr + 1) * BLK]
                dy2 = dy_ref[:, pr * BLK:(pr + 1) * BLK]
                hprev = st_ref[0, pr * BLK:(pr + 1) * BLK, :]
                dhn = dh_ref[pr * BLK:(pr + 1) * BLK, :]
                x2m, dhnm = x2.astype(MXU), dhn.astype(MXU)
                zb = _nt(bg, dhnm)
                y0 = _nt(cg, hprev.astype(MXU))
                esel = _head_cols(ecs, h0)
                wsel = _head_cols(wcol, h0)
                dx2 = _head_cols(d_row, h0) * dy2 + wsel * zb
                r_off = dy2 * y0
                r_w = x2 * zb
                r_d = dy2 * x2
                r_h = dhn * hprev
                for hh in (0, 1):
                    h = h0 + hh
                    hmask = (ci < HEAD_DIM) if hh == 0 else (ci >= HEAD_DIM)
                    onl = (ci == h).astype(F32)
                    ons = (ri == h).astype(F32)
                    dym = jnp.where(hmask, dy2, 0.0).astype(MXU)
                    dt_r = dtt[h:h + 1, :]
                    lm = jnp.exp(jnp.where(tril, cs[:, h:h + 1] - cst[h:h + 1, :], NEG))
                    mm = gm * lm * dt_r
                    dx2 = dx2 + _tn(mm.astype(MXU), dym)
                    dm = _nt(dym, x2m)
                    t1 = dm * lm
                    dgm = dgm + t1 * dt_r
                    tt = t1 * gm
                    ddt_row = ddt_row + ons * jnp.sum(tt, axis=0, keepdims=True)
                    t = tt * dt_r
                    dcs_col = dcs_col + onl * jnp.sum(t, axis=1, keepdims=True)
                    dcs_row = dcs_row - ons * jnp.sum(t, axis=0, keepdims=True)
                    de = jnp.sum(jnp.where(hmask, r_off, 0.0), axis=1, keepdims=True)
                    dcs_col = dcs_col + onl * (ecs[:, h:h + 1] * de)
                    hrow = (ri < HEAD_DIM) if hh == 0 else (ri >= HEAD_DIM)
                    dl_h = elast[:, h:h + 1] * jnp.sum(jnp.where(hrow, r_h, 0.0), keepdims=True)
                    dw = jnp.sum(jnp.where(hmask, r_w, 0.0), axis=1, keepdims=True)
                    ddt_col = ddt_col + onl * (dw * tolast[:, h:h + 1])
                    v = dw * wcol[:, h:h + 1]
                    dcs_col = dcs_col - onl * v
                    dl_h = dl_h + jnp.sum(v, keepdims=True)
                    dlast = dlast + onl[0:1, :] * dl_h
                    ddsk = ddsk + onl[0:1, :] * jnp.sum(jnp.where(hmask, r_d, 0.0), keepdims=True)
                dx_ref[:, pr * BLK:(pr + 1) * BLK] = dx2
                edy = (esel * dy2).astype(MXU)
                dcg = dcg + _nn(edy, hprev.astype(MXU))
                dec = jnp.where(ri < HEAD_DIM, elast[:, h0:h0 + 1], elast[:, h0 + 1:h0 + 2])
                dh_ref[pr * BLK:(pr + 1) * BLK, :] = dec * dhn + _tn(edy, cg)
                dbg = dbg + _nn((x2 * wsel).astype(MXU), dhnm)
            dgmm = dgm.astype(MXU)
            dx_ref[:, 512 + g * BLK:512 + (g + 1) * BLK] = dbg + _tn(dgmm, cg)
            dx_ref[:, 768 + g * BLK:768 + (g + 1) * BLK] = dcg + _nn(dgmm, bg)
        dcs = dcs_col + dcs_row.T + jnp.where(ri == BLK - 1, dlast, 0.0)
        dda = _nn((ri <= ci).astype(F32), dcs, HI)
        ddt_ref[...] = ddt_col + ddt_row.T + a_row * dda
        da = jnp.sum(dt * dda, axis=0, keepdims=True)
        dal_ref[0:1, :] += da * a_row
        dd_ref[0:1, :] += ddsk
        comm.wait_at(c == nc - 1, cm)

    rev = lambda c: (nc - 1 - c, 0)
    res = pl.pallas_call(
        body, name=name, grid=(nc,),
        in_specs=[pl.BlockSpec((BLK, 1024), rev), pl.BlockSpec((BLK, BLK), rev), pl.BlockSpec((8, BLK), lambda c: (0, 0)),
                  pl.BlockSpec((1, SSD_W, SSD_STATE), lambda c: (nc - 1 - c, 0, 0)), pl.BlockSpec((BLK, SSD_W), rev)]
        + [ANY] * comm.n,
        out_specs=[pl.BlockSpec((BLK, 1024), rev), pl.BlockSpec((BLK, BLK), rev),
                   pl.BlockSpec((8, BLK), lambda c: (0, 0)), pl.BlockSpec((8, BLK), lambda c: (0, 0))] + [ANY] * comm.n,
        out_shape=[jax.ShapeDtypeStruct((s, 1024), F32), jax.ShapeDtypeStruct((s, BLK), F32),
                   jax.ShapeDtypeStruct((8, BLK), F32), jax.ShapeDtypeStruct((8, BLK), F32)] + comm.out_shape(),
        scratch_shapes=[pltpu.VMEM((SSD_W, SSD_STATE), F32)] + comm.scratch(),
        compiler_params=_cp("arbitrary"),
    )(xc, dt, par, st, dy, *comm.args())
    return res[0], res[1], res[2], res[3], list(res[4:])


def _ssd_gate(y, z, w):
    t = y * _silu(z)
    outs = []
    for g in (0, 1):
        tg = t[:, g * 256:(g + 1) * 256]
        outs.append(tg * lax.rsqrt(jnp.mean(tg * tg, axis=-1, keepdims=True) + SSD_NORM_EPS))
    return jnp.concatenate(outs, axis=1) * w


def _ssd_post(y, proj, norm_w, name):
    def fn(rv, hv, cv):
        return [_ssd_gate(rv[0], rv[1], cv[0])], []
    return _rows(fn, [y, (proj, SSD_W, C_Z // SSD_W)], [norm_w.reshape(1, -1)], [(SSD_W, F32)], tile=512, name=name)[0]


def _ssd_post_bwd(y, proj, norm_w, dout, name):
    def fn(rv, hv, cv):
        yb, zb, db = rv
        _, vjp = jax.vjp(lambda a, b: _ssd_gate(a, b, cv[0]), yb, zb)
        dy, dz = vjp(db)
        t = yb * _silu(zb)
        nrm = []
        for g in (0, 1):
            tg = t[:, g * 256:(g + 1) * 256]
            nrm.append(tg * lax.rsqrt(jnp.mean(tg * tg, axis=-1, keepdims=True) + SSD_NORM_EPS))
        return [dy, dz], [_colsum8(db * jnp.concatenate(nrm, axis=1))]
    return _rows(fn, [y, (proj, SSD_W, C_Z // SSD_W), dout], [norm_w.reshape(1, -1)],
                 [(SSD_W, F32), (SSD_W, F32)], [(8, SSD_W)], tile=512, name=name)


LRU_T = 256


def _lru_conv(proj, conv_w, conv_b, name):
    def fn(rv, hv, cv):
        return [_conv(rv[0], hv[0], cv[0], cv[1])], []
    return _rows(fn, [(proj, LRU_W, C_XL // LRU_W)], [_pad8(conv_w), conv_b.reshape(1, -1)], [(LRU_W, F32)],
                 tile=512, name=name, halos=[(0, "prev")])[0]


def _lru_conv_bwd(proj, dxc, conv_w, name):
    def fn(rv, hv, cv):
        dx, dw, db = _conv_bwd(rv[0], hv[0], rv[1], hv[1], cv[0])
        return [dx], [dw, jnp.concatenate([db, jnp.zeros((7, db.shape[1]), F32)], axis=0)]
    return _rows(fn, [(proj, LRU_W, C_XL // LRU_W), dxc], [_pad8(conv_w)], [(LRU_W, F32)], [(8, LRU_W), (8, LRU_W)],
                 tile=512, name=name, halos=[(0, "prev"), (1, "next")])


def _lru_au(pre_a, pre_x, xc, ba, bx, lam):
    r = _sigmoid(pre_a + ba)
    i = _sigmoid(pre_x + bx)
    log_a = -LRU_C * r * _softplus(-lam)
    a = jnp.exp(log_a)
    u = jnp.sqrt(1.0 - jnp.exp(2.0 * log_a)) * (i * xc)
    return a, u


def _lru_scan(pre, xc, proj, par, name):
    s = xc.shape[0]
    t = LRU_T

    def body(pre_ref, xc_ref, g_ref, par_ref, out_ref, h_ref, carry):
        c = pl.program_id(0)

        @pl.when(c == 0)
        def _():
            carry[...] = jnp.zeros_like(carry)

        a, u = _lru_au(pre_ref[:, :LRU_W], pre_ref[:, LRU_W:], xc_ref[...], par_ref[0:1, :], par_ref[1:2, :], par_ref[2:3, :])
        row = lax.broadcasted_iota(jnp.int32, (t, LRU_W), 0)
        sft = 1
        while sft < t:
            keep = row >= sft
            a_s = jnp.where(keep, pltpu.roll(a, sft, 0), 1.0)
            u_s = jnp.where(keep, pltpu.roll(u, sft, 0), 0.0)
            u = a * u_s + u
            a = a * a_s
            sft *= 2
        h = a * carry[0:1, :] + u
        h_ref[...] = h
        out_ref[...] = h * _gelu(g_ref[...])
        carry[0:1, :] = h[t - 1:t, :]

    return pl.pallas_call(
        body, name=name, grid=(s // t,),
        in_specs=[pl.BlockSpec((t, 2 * LRU_W), lambda c: (c, 0)), pl.BlockSpec((t, LRU_W), lambda c: (c, 0)),
                  pl.BlockSpec((t, LRU_W), lambda c: (c, C_G // LRU_W)), pl.BlockSpec((8, LRU_W), lambda c: (0, 0))],
        out_specs=[pl.BlockSpec((t, LRU_W), lambda c: (c, 0))] * 2,
        out_shape=[jax.ShapeDtypeStruct((s, LRU_W), F32)] * 2,
        scratch_shapes=[pltpu.VMEM((8, LRU_W), F32)],
        compiler_params=_cp("arbitrary"),
    )(pre, xc, proj, par)


def _lru_scan_bwd(pre, xc, proj, par, h, dout, name):
    s = xc.shape[0]
    t = LRU_T
    n = s // t
    t8 = t // 8

    def body(pre_ref, xc_ref, g_ref, par_ref, h_ref, hh_ref, do_ref, dpre_ref, dxc_ref, dg_ref, dpar_ref, carry):
        c = pl.program_id(0)

        @pl.when(c == 0)
        def _():
            carry[...] = jnp.zeros_like(carry)
            dpar_ref[...] = jnp.zeros_like(dpar_ref)

        pa, px, xcb = pre_ref[:, :LRU_W], pre_ref[:, LRU_W:], xc_ref[...]
        ba, bx, lam = par_ref[0:1, :], par_ref[1:2, :], par_ref[2:3, :]
        (a, u), vjp = jax.vjp(_lru_au, pa, px, xcb, ba, bx, lam)
        g = g_ref[...]
        hcur = h_ref[...]
        do = do_ref[...]
        _, gvjp = jax.vjp(_gelu, g)
        dg_ref[...] = gvjp(do * hcur)[0]
        row = lax.broadcasted_iota(jnp.int32, (t, LRU_W), 0)
        v = do * _gelu(g) + jnp.where(row == t - 1, carry[0:1, :], 0.0)
        b = jnp.where(row == t - 1, 0.0, pltpu.roll(a, t - 1, 0))
        sft = 1
        while sft < t:
            keep = row < t - sft
            b_s = jnp.where(keep, pltpu.roll(b, t - sft, 0), 1.0)
            v_s = jnp.where(keep, pltpu.roll(v, t - sft, 0), 0.0)
            v = b * v_s + v
            b = b * b_s
            sft *= 2
        dh = v
        carry[0:1, :] = a[0:1, :] * dh[0:1, :]
        hhalo = jnp.where(c == n - 1, 0.0, hh_ref[...])
        hprev = _shift_down(hcur, hhalo, 1)
        dpa, dpx, dxc, dba, dbx, dlam = vjp((dh * hprev, dh))
        dpre_ref[:, :LRU_W] = dpa
        dpre_ref[:, LRU_W:] = dpx
        dxc_ref[...] = dxc
        dpar_ref[0:1, :] += dba
        dpar_ref[1:2, :] += dbx
        dpar_ref[2:3, :] += dlam

    rev = lambda c: (n - 1 - c, 0)
    return pl.pallas_call(
        body, name=name, grid=(n,),
        in_specs=[pl.BlockSpec((t, 2 * LRU_W), rev), pl.BlockSpec((t, LRU_W), rev),
                  pl.BlockSpec((t, LRU_W), lambda c: (n - 1 - c, C_G // LRU_W)), pl.BlockSpec((8, LRU_W), lambda c: (0, 0)),
                  pl.BlockSpec((t, LRU_W), rev),
                  pl.BlockSpec((8, LRU_W), lambda c: (jnp.maximum((n - 1 - c) * t8 - 1, 0), 0)),
                  pl.BlockSpec((t, LRU_W), rev)],
        out_specs=[pl.BlockSpec((t, 2 * LRU_W), rev), pl.BlockSpec((t, LRU_W), rev), pl.BlockSpec((t, LRU_W), rev),
                   pl.BlockSpec((8, LRU_W), lambda c: (0, 0))],
        out_shape=[jax.ShapeDtypeStruct((s, 2 * LRU_W), F32), jax.ShapeDtypeStruct((s, LRU_W), F32),
                   jax.ShapeDtypeStruct((s, LRU_W), F32), jax.ShapeDtypeStruct((8, LRU_W), F32)],
        scratch_shapes=[pltpu.VMEM((8, LRU_W), F32)],
        compiler_params=_cp("arbitrary"),
    )(pre, xc, proj, par, h, h, dout)


def _swiglu_act(gu, name):
    def fn(rv, hv, cv):
        return [_silu(rv[0]) * rv[1]], []
    return _rows(fn, [(gu, D_FF, 0), (gu, D_FF, 1)], [], [(D_FF, MXU)], tile=256, name=name)[0]


def _swiglu_bwd(gu, da, name):
    def fn(rv, hv, cv):
        gt, up, dab = rv
        sg = _sigmoid(gt)
        dgate = dab * up * (sg * (1.0 + gt * (1.0 - sg)))
        dup = dab * (gt * sg)
        return [jnp.concatenate([dgate, dup], axis=1)], []
    return _rows(fn, [(gu, D_FF, 0), (gu, D_FF, 1), da], [], [(2 * D_FF, MXU)], tile=256, name=name)[0]


def _loss_head(x, g, target, name):
    d = x.shape[1]

    def fn(rv, hv, cv):
        xb, tb = rv
        y, vjp = jax.vjp(_rms, xb, cv[0])
        err = y - tb
        dy = err * (1.0 / d)
        dx, _ = vjp(dy)
        rstd = lax.rsqrt(jnp.mean(xb * xb, axis=-1, keepdims=True) + NORM_EPS)
        e2 = err * err * (0.5 / d)
        e2 = functools.reduce(lambda a, b: a + b, [e2[:, k * BLK:(k + 1) * BLK] for k in range(d // BLK)])
        return [dx], [_colsum8(dy * xb * rstd), _colsum8(e2)]
    return _rows(fn, [x, target], [g.reshape(1, -1)], [(d, F32)], [(8, d), (8, BLK)], tile=512, name=name)


ANY = pl.BlockSpec(memory_space=pl.ANY)


def _coords():
    return lax.axis_index("x"), lax.axis_index("y"), lax.axis_index("c")


class _Comm:
    def __init__(self, gathers=(), scatters=()):
        self.gathers = list(gathers)
        self.scatters = list(scatters)
        self.n = len(self.gathers) + len(self.scatters)

    def args(self):
        return [a for a, _ in self.gathers] + self.scatters

    def out_shape(self):
        out = [jax.ShapeDtypeStruct((4,) + (a.shape if l is None else a.shape[1:]), a.dtype) for a, l in self.gathers]
        return out + [jax.ShapeDtypeStruct((3,) + a.shape[1:], a.dtype) for a in self.scatters]

    def scratch(self):
        if not self.n:
            return []
        return [pltpu.SemaphoreType.DMA((3 * self.n,)), pltpu.SemaphoreType.DMA((3 * self.n,)),
                pltpu.SemaphoreType.DMA((max(len(self.gathers), 1),))]

    def split(self, refs, n_in, n_out, n_scratch):
        refs = list(refs)
        n = self.n
        own = refs[:n_in] + refs[n_in + n:n_in + n + n_out] + refs[n_in + 2 * n + n_out:n_in + 2 * n + n_out + n_scratch]
        cm = (refs[n_in:n_in + n], refs[n_in + n + n_out:n_in + 2 * n + n_out], refs[n_in + 2 * n + n_out + n_scratch:])
        return own, cm

    def _copies(self, cm, arriving):
        ins, outs, (send, recv, local) = cm
        x, y, c = _coords()
        me = 2 * x + y
        chips = [(1 - x, y), (x, 1 - y), (1 - x, 1 - y)]
        remote, locals_ = [], []
        ng = len(self.gathers)
        for i in range(self.n):
            if i < ng:
                l = self.gathers[i][1]
                slab = ins[i] if l is None else ins[i].at[l]
                if not arriving:
                    locals_.append(pltpu.make_async_copy(slab, outs[i].at[me], local.at[i]))
            for j, (px, py) in enumerate(chips):
                if i < ng:
                    src, dst = slab, outs[i].at[2 * px + py if arriving else me]
                else:
                    src, dst = ins[i].at[2 * px + py], outs[i].at[j]
                remote.append(pltpu.make_async_remote_copy(src, dst, send.at[3 * i + j], recv.at[3 * i + j],
                                                           device_id=(px, py, c), device_id_type=MESH))
        return remote, locals_

    def start_at(self, cond, cm):
        def go():
            remote, locals_ = self._copies(cm, False)
            for cp in locals_ + remote:
                cp.start()

        if self.n:
            go() if cond is True else pl.when(cond)(go)

    def wait_at(self, cond, cm):
        def go():
            for cp in self._copies(cm, True)[0]:
                cp.wait_recv()
            remote, locals_ = self._copies(cm, False)
            for cp in remote:
                cp.wait_send()
            for cp in locals_:
                cp.wait()

        if self.n:
            go() if cond is True else pl.when(cond)(go)


def _comm_call(comm, name):
    def body(*refs):
        _, cm = comm.split(refs, 0, 0, 0)
        comm.start_at(True, cm)
        comm.wait_at(True, cm)

    return list(pl.pallas_call(
        body, name=name, in_specs=[ANY] * comm.n, out_specs=[ANY] * comm.n, out_shape=comm.out_shape(),
        scratch_shapes=comm.scratch(), compiler_params=pltpu.CompilerParams(has_side_effects=True),
    )(*comm.args()))


def _swap_sibling(arrs):
    n = len(arrs)

    def body(*refs):
        ins, outs, send, recv = refs[:n], refs[n:2 * n], refs[2 * n], refs[2 * n + 1]
        x, y, c = _coords()
        cps = [pltpu.make_async_remote_copy(ins[i], outs[i], send.at[i], recv.at[i], device_id=(x, y, 1 - c), device_id_type=MESH)
               for i in range(n)]
        for cp in cps:
            cp.start()
        for cp in cps:
            cp.wait_recv()
        for cp in cps:
            cp.wait_send()

    return list(pl.pallas_call(
        body, name="swap_sibling", in_specs=[ANY] * n, out_specs=[ANY] * n,
        out_shape=[jax.ShapeDtypeStruct(a.shape, a.dtype) for a in arrs],
        scratch_shapes=[pltpu.SemaphoreType.DMA((n,)), pltpu.SemaphoreType.DMA((n,))],
        compiler_params=pltpu.CompilerParams(has_side_effects=True),
    )(*arrs))


def _gather_small(gs):
    def body(g_ref, o_ref, send_sems, recv_sems, local_sem):
        x, y, c = _coords()
        me = 4 * x + 2 * y + c
        mine = pltpu.make_async_copy(g_ref, o_ref.at[me], local_sem)
        mine.start()
        sends = []
        for k in range(1, 8):
            px, py, pc = x ^ (k >> 2), y ^ ((k >> 1) & 1), c ^ (k & 1)
            sends.append((pltpu.make_async_remote_copy(g_ref, o_ref.at[me], send_sems.at[k - 1], recv_sems.at[k - 1],
                                                       device_id=(px, py, pc), device_id_type=MESH), 4 * px + 2 * py + pc, k))
        for cp, _, _ in sends:
            cp.start()
        for cp, src, k in sends:
            pltpu.make_async_remote_copy(g_ref, o_ref.at[src], send_sems.at[k - 1], recv_sems.at[k - 1],
                                         device_id=(x, y, c), device_id_type=MESH).wait_recv()
        for cp, _, _ in sends:
            cp.wait_send()
        mine.wait()

    return pl.pallas_call(
        body, name="gather_small", in_specs=[ANY], out_specs=ANY,
        out_shape=jax.ShapeDtypeStruct((8,) + gs.shape, gs.dtype),
        scratch_shapes=[pltpu.SemaphoreType.DMA((7,)), pltpu.SemaphoreType.DMA((7,)), pltpu.SemaphoreType.DMA],
        compiler_params=pltpu.CompilerParams(has_side_effects=True),
    )(gs)


def _sum_slots(own, others, name, tile):
    k, r, c = others.shape

    def body(*refs):
        if own is None:
            o_ref, out_ref = refs
            acc = o_ref[0]
            first = 1
        else:
            own_ref, o_ref, out_ref = refs
            acc = own_ref[...]
            first = 0
        for j in range(first, k):
            acc = acc + o_ref[j]
        out_ref[...] = acc

    row = pl.BlockSpec((tile, c), lambda i: (i, 0))
    specs = ([] if own is None else [row]) + [pl.BlockSpec((k, tile, c), lambda i: (0, i, 0))]
    args = ([] if own is None else [own]) + [others]
    return pl.pallas_call(body, name=name, grid=(r // tile,), in_specs=specs, out_specs=row,
                          out_shape=jax.ShapeDtypeStruct((r, c), F32), compiler_params=_cp("parallel"))(*args)


def _adamw(w, m, v, ga, gb, name, tile):
    r, c = w.shape

    def body(*refs):
        if gb is None:
            w_ref, m_ref, v_ref, ga_ref, g_ref, d_ref, nm_ref, nv_ref = refs
            g = ga_ref[...]
        else:
            w_ref, m_ref, v_ref, ga_ref, gb_ref, g_ref, d_ref, nm_ref, nv_ref = refs
            g = ga_ref[...] + gb_ref[...]
        nm = ADAM_B1 * m_ref[...] + (1.0 - ADAM_B1) * g
        nv = ADAM_B2 * v_ref[...] + (1.0 - ADAM_B2) * (g * g)
        g_ref[...] = g
        nm_ref[...] = nm
        nv_ref[...] = nv
        d_ref[...] = -ADAM_LR * ((nm / BC1) / (jnp.sqrt(nv / BC2) + ADAM_EPS) + ADAM_WD * w_ref[...])

    row = pl.BlockSpec((tile, c), lambda i: (i, 0))
    args = [w, m, v, ga] + ([] if gb is None else [gb])
    return pl.pallas_call(body, name=name, grid=(r // tile,), in_specs=[row] * len(args), out_specs=[row] * 4,
                          out_shape=[jax.ShapeDtypeStruct((r, c), F32)] * 4, compiler_params=_cp("parallel"))(*args)


MATS = ("w_in", "w_out", "w_gate", "w_up", "w_down")
CONVS = ("ssd_conv_w", "lru_conv_w")
BIG = MATS + CONVS
COL_SHARDED = ("w_in", "w_gate", "w_up", "ssd_conv_w", "lru_conv_w")
SMALL = ("norm_mix", "ssd_conv_b", "ssd_dt_bias", "ssd_a_log", "ssd_d", "ssd_norm", "lru_conv_b", "lru_wa", "lru_ba",
         "lru_wx", "lru_bx", "lru_lambda", "norm_ffn", "norm_final")
WEIGHTS = ("norm_mix", "w_in", "ssd_conv_w", "ssd_conv_b", "ssd_dt_bias", "ssd_a_log", "ssd_d", "ssd_norm", "lru_conv_w",
           "lru_conv_b", "lru_wa", "lru_ba", "lru_wx", "lru_bx", "lru_lambda", "w_out", "norm_ffn", "w_gate", "w_up",
           "w_down", "norm_final")
ROW_TILE = {"w_in": 256, "w_out": 128, "w_gate": 256, "w_up": 256, "w_down": 352}


def _pack(arrs, width, row_mult, dtype):
    flat = jnp.concatenate([a.reshape(-1).astype(dtype) for a in arrs])
    rows = -(-flat.shape[0] // width)
    rows = -(-rows // row_mult) * row_mult
    flat = jnp.pad(flat, (0, rows * width - flat.shape[0]))
    return flat.reshape(rows, width)


def _unpack(buf, shapes):
    flat = buf.reshape(-1)
    out, off = [], 0
    for shp in shapes:
        n = int(np.prod(shp))
        out.append(flat[off:off + n].reshape(shp))
        off += n
    return out


def _join(name, g4):
    if name in COL_SHARDED:
        return jnp.moveaxis(g4, 0, -2).reshape(g4.shape[1:-1] + (4 * g4.shape[-1],))
    return g4.reshape((4 * g4.shape[1],) + g4.shape[2:])


def _slabs(name, g):
    if name in COL_SHARDED:
        return jnp.moveaxis(g.reshape(g.shape[:-1] + (4, g.shape[-1] // 4)), -2, 0)
    return g.reshape((4, g.shape[0] // 4) + g.shape[1:])


def _perm_cols(w):
    pad = jnp.zeros(w.shape[:-1] + (NP - IN_COLS,), w.dtype)
    return jnp.concatenate([w[..., :3072], w[..., 3080:4104], w[..., 3072:3080], pad], axis=-1)


def _unperm_cols(g):
    return jnp.concatenate([g[..., :3072], g[..., C_DT:C_DT + 8], g[..., 3072:4096]], axis=-1)


def _block_diag(w):
    eye = jnp.eye(LRU_BLOCKS, dtype=w.dtype)
    return jnp.einsum("ncd,nm->ncmd", w, eye).reshape(LRU_W, LRU_W)


def _block_diag_extract(g):
    g4 = g.reshape(LRU_BLOCKS, 64, LRU_BLOCKS, 64)
    return jnp.stack([g4[n, :, n, :] for n in range(LRU_BLOCKS)], axis=0)


def _lanes128(v):
    return jnp.pad(v, (0, BLK - v.shape[0])).reshape(1, BLK)


def _layer_mixers(x, p, comm=None):
    h = _rms_fwd(x, p["norm_mix"], "rms_mix")
    proj = _mm(h, p["w_in"], tm=1024, tn=1408, tk=1024, name="mm_in")
    att, lse, got = _att_fwd_fused(proj, "att_fwd", comm)
    xconv, dt = _ssd_pre(proj, p["ssd_conv_w"], p["ssd_conv_b"], _lanes128(p["ssd_dt_bias"]), "ssd_pre")
    spar = jnp.concatenate([_lanes128(p["ssd_a_log"]), _lanes128(p["ssd_d"]), jnp.zeros((6, BLK), F32)], axis=0)
    y, states = _ssd_scan(xconv, dt, spar, "ssd_scan")
    ssd = _ssd_post(y, proj, p["ssd_norm"], "ssd_post")
    xc = _lru_conv(proj, p["lru_conv_w"], p["lru_conv_b"], "lru_conv")
    wab = jnp.concatenate([_block_diag(p["lru_wa"]), _block_diag(p["lru_wx"])], axis=1).astype(MXU)
    pre = _mm(xc, wab, tm=1024, tn=1024, tk=512, name="mm_lru")
    lpar = jnp.concatenate([p["lru_ba"].reshape(1, -1), p["lru_bx"].reshape(1, -1), p["lru_lambda"].reshape(1, -1),
                            jnp.zeros((5, LRU_W), F32)], axis=0)
    lru, hs = _lru_scan(pre, xc, proj, lpar, "lru_scan")
    mix = jnp.concatenate([att, ssd, lru], axis=1).astype(MXU)
    saved = dict(x=x, h=h, proj=proj, att=att, lse=lse, xconv=xconv, dt=dt, spar=spar, y=y, states=states, xc=xc, wab=wab,
                 pre=pre, lpar=lpar, hs=hs, mix=mix)
    return mix, saved, got


def _layer_ffn(x, mix, p, saved):
    x1 = _mm(mix, p["w_out"], add=x, tm=1024, tn=1024, tk=1536, name="mm_out")
    h2 = _rms_fwd(x1, p["norm_ffn"], "rms_ffn")
    gu = _mm(h2, p["w_gu"], tm=1024, tn=1408, tk=1024, name="mm_gu")
    act = _swiglu_act(gu, "swiglu_act")
    x2 = _mm(act, p["w_down"], add=x1, tm=1024, tn=1024, tk=2816, name="mm_down")
    saved.update(x1=x1, h2=h2, gu=gu, act=act)
    return x2


def _layer_bwd(dx2, p, sv, comm_ssd=None, comm_att=None):
    g = {}
    da = _mm(dx2, p["w_down"], tb=True, tm=1024, tn=1408, tk=1024, name="mm_d_act")
    g["w_down"] = _mm(sv["act"], dx2, ta=True, tm=1408, tn=1024, tk=1024, name="mm_g_down")
    dgu = _swiglu_bwd(sv["gu"], da, "swiglu_bwd")
    dh2 = _mm(dgu, p["w_gu"], tb=True, tm=1024, tn=1024, tk=1408, name="mm_d_h2")
    g["w_gu"] = _mm(sv["h2"], dgu, ta=True, tm=1024, tn=1408, tk=1024, name="mm_g_gu")
    dx1, gn = _rms_bwd(sv["x1"], p["norm_ffn"], dh2, dx2, "rms_ffn_bwd")
    g["norm_ffn"] = jnp.sum(gn, axis=0)
    dmix = _mm(dx1, p["w_out"], tb=True, tm=1024, tn=1536, tk=1024, name="mm_d_mix")
    g["w_out"] = _mm(sv["mix"], dx1, ta=True, tm=1536, tn=1024, tk=1024, name="mm_g_out")
    datt, dssd, dlru = dmix[:, :ATT_W], dmix[:, ATT_W:ATT_W + SSD_W], dmix[:, ATT_W + SSD_W:]
    proj = sv["proj"]
    dpre, dxc_u, dgl, dlpar = _lru_scan_bwd(sv["pre"], sv["xc"], proj, sv["lpar"], sv["hs"], dlru, "lru_scan_bwd")
    dxc = _mm(dpre, sv["wab"], tb=True, add=dxc_u, tm=1024, tn=512, tk=1024, name="mm_d_xc")
    gwab = _mm(sv["xc"], dpre, ta=True, tm=512, tn=1024, tk=1024, name="mm_g_lru")
    g["lru_wa"], g["lru_wx"] = _block_diag_extract(gwab[:, :LRU_W]), _block_diag_extract(gwab[:, LRU_W:])
    g["lru_ba"], g["lru_bx"], g["lru_lambda"] = dlpar[0], dlpar[1], dlpar[2]
    dxl, gcw, gcb = _lru_conv_bwd(proj, dxc, p["lru_conv_w"], "lru_conv_bwd")
    g["lru_conv_w"], g["lru_conv_b"] = gcw[:CONV_K], jnp.sum(gcb, axis=0)
    dy, dz, gsn = _ssd_post_bwd(sv["y"], proj, p["ssd_norm"], dssd, "ssd_post_bwd")
    g["ssd_norm"] = jnp.sum(gsn, axis=0)
    dxconv, ddt, dal, ddk, got_ssd = _ssd_scan_bwd(sv["xconv"], sv["dt"], sv["spar"], sv["states"], dy, "ssd_scan_bwd", comm_ssd)
    g["ssd_a_log"], g["ssd_d"] = dal[0, :8], ddk[0, :8]
    dxbc, ddtr, gsw, gsb, gdb = _ssd_pre_bwd(proj, dxconv, ddt, p["ssd_conv_w"], p["ssd_conv_b"],
                                             _lanes128(p["ssd_dt_bias"]), "ssd_pre_bwd")
    g["ssd_conv_w"], g["ssd_conv_b"], g["ssd_dt_bias"] = gsw[:CONV_K], jnp.sum(gsb, axis=0), jnp.sum(gdb, axis=0)[:8]
    delta = _att_delta(datt, sv["att"], "att_delta")
    dq, dk, dv, got_att = _att_bwd_fused(proj, datt, sv["lse"], delta, "att_bwd", None if comm_att is None else comm_att(g))
    dproj = jnp.concatenate([dq, dk, dv, dz, dxbc, dgl, dxl, ddtr], axis=1).astype(MXU)
    dh = _mm(dproj, p["w_in"], tb=True, tm=1024, tn=1024, tk=1408, name="mm_d_h")
    g["w_in"] = _mm(sv["h"], dproj, ta=True, tm=1024, tn=1408, tk=1024, name="mm_g_in")
    dx, gm = _rms_bwd(sv["x"], p["norm_mix"], dh, dx1, "rms_mix_bwd")
    g["norm_mix"] = jnp.sum(gm, axis=0)
    return dx, g, got_ssd, got_att


def _grad_slabs(g, names):
    out = {}
    for n in names:
        if n == "w_in":
            out[n] = _slabs(n, _unperm_cols(g["w_in"]))
        elif n == "w_gate":
            out[n] = _slabs(n, g["w_gu"][:, :D_FF])
        elif n == "w_up":
            out[n] = _slabs(n, g["w_gu"][:, D_FF:])
        else:
            out[n] = _slabs(n, g[n])
    return out


def kernel(x, norm_mix, w_in, ssd_conv_w, ssd_conv_b, ssd_dt_bias, ssd_a_log, ssd_d, ssd_norm, lru_conv_w, lru_conv_b, lru_wa, lru_ba, lru_wx, lru_bx, lru_lambda, w_out, norm_ffn, w_gate, w_up, w_down, norm_final, loss_target, m_norm_mix, m_w_in, m_ssd_conv_w, m_ssd_conv_b, m_ssd_dt_bias, m_ssd_a_log, m_ssd_d, m_ssd_norm, m_lru_conv_w, m_lru_conv_b, m_lru_wa, m_lru_ba, m_lru_wx, m_lru_bx, m_lru_lambda, m_w_out, m_norm_ffn, m_w_gate, m_w_up, m_w_down, m_norm_final, v_norm_mix, v_w_in, v_ssd_conv_w, v_ssd_conv_b, v_ssd_dt_bias, v_ssd_a_log, v_ssd_d, v_ssd_norm, v_lru_conv_w, v_lru_conv_b, v_lru_wa, v_lru_ba, v_lru_wx, v_lru_bx, v_lru_lambda, v_w_out, v_norm_ffn, v_w_gate, v_w_up, v_w_down, v_norm_final):
    loc = dict(locals())
    w = {n: loc[n] for n in WEIGHTS}
    m = {n: loc["m_" + n] for n in WEIGHTS}
    v = {n: loc["v_" + n] for n in WEIGHTS}

    wb = {n: w[n].astype(MXU) for n in MATS}
    first = _comm_call(_Comm(gathers=[(wb["w_in"], 0), (w["ssd_conv_w"], None), (w["lru_conv_w"], None)]), "gather_first")
    convs = {"ssd_conv_w": _join("ssd_conv_w", first[1]), "lru_conv_w": _join("lru_conv_w", first[2])}
    later = [(n, 0) for n in MATS[1:]] + [(n, 1) for n in MATS]
    rest = _Comm(gathers=[(wb[n], l) for n, l in later])
    whole = {("w_in", 0): _join("w_in", first[0])}
    params = {}

    def layer_params(l):
        if l not in params:
            p = {n: w[n][l] for n in SMALL if n != "norm_final"}
            p.update(w_in=_perm_cols(whole["w_in", l]), ssd_conv_w=convs["ssd_conv_w"][l], lru_conv_w=convs["lru_conv_w"][l])
            params[l] = p
        if "w_out" not in params[l] and ("w_out", l) in whole:
            params[l].update(w_out=whole["w_out", l], w_down=whole["w_down", l],
                             w_gu=jnp.concatenate([whole["w_gate", l], whole["w_up", l]], axis=-1))
        return params[l]

    xs = x[0]
    saved = []
    for l in range(DEPTH):
        mix, sv, got = _layer_mixers(xs, layer_params(l), rest if l == 0 else None)
        if l == 0:
            whole.update({k: _join(k[0], a) for k, a in zip(later, got)})
        xs = _layer_ffn(xs, mix, layer_params(l), sv)
        saved.append(sv)
    dx, gnf, lsum = _loss_head(xs, norm_final, loss_target[0], "loss_head")
    loss = lax.psum(jnp.sum(lsum), ("x", "y", "c"))

    dx, g1, _, _ = _layer_bwd(dx, layer_params(1), saved[1])
    s1 = _grad_slabs(g1, BIG)
    ssd_names = ("w_gate", "w_up")
    att1 = ("w_in", "w_out", "w_down") + CONVS
    att0 = ("w_gate", "w_up", "w_down", "w_out")
    s0 = {}

    def comm_att(g0):
        s0.update(_grad_slabs(g0, att0))
        return _Comm(scatters=[s1[n] for n in att1] + [s0[n] for n in att0])

    dx, g0, got_ssd, got_att = _layer_bwd(dx, layer_params(0), saved[0], _Comm(scatters=[s1[n] for n in ssd_names]), comm_att)
    tail0 = ("w_in",) + CONVS
    s0.update(_grad_slabs(g0, tail0))
    got_tail = _comm_call(_Comm(scatters=[s0[n] for n in tail0]), "exchange_tail")
    recv = {(n, 1): a for n, a in zip(ssd_names, got_ssd)}
    recv.update({(n, 1): a for n, a in zip(att1, got_att[:len(att1)])})
    recv.update({(n, 0): a for n, a in zip(att0, got_att[len(att1):])})
    recv.update({(n, 0): a for n, a in zip(tail0, got_tail)})

    me = 2 * lax.axis_index("x") + lax.axis_index("y")
    slabs = (s0, s1)
    part = {}
    for n in BIG:
        per_layer = []
        for l in range(DEPTH):
            own = lax.dynamic_index_in_dim(slabs[l][n], me, axis=0, keepdims=False)
            per_layer.append(_sum_slots(own, recv[n, l], "sum_chips_" + n, ROW_TILE.get(n, own.shape[0])))
        part[n] = jnp.concatenate(per_layer, axis=0)
    sib = dict(zip(BIG, _swap_sibling([part[n] for n in BIG])))
    out_g, out_d, out_m, out_v = {}, {}, {}, {}
    for n in BIG:
        shp = w[n].shape
        flat = (shp[0] * shp[1], shp[2])
        res = _adamw(w[n].reshape(flat), m[n].reshape(flat), v[n].reshape(flat), part[n], sib[n], "adamw_" + n,
                     ROW_TILE.get(n, flat[0]))
        out_g[n], out_d[n], out_m[n], out_v[n] = [r.reshape(shp) for r in res]

    gsm = {n: jnp.stack([g0[n], g1[n]], axis=0) for n in SMALL if n != "norm_final"}
    gsm["norm_final"] = jnp.sum(gnf, axis=0)
    small_shapes = [w[n].shape for n in SMALL]
    gs = _pack([gsm[n].reshape(w[n].shape) for n in SMALL], BLK, 8, F32)
    gall = _gather_small(gs)
    gsum = _sum_slots(None, gall, "sum_devices", gs.shape[0])
    ws = _pack([w[n] for n in SMALL], BLK, 8, F32)
    ms = _pack([m[n] for n in SMALL], BLK, 8, F32)
    vs = _pack([v[n] for n in SMALL], BLK, 8, F32)
    gsr, dsr, nms, nvs = _adamw(ws, ms, vs, gsum, None, "adamw_small", gs.shape[0])
    out_g.update(zip(SMALL, _unpack(gsr, small_shapes)))
    out_d.update(zip(SMALL, _unpack(dsr, small_shapes)))
    out_m.update(zip(SMALL, _unpack(nms, small_shapes)))
    out_v.update(zip(SMALL, _unpack(nvs, small_shapes)))

    return (loss, dx[None], *[out_g[n] for n in WEIGHTS], *[out_d[n] for n in WEIGHTS],
            *[out_m[n] for n in WEIGHTS], *[out_v[n] for n in WEIGHTS])
```

```python
import functools
import math

import jax
import jax.numpy as jnp
import numpy as np
from jax import lax
from jax.experimental import pallas as pl
from jax.experimental.pallas import tpu as pltpu

F32 = jnp.float32
MXU = jnp.bfloat16
HI = lax.Precision.HIGHEST
MESH = pl.DeviceIdType.MESH

D_MODEL = 1024
DEPTH = 2
HEAD_DIM = 64
ATT_W = 512
ATT_PATTERNS = ((128, 1), (512, 4), (2048, 16))
BLK = 128
SSD_W = 512
SSD_STATE = 128
LRU_W = 512
LRU_BLOCKS = 8
LRU_C = 8.0
CONV_K = 4
D_MIX = 1536
D_FF = 2816
IN_COLS = 4104
NP = 4224
NORM_EPS = 1e-6
SSD_NORM_EPS = 1e-5
LN2 = math.log(2.0)
NEG = -1e30

ADAM_LR, ADAM_B1, ADAM_B2, ADAM_EPS, ADAM_WD, ADAM_STEP = 0.001, 0.9, 0.999, 1e-08, 0.01, 10
BC1 = 1.0 - ADAM_B1 ** ADAM_STEP
BC2 = 1.0 - ADAM_B2 ** ADAM_STEP

VMEM_LIMIT = 56 * 1024 * 1024

C_Q, C_K, C_V, C_Z, C_XBC, C_G, C_XL, C_DT = 0, 512, 1024, 1536, 2048, 3072, 3584, 4096


def _cp(*sem):
    return pltpu.CompilerParams(dimension_semantics=sem, vmem_limit_bytes=VMEM_LIMIT)


def _dot(a, b, dims, prec=None):
    return lax.dot_general(a, b, (dims, ((), ())), preferred_element_type=F32, precision=prec)


def _nn(a, b, prec=None):
    return _dot(a, b, ((1,), (0,)), prec)


def _nt(a, b, prec=None):
    return _dot(a, b, ((1,), (1,)), prec)


def _tn(a, b, prec=None):
    return _dot(a, b, ((0,), (0,)), prec)


def _sigmoid(x):
    return jax.nn.sigmoid(x)


def _silu(x):
    return x * _sigmoid(x)


def _softplus(x):
    return jnp.maximum(x, 0.0) + jnp.log(1.0 + jnp.exp(-jnp.abs(x)))


def _gelu(x):
    return 0.5 * x * (1.0 + jnp.tanh(0.7978845608028654 * (x + 0.044715 * x * x * x)))


def _mm(a, b, *, ta=False, tb=False, add=None, out_dtype=F32, tm, tn, tk, name):
    m, k = (a.shape[1], a.shape[0]) if ta else a.shape
    n = b.shape[0] if tb else b.shape[1]
    assert (b.shape[1] if tb else b.shape[0]) == k
    assert m % tm == 0 and n % tn == 0 and k % tk == 0, (name, m, n, k)
    nk = k // tk
    a_spec = pl.BlockSpec((tk, tm), lambda i, j, kk: (kk, i)) if ta else pl.BlockSpec((tm, tk), lambda i, j, kk: (i, kk))
    b_spec = pl.BlockSpec((tn, tk), lambda i, j, kk: (j, kk)) if tb else pl.BlockSpec((tk, tn), lambda i, j, kk: (kk, j))
    o_spec = pl.BlockSpec((tm, tn), lambda i, j, kk: (i, j))
    dims = ((0 if ta else 1,), (1 if tb else 0,))

    def body(*refs):
        if add is None:
            a_ref, b_ref, o_ref, acc = refs
        else:
            a_ref, b_ref, add_ref, o_ref, acc = refs
        kk = pl.program_id(2)

        @pl.when(kk == 0)
        def _():
            acc[...] = jnp.zeros_like(acc)

        acc[...] += _dot(a_ref[...].astype(MXU), b_ref[...].astype(MXU), dims)

        @pl.when(kk == nk - 1)
        def _():
            r = acc[...]
            if add is not None:
                r = r + add_ref[...]
            o_ref[...] = r.astype(out_dtype)

    ins = [a, b] + ([] if add is None else [add])
    specs = [a_spec, b_spec] + ([] if add is None else [o_spec])
    return pl.pallas_call(
        body, name=name, grid=(m // tm, n // tn, nk), in_specs=specs, out_specs=o_spec,
        out_shape=jax.ShapeDtypeStruct((m, n), out_dtype),
        scratch_shapes=[pltpu.VMEM((tm, tn), F32)],
        compiler_params=_cp("parallel", "parallel", "arbitrary"),
    )(*ins)


def _rows(fn, rows, consts=(), outs=(), accs=(), *, tile, name, halos=()):
    rows = [r if isinstance(r, tuple) else (r, r.shape[1], 0) for r in rows]
    s = rows[0][0].shape[0]
    assert s % tile == 0 and tile % 8 == 0
    n = s // tile
    t8 = tile // 8
    nr, nh, nc_, no, na = len(rows), len(halos), len(consts), len(outs), len(accs)

    def body(*refs):
        i = pl.program_id(0)
        rv = [r[...] for r in refs[:nr]]
        hv = []
        for (idx, kind), r in zip(halos, refs[nr:nr + nh]):
            edge = (i == 0) if kind == "prev" else (i == n - 1)
            hv.append(jnp.where(edge, 0.0, r[...]))
        cv = [r[...] for r in refs[nr + nh:nr + nh + nc_]]
        o_refs = refs[nr + nh + nc_:nr + nh + nc_ + no]
        a_refs = refs[nr + nh + nc_ + no:]
        ov, av = fn(rv, hv, cv)
        for r, v in zip(o_refs, ov):
            r[...] = v.astype(r.dtype)
        if na:
            @pl.when(i == 0)
            def _():
                for r in a_refs:
                    r[...] = jnp.zeros_like(r)
            for r, v in zip(a_refs, av):
                r[...] += v

    in_specs = [pl.BlockSpec((tile, w), functools.partial(lambda i, cb: (i, cb), cb=cb)) for (_, w, cb) in rows]
    for idx, kind in halos:
        _, w, cb = rows[idx]
        if kind == "prev":
            in_specs.append(pl.BlockSpec((8, w), functools.partial(lambda i, cb: (jnp.maximum(i * t8 - 1, 0), cb), cb=cb)))
        else:
            in_specs.append(pl.BlockSpec((8, w), functools.partial(lambda i, cb: (jnp.minimum((i + 1) * t8, n * t8 - 1), cb), cb=cb)))
    in_specs += [pl.BlockSpec(c.shape, functools.partial(lambda i, nd: (0,) * nd, nd=c.ndim)) for c in consts]
    out_specs = [pl.BlockSpec((tile, c), lambda i: (i, 0)) for (c, _) in outs]
    out_specs += [pl.BlockSpec((r, c), lambda i: (0, 0)) for (r, c) in accs]
    out_shape = [jax.ShapeDtypeStruct((s, c), dt) for (c, dt) in outs]
    out_shape += [jax.ShapeDtypeStruct((r, c), F32) for (r, c) in accs]
    args = [r[0] for r in rows] + [rows[idx][0] for idx, _ in halos] + list(consts)
    res = pl.pallas_call(
        body, name=name, grid=(n,), in_specs=in_specs, out_specs=out_specs, out_shape=out_shape,
        compiler_params=_cp("arbitrary"),
    )(*args)
    return list(res)


def _colsum8(v):
    t, c = v.shape
    return jnp.sum(v.reshape(t // 8, 8, c), axis=0)


def _rms(x, g):
    return x * lax.rsqrt(jnp.mean(x * x, axis=-1, keepdims=True) + NORM_EPS) * g


def _rms_fwd(x, g, name):
    def fn(rv, hv, cv):
        return [_rms(rv[0], cv[0])], []
    return _rows(fn, [x], [g.reshape(1, -1)], [(x.shape[1], MXU)], tile=512, name=name)[0]


def _rms_bwd(x, g, dh, dres, name):
    def fn(rv, hv, cv):
        xb, dhb, drb = rv
        _, vjp = jax.vjp(_rms, xb, cv[0])
        dx, _ = vjp(dhb)
        rstd = lax.rsqrt(jnp.mean(xb * xb, axis=-1, keepdims=True) + NORM_EPS)
        return [drb + dx], [_colsum8(dhb * xb * rstd)]
    d = x.shape[1]
    return _rows(fn, [x, dh, dres], [g.reshape(1, -1)], [(d, F32)], [(8, d)], tile=512, name=name)


def _slope_dist(hp, hh, dist, dil):
    hf = (2 * hp + hh + 1).astype(F32)
    slope = jnp.exp(jnp.zeros(dist.shape, F32) - hf * LN2)
    return slope * (dist.astype(F32) * float(dil))


def _att_delta(datt, att, name):
    def fn(rv, hv, cv):
        r = lax.broadcasted_iota(jnp.int32, (ATT_W, ATT_W), 0) // HEAD_DIM
        c = lax.broadcasted_iota(jnp.int32, (ATT_W, ATT_W), 1) // HEAD_DIM
        ones = (r == c).astype(F32)
        return [_nn(rv[0] * rv[1], ones, HI)], []
    return _rows(fn, [datt, att], [], [(ATT_W, F32)], tile=512, name=name)[0]


ATT_G = 2048


def _deinterleave(dst, src, dil, ld, region, offset):
    for r in range(dil):
        rows = pl.ds(r, ld, stride=dil) if dil > 1 else pl.ds(0, ld)
        dst[r * region + offset:r * region + offset + ld, :] = src[rows, :]


def _deinterleave_edge(dst, src, dil, region, offset, first_row):
    for r in range(dil):
        rows = pl.ds(first_row + r, BLK, stride=dil) if dil > 1 else pl.ds(first_row, BLK)
        dst[r * region + offset:r * region + offset + BLK, :] = src[rows, :]


def _att_fwd_fused(proj, name, comm=None):
    s, npc = proj.shape
    gsz = ATT_G
    ng = s // gsz
    assert s % gsz == 0
    scale = HEAD_DIM ** -0.5
    comm = comm or _Comm()

    def body(*refs):
        (q_ref, kp_ref, kc_ref, vp_ref, vc_ref, att_ref, lse_ref, qd, kd, vd, nd, md, dd, nn, mn, dn), cm = comm.split(refs, 5, 2, 9)
        hp, g = pl.program_id(0), pl.program_id(1)
        comm.start_at((hp == 0) & (g == 0), cm)
        lane = lax.broadcasted_iota(jnp.int32, (BLK, BLK), 1)
        qi = lax.broadcasted_iota(jnp.int32, (BLK, 2 * BLK), 0)
        ki = lax.broadcasted_iota(jnp.int32, (BLK, 2 * BLK), 1)
        dist = BLK + qi - ki
        band = (dist >= 0) & (dist <= BLK)
        for pi, (_, dil) in enumerate(ATT_PATTERNS):
            ld = gsz // dil
            nbg = ld // BLK
            _deinterleave(qd, q_ref, dil, ld, ld, 0)
            _deinterleave(kd, kc_ref, dil, ld, ld + BLK, BLK)
            _deinterleave(vd, vc_ref, dil, ld, ld + BLK, BLK)
            _deinterleave_edge(kd, kp_ref, dil, ld + BLK, 0, gsz - BLK * dil)
            _deinterleave_edge(vd, vp_ref, dil, ld + BLK, 0, gsz - BLK * dil)
            bias = [_slope_dist(hp, hh, dist, dil) for hh in (0, 1)]

            def tile(t, carry, ld=ld, nbg=nbg, bias=bias):
                r, b = t // nbg, t % nbg
                qo = pl.multiple_of(r * ld + b * BLK, BLK)
                ko = pl.multiple_of(r * (ld + BLK) + b * BLK, BLK)
                q = qd[pl.ds(qo, BLK), :]
                kk = kd[pl.ds(ko, 2 * BLK), :].astype(MXU)
                vv = vd[pl.ds(ko, 2 * BLK), :].astype(MXU)
                valid = band & ((g > 0) | (b > 0) | (ki >= BLK))
                num = jnp.zeros((BLK, BLK), F32)
                mx = jnp.zeros((BLK, BLK), F32)
                den = jnp.zeros((BLK, BLK), F32)
                for hh in (0, 1):
                    hmask = (lane < HEAD_DIM) if hh == 0 else (lane >= HEAD_DIM)
                    qm = jnp.where(hmask, q, 0.0).astype(MXU)
                    sc = jnp.where(valid, _nt(qm, kk) * scale - bias[hh], NEG)
                    m = jnp.max(sc, axis=1, keepdims=True)
                    p = jnp.exp(sc - m)
                    dn_ = jnp.sum(p, axis=1, keepdims=True)
                    o = _nn(p.astype(MXU), vv)
                    num = jnp.where(hmask, o, num)
                    mx = jnp.where(hmask, m, mx)
                    den = jnp.where(hmask, dn_, den)
                nd[pl.ds(qo, BLK), :] = num
                md[pl.ds(qo, BLK), :] = mx
                dd[pl.ds(qo, BLK), :] = den
                return carry

            lax.fori_loop(0, dil * nbg, tile, 0, unroll=4)
            for r in range(dil):
                rows = pl.ds(r, ld, stride=dil) if dil > 1 else pl.ds(0, ld)
                nn.at[pi][rows, :] = nd[r * ld:(r + 1) * ld, :]
                mn.at[pi][rows, :] = md[r * ld:(r + 1) * ld, :]
                dn.at[pi][rows, :] = dd[r * ld:(r + 1) * ld, :]

        def merge(c, carry):
            rows = pl.ds(pl.multiple_of(c * 256, 256), 256)
            ms = [mn[pi, rows, :] for pi in range(len(ATT_PATTERNS))]
            m_all = functools.reduce(jnp.maximum, ms)
            num = jnp.zeros((256, BLK), F32)
            den = jnp.zeros((256, BLK), F32)
            for pi in range(len(ATT_PATTERNS)):
                e = jnp.exp(ms[pi] - m_all)
                num = num + nn[pi, rows, :] * e
                den = den + dn[pi, rows, :] * e
            att_ref[rows, :] = num / den
            lse_ref[rows, :] = m_all + jnp.log(den)
            return carry

        lax.fori_loop(0, gsz // 256, merge, 0)
        comm.wait_at((hp == 3) & (g == ng - 1), cm)

    def cur(base):
        return pl.BlockSpec((gsz, BLK), lambda hp, g: (g, base // BLK + hp))

    def prev(base):
        return pl.BlockSpec((gsz, BLK), lambda hp, g: (jnp.maximum(g - 1, 0), base // BLK + hp))

    o_spec = pl.BlockSpec((gsz, BLK), lambda hp, g: (g, hp))
    npat = len(ATT_PATTERNS)
    res = pl.pallas_call(
        body, name=name, grid=(4, ng),
        in_specs=[cur(C_Q), prev(C_K), cur(C_K), prev(C_V), cur(C_V)] + [ANY] * comm.n,
        out_specs=[o_spec] * 2 + [ANY] * comm.n,
        out_shape=[jax.ShapeDtypeStruct((s, ATT_W), F32)] * 2 + comm.out_shape(),
        scratch_shapes=[pltpu.VMEM((gsz, BLK), F32), pltpu.VMEM((2 * gsz, BLK), F32), pltpu.VMEM((2 * gsz, BLK), F32)]
        + [pltpu.VMEM((gsz, BLK), F32)] * 3 + [pltpu.VMEM((npat, gsz, BLK), F32)] * 3 + comm.scratch(),
        compiler_params=_cp("arbitrary", "arbitrary"),
    )(proj, proj, proj, proj, proj, *comm.args())
    return res[0], res[1], list(res[2:])


def _att_bwd_fused(proj, datt, lse, delta, name, comm=None):
    s, npc = proj.shape
    gsz = ATT_G
    ng = s // gsz
    scale = HEAD_DIM ** -0.5
    comm = comm or _Comm()

    def body(*refs):
        (qc_ref, qn_ref, kp_ref, kc_ref, vp_ref, vc_ref, doc_ref, don_ref, lsc_ref, lsn_ref, dlc_ref, dln_ref,
         dq_ref, dk_ref, dv_ref, qd, dod, lsd, dld, kd, vd, dqd, dkd, dvd), cm = comm.split(refs, 12, 3, 9)
        hp, g = pl.program_id(0), pl.program_id(1)
        comm.start_at((hp == 0) & (g == 0), cm)
        lane = lax.broadcasted_iota(jnp.int32, (BLK, BLK), 1)
        qi = lax.broadcasted_iota(jnp.int32, (BLK, BLK), 0)
        ki = lax.broadcasted_iota(jnp.int32, (BLK, BLK), 1)
        d_far = BLK + qi - ki
        d_near = qi - ki
        for pi, (_, dil) in enumerate(ATT_PATTERNS):
            ld = gsz // dil
            nbg = ld // BLK
            reg = ld + BLK
            for dst, c_ref, n_ref in ((qd, qc_ref, qn_ref), (dod, doc_ref, don_ref), (lsd, lsc_ref, lsn_ref), (dld, dlc_ref, dln_ref)):
                _deinterleave(dst, c_ref, dil, ld, reg, 0)
                _deinterleave_edge(dst, n_ref, dil, reg, ld, 0)
            for dst, p_ref, c_ref in ((kd, kp_ref, kc_ref), (vd, vp_ref, vc_ref)):
                _deinterleave(dst, c_ref, dil, ld, reg, BLK)
                _deinterleave_edge(dst, p_ref, dil, reg, 0, gsz - BLK * dil)
            b_far = [_slope_dist(hp, hh, d_far, dil) for hh in (0, 1)]
            b_near = [_slope_dist(hp, hh, d_near, dil) for hh in (0, 1)]

            def tile(t, carry, ld=ld, nbg=nbg, reg=reg, b_far=b_far, b_near=b_near):
                r, b = t // nbg, t % nbg
                oo = pl.multiple_of(r * ld + b * BLK, BLK)
                ro = pl.multiple_of(r * reg + b * BLK, BLK)
                qn, qx = qd[pl.ds(ro, BLK), :], qd[pl.ds(ro + BLK, BLK), :]
                don, dox = dod[pl.ds(ro, BLK), :], dod[pl.ds(ro + BLK, BLK), :]
                lsn, lsx = lsd[pl.ds(ro, BLK), :], lsd[pl.ds(ro + BLK, BLK), :]
                dln, dlx = dld[pl.ds(ro, BLK), :], dld[pl.ds(ro + BLK, BLK), :]
                kp, kc = kd[pl.ds(ro, BLK), :].astype(MXU), kd[pl.ds(ro + BLK, BLK), :].astype(MXU)
                vp, vc = vd[pl.ds(ro, BLK), :].astype(MXU), vd[pl.ds(ro + BLK, BLK), :].astype(MXU)
                ok_a = (d_far <= BLK) & ((g > 0) | (b > 0))
                ok_b = d_near >= 0
                ok_c = (d_far <= BLK) & ((g < ng - 1) | (b < nbg - 1))

                def grads(qm, dom, k, v, ls, dl, bias, valid, hh):
                    c0 = hh * HEAD_DIM
                    sc = _nt(qm, k) * scale - bias
                    p = jnp.exp(jnp.where(valid, sc - ls[:, c0:c0 + 1], NEG))
                    ds = p * (_nt(dom, v) - dl[:, c0:c0 + 1])
                    return p.astype(MXU), ds.astype(MXU)

                dq = jnp.zeros((BLK, BLK), F32)
                dk = jnp.zeros((BLK, BLK), F32)
                dv = jnp.zeros((BLK, BLK), F32)
                for hh in (0, 1):
                    hmask = (lane < HEAD_DIM) if hh == 0 else (lane >= HEAD_DIM)
                    qnm = jnp.where(hmask, qn, 0.0).astype(MXU)
                    qxm = jnp.where(hmask, qx, 0.0).astype(MXU)
                    donm = jnp.where(hmask, don, 0.0).astype(MXU)
                    doxm = jnp.where(hmask, dox, 0.0).astype(MXU)
                    _, ds_a = grads(qnm, donm, kp, vp, lsn, dln, b_far[hh], ok_a, hh)
                    p_b, ds_b = grads(qnm, donm, kc, vc, lsn, dln, b_near[hh], ok_b, hh)
                    p_c, ds_c = grads(qxm, doxm, kc, vc, lsx, dlx, b_far[hh], ok_c, hh)
                    dq = jnp.where(hmask, _nn(ds_a, kp) + _nn(ds_b, kc), dq)
                    dk = dk + _tn(ds_b, qnm) + _tn(ds_c, qxm)
                    dv = dv + _tn(p_b, donm) + _tn(p_c, doxm)
                dqd[pl.ds(oo, BLK), :] = dq * scale
                dkd[pl.ds(oo, BLK), :] = dk * scale
                dvd[pl.ds(oo, BLK), :] = dv
                return carry

            lax.fori_loop(0, dil * nbg, tile, 0, unroll=2)
            for out, src in ((dq_ref, dqd), (dk_ref, dkd), (dv_ref, dvd)):
                for r in range(dil):
                    rows = pl.ds(r, ld, stride=dil) if dil > 1 else pl.ds(0, ld)
                    if pi == 0:
                        out[rows, :] = src[r * ld:(r + 1) * ld, :]
                    else:
                        out[rows, :] = out[rows, :] + src[r * ld:(r + 1) * ld, :]
        comm.wait_at((hp == 3) & (g == ng - 1), cm)

    def pspec(base, shift):
        return pl.BlockSpec((gsz, BLK), lambda hp, g: (jnp.clip(g + shift, 0, ng - 1), base // BLK + hp))

    def wspec(shift):
        return pl.BlockSpec((gsz, BLK), lambda hp, g: (jnp.clip(g + shift, 0, ng - 1), hp))

    in_specs = [pspec(C_Q, 0), pspec(C_Q, 1), pspec(C_K, -1), pspec(C_K, 0), pspec(C_V, -1), pspec(C_V, 0),
                wspec(0), wspec(1), wspec(0), wspec(1), wspec(0), wspec(1)] + [ANY] * comm.n
    res = pl.pallas_call(
        body, name=name, grid=(4, ng), in_specs=in_specs,
        out_specs=[wspec(0)] * 3 + [ANY] * comm.n,
        out_shape=[jax.ShapeDtypeStruct((s, ATT_W), F32)] * 3 + comm.out_shape(),
        scratch_shapes=[pltpu.VMEM((2 * gsz, BLK), F32)] * 6 + [pltpu.VMEM((gsz, BLK), F32)] * 3 + comm.scratch(),
        compiler_params=_cp("arbitrary", "arbitrary"),
    )(proj, proj, proj, proj, proj, proj, datt, datt, lse, lse, delta, delta, *comm.args())
    return res[0], res[1], res[2], list(res[3:])


def _shift_down(cur, halo, sft):
    if sft == 0:
        return cur
    t = cur.shape[0]
    rolled = pltpu.roll(cur, sft, 0)
    hr = pltpu.roll(halo, sft, 0)
    row = lax.broadcasted_iota(jnp.int32, cur.shape, 0)
    return jnp.where(row < sft, jnp.tile(hr, (t // 8, 1)), rolled)


def _shift_up(cur, halo, sft):
    if sft == 0:
        return cur
    t = cur.shape[0]
    rolled = pltpu.roll(cur, t - sft, 0)
    hr = pltpu.roll(halo, 8 - sft, 0)
    row = lax.broadcasted_iota(jnp.int32, cur.shape, 0)
    return jnp.where(row >= t - sft, jnp.tile(hr, (t // 8, 1)), rolled)


def _conv(x, xh, w, b):
    y = b + x * w[CONV_K - 1:CONV_K]
    for k in range(CONV_K - 1):
        y = y + _shift_down(x, xh, CONV_K - 1 - k) * w[k:k + 1]
    return y


def _conv_bwd(x, xh, dy, dyh, w):
    dx = dy * w[CONV_K - 1:CONV_K]
    dws = []
    for k in range(CONV_K - 1):
        sft = CONV_K - 1 - k
        dx = dx + _shift_up(dy, dyh, sft) * w[k:k + 1]
        dws.append(jnp.sum(dy * _shift_down(x, xh, sft), axis=0, keepdims=True))
    dws.append(jnp.sum(dy * x, axis=0, keepdims=True))
    c = x.shape[1]
    dw = jnp.concatenate(dws + [jnp.zeros((8 - CONV_K, c), F32)], axis=0)
    return dx, dw, jnp.sum(dy, axis=0, keepdims=True)


def _pad8(w):
    return jnp.concatenate([w, jnp.zeros((8 - w.shape[0], w.shape[1]), w.dtype)], axis=0)


def _ssd_pre(proj, conv_w, conv_b, dt_bias128, name):
    def fn(rv, hv, cv):
        xbc, dtr = rv
        return [_silu(_conv(xbc, hv[0], cv[0], cv[1])), _softplus(dtr + cv[2])], []
    return _rows(fn, [(proj, 1024, C_XBC // 1024), (proj, BLK, C_DT // BLK)],
                 [_pad8(conv_w), conv_b.reshape(1, -1), dt_bias128],
                 [(1024, F32), (BLK, F32)], tile=256, name=name, halos=[(0, "prev")])


def _ssd_pre_bwd(proj, dxc, ddt, conv_w, conv_b, dt_bias128, name):
    def fn(rv, hv, cv):
        xbc, dtr, dxcb, ddtb = rv
        xh, dxch_raw, xnext = hv
        w, b, bias = cv
        pre = _conv(xbc, xh, w, b)
        sg = _sigmoid(pre)
        dpre = dxcb * (sg * (1.0 + pre * (1.0 - sg)))
        t = xbc.shape[0]
        tail = jnp.concatenate([xbc[t - 8:], xnext], axis=0)
        pre_n = _conv(tail[8:], tail[:8], w, b)
        sgn = _sigmoid(pre_n)
        dpre_h = dxch_raw * (sgn * (1.0 + pre_n * (1.0 - sgn)))
        dx, dw, db = _conv_bwd(xbc, xh, dpre, dpre_h, w)
        ddr = ddtb * _sigmoid(dtr + bias)
        return [dx, ddr], [dw, jnp.concatenate([db, jnp.zeros((7, db.shape[1]), F32)], axis=0), _colsum8(ddr)]
    return _rows(fn, [(proj, 1024, C_XBC // 1024), (proj, BLK, C_DT // BLK), dxc, ddt],
                 [_pad8(conv_w), conv_b.reshape(1, -1), dt_bias128],
                 [(1024, F32), (BLK, F32)], [(8, 1024), (8, 1024), (8, BLK)], tile=256, name=name,
                 halos=[(0, "prev"), (2, "next"), (0, "next")])


def _head_cols(v, h0):
    lane = lax.broadcasted_iota(jnp.int32, (v.shape[0], BLK), 1)
    return jnp.where(lane < HEAD_DIM, v[:, h0:h0 + 1], v[:, h0 + 1:h0 + 2])


def _ssd_scan(xc, dt, par, name):
    s = xc.shape[0]
    nc = s // BLK

    def body(x_ref, dt_ref, par_ref, y_ref, st_ref, h_ref):
        c = pl.program_id(0)

        @pl.when(c == 0)
        def _():
            h_ref[...] = jnp.zeros_like(h_ref)

        st_ref[0] = h_ref[...]
        dt = dt_ref[...]
        a_row = -jnp.exp(par_ref[0:1, :])
        d_row = par_ref[1:2, :]
        ri = lax.broadcasted_iota(jnp.int32, (BLK, BLK), 0)
        ci = lax.broadcasted_iota(jnp.int32, (BLK, BLK), 1)
        tril = ri >= ci
        cs = _nn(tril.astype(F32), dt * a_row, HI)
        cst, dtt = cs.T, dt.T
        last = cs[BLK - 1:BLK, :]
        wcol = jnp.exp(last - cs) * dt
        ecs = jnp.exp(cs)
        elast = jnp.exp(last)
        for g in (0, 1):
            bg = x_ref[:, 512 + g * BLK:512 + (g + 1) * BLK].astype(MXU)
            cg = x_ref[:, 768 + g * BLK:768 + (g + 1) * BLK].astype(MXU)
            gm = _nt(cg, bg)
            for pp in (0, 1):
                pr = 2 * g + pp
                h0 = 2 * pr
                x2 = x_ref[:, pr * BLK:(pr + 1) * BLK]
                hprev = h_ref[pr * BLK:(pr + 1) * BLK, :]
                yp = jnp.zeros((BLK, BLK), F32)
                for hh in (0, 1):
                    h = h0 + hh
                    hmask = (ci < HEAD_DIM) if hh == 0 else (ci >= HEAD_DIM)
                    lm = jnp.exp(jnp.where(tril, cs[:, h:h + 1] - cst[h:h + 1, :], NEG))
                    mm = gm * lm * dtt[h:h + 1, :]
                    yp = yp + _nn(mm.astype(MXU), jnp.where(hmask, x2, 0.0).astype(MXU))
                y0 = _nt(cg, hprev.astype(MXU))
                y_ref[:, pr * BLK:(pr + 1) * BLK] = yp + _head_cols(ecs, h0) * y0 + _head_cols(d_row, h0) * x2
                dec = jnp.where(ri < HEAD_DIM, elast[:, h0:h0 + 1], elast[:, h0 + 1:h0 + 2])
                xw = (x2 * _head_cols(wcol, h0)).astype(MXU)
                h_ref[pr * BLK:(pr + 1) * BLK, :] = dec * hprev + _tn(xw, bg)

    return pl.pallas_call(
        body, name=name, grid=(nc,),
        in_specs=[pl.BlockSpec((BLK, 1024), lambda c: (c, 0)), pl.BlockSpec((BLK, BLK), lambda c: (c, 0)),
                  pl.BlockSpec((8, BLK), lambda c: (0, 0))],
        out_specs=[pl.BlockSpec((BLK, SSD_W), lambda c: (c, 0)), pl.BlockSpec((1, SSD_W, SSD_STATE), lambda c: (c, 0, 0))],
        out_shape=[jax.ShapeDtypeStruct((s, SSD_W), F32), jax.ShapeDtypeStruct((nc, SSD_W, SSD_STATE), F32)],
        scratch_shapes=[pltpu.VMEM((SSD_W, SSD_STATE), F32)],
        compiler_params=_cp("arbitrary"),
    )(xc, dt, par)


def _ssd_scan_bwd(xc, dt, par, st, dy, name, comm=None):
    s = xc.shape[0]
    nc = s // BLK
    comm = comm or _Comm()

    def body(*refs):
        (x_ref, dt_ref, par_ref, st_ref, dy_ref, dx_ref, ddt_ref, dal_ref, dd_ref, dh_ref), cm = comm.split(refs, 5, 4, 1)
        c = pl.program_id(0)
        comm.start_at(c == 0, cm)

        @pl.when(c == 0)
        def _():
            dh_ref[...] = jnp.zeros_like(dh_ref)
            dal_ref[...] = jnp.zeros_like(dal_ref)
            dd_ref[...] = jnp.zeros_like(dd_ref)

        dt = dt_ref[...]
        a_row = -jnp.exp(par_ref[0:1, :])
        d_row = par_ref[1:2, :]
        ri = lax.broadcasted_iota(jnp.int32, (BLK, BLK), 0)
        ci = lax.broadcasted_iota(jnp.int32, (BLK, BLK), 1)
        tril = ri >= ci
        cs = _nn(tril.astype(F32), dt * a_row, HI)
        cst, dtt = cs.T, dt.T
        last = cs[BLK - 1:BLK, :]
        tolast = jnp.exp(last - cs)
        wcol = tolast * dt
        ecs = jnp.exp(cs)
        elast = jnp.exp(last)
        dcs_col = jnp.zeros((BLK, BLK), F32)
        ddt_col = jnp.zeros((BLK, BLK), F32)
        dcs_row = jnp.zeros((BLK, BLK), F32)
        ddt_row = jnp.zeros((BLK, BLK), F32)
        dlast = jnp.zeros((1, BLK), F32)
        ddsk = jnp.zeros((1, BLK), F32)
        for g in (0, 1):
            bg32 = x_ref[:, 512 + g * BLK:512 + (g + 1) * BLK]
            cg32 = x_ref[:, 768 + g * BLK:768 + (g + 1) * BLK]
            bg, cg = bg32.astype(MXU), cg32.astype(MXU)
            gm = _nt(cg, bg)
            dgm = jnp.zeros((BLK, BLK), F32)
            dbg = jnp.zeros((BLK, BLK), F32)
            dcg = jnp.zeros((BLK, BLK), F32)
            for pp in (0, 1):
                pr = 2 * g + pp
                h0 = 2 * pr
                x2 = x_ref[:, pr * BLK:(pr + 1) * BLK]
                dy2 = dy_ref[:, pr * BLK:(pr + 1) * BLK]
                hprev = st_ref[0, pr * BLK:(pr + 1) * BLK, :]
                dhn = dh_ref[pr * BLK:(pr + 1) * BLK, :]
                x2m, dhnm = x2.astype(MXU), dhn.astype(MXU)
                zb = _nt(bg, dhnm)
                y0 = _nt(cg, hprev.astype(MXU))
                esel = _head_cols(ecs, h0)
                wsel = _head_cols(wcol, h0)
                dx2 = _head_cols(d_row, h0) * dy2 + wsel * zb
                r_off = dy2 * y0
                r_w = x2 * zb
                r_d = dy2 * x2
                r_h = dhn * hprev
                for hh in (0, 1):
                    h = h0 + hh
                    hmask = (ci < HEAD_DIM) if hh == 0 else (ci >= HEAD_DIM)
                    onl = (ci == h).astype(F32)
                    ons = (ri == h).astype(F32)
                    dym = jnp.where(hmask, dy2, 0.0).astype(MXU)
                    dt_r = dtt[h:h + 1, :]
                    lm = jnp.exp(jnp.where(tril, cs[:, h:h + 1] - cst[h:h + 1, :], NEG))
                    mm = gm * lm * dt_r
                    dx2 = dx2 + _tn(mm.astype(MXU), dym)
                    dm = _nt(dym, x2m)
                    t1 = dm * lm
                    dgm = dgm + t1 * dt_r
                    tt = t1 * gm
                    ddt_row = ddt_row + ons * jnp.sum(tt, axis=0, keepdims=True)
                    t = tt * dt_r
                    dcs_col = dcs_col + onl * jnp.sum(t, axis=1, keepdims=True)
                    dcs_row = dcs_row - ons * jnp.sum(t, axis=0, keepdims=True)
                    de = jnp.sum(jnp.where(hmask, r_off, 0.0), axis=1, keepdims=True)
                    dcs_col = dcs_col + onl * (ecs[:, h:h + 1] * de)
                    hrow = (ri < HEAD_DIM) if hh == 0 else (ri >= HEAD_DIM)
                    dl_h = elast[:, h:h + 1] * jnp.sum(jnp.where(hrow, r_h, 0.0), keepdims=True)
                    dw = jnp.sum(jnp.where(hmask, r_w, 0.0), axis=1, keepdims=True)
                    ddt_col = ddt_col + onl * (dw * tolast[:, h:h + 1])
                    v = dw * wcol[:, h:h + 1]
                    dcs_col = dcs_col - onl * v
                    dl_h = dl_h + jnp.sum(v, keepdims=True)
                    dlast = dlast + onl[0:1, :] * dl_h
                    ddsk = ddsk + onl[0:1, :] * jnp.sum(jnp.where(hmask, r_d, 0.0), keepdims=True)
                dx_ref[:, pr * BLK:(pr + 1) * BLK] = dx2
                edy = (esel * dy2).astype(MXU)
                dcg = dcg + _nn(edy, hprev.astype(MXU))
                dec = jnp.where(ri < HEAD_DIM, elast[:, h0:h0 + 1], elast[:, h0 + 1:h0 + 2])
                dh_ref[pr * BLK:(pr + 1) * BLK, :] = dec * dhn + _tn(edy, cg)
                dbg = dbg + _nn((x2 * wsel).astype(MXU), dhnm)
            dgmm = dgm.astype(MXU)
            dx_ref[:, 512 + g * BLK:512 + (g + 1) * BLK] = dbg + _tn(dgmm, cg)
            dx_ref[:, 768 + g * BLK:768 + (g + 1) * BLK] = dcg + _nn(dgmm, bg)
        dcs = dcs_col + dcs_row.T + jnp.where(ri == BLK - 1, dlast, 0.0)
        dda = _nn((ri <= ci).astype(F32), dcs, HI)
        ddt_ref[...] = ddt_col + ddt_row.T + a_row * dda
        da = jnp.sum(dt * dda, axis=0, keepdims=True)
        dal_ref[0:1, :] += da * a_row
        dd_ref[0:1, :] += ddsk
        comm.wait_at(c == nc - 1, cm)

    rev = lambda c: (nc - 1 - c, 0)
    res = pl.pallas_call(
        body, name=name, grid=(nc,),
        in_specs=[pl.BlockSpec((BLK, 1024), rev), pl.BlockSpec((BLK, BLK), rev), pl.BlockSpec((8, BLK), lambda c: (0, 0)),
                  pl.BlockSpec((1, SSD_W, SSD_STATE), lambda c: (nc - 1 - c, 0, 0)), pl.BlockSpec((BLK, SSD_W), rev)]
        + [ANY] * comm.n,
        out_specs=[pl.BlockSpec((BLK, 1024), rev), pl.BlockSpec((BLK, BLK), rev),
                   pl.BlockSpec((8, BLK), lambda c: (0, 0)), pl.BlockSpec((8, BLK), lambda c: (0, 0))] + [ANY] * comm.n,
        out_shape=[jax.ShapeDtypeStruct((s, 1024), F32), jax.ShapeDtypeStruct((s, BLK), F32),
                   jax.ShapeDtypeStruct((8, BLK), F32), jax.ShapeDtypeStruct((8, BLK), F32)] + comm.out_shape(),
        scratch_shapes=[pltpu.VMEM((SSD_W, SSD_STATE), F32)] + comm.scratch(),
        compiler_params=_cp("arbitrary"),
    )(xc, dt, par, st, dy, *comm.args())
    return res[0], res[1], res[2], res[3], list(res[4:])


def _ssd_gate(y, z, w):
    t = y * _silu(z)
    outs = []
    for g in (0, 1):
        tg = t[:, g * 256:(g + 1) * 256]
        outs.append(tg * lax.rsqrt(jnp.mean(tg * tg, axis=-1, keepdims=True) + SSD_NORM_EPS))
    return jnp.concatenate(outs, axis=1) * w


def _ssd_post(y, proj, norm_w, name):
    def fn(rv, hv, cv):
        return [_ssd_gate(rv[0], rv[1], cv[0])], []
    return _rows(fn, [y, (proj, SSD_W, C_Z // SSD_W)], [norm_w.reshape(1, -1)], [(SSD_W, F32)], tile=512, name=name)[0]


def _ssd_post_bwd(y, proj, norm_w, dout, name):
    def fn(rv, hv, cv):
        yb, zb, db = rv
        _, vjp = jax.vjp(lambda a, b: _ssd_gate(a, b, cv[0]), yb, zb)
        dy, dz = vjp(db)
        t = yb * _silu(zb)
        nrm = []
        for g in (0, 1):
            tg = t[:, g * 256:(g + 1) * 256]
            nrm.append(tg * lax.rsqrt(jnp.mean(tg * tg, axis=-1, keepdims=True) + SSD_NORM_EPS))
        return [dy, dz], [_colsum8(db * jnp.concatenate(nrm, axis=1))]
    return _rows(fn, [y, (proj, SSD_W, C_Z // SSD_W), dout], [norm_w.reshape(1, -1)],
                 [(SSD_W, F32), (SSD_W, F32)], [(8, SSD_W)], tile=512, name=name)


LRU_T = 256


def _lru_conv(proj, conv_w, conv_b, name):
    def fn(rv, hv, cv):
        return [_conv(rv[0], hv[0], cv[0], cv[1])], []
    return _rows(fn, [(proj, LRU_W, C_XL // LRU_W)], [_pad8(conv_w), conv_b.reshape(1, -1)], [(LRU_W, F32)],
                 tile=512, name=name, halos=[(0, "prev")])[0]


def _lru_conv_bwd(proj, dxc, conv_w, name):
    def fn(rv, hv, cv):
        dx, dw, db = _conv_bwd(rv[0], hv[0], rv[1], hv[1], cv[0])
        return [dx], [dw, jnp.concatenate([db, jnp.zeros((7, db.shape[1]), F32)], axis=0)]
    return _rows(fn, [(proj, LRU_W, C_XL // LRU_W), dxc], [_pad8(conv_w)], [(LRU_W, F32)], [(8, LRU_W), (8, LRU_W)],
                 tile=512, name=name, halos=[(0, "prev"), (1, "next")])


def _lru_au(pre_a, pre_x, xc, ba, bx, lam):
    r = _sigmoid(pre_a + ba)
    i = _sigmoid(pre_x + bx)
    log_a = -LRU_C * r * _softplus(-lam)
    a = jnp.exp(log_a)
    u = jnp.sqrt(1.0 - jnp.exp(2.0 * log_a)) * (i * xc)
    return a, u


def _lru_scan(pre, xc, proj, par, name):
    s = xc.shape[0]
    t = LRU_T

    def body(pre_ref, xc_ref, g_ref, par_ref, out_ref, h_ref, carry):
        c = pl.program_id(0)

        @pl.when(c == 0)
        def _():
            carry[...] = jnp.zeros_like(carry)

        a, u = _lru_au(pre_ref[:, :LRU_W], pre_ref[:, LRU_W:], xc_ref[...], par_ref[0:1, :], par_ref[1:2, :], par_ref[2:3, :])
        row = lax.broadcasted_iota(jnp.int32, (t, LRU_W), 0)
        sft = 1
        while sft < t:
            keep = row >= sft
            a_s = jnp.where(keep, pltpu.roll(a, sft, 0), 1.0)
            u_s = jnp.where(keep, pltpu.roll(u, sft, 0), 0.0)
            u = a * u_s + u
            a = a * a_s
            sft *= 2
        h = a * carry[0:1, :] + u
        h_ref[...] = h
        out_ref[...] = h * _gelu(g_ref[...])
        carry[0:1, :] = h[t - 1:t, :]

    return pl.pallas_call(
        body, name=name, grid=(s // t,),
        in_specs=[pl.BlockSpec((t, 2 * LRU_W), lambda c: (c, 0)), pl.BlockSpec((t, LRU_W), lambda c: (c, 0)),
                  pl.BlockSpec((t, LRU_W), lambda c: (c, C_G // LRU_W)), pl.BlockSpec((8, LRU_W), lambda c: (0, 0))],
        out_specs=[pl.BlockSpec((t, LRU_W), lambda c: (c, 0))] * 2,
        out_shape=[jax.ShapeDtypeStruct((s, LRU_W), F32)] * 2,
        scratch_shapes=[pltpu.VMEM((8, LRU_W), F32)],
        compiler_params=_cp("arbitrary"),
    )(pre, xc, proj, par)


def _lru_scan_bwd(pre, xc, proj, par, h, dout, name):
    s = xc.shape[0]
    t = LRU_T
    n = s // t
    t8 = t // 8

    def body(pre_ref, xc_ref, g_ref, par_ref, h_ref, hh_ref, do_ref, dpre_ref, dxc_ref, dg_ref, dpar_ref, carry):
        c = pl.program_id(0)

        @pl.when(c == 0)
        def _():
            carry[...] = jnp.zeros_like(carry)
            dpar_ref[...] = jnp.zeros_like(dpar_ref)

        pa, px, xcb = pre_ref[:, :LRU_W], pre_ref[:, LRU_W:], xc_ref[...]
        ba, bx, lam = par_ref[0:1, :], par_ref[1:2, :], par_ref[2:3, :]
        (a, u), vjp = jax.vjp(_lru_au, pa, px, xcb, ba, bx, lam)
        g = g_ref[...]
        hcur = h_ref[...]
        do = do_ref[...]
        _, gvjp = jax.vjp(_gelu, g)
        dg_ref[...] = gvjp(do * hcur)[0]
        row = lax.broadcasted_iota(jnp.int32, (t, LRU_W), 0)
        v = do * _gelu(g) + jnp.where(row == t - 1, carry[0:1, :], 0.0)
        b = jnp.where(row == t - 1, 0.0, pltpu.roll(a, t - 1, 0))
        sft = 1
        while sft < t:
            keep = row < t - sft
            b_s = jnp.where(keep, pltpu.roll(b, t - sft, 0), 1.0)
            v_s = jnp.where(keep, pltpu.roll(v, t - sft, 0), 0.0)
            v = b * v_s + v
            b = b * b_s
            sft *= 2
        dh = v
        carry[0:1, :] = a[0:1, :] * dh[0:1, :]
        hhalo = jnp.where(c == n - 1, 0.0, hh_ref[...])
        hprev = _shift_down(hcur, hhalo, 1)
        dpa, dpx, dxc, dba, dbx, dlam = vjp((dh * hprev, dh))
        dpre_ref[:, :LRU_W] = dpa
        dpre_ref[:, LRU_W:] = dpx
        dxc_ref[...] = dxc
        dpar_ref[0:1, :] += dba
        dpar_ref[1:2, :] += dbx
        dpar_ref[2:3, :] += dlam

    rev = lambda c: (n - 1 - c, 0)
    return pl.pallas_call(
        body, name=name, grid=(n,),
        in_specs=[pl.BlockSpec((t, 2 * LRU_W), rev), pl.BlockSpec((t, LRU_W), rev),
                  pl.BlockSpec((t, LRU_W), lambda c: (n - 1 - c, C_G // LRU_W)), pl.BlockSpec((8, LRU_W), lambda c: (0, 0)),
                  pl.BlockSpec((t, LRU_W), rev),
                  pl.BlockSpec((8, LRU_W), lambda c: (jnp.maximum((n - 1 - c) * t8 - 1, 0), 0)),
                  pl.BlockSpec((t, LRU_W), rev)],
        out_specs=[pl.BlockSpec((t, 2 * LRU_W), rev), pl.BlockSpec((t, LRU_W), rev), pl.BlockSpec((t, LRU_W), rev),
                   pl.BlockSpec((8, LRU_W), lambda c: (0, 0))],
        out_shape=[jax.ShapeDtypeStruct((s, 2 * LRU_W), F32), jax.ShapeDtypeStruct((s, LRU_W), F32),
                   jax.ShapeDtypeStruct((s, LRU_W), F32), jax.ShapeDtypeStruct((8, LRU_W), F32)],
        scratch_shapes=[pltpu.VMEM((8, LRU_W), F32)],
        compiler_params=_cp("arbitrary"),
    )(pre, xc, proj, par, h, h, dout)


def _swiglu_act(gu, name):
    def fn(rv, hv, cv):
        return [_silu(rv[0]) * rv[1]], []
    return _rows(fn, [(gu, D_FF, 0), (gu, D_FF, 1)], [], [(D_FF, MXU)], tile=256, name=name)[0]


def _swiglu_bwd(gu, da, name):
    def fn(rv, hv, cv):
        gt, up, dab = rv
        sg = _sigmoid(gt)
        dgate = dab * up * (sg * (1.0 + gt * (1.0 - sg)))
        dup = dab * (gt * sg)
        return [jnp.concatenate([dgate, dup], axis=1)], []
    return _rows(fn, [(gu, D_FF, 0), (gu, D_FF, 1), da], [], [(2 * D_FF, MXU)], tile=256, name=name)[0]


def _loss_head(x, g, target, name):
    d = x.shape[1]

    def fn(rv, hv, cv):
        xb, tb = rv
        y, vjp = jax.vjp(_rms, xb, cv[0])
        err = y - tb
        dy = err * (1.0 / d)
        dx, _ = vjp(dy)
        rstd = lax.rsqrt(jnp.mean(xb * xb, axis=-1, keepdims=True) + NORM_EPS)
        e2 = err * err * (0.5 / d)
        e2 = functools.reduce(lambda a, b: a + b, [e2[:, k * BLK:(k + 1) * BLK] for k in range(d // BLK)])
        return [dx], [_colsum8(dy * xb * rstd), _colsum8(e2)]
    return _rows(fn, [x, target], [g.reshape(1, -1)], [(d, F32)], [(8, d), (8, BLK)], tile=512, name=name)


ANY = pl.BlockSpec(memory_space=pl.ANY)


def _coords():
    return lax.axis_index("x"), lax.axis_index("y"), lax.axis_index("c")


class _Comm:
    def __init__(self, gathers=(), scatters=()):
        self.gathers = list(gathers)
        self.scatters = list(scatters)
        self.n = len(self.gathers) + len(self.scatters)

    def args(self):
        return [a for a, _ in self.gathers] + self.scatters

    def out_shape(self):
        out = [jax.ShapeDtypeStruct((4,) + (a.shape if l is None else a.shape[1:]), a.dtype) for a, l in self.gathers]
        return out + [jax.ShapeDtypeStruct((3,) + a.shape[1:], a.dtype) for a in self.scatters]

    def scratch(self):
        if not self.n:
            return []
        return [pltpu.SemaphoreType.DMA((3 * self.n,)), pltpu.SemaphoreType.DMA((3 * self.n,)),
                pltpu.SemaphoreType.DMA((max(len(self.gathers), 1),))]

    def split(self, refs, n_in, n_out, n_scratch):
        refs = list(refs)
        n = self.n
        own = refs[:n_in] + refs[n_in + n:n_in + n + n_out] + refs[n_in + 2 * n + n_out:n_in + 2 * n + n_out + n_scratch]
        cm = (refs[n_in:n_in + n], refs[n_in + n + n_out:n_in + 2 * n + n_out], refs[n_in + 2 * n + n_out + n_scratch:])
        return own, cm

    def _copies(self, cm, arriving):
        ins, outs, (send, recv, local) = cm
        x, y, c = _coords()
        me = 2 * x + y
        chips = [(1 - x, y), (x, 1 - y), (1 - x, 1 - y)]
        remote, locals_ = [], []
        ng = len(self.gathers)
        for i in range(self.n):
            if i < ng:
                l = self.gathers[i][1]
                slab = ins[i] if l is None else ins[i].at[l]
                if not arriving:
                    locals_.append(pltpu.make_async_copy(slab, outs[i].at[me], local.at[i]))
            for j, (px, py) in enumerate(chips):
                if i < ng:
                    src, dst = slab, outs[i].at[2 * px + py if arriving else me]
                else:
                    src, dst = ins[i].at[2 * px + py], outs[i].at[j]
                remote.append(pltpu.make_async_remote_copy(src, dst, send.at[3 * i + j], recv.at[3 * i + j],
                                                           device_id=(px, py, c), device_id_type=MESH))
        return remote, locals_

    def start_at(self, cond, cm):
        def go():
            remote, locals_ = self._copies(cm, False)
            for cp in locals_ + remote:
                cp.start()

        if self.n:
            go() if cond is True else pl.when(cond)(go)

    def wait_at(self, cond, cm):
        def go():
            for cp in self._copies(cm, True)[0]:
                cp.wait_recv()
            remote, locals_ = self._copies(cm, False)
            for cp in remote:
                cp.wait_send()
            for cp in locals_:
                cp.wait()

        if self.n:
            go() if cond is True else pl.when(cond)(go)


def _comm_call(comm, name):
    def body(*refs):
        _, cm = comm.split(refs, 0, 0, 0)
        comm.start_at(True, cm)
        comm.wait_at(True, cm)

    return list(pl.pallas_call(
        body, name=name, in_specs=[ANY] * comm.n, out_specs=[ANY] * comm.n, out_shape=comm.out_shape(),
        scratch_shapes=comm.scratch(), compiler_params=pltpu.CompilerParams(has_side_effects=True),
    )(*comm.args()))


def _swap_sibling(arrs):
    n = len(arrs)

    def body(*refs):
        ins, outs, send, recv = refs[:n], refs[n:2 * n], refs[2 * n], refs[2 * n + 1]
        x, y, c = _coords()
        cps = [pltpu.make_async_remote_copy(ins[i], outs[i], send.at[i], recv.at[i], device_id=(x, y, 1 - c), device_id_type=MESH)
               for i in range(n)]
        for cp in cps:
            cp.start()
        for cp in cps:
            cp.wait_recv()
        for cp in cps:
            cp.wait_send()

    return list(pl.pallas_call(
        body, name="swap_sibling", in_specs=[ANY] * n, out_specs=[ANY] * n,
        out_shape=[jax.ShapeDtypeStruct(a.shape, a.dtype) for a in arrs],
        scratch_shapes=[pltpu.SemaphoreType.DMA((n,)), pltpu.SemaphoreType.DMA((n,))],
        compiler_params=pltpu.CompilerParams(has_side_effects=True),
    )(*arrs))


def _gather_small(gs):
    def body(g_ref, o_ref, send_sems, recv_sems, local_sem):
        x, y, c = _coords()
        me = 4 * x + 2 * y + c
        mine = pltpu.make_async_copy(g_ref, o_ref.at[me], local_sem)
        mine.start()
        sends = []
        for k in range(1, 8):
            px, py, pc = x ^ (k >> 2), y ^ ((k >> 1) & 1), c ^ (k & 1)
            sends.append((pltpu.make_async_remote_copy(g_ref, o_ref.at[me], send_sems.at[k - 1], recv_sems.at[k - 1],
                                                       device_id=(px, py, pc), device_id_type=MESH), 4 * px + 2 * py + pc, k))
        for cp, _, _ in sends:
            cp.start()
        for cp, src, k in sends:
            pltpu.make_async_remote_copy(g_ref, o_ref.at[src], send_sems.at[k - 1], recv_sems.at[k - 1],
                                         device_id=(x, y, c), device_id_type=MESH).wait_recv()
        for cp, _, _ in sends:
            cp.wait_send()
        mine.wait()

    return pl.pallas_call(
        body, name="gather_small", in_specs=[ANY], out_specs=ANY,
        out_shape=jax.ShapeDtypeStruct((8,) + gs.shape, gs.dtype),
        scratch_shapes=[pltpu.SemaphoreType.DMA((7,)), pltpu.SemaphoreType.DMA((7,)), pltpu.SemaphoreType.DMA],
        compiler_params=pltpu.CompilerParams(has_side_effects=True),
    )(gs)


def _sum_slots(own, others, name, tile):
    k, r, c = others.shape

    def body(*refs):
        if own is None:
            o_ref, out_ref = refs
            acc = o_ref[0]
            first = 1
        else:
            own_ref, o_ref, out_ref = refs
            acc = own_ref[...]
            first = 0
        for j in range(first, k):
            acc = acc + o_ref[j]
        out_ref[...] = acc

    row = pl.BlockSpec((tile, c), lambda i: (i, 0))
    specs = ([] if own is None else [row]) + [pl.BlockSpec((k, tile, c), lambda i: (0, i, 0))]
    args = ([] if own is None else [own]) + [others]
    return pl.pallas_call(body, name=name, grid=(r // tile,), in_specs=specs, out_specs=row,
                          out_shape=jax.ShapeDtypeStruct((r, c), F32), compiler_params=_cp("parallel"))(*args)


def _adamw(w, m, v, ga, gb, name, tile):
    r, c = w.shape

    def body(*refs):
        if gb is None:
            w_ref, m_ref, v_ref, ga_ref, g_ref, d_ref, nm_ref, nv_ref = refs
            g = ga_ref[...]
        else:
            w_ref, m_ref, v_ref, ga_ref, gb_ref, g_ref, d_ref, nm_ref, nv_ref = refs
            g = ga_ref[...] + gb_ref[...]
        nm = ADAM_B1 * m_ref[...] + (1.0 - ADAM_B1) * g
        nv = ADAM_B2 * v_ref[...] + (1.0 - ADAM_B2) * (g * g)
        g_ref[...] = g
        nm_ref[...] = nm
        nv_ref[...] = nv
        d_ref[...] = -ADAM_LR * ((nm / BC1) / (jnp.sqrt(nv / BC2) + ADAM_EPS) + ADAM_WD * w_ref[...])

    row = pl.BlockSpec((tile, c), lambda i: (i, 0))
    args = [w, m, v, ga] + ([] if gb is None else [gb])
    return pl.pallas_call(body, name=name, grid=(r // tile,), in_specs=[row] * len(args), out_specs=[row] * 4,
                          out_shape=[jax.ShapeDtypeStruct((r, c), F32)] * 4, compiler_params=_cp("parallel"))(*args)


MATS = ("w_in", "w_out", "w_gate", "w_up", "w_down")
CONVS = ("ssd_conv_w", "lru_conv_w")
BIG = MATS + CONVS
COL_SHARDED = ("w_in", "w_gate", "w_up", "ssd_conv_w", "lru_conv_w")
SMALL = ("norm_mix", "ssd_conv_b", "ssd_dt_bias", "ssd_a_log", "ssd_d", "ssd_norm", "lru_conv_b", "lru_wa", "lru_ba",
         "lru_wx", "lru_bx", "lru_lambda", "norm_ffn", "norm_final")
WEIGHTS = ("norm_mix", "w_in", "ssd_conv_w", "ssd_conv_b", "ssd_dt_bias", "ssd_a_log", "ssd_d", "ssd_norm", "lru_conv_w",
           "lru_conv_b", "lru_wa", "lru_ba", "lru_wx", "lru_bx", "lru_lambda", "w_out", "norm_ffn", "w_gate", "w_up",
           "w_down", "norm_final")
ROW_TILE = {"w_in": 256, "w_out": 128, "w_gate": 256, "w_up": 256, "w_down": 352}


def _pack(arrs, width, row_mult, dtype):
    flat = jnp.concatenate([a.reshape(-1).astype(dtype) for a in arrs])
    rows = -(-flat.shape[0] // width)
    rows = -(-rows // row_mult) * row_mult
    flat = jnp.pad(flat, (0, rows * width - flat.shape[0]))
    return flat.reshape(rows, width)


def _unpack(buf, shapes):
    flat = buf.reshape(-1)
    out, off = [], 0
    for shp in shapes:
        n = int(np.prod(shp))
        out.append(flat[off:off + n].reshape(shp))
        off += n
    return out


def _join(name, g4):
    if name in COL_SHARDED:
        return jnp.moveaxis(g4, 0, -2).reshape(g4.shape[1:-1] + (4 * g4.shape[-1],))
    return g4.reshape((4 * g4.shape[1],) + g4.shape[2:])


def _slabs(name, g):
    if name in COL_SHARDED:
        return jnp.moveaxis(g.reshape(g.shape[:-1] + (4, g.shape[-1] // 4)), -2, 0)
    return g.reshape((4, g.shape[0] // 4) + g.shape[1:])


def _perm_cols(w):
    pad = jnp.zeros(w.shape[:-1] + (NP - IN_COLS,), w.dtype)
    return jnp.concatenate([w[..., :3072], w[..., 3080:4104], w[..., 3072:3080], pad], axis=-1)


def _unperm_cols(g):
    return jnp.concatenate([g[..., :3072], g[..., C_DT:C_DT + 8], g[..., 3072:4096]], axis=-1)


def _block_diag(w):
    eye = jnp.eye(LRU_BLOCKS, dtype=w.dtype)
    return jnp.einsum("ncd,nm->ncmd", w, eye).reshape(LRU_W, LRU_W)


def _block_diag_extract(g):
    g4 = g.reshape(LRU_BLOCKS, 64, LRU_BLOCKS, 64)
    return jnp.stack([g4[n, :, n, :] for n in range(LRU_BLOCKS)], axis=0)


def _lanes128(v):
    return jnp.pad(v, (0, BLK - v.shape[0])).reshape(1, BLK)


def _layer_mixers(x, p, comm=None):
    h = _rms_fwd(x, p["norm_mix"], "rms_mix")
    proj = _mm(h, p["w_in"], tm=1024, tn=1408, tk=1024, name="mm_in")
    att, lse, got = _att_fwd_fused(proj, "att_fwd", comm)
    xconv, dt = _ssd_pre(proj, p["ssd_conv_w"], p["ssd_conv_b"], _lanes128(p["ssd_dt_bias"]), "ssd_pre")
    spar = jnp.concatenate([_lanes128(p["ssd_a_log"]), _lanes128(p["ssd_d"]), jnp.zeros((6, BLK), F32)], axis=0)
    y, states = _ssd_scan(xconv, dt, spar, "ssd_scan")
    ssd = _ssd_post(y, proj, p["ssd_norm"], "ssd_post")
    xc = _lru_conv(proj, p["lru_conv_w"], p["lru_conv_b"], "lru_conv")
    wab = jnp.concatenate([_block_diag(p["lru_wa"]), _block_diag(p["lru_wx"])], axis=1).astype(MXU)
    pre = _mm(xc, wab, tm=1024, tn=1024, tk=512, name="mm_lru")
    lpar = jnp.concatenate([p["lru_ba"].reshape(1, -1), p["lru_bx"].reshape(1, -1), p["lru_lambda"].reshape(1, -1),
                            jnp.zeros((5, LRU_W), F32)], axis=0)
    lru, hs = _lru_scan(pre, xc, proj, lpar, "lru_scan")
    mix = jnp.concatenate([att, ssd, lru], axis=1).astype(MXU)
    saved = dict(x=x, h=h, proj=proj, att=att, lse=lse, xconv=xconv, dt=dt, spar=spar, y=y, states=states, xc=xc, wab=wab,
                 pre=pre, lpar=lpar, hs=hs, mix=mix)
    return mix, saved, got


def _layer_ffn(x, mix, p, saved):
    x1 = _mm(mix, p["w_out"], add=x, tm=1024, tn=1024, tk=1536, name="mm_out")
    h2 = _rms_fwd(x1, p["norm_ffn"], "rms_ffn")
    gu = _mm(h2, p["w_gu"], tm=1024, tn=1408, tk=1024, name="mm_gu")
    act = _swiglu_act(gu, "swiglu_act")
    x2 = _mm(act, p["w_down"], add=x1, tm=1024, tn=1024, tk=2816, name="mm_down")
    saved.update(x1=x1, h2=h2, gu=gu, act=act)
    return x2


def _layer_bwd(dx2, p, sv, comm_ssd=None, comm_att=None):
    g = {}
    da = _mm(dx2, p["w_down"], tb=True, tm=1024, tn=1408, tk=1024, name="mm_d_act")
    g["w_down"] = _mm(sv["act"], dx2, ta=True, tm=1408, tn=1024, tk=1024, name="mm_g_down")
    dgu = _swiglu_bwd(sv["gu"], da, "swiglu_bwd")
    dh2 = _mm(dgu, p["w_gu"], tb=True, tm=1024, tn=1024, tk=1408, name="mm_d_h2")
    g["w_gu"] = _mm(sv["h2"], dgu, ta=True, tm=1024, tn=1408, tk=1024, name="mm_g_gu")
    dx1, gn = _rms_bwd(sv["x1"], p["norm_ffn"], dh2, dx2, "rms_ffn_bwd")
    g["norm_ffn"] = jnp.sum(gn, axis=0)
    dmix = _mm(dx1, p["w_out"], tb=True, tm=1024, tn=1536, tk=1024, name="mm_d_mix")
    g["w_out"] = _mm(sv["mix"], dx1, ta=True, tm=1536, tn=1024, tk=1024, name="mm_g_out")
    datt, dssd, dlru = dmix[:, :ATT_W], dmix[:, ATT_W:ATT_W + SSD_W], dmix[:, ATT_W + SSD_W:]
    proj = sv["proj"]
    dpre, dxc_u, dgl, dlpar = _lru_scan_bwd(sv["pre"], sv["xc"], proj, sv["lpar"], sv["hs"], dlru, "lru_scan_bwd")
    dxc = _mm(dpre, sv["wab"], tb=True, add=dxc_u, tm=1024, tn=512, tk=1024, name="mm_d_xc")
    gwab = _mm(sv["xc"], dpre, ta=True, tm=512, tn=1024, tk=1024, name="mm_g_lru")
    g["lru_wa"], g["lru_wx"] = _block_diag_extract(gwab[:, :LRU_W]), _block_diag_extract(gwab[:, LRU_W:])
    g["lru_ba"], g["lru_bx"], g["lru_lambda"] = dlpar[0], dlpar[1], dlpar[2]
    dxl, gcw, gcb = _lru_conv_bwd(proj, dxc, p["lru_conv_w"], "lru_conv_bwd")
    g["lru_conv_w"], g["lru_conv_b"] = gcw[:CONV_K], jnp.sum(gcb, axis=0)
    dy, dz, gsn = _ssd_post_bwd(sv["y"], proj, p["ssd_norm"], dssd, "ssd_post_bwd")
    g["ssd_norm"] = jnp.sum(gsn, axis=0)
    dxconv, ddt, dal, ddk, got_ssd = _ssd_scan_bwd(sv["xconv"], sv["dt"], sv["spar"], sv["states"], dy, "ssd_scan_bwd", comm_ssd)
    g["ssd_a_log"], g["ssd_d"] = dal[0, :8], ddk[0, :8]
    dxbc, ddtr, gsw, gsb, gdb = _ssd_pre_bwd(proj, dxconv, ddt, p["ssd_conv_w"], p["ssd_conv_b"],
                                             _lanes128(p["ssd_dt_bias"]), "ssd_pre_bwd")
    g["ssd_conv_w"], g["ssd_conv_b"], g["ssd_dt_bias"] = gsw[:CONV_K], jnp.sum(gsb, axis=0), jnp.sum(gdb, axis=0)[:8]
    delta = _att_delta(datt, sv["att"], "att_delta")
    dq, dk, dv, got_att = _att_bwd_fused(proj, datt, sv["lse"], delta, "att_bwd", None if comm_att is None else comm_att(g))
    dproj = jnp.concatenate([dq, dk, dv, dz, dxbc, dgl, dxl, ddtr], axis=1).astype(MXU)
    dh = _mm(dproj, p["w_in"], tb=True, tm=1024, tn=1024, tk=1408, name="mm_d_h")
    g["w_in"] = _mm(sv["h"], dproj, ta=True, tm=1024, tn=1408, tk=1024, name="mm_g_in")
    dx, gm = _rms_bwd(sv["x"], p["norm_mix"], dh, dx1, "rms_mix_bwd")
    g["norm_mix"] = jnp.sum(gm, axis=0)
    return dx, g, got_ssd, got_att


def _grad_slabs(g, names):
    out = {}
    for n in names:
        if n == "w_in":
            out[n] = _slabs(n, _unperm_cols(g["w_in"]))
        elif n == "w_gate":
            out[n] = _slabs(n, g["w_gu"][:, :D_FF])
        elif n == "w_up":
            out[n] = _slabs(n, g["w_gu"][:, D_FF:])
        else:
            out[n] = _slabs(n, g[n])
    return out


def kernel(x, norm_mix, w_in, ssd_conv_w, ssd_conv_b, ssd_dt_bias, ssd_a_log, ssd_d, ssd_norm, lru_conv_w, lru_conv_b, lru_wa, lru_ba, lru_wx, lru_bx, lru_lambda, w_out, norm_ffn, w_gate, w_up, w_down, norm_final, loss_target, m_norm_mix, m_w_in, m_ssd_conv_w, m_ssd_conv_b, m_ssd_dt_bias, m_ssd_a_log, m_ssd_d, m_ssd_norm, m_lru_conv_w, m_lru_conv_b, m_lru_wa, m_lru_ba, m_lru_wx, m_lru_bx, m_lru_lambda, m_w_out, m_norm_ffn, m_w_gate, m_w_up, m_w_down, m_norm_final, v_norm_mix, v_w_in, v_ssd_conv_w, v_ssd_conv_b, v_ssd_dt_bias, v_ssd_a_log, v_ssd_d, v_ssd_norm, v_lru_conv_w, v_lru_conv_b, v_lru_wa, v_lru_ba, v_lru_wx, v_lru_bx, v_lru_lambda, v_w_out, v_norm_ffn, v_w_gate, v_w_up, v_w_down, v_norm_final):
    loc = dict(locals())
    w = {n: loc[n] for n in WEIGHTS}
    m = {n: loc["m_" + n] for n in WEIGHTS}
    v = {n: loc["v_" + n] for n in WEIGHTS}

    wb = {n: w[n].astype(MXU) for n in MATS}
    first = _comm_call(_Comm(gathers=[(wb["w_in"], 0), (w["ssd_conv_w"], None), (w["lru_conv_w"], None)]), "gather_first")
    convs = {"ssd_conv_w": _join("ssd_conv_w", first[1]), "lru_conv_w": _join("lru_conv_w", first[2])}
    later = [(n, 0) for n in MATS[1:]] + [(n, 1) for n in MATS]
    rest = _Comm(gathers=[(wb[n], l) for n, l in later])
    whole = {("w_in", 0): _join("w_in", first[0])}
    params = {}

    def layer_params(l):
        if l not in params:
            p = {n: w[n][l] for n in SMALL if n != "norm_final"}
            p.update(w_in=_perm_cols(whole["w_in", l]), ssd_conv_w=convs["ssd_conv_w"][l], lru_conv_w=convs["lru_conv_w"][l])
            params[l] = p
        if "w_out" not in params[l] and ("w_out", l) in whole:
            params[l].update(w_out=whole["w_out", l], w_down=whole["w_down", l],
                             w_gu=jnp.concatenate([whole["w_gate", l], whole["w_up", l]], axis=-1))
        return params[l]

    xs = x[0]
    saved = []
    for l in range(DEPTH):
        mix, sv, got = _layer_mixers(xs, layer_params(l), rest if l == 0 else None)
        if l == 0:
            whole.update({k: _join(k[0], a) for k, a in zip(later, got)})
        xs = _layer_ffn(xs, mix, layer_params(l), sv)
        saved.append(sv)
    dx, gnf, lsum = _loss_head(xs, norm_final, loss_target[0], "loss_head")
    loss = lax.psum(jnp.sum(lsum), ("x", "y", "c"))

    dx, g1, _, _ = _layer_bwd(dx, layer_params(1), saved[1])
    s1 = _grad_slabs(g1, BIG)
    ssd_names = ("w_gate", "w_up")
    att1 = ("w_in", "w_out", "w_down") + CONVS
    att0 = ("w_gate", "w_up", "w_down", "w_out")
    s0 = {}

    def comm_att(g0):
        s0.update(_grad_slabs(g0, att0))
        return _Comm(scatters=[s1[n] for n in att1] + [s0[n] for n in att0])

    dx, g0, got_ssd, got_att = _layer_bwd(dx, layer_params(0), saved[0], _Comm(scatters=[s1[n] for n in ssd_names]), comm_att)
    tail0 = ("w_in",) + CONVS
    s0.update(_grad_slabs(g0, tail0))
    got_tail = _comm_call(_Comm(scatters=[s0[n] for n in tail0]), "exchange_tail")
    recv = {(n, 1): a for n, a in zip(ssd_names, got_ssd)}
    recv.update({(n, 1): a for n, a in zip(att1, got_att[:len(att1)])})
    recv.update({(n, 0): a for n, a in zip(att0, got_att[len(att1):])})
    recv.update({(n, 0): a for n, a in zip(tail0, got_tail)})

    me = 2 * lax.axis_index("x") + lax.axis_index("y")
    slabs = (s0, s1)
    part = {}
    for n in BIG:
        per_layer = []
        for l in range(DEPTH):
            own = lax.dynamic_index_in_dim(slabs[l][n], me, axis=0, keepdims=False)
            per_layer.append(_sum_slots(own, recv[n, l], "sum_chips_" + n, ROW_TILE.get(n, own.shape[0])))
        part[n] = jnp.concatenate(per_layer, axis=0)
    sib = dict(zip(BIG, _swap_sibling([part[n] for n in BIG])))
    out_g, out_d, out_m, out_v = {}, {}, {}, {}
    for n in BIG:
        shp = w[n].shape
        flat = (shp[0] * shp[1], shp[2])
        res = _adamw(w[n].reshape(flat), m[n].reshape(flat), v[n].reshape(flat), part[n], sib[n], "adamw_" + n,
                     ROW_TILE.get(n, flat[0]))
        out_g[n], out_d[n], out_m[n], out_v[n] = [r.reshape(shp) for r in res]

    gsm = {n: jnp.stack([g0[n], g1[n]], axis=0) for n in SMALL if n != "norm_final"}
    gsm["norm_final"] = jnp.sum(gnf, axis=0)
    small_shapes = [w[n].shape for n in SMALL]
    gs = _pack([gsm[n].reshape(w[n].shape) for n in SMALL], BLK, 8, F32)
    gall = _gather_small(gs)
    gsum = _sum_slots(None, gall, "sum_devices", gs.shape[0])
    ws = _pack([w[n] for n in SMALL], BLK, 8, F32)
    ms = _pack([m[n] for n in SMALL], BLK, 8, F32)
    vs = _pack([v[n] for n in SMALL], BLK, 8, F32)
    gsr, dsr, nms, nvs = _adamw(ws, ms, vs, gsum, None, "adamw_small", gs.shape[0])
    out_g.update(zip(SMALL, _unpack(gsr, small_shapes)))
    out_d.update(zip(SMALL, _unpack(dsr, small_shapes)))
    out_m.update(zip(SMALL, _unpack(nms, small_shapes)))
    out_v.update(zip(SMALL, _unpack(nvs, small_shapes)))

    return (loss, dx[None], *[out_g[n] for n in WEIGHTS], *[out_d[n] for n in WEIGHTS],
            *[out_m[n] for n in WEIGHTS], *[out_v[n] for n in WEIGHTS])
```

```python
import functools
import math

import jax
import jax.numpy as jnp
import numpy as np
from jax import lax
from jax.experimental import pallas as pl
from jax.experimental.pallas import tpu as pltpu

F32 = jnp.float32
MXU = jnp.bfloat16
HI = lax.Precision.HIGHEST
MESH = pl.DeviceIdType.MESH

D_MODEL = 1024
DEPTH = 2
HEAD_DIM = 64
ATT_W = 512
ATT_PATTERNS = ((128, 1), (512, 4), (2048, 16))
BLK = 128
SSD_W = 512
SSD_STATE = 128
LRU_W = 512
LRU_BLOCKS = 8
LRU_C = 8.0
CONV_K = 4
D_MIX = 1536
D_FF = 2816
IN_COLS = 4104
NP = 4224
NORM_EPS = 1e-6
SSD_NORM_EPS = 1e-5
LN2 = math.log(2.0)
NEG = -1e30

ADAM_LR, ADAM_B1, ADAM_B2, ADAM_EPS, ADAM_WD, ADAM_STEP = 0.001, 0.9, 0.999, 1e-08, 0.01, 10
BC1 = 1.0 - ADAM_B1 ** ADAM_STEP
BC2 = 1.0 - ADAM_B2 ** ADAM_STEP

VMEM_LIMIT = 56 * 1024 * 1024

C_Q, C_K, C_V, C_Z, C_XBC, C_G, C_XL, C_DT = 0, 512, 1024, 1536, 2048, 3072, 3584, 4096


def _cp(*sem):
    return pltpu.CompilerParams(dimension_semantics=sem, vmem_limit_bytes=VMEM_LIMIT)


def _dot(a, b, dims, prec=None):
    return lax.dot_general(a, b, (dims, ((), ())), preferred_element_type=F32, precision=prec)


def _nn(a, b, prec=None):
    return _dot(a, b, ((1,), (0,)), prec)


def _nt(a, b, prec=None):
    return _dot(a, b, ((1,), (1,)), prec)


def _tn(a, b, prec=None):
    return _dot(a, b, ((0,), (0,)), prec)


def _sigmoid(x):
    return jax.nn.sigmoid(x)


def _silu(x):
    return x * _sigmoid(x)


def _softplus(x):
    return jnp.maximum(x, 0.0) + jnp.log(1.0 + jnp.exp(-jnp.abs(x)))


def _gelu(x):
    return 0.5 * x * (1.0 + jnp.tanh(0.7978845608028654 * (x + 0.044715 * x * x * x)))


def _mm(a, b, *, ta=False, tb=False, add=None, out_dtype=F32, tm, tn, tk, name):
    m, k = (a.shape[1], a.shape[0]) if ta else a.shape
    n = b.shape[0] if tb else b.shape[1]
    assert (b.shape[1] if tb else b.shape[0]) == k
    assert m % tm == 0 and n % tn == 0 and k % tk == 0, (name, m, n, k)
    nk = k // tk
    a_spec = pl.BlockSpec((tk, tm), lambda i, j, kk: (kk, i)) if ta else pl.BlockSpec((tm, tk), lambda i, j, kk: (i, kk))
    b_spec = pl.BlockSpec((tn, tk), lambda i, j, kk: (j, kk)) if tb else pl.BlockSpec((tk, tn), lambda i, j, kk: (kk, j))
    o_spec = pl.BlockSpec((tm, tn), lambda i, j, kk: (i, j))
    dims = ((0 if ta else 1,), (1 if tb else 0,))

    def body(*refs):
        if add is None:
            a_ref, b_ref, o_ref, acc = refs
        else:
            a_ref, b_ref, add_ref, o_ref, acc = refs
        kk = pl.program_id(2)

        @pl.when(kk == 0)
        def _():
            acc[...] = jnp.zeros_like(acc)

        acc[...] += _dot(a_ref[...].astype(MXU), b_ref[...].astype(MXU), dims)

        @pl.when(kk == nk - 1)
        def _():
            r = acc[...]
            if add is not None:
                r = r + add_ref[...]
            o_ref[...] = r.astype(out_dtype)

    ins = [a, b] + ([] if add is None else [add])
    specs = [a_spec, b_spec] + ([] if add is None else [o_spec])
    return pl.pallas_call(
        body, name=name, grid=(m // tm, n // tn, nk), in_specs=specs, out_specs=o_spec,
        out_shape=jax.ShapeDtypeStruct((m, n), out_dtype),
        scratch_shapes=[pltpu.VMEM((tm, tn), F32)],
        compiler_params=_cp("parallel", "parallel", "arbitrary"),
    )(*ins)


def _rows(fn, rows, consts=(), outs=(), accs=(), *, tile, name, halos=()):
    rows = [r if isinstance(r, tuple) else (r, r.shape[1], 0) for r in rows]
    s = rows[0][0].shape[0]
    assert s % tile == 0 and tile % 8 == 0
    n = s // tile
    t8 = tile // 8
    nr, nh, nc_, no, na = len(rows), len(halos), len(consts), len(outs), len(accs)

    def body(*refs):
        i = pl.program_id(0)
        rv = [r[...] for r in refs[:nr]]
        hv = []
        for (idx, kind), r in zip(halos, refs[nr:nr + nh]):
            edge = (i == 0) if kind == "prev" else (i == n - 1)
            hv.append(jnp.where(edge, 0.0, r[...]))
        cv = [r[...] for r in refs[nr + nh:nr + nh + nc_]]
        o_refs = refs[nr + nh + nc_:nr + nh + nc_ + no]
        a_refs = refs[nr + nh + nc_ + no:]
        ov, av = fn(rv, hv, cv)
        for r, v in zip(o_refs, ov):
            r[...] = v.astype(r.dtype)
        if na:
            @pl.when(i == 0)
            def _():
                for r in a_refs:
                    r[...] = jnp.zeros_like(r)
            for r, v in zip(a_refs, av):
                r[...] += v

    in_specs = [pl.BlockSpec((tile, w), functools.partial(lambda i, cb: (i, cb), cb=cb)) for (_, w, cb) in rows]
    for idx, kind in halos:
        _, w, cb = rows[idx]
        if kind == "prev":
            in_specs.append(pl.BlockSpec((8, w), functools.partial(lambda i, cb: (jnp.maximum(i * t8 - 1, 0), cb), cb=cb)))
        else:
            in_specs.append(pl.BlockSpec((8, w), functools.partial(lambda i, cb: (jnp.minimum((i + 1) * t8, n * t8 - 1), cb), cb=cb)))
    in_specs += [pl.BlockSpec(c.shape, functools.partial(lambda i, nd: (0,) * nd, nd=c.ndim)) for c in consts]
    out_specs = [pl.BlockSpec((tile, c), lambda i: (i, 0)) for (c, _) in outs]
    out_specs += [pl.BlockSpec((r, c), lambda i: (0, 0)) for (r, c) in accs]
    out_shape = [jax.ShapeDtypeStruct((s, c), dt) for (c, dt) in outs]
    out_shape += [jax.ShapeDtypeStruct((r, c), F32) for (r, c) in accs]
    args = [r[0] for r in rows] + [rows[idx][0] for idx, _ in halos] + list(consts)
    res = pl.pallas_call(
        body, name=name, grid=(n,), in_specs=in_specs, out_specs=out_specs, out_shape=out_shape,
        compiler_params=_cp("arbitrary"),
    )(*args)
    return list(res)


def _colsum8(v):
    t, c = v.shape
    return jnp.sum(v.reshape(t // 8, 8, c), axis=0)


def _rms(x, g):
    return x * lax.rsqrt(jnp.mean(x * x, axis=-1, keepdims=True) + NORM_EPS) * g


def _rms_fwd(x, g, name):
    def fn(rv, hv, cv):
        return [_rms(rv[0], cv[0])], []
    return _rows(fn, [x], [g.reshape(1, -1)], [(x.shape[1], MXU)], tile=512, name=name)[0]


def _rms_bwd(x, g, dh, dres, name):
    def fn(rv, hv, cv):
        xb, dhb, drb = rv
        _, vjp = jax.vjp(_rms, xb, cv[0])
        dx, _ = vjp(dhb)
        rstd = lax.rsqrt(jnp.mean(xb * xb, axis=-1, keepdims=True) + NORM_EPS)
        return [drb + dx], [_colsum8(dhb * xb * rstd)]
    d = x.shape[1]
    return _rows(fn, [x, dh, dres], [g.reshape(1, -1)], [(d, F32)], [(8, d)], tile=512, name=name)


def _slope_dist(hp, hh, dist, dil):
    hf = (2 * hp + hh + 1).astype(F32)
    slope = jnp.exp(jnp.zeros(dist.shape, F32) - hf * LN2)
    return slope * (dist.astype(F32) * float(dil))


def _att_delta(datt, att, name):
    def fn(rv, hv, cv):
        r = lax.broadcasted_iota(jnp.int32, (ATT_W, ATT_W), 0) // HEAD_DIM
        c = lax.broadcasted_iota(jnp.int32, (ATT_W, ATT_W), 1) // HEAD_DIM
        ones = (r == c).astype(F32)
        return [_nn(rv[0] * rv[1], ones, HI)], []
    return _rows(fn, [datt, att], [], [(ATT_W, F32)], tile=512, name=name)[0]


ATT_G = 2048


def _deinterleave(dst, src, dil, ld, region, offset):
    for r in range(dil):
        rows = pl.ds(r, ld, stride=dil) if dil > 1 else pl.ds(0, ld)
        dst[r * region + offset:r * region + offset + ld, :] = src[rows, :]


def _deinterleave_edge(dst, src, dil, region, offset, first_row):
    for r in range(dil):
        rows = pl.ds(first_row + r, BLK, stride=dil) if dil > 1 else pl.ds(first_row, BLK)
        dst[r * region + offset:r * region + offset + BLK, :] = src[rows, :]


def _att_fwd_fused(proj, name, comm=None):
    s, npc = proj.shape
    gsz = ATT_G
    ng = s // gsz
    assert s % gsz == 0
    scale = HEAD_DIM ** -0.5
    comm = comm or _Comm()

    def body(*refs):
        (q_ref, kp_ref, kc_ref, vp_ref, vc_ref, att_ref, lse_ref, qd, kd, vd, nd, md, dd, nn, mn, dn), cm = comm.split(refs, 5, 2, 9)
        hp, g = pl.program_id(0), pl.program_id(1)
        comm.start_at((hp == 0) & (g == 0), cm)
        lane = lax.broadcasted_iota(jnp.int32, (BLK, BLK), 1)
        qi = lax.broadcasted_iota(jnp.int32, (BLK, 2 * BLK), 0)
        ki = lax.broadcasted_iota(jnp.int32, (BLK, 2 * BLK), 1)
        dist = BLK + qi - ki
        band = (dist >= 0) & (dist <= BLK)
        for pi, (_, dil) in enumerate(ATT_PATTERNS):
            ld = gsz // dil
            nbg = ld // BLK
            _deinterleave(qd, q_ref, dil, ld, ld, 0)
            _deinterleave(kd, kc_ref, dil, ld, ld + BLK, BLK)
            _deinterleave(vd, vc_ref, dil, ld, ld + BLK, BLK)
            _deinterleave_edge(kd, kp_ref, dil, ld + BLK, 0, gsz - BLK * dil)
            _deinterleave_edge(vd, vp_ref, dil, ld + BLK, 0, gsz - BLK * dil)
            bias = [_slope_dist(hp, hh, dist, dil) for hh in (0, 1)]

            def tile(t, carry, ld=ld, nbg=nbg, bias=bias):
                r, b = t // nbg, t % nbg
                qo = pl.multiple_of(r * ld + b * BLK, BLK)
                ko = pl.multiple_of(r * (ld + BLK) + b * BLK, BLK)
                q = qd[pl.ds(qo, BLK), :]
                kk = kd[pl.ds(ko, 2 * BLK), :].astype(MXU)
                vv = vd[pl.ds(ko, 2 * BLK), :].astype(MXU)
                valid = band & ((g > 0) | (b > 0) | (ki >= BLK))
                num = jnp.zeros((BLK, BLK), F32)
                mx = jnp.zeros((BLK, BLK), F32)
                den = jnp.zeros((BLK, BLK), F32)
                for hh in (0, 1):
                    hmask = (lane < HEAD_DIM) if hh == 0 else (lane >= HEAD_DIM)
                    qm = jnp.where(hmask, q, 0.0).astype(MXU)
                    sc = jnp.where(valid, _nt(qm, kk) * scale - bias[hh], NEG)
                    m = jnp.max(sc, axis=1, keepdims=True)
                    p = jnp.exp(sc - m)
                    dn_ = jnp.sum(p, axis=1, keepdims=True)
                    o = _nn(p.astype(MXU), vv)
                    num = jnp.where(hmask, o, num)
                    mx = jnp.where(hmask, m, mx)
                    den = jnp.where(hmask, dn_, den)
                nd[pl.ds(qo, BLK), :] = num
                md[pl.ds(qo, BLK), :] = mx
                dd[pl.ds(qo, BLK), :] = den
                return carry

            lax.fori_loop(0, dil * nbg, tile, 0, unroll=4)
            for r in range(dil):
                rows = pl.ds(r, ld, stride=dil) if dil > 1 else pl.ds(0, ld)
                nn.at[pi][rows, :] = nd[r * ld:(r + 1) * ld, :]
                mn.at[pi][rows, :] = md[r * ld:(r + 1) * ld, :]
                dn.at[pi][rows, :] = dd[r * ld:(r + 1) * ld, :]

        def merge(c, carry):
            rows = pl.ds(pl.multiple_of(c * 256, 256), 256)
            ms = [mn[pi, rows, :] for pi in range(len(ATT_PATTERNS))]
            m_all = functools.reduce(jnp.maximum, ms)
            num = jnp.zeros((256, BLK), F32)
            den = jnp.zeros((256, BLK), F32)
            for pi in range(len(ATT_PATTERNS)):
                e = jnp.exp(ms[pi] - m_all)
                num = num + nn[pi, rows, :] * e
                den = den + dn[pi, rows, :] * e
            att_ref[rows, :] = num / den
            lse_ref[rows, :] = m_all + jnp.log(den)
            return carry

        lax.fori_loop(0, gsz // 256, merge, 0)
        comm.wait_at((hp == 3) & (g == ng - 1), cm)

    def cur(base):
        return pl.BlockSpec((gsz, BLK), lambda hp, g: (g, base // BLK + hp))

    def prev(base):
        return pl.BlockSpec((gsz, BLK), lambda hp, g: (jnp.maximum(g - 1, 0), base // BLK + hp))

    o_spec = pl.BlockSpec((gsz, BLK), lambda hp, g: (g, hp))
    npat = len(ATT_PATTERNS)
    res = pl.pallas_call(
        body, name=name, grid=(4, ng),
        in_specs=[cur(C_Q), prev(C_K), cur(C_K), prev(C_V), cur(C_V)] + [ANY] * comm.n,
        out_specs=[o_spec] * 2 + [ANY] * comm.n,
        out_shape=[jax.ShapeDtypeStruct((s, ATT_W), F32)] * 2 + comm.out_shape(),
        scratch_shapes=[pltpu.VMEM((gsz, BLK), F32), pltpu.VMEM((2 * gsz, BLK), F32), pltpu.VMEM((2 * gsz, BLK), F32)]
        + [pltpu.VMEM((gsz, BLK), F32)] * 3 + [pltpu.VMEM((npat, gsz, BLK), F32)] * 3 + comm.scratch(),
        compiler_params=_cp("arbitrary", "arbitrary"),
    )(proj, proj, proj, proj, proj, *comm.args())
    return res[0], res[1], list(res[2:])


def _att_bwd_fused(proj, datt, lse, delta, name, comm=None):
    s, npc = proj.shape
    gsz = ATT_G
    ng = s // gsz
    scale = HEAD_DIM ** -0.5
    comm = comm or _Comm()

    def body(*refs):
        (qc_ref, qn_ref, kp_ref, kc_ref, vp_ref, vc_ref, doc_ref, don_ref, lsc_ref, lsn_ref, dlc_ref, dln_ref,
         dq_ref, dk_ref, dv_ref, qd, dod, lsd, dld, kd, vd, dqd, dkd, dvd), cm = comm.split(refs, 12, 3, 9)
        hp, g = pl.program_id(0), pl.program_id(1)
        comm.start_at((hp == 0) & (g == 0), cm)
        lane = lax.broadcasted_iota(jnp.int32, (BLK, BLK), 1)
        qi = lax.broadcasted_iota(jnp.int32, (BLK, BLK), 0)
        ki = lax.broadcasted_iota(jnp.int32, (BLK, BLK), 1)
        d_far = BLK + qi - ki
        d_near = qi - ki
        for pi, (_, dil) in enumerate(ATT_PATTERNS):
            ld = gsz // dil
            nbg = ld // BLK
            reg = ld + BLK
            for dst, c_ref, n_ref in ((qd, qc_ref, qn_ref), (dod, doc_ref, don_ref), (lsd, lsc_ref, lsn_ref), (dld, dlc_ref, dln_ref)):
                _deinterleave(dst, c_ref, dil, ld, reg, 0)
                _deinterleave_edge(dst, n_ref, dil, reg, ld, 0)
            for dst, p_ref, c_ref in ((kd, kp_ref, kc_ref), (vd, vp_ref, vc_ref)):
                _deinterleave(dst, c_ref, dil, ld, reg, BLK)
                _deinterleave_edge(dst, p_ref, dil, reg, 0, gsz - BLK * dil)
            b_far = [_slope_dist(hp, hh, d_far, dil) for hh in (0, 1)]
            b_near = [_slope_dist(hp, hh, d_near, dil) for hh in (0, 1)]

            def tile(t, carry, ld=ld, nbg=nbg, reg=reg, b_far=b_far, b_near=b_near):
                r, b = t // nbg, t % nbg
                oo = pl.multiple_of(r * ld + b * BLK, BLK)
                ro = pl.multiple_of(r * reg + b * BLK, BLK)
                qn, qx = qd[pl.ds(ro, BLK), :], qd[pl.ds(ro + BLK, BLK), :]
                don, dox = dod[pl.ds(ro, BLK), :], dod[pl.ds(ro + BLK, BLK), :]
                lsn, lsx = lsd[pl.ds(ro, BLK), :], lsd[pl.ds(ro + BLK, BLK), :]
                dln, dlx = dld[pl.ds(ro, BLK), :], dld[pl.ds(ro + BLK, BLK), :]
                kp, kc = kd[pl.ds(ro, BLK), :].astype(MXU), kd[pl.ds(ro + BLK, BLK), :].astype(MXU)
                vp, vc = vd[pl.ds(ro, BLK), :].astype(MXU), vd[pl.ds(ro + BLK, BLK), :].astype(MXU)
                ok_a = (d_far <= BLK) & ((g > 0) | (b > 0))
                ok_b = d_near >= 0
                ok_c = (d_far <= BLK) & ((g < ng - 1) | (b < nbg - 1))

                def grads(qm, dom, k, v, ls, dl, bias, valid, hh):
                    c0 = hh * HEAD_DIM
                    sc = _nt(qm, k) * scale - bias
                    p = jnp.exp(jnp.where(valid, sc - ls[:, c0:c0 + 1], NEG))
                    ds = p * (_nt(dom, v) - dl[:, c0:c0 + 1])
                    return p.astype(MXU), ds.astype(MXU)

                dq = jnp.zeros((BLK, BLK), F32)
                dk = jnp.zeros((BLK, BLK), F32)
                dv = jnp.zeros((BLK, BLK), F32)
                for hh in (0, 1):
                    hmask = (lane < HEAD_DIM) if hh == 0 else (lane >= HEAD_DIM)
                    qnm = jnp.where(hmask, qn, 0.0).astype(MXU)
                    qxm = jnp.where(hmask, qx, 0.0).astype(MXU)
                    donm = jnp.where(hmask, don, 0.0).astype(MXU)
                    doxm = jnp.where(hmask, dox, 0.0).astype(MXU)
                    _, ds_a = grads(qnm, donm, kp, vp, lsn, dln, b_far[hh], ok_a, hh)
                    p_b, ds_b = grads(qnm, donm, kc, vc, lsn, dln, b_near[hh], ok_b, hh)
                    p_c, ds_c = grads(qxm, doxm, kc, vc, lsx, dlx, b_far[hh], ok_c, hh)
                    dq = jnp.where(hmask, _nn(ds_a, kp) + _nn(ds_b, kc), dq)
                    dk = dk + _tn(ds_b, qnm) + _tn(ds_c, qxm)
                    dv = dv + _tn(p_b, donm) + _tn(p_c, doxm)
                dqd[pl.ds(oo, BLK), :] = dq * scale
                dkd[pl.ds(oo, BLK), :] = dk * scale
                dvd[pl.ds(oo, BLK), :] = dv
                return carry

            lax.fori_loop(0, dil * nbg, tile, 0, unroll=2)
            for out, src in ((dq_ref, dqd), (dk_ref, dkd), (dv_ref, dvd)):
                for r in range(dil):
                    rows = pl.ds(r, ld, stride=dil) if dil > 1 else pl.ds(0, ld)
                    if pi == 0:
                        out[rows, :] = src[r * ld:(r + 1) * ld, :]
                    else:
                        out[rows, :] = out[rows, :] + src[r * ld:(r + 1) * ld, :]
        comm.wait_at((hp == 3) & (g == ng - 1), cm)

    def pspec(base, shift):
        return pl.BlockSpec((gsz, BLK), lambda hp, g: (jnp.clip(g + shift, 0, ng - 1), base // BLK + hp))

    def wspec(shift):
        return pl.BlockSpec((gsz, BLK), lambda hp, g: (jnp.clip(g + shift, 0, ng - 1), hp))

    in_specs = [pspec(C_Q, 0), pspec(C_Q, 1), pspec(C_K, -1), pspec(C_K, 0), pspec(C_V, -1), pspec(C_V, 0),
                wspec(0), wspec(1), wspec(0), wspec(1), wspec(0), wspec(1)] + [ANY] * comm.n
    res = pl.pallas_call(
        body, name=name, grid=(4, ng), in_specs=in_specs,
        out_specs=[wspec(0)] * 3 + [ANY] * comm.n,
        out_shape=[jax.ShapeDtypeStruct((s, ATT_W), F32)] * 3 + comm.out_shape(),
        scratch_shapes=[pltpu.VMEM((2 * gsz, BLK), F32)] * 6 + [pltpu.VMEM((gsz, BLK), F32)] * 3 + comm.scratch(),
        compiler_params=_cp("arbitrary", "arbitrary"),
    )(proj, proj, proj, proj, proj, proj, datt, datt, lse, lse, delta, delta, *comm.args())
    return res[0], res[1], res[2], list(res[3:])


def _shift_down(cur, halo, sft):
    if sft == 0:
        return cur
    t = cur.shape[0]
    rolled = pltpu.roll(cur, sft, 0)
    hr = pltpu.roll(halo, sft, 0)
    row = lax.broadcasted_iota(jnp.int32, cur.shape, 0)
    return jnp.where(row < sft, jnp.tile(hr, (t // 8, 1)), rolled)


def _shift_up(cur, halo, sft):
    if sft == 0:
        return cur
    t = cur.shape[0]
    rolled = pltpu.roll(cur, t - sft, 0)
    hr = pltpu.roll(halo, 8 - sft, 0)
    row = lax.broadcasted_iota(jnp.int32, cur.shape, 0)
    return jnp.where(row >= t - sft, jnp.tile(hr, (t // 8, 1)), rolled)


def _conv(x, xh, w, b):
    y = b + x * w[CONV_K - 1:CONV_K]
    for k in range(CONV_K - 1):
        y = y + _shift_down(x, xh, CONV_K - 1 - k) * w[k:k + 1]
    return y


def _conv_bwd(x, xh, dy, dyh, w):
    dx = dy * w[CONV_K - 1:CONV_K]
    dws = []
    for k in range(CONV_K - 1):
        sft = CONV_K - 1 - k
        dx = dx + _shift_up(dy, dyh, sft) * w[k:k + 1]
        dws.append(jnp.sum(dy * _shift_down(x, xh, sft), axis=0, keepdims=True))
    dws.append(jnp.sum(dy * x, axis=0, keepdims=True))
    c = x.shape[1]
    dw = jnp.concatenate(dws + [jnp.zeros((8 - CONV_K, c), F32)], axis=0)
    return dx, dw, jnp.sum(dy, axis=0, keepdims=True)


def _pad8(w):
    return jnp.concatenate([w, jnp.zeros((8 - w.shape[0], w.shape[1]), w.dtype)], axis=0)


def _ssd_pre(proj, conv_w, conv_b, dt_bias128, name):
    def fn(rv, hv, cv):
        xbc, dtr = rv
        return [_silu(_conv(xbc, hv[0], cv[0], cv[1])), _softplus(dtr + cv[2])], []
    return _rows(fn, [(proj, 1024, C_XBC // 1024), (proj, BLK, C_DT // BLK)],
                 [_pad8(conv_w), conv_b.reshape(1, -1), dt_bias128],
                 [(1024, F32), (BLK, F32)], tile=256, name=name, halos=[(0, "prev")])


def _ssd_pre_bwd(proj, dxc, ddt, conv_w, conv_b, dt_bias128, name):
    def fn(rv, hv, cv):
        xbc, dtr, dxcb, ddtb = rv
        xh, dxch_raw, xnext = hv
        w, b, bias = cv
        pre = _conv(xbc, xh, w, b)
        sg = _sigmoid(pre)
        dpre = dxcb * (sg * (1.0 + pre * (1.0 - sg)))
        t = xbc.shape[0]
        tail = jnp.concatenate([xbc[t - 8:], xnext], axis=0)
        pre_n = _conv(tail[8:], tail[:8], w, b)
        sgn = _sigmoid(pre_n)
        dpre_h = dxch_raw * (sgn * (1.0 + pre_n * (1.0 - sgn)))
        dx, dw, db = _conv_bwd(xbc, xh, dpre, dpre_h, w)
        ddr = ddtb * _sigmoid(dtr + bias)
        return [dx, ddr], [dw, jnp.concatenate([db, jnp.zeros((7, db.shape[1]), F32)], axis=0), _colsum8(ddr)]
    return _rows(fn, [(proj, 1024, C_XBC // 1024), (proj, BLK, C_DT // BLK), dxc, ddt],
                 [_pad8(conv_w), conv_b.reshape(1, -1), dt_bias128],
                 [(1024, F32), (BLK, F32)], [(8, 1024), (8, 1024), (8, BLK)], tile=256, name=name,
                 halos=[(0, "prev"), (2, "next"), (0, "next")])


def _head_cols(v, h0):
    lane = lax.broadcasted_iota(jnp.int32, (v.shape[0], BLK), 1)
    return jnp.where(lane < HEAD_DIM, v[:, h0:h0 + 1], v[:, h0 + 1:h0 + 2])


def _ssd_scan(xc, dt, par, name):
    s = xc.shape[0]
    nc = s // BLK

    def body(x_ref, dt_ref, par_ref, y_ref, st_ref, h_ref):
        c = pl.program_id(0)

        @pl.when(c == 0)
        def _():
            h_ref[...] = jnp.zeros_like(h_ref)

        st_ref[0] = h_ref[...]
        dt = dt_ref[...]
        a_row = -jnp.exp(par_ref[0:1, :])
        d_row = par_ref[1:2, :]
        ri = lax.broadcasted_iota(jnp.int32, (BLK, BLK), 0)
        ci = lax.broadcasted_iota(jnp.int32, (BLK, BLK), 1)
        tril = ri >= ci
        cs = _nn(tril.astype(F32), dt * a_row, HI)
        cst, dtt = cs.T, dt.T
        last = cs[BLK - 1:BLK, :]
        wcol = jnp.exp(last - cs) * dt
        ecs = jnp.exp(cs)
        elast = jnp.exp(last)
        for g in (0, 1):
            bg = x_ref[:, 512 + g * BLK:512 + (g + 1) * BLK].astype(MXU)
            cg = x_ref[:, 768 + g * BLK:768 + (g + 1) * BLK].astype(MXU)
            gm = _nt(cg, bg)
            for pp in (0, 1):
                pr = 2 * g + pp
                h0 = 2 * pr
                x2 = x_ref[:, pr * BLK:(pr + 1) * BLK]
                hprev = h_ref[pr * BLK:(pr + 1) * BLK, :]
                yp = jnp.zeros((BLK, BLK), F32)
                for hh in (0, 1):
                    h = h0 + hh
                    hmask = (ci < HEAD_DIM) if hh == 0 else (ci >= HEAD_DIM)
                    lm = jnp.exp(jnp.where(tril, cs[:, h:h + 1] - cst[h:h + 1, :], NEG))
                    mm = gm * lm * dtt[h:h + 1, :]
                    yp = yp + _nn(mm.astype(MXU), jnp.where(hmask, x2, 0.0).astype(MXU))
                y0 = _nt(cg, hprev.astype(MXU))
                y_ref[:, pr * BLK:(pr + 1) * BLK] = yp + _head_cols(ecs, h0) * y0 + _head_cols(d_row, h0) * x2
                dec = jnp.where(ri < HEAD_DIM, elast[:, h0:h0 + 1], elast[:, h0 + 1:h0 + 2])
                xw = (x2 * _head_cols(wcol, h0)).astype(MXU)
                h_ref[pr * BLK:(pr + 1) * BLK, :] = dec * hprev + _tn(xw, bg)

    return pl.pallas_call(
        body, name=name, grid=(nc,),
        in_specs=[pl.BlockSpec((BLK, 1024), lambda c: (c, 0)), pl.BlockSpec((BLK, BLK), lambda c: (c, 0)),
                  pl.BlockSpec((8, BLK), lambda c: (0, 0))],
        out_specs=[pl.BlockSpec((BLK, SSD_W), lambda c: (c, 0)), pl.BlockSpec((1, SSD_W, SSD_STATE), lambda c: (c, 0, 0))],
        out_shape=[jax.ShapeDtypeStruct((s, SSD_W), F32), jax.ShapeDtypeStruct((nc, SSD_W, SSD_STATE), F32)],
        scratch_shapes=[pltpu.VMEM((SSD_W, SSD_STATE), F32)],
        compiler_params=_cp("arbitrary"),
    )(xc, dt, par)


def _ssd_scan_bwd(xc, dt, par, st, dy, name, comm=None):
    s = xc.shape[0]
    nc = s // BLK
    comm = comm or _Comm()

    def body(*refs):
        (x_ref, dt_ref, par_ref, st_ref, dy_ref, dx_ref, ddt_ref, dal_ref, dd_ref, dh_ref), cm = comm.split(refs, 5, 4, 1)
        c = pl.program_id(0)
        comm.start_at(c == 0, cm)

        @pl.when(c == 0)
        def _():
            dh_ref[...] = jnp.zeros_like(dh_ref)
            dal_ref[...] = jnp.zeros_like(dal_ref)
            dd_ref[...] = jnp.zeros_like(dd_ref)

        dt = dt_ref[...]
        a_row = -jnp.exp(par_ref[0:1, :])
        d_row = par_ref[1:2, :]
        ri = lax.broadcasted_iota(jnp.int32, (BLK, BLK), 0)
        ci = lax.broadcasted_iota(jnp.int32, (BLK, BLK), 1)
        tril = ri >= ci
        cs = _nn(tril.astype(F32), dt * a_row, HI)
        cst, dtt = cs.T, dt.T
        last = cs[BLK - 1:BLK, :]
        tolast = jnp.exp(last - cs)
        wcol = tolast * dt
        ecs = jnp.exp(cs)
        elast = jnp.exp(last)
        dcs_col = jnp.zeros((BLK, BLK), F32)
        ddt_col = jnp.zeros((BLK, BLK), F32)
        dcs_row = jnp.zeros((BLK, BLK), F32)
        ddt_row = jnp.zeros((BLK, BLK), F32)
        dlast = jnp.zeros((1, BLK), F32)
        ddsk = jnp.zeros((1, BLK), F32)
        for g in (0, 1):
            bg32 = x_ref[:, 512 + g * BLK:512 + (g + 1) * BLK]
            cg32 = x_ref[:, 768 + g * BLK:768 + (g + 1) * BLK]
            bg, cg = bg32.astype(MXU), cg32.astype(MXU)
            gm = _nt(cg, bg)
            dgm = jnp.zeros((BLK, BLK), F32)
            dbg = jnp.zeros((BLK, BLK), F32)
            dcg = jnp.zeros((BLK, BLK), F32)
            for pp in (0, 1):
                pr = 2 * g + pp
                h0 = 2 * pr
                x2 = x_ref[:, pr * BLK:(pr + 1) * BLK]
                dy2 = dy_ref[:, pr * BLK:(pr + 1) * BLK]
                hprev = st_ref[0, pr * BLK:(pr + 1) * BLK, :]
                dhn = dh_ref[pr * BLK:(pr + 1) * BLK, :]
                x2m, dhnm = x2.astype(MXU), dhn.astype(MXU)
                zb = _nt(bg, dhnm)
                y0 = _nt(cg, hprev.astype(MXU))
                esel = _head_cols(ecs, h0)
                wsel = _head_cols(wcol, h0)
                dx2 = _head_cols(d_row, h0) * dy2 + wsel * zb
                r_off = dy2 * y0
                r_w = x2 * zb
                r_d = dy2 * x2
                r_h = dhn * hprev
                for hh in (0, 1):
                    h = h0 + hh
                    hmask = (ci < HEAD_DIM) if hh == 0 else (ci >= HEAD_DIM)
                    onl = (ci == h).astype(F32)
                    ons = (ri == h).astype(F32)
                    dym = jnp.where(hmask, dy2, 0.0).astype(MXU)
                    dt_r = dtt[h:h + 1, :]
                    lm = jnp.exp(jnp.where(tril, cs[:, h:h + 1] - cst[h:h + 1, :], NEG))
                    mm = gm * lm * dt_r
                    dx2 = dx2 + _tn(mm.astype(MXU), dym)
                    dm = _nt(dym, x2m)
                    t1 = dm * lm
                    dgm = dgm + t1 * dt_r
                    tt = t1 * gm
                    ddt_row = ddt_row + ons * jnp.sum(tt, axis=0, keepdims=True)
                    t = tt * dt_r
                    dcs_col = dcs_col + onl * jnp.sum(t, axis=1, keepdims=True)
                    dcs_row = dcs_row - ons * jnp.sum(t, axis=0, keepdims=True)
                    de = jnp.sum(jnp.where(hmask, r_off, 0.0), axis=1, keepdims=True)
                    dcs_col = dcs_col + onl * (ecs[:, h:h + 1] * de)
                    hrow = (ri < HEAD_DIM) if hh == 0 else (ri >= HEAD_DIM)
                    dl_h = elast[:, h:h + 1] * jnp.sum(jnp.where(hrow, r_h, 0.0), keepdims=True)
                    dw = jnp.sum(jnp.where(hmask, r_w, 0.0), axis=1, keepdims=True)
                    ddt_col = ddt_col + onl * (dw * tolast[:, h:h + 1])
                    v = dw * wcol[:, h:h + 1]
                    dcs_col = dcs_col - onl * v
                    dl_h = dl_h + jnp.sum(v, keepdims=True)
                    dlast = dlast + onl[0:1, :] * dl_h
                    ddsk = ddsk + onl[0:1, :] * jnp.sum(jnp.where(hmask, r_d, 0.0), keepdims=True)
                dx_ref[:, pr * BLK:(pr + 1) * BLK] = dx2
                edy = (esel * dy2).astype(MXU)
                dcg = dcg + _nn(edy, hprev.astype(MXU))
                dec = jnp.where(ri < HEAD_DIM, elast[:, h0:h0 + 1], elast[:, h0 + 1:h0 + 2])
                dh_ref[pr * BLK:(pr + 1) * BLK, :] = dec * dhn + _tn(edy, cg)
                dbg = dbg + _nn((x2 * wsel).astype(MXU), dhnm)
            dgmm = dgm.astype(MXU)
            dx_ref[:, 512 + g * BLK:512 + (g + 1) * BLK] = dbg + _tn(dgmm, cg)
            dx_ref[:, 768 + g * BLK:768 + (g + 1) * BLK] = dcg + _nn(dgmm, bg)
        dcs = dcs_col + dcs_row.T + jnp.where(ri == BLK - 1, dlast, 0.0)
        dda = _nn((ri <= ci).astype(F32), dcs, HI)
        ddt_ref[...] = ddt_col + ddt_row.T + a_row * dda
        da = jnp.sum(dt * dda, axis=0, keepdims=True)
        dal_ref[0:1, :] += da * a_row
        dd_ref[0:1, :] += ddsk
        comm.wait_at(c == nc - 1, cm)

    rev = lambda c: (nc - 1 - c, 0)
    res = pl.pallas_call(
        body, name=name, grid=(nc,),
        in_specs=[pl.BlockSpec((BLK, 1024), rev), pl.BlockSpec((BLK, BLK), rev), pl.BlockSpec((8, BLK), lambda c: (0, 0)),
                  pl.BlockSpec((1, SSD_W, SSD_STATE), lambda c: (nc - 1 - c, 0, 0)), pl.BlockSpec((BLK, SSD_W), rev)]
        + [ANY] * comm.n,
        out_specs=[pl.BlockSpec((BLK, 1024), rev), pl.BlockSpec((BLK, BLK), rev),
                   pl.BlockSpec((8, BLK), lambda c: (0, 0)), pl.BlockSpec((8, BLK), lambda c: (0, 0))] + [ANY] * comm.n,
        out_shape=[jax.ShapeDtypeStruct((s, 1024), F32), jax.ShapeDtypeStruct((s, BLK), F32),
                   jax.ShapeDtypeStruct((8, BLK), F32), jax.ShapeDtypeStruct((8, BLK), F32)] + comm.out_shape(),
        scratch_shapes=[pltpu.VMEM((SSD_W, SSD_STATE), F32)] + comm.scratch(),
        compiler_params=_cp("arbitrary"),
    )(xc, dt, par, st, dy, *comm.args())
    return res[0], res[1], res[2], res[3], list(res[4:])


def _ssd_gate(y, z, w):
    t = y * _silu(z)
    outs = []
    for g in (0, 1):
        tg = t[:, g * 256:(g + 1) * 256]
        outs.append(tg * lax.rsqrt(jnp.mean(tg * tg, axis=-1, keepdims=True) + SSD_NORM_EPS))
    return jnp.concatenate(outs, axis=1) * w


def _ssd_post(y, proj, norm_w, name):
    def fn(rv, hv, cv):
        return [_ssd_gate(rv[0], rv[1], cv[0])], []
    return _rows(fn, [y, (proj, SSD_W, C_Z // SSD_W)], [norm_w.reshape(1, -1)], [(SSD_W, F32)], tile=512, name=name)[0]


def _ssd_post_bwd(y, proj, norm_w, dout, name):
    def fn(rv, hv, cv):
        yb, zb, db = rv
        _, vjp = jax.vjp(lambda a, b: _ssd_gate(a, b, cv[0]), yb, zb)
        dy, dz = vjp(db)
        t = yb * _silu(zb)
        nrm = []
        for g in (0, 1):
            tg = t[:, g * 256:(g + 1) * 256]
            nrm.append(tg * lax.rsqrt(jnp.mean(tg * tg, axis=-1, keepdims=True) + SSD_NORM_EPS))
        return [dy, dz], [_colsum8(db * jnp.concatenate(nrm, axis=1))]
    return _rows(fn, [y, (proj, SSD_W, C_Z // SSD_W), dout], [norm_w.reshape(1, -1)],
                 [(SSD_W, F32), (SSD_W, F32)], [(8, SSD_W)], tile=512, name=name)


LRU_T = 256


def _lru_conv(proj, conv_w, conv_b, name):
    def fn(rv, hv, cv):
        return [_conv(rv[0], hv[0], cv[0], cv[1])], []
    return _rows(fn, [(proj, LRU_W, C_XL // LRU_W)], [_pad8(conv_w), conv_b.reshape(1, -1)], [(LRU_W, F32)],
                 tile=512, name=name, halos=[(0, "prev")])[0]


def _lru_conv_bwd(proj, dxc, conv_w, name):
    def fn(rv, hv, cv):
        dx, dw, db = _conv_bwd(rv[0], hv[0], rv[1], hv[1], cv[0])
        return [dx], [dw, jnp.concatenate([db, jnp.zeros((7, db.shape[1]), F32)], axis=0)]
    return _rows(fn, [(proj, LRU_W, C_XL // LRU_W), dxc], [_pad8(conv_w)], [(LRU_W, F32)], [(8, LRU_W), (8, LRU_W)],
                 tile=512, name=name, halos=[(0, "prev"), (1, "next")])


def _lru_au(pre_a, pre_x, xc, ba, bx, lam):
    r = _sigmoid(pre_a + ba)
    i = _sigmoid(pre_x + bx)
    log_a = -LRU_C * r * _softplus(-lam)
    a = jnp.exp(log_a)
    u = jnp.sqrt(1.0 - jnp.exp(2.0 * log_a)) * (i * xc)
    return a, u


def _lru_scan(pre, xc, proj, par, name):
    s = xc.shape[0]
    t = LRU_T

    def body(pre_ref, xc_ref, g_ref, par_ref, out_ref, h_ref, carry):
        c = pl.program_id(0)

        @pl.when(c == 0)
        def _():
            carry[...] = jnp.zeros_like(carry)

        a, u = _lru_au(pre_ref[:, :LRU_W], pre_ref[:, LRU_W:], xc_ref[...], par_ref[0:1, :], par_ref[1:2, :], par_ref[2:3, :])
        row = lax.broadcasted_iota(jnp.int32, (t, LRU_W), 0)
        sft = 1
        while sft < t:
            keep = row >= sft
            a_s = jnp.where(keep, pltpu.roll(a, sft, 0), 1.0)
            u_s = jnp.where(keep, pltpu.roll(u, sft, 0), 0.0)
            u = a * u_s + u
            a = a * a_s
            sft *= 2
        h = a * carry[0:1, :] + u
        h_ref[...] = h
        out_ref[...] = h * _gelu(g_ref[...])
        carry[0:1, :] = h[t - 1:t, :]

    return pl.pallas_call(
        body, name=name, grid=(s // t,),
        in_specs=[pl.BlockSpec((t, 2 * LRU_W), lambda c: (c, 0)), pl.BlockSpec((t, LRU_W), lambda c: (c, 0)),
                  pl.BlockSpec((t, LRU_W), lambda c: (c, C_G // LRU_W)), pl.BlockSpec((8, LRU_W), lambda c: (0, 0))],
        out_specs=[pl.BlockSpec((t, LRU_W), lambda c: (c, 0))] * 2,
        out_shape=[jax.ShapeDtypeStruct((s, LRU_W), F32)] * 2,
        scratch_shapes=[pltpu.VMEM((8, LRU_W), F32)],
        compiler_params=_cp("arbitrary"),
    )(pre, xc, proj, par)


def _lru_scan_bwd(pre, xc, proj, par, h, dout, name):
    s = xc.shape[0]
    t = LRU_T
    n = s // t
    t8 = t // 8

    def body(pre_ref, xc_ref, g_ref, par_ref, h_ref, hh_ref, do_ref, dpre_ref, dxc_ref, dg_ref, dpar_ref, carry):
        c = pl.program_id(0)

        @pl.when(c == 0)
        def _():
            carry[...] = jnp.zeros_like(carry)
            dpar_ref[...] = jnp.zeros_like(dpar_ref)

        pa, px, xcb = pre_ref[:, :LRU_W], pre_ref[:, LRU_W:], xc_ref[...]
        ba, bx, lam = par_ref[0:1, :], par_ref[1:2, :], par_ref[2:3, :]
        (a, u), vjp = jax.vjp(_lru_au, pa, px, xcb, ba, bx, lam)
        g = g_ref[...]
        hcur = h_ref[...]
        do = do_ref[...]
        _, gvjp = jax.vjp(_gelu, g)
        dg_ref[...] = gvjp(do * hcur)[0]
        row = lax.broadcasted_iota(jnp.int32, (t, LRU_W), 0)
        v = do * _gelu(g) + jnp.where(row == t - 1, carry[0:1, :], 0.0)
        b = jnp.where(row == t - 1, 0.0, pltpu.roll(a, t - 1, 0))
        sft = 1
        while sft < t:
            keep = row < t - sft
            b_s = jnp.where(keep, pltpu.roll(b, t - sft, 0), 1.0)
            v_s = jnp.where(keep, pltpu.roll(v, t - sft, 0), 0.0)
            v = b * v_s + v
            b = b * b_s
            sft *= 2
        dh = v
        carry[0:1, :] = a[0:1, :] * dh[0:1, :]
        hhalo = jnp.where(c == n - 1, 0.0, hh_ref[...])
        hprev = _shift_down(hcur, hhalo, 1)
        dpa, dpx, dxc, dba, dbx, dlam = vjp((dh * hprev, dh))
        dpre_ref[:, :LRU_W] = dpa
        dpre_ref[:, LRU_W:] = dpx
        dxc_ref[...] = dxc
        dpar_ref[0:1, :] += dba
        dpar_ref[1:2, :] += dbx
        dpar_ref[2:3, :] += dlam

    rev = lambda c: (n - 1 - c, 0)
    return pl.pallas_call(
        body, name=name, grid=(n,),
        in_specs=[pl.BlockSpec((t, 2 * LRU_W), rev), pl.BlockSpec((t, LRU_W), rev),
                  pl.BlockSpec((t, LRU_W), lambda c: (n - 1 - c, C_G // LRU_W)), pl.BlockSpec((8, LRU_W), lambda c: (0, 0)),
                  pl.BlockSpec((t, LRU_W), rev),
                  pl.BlockSpec((8, LRU_W), lambda c: (jnp.maximum((n - 1 - c) * t8 - 1, 0), 0)),
                  pl.BlockSpec((t, LRU_W), rev)],
        out_specs=[pl.BlockSpec((t, 2 * LRU_W), rev), pl.BlockSpec((t, LRU_W), rev), pl.BlockSpec((t, LRU_W), rev),
                   pl.BlockSpec((8, LRU_W), lambda c: (0, 0))],
        out_shape=[jax.ShapeDtypeStruct((s, 2 * LRU_W), F32), jax.ShapeDtypeStruct((s, LRU_W), F32),
                   jax.ShapeDtypeStruct((s, LRU_W), F32), jax.ShapeDtypeStruct((8, LRU_W), F32)],
        scratch_shapes=[pltpu.VMEM((8, LRU_W), F32)],
        compiler_params=_cp("arbitrary"),
    )(pre, xc, proj, par, h, h, dout)


def _swiglu_act(gu, name):
    def fn(rv, hv, cv):
        return [_silu(rv[0]) * rv[1]], []
    return _rows(fn, [(gu, D_FF, 0), (gu, D_FF, 1)], [], [(D_FF, MXU)], tile=256, name=name)[0]


def _swiglu_bwd(gu, da, name):
    def fn(rv, hv, cv):
        gt, up, dab = rv
        sg = _sigmoid(gt)
        dgate = dab * up * (sg * (1.0 + gt * (1.0 - sg)))
        dup = dab * (gt * sg)
        return [jnp.concatenate([dgate, dup], axis=1)], []
    return _rows(fn, [(gu, D_FF, 0), (gu, D_FF, 1), da], [], [(2 * D_FF, MXU)], tile=256, name=name)[0]


def _loss_head(x, g, target, name):
    d = x.shape[1]

    def fn(rv, hv, cv):
        xb, tb = rv
        y, vjp = jax.vjp(_rms, xb, cv[0])
        err = y - tb
        dy = err * (1.0 / d)
        dx, _ = vjp(dy)
        rstd = lax.rsqrt(jnp.mean(xb * xb, axis=-1, keepdims=True) + NORM_EPS)
        e2 = err * err * (0.5 / d)
        e2 = functools.reduce(lambda a, b: a + b, [e2[:, k * BLK:(k + 1) * BLK] for k in range(d // BLK)])
        return [dx], [_colsum8(dy * xb * rstd), _colsum8(e2)]
    return _rows(fn, [x, target], [g.reshape(1, -1)], [(d, F32)], [(8, d), (8, BLK)], tile=512, name=name)


ANY = pl.BlockSpec(memory_space=pl.ANY)


def _coords():
    return lax.axis_index("x"), lax.axis_index("y"), lax.axis_index("c")


class _Comm:
    def __init__(self, gathers=(), scatters=()):
        self.gathers = list(gathers)
        self.scatters = list(scatters)
        self.n = len(self.gathers) + len(self.scatters)

    def args(self):
        return [g[0] for g in self.gathers] + self.scatters

    def out_shape(self):
        out = [jax.ShapeDtypeStruct((4,) + (a.shape if l is None else a.shape[1:]), a.dtype) for a, l, _ in self.gathers]
        return out + [jax.ShapeDtypeStruct((3,) + a.shape[1:], a.dtype) for a in self.scatters]

    def scratch(self):
        if not self.n:
            return []
        return [pltpu.SemaphoreType.DMA((3 * self.n,)), pltpu.SemaphoreType.DMA((3 * self.n,)),
                pltpu.SemaphoreType.DMA((max(len(self.gathers), 1),)),
                pltpu.SemaphoreType.DMA((3 * self.n,)), pltpu.SemaphoreType.DMA((3 * self.n,))]

    def split(self, refs, n_in, n_out, n_scratch):
        refs = list(refs)
        n = self.n
        own = refs[:n_in] + refs[n_in + n:n_in + n + n_out] + refs[n_in + 2 * n + n_out:n_in + 2 * n + n_out + n_scratch]
        cm = (refs[n_in:n_in + n], refs[n_in + n + n_out:n_in + 2 * n + n_out], refs[n_in + 2 * n + n_out + n_scratch:])
        return own, cm

    def _copies(self, cm, arriving):
        ins, outs, (send, recv, local, _, _) = cm
        x, y, c = _coords()
        me = 2 * x + y
        chips = [(1 - x, y), (x, 1 - y), (1 - x, 1 - y)]
        remote, locals_ = [], []
        ng = len(self.gathers)
        for i in range(self.n):
            if i < ng:
                _, l, halved = self.gathers[i]
                slab = ins[i] if l is None else ins[i].at[l]
                if not arriving:
                    locals_.append(pltpu.make_async_copy(slab, outs[i].at[me], local.at[i]))
            for j, (px, py) in enumerate(chips):
                if i < ng:
                    slot = 2 * px + py if arriving else me
                    src, dst = (slab.at[c], outs[i].at[slot, c]) if halved else (slab, outs[i].at[slot])
                else:
                    src, dst = ins[i].at[2 * px + py], outs[i].at[j]
                remote.append(pltpu.make_async_remote_copy(src, dst, send.at[3 * i + j], recv.at[3 * i + j],
                                                           device_id=(px, py, c), device_id_type=MESH))
        return remote, locals_

    def _handovers(self, cm, arriving):
        _, outs, (_, _, _, send, recv) = cm
        x, y, c = _coords()
        chips = [(1 - x, y), (x, 1 - y), (1 - x, 1 - y)]
        cps = []
        for i, (_, _, halved) in enumerate(self.gathers):
            if halved:
                for j, (px, py) in enumerate(chips):
                    src = outs[i].at[2 * px + py, c]
                    dst = outs[i].at[2 * px + py, 1 - c if arriving else c]
                    cps.append(pltpu.make_async_remote_copy(src, dst, send.at[3 * i + j], recv.at[3 * i + j],
                                                            device_id=(x, y, 1 - c), device_id_type=MESH))
        return cps

    def start_at(self, cond, cm):
        def go():
            remote, locals_ = self._copies(cm, False)
            for cp in locals_ + remote:
                cp.start()

        if self.n:
            go() if cond is True else pl.when(cond)(go)

    def wait_at(self, cond, cm):
        def go():
            for cp in self._copies(cm, True)[0]:
                cp.wait_recv()
            handed = self._handovers(cm, False)
            for cp in handed:
                cp.start()
            for cp in self._handovers(cm, True):
                cp.wait_recv()
            remote, locals_ = self._copies(cm, False)
            for cp in handed + remote:
                cp.wait_send()
            for cp in locals_:
                cp.wait()

        if self.n:
            go() if cond is True else pl.when(cond)(go)


def _comm_call(comm, name):
    def body(*refs):
        _, cm = comm.split(refs, 0, 0, 0)
        comm.start_at(True, cm)
        comm.wait_at(True, cm)

    return list(pl.pallas_call(
        body, name=name, in_specs=[ANY] * comm.n, out_specs=[ANY] * comm.n, out_shape=comm.out_shape(),
        scratch_shapes=comm.scratch(), compiler_params=pltpu.CompilerParams(has_side_effects=True),
    )(*comm.args()))


def _swap_sibling(arrs):
    n = len(arrs)

    def body(*refs):
        ins, outs, send, recv = refs[:n], refs[n:2 * n], refs[2 * n], refs[2 * n + 1]
        x, y, c = _coords()
        cps = [pltpu.make_async_remote_copy(ins[i], outs[i], send.at[i], recv.at[i], device_id=(x, y, 1 - c), device_id_type=MESH)
               for i in range(n)]
        for cp in cps:
            cp.start()
        for cp in cps:
            cp.wait_recv()
        for cp in cps:
            cp.wait_send()

    return list(pl.pallas_call(
        body, name="swap_sibling", in_specs=[ANY] * n, out_specs=[ANY] * n,
        out_shape=[jax.ShapeDtypeStruct(a.shape, a.dtype) for a in arrs],
        scratch_shapes=[pltpu.SemaphoreType.DMA((n,)), pltpu.SemaphoreType.DMA((n,))],
        compiler_params=pltpu.CompilerParams(has_side_effects=True),
    )(*arrs))


def _gather_small(gs):
    def body(g_ref, o_ref, send_sems, recv_sems, local_sem):
        x, y, c = _coords()
        me = 4 * x + 2 * y + c
        mine = pltpu.make_async_copy(g_ref, o_ref.at[me], local_sem)
        mine.start()
        sends = []
        for k in range(1, 8):
            px, py, pc = x ^ (k >> 2), y ^ ((k >> 1) & 1), c ^ (k & 1)
            sends.append((pltpu.make_async_remote_copy(g_ref, o_ref.at[me], send_sems.at[k - 1], recv_sems.at[k - 1],
                                                       device_id=(px, py, pc), device_id_type=MESH), 4 * px + 2 * py + pc, k))
        for cp, _, _ in sends:
            cp.start()
        for cp, src, k in sends:
            pltpu.make_async_remote_copy(g_ref, o_ref.at[src], send_sems.at[k - 1], recv_sems.at[k - 1],
                                         device_id=(x, y, c), device_id_type=MESH).wait_recv()
        for cp, _, _ in sends:
            cp.wait_send()
        mine.wait()

    return pl.pallas_call(
        body, name="gather_small", in_specs=[ANY], out_specs=ANY,
        out_shape=jax.ShapeDtypeStruct((8,) + gs.shape, gs.dtype),
        scratch_shapes=[pltpu.SemaphoreType.DMA((7,)), pltpu.SemaphoreType.DMA((7,)), pltpu.SemaphoreType.DMA],
        compiler_params=pltpu.CompilerParams(has_side_effects=True),
    )(gs)


def _sum_slots(own, others, name, tile):
    k, r, c = others.shape

    def body(*refs):
        if own is None:
            o_ref, out_ref = refs
            acc = o_ref[0].astype(F32)
            first = 1
        else:
            own_ref, o_ref, out_ref = refs
            acc = own_ref[...]
            first = 0
        for j in range(first, k):
            acc = acc + o_ref[j].astype(F32)
        out_ref[...] = acc

    row = pl.BlockSpec((tile, c), lambda i: (i, 0))
    specs = ([] if own is None else [row]) + [pl.BlockSpec((k, tile, c), lambda i: (0, i, 0))]
    args = ([] if own is None else [own]) + [others]
    return pl.pallas_call(body, name=name, grid=(r // tile,), in_specs=specs, out_specs=row,
                          out_shape=jax.ShapeDtypeStruct((r, c), F32), compiler_params=_cp("parallel"))(*args)


def _adamw(w, m, v, ga, gb, name, tile):
    r, c = w.shape

    def body(*refs):
        if gb is None:
            w_ref, m_ref, v_ref, ga_ref, g_ref, d_ref, nm_ref, nv_ref = refs
            g = ga_ref[...]
        else:
            w_ref, m_ref, v_ref, ga_ref, gb_ref, g_ref, d_ref, nm_ref, nv_ref = refs
            g = ga_ref[...] + gb_ref[...]
        nm = ADAM_B1 * m_ref[...] + (1.0 - ADAM_B1) * g
        nv = ADAM_B2 * v_ref[...] + (1.0 - ADAM_B2) * (g * g)
        g_ref[...] = g
        nm_ref[...] = nm
        nv_ref[...] = nv
        d_ref[...] = -ADAM_LR * ((nm / BC1) / (jnp.sqrt(nv / BC2) + ADAM_EPS) + ADAM_WD * w_ref[...])

    row = pl.BlockSpec((tile, c), lambda i: (i, 0))
    args = [w, m, v, ga] + ([] if gb is None else [gb])
    return pl.pallas_call(body, name=name, grid=(r // tile,), in_specs=[row] * len(args), out_specs=[row] * 4,
                          out_shape=[jax.ShapeDtypeStruct((r, c), F32)] * 4, compiler_params=_cp("parallel"))(*args)


MATS = ("w_in", "w_out", "w_gate", "w_up", "w_down")
CONVS = ("ssd_conv_w", "lru_conv_w")
BIG = MATS + CONVS
COL_SHARDED = ("w_in", "w_gate", "w_up", "ssd_conv_w", "lru_conv_w")
SMALL = ("norm_mix", "ssd_conv_b", "ssd_dt_bias", "ssd_a_log", "ssd_d", "ssd_norm", "lru_conv_b", "lru_wa", "lru_ba",
         "lru_wx", "lru_bx", "lru_lambda", "norm_ffn", "norm_final")
WEIGHTS = ("norm_mix", "w_in", "ssd_conv_w", "ssd_conv_b", "ssd_dt_bias", "ssd_a_log", "ssd_d", "ssd_norm", "lru_conv_w",
           "lru_conv_b", "lru_wa", "lru_ba", "lru_wx", "lru_bx", "lru_lambda", "w_out", "norm_ffn", "w_gate", "w_up",
           "w_down", "norm_final")
ROW_TILE = {"w_in": 256, "w_out": 128, "w_gate": 256, "w_up": 256, "w_down": 352}


def _pack(arrs, width, row_mult, dtype):
    flat = jnp.concatenate([a.reshape(-1).astype(dtype) for a in arrs])
    rows = -(-flat.shape[0] // width)
    rows = -(-rows // row_mult) * row_mult
    flat = jnp.pad(flat, (0, rows * width - flat.shape[0]))
    return flat.reshape(rows, width)


def _unpack(buf, shapes):
    flat = buf.reshape(-1)
    out, off = [], 0
    for shp in shapes:
        n = int(np.prod(shp))
        out.append(flat[off:off + n].reshape(shp))
        off += n
    return out


def _join(name, g4):
    if name in COL_SHARDED:
        return jnp.moveaxis(g4, 0, -2).reshape(g4.shape[1:-1] + (4 * g4.shape[-1],))
    return g4.reshape((4 * g4.shape[1],) + g4.shape[2:])


def _slabs(name, g):
    if name in COL_SHARDED:
        return jnp.moveaxis(g.reshape(g.shape[:-1] + (4, g.shape[-1] // 4)), -2, 0)
    return g.reshape((4, g.shape[0] // 4) + g.shape[1:])


def _perm_cols(w):
    pad = jnp.zeros(w.shape[:-1] + (NP - IN_COLS,), w.dtype)
    return jnp.concatenate([w[..., :3072], w[..., 3080:4104], w[..., 3072:3080], pad], axis=-1)


def _unperm_cols(g):
    return jnp.concatenate([g[..., :3072], g[..., C_DT:C_DT + 8], g[..., 3072:4096]], axis=-1)


def _block_diag(w):
    eye = jnp.eye(LRU_BLOCKS, dtype=w.dtype)
    return jnp.einsum("ncd,nm->ncmd", w, eye).reshape(LRU_W, LRU_W)


def _block_diag_extract(g):
    g4 = g.reshape(LRU_BLOCKS, 64, LRU_BLOCKS, 64)
    return jnp.stack([g4[n, :, n, :] for n in range(LRU_BLOCKS)], axis=0)


def _lanes128(v):
    return jnp.pad(v, (0, BLK - v.shape[0])).reshape(1, BLK)


def _layer_mixers(x, p, comm=None):
    h = _rms_fwd(x, p["norm_mix"], "rms_mix")
    proj = _mm(h, p["w_in"], tm=1024, tn=1408, tk=1024, name="mm_in")
    att, lse, got = _att_fwd_fused(proj, "att_fwd", comm)
    xconv, dt = _ssd_pre(proj, p["ssd_conv_w"], p["ssd_conv_b"], _lanes128(p["ssd_dt_bias"]), "ssd_pre")
    spar = jnp.concatenate([_lanes128(p["ssd_a_log"]), _lanes128(p["ssd_d"]), jnp.zeros((6, BLK), F32)], axis=0)
    y, states = _ssd_scan(xconv, dt, spar, "ssd_scan")
    ssd = _ssd_post(y, proj, p["ssd_norm"], "ssd_post")
    xc = _lru_conv(proj, p["lru_conv_w"], p["lru_conv_b"], "lru_conv")
    wab = jnp.concatenate([_block_diag(p["lru_wa"]), _block_diag(p["lru_wx"])], axis=1).astype(MXU)
    pre = _mm(xc, wab, tm=1024, tn=1024, tk=512, name="mm_lru")
    lpar = jnp.concatenate([p["lru_ba"].reshape(1, -1), p["lru_bx"].reshape(1, -1), p["lru_lambda"].reshape(1, -1),
                            jnp.zeros((5, LRU_W), F32)], axis=0)
    lru, hs = _lru_scan(pre, xc, proj, lpar, "lru_scan")
    mix = jnp.concatenate([att, ssd, lru], axis=1).astype(MXU)
    saved = dict(x=x, h=h, proj=proj, att=att, lse=lse, xconv=xconv, dt=dt, spar=spar, y=y, states=states, xc=xc, wab=wab,
                 pre=pre, lpar=lpar, hs=hs, mix=mix)
    return mix, saved, got


def _layer_ffn(x, mix, p, saved):
    x1 = _mm(mix, p["w_out"], add=x, tm=1024, tn=1024, tk=1536, name="mm_out")
    h2 = _rms_fwd(x1, p["norm_ffn"], "rms_ffn")
    gu = _mm(h2, p["w_gu"], tm=1024, tn=1408, tk=1024, name="mm_gu")
    act = _swiglu_act(gu, "swiglu_act")
    x2 = _mm(act, p["w_down"], add=x1, tm=1024, tn=1024, tk=2816, name="mm_down")
    saved.update(x1=x1, h2=h2, gu=gu, act=act)
    return x2


def _layer_bwd(dx2, p, sv, comm_ssd=None, comm_att=None):
    g = {}
    da = _mm(dx2, p["w_down"], tb=True, tm=1024, tn=1408, tk=1024, name="mm_d_act")
    g["w_down"] = _mm(sv["act"], dx2, ta=True, tm=1408, tn=1024, tk=1024, name="mm_g_down")
    dgu = _swiglu_bwd(sv["gu"], da, "swiglu_bwd")
    dh2 = _mm(dgu, p["w_gu"], tb=True, tm=1024, tn=1024, tk=1408, name="mm_d_h2")
    g["w_gu"] = _mm(sv["h2"], dgu, ta=True, tm=1024, tn=1408, tk=1024, name="mm_g_gu")
    dx1, gn = _rms_bwd(sv["x1"], p["norm_ffn"], dh2, dx2, "rms_ffn_bwd")
    g["norm_ffn"] = jnp.sum(gn, axis=0)
    dmix = _mm(dx1, p["w_out"], tb=True, tm=1024, tn=1536, tk=1024, name="mm_d_mix")
    g["w_out"] = _mm(sv["mix"], dx1, ta=True, tm=1536, tn=1024, tk=1024, name="mm_g_out")
    datt, dssd, dlru = dmix[:, :ATT_W], dmix[:, ATT_W:ATT_W + SSD_W], dmix[:, ATT_W + SSD_W:]
    proj = sv["proj"]
    dpre, dxc_u, dgl, dlpar = _lru_scan_bwd(sv["pre"], sv["xc"], proj, sv["lpar"], sv["hs"], dlru, "lru_scan_bwd")
    dxc = _mm(dpre, sv["wab"], tb=True, add=dxc_u, tm=1024, tn=512, tk=1024, name="mm_d_xc")
    gwab = _mm(sv["xc"], dpre, ta=True, tm=512, tn=1024, tk=1024, name="mm_g_lru")
    g["lru_wa"], g["lru_wx"] = _block_diag_extract(gwab[:, :LRU_W]), _block_diag_extract(gwab[:, LRU_W:])
    g["lru_ba"], g["lru_bx"], g["lru_lambda"] = dlpar[0], dlpar[1], dlpar[2]
    dxl, gcw, gcb = _lru_conv_bwd(proj, dxc, p["lru_conv_w"], "lru_conv_bwd")
    g["lru_conv_w"], g["lru_conv_b"] = gcw[:CONV_K], jnp.sum(gcb, axis=0)
    dy, dz, gsn = _ssd_post_bwd(sv["y"], proj, p["ssd_norm"], dssd, "ssd_post_bwd")
    g["ssd_norm"] = jnp.sum(gsn, axis=0)
    dxconv, ddt, dal, ddk, got_ssd = _ssd_scan_bwd(sv["xconv"], sv["dt"], sv["spar"], sv["states"], dy, "ssd_scan_bwd", comm_ssd)
    g["ssd_a_log"], g["ssd_d"] = dal[0, :8], ddk[0, :8]
    dxbc, ddtr, gsw, gsb, gdb = _ssd_pre_bwd(proj, dxconv, ddt, p["ssd_conv_w"], p["ssd_conv_b"],
                                             _lanes128(p["ssd_dt_bias"]), "ssd_pre_bwd")
    g["ssd_conv_w"], g["ssd_conv_b"], g["ssd_dt_bias"] = gsw[:CONV_K], jnp.sum(gsb, axis=0), jnp.sum(gdb, axis=0)[:8]
    delta = _att_delta(datt, sv["att"], "att_delta")
    dq, dk, dv, got_att = _att_bwd_fused(proj, datt, sv["lse"], delta, "att_bwd", None if comm_att is None else comm_att(g))
    dproj = jnp.concatenate([dq, dk, dv, dz, dxbc, dgl, dxl, ddtr], axis=1).astype(MXU)
    dh = _mm(dproj, p["w_in"], tb=True, tm=1024, tn=1024, tk=1408, name="mm_d_h")
    g["w_in"] = _mm(sv["h"], dproj, ta=True, tm=1024, tn=1408, tk=1024, name="mm_g_in")
    dx, gm = _rms_bwd(sv["x"], p["norm_mix"], dh, dx1, "rms_mix_bwd")
    g["norm_mix"] = jnp.sum(gm, axis=0)
    return dx, g, got_ssd, got_att


def _grad_slabs(g, names):
    out = {}
    for n in names:
        if n == "w_in":
            out[n] = _slabs(n, _unperm_cols(g["w_in"]))
        elif n == "w_gate":
            out[n] = _slabs(n, g["w_gu"][:, :D_FF])
        elif n == "w_up":
            out[n] = _slabs(n, g["w_gu"][:, D_FF:])
        else:
            out[n] = _slabs(n, g[n])
    return out


def kernel(x, norm_mix, w_in, ssd_conv_w, ssd_conv_b, ssd_dt_bias, ssd_a_log, ssd_d, ssd_norm, lru_conv_w, lru_conv_b, lru_wa, lru_ba, lru_wx, lru_bx, lru_lambda, w_out, norm_ffn, w_gate, w_up, w_down, norm_final, loss_target, m_norm_mix, m_w_in, m_ssd_conv_w, m_ssd_conv_b, m_ssd_dt_bias, m_ssd_a_log, m_ssd_d, m_ssd_norm, m_lru_conv_w, m_lru_conv_b, m_lru_wa, m_lru_ba, m_lru_wx, m_lru_bx, m_lru_lambda, m_w_out, m_norm_ffn, m_w_gate, m_w_up, m_w_down, m_norm_final, v_norm_mix, v_w_in, v_ssd_conv_w, v_ssd_conv_b, v_ssd_dt_bias, v_ssd_a_log, v_ssd_d, v_ssd_norm, v_lru_conv_w, v_lru_conv_b, v_lru_wa, v_lru_ba, v_lru_wx, v_lru_bx, v_lru_lambda, v_w_out, v_norm_ffn, v_w_gate, v_w_up, v_w_down, v_norm_final):
    loc = dict(locals())
    w = {n: loc[n] for n in WEIGHTS}
    m = {n: loc["m_" + n] for n in WEIGHTS}
    v = {n: loc["v_" + n] for n in WEIGHTS}

    def halves(a):
        return a.reshape(a.shape[0], 2, a.shape[1] // 2, a.shape[2])

    def unhalve(a):
        return a.reshape(4, 2 * a.shape[2], a.shape[3])

    wb = {n: halves(w[n].astype(MXU)) for n in MATS}
    first = _comm_call(_Comm(gathers=[(wb["w_in"], 0, True), (w["ssd_conv_w"], None, False), (w["lru_conv_w"], None, False)]),
                       "gather_first")
    convs = {"ssd_conv_w": _join("ssd_conv_w", first[1]), "lru_conv_w": _join("lru_conv_w", first[2])}
    later = [(n, 0) for n in MATS[1:]] + [(n, 1) for n in MATS]
    rest = _Comm(gathers=[(wb[n], l, True) for n, l in later])
    whole = {("w_in", 0): _join("w_in", unhalve(first[0]))}
    params = {}

    def layer_params(l):
        if l not in params:
            p = {n: w[n][l] for n in SMALL if n != "norm_final"}
            p.update(w_in=_perm_cols(whole["w_in", l]), ssd_conv_w=convs["ssd_conv_w"][l], lru_conv_w=convs["lru_conv_w"][l])
            params[l] = p
        if "w_out" not in params[l] and ("w_out", l) in whole:
            params[l].update(w_out=whole["w_out", l], w_down=whole["w_down", l],
                             w_gu=jnp.concatenate([whole["w_gate", l], whole["w_up", l]], axis=-1))
        return params[l]

    xs = x[0]
    saved = []
    for l in range(DEPTH):
        mix, sv, got = _layer_mixers(xs, layer_params(l), rest if l == 0 else None)
        if l == 0:
            whole.update({k: _join(k[0], unhalve(a)) for k, a in zip(later, got)})
        xs = _layer_ffn(xs, mix, layer_params(l), sv)
        saved.append(sv)
    dx, gnf, lsum = _loss_head(xs, norm_final, loss_target[0], "loss_head")
    loss = lax.psum(jnp.sum(lsum), ("x", "y", "c"))

    dx, g1, _, _ = _layer_bwd(dx, layer_params(1), saved[1])
    s1 = _grad_slabs(g1, BIG)
    att0 = ("w_gate", "w_up", "w_down", "w_out")
    s0 = {}

    def wire(s, n):
        return s[n].astype(MXU) if n in MATS else s[n]

    def comm_att(g0):
        s0.update(_grad_slabs(g0, att0))
        return _Comm(scatters=[wire(s0, n) for n in att0])

    dx, g0, got_ssd, got_att = _layer_bwd(dx, layer_params(0), saved[0], _Comm(scatters=[wire(s1, n) for n in BIG]), comm_att)
    tail0 = ("w_in",) + CONVS
    s0.update(_grad_slabs(g0, tail0))
    got_tail = _comm_call(_Comm(scatters=[wire(s0, n) for n in tail0]), "exchange_tail")
    recv = {(n, 1): a for n, a in zip(BIG, got_ssd)}
    recv.update({(n, 0): a for n, a in zip(att0, got_att)})
    recv.update({(n, 0): a for n, a in zip(tail0, got_tail)})

    me = 2 * lax.axis_index("x") + lax.axis_index("y")
    slabs = (s0, s1)
    part = {}
    for n in BIG:
        per_layer = []
        for l in range(DEPTH):
            own = lax.dynamic_index_in_dim(slabs[l][n], me, axis=0, keepdims=False)
            per_layer.append(_sum_slots(own, recv[n, l], "sum_chips_" + n, ROW_TILE.get(n, own.shape[0])))
        part[n] = jnp.concatenate(per_layer, axis=0)
    sib = dict(zip(BIG, _swap_sibling([part[n] for n in BIG])))
    out_g, out_d, out_m, out_v = {}, {}, {}, {}
    for n in BIG:
        shp = w[n].shape
        flat = (shp[0] * shp[1], shp[2])
        res = _adamw(w[n].reshape(flat), m[n].reshape(flat), v[n].reshape(flat), part[n], sib[n], "adamw_" + n,
                     ROW_TILE.get(n, flat[0]))
        out_g[n], out_d[n], out_m[n], out_v[n] = [r.reshape(shp) for r in res]

    gsm = {n: jnp.stack([g0[n], g1[n]], axis=0) for n in SMALL if n != "norm_final"}
    gsm["norm_final"] = jnp.sum(gnf, axis=0)
    small_shapes = [w[n].shape for n in SMALL]
    gs = _pack([gsm[n].reshape(w[n].shape) for n in SMALL], BLK, 8, F32)
    gall = _gather_small(gs)
    gsum = _sum_slots(None, gall, "sum_devices", gs.shape[0])
    ws = _pack([w[n] for n in SMALL], BLK, 8, F32)
    ms = _pack([m[n] for n in SMALL], BLK, 8, F32)
    vs = _pack([v[n] for n in SMALL], BLK, 8, F32)
    gsr, dsr, nms, nvs = _adamw(ws, ms, vs, gsum, None, "adamw_small", gs.shape[0])
    out_g.update(zip(SMALL, _unpack(gsr, small_shapes)))
    out_d.update(zip(SMALL, _unpack(dsr, small_shapes)))
    out_m.update(zip(SMALL, _unpack(nms, small_shapes)))
    out_v.update(zip(SMALL, _unpack(nvs, small_shapes)))

    return (loss, dx[None], *[out_g[n] for n in WEIGHTS], *[out_d[n] for n in WEIGHTS],
            *[out_m[n] for n in WEIGHTS], *[out_v[n] for n in WEIGHTS])
```

```python
import functools
import math

import jax
import jax.numpy as jnp
import numpy as np
from jax import lax
from jax.experimental import pallas as pl
from jax.experimental.pallas import tpu as pltpu

F32 = jnp.float32
MXU = jnp.bfloat16
HI = lax.Precision.HIGHEST
MESH = pl.DeviceIdType.MESH

D_MODEL = 1024
DEPTH = 2
HEAD_DIM = 64
ATT_W = 512
ATT_PATTERNS = ((128, 1), (512, 4), (2048, 16))
BLK = 128
SSD_W = 512
SSD_STATE = 128
LRU_W = 512
LRU_BLOCKS = 8
LRU_C = 8.0
CONV_K = 4
D_MIX = 1536
D_FF = 2816
IN_COLS = 4104
NP = 4224
NORM_EPS = 1e-6
SSD_NORM_EPS = 1e-5
LN2 = math.log(2.0)
NEG = -1e30

ADAM_LR, ADAM_B1, ADAM_B2, ADAM_EPS, ADAM_WD, ADAM_STEP = 0.001, 0.9, 0.999, 1e-08, 0.01, 10
BC1 = 1.0 - ADAM_B1 ** ADAM_STEP
BC2 = 1.0 - ADAM_B2 ** ADAM_STEP

VMEM_LIMIT = 56 * 1024 * 1024

C_Q, C_K, C_V, C_Z, C_XBC, C_G, C_XL, C_DT = 0, 512, 1024, 1536, 2048, 3072, 3584, 4096


def _cp(*sem):
    return pltpu.CompilerParams(dimension_semantics=sem, vmem_limit_bytes=VMEM_LIMIT)


def _dot(a, b, dims, prec=None):
    return lax.dot_general(a, b, (dims, ((), ())), preferred_element_type=F32, precision=prec)


def _nn(a, b, prec=None):
    return _dot(a, b, ((1,), (0,)), prec)


def _nt(a, b, prec=None):
    return _dot(a, b, ((1,), (1,)), prec)


def _tn(a, b, prec=None):
    return _dot(a, b, ((0,), (0,)), prec)


def _sigmoid(x):
    return jax.nn.sigmoid(x)


def _silu(x):
    return x * _sigmoid(x)


def _softplus(x):
    return jnp.maximum(x, 0.0) + jnp.log(1.0 + jnp.exp(-jnp.abs(x)))


def _gelu(x):
    return 0.5 * x * (1.0 + jnp.tanh(0.7978845608028654 * (x + 0.044715 * x * x * x)))


def _mm(a, b, *, ta=False, tb=False, add=None, out_dtype=F32, tm, tn, tk, name):
    m, k = (a.shape[1], a.shape[0]) if ta else a.shape
    n = b.shape[0] if tb else b.shape[1]
    assert (b.shape[1] if tb else b.shape[0]) == k
    assert m % tm == 0 and n % tn == 0 and k % tk == 0, (name, m, n, k)
    nk = k // tk
    a_spec = pl.BlockSpec((tk, tm), lambda i, j, kk: (kk, i)) if ta else pl.BlockSpec((tm, tk), lambda i, j, kk: (i, kk))
    b_spec = pl.BlockSpec((tn, tk), lambda i, j, kk: (j, kk)) if tb else pl.BlockSpec((tk, tn), lambda i, j, kk: (kk, j))
    o_spec = pl.BlockSpec((tm, tn), lambda i, j, kk: (i, j))
    dims = ((0 if ta else 1,), (1 if tb else 0,))

    def body(*refs):
        if add is None:
            a_ref, b_ref, o_ref, acc = refs
        else:
            a_ref, b_ref, add_ref, o_ref, acc = refs
        kk = pl.program_id(2)

        @pl.when(kk == 0)
        def _():
            acc[...] = jnp.zeros_like(acc)

        acc[...] += _dot(a_ref[...].astype(MXU), b_ref[...].astype(MXU), dims)

        @pl.when(kk == nk - 1)
        def _():
            r = acc[...]
            if add is not None:
                r = r + add_ref[...]
            o_ref[...] = r.astype(out_dtype)

    ins = [a, b] + ([] if add is None else [add])
    specs = [a_spec, b_spec] + ([] if add is None else [o_spec])
    return pl.pallas_call(
        body, name=name, grid=(m // tm, n // tn, nk), in_specs=specs, out_specs=o_spec,
        out_shape=jax.ShapeDtypeStruct((m, n), out_dtype),
        scratch_shapes=[pltpu.VMEM((tm, tn), F32)],
        compiler_params=_cp("parallel", "parallel", "arbitrary"),
    )(*ins)


def _rows(fn, rows, consts=(), outs=(), accs=(), *, tile, name, halos=()):
    rows = [r if isinstance(r, tuple) else (r, r.shape[1], 0) for r in rows]
    s = rows[0][0].shape[0]
    assert s % tile == 0 and tile % 8 == 0
    n = s // tile
    t8 = tile // 8
    nr, nh, nc_, no, na = len(rows), len(halos), len(consts), len(outs), len(accs)

    def body(*refs):
        i = pl.program_id(0)
        rv = [r[...] for r in refs[:nr]]
        hv = []
        for (idx, kind), r in zip(halos, refs[nr:nr + nh]):
            edge = (i == 0) if kind == "prev" else (i == n - 1)
            hv.append(jnp.where(edge, 0.0, r[...]))
        cv = [r[...] for r in refs[nr + nh:nr + nh + nc_]]
        o_refs = refs[nr + nh + nc_:nr + nh + nc_ + no]
        a_refs = refs[nr + nh + nc_ + no:]
        ov, av = fn(rv, hv, cv)
        for r, v in zip(o_refs, ov):
            r[...] = v.astype(r.dtype)
        if na:
            @pl.when(i == 0)
            def _():
                for r in a_refs:
                    r[...] = jnp.zeros_like(r)
            for r, v in zip(a_refs, av):
                r[...] += v

    in_specs = [pl.BlockSpec((tile, w), functools.partial(lambda i, cb: (i, cb), cb=cb)) for (_, w, cb) in rows]
    for idx, kind in halos:
        _, w, cb = rows[idx]
        if kind == "prev":
            in_specs.append(pl.BlockSpec((8, w), functools.partial(lambda i, cb: (jnp.maximum(i * t8 - 1, 0), cb), cb=cb)))
        else:
            in_specs.append(pl.BlockSpec((8, w), functools.partial(lambda i, cb: (jnp.minimum((i + 1) * t8, n * t8 - 1), cb), cb=cb)))
    in_specs += [pl.BlockSpec(c.shape, functools.partial(lambda i, nd: (0,) * nd, nd=c.ndim)) for c in consts]
    out_specs = [pl.BlockSpec((tile, c), lambda i: (i, 0)) for (c, _) in outs]
    out_specs += [pl.BlockSpec((r, c), lambda i: (0, 0)) for (r, c) in accs]
    out_shape = [jax.ShapeDtypeStruct((s, c), dt) for (c, dt) in outs]
    out_shape += [jax.ShapeDtypeStruct((r, c), F32) for (r, c) in accs]
    args = [r[0] for r in rows] + [rows[idx][0] for idx, _ in halos] + list(consts)
    res = pl.pallas_call(
        body, name=name, grid=(n,), in_specs=in_specs, out_specs=out_specs, out_shape=out_shape,
        compiler_params=_cp("arbitrary"),
    )(*args)
    return list(res)


def _colsum8(v):
    t, c = v.shape
    return jnp.sum(v.reshape(t // 8, 8, c), axis=0)


def _rms(x, g):
    return x * lax.rsqrt(jnp.mean(x * x, axis=-1, keepdims=True) + NORM_EPS) * g


def _rms_fwd(x, g, name):
    def fn(rv, hv, cv):
        return [_rms(rv[0], cv[0])], []
    return _rows(fn, [x], [g.reshape(1, -1)], [(x.shape[1], MXU)], tile=512, name=name)[0]


def _rms_bwd(x, g, dh, dres, name):
    def fn(rv, hv, cv):
        xb, dhb, drb = rv
        _, vjp = jax.vjp(_rms, xb, cv[0])
        dx, _ = vjp(dhb)
        rstd = lax.rsqrt(jnp.mean(xb * xb, axis=-1, keepdims=True) + NORM_EPS)
        return [drb + dx], [_colsum8(dhb * xb * rstd)]
    d = x.shape[1]
    return _rows(fn, [x, dh, dres], [g.reshape(1, -1)], [(d, F32)], [(8, d)], tile=512, name=name)


def _slope_dist(hp, hh, dist, dil):
    hf = (2 * hp + hh + 1).astype(F32)
    slope = jnp.exp(jnp.zeros(dist.shape, F32) - hf * LN2)
    return slope * (dist.astype(F32) * float(dil))


def _att_delta(datt, att, name):
    def fn(rv, hv, cv):
        r = lax.broadcasted_iota(jnp.int32, (ATT_W, ATT_W), 0) // HEAD_DIM
        c = lax.broadcasted_iota(jnp.int32, (ATT_W, ATT_W), 1) // HEAD_DIM
        ones = (r == c).astype(F32)
        return [_nn(rv[0] * rv[1], ones, HI)], []
    return _rows(fn, [datt, att], [], [(ATT_W, F32)], tile=512, name=name)[0]


ATT_G = 2048


def _deinterleave(dst, src, dil, ld, region, offset):
    for r in range(dil):
        rows = pl.ds(r, ld, stride=dil) if dil > 1 else pl.ds(0, ld)
        dst[r * region + offset:r * region + offset + ld, :] = src[rows, :]


def _deinterleave_edge(dst, src, dil, region, offset, first_row):
    for r in range(dil):
        rows = pl.ds(first_row + r, BLK, stride=dil) if dil > 1 else pl.ds(first_row, BLK)
        dst[r * region + offset:r * region + offset + BLK, :] = src[rows, :]


def _att_fwd_fused(proj, name, comm=None):
    s, npc = proj.shape
    gsz = ATT_G
    ng = s // gsz
    assert s % gsz == 0
    scale = HEAD_DIM ** -0.5
    comm = comm or _Comm()

    def body(*refs):
        (q_ref, kp_ref, kc_ref, vp_ref, vc_ref, att_ref, lse_ref, qd, kd, vd, nd, md, dd, nn, mn, dn), cm = comm.split(refs, 5, 2, 9)
        hp, g = pl.program_id(0), pl.program_id(1)
        comm.start_at((hp == 0) & (g == 0), cm)
        lane = lax.broadcasted_iota(jnp.int32, (BLK, BLK), 1)
        qi = lax.broadcasted_iota(jnp.int32, (BLK, 2 * BLK), 0)
        ki = lax.broadcasted_iota(jnp.int32, (BLK, 2 * BLK), 1)
        dist = BLK + qi - ki
        band = (dist >= 0) & (dist <= BLK)
        for pi, (_, dil) in enumerate(ATT_PATTERNS):
            ld = gsz // dil
            nbg = ld // BLK
            _deinterleave(qd, q_ref, dil, ld, ld, 0)
            _deinterleave(kd, kc_ref, dil, ld, ld + BLK, BLK)
            _deinterleave(vd, vc_ref, dil, ld, ld + BLK, BLK)
            _deinterleave_edge(kd, kp_ref, dil, ld + BLK, 0, gsz - BLK * dil)
            _deinterleave_edge(vd, vp_ref, dil, ld + BLK, 0, gsz - BLK * dil)
            bias = [_slope_dist(hp, hh, dist, dil) for hh in (0, 1)]

            def tile(t, carry, ld=ld, nbg=nbg, bias=bias):
                r, b = t // nbg, t % nbg
                qo = pl.multiple_of(r * ld + b * BLK, BLK)
                ko = pl.multiple_of(r * (ld + BLK) + b * BLK, BLK)
                q = qd[pl.ds(qo, BLK), :]
                kk = kd[pl.ds(ko, 2 * BLK), :].astype(MXU)
                vv = vd[pl.ds(ko, 2 * BLK), :].astype(MXU)
                valid = band & ((g > 0) | (b > 0) | (ki >= BLK))
                num = jnp.zeros((BLK, BLK), F32)
                mx = jnp.zeros((BLK, BLK), F32)
                den = jnp.zeros((BLK, BLK), F32)
                for hh in (0, 1):
                    hmask = (lane < HEAD_DIM) if hh == 0 else (lane >= HEAD_DIM)
                    qm = jnp.where(hmask, q, 0.0).astype(MXU)
                    sc = jnp.where(valid, _nt(qm, kk) * scale - bias[hh], NEG)
                    m = jnp.max(sc, axis=1, keepdims=True)
                    p = jnp.exp(sc - m)
                    dn_ = jnp.sum(p, axis=1, keepdims=True)
                    o = _nn(p.astype(MXU), vv)
                    num = jnp.where(hmask, o, num)
                    mx = jnp.where(hmask, m, mx)
                    den = jnp.where(hmask, dn_, den)
                nd[pl.ds(qo, BLK), :] = num
                md[pl.ds(qo, BLK), :] = mx
                dd[pl.ds(qo, BLK), :] = den
                return carry

            lax.fori_loop(0, dil * nbg, tile, 0, unroll=4)
            for r in range(dil):
                rows = pl.ds(r, ld, stride=dil) if dil > 1 else pl.ds(0, ld)
                nn.at[pi][rows, :] = nd[r * ld:(r + 1) * ld, :]
                mn.at[pi][rows, :] = md[r * ld:(r + 1) * ld, :]
                dn.at[pi][rows, :] = dd[r * ld:(r + 1) * ld, :]

        def merge(c, carry):
            rows = pl.ds(pl.multiple_of(c * 256, 256), 256)
            ms = [mn[pi, rows, :] for pi in range(len(ATT_PATTERNS))]
            m_all = functools.reduce(jnp.maximum, ms)
            num = jnp.zeros((256, BLK), F32)
            den = jnp.zeros((256, BLK), F32)
            for pi in range(len(ATT_PATTERNS)):
                e = jnp.exp(ms[pi] - m_all)
                num = num + nn[pi, rows, :] * e
                den = den + dn[pi, rows, :] * e
            att_ref[rows, :] = num / den
            lse_ref[rows, :] = m_all + jnp.log(den)
            return carry

        lax.fori_loop(0, gsz // 256, merge, 0)
        comm.wait_at((hp == 3) & (g == ng - 1), cm)

    def cur(base):
        return pl.BlockSpec((gsz, BLK), lambda hp, g: (g, base // BLK + hp))

    def prev(base):
        return pl.BlockSpec((gsz, BLK), lambda hp, g: (jnp.maximum(g - 1, 0), base // BLK + hp))

    o_spec = pl.BlockSpec((gsz, BLK), lambda hp, g: (g, hp))
    npat = len(ATT_PATTERNS)
    res = pl.pallas_call(
        body, name=name, grid=(4, ng),
        in_specs=[cur(C_Q), prev(C_K), cur(C_K), prev(C_V), cur(C_V)] + [ANY] * comm.n,
        out_specs=[o_spec] * 2 + [ANY] * comm.n,
        out_shape=[jax.ShapeDtypeStruct((s, ATT_W), F32)] * 2 + comm.out_shape(),
        scratch_shapes=[pltpu.VMEM((gsz, BLK), F32), pltpu.VMEM((2 * gsz, BLK), F32), pltpu.VMEM((2 * gsz, BLK), F32)]
        + [pltpu.VMEM((gsz, BLK), F32)] * 3 + [pltpu.VMEM((npat, gsz, BLK), F32)] * 3 + comm.scratch(),
        compiler_params=_cp("arbitrary", "arbitrary"),
    )(proj, proj, proj, proj, proj, *comm.args())
    return res[0], res[1], list(res[2:])


def _att_bwd_fused(proj, datt, lse, delta, name, comm=None):
    s, npc = proj.shape
    gsz = ATT_G
    ng = s // gsz
    scale = HEAD_DIM ** -0.5
    comm = comm or _Comm()

    def body(*refs):
        (qc_ref, qn_ref, kp_ref, kc_ref, vp_ref, vc_ref, doc_ref, don_ref, lsc_ref, lsn_ref, dlc_ref, dln_ref,
         dq_ref, dk_ref, dv_ref, qd, dod, lsd, dld, kd, vd, dqd, dkd, dvd), cm = comm.split(refs, 12, 3, 9)
        hp, g = pl.program_id(0), pl.program_id(1)
        comm.start_at((hp == 0) & (g == 0), cm)
        lane = lax.broadcasted_iota(jnp.int32, (BLK, BLK), 1)
        qi = lax.broadcasted_iota(jnp.int32, (BLK, BLK), 0)
        ki = lax.broadcasted_iota(jnp.int32, (BLK, BLK), 1)
        d_far = BLK + qi - ki
        d_near = qi - ki
        for pi, (_, dil) in enumerate(ATT_PATTERNS):
            ld = gsz // dil
            nbg = ld // BLK
            reg = ld + BLK
            for dst, c_ref, n_ref in ((qd, qc_ref, qn_ref), (dod, doc_ref, don_ref), (lsd, lsc_ref, lsn_ref), (dld, dlc_ref, dln_ref)):
                _deinterleave(dst, c_ref, dil, ld, reg, 0)
                _deinterleave_edge(dst, n_ref, dil, reg, ld, 0)
            for dst, p_ref, c_ref in ((kd, kp_ref, kc_ref), (vd, vp_ref, vc_ref)):
                _deinterleave(dst, c_ref, dil, ld, reg, BLK)
                _deinterleave_edge(dst, p_ref, dil, reg, 0, gsz - BLK * dil)
            b_far = [_slope_dist(hp, hh, d_far, dil) for hh in (0, 1)]
            b_near = [_slope_dist(hp, hh, d_near, dil) for hh in (0, 1)]

            def tile(t, carry, ld=ld, nbg=nbg, reg=reg, b_far=b_far, b_near=b_near):
                r, b = t // nbg, t % nbg
                oo = pl.multiple_of(r * ld + b * BLK, BLK)
                ro = pl.multiple_of(r * reg + b * BLK, BLK)
                qn, qx = qd[pl.ds(ro, BLK), :], qd[pl.ds(ro + BLK, BLK), :]
                don, dox = dod[pl.ds(ro, BLK), :], dod[pl.ds(ro + BLK, BLK), :]
                lsn, lsx = lsd[pl.ds(ro, BLK), :], lsd[pl.ds(ro + BLK, BLK), :]
                dln, dlx = dld[pl.ds(ro, BLK), :], dld[pl.ds(ro + BLK, BLK), :]
                kp, kc = kd[pl.ds(ro, BLK), :].astype(MXU), kd[pl.ds(ro + BLK, BLK), :].astype(MXU)
                vp, vc = vd[pl.ds(ro, BLK), :].astype(MXU), vd[pl.ds(ro + BLK, BLK), :].astype(MXU)
                ok_a = (d_far <= BLK) & ((g > 0) | (b > 0))
                ok_b = d_near >= 0
                ok_c = (d_far <= BLK) & ((g < ng - 1) | (b < nbg - 1))

                def grads(qm, dom, k, v, ls, dl, bias, valid, hh):
                    c0 = hh * HEAD_DIM
                    sc = _nt(qm, k) * scale - bias
                    p = jnp.exp(jnp.where(valid, sc - ls[:, c0:c0 + 1], NEG))
                    ds = p * (_nt(dom, v) - dl[:, c0:c0 + 1])
                    return p.astype(MXU), ds.astype(MXU)

                dq = jnp.zeros((BLK, BLK), F32)
                dk = jnp.zeros((BLK, BLK), F32)
                dv = jnp.zeros((BLK, BLK), F32)
                for hh in (0, 1):
                    hmask = (lane < HEAD_DIM) if hh == 0 else (lane >= HEAD_DIM)
                    qnm = jnp.where(hmask, qn, 0.0).astype(MXU)
                    qxm = jnp.where(hmask, qx, 0.0).astype(MXU)
                    donm = jnp.where(hmask, don, 0.0).astype(MXU)
                    doxm = jnp.where(hmask, dox, 0.0).astype(MXU)
                    _, ds_a = grads(qnm, donm, kp, vp, lsn, dln, b_far[hh], ok_a, hh)
                    p_b, ds_b = grads(qnm, donm, kc, vc, lsn, dln, b_near[hh], ok_b, hh)
                    p_c, ds_c = grads(qxm, doxm, kc, vc, lsx, dlx, b_far[hh], ok_c, hh)
                    dq = jnp.where(hmask, _nn(ds_a, kp) + _nn(ds_b, kc), dq)
                    dk = dk + _tn(ds_b, qnm) + _tn(ds_c, qxm)
                    dv = dv + _tn(p_b, donm) + _tn(p_c, doxm)
                dqd[pl.ds(oo, BLK), :] = dq * scale
                dkd[pl.ds(oo, BLK), :] = dk * scale
                dvd[pl.ds(oo, BLK), :] = dv
                return carry

            lax.fori_loop(0, dil * nbg, tile, 0, unroll=4)
            for out, src in ((dq_ref, dqd), (dk_ref, dkd), (dv_ref, dvd)):
                for r in range(dil):
                    rows = pl.ds(r, ld, stride=dil) if dil > 1 else pl.ds(0, ld)
                    if pi == 0:
                        out[rows, :] = src[r * ld:(r + 1) * ld, :]
                    else:
                        out[rows, :] = out[rows, :] + src[r * ld:(r + 1) * ld, :]
        comm.wait_at((hp == 3) & (g == ng - 1), cm)

    def pspec(base, shift):
        return pl.BlockSpec((gsz, BLK), lambda hp, g: (jnp.clip(g + shift, 0, ng - 1), base // BLK + hp))

    def wspec(shift):
        return pl.BlockSpec((gsz, BLK), lambda hp, g: (jnp.clip(g + shift, 0, ng - 1), hp))

    in_specs = [pspec(C_Q, 0), pspec(C_Q, 1), pspec(C_K, -1), pspec(C_K, 0), pspec(C_V, -1), pspec(C_V, 0),
                wspec(0), wspec(1), wspec(0), wspec(1), wspec(0), wspec(1)] + [ANY] * comm.n
    res = pl.pallas_call(
        body, name=name, grid=(4, ng), in_specs=in_specs,
        out_specs=[wspec(0)] * 3 + [ANY] * comm.n,
        out_shape=[jax.ShapeDtypeStruct((s, ATT_W), F32)] * 3 + comm.out_shape(),
        scratch_shapes=[pltpu.VMEM((2 * gsz, BLK), F32)] * 6 + [pltpu.VMEM((gsz, BLK), F32)] * 3 + comm.scratch(),
        compiler_params=_cp("arbitrary", "arbitrary"),
    )(proj, proj, proj, proj, proj, proj, datt, datt, lse, lse, delta, delta, *comm.args())
    return res[0], res[1], res[2], list(res[3:])


def _att_bwd_rev(proj, datt, lse, delta, name, comm=None):
    s, npc = proj.shape
    gsz = ATT_G
    ng = s // gsz
    npat = len(ATT_PATTERNS)
    scale = HEAD_DIM ** -0.5
    comm = comm or _Comm()

    def body(*refs):
        (q_ref, kp_ref, kc_ref, vp_ref, vc_ref, do_ref, ls_ref, dl_ref, dq_ref, dk_ref, dv_ref,
         qd, dod, lsd, dld, kd, vd, dqd, dkc, dvc, dkp, dvp, kcar, vcar), cm = comm.split(refs, 8, 3, 13)
        hp, gi = pl.program_id(0), pl.program_id(1)
        g = ng - 1 - gi
        comm.start_at((hp == 0) & (gi == 0), cm)

        @pl.when(gi == 0)
        def _():
            kcar[...] = jnp.zeros_like(kcar)
            vcar[...] = jnp.zeros_like(vcar)

        lane = lax.broadcasted_iota(jnp.int32, (BLK, BLK), 1)
        qi = lax.broadcasted_iota(jnp.int32, (BLK, 2 * BLK), 0)
        ki = lax.broadcasted_iota(jnp.int32, (BLK, 2 * BLK), 1)
        dist = BLK + qi - ki
        band = (dist >= 0) & (dist <= BLK)
        for pi, (_, dil) in enumerate(ATT_PATTERNS):
            ld = gsz // dil
            nbg = ld // BLK
            reg = ld + BLK
            for dst, src in ((qd, q_ref), (dod, do_ref), (lsd, ls_ref), (dld, dl_ref)):
                _deinterleave(dst, src, dil, ld, ld, 0)
            for dst, p_ref, c_ref in ((kd, kp_ref, kc_ref), (vd, vp_ref, vc_ref)):
                _deinterleave(dst, c_ref, dil, ld, reg, BLK)
                _deinterleave_edge(dst, p_ref, dil, reg, 0, gsz - BLK * dil)
            bias = [_slope_dist(hp, hh, dist, dil) for hh in (0, 1)]

            def tile(t, carry, ld=ld, nbg=nbg, reg=reg, bias=bias):
                r, b = t // nbg, t % nbg
                oo = pl.multiple_of(r * ld + b * BLK, BLK)
                ko = pl.multiple_of(r * reg + b * BLK, BLK)
                q, do = qd[pl.ds(oo, BLK), :], dod[pl.ds(oo, BLK), :]
                ls, dl = lsd[pl.ds(oo, BLK), :], dld[pl.ds(oo, BLK), :]
                kk = kd[pl.ds(ko, 2 * BLK), :].astype(MXU)
                vv = vd[pl.ds(ko, 2 * BLK), :].astype(MXU)
                valid = band & ((g > 0) | (b > 0) | (ki >= BLK))
                dq = jnp.zeros((BLK, BLK), F32)
                dkk = jnp.zeros((2 * BLK, BLK), F32)
                dvv = jnp.zeros((2 * BLK, BLK), F32)
                for hh in (0, 1):
                    c0 = hh * HEAD_DIM
                    hmask = (lane < HEAD_DIM) if hh == 0 else (lane >= HEAD_DIM)
                    qm = jnp.where(hmask, q, 0.0).astype(MXU)
                    dom = jnp.where(hmask, do, 0.0).astype(MXU)
                    sc = _nt(qm, kk) * scale - bias[hh]
                    p = jnp.exp(jnp.where(valid, sc - ls[:, c0:c0 + 1], NEG))
                    ds = (p * (_nt(dom, vv) - dl[:, c0:c0 + 1])).astype(MXU)
                    dq = jnp.where(hmask, _nn(ds, kk), dq)
                    dkk = dkk + _tn(ds, qm)
                    dvv = dvv + _tn(p.astype(MXU), dom)
                dqd[pl.ds(oo, BLK), :] = dq * scale
                dkp[pl.ds(oo, BLK), :] = dkk[:BLK] * scale
                dkc[pl.ds(oo, BLK), :] = dkk[BLK:] * scale
                dvp[pl.ds(oo, BLK), :] = dvv[:BLK]
                dvc[pl.ds(oo, BLK), :] = dvv[BLK:]
                return carry

            lax.fori_loop(0, dil * nbg, tile, 0, unroll=4)
            for r in range(dil):
                rows = pl.ds(r, ld, stride=dil) if dil > 1 else pl.ds(0, ld)
                lo, hi = r * ld, (r + 1) * ld
                edge = slice(pi * gsz + r * BLK, pi * gsz + (r + 1) * BLK)
                for out, cur, prv, car in ((dk_ref, dkc, dkp, kcar), (dv_ref, dvc, dvp, vcar)):
                    later = car[edge, :] if nbg == 1 else jnp.concatenate([prv[lo + BLK:hi, :], car[edge, :]], axis=0)
                    total = cur[lo:hi, :] + later
                    car[edge, :] = prv[lo:lo + BLK, :]
                    out[rows, :] = total if pi == 0 else out[rows, :] + total
                dq_ref[rows, :] = dqd[lo:hi, :] if pi == 0 else dq_ref[rows, :] + dqd[lo:hi, :]
        comm.wait_at((hp == 3) & (gi == ng - 1), cm)

    def pspec(base, shift):
        return pl.BlockSpec((gsz, BLK), lambda hp, gi: (jnp.maximum(ng - 1 - gi + shift, 0), base // BLK + hp))

    wspec = pl.BlockSpec((gsz, BLK), lambda hp, gi: (ng - 1 - gi, hp))
    in_specs = [pspec(C_Q, 0), pspec(C_K, -1), pspec(C_K, 0), pspec(C_V, -1), pspec(C_V, 0), wspec, wspec, wspec] + [ANY] * comm.n
    res = pl.pallas_call(
        body, name=name, grid=(4, ng), in_specs=in_specs,
        out_specs=[wspec] * 3 + [ANY] * comm.n,
        out_shape=[jax.ShapeDtypeStruct((s, ATT_W), F32)] * 3 + comm.out_shape(),
        scratch_shapes=[pltpu.VMEM((gsz, BLK), F32)] * 4 + [pltpu.VMEM((2 * gsz, BLK), F32)] * 2
        + [pltpu.VMEM((gsz, BLK), F32)] * 5 + [pltpu.VMEM((npat * gsz, BLK), F32)] * 2 + comm.scratch(),
        compiler_params=_cp("arbitrary", "arbitrary"),
    )(proj, proj, proj, proj, proj, datt, lse, delta, *comm.args())
    return res[0], res[1], res[2], list(res[3:])


def _shift_down(cur, halo, sft):
    if sft == 0:
        return cur
    t = cur.shape[0]
    rolled = pltpu.roll(cur, sft, 0)
    hr = pltpu.roll(halo, sft, 0)
    row = lax.broadcasted_iota(jnp.int32, cur.shape, 0)
    return jnp.where(row < sft, jnp.tile(hr, (t // 8, 1)), rolled)


def _shift_up(cur, halo, sft):
    if sft == 0:
        return cur
    t = cur.shape[0]
    rolled = pltpu.roll(cur, t - sft, 0)
    hr = pltpu.roll(halo, 8 - sft, 0)
    row = lax.broadcasted_iota(jnp.int32, cur.shape, 0)
    return jnp.where(row >= t - sft, jnp.tile(hr, (t // 8, 1)), rolled)


def _conv(x, xh, w, b):
    y = b + x * w[CONV_K - 1:CONV_K]
    for k in range(CONV_K - 1):
        y = y + _shift_down(x, xh, CONV_K - 1 - k) * w[k:k + 1]
    return y


def _conv_bwd(x, xh, dy, dyh, w):
    dx = dy * w[CONV_K - 1:CONV_K]
    dws = []
    for k in range(CONV_K - 1):
        sft = CONV_K - 1 - k
        dx = dx + _shift_up(dy, dyh, sft) * w[k:k + 1]
        dws.append(jnp.sum(dy * _shift_down(x, xh, sft), axis=0, keepdims=True))
    dws.append(jnp.sum(dy * x, axis=0, keepdims=True))
    c = x.shape[1]
    dw = jnp.concatenate(dws + [jnp.zeros((8 - CONV_K, c), F32)], axis=0)
    return dx, dw, jnp.sum(dy, axis=0, keepdims=True)


def _pad8(w):
    return jnp.concatenate([w, jnp.zeros((8 - w.shape[0], w.shape[1]), w.dtype)], axis=0)


def _ssd_pre(proj, conv_w, conv_b, dt_bias128, name):
    def fn(rv, hv, cv):
        xbc, dtr = rv
        return [_silu(_conv(xbc, hv[0], cv[0], cv[1])), _softplus(dtr + cv[2])], []
    return _rows(fn, [(proj, 1024, C_XBC // 1024), (proj, BLK, C_DT // BLK)],
                 [_pad8(conv_w), conv_b.reshape(1, -1), dt_bias128],
                 [(1024, F32), (BLK, F32)], tile=256, name=name, halos=[(0, "prev")])


def _ssd_pre_bwd(proj, dxc, ddt, conv_w, conv_b, dt_bias128, name):
    def fn(rv, hv, cv):
        xbc, dtr, dxcb, ddtb = rv
        xh, dxch_raw, xnext = hv
        w, b, bias = cv
        pre = _conv(xbc, xh, w, b)
        sg = _sigmoid(pre)
        dpre = dxcb * (sg * (1.0 + pre * (1.0 - sg)))
        t = xbc.shape[0]
        tail = jnp.concatenate([xbc[t - 8:], xnext], axis=0)
        pre_n = _conv(tail[8:], tail[:8], w, b)
        sgn = _sigmoid(pre_n)
        dpre_h = dxch_raw * (sgn * (1.0 + pre_n * (1.0 - sgn)))
        dx, dw, db = _conv_bwd(xbc, xh, dpre, dpre_h, w)
        ddr = ddtb * _sigmoid(dtr + bias)
        return [dx, ddr], [dw, jnp.concatenate([db, jnp.zeros((7, db.shape[1]), F32)], axis=0), _colsum8(ddr)]
    return _rows(fn, [(proj, 1024, C_XBC // 1024), (proj, BLK, C_DT // BLK), dxc, ddt],
                 [_pad8(conv_w), conv_b.reshape(1, -1), dt_bias128],
                 [(1024, F32), (BLK, F32)], [(8, 1024), (8, 1024), (8, BLK)], tile=256, name=name,
                 halos=[(0, "prev"), (2, "next"), (0, "next")])


def _head_cols(v, h0):
    lane = lax.broadcasted_iota(jnp.int32, (v.shape[0], BLK), 1)
    return jnp.where(lane < HEAD_DIM, v[:, h0:h0 + 1], v[:, h0 + 1:h0 + 2])


def _ssd_scan(xc, dt, par, name):
    s = xc.shape[0]
    nc = s // BLK

    def body(x_ref, dt_ref, par_ref, y_ref, st_ref, h_ref):
        c = pl.program_id(0)

        @pl.when(c == 0)
        def _():
            h_ref[...] = jnp.zeros_like(h_ref)

        st_ref[0] = h_ref[...]
        dt = dt_ref[...]
        a_row = -jnp.exp(par_ref[0:1, :])
        d_row = par_ref[1:2, :]
        ri = lax.broadcasted_iota(jnp.int32, (BLK, BLK), 0)
        ci = lax.broadcasted_iota(jnp.int32, (BLK, BLK), 1)
        tril = ri >= ci
        cs = _nn(tril.astype(F32), dt * a_row, HI)
        cst, dtt = cs.T, dt.T
        last = cs[BLK - 1:BLK, :]
        wcol = jnp.exp(last - cs) * dt
        ecs = jnp.exp(cs)
        elast = jnp.exp(last)
        for g in (0, 1):
            bg = x_ref[:, 512 + g * BLK:512 + (g + 1) * BLK].astype(MXU)
            cg = x_ref[:, 768 + g * BLK:768 + (g + 1) * BLK].astype(MXU)
            gm = _nt(cg, bg)
            for pp in (0, 1):
                pr = 2 * g + pp
                h0 = 2 * pr
                x2 = x_ref[:, pr * BLK:(pr + 1) * BLK]
                hprev = h_ref[pr * BLK:(pr + 1) * BLK, :]
                yp = jnp.zeros((BLK, BLK), F32)
                for hh in (0, 1):
                    h = h0 + hh
                    hmask = (ci < HEAD_DIM) if hh == 0 else (ci >= HEAD_DIM)
                    lm = jnp.exp(jnp.where(tril, cs[:, h:h + 1] - cst[h:h + 1, :], NEG))
                    mm = gm * lm * dtt[h:h + 1, :]
                    yp = yp + _nn(mm.astype(MXU), jnp.where(hmask, x2, 0.0).astype(MXU))
                y0 = _nt(cg, hprev.astype(MXU))
                y_ref[:, pr * BLK:(pr + 1) * BLK] = yp + _head_cols(ecs, h0) * y0 + _head_cols(d_row, h0) * x2
                dec = jnp.where(ri < HEAD_DIM, elast[:, h0:h0 + 1], elast[:, h0 + 1:h0 + 2])
                xw = (x2 * _head_cols(wcol, h0)).astype(MXU)
                h_ref[pr * BLK:(pr + 1) * BLK, :] = dec * hprev + _tn(xw, bg)

    return pl.pallas_call(
        body, name=name, grid=(nc,),
        in_specs=[pl.BlockSpec((BLK, 1024), lambda c: (c, 0)), pl.BlockSpec((BLK, BLK), lambda c: (c, 0)),
                  pl.BlockSpec((8, BLK), lambda c: (0, 0))],
        out_specs=[pl.BlockSpec((BLK, SSD_W), lambda c: (c, 0)), pl.BlockSpec((1, SSD_W, SSD_STATE), lambda c: (c, 0, 0))],
        out_shape=[jax.ShapeDtypeStruct((s, SSD_W), F32), jax.ShapeDtypeStruct((nc, SSD_W, SSD_STATE), F32)],
        scratch_shapes=[pltpu.VMEM((SSD_W, SSD_STATE), F32)],
        compiler_params=_cp("arbitrary"),
    )(xc, dt, par)


def _ssd_scan_bwd(xc, dt, par, st, dy, name, comm=None):
    s = xc.shape[0]
    nc = s // BLK
    comm = comm or _Comm()

    def body(*refs):
        (x_ref, dt_ref, par_ref, st_ref, dy_ref, dx_ref, ddt_ref, dal_ref, dd_ref, dh_ref), cm = comm.split(refs, 5, 4, 1)
        c = pl.program_id(0)
        comm.start_at(c == 0, cm)

        @pl.when(c == 0)
        def _():
            dh_ref[...] = jnp.zeros_like(dh_ref)
            dal_ref[...] = jnp.zeros_like(dal_ref)
            dd_ref[...] = jnp.zeros_like(dd_ref)

        dt = dt_ref[...]
        a_row = -jnp.exp(par_ref[0:1, :])
        d_row = par_ref[1:2, :]
        ri = lax.broadcasted_iota(jnp.int32, (BLK, BLK), 0)
        ci = lax.broadcasted_iota(jnp.int32, (BLK, BLK), 1)
        tril = ri >= ci
        cs = _nn(tril.astype(F32), dt * a_row, HI)
        cst, dtt = cs.T, dt.T
        last = cs[BLK - 1:BLK, :]
        tolast = jnp.exp(last - cs)
        wcol = tolast * dt
        ecs = jnp.exp(cs)
        elast = jnp.exp(last)
        dcs_col = jnp.zeros((BLK, BLK), F32)
        ddt_col = jnp.zeros((BLK, BLK), F32)
        dcs_row = jnp.zeros((BLK, BLK), F32)
        ddt_row = jnp.zeros((BLK, BLK), F32)
        dlast = jnp.zeros((1, BLK), F32)
        ddsk = jnp.zeros((1, BLK), F32)
        for g in (0, 1):
            bg32 = x_ref[:, 512 + g * BLK:512 + (g + 1) * BLK]
            cg32 = x_ref[:, 768 + g * BLK:768 + (g + 1) * BLK]
            bg, cg = bg32.astype(MXU), cg32.astype(MXU)
            gm = _nt(cg, bg)
            dgm = jnp.zeros((BLK, BLK), F32)
            dbg = jnp.zeros((BLK, BLK), F32)
            dcg = jnp.zeros((BLK, BLK), F32)
            for pp in (0, 1):
                pr = 2 * g + pp
                h0 = 2 * pr
                x2 = x_ref[:, pr * BLK:(pr + 1) * BLK]
                dy2 = dy_ref[:, pr * BLK:(pr + 1) * BLK]
                hprev = st_ref[0, pr * BLK:(pr + 1) * BLK, :]
                dhn = dh_ref[pr * BLK:(pr + 1) * BLK, :]
                x2m, dhnm = x2.astype(MXU), dhn.astype(MXU)
                zb = _nt(bg, dhnm)
                y0 = _nt(cg, hprev.astype(MXU))
                esel = _head_cols(ecs, h0)
                wsel = _head_cols(wcol, h0)
                dx2 = _head_cols(d_row, h0) * dy2 + wsel * zb
                r_off = dy2 * y0
                r_w = x2 * zb
                r_d = dy2 * x2
                r_h = dhn * hprev
                for hh in (0, 1):
                    h = h0 + hh
                    hmask = (ci < HEAD_DIM) if hh == 0 else (ci >= HEAD_DIM)
                    onl = (ci == h).astype(F32)
                    ons = (ri == h).astype(F32)
                    dym = jnp.where(hmask, dy2, 0.0).astype(MXU)
                    dt_r = dtt[h:h + 1, :]
                    lm = jnp.exp(jnp.where(tril, cs[:, h:h + 1] - cst[h:h + 1, :], NEG))
                    mm = gm * lm * dt_r
                    dx2 = dx2 + _tn(mm.astype(MXU), dym)
                    dm = _nt(dym, x2m)
                    t1 = dm * lm
                    dgm = dgm + t1 * dt_r
                    tt = t1 * gm
                    ddt_row = ddt_row + ons * jnp.sum(tt, axis=0, keepdims=True)
                    t = tt * dt_r
                    dcs_col = dcs_col + onl * jnp.sum(t, axis=1, keepdims=True)
                    dcs_row = dcs_row - ons * jnp.sum(t, axis=0, keepdims=True)
                    de = jnp.sum(jnp.where(hmask, r_off, 0.0), axis=1, keepdims=True)
                    dcs_col = dcs_col + onl * (ecs[:, h:h + 1] * de)
                    hrow = (ri < HEAD_DIM) if hh == 0 else (ri >= HEAD_DIM)
                    dl_h = elast[:, h:h + 1] * jnp.sum(jnp.where(hrow, r_h, 0.0), keepdims=True)
                    dw = jnp.sum(jnp.where(hmask, r_w, 0.0), axis=1, keepdims=True)
                    ddt_col = ddt_col + onl * (dw * tolast[:, h:h + 1])
                    v = dw * wcol[:, h:h + 1]
                    dcs_col = dcs_col - onl * v
                    dl_h = dl_h + jnp.sum(v, keepdims=True)
                    dlast = dlast + onl[0:1, :] * dl_h
                    ddsk = ddsk + onl[0:1, :] * jnp.sum(jnp.where(hmask, r_d, 0.0), keepdims=True)
                dx_ref[:, pr * BLK:(pr + 1) * BLK] = dx2
                edy = (esel * dy2).astype(MXU)
                dcg = dcg + _nn(edy, hprev.astype(MXU))
                dec = jnp.where(ri < HEAD_DIM, elast[:, h0:h0 + 1], elast[:, h0 + 1:h0 + 2])
                dh_ref[pr * BLK:(pr + 1) * BLK, :] = dec * dhn + _tn(edy, cg)
                dbg = dbg + _nn((x2 * wsel).astype(MXU), dhnm)
            dgmm = dgm.astype(MXU)
            dx_ref[:, 512 + g * BLK:512 + (g + 1) * BLK] = dbg + _tn(dgmm, cg)
            dx_ref[:, 768 + g * BLK:768 + (g + 1) * BLK] = dcg + _nn(dgmm, bg)
        dcs = dcs_col + dcs_row.T + jnp.where(ri == BLK - 1, dlast, 0.0)
        dda = _nn((ri <= ci).astype(F32), dcs, HI)
        ddt_ref[...] = ddt_col + ddt_row.T + a_row * dda
        da = jnp.sum(dt * dda, axis=0, keepdims=True)
        dal_ref[0:1, :] += da * a_row
        dd_ref[0:1, :] += ddsk
        comm.wait_at(c == nc - 1, cm)

    rev = lambda c: (nc - 1 - c, 0)
    res = pl.pallas_call(
        body, name=name, grid=(nc,),
        in_specs=[pl.BlockSpec((BLK, 1024), rev), pl.BlockSpec((BLK, BLK), rev), pl.BlockSpec((8, BLK), lambda c: (0, 0)),
                  pl.BlockSpec((1, SSD_W, SSD_STATE), lambda c: (nc - 1 - c, 0, 0)), pl.BlockSpec((BLK, SSD_W), rev)]
        + [ANY] * comm.n,
        out_specs=[pl.BlockSpec((BLK, 1024), rev), pl.BlockSpec((BLK, BLK), rev),
                   pl.BlockSpec((8, BLK), lambda c: (0, 0)), pl.BlockSpec((8, BLK), lambda c: (0, 0))] + [ANY] * comm.n,
        out_shape=[jax.ShapeDtypeStruct((s, 1024), F32), jax.ShapeDtypeStruct((s, BLK), F32),
                   jax.ShapeDtypeStruct((8, BLK), F32), jax.ShapeDtypeStruct((8, BLK), F32)] + comm.out_shape(),
        scratch_shapes=[pltpu.VMEM((SSD_W, SSD_STATE), F32)] + comm.scratch(),
        compiler_params=_cp("arbitrary"),
    )(xc, dt, par, st, dy, *comm.args())
    return res[0], res[1], res[2], res[3], list(res[4:])


def _ssd_gate(y, z, w):
    t = y * _silu(z)
    outs = []
    for g in (0, 1):
        tg = t[:, g * 256:(g + 1) * 256]
        outs.append(tg * lax.rsqrt(jnp.mean(tg * tg, axis=-1, keepdims=True) + SSD_NORM_EPS))
    return jnp.concatenate(outs, axis=1) * w


def _ssd_post(y, proj, norm_w, name):
    def fn(rv, hv, cv):
        return [_ssd_gate(rv[0], rv[1], cv[0])], []
    return _rows(fn, [y, (proj, SSD_W, C_Z // SSD_W)], [norm_w.reshape(1, -1)], [(SSD_W, F32)], tile=512, name=name)[0]


def _ssd_post_bwd(y, proj, norm_w, dout, name):
    def fn(rv, hv, cv):
        yb, zb, db = rv
        _, vjp = jax.vjp(lambda a, b: _ssd_gate(a, b, cv[0]), yb, zb)
        dy, dz = vjp(db)
        t = yb * _silu(zb)
        nrm = []
        for g in (0, 1):
            tg = t[:, g * 256:(g + 1) * 256]
            nrm.append(tg * lax.rsqrt(jnp.mean(tg * tg, axis=-1, keepdims=True) + SSD_NORM_EPS))
        return [dy, dz], [_colsum8(db * jnp.concatenate(nrm, axis=1))]
    return _rows(fn, [y, (proj, SSD_W, C_Z // SSD_W), dout], [norm_w.reshape(1, -1)],
                 [(SSD_W, F32), (SSD_W, F32)], [(8, SSD_W)], tile=512, name=name)


LRU_T = 256


def _lru_conv(proj, conv_w, conv_b, name):
    def fn(rv, hv, cv):
        return [_conv(rv[0], hv[0], cv[0], cv[1])], []
    return _rows(fn, [(proj, LRU_W, C_XL // LRU_W)], [_pad8(conv_w), conv_b.reshape(1, -1)], [(LRU_W, F32)],
                 tile=512, name=name, halos=[(0, "prev")])[0]


def _lru_conv_bwd(proj, dxc, conv_w, name):
    def fn(rv, hv, cv):
        dx, dw, db = _conv_bwd(rv[0], hv[0], rv[1], hv[1], cv[0])
        return [dx], [dw, jnp.concatenate([db, jnp.zeros((7, db.shape[1]), F32)], axis=0)]
    return _rows(fn, [(proj, LRU_W, C_XL // LRU_W), dxc], [_pad8(conv_w)], [(LRU_W, F32)], [(8, LRU_W), (8, LRU_W)],
                 tile=512, name=name, halos=[(0, "prev"), (1, "next")])


def _lru_au(pre_a, pre_x, xc, ba, bx, lam):
    r = _sigmoid(pre_a + ba)
    i = _sigmoid(pre_x + bx)
    log_a = -LRU_C * r * _softplus(-lam)
    a = jnp.exp(log_a)
    u = jnp.sqrt(1.0 - jnp.exp(2.0 * log_a)) * (i * xc)
    return a, u


def _lru_scan(pre, xc, proj, par, name):
    s = xc.shape[0]
    t = LRU_T

    def body(pre_ref, xc_ref, g_ref, par_ref, out_ref, h_ref, carry):
        c = pl.program_id(0)

        @pl.when(c == 0)
        def _():
            carry[...] = jnp.zeros_like(carry)

        a, u = _lru_au(pre_ref[:, :LRU_W], pre_ref[:, LRU_W:], xc_ref[...], par_ref[0:1, :], par_ref[1:2, :], par_ref[2:3, :])
        row = lax.broadcasted_iota(jnp.int32, (t, LRU_W), 0)
        sft = 1
        while sft < t:
            keep = row >= sft
            a_s = jnp.where(keep, pltpu.roll(a, sft, 0), 1.0)
            u_s = jnp.where(keep, pltpu.roll(u, sft, 0), 0.0)
            u = a * u_s + u
            a = a * a_s
            sft *= 2
        h = a * carry[0:1, :] + u
        h_ref[...] = h
        out_ref[...] = h * _gelu(g_ref[...])
        carry[0:1, :] = h[t - 1:t, :]

    return pl.pallas_call(
        body, name=name, grid=(s // t,),
        in_specs=[pl.BlockSpec((t, 2 * LRU_W), lambda c: (c, 0)), pl.BlockSpec((t, LRU_W), lambda c: (c, 0)),
                  pl.BlockSpec((t, LRU_W), lambda c: (c, C_G // LRU_W)), pl.BlockSpec((8, LRU_W), lambda c: (0, 0))],
        out_specs=[pl.BlockSpec((t, LRU_W), lambda c: (c, 0))] * 2,
        out_shape=[jax.ShapeDtypeStruct((s, LRU_W), F32)] * 2,
        scratch_shapes=[pltpu.VMEM((8, LRU_W), F32)],
        compiler_params=_cp("arbitrary"),
    )(pre, xc, proj, par)


def _lru_scan_bwd(pre, xc, proj, par, h, dout, name):
    s = xc.shape[0]
    t = LRU_T
    n = s // t
    t8 = t // 8

    def body(pre_ref, xc_ref, g_ref, par_ref, h_ref, hh_ref, do_ref, dpre_ref, dxc_ref, dg_ref, dpar_ref, carry):
        c = pl.program_id(0)

        @pl.when(c == 0)
        def _():
            carry[...] = jnp.zeros_like(carry)
            dpar_ref[...] = jnp.zeros_like(dpar_ref)

        pa, px, xcb = pre_ref[:, :LRU_W], pre_ref[:, LRU_W:], xc_ref[...]
        ba, bx, lam = par_ref[0:1, :], par_ref[1:2, :], par_ref[2:3, :]
        (a, u), vjp = jax.vjp(_lru_au, pa, px, xcb, ba, bx, lam)
        g = g_ref[...]
        hcur = h_ref[...]
        do = do_ref[...]
        _, gvjp = jax.vjp(_gelu, g)
        dg_ref[...] = gvjp(do * hcur)[0]
        row = lax.broadcasted_iota(jnp.int32, (t, LRU_W), 0)
        v = do * _gelu(g) + jnp.where(row == t - 1, carry[0:1, :], 0.0)
        b = jnp.where(row == t - 1, 0.0, pltpu.roll(a, t - 1, 0))
        sft = 1
        while sft < t:
            keep = row < t - sft
            b_s = jnp.where(keep, pltpu.roll(b, t - sft, 0), 1.0)
            v_s = jnp.where(keep, pltpu.roll(v, t - sft, 0), 0.0)
            v = b * v_s + v
            b = b * b_s
            sft *= 2
        dh = v
        carry[0:1, :] = a[0:1, :] * dh[0:1, :]
        hhalo = jnp.where(c == n - 1, 0.0, hh_ref[...])
        hprev = _shift_down(hcur, hhalo, 1)
        dpa, dpx, dxc, dba, dbx, dlam = vjp((dh * hprev, dh))
        dpre_ref[:, :LRU_W] = dpa
        dpre_ref[:, LRU_W:] = dpx
        dxc_ref[...] = dxc
        dpar_ref[0:1, :] += dba
        dpar_ref[1:2, :] += dbx
        dpar_ref[2:3, :] += dlam

    rev = lambda c: (n - 1 - c, 0)
    return pl.pallas_call(
        body, name=name, grid=(n,),
        in_specs=[pl.BlockSpec((t, 2 * LRU_W), rev), pl.BlockSpec((t, LRU_W), rev),
                  pl.BlockSpec((t, LRU_W), lambda c: (n - 1 - c, C_G // LRU_W)), pl.BlockSpec((8, LRU_W), lambda c: (0, 0)),
                  pl.BlockSpec((t, LRU_W), rev),
                  pl.BlockSpec((8, LRU_W), lambda c: (jnp.maximum((n - 1 - c) * t8 - 1, 0), 0)),
                  pl.BlockSpec((t, LRU_W), lambda c: (n - 1 - c, dout.shape[1] // LRU_W - 1))],
        out_specs=[pl.BlockSpec((t, 2 * LRU_W), rev), pl.BlockSpec((t, LRU_W), rev), pl.BlockSpec((t, LRU_W), rev),
                   pl.BlockSpec((8, LRU_W), lambda c: (0, 0))],
        out_shape=[jax.ShapeDtypeStruct((s, 2 * LRU_W), F32), jax.ShapeDtypeStruct((s, LRU_W), F32),
                   jax.ShapeDtypeStruct((s, LRU_W), F32), jax.ShapeDtypeStruct((8, LRU_W), F32)],
        scratch_shapes=[pltpu.VMEM((8, LRU_W), F32)],
        compiler_params=_cp("arbitrary"),
    )(pre, xc, proj, par, h, h, dout)


def _swiglu_act(gu, name):
    def fn(rv, hv, cv):
        return [_silu(rv[0]) * rv[1]], []
    return _rows(fn, [(gu, D_FF, 0), (gu, D_FF, 1)], [], [(D_FF, MXU)], tile=256, name=name)[0]


def _swiglu_bwd(gu, da, name):
    def fn(rv, hv, cv):
        gt, up, dab = rv
        sg = _sigmoid(gt)
        dgate = dab * up * (sg * (1.0 + gt * (1.0 - sg)))
        dup = dab * (gt * sg)
        return [jnp.concatenate([dgate, dup], axis=1)], []
    return _rows(fn, [(gu, D_FF, 0), (gu, D_FF, 1), da], [], [(2 * D_FF, MXU)], tile=256, name=name)[0]


def _loss_head(x, g, target, name):
    d = x.shape[1]

    def fn(rv, hv, cv):
        xb, tb = rv
        y, vjp = jax.vjp(_rms, xb, cv[0])
        err = y - tb
        dy = err * (1.0 / d)
        dx, _ = vjp(dy)
        rstd = lax.rsqrt(jnp.mean(xb * xb, axis=-1, keepdims=True) + NORM_EPS)
        e2 = err * err * (0.5 / d)
        e2 = functools.reduce(lambda a, b: a + b, [e2[:, k * BLK:(k + 1) * BLK] for k in range(d // BLK)])
        return [dx], [_colsum8(dy * xb * rstd), _colsum8(e2)]
    return _rows(fn, [x, target], [g.reshape(1, -1)], [(d, F32)], [(8, d), (8, BLK)], tile=512, name=name)


ANY = pl.BlockSpec(memory_space=pl.ANY)


def _coords():
    return lax.axis_index("x"), lax.axis_index("y"), lax.axis_index("c")


class _Comm:
    def __init__(self, gathers=(), scatters=()):
        self.gathers = list(gathers)
        self.scatters = list(scatters)
        self.n = len(self.gathers) + len(self.scatters)

    def args(self):
        return [g[0] for g in self.gathers] + self.scatters

    def out_shape(self):
        out = [jax.ShapeDtypeStruct((4,) + (a.shape if l is None else a.shape[1:]), a.dtype) for a, l, _ in self.gathers]
        return out + [jax.ShapeDtypeStruct((3,) + a.shape[1:], a.dtype) for a in self.scatters]

    def scratch(self):
        if not self.n:
            return []
        return [pltpu.SemaphoreType.DMA((3 * self.n,)), pltpu.SemaphoreType.DMA((3 * self.n,)),
                pltpu.SemaphoreType.DMA((max(len(self.gathers), 1),)),
                pltpu.SemaphoreType.DMA((3 * self.n,)), pltpu.SemaphoreType.DMA((3 * self.n,))]

    def split(self, refs, n_in, n_out, n_scratch):
        refs = list(refs)
        n = self.n
        own = refs[:n_in] + refs[n_in + n:n_in + n + n_out] + refs[n_in + 2 * n + n_out:n_in + 2 * n + n_out + n_scratch]
        cm = (refs[n_in:n_in + n], refs[n_in + n + n_out:n_in + 2 * n + n_out], refs[n_in + 2 * n + n_out + n_scratch:])
        return own, cm

    def _copies(self, cm, arriving):
        ins, outs, (send, recv, local, _, _) = cm
        x, y, c = _coords()
        me = 2 * x + y
        chips = [(1 - x, y), (x, 1 - y), (1 - x, 1 - y)]
        remote, locals_ = [], []
        ng = len(self.gathers)
        for i in range(self.n):
            if i < ng:
                _, l, halved = self.gathers[i]
                slab = ins[i] if l is None else ins[i].at[l]
                if not arriving:
                    locals_.append(pltpu.make_async_copy(slab, outs[i].at[me], local.at[i]))
            for j, (px, py) in enumerate(chips):
                if i < ng:
                    slot = 2 * px + py if arriving else me
                    src, dst = (slab.at[c], outs[i].at[slot, c]) if halved else (slab, outs[i].at[slot])
                else:
                    src, dst = ins[i].at[2 * px + py], outs[i].at[j]
                remote.append(pltpu.make_async_remote_copy(src, dst, send.at[3 * i + j], recv.at[3 * i + j],
                                                           device_id=(px, py, c), device_id_type=MESH))
        return remote, locals_

    def _handovers(self, cm, arriving):
        _, outs, (_, _, _, send, recv) = cm
        x, y, c = _coords()
        chips = [(1 - x, y), (x, 1 - y), (1 - x, 1 - y)]
        cps = []
        for i, (_, _, halved) in enumerate(self.gathers):
            if halved:
                for j, (px, py) in enumerate(chips):
                    src = outs[i].at[2 * px + py, c]
                    dst = outs[i].at[2 * px + py, 1 - c if arriving else c]
                    cps.append(pltpu.make_async_remote_copy(src, dst, send.at[3 * i + j], recv.at[3 * i + j],
                                                            device_id=(x, y, 1 - c), device_id_type=MESH))
        return cps

    def start_at(self, cond, cm):
        def go():
            remote, locals_ = self._copies(cm, False)
            for cp in locals_ + remote:
                cp.start()

        if self.n:
            go() if cond is True else pl.when(cond)(go)

    def wait_at(self, cond, cm):
        def go():
            for cp in self._copies(cm, True)[0]:
                cp.wait_recv()
            handed = self._handovers(cm, False)
            for cp in handed:
                cp.start()
            for cp in self._handovers(cm, True):
                cp.wait_recv()
            remote, locals_ = self._copies(cm, False)
            for cp in handed + remote:
                cp.wait_send()
            for cp in locals_:
                cp.wait()

        if self.n:
            go() if cond is True else pl.when(cond)(go)


def _comm_call(comm, name):
    def body(*refs):
        _, cm = comm.split(refs, 0, 0, 0)
        comm.start_at(True, cm)
        comm.wait_at(True, cm)

    return list(pl.pallas_call(
        body, name=name, in_specs=[ANY] * comm.n, out_specs=[ANY] * comm.n, out_shape=comm.out_shape(),
        scratch_shapes=comm.scratch(), compiler_params=pltpu.CompilerParams(has_side_effects=True),
    )(*comm.args()))


def _swap_sibling(arrs):
    n = len(arrs)

    def body(*refs):
        ins, outs, send, recv = refs[:n], refs[n:2 * n], refs[2 * n], refs[2 * n + 1]
        x, y, c = _coords()
        cps = [pltpu.make_async_remote_copy(ins[i], outs[i], send.at[i], recv.at[i], device_id=(x, y, 1 - c), device_id_type=MESH)
               for i in range(n)]
        for cp in cps:
            cp.start()
        for cp in cps:
            cp.wait_recv()
        for cp in cps:
            cp.wait_send()

    return list(pl.pallas_call(
        body, name="swap_sibling", in_specs=[ANY] * n, out_specs=[ANY] * n,
        out_shape=[jax.ShapeDtypeStruct(a.shape, a.dtype) for a in arrs],
        scratch_shapes=[pltpu.SemaphoreType.DMA((n,)), pltpu.SemaphoreType.DMA((n,))],
        compiler_params=pltpu.CompilerParams(has_side_effects=True),
    )(*arrs))


def _gather_small(gs):
    def body(g_ref, o_ref, send_sems, recv_sems, local_sem):
        x, y, c = _coords()
        me = 4 * x + 2 * y + c
        mine = pltpu.make_async_copy(g_ref, o_ref.at[me], local_sem)
        mine.start()
        sends = []
        for k in range(1, 8):
            px, py, pc = x ^ (k >> 2), y ^ ((k >> 1) & 1), c ^ (k & 1)
            sends.append((pltpu.make_async_remote_copy(g_ref, o_ref.at[me], send_sems.at[k - 1], recv_sems.at[k - 1],
                                                       device_id=(px, py, pc), device_id_type=MESH), 4 * px + 2 * py + pc, k))
        for cp, _, _ in sends:
            cp.start()
        for cp, src, k in sends:
            pltpu.make_async_remote_copy(g_ref, o_ref.at[src], send_sems.at[k - 1], recv_sems.at[k - 1],
                                         device_id=(x, y, c), device_id_type=MESH).wait_recv()
        for cp, _, _ in sends:
            cp.wait_send()
        mine.wait()

    return pl.pallas_call(
        body, name="gather_small", in_specs=[ANY], out_specs=ANY,
        out_shape=jax.ShapeDtypeStruct((8,) + gs.shape, gs.dtype),
        scratch_shapes=[pltpu.SemaphoreType.DMA((7,)), pltpu.SemaphoreType.DMA((7,)), pltpu.SemaphoreType.DMA],
        compiler_params=pltpu.CompilerParams(has_side_effects=True),
    )(gs)


def _sum_slots(own, others, name, tile):
    k, r, c = others.shape

    def body(*refs):
        if own is None:
            o_ref, out_ref = refs
            acc = o_ref[0].astype(F32)
            first = 1
        else:
            own_ref, o_ref, out_ref = refs
            acc = own_ref[...]
            first = 0
        for j in range(first, k):
            acc = acc + o_ref[j].astype(F32)
        out_ref[...] = acc

    row = pl.BlockSpec((tile, c), lambda i: (i, 0))
    specs = ([] if own is None else [row]) + [pl.BlockSpec((k, tile, c), lambda i: (0, i, 0))]
    args = ([] if own is None else [own]) + [others]
    return pl.pallas_call(body, name=name, grid=(r // tile,), in_specs=specs, out_specs=row,
                          out_shape=jax.ShapeDtypeStruct((r, c), F32), compiler_params=_cp("parallel"))(*args)


def _adamw(w, m, v, ga, gb, name, tile):
    r, c = w.shape

    def body(*refs):
        if gb is None:
            w_ref, m_ref, v_ref, ga_ref, g_ref, d_ref, nm_ref, nv_ref = refs
            g = ga_ref[...]
        else:
            w_ref, m_ref, v_ref, ga_ref, gb_ref, g_ref, d_ref, nm_ref, nv_ref = refs
            g = ga_ref[...] + gb_ref[...]
        nm = ADAM_B1 * m_ref[...] + (1.0 - ADAM_B1) * g
        nv = ADAM_B2 * v_ref[...] + (1.0 - ADAM_B2) * (g * g)
        g_ref[...] = g
        nm_ref[...] = nm
        nv_ref[...] = nv
        d_ref[...] = -ADAM_LR * ((nm / BC1) / (jnp.sqrt(nv / BC2) + ADAM_EPS) + ADAM_WD * w_ref[...])

    row = pl.BlockSpec((tile, c), lambda i: (i, 0))
    args = [w, m, v, ga] + ([] if gb is None else [gb])
    return pl.pallas_call(body, name=name, grid=(r // tile,), in_specs=[row] * len(args), out_specs=[row] * 4,
                          out_shape=[jax.ShapeDtypeStruct((r, c), F32)] * 4, compiler_params=_cp("parallel"))(*args)


MATS = ("w_in", "w_out", "w_gate", "w_up", "w_down")
CONVS = ("ssd_conv_w", "lru_conv_w")
BIG = MATS + CONVS
COL_SHARDED = ("w_in", "w_gate", "w_up", "ssd_conv_w", "lru_conv_w")
SMALL = ("norm_mix", "ssd_conv_b", "ssd_dt_bias", "ssd_a_log", "ssd_d", "ssd_norm", "lru_conv_b", "lru_wa", "lru_ba",
         "lru_wx", "lru_bx", "lru_lambda", "norm_ffn", "norm_final")
WEIGHTS = ("norm_mix", "w_in", "ssd_conv_w", "ssd_conv_b", "ssd_dt_bias", "ssd_a_log", "ssd_d", "ssd_norm", "lru_conv_w",
           "lru_conv_b", "lru_wa", "lru_ba", "lru_wx", "lru_bx", "lru_lambda", "w_out", "norm_ffn", "w_gate", "w_up",
           "w_down", "norm_final")
ROW_TILE = {"w_in": 256, "w_out": 128, "w_gate": 256, "w_up": 256, "w_down": 352}


def _pack(arrs, width, row_mult, dtype):
    flat = jnp.concatenate([a.reshape(-1).astype(dtype) for a in arrs])
    rows = -(-flat.shape[0] // width)
    rows = -(-rows // row_mult) * row_mult
    flat = jnp.pad(flat, (0, rows * width - flat.shape[0]))
    return flat.reshape(rows, width)


def _unpack(buf, shapes):
    flat = buf.reshape(-1)
    out, off = [], 0
    for shp in shapes:
        n = int(np.prod(shp))
        out.append(flat[off:off + n].reshape(shp))
        off += n
    return out


def _join(name, g4):
    if name in COL_SHARDED:
        return jnp.moveaxis(g4, 0, -2).reshape(g4.shape[1:-1] + (4 * g4.shape[-1],))
    return g4.reshape((4 * g4.shape[1],) + g4.shape[2:])


def _slabs(name, g):
    if name in COL_SHARDED:
        return jnp.moveaxis(g.reshape(g.shape[:-1] + (4, g.shape[-1] // 4)), -2, 0)
    return g.reshape((4, g.shape[0] // 4) + g.shape[1:])


def _perm_cols(w):
    pad = jnp.zeros(w.shape[:-1] + (NP - IN_COLS,), w.dtype)
    return jnp.concatenate([w[..., :3072], w[..., 3080:4104], w[..., 3072:3080], pad], axis=-1)


def _unperm_cols(g):
    return jnp.concatenate([g[..., :3072], g[..., C_DT:C_DT + 8], g[..., 3072:4096]], axis=-1)


def _block_diag(w):
    eye = jnp.eye(LRU_BLOCKS, dtype=w.dtype)
    return jnp.einsum("ncd,nm->ncmd", w, eye).reshape(LRU_W, LRU_W)


def _block_diag_extract(g):
    g4 = g.reshape(LRU_BLOCKS, 64, LRU_BLOCKS, 64)
    return jnp.stack([g4[n, :, n, :] for n in range(LRU_BLOCKS)], axis=0)


def _lanes128(v):
    return jnp.pad(v, (0, BLK - v.shape[0])).reshape(1, BLK)


def _layer_mixers(x, p, comm=None):
    h = _rms_fwd(x, p["norm_mix"], "rms_mix")
    proj = _mm(h, p["w_in"], tm=1024, tn=1408, tk=1024, name="mm_in")
    att, lse, got = _att_fwd_fused(proj, "att_fwd", comm)
    xconv, dt = _ssd_pre(proj, p["ssd_conv_w"], p["ssd_conv_b"], _lanes128(p["ssd_dt_bias"]), "ssd_pre")
    spar = jnp.concatenate([_lanes128(p["ssd_a_log"]), _lanes128(p["ssd_d"]), jnp.zeros((6, BLK), F32)], axis=0)
    y, states = _ssd_scan(xconv, dt, spar, "ssd_scan")
    ssd = _ssd_post(y, proj, p["ssd_norm"], "ssd_post")
    xc = _lru_conv(proj, p["lru_conv_w"], p["lru_conv_b"], "lru_conv")
    wab = jnp.concatenate([_block_diag(p["lru_wa"]), _block_diag(p["lru_wx"])], axis=1).astype(MXU)
    pre = _mm(xc, wab, tm=1024, tn=1024, tk=512, name="mm_lru")
    lpar = jnp.concatenate([p["lru_ba"].reshape(1, -1), p["lru_bx"].reshape(1, -1), p["lru_lambda"].reshape(1, -1),
                            jnp.zeros((5, LRU_W), F32)], axis=0)
    lru, hs = _lru_scan(pre, xc, proj, lpar, "lru_scan")
    mix = jnp.concatenate([att, ssd, lru], axis=1).astype(MXU)
    saved = dict(x=x, h=h, proj=proj, att=att, lse=lse, xconv=xconv, dt=dt, spar=spar, y=y, states=states, xc=xc, wab=wab,
                 pre=pre, lpar=lpar, hs=hs, mix=mix)
    return mix, saved, got


def _layer_ffn(x, mix, p, saved):
    x1 = _mm(mix, p["w_out"], add=x, tm=1024, tn=1024, tk=1536, name="mm_out")
    h2 = _rms_fwd(x1, p["norm_ffn"], "rms_ffn")
    gu = _mm(h2, p["w_gu"], tm=1024, tn=1408, tk=1024, name="mm_gu")
    act = _swiglu_act(gu, "swiglu_act")
    x2 = _mm(act, p["w_down"], add=x1, tm=1024, tn=1024, tk=2816, name="mm_down")
    saved.update(x1=x1, h2=h2, gu=gu, act=act)
    return x2


def _layer_bwd(dx2, p, sv, comm_ssd=None, comm_att=None):
    g = {}
    da = _mm(dx2, p["w_down"], tb=True, tm=1024, tn=1408, tk=1024, name="mm_d_act")
    g["w_down"] = _mm(sv["act"], dx2, ta=True, tm=1408, tn=1024, tk=1024, name="mm_g_down")
    dgu = _swiglu_bwd(sv["gu"], da, "swiglu_bwd")
    dh2 = _mm(dgu, p["w_gu"], tb=True, tm=1024, tn=1024, tk=1408, name="mm_d_h2")
    g["w_gu"] = _mm(sv["h2"], dgu, ta=True, tm=1024, tn=1408, tk=1024, name="mm_g_gu")
    dx1, gn = _rms_bwd(sv["x1"], p["norm_ffn"], dh2, dx2, "rms_ffn_bwd")
    g["norm_ffn"] = jnp.sum(gn, axis=0)
    dmix = _mm(dx1, p["w_out"], tb=True, tm=1024, tn=1536, tk=1024, name="mm_d_mix")
    g["w_out"] = _mm(sv["mix"], dx1, ta=True, tm=1536, tn=1024, tk=1024, name="mm_g_out")
    proj = sv["proj"]
    dpre, dxc_u, dgl, dlpar = _lru_scan_bwd(sv["pre"], sv["xc"], proj, sv["lpar"], sv["hs"], dmix, "lru_scan_bwd")
    dxc = _mm(dpre, sv["wab"], tb=True, add=dxc_u, tm=1024, tn=512, tk=1024, name="mm_d_xc")
    gwab = _mm(sv["xc"], dpre, ta=True, tm=512, tn=1024, tk=1024, name="mm_g_lru")
    g["lru_wa"], g["lru_wx"] = _block_diag_extract(gwab[:, :LRU_W]), _block_diag_extract(gwab[:, LRU_W:])
    g["lru_ba"], g["lru_bx"], g["lru_lambda"] = dlpar[0], dlpar[1], dlpar[2]
    dxl, gcw, gcb = _lru_conv_bwd(proj, dxc, p["lru_conv_w"], "lru_conv_bwd")
    g["lru_conv_w"], g["lru_conv_b"] = gcw[:CONV_K], jnp.sum(gcb, axis=0)
    dy, dz, gsn = _ssd_post_bwd(sv["y"], proj, p["ssd_norm"], (dmix, SSD_W, 1), "ssd_post_bwd")
    g["ssd_norm"] = jnp.sum(gsn, axis=0)
    dxconv, ddt, dal, ddk, got_ssd = _ssd_scan_bwd(sv["xconv"], sv["dt"], sv["spar"], sv["states"], dy, "ssd_scan_bwd", comm_ssd)
    g["ssd_a_log"], g["ssd_d"] = dal[0, :8], ddk[0, :8]
    dxbc, ddtr, gsw, gsb, gdb = _ssd_pre_bwd(proj, dxconv, ddt, p["ssd_conv_w"], p["ssd_conv_b"],
                                             _lanes128(p["ssd_dt_bias"]), "ssd_pre_bwd")
    g["ssd_conv_w"], g["ssd_conv_b"], g["ssd_dt_bias"] = gsw[:CONV_K], jnp.sum(gsb, axis=0), jnp.sum(gdb, axis=0)[:8]
    delta = _att_delta((dmix, ATT_W, 0), sv["att"], "att_delta")
    dq, dk, dv, got_att = _att_bwd_rev(proj, dmix, sv["lse"], delta, "att_bwd", None if comm_att is None else comm_att(g))
    dproj = jnp.concatenate([dq, dk, dv, dz, dxbc, dgl, dxl, ddtr], axis=1).astype(MXU)
    dh = _mm(dproj, p["w_in"], tb=True, tm=1024, tn=1024, tk=1408, name="mm_d_h")
    g["w_in"] = _mm(sv["h"], dproj, ta=True, tm=1024, tn=1408, tk=1024, name="mm_g_in")
    dx, gm = _rms_bwd(sv["x"], p["norm_mix"], dh, dx1, "rms_mix_bwd")
    g["norm_mix"] = jnp.sum(gm, axis=0)
    return dx, g, got_ssd, got_att


def _grad_slabs(g, names):
    out = {}
    for n in names:
        if n == "w_in":
            out[n] = _slabs(n, _unperm_cols(g["w_in"]))
        elif n == "w_gate":
            out[n] = _slabs(n, g["w_gu"][:, :D_FF])
        elif n == "w_up":
            out[n] = _slabs(n, g["w_gu"][:, D_FF:])
        else:
            out[n] = _slabs(n, g[n])
    return out


def kernel(x, norm_mix, w_in, ssd_conv_w, ssd_conv_b, ssd_dt_bias, ssd_a_log, ssd_d, ssd_norm, lru_conv_w, lru_conv_b, lru_wa, lru_ba, lru_wx, lru_bx, lru_lambda, w_out, norm_ffn, w_gate, w_up, w_down, norm_final, loss_target, m_norm_mix, m_w_in, m_ssd_conv_w, m_ssd_conv_b, m_ssd_dt_bias, m_ssd_a_log, m_ssd_d, m_ssd_norm, m_lru_conv_w, m_lru_conv_b, m_lru_wa, m_lru_ba, m_lru_wx, m_lru_bx, m_lru_lambda, m_w_out, m_norm_ffn, m_w_gate, m_w_up, m_w_down, m_norm_final, v_norm_mix, v_w_in, v_ssd_conv_w, v_ssd_conv_b, v_ssd_dt_bias, v_ssd_a_log, v_ssd_d, v_ssd_norm, v_lru_conv_w, v_lru_conv_b, v_lru_wa, v_lru_ba, v_lru_wx, v_lru_bx, v_lru_lambda, v_w_out, v_norm_ffn, v_w_gate, v_w_up, v_w_down, v_norm_final):
    loc = dict(locals())
    w = {n: loc[n] for n in WEIGHTS}
    m = {n: loc["m_" + n] for n in WEIGHTS}
    v = {n: loc["v_" + n] for n in WEIGHTS}

    def halves(a):
        return a.reshape(a.shape[0], 2, a.shape[1] // 2, a.shape[2])

    def unhalve(a):
        return a.reshape(4, 2 * a.shape[2], a.shape[3])

    wb = {n: halves(w[n].astype(MXU)) for n in MATS}
    first = _comm_call(_Comm(gathers=[(wb["w_in"], 0, True), (w["ssd_conv_w"], None, False), (w["lru_conv_w"], None, False)]),
                       "gather_first")
    convs = {"ssd_conv_w": _join("ssd_conv_w", first[1]), "lru_conv_w": _join("lru_conv_w", first[2])}
    later = [(n, 0) for n in MATS[1:]] + [(n, 1) for n in MATS]
    rest = _Comm(gathers=[(wb[n], l, True) for n, l in later])
    whole = {("w_in", 0): _join("w_in", unhalve(first[0]))}
    params = {}

    def layer_params(l):
        if l not in params:
            p = {n: w[n][l] for n in SMALL if n != "norm_final"}
            p.update(w_in=_perm_cols(whole["w_in", l]), ssd_conv_w=convs["ssd_conv_w"][l], lru_conv_w=convs["lru_conv_w"][l])
            params[l] = p
        if "w_out" not in params[l] and ("w_out", l) in whole:
            params[l].update(w_out=whole["w_out", l], w_down=whole["w_down", l],
                             w_gu=jnp.concatenate([whole["w_gate", l], whole["w_up", l]], axis=-1))
        return params[l]

    xs = x[0]
    saved = []
    for l in range(DEPTH):
        mix, sv, got = _layer_mixers(xs, layer_params(l), rest if l == 0 else None)
        if l == 0:
            whole.update({k: _join(k[0], unhalve(a)) for k, a in zip(later, got)})
        xs = _layer_ffn(xs, mix, layer_params(l), sv)
        saved.append(sv)
    dx, gnf, lsum = _loss_head(xs, norm_final, loss_target[0], "loss_head")
    loss = lax.psum(jnp.sum(lsum), ("x", "y", "c"))

    dx, g1, _, _ = _layer_bwd(dx, layer_params(1), saved[1])
    s1 = _grad_slabs(g1, BIG)
    att0 = ("w_gate", "w_up", "w_down", "w_out")
    s0 = {}

    def wire(s, n):
        return s[n].astype(MXU) if n in MATS else s[n]

    def comm_att(g0):
        s0.update(_grad_slabs(g0, att0))
        return _Comm(scatters=[wire(s0, n) for n in att0])

    dx, g0, got_ssd, got_att = _layer_bwd(dx, layer_params(0), saved[0], _Comm(scatters=[wire(s1, n) for n in BIG]), comm_att)
    tail0 = ("w_in",) + CONVS
    s0.update(_grad_slabs(g0, tail0))
    got_tail = _comm_call(_Comm(scatters=[wire(s0, n) for n in tail0]), "exchange_tail")
    recv = {(n, 1): a for n, a in zip(BIG, got_ssd)}
    recv.update({(n, 0): a for n, a in zip(att0, got_att)})
    recv.update({(n, 0): a for n, a in zip(tail0, got_tail)})

    me = 2 * lax.axis_index("x") + lax.axis_index("y")
    slabs = (s0, s1)
    part = {}
    for n in BIG:
        per_layer = []
        for l in range(DEPTH):
            own = lax.dynamic_index_in_dim(slabs[l][n], me, axis=0, keepdims=False)
            per_layer.append(_sum_slots(own, recv[n, l], "sum_chips_" + n, ROW_TILE.get(n, own.shape[0])))
        part[n] = jnp.concatenate(per_layer, axis=0)
    sib = dict(zip(BIG, _swap_sibling([part[n] for n in BIG])))
    out_g, out_d, out_m, out_v = {}, {}, {}, {}
    for n in BIG:
        shp = w[n].shape
        flat = (shp[0] * shp[1], shp[2])
        res = _adamw(w[n].reshape(flat), m[n].reshape(flat), v[n].reshape(flat), part[n], sib[n], "adamw_" + n,
                     ROW_TILE.get(n, flat[0]))
        out_g[n], out_d[n], out_m[n], out_v[n] = [r.reshape(shp) for r in res]

    gsm = {n: jnp.stack([g0[n], g1[n]], axis=0) for n in SMALL if n != "norm_final"}
    gsm["norm_final"] = jnp.sum(gnf, axis=0)
    small_shapes = [w[n].shape for n in SMALL]
    gs = _pack([gsm[n].reshape(w[n].shape) for n in SMALL], BLK, 8, F32)
    gall = _gather_small(gs)
    gsum = _sum_slots(None, gall, "sum_devices", gs.shape[0])
    ws = _pack([w[n] for n in SMALL], BLK, 8, F32)
    ms = _pack([m[n] for n in SMALL], BLK, 8, F32)
    vs = _pack([v[n] for n in SMALL], BLK, 8, F32)
    gsr, dsr, nms, nvs = _adamw(ws, ms, vs, gsum, None, "adamw_small", gs.shape[0])
    out_g.update(zip(SMALL, _unpack(gsr, small_shapes)))
    out_d.update(zip(SMALL, _unpack(dsr, small_shapes)))
    out_m.update(zip(SMALL, _unpack(nms, small_shapes)))
    out_v.update(zip(SMALL, _unpack(nvs, small_shapes)))

    return (loss, dx[None], *[out_g[n] for n in WEIGHTS], *[out_d[n] for n in WEIGHTS],
            *[out_m[n] for n in WEIGHTS], *[out_v[n] for n in WEIGHTS])
```

```python
import functools
import math

import jax
import jax.numpy as jnp
import numpy as np
from jax import lax
from jax.experimental import pallas as pl
from jax.experimental.pallas import tpu as pltpu

F32 = jnp.float32
MXU = jnp.bfloat16
HI = lax.Precision.HIGHEST
MESH = pl.DeviceIdType.MESH

D_MODEL = 1024
DEPTH = 2
HEAD_DIM = 64
ATT_W = 512
ATT_PATTERNS = ((128, 1), (512, 4), (2048, 16))
BLK = 128
SSD_W = 512
SSD_STATE = 128
LRU_W = 512
LRU_BLOCKS = 8
LRU_C = 8.0
CONV_K = 4
D_MIX = 1536
D_FF = 2816
IN_COLS = 4104
NP = 4224
NORM_EPS = 1e-6
SSD_NORM_EPS = 1e-5
LN2 = math.log(2.0)
NEG = -1e30

ADAM_LR, ADAM_B1, ADAM_B2, ADAM_EPS, ADAM_WD, ADAM_STEP = 0.001, 0.9, 0.999, 1e-08, 0.01, 10
BC1 = 1.0 - ADAM_B1 ** ADAM_STEP
BC2 = 1.0 - ADAM_B2 ** ADAM_STEP

VMEM_LIMIT = 56 * 1024 * 1024

C_Q, C_K, C_V, C_Z, C_XBC, C_G, C_XL, C_DT = 0, 512, 1024, 1536, 2048, 3072, 3584, 4096


def _cp(*sem):
    return pltpu.CompilerParams(dimension_semantics=sem, vmem_limit_bytes=VMEM_LIMIT)


def _dot(a, b, dims, prec=None):
    return lax.dot_general(a, b, (dims, ((), ())), preferred_element_type=F32, precision=prec)


def _nn(a, b, prec=None):
    return _dot(a, b, ((1,), (0,)), prec)


def _nt(a, b, prec=None):
    return _dot(a, b, ((1,), (1,)), prec)


def _tn(a, b, prec=None):
    return _dot(a, b, ((0,), (0,)), prec)


def _sigmoid(x):
    return jax.nn.sigmoid(x)


def _silu(x):
    return x * _sigmoid(x)


def _softplus(x):
    return jnp.maximum(x, 0.0) + jnp.log(1.0 + jnp.exp(-jnp.abs(x)))


def _gelu(x):
    return 0.5 * x * (1.0 + jnp.tanh(0.7978845608028654 * (x + 0.044715 * x * x * x)))


def _mm(a, b, *, ta=False, tb=False, add=None, out_dtype=F32, tm, tn, tk, name, comm=None):
    m, k = (a.shape[1], a.shape[0]) if ta else a.shape
    n = b.shape[0] if tb else b.shape[1]
    assert (b.shape[1] if tb else b.shape[0]) == k
    assert m % tm == 0 and n % tn == 0 and k % tk == 0, (name, m, n, k)
    nk = k // tk
    a_spec = pl.BlockSpec((tk, tm), lambda i, j, kk: (kk, i)) if ta else pl.BlockSpec((tm, tk), lambda i, j, kk: (i, kk))
    b_spec = pl.BlockSpec((tn, tk), lambda i, j, kk: (j, kk)) if tb else pl.BlockSpec((tk, tn), lambda i, j, kk: (kk, j))
    o_spec = pl.BlockSpec((tm, tn), lambda i, j, kk: (i, j))
    dims = ((0 if ta else 1,), (1 if tb else 0,))
    carried = comm is not None
    comm = comm or _Comm()
    ni, nj = m // tm, n // tn

    def body(*refs):
        refs, cm = comm.split(refs, 2 if add is None else 3, 1, 1)
        if add is None:
            a_ref, b_ref, o_ref, acc = refs
        else:
            a_ref, b_ref, add_ref, o_ref, acc = refs
        i, j, kk = pl.program_id(0), pl.program_id(1), pl.program_id(2)
        comm.start_at((i == 0) & (j == 0) & (kk == 0), cm)

        @pl.when(kk == 0)
        def _():
            acc[...] = jnp.zeros_like(acc)

        acc[...] += _dot(a_ref[...].astype(MXU), b_ref[...].astype(MXU), dims)

        @pl.when(kk == nk - 1)
        def _():
            r = acc[...]
            if add is not None:
                r = r + add_ref[...]
            o_ref[...] = r.astype(out_dtype)

        comm.wait_at((i == ni - 1) & (j == nj - 1) & (kk == nk - 1), cm)

    ins = [a, b] + ([] if add is None else [add])
    specs = [a_spec, b_spec] + ([] if add is None else [o_spec])
    res = pl.pallas_call(
        body, name=name, grid=(ni, nj, nk), in_specs=specs + [ANY] * comm.n, out_specs=[o_spec] + [ANY] * comm.n,
        out_shape=[jax.ShapeDtypeStruct((m, n), out_dtype)] + comm.out_shape(),
        scratch_shapes=[pltpu.VMEM((tm, tn), F32)] + comm.scratch(),
        compiler_params=_cp(*((["arbitrary"] * 3) if comm.n else ["parallel", "parallel", "arbitrary"])),
    )(*ins, *comm.args())
    return (res[0], list(res[1:])) if carried else res[0]


def _rows(fn, rows, consts=(), outs=(), accs=(), *, tile, name, halos=(), comm=None):
    rows = [r if isinstance(r, tuple) else (r, r.shape[1], 0) for r in rows]
    s = rows[0][0].shape[0]
    assert s % tile == 0 and tile % 8 == 0
    n = s // tile
    t8 = tile // 8
    nr, nh, nc_, no, na = len(rows), len(halos), len(consts), len(outs), len(accs)
    carried = comm is not None
    comm = comm or _Comm()

    def body(*refs):
        refs, cm = comm.split(refs, nr + nh + nc_, no + na, 0)
        i = pl.program_id(0)
        comm.start_at(i == 0, cm)
        rv = [r[...] for r in refs[:nr]]
        hv = []
        for (idx, kind), r in zip(halos, refs[nr:nr + nh]):
            edge = (i == 0) if kind == "prev" else (i == n - 1)
            hv.append(jnp.where(edge, 0.0, r[...]))
        cv = [r[...] for r in refs[nr + nh:nr + nh + nc_]]
        o_refs = refs[nr + nh + nc_:nr + nh + nc_ + no]
        a_refs = refs[nr + nh + nc_ + no:]
        ov, av = fn(rv, hv, cv)
        for r, v in zip(o_refs, ov):
            r[...] = v.astype(r.dtype)
        if na:
            @pl.when(i == 0)
            def _():
                for r in a_refs:
                    r[...] = jnp.zeros_like(r)
            for r, v in zip(a_refs, av):
                r[...] += v
        comm.wait_at(i == n - 1, cm)

    in_specs = [pl.BlockSpec((tile, w), functools.partial(lambda i, cb: (i, cb), cb=cb)) for (_, w, cb) in rows]
    for idx, kind in halos:
        _, w, cb = rows[idx]
        if kind == "prev":
            in_specs.append(pl.BlockSpec((8, w), functools.partial(lambda i, cb: (jnp.maximum(i * t8 - 1, 0), cb), cb=cb)))
        else:
            in_specs.append(pl.BlockSpec((8, w), functools.partial(lambda i, cb: (jnp.minimum((i + 1) * t8, n * t8 - 1), cb), cb=cb)))
    in_specs += [pl.BlockSpec(c.shape, functools.partial(lambda i, nd: (0,) * nd, nd=c.ndim)) for c in consts]
    out_specs = [pl.BlockSpec((tile, c), lambda i: (i, 0)) for (c, _) in outs]
    out_specs += [pl.BlockSpec((r, c), lambda i: (0, 0)) for (r, c) in accs]
    out_shape = [jax.ShapeDtypeStruct((s, c), dt) for (c, dt) in outs]
    out_shape += [jax.ShapeDtypeStruct((r, c), F32) for (r, c) in accs]
    args = [r[0] for r in rows] + [rows[idx][0] for idx, _ in halos] + list(consts)
    res = pl.pallas_call(
        body, name=name, grid=(n,), in_specs=in_specs + [ANY] * comm.n, out_specs=out_specs + [ANY] * comm.n,
        out_shape=out_shape + comm.out_shape(), scratch_shapes=comm.scratch(), compiler_params=_cp("arbitrary"),
    )(*args, *comm.args())
    return (list(res[:no + na]), list(res[no + na:])) if carried else list(res)


def _colsum8(v):
    t, c = v.shape
    return jnp.sum(v.reshape(t // 8, 8, c), axis=0)


def _rms(x, g):
    return x * lax.rsqrt(jnp.mean(x * x, axis=-1, keepdims=True) + NORM_EPS) * g


def _rms_fwd(x, g, name, comm=None):
    def fn(rv, hv, cv):
        return [_rms(rv[0], cv[0])], []
    res = _rows(fn, [x], [g.reshape(1, -1)], [(x.shape[1], MXU)], tile=512, name=name, comm=comm)
    return res[0] if comm is None else (res[0][0], res[1])


def _rms_bwd(x, g, dh, dres, name):
    def fn(rv, hv, cv):
        xb, dhb, drb = rv
        _, vjp = jax.vjp(_rms, xb, cv[0])
        dx, _ = vjp(dhb)
        rstd = lax.rsqrt(jnp.mean(xb * xb, axis=-1, keepdims=True) + NORM_EPS)
        return [drb + dx], [_colsum8(dhb * xb * rstd)]
    d = x.shape[1]
    return _rows(fn, [x, dh, dres], [g.reshape(1, -1)], [(d, F32)], [(8, d)], tile=512, name=name)


def _slope_dist(hp, hh, dist, dil):
    hf = (2 * hp + hh + 1).astype(F32)
    slope = jnp.exp(jnp.zeros(dist.shape, F32) - hf * LN2)
    return slope * (dist.astype(F32) * float(dil))


def _att_delta(datt, att, name):
    def fn(rv, hv, cv):
        r = lax.broadcasted_iota(jnp.int32, (ATT_W, ATT_W), 0) // HEAD_DIM
        c = lax.broadcasted_iota(jnp.int32, (ATT_W, ATT_W), 1) // HEAD_DIM
        ones = (r == c).astype(F32)
        return [_nn(rv[0] * rv[1], ones, HI)], []
    return _rows(fn, [datt, att], [], [(ATT_W, F32)], tile=512, name=name)[0]


ATT_G = 2048


def _deinterleave(dst, src, dil, ld, region, offset):
    for r in range(dil):
        rows = pl.ds(r, ld, stride=dil) if dil > 1 else pl.ds(0, ld)
        dst[r * region + offset:r * region + offset + ld, :] = src[rows, :]


def _deinterleave_edge(dst, src, dil, region, offset, first_row):
    for r in range(dil):
        rows = pl.ds(first_row + r, BLK, stride=dil) if dil > 1 else pl.ds(first_row, BLK)
        dst[r * region + offset:r * region + offset + BLK, :] = src[rows, :]


def _att_fwd_fused(proj, name, comm=None):
    s, npc = proj.shape
    gsz = ATT_G
    ng = s // gsz
    assert s % gsz == 0
    scale = HEAD_DIM ** -0.5
    comm = comm or _Comm()

    def body(*refs):
        (q_ref, kp_ref, kc_ref, vp_ref, vc_ref, att_ref, lse_ref, attb_ref, qd, kd, vd, nd, md, dd, nn, mn, dn), cm = comm.split(refs, 5, 3, 9)
        hp, g = pl.program_id(0), pl.program_id(1)
        comm.start_at((hp == 0) & (g == 0), cm)
        lane = lax.broadcasted_iota(jnp.int32, (BLK, BLK), 1)
        qi = lax.broadcasted_iota(jnp.int32, (BLK, 2 * BLK), 0)
        ki = lax.broadcasted_iota(jnp.int32, (BLK, 2 * BLK), 1)
        dist = BLK + qi - ki
        band = (dist >= 0) & (dist <= BLK)
        for pi, (_, dil) in enumerate(ATT_PATTERNS):
            ld = gsz // dil
            nbg = ld // BLK
            _deinterleave(qd, q_ref, dil, ld, ld, 0)
            _deinterleave(kd, kc_ref, dil, ld, ld + BLK, BLK)
            _deinterleave(vd, vc_ref, dil, ld, ld + BLK, BLK)
            _deinterleave_edge(kd, kp_ref, dil, ld + BLK, 0, gsz - BLK * dil)
            _deinterleave_edge(vd, vp_ref, dil, ld + BLK, 0, gsz - BLK * dil)
            bias = [_slope_dist(hp, hh, dist, dil) for hh in (0, 1)]

            def tile(t, carry, ld=ld, nbg=nbg, bias=bias):
                r, b = t // nbg, t % nbg
                qo = pl.multiple_of(r * ld + b * BLK, BLK)
                ko = pl.multiple_of(r * (ld + BLK) + b * BLK, BLK)
                q = qd[pl.ds(qo, BLK), :]
                kk = kd[pl.ds(ko, 2 * BLK), :].astype(MXU)
                vv = vd[pl.ds(ko, 2 * BLK), :].astype(MXU)
                valid = band & ((g > 0) | (b > 0) | (ki >= BLK))
                num = jnp.zeros((BLK, BLK), F32)
                mx = jnp.zeros((BLK, BLK), F32)
                den = jnp.zeros((BLK, BLK), F32)
                for hh in (0, 1):
                    hmask = (lane < HEAD_DIM) if hh == 0 else (lane >= HEAD_DIM)
                    qm = jnp.where(hmask, q, 0.0).astype(MXU)
                    sc = jnp.where(valid, _nt(qm, kk) * scale - bias[hh], NEG)
                    m = jnp.max(sc, axis=1, keepdims=True)
                    p = jnp.exp(sc - m)
                    dn_ = jnp.sum(p, axis=1, keepdims=True)
                    o = _nn(p.astype(MXU), vv)
                    num = jnp.where(hmask, o, num)
                    mx = jnp.where(hmask, m, mx)
                    den = jnp.where(hmask, dn_, den)
                nd[pl.ds(qo, BLK), :] = num
                md[pl.ds(qo, BLK), :] = mx
                dd[pl.ds(qo, BLK), :] = den
                return carry

            lax.fori_loop(0, dil * nbg, tile, 0, unroll=4)
            for r in range(dil):
                rows = pl.ds(r, ld, stride=dil) if dil > 1 else pl.ds(0, ld)
                nn.at[pi][rows, :] = nd[r * ld:(r + 1) * ld, :]
                mn.at[pi][rows, :] = md[r * ld:(r + 1) * ld, :]
                dn.at[pi][rows, :] = dd[r * ld:(r + 1) * ld, :]

        def merge(c, carry):
            rows = pl.ds(pl.multiple_of(c * 256, 256), 256)
            ms = [mn[pi, rows, :] for pi in range(len(ATT_PATTERNS))]
            m_all = functools.reduce(jnp.maximum, ms)
            num = jnp.zeros((256, BLK), F32)
            den = jnp.zeros((256, BLK), F32)
            for pi in range(len(ATT_PATTERNS)):
                e = jnp.exp(ms[pi] - m_all)
                num = num + nn[pi, rows, :] * e
                den = den + dn[pi, rows, :] * e
            att = num / den
            att_ref[rows, :] = att
            attb_ref[rows, :] = att.astype(MXU)
            lse_ref[rows, :] = m_all + jnp.log(den)
            return carry

        lax.fori_loop(0, gsz // 256, merge, 0)
        comm.wait_at((hp == 3) & (g == ng - 1), cm)

    def cur(base):
        return pl.BlockSpec((gsz, BLK), lambda hp, g: (g, base // BLK + hp))

    def prev(base):
        return pl.BlockSpec((gsz, BLK), lambda hp, g: (jnp.maximum(g - 1, 0), base // BLK + hp))

    o_spec = pl.BlockSpec((gsz, BLK), lambda hp, g: (g, hp))
    npat = len(ATT_PATTERNS)
    res = pl.pallas_call(
        body, name=name, grid=(4, ng),
        in_specs=[cur(C_Q), prev(C_K), cur(C_K), prev(C_V), cur(C_V)] + [ANY] * comm.n,
        out_specs=[o_spec] * 3 + [ANY] * comm.n,
        out_shape=[jax.ShapeDtypeStruct((s, ATT_W), F32)] * 2 + [jax.ShapeDtypeStruct((s, ATT_W), MXU)] + comm.out_shape(),
        scratch_shapes=[pltpu.VMEM((gsz, BLK), F32), pltpu.VMEM((2 * gsz, BLK), F32), pltpu.VMEM((2 * gsz, BLK), F32)]
        + [pltpu.VMEM((gsz, BLK), F32)] * 3 + [pltpu.VMEM((npat, gsz, BLK), F32)] * 3 + comm.scratch(),
        compiler_params=_cp("arbitrary", "arbitrary"),
    )(proj, proj, proj, proj, proj, *comm.args())
    return res[0], res[1], res[2], list(res[3:])


def _att_bwd_fused(proj, datt, lse, delta, name, comm=None):
    s, npc = proj.shape
    gsz = ATT_G
    ng = s // gsz
    scale = HEAD_DIM ** -0.5
    comm = comm or _Comm()

    def body(*refs):
        (qc_ref, qn_ref, kp_ref, kc_ref, vp_ref, vc_ref, doc_ref, don_ref, lsc_ref, lsn_ref, dlc_ref, dln_ref,
         dq_ref, dk_ref, dv_ref, qd, dod, lsd, dld, kd, vd, dqd, dkd, dvd), cm = comm.split(refs, 12, 3, 9)
        hp, g = pl.program_id(0), pl.program_id(1)
        comm.start_at((hp == 0) & (g == 0), cm)
        lane = lax.broadcasted_iota(jnp.int32, (BLK, BLK), 1)
        qi = lax.broadcasted_iota(jnp.int32, (BLK, BLK), 0)
        ki = lax.broadcasted_iota(jnp.int32, (BLK, BLK), 1)
        d_far = BLK + qi - ki
        d_near = qi - ki
        for pi, (_, dil) in enumerate(ATT_PATTERNS):
            ld = gsz // dil
            nbg = ld // BLK
            reg = ld + BLK
            for dst, c_ref, n_ref in ((qd, qc_ref, qn_ref), (dod, doc_ref, don_ref), (lsd, lsc_ref, lsn_ref), (dld, dlc_ref, dln_ref)):
                _deinterleave(dst, c_ref, dil, ld, reg, 0)
                _deinterleave_edge(dst, n_ref, dil, reg, ld, 0)
            for dst, p_ref, c_ref in ((kd, kp_ref, kc_ref), (vd, vp_ref, vc_ref)):
                _deinterleave(dst, c_ref, dil, ld, reg, BLK)
                _deinterleave_edge(dst, p_ref, dil, reg, 0, gsz - BLK * dil)
            b_far = [_slope_dist(hp, hh, d_far, dil) for hh in (0, 1)]
            b_near = [_slope_dist(hp, hh, d_near, dil) for hh in (0, 1)]

            def tile(t, carry, ld=ld, nbg=nbg, reg=reg, b_far=b_far, b_near=b_near):
                r, b = t // nbg, t % nbg
                oo = pl.multiple_of(r * ld + b * BLK, BLK)
                ro = pl.multiple_of(r * reg + b * BLK, BLK)
                qn, qx = qd[pl.ds(ro, BLK), :], qd[pl.ds(ro + BLK, BLK), :]
                don, dox = dod[pl.ds(ro, BLK), :], dod[pl.ds(ro + BLK, BLK), :]
                lsn, lsx = lsd[pl.ds(ro, BLK), :], lsd[pl.ds(ro + BLK, BLK), :]
                dln, dlx = dld[pl.ds(ro, BLK), :], dld[pl.ds(ro + BLK, BLK), :]
                kp, kc = kd[pl.ds(ro, BLK), :].astype(MXU), kd[pl.ds(ro + BLK, BLK), :].astype(MXU)
                vp, vc = vd[pl.ds(ro, BLK), :].astype(MXU), vd[pl.ds(ro + BLK, BLK), :].astype(MXU)
                ok_a = (d_far <= BLK) & ((g > 0) | (b > 0))
                ok_b = d_near >= 0
                ok_c = (d_far <= BLK) & ((g < ng - 1) | (b < nbg - 1))

                def grads(qm, dom, k, v, ls, dl, bias, valid, hh):
                    c0 = hh * HEAD_DIM
                    sc = _nt(qm, k) * scale - bias
                    p = jnp.exp(jnp.where(valid, sc - ls[:, c0:c0 + 1], NEG))
                    ds = p * (_nt(dom, v) - dl[:, c0:c0 + 1])
                    return p.astype(MXU), ds.astype(MXU)

                dq = jnp.zeros((BLK, BLK), F32)
                dk = jnp.zeros((BLK, BLK), F32)
                dv = jnp.zeros((BLK, BLK), F32)
                for hh in (0, 1):
                    hmask = (lane < HEAD_DIM) if hh == 0 else (lane >= HEAD_DIM)
                    qnm = jnp.where(hmask, qn, 0.0).astype(MXU)
                    qxm = jnp.where(hmask, qx, 0.0).astype(MXU)
                    donm = jnp.where(hmask, don, 0.0).astype(MXU)
                    doxm = jnp.where(hmask, dox, 0.0).astype(MXU)
                    _, ds_a = grads(qnm, donm, kp, vp, lsn, dln, b_far[hh], ok_a, hh)
                    p_b, ds_b = grads(qnm, donm, kc, vc, lsn, dln, b_near[hh], ok_b, hh)
                    p_c, ds_c = grads(qxm, doxm, kc, vc, lsx, dlx, b_far[hh], ok_c, hh)
                    dq = jnp.where(hmask, _nn(ds_a, kp) + _nn(ds_b, kc), dq)
                    dk = dk + _tn(ds_b, qnm) + _tn(ds_c, qxm)
                    dv = dv + _tn(p_b, donm) + _tn(p_c, doxm)
                dqd[pl.ds(oo, BLK), :] = dq * scale
                dkd[pl.ds(oo, BLK), :] = dk * scale
                dvd[pl.ds(oo, BLK), :] = dv
                return carry

            lax.fori_loop(0, dil * nbg, tile, 0, unroll=4)
            for out, src in ((dq_ref, dqd), (dk_ref, dkd), (dv_ref, dvd)):
                for r in range(dil):
                    rows = pl.ds(r, ld, stride=dil) if dil > 1 else pl.ds(0, ld)
                    if pi == 0:
                        out[rows, :] = src[r * ld:(r + 1) * ld, :]
                    else:
                        out[rows, :] = out[rows, :] + src[r * ld:(r + 1) * ld, :]
        comm.wait_at((hp == 3) & (g == ng - 1), cm)

    def pspec(base, shift):
        return pl.BlockSpec((gsz, BLK), lambda hp, g: (jnp.clip(g + shift, 0, ng - 1), base // BLK + hp))

    def wspec(shift):
        return pl.BlockSpec((gsz, BLK), lambda hp, g: (jnp.clip(g + shift, 0, ng - 1), hp))

    in_specs = [pspec(C_Q, 0), pspec(C_Q, 1), pspec(C_K, -1), pspec(C_K, 0), pspec(C_V, -1), pspec(C_V, 0),
                wspec(0), wspec(1), wspec(0), wspec(1), wspec(0), wspec(1)] + [ANY] * comm.n
    res = pl.pallas_call(
        body, name=name, grid=(4, ng), in_specs=in_specs,
        out_specs=[wspec(0)] * 3 + [ANY] * comm.n,
        out_shape=[jax.ShapeDtypeStruct((s, ATT_W), F32)] * 3 + comm.out_shape(),
        scratch_shapes=[pltpu.VMEM((2 * gsz, BLK), F32)] * 6 + [pltpu.VMEM((gsz, BLK), F32)] * 3 + comm.scratch(),
        compiler_params=_cp("arbitrary", "arbitrary"),
    )(proj, proj, proj, proj, proj, proj, datt, datt, lse, lse, delta, delta, *comm.args())
    return res[0], res[1], res[2], list(res[3:])


def _att_bwd_rev(proj, datt, lse, delta, name, comm=None):
    s, npc = proj.shape
    gsz = ATT_G
    ng = s // gsz
    npat = len(ATT_PATTERNS)
    scale = HEAD_DIM ** -0.5
    comm = comm or _Comm()

    def body(*refs):
        (q_ref, kp_ref, kc_ref, vp_ref, vc_ref, do_ref, ls_ref, dl_ref, dq_out, dk_out, dv_out,
         qd, dod, lsd, dld, kd, vd, dqd, dkc, dvc, dkp, dvp, kcar, vcar, dq_ref, dk_ref, dv_ref), cm = comm.split(refs, 8, 3, 16)
        hp, gi = pl.program_id(0), pl.program_id(1)
        g = ng - 1 - gi
        comm.start_at((hp == 0) & (gi == 0), cm)

        @pl.when(gi == 0)
        def _():
            kcar[...] = jnp.zeros_like(kcar)
            vcar[...] = jnp.zeros_like(vcar)

        lane = lax.broadcasted_iota(jnp.int32, (BLK, BLK), 1)
        qi = lax.broadcasted_iota(jnp.int32, (BLK, 2 * BLK), 0)
        ki = lax.broadcasted_iota(jnp.int32, (BLK, 2 * BLK), 1)
        dist = BLK + qi - ki
        band = (dist >= 0) & (dist <= BLK)
        for pi, (_, dil) in enumerate(ATT_PATTERNS):
            ld = gsz // dil
            nbg = ld // BLK
            reg = ld + BLK
            for dst, src in ((qd, q_ref), (dod, do_ref), (lsd, ls_ref), (dld, dl_ref)):
                _deinterleave(dst, src, dil, ld, ld, 0)
            for dst, p_ref, c_ref in ((kd, kp_ref, kc_ref), (vd, vp_ref, vc_ref)):
                _deinterleave(dst, c_ref, dil, ld, reg, BLK)
                _deinterleave_edge(dst, p_ref, dil, reg, 0, gsz - BLK * dil)
            bias = [_slope_dist(hp, hh, dist, dil) for hh in (0, 1)]

            def tile(t, carry, ld=ld, nbg=nbg, reg=reg, bias=bias):
                r, b = t // nbg, t % nbg
                oo = pl.multiple_of(r * ld + b * BLK, BLK)
                ko = pl.multiple_of(r * reg + b * BLK, BLK)
                q, do = qd[pl.ds(oo, BLK), :], dod[pl.ds(oo, BLK), :]
                ls, dl = lsd[pl.ds(oo, BLK), :], dld[pl.ds(oo, BLK), :]
                kk = kd[pl.ds(ko, 2 * BLK), :].astype(MXU)
                vv = vd[pl.ds(ko, 2 * BLK), :].astype(MXU)
                valid = band & ((g > 0) | (b > 0) | (ki >= BLK))
                dq = jnp.zeros((BLK, BLK), F32)
                dkk = jnp.zeros((2 * BLK, BLK), F32)
                dvv = jnp.zeros((2 * BLK, BLK), F32)
                for hh in (0, 1):
                    c0 = hh * HEAD_DIM
                    hmask = (lane < HEAD_DIM) if hh == 0 else (lane >= HEAD_DIM)
                    qm = jnp.where(hmask, q, 0.0).astype(MXU)
                    dom = jnp.where(hmask, do, 0.0).astype(MXU)
                    sc = _nt(qm, kk) * scale - bias[hh]
                    p = jnp.exp(jnp.where(valid, sc - ls[:, c0:c0 + 1], NEG))
                    ds = (p * (_nt(dom, vv) - dl[:, c0:c0 + 1])).astype(MXU)
                    dq = jnp.where(hmask, _nn(ds, kk), dq)
                    dkk = dkk + _tn(ds, qm)
                    dvv = dvv + _tn(p.astype(MXU), dom)
                dqd[pl.ds(oo, BLK), :] = dq * scale
                dkp[pl.ds(oo, BLK), :] = dkk[:BLK] * scale
                dkc[pl.ds(oo, BLK), :] = dkk[BLK:] * scale
                dvp[pl.ds(oo, BLK), :] = dvv[:BLK]
                dvc[pl.ds(oo, BLK), :] = dvv[BLK:]
                return carry

            lax.fori_loop(0, dil * nbg, tile, 0, unroll=4)
            for r in range(dil):
                rows = pl.ds(r, ld, stride=dil) if dil > 1 else pl.ds(0, ld)
                lo, hi = r * ld, (r + 1) * ld
                edge = slice(pi * gsz + r * BLK, pi * gsz + (r + 1) * BLK)
                for out, cur, prv, car in ((dk_ref, dkc, dkp, kcar), (dv_ref, dvc, dvp, vcar)):
                    later = car[edge, :] if nbg == 1 else jnp.concatenate([prv[lo + BLK:hi, :], car[edge, :]], axis=0)
                    total = cur[lo:hi, :] + later
                    car[edge, :] = prv[lo:lo + BLK, :]
                    out[rows, :] = total if pi == 0 else out[rows, :] + total
                dq_ref[rows, :] = dqd[lo:hi, :] if pi == 0 else dq_ref[rows, :] + dqd[lo:hi, :]
        for out, acc in ((dq_out, dq_ref), (dk_out, dk_ref), (dv_out, dv_ref)):
            out[...] = acc[...].astype(out.dtype)
        comm.wait_at((hp == 3) & (gi == ng - 1), cm)

    def pspec(base, shift):
        return pl.BlockSpec((gsz, BLK), lambda hp, gi: (jnp.maximum(ng - 1 - gi + shift, 0), base // BLK + hp))

    wspec = pl.BlockSpec((gsz, BLK), lambda hp, gi: (ng - 1 - gi, hp))
    in_specs = [pspec(C_Q, 0), pspec(C_K, -1), pspec(C_K, 0), pspec(C_V, -1), pspec(C_V, 0), wspec, wspec, wspec] + [ANY] * comm.n
    res = pl.pallas_call(
        body, name=name, grid=(4, ng), in_specs=in_specs,
        out_specs=[wspec] * 3 + [ANY] * comm.n,
        out_shape=[jax.ShapeDtypeStruct((s, ATT_W), MXU)] * 3 + comm.out_shape(),
        scratch_shapes=[pltpu.VMEM((gsz, BLK), F32)] * 4 + [pltpu.VMEM((2 * gsz, BLK), F32)] * 2
        + [pltpu.VMEM((gsz, BLK), F32)] * 5 + [pltpu.VMEM((npat * gsz, BLK), F32)] * 2 + [pltpu.VMEM((gsz, BLK), F32)] * 3
        + comm.scratch(),
        compiler_params=_cp("arbitrary", "arbitrary"),
    )(proj, proj, proj, proj, proj, datt, lse, delta, *comm.args())
    return res[0], res[1], res[2], list(res[3:])


def _shift_down(cur, halo, sft):
    if sft == 0:
        return cur
    t = cur.shape[0]
    rolled = pltpu.roll(cur, sft, 0)
    hr = pltpu.roll(halo, sft, 0)
    row = lax.broadcasted_iota(jnp.int32, cur.shape, 0)
    return jnp.where(row < sft, jnp.tile(hr, (t // 8, 1)), rolled)


def _shift_up(cur, halo, sft):
    if sft == 0:
        return cur
    t = cur.shape[0]
    rolled = pltpu.roll(cur, t - sft, 0)
    hr = pltpu.roll(halo, 8 - sft, 0)
    row = lax.broadcasted_iota(jnp.int32, cur.shape, 0)
    return jnp.where(row >= t - sft, jnp.tile(hr, (t // 8, 1)), rolled)


def _conv(x, xh, w, b):
    y = b + x * w[CONV_K - 1:CONV_K]
    for k in range(CONV_K - 1):
        y = y + _shift_down(x, xh, CONV_K - 1 - k) * w[k:k + 1]
    return y


def _conv_bwd(x, xh, dy, dyh, w):
    dx = dy * w[CONV_K - 1:CONV_K]
    dws = []
    for k in range(CONV_K - 1):
        sft = CONV_K - 1 - k
        dx = dx + _shift_up(dy, dyh, sft) * w[k:k + 1]
        dws.append(jnp.sum(dy * _shift_down(x, xh, sft), axis=0, keepdims=True))
    dws.append(jnp.sum(dy * x, axis=0, keepdims=True))
    c = x.shape[1]
    dw = jnp.concatenate(dws + [jnp.zeros((8 - CONV_K, c), F32)], axis=0)
    return dx, dw, jnp.sum(dy, axis=0, keepdims=True)


def _pad8(w):
    return jnp.concatenate([w, jnp.zeros((8 - w.shape[0], w.shape[1]), w.dtype)], axis=0)


def _ssd_pre(proj, conv_w, conv_b, dt_bias128, name):
    def fn(rv, hv, cv):
        xbc, dtr = rv
        return [_silu(_conv(xbc, hv[0], cv[0], cv[1])), _softplus(dtr + cv[2])], []
    return _rows(fn, [(proj, 1024, C_XBC // 1024), (proj, BLK, C_DT // BLK)],
                 [_pad8(conv_w), conv_b.reshape(1, -1), dt_bias128],
                 [(1024, F32), (BLK, F32)], tile=256, name=name, halos=[(0, "prev")])


def _ssd_pre_bwd(proj, dxc, ddt, conv_w, conv_b, dt_bias128, name):
    def fn(rv, hv, cv):
        xbc, dtr, dxcb, ddtb = rv
        xh, dxch_raw, xnext = hv
        w, b, bias = cv
        pre = _conv(xbc, xh, w, b)
        sg = _sigmoid(pre)
        dpre = dxcb * (sg * (1.0 + pre * (1.0 - sg)))
        t = xbc.shape[0]
        tail = jnp.concatenate([xbc[t - 8:], xnext], axis=0)
        pre_n = _conv(tail[8:], tail[:8], w, b)
        sgn = _sigmoid(pre_n)
        dpre_h = dxch_raw * (sgn * (1.0 + pre_n * (1.0 - sgn)))
        dx, dw, db = _conv_bwd(xbc, xh, dpre, dpre_h, w)
        ddr = ddtb * _sigmoid(dtr + bias)
        return [dx, ddr], [dw, jnp.concatenate([db, jnp.zeros((7, db.shape[1]), F32)], axis=0), _colsum8(ddr)]
    return _rows(fn, [(proj, 1024, C_XBC // 1024), (proj, BLK, C_DT // BLK), dxc, ddt],
                 [_pad8(conv_w), conv_b.reshape(1, -1), dt_bias128],
                 [(1024, MXU), (BLK, MXU)], [(8, 1024), (8, 1024), (8, BLK)], tile=256, name=name,
                 halos=[(0, "prev"), (2, "next"), (0, "next")])


def _head_cols(v, h0):
    lane = lax.broadcasted_iota(jnp.int32, (v.shape[0], BLK), 1)
    return jnp.where(lane < HEAD_DIM, v[:, h0:h0 + 1], v[:, h0 + 1:h0 + 2])


def _ssd_scan(xc, dt, par, name):
    s = xc.shape[0]
    nc = s // BLK

    def body(x_ref, dt_ref, par_ref, y_ref, st_ref, h_ref):
        c = pl.program_id(0)

        @pl.when(c == 0)
        def _():
            h_ref[...] = jnp.zeros_like(h_ref)

        st_ref[0] = h_ref[...]
        dt = dt_ref[...]
        a_row = -jnp.exp(par_ref[0:1, :])
        d_row = par_ref[1:2, :]
        ri = lax.broadcasted_iota(jnp.int32, (BLK, BLK), 0)
        ci = lax.broadcasted_iota(jnp.int32, (BLK, BLK), 1)
        tril = ri >= ci
        cs = _nn(tril.astype(F32), dt * a_row, HI)
        cst, dtt = cs.T, dt.T
        last = cs[BLK - 1:BLK, :]
        wcol = jnp.exp(last - cs) * dt
        ecs = jnp.exp(cs)
        elast = jnp.exp(last)
        for g in (0, 1):
            bg = x_ref[:, 512 + g * BLK:512 + (g + 1) * BLK].astype(MXU)
            cg = x_ref[:, 768 + g * BLK:768 + (g + 1) * BLK].astype(MXU)
            gm = _nt(cg, bg)
            for pp in (0, 1):
                pr = 2 * g + pp
                h0 = 2 * pr
                x2 = x_ref[:, pr * BLK:(pr + 1) * BLK]
                hprev = h_ref[pr * BLK:(pr + 1) * BLK, :]
                yp = jnp.zeros((BLK, BLK), F32)
                for hh in (0, 1):
                    h = h0 + hh
                    hmask = (ci < HEAD_DIM) if hh == 0 else (ci >= HEAD_DIM)
                    lm = jnp.exp(jnp.where(tril, cs[:, h:h + 1] - cst[h:h + 1, :], NEG))
                    mm = gm * lm * dtt[h:h + 1, :]
                    yp = yp + _nn(mm.astype(MXU), jnp.where(hmask, x2, 0.0).astype(MXU))
                y0 = _nt(cg, hprev.astype(MXU))
                y_ref[:, pr * BLK:(pr + 1) * BLK] = yp + _head_cols(ecs, h0) * y0 + _head_cols(d_row, h0) * x2
                dec = jnp.where(ri < HEAD_DIM, elast[:, h0:h0 + 1], elast[:, h0 + 1:h0 + 2])
                xw = (x2 * _head_cols(wcol, h0)).astype(MXU)
                h_ref[pr * BLK:(pr + 1) * BLK, :] = dec * hprev + _tn(xw, bg)

    return pl.pallas_call(
        body, name=name, grid=(nc,),
        in_specs=[pl.BlockSpec((BLK, 1024), lambda c: (c, 0)), pl.BlockSpec((BLK, BLK), lambda c: (c, 0)),
                  pl.BlockSpec((8, BLK), lambda c: (0, 0))],
        out_specs=[pl.BlockSpec((BLK, SSD_W), lambda c: (c, 0)), pl.BlockSpec((1, SSD_W, SSD_STATE), lambda c: (c, 0, 0))],
        out_shape=[jax.ShapeDtypeStruct((s, SSD_W), F32), jax.ShapeDtypeStruct((nc, SSD_W, SSD_STATE), F32)],
        scratch_shapes=[pltpu.VMEM((SSD_W, SSD_STATE), F32)],
        compiler_params=_cp("arbitrary"),
    )(xc, dt, par)


def _ssd_scan_bwd(xc, dt, par, st, dy, name, comm=None):
    s = xc.shape[0]
    nc = s // BLK
    comm = comm or _Comm()

    def body(*refs):
        (x_ref, dt_ref, par_ref, st_ref, dy_ref, dx_ref, ddt_ref, dal_ref, dd_ref, dh_ref), cm = comm.split(refs, 5, 4, 1)
        c = pl.program_id(0)
        comm.start_at(c == 0, cm)

        @pl.when(c == 0)
        def _():
            dh_ref[...] = jnp.zeros_like(dh_ref)
            dal_ref[...] = jnp.zeros_like(dal_ref)
            dd_ref[...] = jnp.zeros_like(dd_ref)

        dt = dt_ref[...]
        a_row = -jnp.exp(par_ref[0:1, :])
        d_row = par_ref[1:2, :]
        ri = lax.broadcasted_iota(jnp.int32, (BLK, BLK), 0)
        ci = lax.broadcasted_iota(jnp.int32, (BLK, BLK), 1)
        tril = ri >= ci
        cs = _nn(tril.astype(F32), dt * a_row, HI)
        cst, dtt = cs.T, dt.T
        last = cs[BLK - 1:BLK, :]
        tolast = jnp.exp(last - cs)
        wcol = tolast * dt
        ecs = jnp.exp(cs)
        elast = jnp.exp(last)
        dcs_col = jnp.zeros((BLK, BLK), F32)
        ddt_col = jnp.zeros((BLK, BLK), F32)
        dcs_row = jnp.zeros((BLK, BLK), F32)
        ddt_row = jnp.zeros((BLK, BLK), F32)
        dlast = jnp.zeros((1, BLK), F32)
        ddsk = jnp.zeros((1, BLK), F32)
        for g in (0, 1):
            bg32 = x_ref[:, 512 + g * BLK:512 + (g + 1) * BLK]
            cg32 = x_ref[:, 768 + g * BLK:768 + (g + 1) * BLK]
            bg, cg = bg32.astype(MXU), cg32.astype(MXU)
            gm = _nt(cg, bg)
            dgm = jnp.zeros((BLK, BLK), F32)
            dbg = jnp.zeros((BLK, BLK), F32)
            dcg = jnp.zeros((BLK, BLK), F32)
            for pp in (0, 1):
                pr = 2 * g + pp
                h0 = 2 * pr
                x2 = x_ref[:, pr * BLK:(pr + 1) * BLK]
                dy2 = dy_ref[:, pr * BLK:(pr + 1) * BLK]
                hprev = st_ref[0, pr * BLK:(pr + 1) * BLK, :]
                dhn = dh_ref[pr * BLK:(pr + 1) * BLK, :]
                x2m, dhnm = x2.astype(MXU), dhn.astype(MXU)
                zb = _nt(bg, dhnm)
                y0 = _nt(cg, hprev.astype(MXU))
                esel = _head_cols(ecs, h0)
                wsel = _head_cols(wcol, h0)
                dx2 = _head_cols(d_row, h0) * dy2 + wsel * zb
                r_off = dy2 * y0
                r_w = x2 * zb
                r_d = dy2 * x2
                r_h = dhn * hprev
                for hh in (0, 1):
                    h = h0 + hh
                    hmask = (ci < HEAD_DIM) if hh == 0 else (ci >= HEAD_DIM)
                    onl = (ci == h).astype(F32)
                    ons = (ri == h).astype(F32)
                    dym = jnp.where(hmask, dy2, 0.0).astype(MXU)
                    dt_r = dtt[h:h + 1, :]
                    lm = jnp.exp(jnp.where(tril, cs[:, h:h + 1] - cst[h:h + 1, :], NEG))
                    mm = gm * lm * dt_r
                    dx2 = dx2 + _tn(mm.astype(MXU), dym)
                    dm = _nt(dym, x2m)
                    t1 = dm * lm
                    dgm = dgm + t1 * dt_r
                    tt = t1 * gm
                    ddt_row = ddt_row + ons * jnp.sum(tt, axis=0, keepdims=True)
                    t = tt * dt_r
                    dcs_col = dcs_col + onl * jnp.sum(t, axis=1, keepdims=True)
                    dcs_row = dcs_row - ons * jnp.sum(t, axis=0, keepdims=True)
                    de = jnp.sum(jnp.where(hmask, r_off, 0.0), axis=1, keepdims=True)
                    dcs_col = dcs_col + onl * (ecs[:, h:h + 1] * de)
                    hrow = (ri < HEAD_DIM) if hh == 0 else (ri >= HEAD_DIM)
                    dl_h = elast[:, h:h + 1] * jnp.sum(jnp.where(hrow, r_h, 0.0), keepdims=True)
                    dw = jnp.sum(jnp.where(hmask, r_w, 0.0), axis=1, keepdims=True)
                    ddt_col = ddt_col + onl * (dw * tolast[:, h:h + 1])
                    v = dw * wcol[:, h:h + 1]
                    dcs_col = dcs_col - onl * v
                    dl_h = dl_h + jnp.sum(v, keepdims=True)
                    dlast = dlast + onl[0:1, :] * dl_h
                    ddsk = ddsk + onl[0:1, :] * jnp.sum(jnp.where(hmask, r_d, 0.0), keepdims=True)
                dx_ref[:, pr * BLK:(pr + 1) * BLK] = dx2
                edy = (esel * dy2).astype(MXU)
                dcg = dcg + _nn(edy, hprev.astype(MXU))
                dec = jnp.where(ri < HEAD_DIM, elast[:, h0:h0 + 1], elast[:, h0 + 1:h0 + 2])
                dh_ref[pr * BLK:(pr + 1) * BLK, :] = dec * dhn + _tn(edy, cg)
                dbg = dbg + _nn((x2 * wsel).astype(MXU), dhnm)
            dgmm = dgm.astype(MXU)
            dx_ref[:, 512 + g * BLK:512 + (g + 1) * BLK] = dbg + _tn(dgmm, cg)
            dx_ref[:, 768 + g * BLK:768 + (g + 1) * BLK] = dcg + _nn(dgmm, bg)
        dcs = dcs_col + dcs_row.T + jnp.where(ri == BLK - 1, dlast, 0.0)
        dda = _nn((ri <= ci).astype(F32), dcs, HI)
        ddt_ref[...] = ddt_col + ddt_row.T + a_row * dda
        da = jnp.sum(dt * dda, axis=0, keepdims=True)
        dal_ref[0:1, :] += da * a_row
        dd_ref[0:1, :] += ddsk
        comm.wait_at(c == nc - 1, cm)

    rev = lambda c: (nc - 1 - c, 0)
    res = pl.pallas_call(
        body, name=name, grid=(nc,),
        in_specs=[pl.BlockSpec((BLK, 1024), rev), pl.BlockSpec((BLK, BLK), rev), pl.BlockSpec((8, BLK), lambda c: (0, 0)),
                  pl.BlockSpec((1, SSD_W, SSD_STATE), lambda c: (nc - 1 - c, 0, 0)), pl.BlockSpec((BLK, SSD_W), rev)]
        + [ANY] * comm.n,
        out_specs=[pl.BlockSpec((BLK, 1024), rev), pl.BlockSpec((BLK, BLK), rev),
                   pl.BlockSpec((8, BLK), lambda c: (0, 0)), pl.BlockSpec((8, BLK), lambda c: (0, 0))] + [ANY] * comm.n,
        out_shape=[jax.ShapeDtypeStruct((s, 1024), F32), jax.ShapeDtypeStruct((s, BLK), F32),
                   jax.ShapeDtypeStruct((8, BLK), F32), jax.ShapeDtypeStruct((8, BLK), F32)] + comm.out_shape(),
        scratch_shapes=[pltpu.VMEM((SSD_W, SSD_STATE), F32)] + comm.scratch(),
        compiler_params=_cp("arbitrary"),
    )(xc, dt, par, st, dy, *comm.args())
    return res[0], res[1], res[2], res[3], list(res[4:])


def _ssd_gate(y, z, w):
    t = y * _silu(z)
    outs = []
    for g in (0, 1):
        tg = t[:, g * 256:(g + 1) * 256]
        outs.append(tg * lax.rsqrt(jnp.mean(tg * tg, axis=-1, keepdims=True) + SSD_NORM_EPS))
    return jnp.concatenate(outs, axis=1) * w


def _ssd_post(y, proj, norm_w, name):
    def fn(rv, hv, cv):
        return [_ssd_gate(rv[0], rv[1], cv[0])], []
    return _rows(fn, [y, (proj, SSD_W, C_Z // SSD_W)], [norm_w.reshape(1, -1)], [(SSD_W, MXU)], tile=512, name=name)[0]


def _ssd_post_bwd(y, proj, norm_w, dout, name):
    def fn(rv, hv, cv):
        yb, zb, db = rv
        _, vjp = jax.vjp(lambda a, b: _ssd_gate(a, b, cv[0]), yb, zb)
        dy, dz = vjp(db)
        t = yb * _silu(zb)
        nrm = []
        for g in (0, 1):
            tg = t[:, g * 256:(g + 1) * 256]
            nrm.append(tg * lax.rsqrt(jnp.mean(tg * tg, axis=-1, keepdims=True) + SSD_NORM_EPS))
        return [dy, dz], [_colsum8(db * jnp.concatenate(nrm, axis=1))]
    return _rows(fn, [y, (proj, SSD_W, C_Z // SSD_W), dout], [norm_w.reshape(1, -1)],
                 [(SSD_W, F32), (SSD_W, MXU)], [(8, SSD_W)], tile=512, name=name)


LRU_T = 256


def _lru_conv(proj, conv_w, conv_b, name):
    def fn(rv, hv, cv):
        return [_conv(rv[0], hv[0], cv[0], cv[1])], []
    return _rows(fn, [(proj, LRU_W, C_XL // LRU_W)], [_pad8(conv_w), conv_b.reshape(1, -1)], [(LRU_W, F32)],
                 tile=512, name=name, halos=[(0, "prev")])[0]


def _lru_conv_bwd(proj, dxc, conv_w, name):
    def fn(rv, hv, cv):
        dx, dw, db = _conv_bwd(rv[0], hv[0], rv[1], hv[1], cv[0])
        return [dx], [dw, jnp.concatenate([db, jnp.zeros((7, db.shape[1]), F32)], axis=0)]
    return _rows(fn, [(proj, LRU_W, C_XL // LRU_W), dxc], [_pad8(conv_w)], [(LRU_W, MXU)], [(8, LRU_W), (8, LRU_W)],
                 tile=512, name=name, halos=[(0, "prev"), (1, "next")])


def _lru_au(pre_a, pre_x, xc, ba, bx, lam):
    r = _sigmoid(pre_a + ba)
    i = _sigmoid(pre_x + bx)
    log_a = -LRU_C * r * _softplus(-lam)
    a = jnp.exp(log_a)
    u = jnp.sqrt(1.0 - jnp.exp(2.0 * log_a)) * (i * xc)
    return a, u


def _lru_scan(pre, xc, proj, par, name):
    s = xc.shape[0]
    t = LRU_T

    def body(pre_ref, xc_ref, g_ref, par_ref, out_ref, h_ref, carry):
        c = pl.program_id(0)

        @pl.when(c == 0)
        def _():
            carry[...] = jnp.zeros_like(carry)

        a, u = _lru_au(pre_ref[:, :LRU_W], pre_ref[:, LRU_W:], xc_ref[...], par_ref[0:1, :], par_ref[1:2, :], par_ref[2:3, :])
        row = lax.broadcasted_iota(jnp.int32, (t, LRU_W), 0)
        sft = 1
        while sft < t:
            keep = row >= sft
            a_s = jnp.where(keep, pltpu.roll(a, sft, 0), 1.0)
            u_s = jnp.where(keep, pltpu.roll(u, sft, 0), 0.0)
            u = a * u_s + u
            a = a * a_s
            sft *= 2
        h = a * carry[0:1, :] + u
        h_ref[...] = h
        out_ref[...] = (h * _gelu(g_ref[...])).astype(out_ref.dtype)
        carry[0:1, :] = h[t - 1:t, :]

    return pl.pallas_call(
        body, name=name, grid=(s // t,),
        in_specs=[pl.BlockSpec((t, 2 * LRU_W), lambda c: (c, 0)), pl.BlockSpec((t, LRU_W), lambda c: (c, 0)),
                  pl.BlockSpec((t, LRU_W), lambda c: (c, C_G // LRU_W)), pl.BlockSpec((8, LRU_W), lambda c: (0, 0))],
        out_specs=[pl.BlockSpec((t, LRU_W), lambda c: (c, 0))] * 2,
        out_shape=[jax.ShapeDtypeStruct((s, LRU_W), MXU), jax.ShapeDtypeStruct((s, LRU_W), F32)],
        scratch_shapes=[pltpu.VMEM((8, LRU_W), F32)],
        compiler_params=_cp("arbitrary"),
    )(pre, xc, proj, par)


def _lru_scan_bwd(pre, xc, proj, par, h, dout, name):
    s = xc.shape[0]
    t = LRU_T
    n = s // t
    t8 = t // 8

    def body(pre_ref, xc_ref, g_ref, par_ref, h_ref, hh_ref, do_ref, dpre_ref, dxc_ref, dg_ref, dpar_ref, carry):
        c = pl.program_id(0)

        @pl.when(c == 0)
        def _():
            carry[...] = jnp.zeros_like(carry)
            dpar_ref[...] = jnp.zeros_like(dpar_ref)

        pa, px, xcb = pre_ref[:, :LRU_W], pre_ref[:, LRU_W:], xc_ref[...]
        ba, bx, lam = par_ref[0:1, :], par_ref[1:2, :], par_ref[2:3, :]
        (a, u), vjp = jax.vjp(_lru_au, pa, px, xcb, ba, bx, lam)
        g = g_ref[...]
        hcur = h_ref[...]
        do = do_ref[...]
        _, gvjp = jax.vjp(_gelu, g)
        dg_ref[...] = gvjp(do * hcur)[0].astype(dg_ref.dtype)
        row = lax.broadcasted_iota(jnp.int32, (t, LRU_W), 0)
        v = do * _gelu(g) + jnp.where(row == t - 1, carry[0:1, :], 0.0)
        b = jnp.where(row == t - 1, 0.0, pltpu.roll(a, t - 1, 0))
        sft = 1
        while sft < t:
            keep = row < t - sft
            b_s = jnp.where(keep, pltpu.roll(b, t - sft, 0), 1.0)
            v_s = jnp.where(keep, pltpu.roll(v, t - sft, 0), 0.0)
            v = b * v_s + v
            b = b * b_s
            sft *= 2
        dh = v
        carry[0:1, :] = a[0:1, :] * dh[0:1, :]
        hhalo = jnp.where(c == n - 1, 0.0, hh_ref[...])
        hprev = _shift_down(hcur, hhalo, 1)
        dpa, dpx, dxc, dba, dbx, dlam = vjp((dh * hprev, dh))
        dpre_ref[:, :LRU_W] = dpa
        dpre_ref[:, LRU_W:] = dpx
        dxc_ref[...] = dxc
        dpar_ref[0:1, :] += dba
        dpar_ref[1:2, :] += dbx
        dpar_ref[2:3, :] += dlam

    rev = lambda c: (n - 1 - c, 0)
    return pl.pallas_call(
        body, name=name, grid=(n,),
        in_specs=[pl.BlockSpec((t, 2 * LRU_W), rev), pl.BlockSpec((t, LRU_W), rev),
                  pl.BlockSpec((t, LRU_W), lambda c: (n - 1 - c, C_G // LRU_W)), pl.BlockSpec((8, LRU_W), lambda c: (0, 0)),
                  pl.BlockSpec((t, LRU_W), rev),
                  pl.BlockSpec((8, LRU_W), lambda c: (jnp.maximum((n - 1 - c) * t8 - 1, 0), 0)),
                  pl.BlockSpec((t, LRU_W), lambda c: (n - 1 - c, dout.shape[1] // LRU_W - 1))],
        out_specs=[pl.BlockSpec((t, 2 * LRU_W), rev), pl.BlockSpec((t, LRU_W), rev), pl.BlockSpec((t, LRU_W), rev),
                   pl.BlockSpec((8, LRU_W), lambda c: (0, 0))],
        out_shape=[jax.ShapeDtypeStruct((s, 2 * LRU_W), F32), jax.ShapeDtypeStruct((s, LRU_W), F32),
                   jax.ShapeDtypeStruct((s, LRU_W), MXU), jax.ShapeDtypeStruct((8, LRU_W), F32)],
        scratch_shapes=[pltpu.VMEM((8, LRU_W), F32)],
        compiler_params=_cp("arbitrary"),
    )(pre, xc, proj, par, h, h, dout)


def _swiglu_act(gu, name):
    def fn(rv, hv, cv):
        return [_silu(rv[0]) * rv[1]], []
    return _rows(fn, [(gu, D_FF, 0), (gu, D_FF, 1)], [], [(D_FF, MXU)], tile=256, name=name)[0]


def _swiglu_bwd(gu, da, name):
    def fn(rv, hv, cv):
        gt, up, dab = rv
        sg = _sigmoid(gt)
        dgate = dab * up * (sg * (1.0 + gt * (1.0 - sg)))
        dup = dab * (gt * sg)
        return [jnp.concatenate([dgate, dup], axis=1)], []
    return _rows(fn, [(gu, D_FF, 0), (gu, D_FF, 1), da], [], [(2 * D_FF, MXU)], tile=256, name=name)[0]


def _loss_head(x, g, target, name):
    d = x.shape[1]

    def fn(rv, hv, cv):
        xb, tb = rv
        y, vjp = jax.vjp(_rms, xb, cv[0])
        err = y - tb
        dy = err * (1.0 / d)
        dx, _ = vjp(dy)
        rstd = lax.rsqrt(jnp.mean(xb * xb, axis=-1, keepdims=True) + NORM_EPS)
        e2 = err * err * (0.5 / d)
        e2 = functools.reduce(lambda a, b: a + b, [e2[:, k * BLK:(k + 1) * BLK] for k in range(d // BLK)])
        return [dx], [_colsum8(dy * xb * rstd), _colsum8(e2)]
    return _rows(fn, [x, target], [g.reshape(1, -1)], [(d, F32)], [(8, d), (8, BLK)], tile=512, name=name)


ANY = pl.BlockSpec(memory_space=pl.ANY)


def _coords():
    return lax.axis_index("x"), lax.axis_index("y"), lax.axis_index("c")


class _Comm:
    def __init__(self, gathers=(), scatters=()):
        self.gathers = list(gathers)
        self.scatters = list(scatters)
        self.n = len(self.gathers) + len(self.scatters)

    def args(self):
        return [g[0] for g in self.gathers] + self.scatters

    def out_shape(self):
        out = [jax.ShapeDtypeStruct((4,) + (a.shape if l is None else a.shape[1:]), a.dtype) for a, l, _ in self.gathers]
        return out + [jax.ShapeDtypeStruct((3,) + a.shape[1:], a.dtype) for a in self.scatters]

    def scratch(self):
        if not self.n:
            return []
        return [pltpu.SemaphoreType.DMA((3 * self.n,)), pltpu.SemaphoreType.DMA((3 * self.n,)),
                pltpu.SemaphoreType.DMA((max(len(self.gathers), 1),)),
                pltpu.SemaphoreType.DMA((3 * self.n,)), pltpu.SemaphoreType.DMA((3 * self.n,))]

    def split(self, refs, n_in, n_out, n_scratch):
        refs = list(refs)
        n = self.n
        own = refs[:n_in] + refs[n_in + n:n_in + n + n_out] + refs[n_in + 2 * n + n_out:n_in + 2 * n + n_out + n_scratch]
        cm = (refs[n_in:n_in + n], refs[n_in + n + n_out:n_in + 2 * n + n_out], refs[n_in + 2 * n + n_out + n_scratch:])
        return own, cm

    def _copies(self, cm, arriving):
        ins, outs, (send, recv, local, _, _) = cm
        x, y, c = _coords()
        me = 2 * x + y
        chips = [(1 - x, y), (x, 1 - y), (1 - x, 1 - y)]
        remote, locals_ = [], []
        ng = len(self.gathers)
        for i in range(self.n):
            if i < ng:
                _, l, halved = self.gathers[i]
                slab = ins[i] if l is None else ins[i].at[l]
                if not arriving:
                    locals_.append(pltpu.make_async_copy(slab, outs[i].at[me], local.at[i]))
            for j, (px, py) in enumerate(chips):
                if i < ng:
                    slot = 2 * px + py if arriving else me
                    src, dst = (slab.at[c], outs[i].at[slot, c]) if halved else (slab, outs[i].at[slot])
                else:
                    src, dst = ins[i].at[2 * px + py], outs[i].at[j]
                remote.append(pltpu.make_async_remote_copy(src, dst, send.at[3 * i + j], recv.at[3 * i + j],
                                                           device_id=(px, py, c), device_id_type=MESH))
        return remote, locals_

    def _handovers(self, cm, arriving):
        _, outs, (_, _, _, send, recv) = cm
        x, y, c = _coords()
        chips = [(1 - x, y), (x, 1 - y), (1 - x, 1 - y)]
        cps = []
        for i, (_, _, halved) in enumerate(self.gathers):
            if halved:
                for j, (px, py) in enumerate(chips):
                    src = outs[i].at[2 * px + py, c]
                    dst = outs[i].at[2 * px + py, 1 - c if arriving else c]
                    cps.append(pltpu.make_async_remote_copy(src, dst, send.at[3 * i + j], recv.at[3 * i + j],
                                                            device_id=(x, y, 1 - c), device_id_type=MESH))
        return cps

    def start_at(self, cond, cm):
        def go():
            remote, locals_ = self._copies(cm, False)
            for cp in locals_ + remote:
                cp.start()

        if self.n:
            go() if cond is True else pl.when(cond)(go)

    def wait_at(self, cond, cm):
        def go():
            for cp in self._copies(cm, True)[0]:
                cp.wait_recv()
            handed = self._handovers(cm, False)
            for cp in handed:
                cp.start()
            for cp in self._handovers(cm, True):
                cp.wait_recv()
            remote, locals_ = self._copies(cm, False)
            for cp in handed + remote:
                cp.wait_send()
            for cp in locals_:
                cp.wait()

        if self.n:
            go() if cond is True else pl.when(cond)(go)


def _comm_call(comm, name):
    def body(*refs):
        _, cm = comm.split(refs, 0, 0, 0)
        comm.start_at(True, cm)
        comm.wait_at(True, cm)

    return list(pl.pallas_call(
        body, name=name, in_specs=[ANY] * comm.n, out_specs=[ANY] * comm.n, out_shape=comm.out_shape(),
        scratch_shapes=comm.scratch(), compiler_params=pltpu.CompilerParams(has_side_effects=True),
    )(*comm.args()))


def _swap_sibling(arrs):
    n = len(arrs)

    def body(*refs):
        ins, outs, send, recv = refs[:n], refs[n:2 * n], refs[2 * n], refs[2 * n + 1]
        x, y, c = _coords()
        cps = [pltpu.make_async_remote_copy(ins[i], outs[i], send.at[i], recv.at[i], device_id=(x, y, 1 - c), device_id_type=MESH)
               for i in range(n)]
        for cp in cps:
            cp.start()
        for cp in cps:
            cp.wait_recv()
        for cp in cps:
            cp.wait_send()

    return list(pl.pallas_call(
        body, name="swap_sibling", in_specs=[ANY] * n, out_specs=[ANY] * n,
        out_shape=[jax.ShapeDtypeStruct(a.shape, a.dtype) for a in arrs],
        scratch_shapes=[pltpu.SemaphoreType.DMA((n,)), pltpu.SemaphoreType.DMA((n,))],
        compiler_params=pltpu.CompilerParams(has_side_effects=True),
    )(*arrs))


def _gather_small(gs):
    def body(g_ref, o_ref, send_sems, recv_sems, local_sem):
        x, y, c = _coords()
        me = 4 * x + 2 * y + c
        mine = pltpu.make_async_copy(g_ref, o_ref.at[me], local_sem)
        mine.start()
        sends = []
        for k in range(1, 8):
            px, py, pc = x ^ (k >> 2), y ^ ((k >> 1) & 1), c ^ (k & 1)
            sends.append((pltpu.make_async_remote_copy(g_ref, o_ref.at[me], send_sems.at[k - 1], recv_sems.at[k - 1],
                                                       device_id=(px, py, pc), device_id_type=MESH), 4 * px + 2 * py + pc, k))
        for cp, _, _ in sends:
            cp.start()
        for cp, src, k in sends:
            pltpu.make_async_remote_copy(g_ref, o_ref.at[src], send_sems.at[k - 1], recv_sems.at[k - 1],
                                         device_id=(x, y, c), device_id_type=MESH).wait_recv()
        for cp, _, _ in sends:
            cp.wait_send()
        mine.wait()

    return pl.pallas_call(
        body, name="gather_small", in_specs=[ANY], out_specs=ANY,
        out_shape=jax.ShapeDtypeStruct((8,) + gs.shape, gs.dtype),
        scratch_shapes=[pltpu.SemaphoreType.DMA((7,)), pltpu.SemaphoreType.DMA((7,)), pltpu.SemaphoreType.DMA],
        compiler_params=pltpu.CompilerParams(has_side_effects=True),
    )(gs)


def _sum_slots(own, others, name, tile):
    k, r, c = others.shape

    def body(*refs):
        if own is None:
            o_ref, out_ref = refs
            acc = o_ref[0].astype(F32)
            first = 1
        else:
            own_ref, o_ref, out_ref = refs
            acc = own_ref[...]
            first = 0
        for j in range(first, k):
            acc = acc + o_ref[j].astype(F32)
        out_ref[...] = acc

    row = pl.BlockSpec((tile, c), lambda i: (i, 0))
    specs = ([] if own is None else [row]) + [pl.BlockSpec((k, tile, c), lambda i: (0, i, 0))]
    args = ([] if own is None else [own]) + [others]
    return pl.pallas_call(body, name=name, grid=(r // tile,), in_specs=specs, out_specs=row,
                          out_shape=jax.ShapeDtypeStruct((r, c), F32), compiler_params=_cp("parallel"))(*args)


def _adamw(w, m, v, ga, gb, name, tile):
    lead = w.ndim - 2
    r, c = w.shape[-2:]

    def body(*refs):
        vals = [ref[0] if lead else ref[...] for ref in refs[:len(refs) - 4]]
        w_, m_, v_, g = vals[0], vals[1], vals[2], vals[3]
        if gb is not None:
            g = g + vals[4]
        nm = ADAM_B1 * m_ + (1.0 - ADAM_B1) * g
        nv = ADAM_B2 * v_ + (1.0 - ADAM_B2) * (g * g)
        d = -ADAM_LR * ((nm / BC1) / (jnp.sqrt(nv / BC2) + ADAM_EPS) + ADAM_WD * w_)
        for ref, val in zip(refs[len(refs) - 4:], (g, d, nm, nv)):
            if lead:
                ref[0] = val
            else:
                ref[...] = val

    if lead:
        row = pl.BlockSpec((1, tile, c), lambda l, i: (l, i, 0))
        grid = (w.shape[0], r // tile)
    else:
        row = pl.BlockSpec((tile, c), lambda i: (i, 0))
        grid = (r // tile,)
    args = [w, m, v, ga] + ([] if gb is None else [gb])
    return pl.pallas_call(body, name=name, grid=grid, in_specs=[row] * len(args), out_specs=[row] * 4,
                          out_shape=[jax.ShapeDtypeStruct(w.shape, F32)] * 4,
                          compiler_params=_cp(*(["parallel"] * len(grid))))(*args)


MATS = ("w_in", "w_out", "w_gate", "w_up", "w_down")
CONVS = ("ssd_conv_w", "lru_conv_w")
BIG = MATS + CONVS
COL_SHARDED = ("w_in", "w_gate", "w_up", "ssd_conv_w", "lru_conv_w")
SMALL = ("norm_mix", "ssd_conv_b", "ssd_dt_bias", "ssd_a_log", "ssd_d", "ssd_norm", "lru_conv_b", "lru_wa", "lru_ba",
         "lru_wx", "lru_bx", "lru_lambda", "norm_ffn", "norm_final")
WEIGHTS = ("norm_mix", "w_in", "ssd_conv_w", "ssd_conv_b", "ssd_dt_bias", "ssd_a_log", "ssd_d", "ssd_norm", "lru_conv_w",
           "lru_conv_b", "lru_wa", "lru_ba", "lru_wx", "lru_bx", "lru_lambda", "w_out", "norm_ffn", "w_gate", "w_up",
           "w_down", "norm_final")
ROW_TILE = {"w_in": 256, "w_out": 128, "w_gate": 256, "w_up": 256, "w_down": 352}


def _pack(arrs, width, row_mult, dtype):
    flat = jnp.concatenate([a.reshape(-1).astype(dtype) for a in arrs])
    rows = -(-flat.shape[0] // width)
    rows = -(-rows // row_mult) * row_mult
    flat = jnp.pad(flat, (0, rows * width - flat.shape[0]))
    return flat.reshape(rows, width)


def _unpack(buf, shapes):
    flat = buf.reshape(-1)
    out, off = [], 0
    for shp in shapes:
        n = int(np.prod(shp))
        out.append(flat[off:off + n].reshape(shp))
        off += n
    return out


def _join(name, g4):
    if name in COL_SHARDED:
        return jnp.moveaxis(g4, 0, -2).reshape(g4.shape[1:-1] + (4 * g4.shape[-1],))
    return g4.reshape((4 * g4.shape[1],) + g4.shape[2:])


def _slabs(name, g):
    if name in COL_SHARDED:
        return jnp.moveaxis(g.reshape(g.shape[:-1] + (4, g.shape[-1] // 4)), -2, 0)
    return g.reshape((4, g.shape[0] // 4) + g.shape[1:])


def _perm_cols(w):
    pad = jnp.zeros(w.shape[:-1] + (NP - IN_COLS,), w.dtype)
    return jnp.concatenate([w[..., :3072], w[..., 3080:4104], w[..., 3072:3080], pad], axis=-1)


def _unperm_cols(g):
    return jnp.concatenate([g[..., :3072], g[..., C_DT:C_DT + 8], g[..., 3072:4096]], axis=-1)


def _block_diag(w):
    eye = jnp.eye(LRU_BLOCKS, dtype=w.dtype)
    return jnp.einsum("ncd,nm->ncmd", w, eye).reshape(LRU_W, LRU_W)


def _block_diag_extract(g):
    g4 = g.reshape(LRU_BLOCKS, 64, LRU_BLOCKS, 64)
    return jnp.stack([g4[n, :, n, :] for n in range(LRU_BLOCKS)], axis=0)


def _lanes128(v):
    return jnp.pad(v, (0, BLK - v.shape[0])).reshape(1, BLK)


def _layer_mixers(x, p, comm=None, h=None):
    if h is None:
        h = _rms_fwd(x, p["norm_mix"], "rms_mix")
    proj = _mm(h, p["w_in"], tm=1024, tn=1408, tk=1024, name="mm_in")
    att, lse, attb, got = _att_fwd_fused(proj, "att_fwd", comm)
    xconv, dt = _ssd_pre(proj, p["ssd_conv_w"], p["ssd_conv_b"], _lanes128(p["ssd_dt_bias"]), "ssd_pre")
    spar = jnp.concatenate([_lanes128(p["ssd_a_log"]), _lanes128(p["ssd_d"]), jnp.zeros((6, BLK), F32)], axis=0)
    y, states = _ssd_scan(xconv, dt, spar, "ssd_scan")
    ssd = _ssd_post(y, proj, p["ssd_norm"], "ssd_post")
    xc = _lru_conv(proj, p["lru_conv_w"], p["lru_conv_b"], "lru_conv")
    wab = jnp.concatenate([_block_diag(p["lru_wa"]), _block_diag(p["lru_wx"])], axis=1).astype(MXU)
    pre = _mm(xc, wab, tm=1024, tn=1024, tk=512, name="mm_lru")
    lpar = jnp.concatenate([p["lru_ba"].reshape(1, -1), p["lru_bx"].reshape(1, -1), p["lru_lambda"].reshape(1, -1),
                            jnp.zeros((5, LRU_W), F32)], axis=0)
    lru, hs = _lru_scan(pre, xc, proj, lpar, "lru_scan")
    mix = jnp.concatenate([attb, ssd, lru], axis=1)
    saved = dict(x=x, h=h, proj=proj, att=att, lse=lse, xconv=xconv, dt=dt, spar=spar, y=y, states=states, xc=xc, wab=wab,
                 pre=pre, lpar=lpar, hs=hs, mix=mix)
    return mix, saved, got


def _layer_ffn(x, mix, p, saved, comm=None):
    x1 = _mm(mix, p["w_out"], add=x, tm=1024, tn=1024, tk=1536, name="mm_out")
    h2 = _rms_fwd(x1, p["norm_ffn"], "rms_ffn")
    gu = _mm(h2, p["w_gu"], tm=1024, tn=1408, tk=1024, name="mm_gu", comm=comm)
    gu, got = gu if comm is not None else (gu, [])
    act = _swiglu_act(gu, "swiglu_act")
    x2 = _mm(act, p["w_down"], add=x1, tm=1024, tn=1024, tk=2816, name="mm_down")
    saved.update(x1=x1, h2=h2, gu=gu, act=act)
    return x2, got


def _layer_bwd(dx2, p, sv, comm_ssd=None, comm_att=None, comm_tail=None):
    g = {}
    da = _mm(dx2, p["w_down"], tb=True, tm=1024, tn=1408, tk=1024, name="mm_d_act")
    g["w_down"] = _mm(sv["act"], dx2, ta=True, tm=1408, tn=1024, tk=1024, name="mm_g_down")
    dgu = _swiglu_bwd(sv["gu"], da, "swiglu_bwd")
    dh2 = _mm(dgu, p["w_gu"], tb=True, tm=1024, tn=1024, tk=1408, name="mm_d_h2")
    g["w_gu"] = _mm(sv["h2"], dgu, ta=True, tm=1024, tn=1408, tk=1024, name="mm_g_gu")
    dx1, gn = _rms_bwd(sv["x1"], p["norm_ffn"], dh2, dx2, "rms_ffn_bwd")
    g["norm_ffn"] = jnp.sum(gn, axis=0)
    dmix = _mm(dx1, p["w_out"], tb=True, tm=1024, tn=1536, tk=1024, name="mm_d_mix")
    g["w_out"] = _mm(sv["mix"], dx1, ta=True, tm=1536, tn=1024, tk=1024, name="mm_g_out")
    proj = sv["proj"]
    dpre, dxc_u, dgl, dlpar = _lru_scan_bwd(sv["pre"], sv["xc"], proj, sv["lpar"], sv["hs"], dmix, "lru_scan_bwd")
    dxc = _mm(dpre, sv["wab"], tb=True, add=dxc_u, tm=1024, tn=512, tk=1024, name="mm_d_xc")
    gwab = _mm(sv["xc"], dpre, ta=True, tm=512, tn=1024, tk=1024, name="mm_g_lru")
    g["lru_wa"], g["lru_wx"] = _block_diag_extract(gwab[:, :LRU_W]), _block_diag_extract(gwab[:, LRU_W:])
    g["lru_ba"], g["lru_bx"], g["lru_lambda"] = dlpar[0], dlpar[1], dlpar[2]
    dxl, gcw, gcb = _lru_conv_bwd(proj, dxc, p["lru_conv_w"], "lru_conv_bwd")
    g["lru_conv_w"], g["lru_conv_b"] = gcw[:CONV_K], jnp.sum(gcb, axis=0)
    dy, dz, gsn = _ssd_post_bwd(sv["y"], proj, p["ssd_norm"], (dmix, SSD_W, 1), "ssd_post_bwd")
    g["ssd_norm"] = jnp.sum(gsn, axis=0)
    dxconv, ddt, dal, ddk, got_ssd = _ssd_scan_bwd(sv["xconv"], sv["dt"], sv["spar"], sv["states"], dy, "ssd_scan_bwd", comm_ssd)
    g["ssd_a_log"], g["ssd_d"] = dal[0, :8], ddk[0, :8]
    dxbc, ddtr, gsw, gsb, gdb = _ssd_pre_bwd(proj, dxconv, ddt, p["ssd_conv_w"], p["ssd_conv_b"],
                                             _lanes128(p["ssd_dt_bias"]), "ssd_pre_bwd")
    g["ssd_conv_w"], g["ssd_conv_b"], g["ssd_dt_bias"] = gsw[:CONV_K], jnp.sum(gsb, axis=0), jnp.sum(gdb, axis=0)[:8]
    delta = _att_delta((dmix, ATT_W, 0), sv["att"], "att_delta")
    dq, dk, dv, got_att = _att_bwd_rev(proj, dmix, sv["lse"], delta, "att_bwd", None if comm_att is None else comm_att(g))
    dproj = jnp.concatenate([dq, dk, dv, dz, dxbc, dgl, dxl, ddtr], axis=1)
    g["w_in"] = _mm(sv["h"], dproj, ta=True, tm=1024, tn=1408, tk=1024, name="mm_g_in")
    dh = _mm(dproj, p["w_in"], tb=True, tm=1024, tn=1024, tk=1408, name="mm_d_h", comm=None if comm_tail is None else comm_tail(g))
    dh, got_tail = dh if comm_tail is not None else (dh, [])
    dx, gm = _rms_bwd(sv["x"], p["norm_mix"], dh, dx1, "rms_mix_bwd")
    g["norm_mix"] = jnp.sum(gm, axis=0)
    return dx, g, got_ssd, got_att, got_tail


def _grad_slabs(g, names):
    out = {}
    for n in names:
        if n == "w_in":
            out[n] = _slabs(n, _unperm_cols(g["w_in"]))
        elif n == "w_gate":
            out[n] = _slabs(n, g["w_gu"][:, :D_FF])
        elif n == "w_up":
            out[n] = _slabs(n, g["w_gu"][:, D_FF:])
        else:
            out[n] = _slabs(n, g[n])
    return out


def kernel(x, norm_mix, w_in, ssd_conv_w, ssd_conv_b, ssd_dt_bias, ssd_a_log, ssd_d, ssd_norm, lru_conv_w, lru_conv_b, lru_wa, lru_ba, lru_wx, lru_bx, lru_lambda, w_out, norm_ffn, w_gate, w_up, w_down, norm_final, loss_target, m_norm_mix, m_w_in, m_ssd_conv_w, m_ssd_conv_b, m_ssd_dt_bias, m_ssd_a_log, m_ssd_d, m_ssd_norm, m_lru_conv_w, m_lru_conv_b, m_lru_wa, m_lru_ba, m_lru_wx, m_lru_bx, m_lru_lambda, m_w_out, m_norm_ffn, m_w_gate, m_w_up, m_w_down, m_norm_final, v_norm_mix, v_w_in, v_ssd_conv_w, v_ssd_conv_b, v_ssd_dt_bias, v_ssd_a_log, v_ssd_d, v_ssd_norm, v_lru_conv_w, v_lru_conv_b, v_lru_wa, v_lru_ba, v_lru_wx, v_lru_bx, v_lru_lambda, v_w_out, v_norm_ffn, v_w_gate, v_w_up, v_w_down, v_norm_final):
    loc = dict(locals())
    w = {n: loc[n] for n in WEIGHTS}
    m = {n: loc["m_" + n] for n in WEIGHTS}
    v = {n: loc["v_" + n] for n in WEIGHTS}

    def halves(a):
        return a.reshape(a.shape[0], 2, a.shape[1] // 2, a.shape[2])

    def unhalve(a):
        return a.reshape(4, 2 * a.shape[2], a.shape[3])

    wb = {n: halves(w[n].astype(MXU)) for n in MATS}
    xs = x[0]
    h0, first = _rms_fwd(xs, norm_mix[0], "rms_mix", _Comm(gathers=[(wb["w_in"], 0, True), (w["ssd_conv_w"], None, False),
                                                                    (w["lru_conv_w"], None, False)]))
    convs = {"ssd_conv_w": _join("ssd_conv_w", first[1]), "lru_conv_w": _join("lru_conv_w", first[2])}
    behind_att = [(n, 0) for n in MATS[1:]] + [("w_in", 1)]
    behind_ffn = [(n, 1) for n in MATS[1:]]
    whole = {("w_in", 0): _join("w_in", unhalve(first[0]))}
    params = {}

    def layer_params(l):
        if l not in params:
            p = {n: w[n][l] for n in SMALL if n != "norm_final"}
            p.update(w_in=_perm_cols(whole["w_in", l]), ssd_conv_w=convs["ssd_conv_w"][l], lru_conv_w=convs["lru_conv_w"][l])
            params[l] = p
        if "w_out" not in params[l] and ("w_out", l) in whole:
            params[l].update(w_out=whole["w_out", l], w_down=whole["w_down", l],
                             w_gu=jnp.concatenate([whole["w_gate", l], whole["w_up", l]], axis=-1))
        return params[l]

    saved = []
    for l in range(DEPTH):
        first_layer = l == 0
        mix, sv, got = _layer_mixers(xs, layer_params(l), _Comm(gathers=[(wb[n], k, True) for n, k in behind_att]) if first_layer else None,
                                     h0 if first_layer else None)
        whole.update({k: _join(k[0], unhalve(a)) for k, a in zip(behind_att, got)})
        xs, got = _layer_ffn(xs, mix, layer_params(l), sv, _Comm(gathers=[(wb[n], k, True) for n, k in behind_ffn]) if first_layer else None)
        whole.update({k: _join(k[0], unhalve(a)) for k, a in zip(behind_ffn, got)})
        saved.append(sv)
    dx, gnf, lsum = _loss_head(xs, norm_final, loss_target[0], "loss_head")
    loss = lax.psum(jnp.sum(lsum), ("x", "y", "c"))

    dx, g1, _, _, _ = _layer_bwd(dx, layer_params(1), saved[1])
    s1 = _grad_slabs(g1, BIG)
    att0 = ("w_gate", "w_up", "w_down", "w_out")
    s0 = {}

    def wire(s, n):
        return s[n].astype(MXU) if n in MATS else s[n]

    def comm_att(g0):
        s0.update(_grad_slabs(g0, att0))
        return _Comm(scatters=[wire(s0, n) for n in att0])

    tail0 = ("w_in",) + CONVS

    def comm_tail(g0):
        s0.update(_grad_slabs(g0, tail0))
        return _Comm(scatters=[wire(s0, n) for n in tail0])

    dx, g0, got_ssd, got_att, got_tail = _layer_bwd(dx, layer_params(0), saved[0], _Comm(scatters=[wire(s1, n) for n in BIG]),
                                                    comm_att, comm_tail)
    recv = {(n, 1): a for n, a in zip(BIG, got_ssd)}
    recv.update({(n, 0): a for n, a in zip(att0, got_att)})
    recv.update({(n, 0): a for n, a in zip(tail0, got_tail)})

    me = 2 * lax.axis_index("x") + lax.axis_index("y")
    slabs = (s0, s1)
    part = {}
    for n in BIG:
        per_layer = []
        for l in range(DEPTH):
            own = lax.dynamic_index_in_dim(slabs[l][n], me, axis=0, keepdims=False)
            per_layer.append(_sum_slots(own, recv[n, l], "sum_chips_" + n, ROW_TILE.get(n, own.shape[0])))
        part[n] = jnp.stack(per_layer, axis=0)
    sib = dict(zip(BIG, _swap_sibling([part[n] for n in BIG])))
    out_g, out_d, out_m, out_v = {}, {}, {}, {}
    for n in BIG:
        out_g[n], out_d[n], out_m[n], out_v[n] = _adamw(w[n], m[n], v[n], part[n], sib[n], "adamw_" + n,
                                                        ROW_TILE.get(n, w[n].shape[1]))

    gsm = {n: jnp.stack([g0[n], g1[n]], axis=0) for n in SMALL if n != "norm_final"}
    gsm["norm_final"] = jnp.sum(gnf, axis=0)
    small_shapes = [w[n].shape for n in SMALL]
    gs = _pack([gsm[n].reshape(w[n].shape) for n in SMALL], BLK, 8, F32)
    gall = _gather_small(gs)
    gsum = _sum_slots(None, gall, "sum_devices", gs.shape[0])
    ws = _pack([w[n] for n in SMALL], BLK, 8, F32)
    ms = _pack([m[n] for n in SMALL], BLK, 8, F32)
    vs = _pack([v[n] for n in SMALL], BLK, 8, F32)
    gsr, dsr, nms, nvs = _adamw(ws, ms, vs, gsum, None, "adamw_small", gs.shape[0])
    out_g.update(zip(SMALL, _unpack(gsr, small_shapes)))
    out_d.update(zip(SMALL, _unpack(dsr, small_shapes)))
    out_m.update(zip(SMALL, _unpack(nms, small_shapes)))
    out_v.update(zip(SMALL, _unpack(nvs, small_shapes)))

    return (loss, dx[None], *[out_g[n] for n in WEIGHTS], *[out_d[n] for n in WEIGHTS],
            *[out_m[n] for n in WEIGHTS], *[out_v[n] for n in WEIGHTS])
```

```python
import functools
import math

import jax
import jax.numpy as jnp
import numpy as np
from jax import lax
from jax.experimental import pallas as pl
from jax.experimental.pallas import tpu as pltpu

F32 = jnp.float32
MXU = jnp.bfloat16
HI = lax.Precision.HIGHEST
MESH = pl.DeviceIdType.MESH

D_MODEL = 1024
DEPTH = 2
HEAD_DIM = 64
ATT_W = 512
ATT_PATTERNS = ((128, 1), (512, 4), (2048, 16))
BLK = 128
SSD_W = 512
SSD_STATE = 128
LRU_W = 512
LRU_BLOCKS = 8
LRU_C = 8.0
CONV_K = 4
D_MIX = 1536
D_FF = 2816
IN_COLS = 4104
NP = 4224
NORM_EPS = 1e-6
SSD_NORM_EPS = 1e-5
LN2 = math.log(2.0)
NEG = -1e30

ADAM_LR, ADAM_B1, ADAM_B2, ADAM_EPS, ADAM_WD, ADAM_STEP = 0.001, 0.9, 0.999, 1e-08, 0.01, 10
BC1 = 1.0 - ADAM_B1 ** ADAM_STEP
BC2 = 1.0 - ADAM_B2 ** ADAM_STEP

VMEM_LIMIT = 56 * 1024 * 1024

C_Q, C_K, C_V, C_Z, C_XBC, C_G, C_XL, C_DT = 0, 512, 1024, 1536, 2048, 3072, 3584, 4096


def _cp(*sem):
    return pltpu.CompilerParams(dimension_semantics=sem, vmem_limit_bytes=VMEM_LIMIT)


def _dot(a, b, dims, prec=None):
    return lax.dot_general(a, b, (dims, ((), ())), preferred_element_type=F32, precision=prec)


def _nn(a, b, prec=None):
    return _dot(a, b, ((1,), (0,)), prec)


def _nt(a, b, prec=None):
    return _dot(a, b, ((1,), (1,)), prec)


def _tn(a, b, prec=None):
    return _dot(a, b, ((0,), (0,)), prec)


def _sigmoid(x):
    return jax.nn.sigmoid(x)


def _silu(x):
    return x * _sigmoid(x)


def _softplus(x):
    return jnp.maximum(x, 0.0) + jnp.log(1.0 + jnp.exp(-jnp.abs(x)))


def _gelu(x):
    return 0.5 * x * (1.0 + jnp.tanh(0.7978845608028654 * (x + 0.044715 * x * x * x)))


def _mm(a, b, *, ta=False, tb=False, add=None, out_dtype=F32, tm, tn, tk, name, comm=None):
    m, k = (a.shape[1], a.shape[0]) if ta else a.shape
    n = b.shape[0] if tb else b.shape[1]
    assert (b.shape[1] if tb else b.shape[0]) == k
    assert m % tm == 0 and n % tn == 0 and k % tk == 0, (name, m, n, k)
    nk = k // tk
    a_spec = pl.BlockSpec((tk, tm), lambda i, j, kk: (kk, i)) if ta else pl.BlockSpec((tm, tk), lambda i, j, kk: (i, kk))
    b_spec = pl.BlockSpec((tn, tk), lambda i, j, kk: (j, kk)) if tb else pl.BlockSpec((tk, tn), lambda i, j, kk: (kk, j))
    o_spec = pl.BlockSpec((tm, tn), lambda i, j, kk: (i, j))
    dims = ((0 if ta else 1,), (1 if tb else 0,))
    carried = comm is not None
    comm = comm or _Comm()
    ni, nj = m // tm, n // tn

    def body(*refs):
        refs, cm = comm.split(refs, 2 if add is None else 3, 1, 1)
        if add is None:
            a_ref, b_ref, o_ref, acc = refs
        else:
            a_ref, b_ref, add_ref, o_ref, acc = refs
        i, j, kk = pl.program_id(0), pl.program_id(1), pl.program_id(2)
        comm.start_at((i == 0) & (j == 0) & (kk == 0), cm)

        @pl.when(kk == 0)
        def _():
            acc[...] = jnp.zeros_like(acc)

        acc[...] += _dot(a_ref[...].astype(MXU), b_ref[...].astype(MXU), dims)

        @pl.when(kk == nk - 1)
        def _():
            r = acc[...]
            if add is not None:
                r = r + add_ref[...]
            o_ref[...] = r.astype(out_dtype)

        comm.wait_at((i == ni - 1) & (j == nj - 1) & (kk == nk - 1), cm)

    ins = [a, b] + ([] if add is None else [add])
    specs = [a_spec, b_spec] + ([] if add is None else [o_spec])
    res = pl.pallas_call(
        body, name=name, grid=(ni, nj, nk), in_specs=specs + [ANY] * comm.n, out_specs=[o_spec] + [ANY] * comm.n,
        out_shape=[jax.ShapeDtypeStruct((m, n), out_dtype)] + comm.out_shape(),
        scratch_shapes=[pltpu.VMEM((tm, tn), F32)] + comm.scratch(),
        compiler_params=_cp(*((["arbitrary"] * 3) if comm.n else ["parallel", "parallel", "arbitrary"])),
    )(*ins, *comm.args())
    return (res[0], list(res[1:])) if carried else res[0]


def _rows(fn, rows, consts=(), outs=(), accs=(), *, tile, name, halos=(), comm=None):
    rows = [r if isinstance(r, tuple) else (r, r.shape[1], 0) for r in rows]
    s = rows[0][0].shape[0]
    assert s % tile == 0 and tile % 8 == 0
    n = s // tile
    t8 = tile // 8
    nr, nh, nc_, no, na = len(rows), len(halos), len(consts), len(outs), len(accs)
    carried = comm is not None
    comm = comm or _Comm()

    def body(*refs):
        refs, cm = comm.split(refs, nr + nh + nc_, no + na, 0)
        i = pl.program_id(0)
        comm.start_at(i == 0, cm)
        rv = [r[...] for r in refs[:nr]]
        hv = []
        for (idx, kind), r in zip(halos, refs[nr:nr + nh]):
            edge = (i == 0) if kind == "prev" else (i == n - 1)
            hv.append(jnp.where(edge, 0.0, r[...]))
        cv = [r[...] for r in refs[nr + nh:nr + nh + nc_]]
        o_refs = refs[nr + nh + nc_:nr + nh + nc_ + no]
        a_refs = refs[nr + nh + nc_ + no:]
        ov, av = fn(rv, hv, cv)
        for r, v in zip(o_refs, ov):
            r[...] = v.astype(r.dtype)
        if na:
            @pl.when(i == 0)
            def _():
                for r in a_refs:
                    r[...] = jnp.zeros_like(r)
            for r, v in zip(a_refs, av):
                r[...] += v
        comm.wait_at(i == n - 1, cm)

    in_specs = [pl.BlockSpec((tile, w), functools.partial(lambda i, cb: (i, cb), cb=cb)) for (_, w, cb) in rows]
    for idx, kind in halos:
        _, w, cb = rows[idx]
        if kind == "prev":
            in_specs.append(pl.BlockSpec((8, w), functools.partial(lambda i, cb: (jnp.maximum(i * t8 - 1, 0), cb), cb=cb)))
        else:
            in_specs.append(pl.BlockSpec((8, w), functools.partial(lambda i, cb: (jnp.minimum((i + 1) * t8, n * t8 - 1), cb), cb=cb)))
    in_specs += [pl.BlockSpec(c.shape, functools.partial(lambda i, nd: (0,) * nd, nd=c.ndim)) for c in consts]
    out_specs = [pl.BlockSpec((tile, c), lambda i: (i, 0)) for (c, _) in outs]
    out_specs += [pl.BlockSpec((r, c), lambda i: (0, 0)) for (r, c) in accs]
    out_shape = [jax.ShapeDtypeStruct((s, c), dt) for (c, dt) in outs]
    out_shape += [jax.ShapeDtypeStruct((r, c), F32) for (r, c) in accs]
    args = [r[0] for r in rows] + [rows[idx][0] for idx, _ in halos] + list(consts)
    res = pl.pallas_call(
        body, name=name, grid=(n,), in_specs=in_specs + [ANY] * comm.n, out_specs=out_specs + [ANY] * comm.n,
        out_shape=out_shape + comm.out_shape(), scratch_shapes=comm.scratch(), compiler_params=_cp("arbitrary"),
    )(*args, *comm.args())
    return (list(res[:no + na]), list(res[no + na:])) if carried else list(res)


def _colsum8(v):
    t, c = v.shape
    return jnp.sum(v.reshape(t // 8, 8, c), axis=0)


def _rms(x, g):
    return x * lax.rsqrt(jnp.mean(x * x, axis=-1, keepdims=True) + NORM_EPS) * g


def _rms_fwd(x, g, name, comm=None):
    def fn(rv, hv, cv):
        return [_rms(rv[0], cv[0])], []
    res = _rows(fn, [x], [g.reshape(1, -1)], [(x.shape[1], MXU)], tile=512, name=name, comm=comm)
    return res[0] if comm is None else (res[0][0], res[1])


def _rms_bwd(x, g, dh, dres, name):
    def fn(rv, hv, cv):
        xb, dhb, drb = rv
        _, vjp = jax.vjp(_rms, xb, cv[0])
        dx, _ = vjp(dhb)
        rstd = lax.rsqrt(jnp.mean(xb * xb, axis=-1, keepdims=True) + NORM_EPS)
        return [drb + dx], [_colsum8(dhb * xb * rstd)]
    d = x.shape[1]
    return _rows(fn, [x, dh, dres], [g.reshape(1, -1)], [(d, F32)], [(8, d)], tile=512, name=name)


def _slope_dist(hp, hh, dist, dil):
    hf = (2 * hp + hh + 1).astype(F32)
    slope = jnp.exp(jnp.zeros(dist.shape, F32) - hf * LN2)
    return slope * (dist.astype(F32) * float(dil))


def _att_delta(datt, att, name):
    def fn(rv, hv, cv):
        r = lax.broadcasted_iota(jnp.int32, (ATT_W, ATT_W), 0) // HEAD_DIM
        c = lax.broadcasted_iota(jnp.int32, (ATT_W, ATT_W), 1) // HEAD_DIM
        ones = (r == c).astype(F32)
        return [_nn(rv[0] * rv[1], ones, HI)], []
    return _rows(fn, [datt, att], [], [(ATT_W, F32)], tile=512, name=name)[0]


ATT_G = 2048


def _deinterleave(dst, src, dil, ld, region, offset):
    for r in range(dil):
        rows = pl.ds(r, ld, stride=dil) if dil > 1 else pl.ds(0, ld)
        dst[r * region + offset:r * region + offset + ld, :] = src[rows, :]


def _deinterleave_edge(dst, src, dil, region, offset, first_row):
    for r in range(dil):
        rows = pl.ds(first_row + r, BLK, stride=dil) if dil > 1 else pl.ds(first_row, BLK)
        dst[r * region + offset:r * region + offset + BLK, :] = src[rows, :]


def _att_fwd_fused(proj, name, comm=None):
    s, npc = proj.shape
    gsz = ATT_G
    ng = s // gsz
    assert s % gsz == 0
    scale = HEAD_DIM ** -0.5
    comm = comm or _Comm()

    def body(*refs):
        (q_ref, kp_ref, kc_ref, vp_ref, vc_ref, att_ref, lse_ref, attb_ref, qd, kd, vd, nd, md, dd, nn, mn, dn), cm = comm.split(refs, 5, 3, 9)
        hp, g = pl.program_id(0), pl.program_id(1)
        comm.start_at((hp == 0) & (g == 0), cm)
        lane = lax.broadcasted_iota(jnp.int32, (BLK, BLK), 1)
        qi = lax.broadcasted_iota(jnp.int32, (BLK, 2 * BLK), 0)
        ki = lax.broadcasted_iota(jnp.int32, (BLK, 2 * BLK), 1)
        dist = BLK + qi - ki
        band = (dist >= 0) & (dist <= BLK)
        for pi, (_, dil) in enumerate(ATT_PATTERNS):
            ld = gsz // dil
            nbg = ld // BLK
            _deinterleave(qd, q_ref, dil, ld, ld, 0)
            _deinterleave(kd, kc_ref, dil, ld, ld + BLK, BLK)
            _deinterleave(vd, vc_ref, dil, ld, ld + BLK, BLK)
            _deinterleave_edge(kd, kp_ref, dil, ld + BLK, 0, gsz - BLK * dil)
            _deinterleave_edge(vd, vp_ref, dil, ld + BLK, 0, gsz - BLK * dil)
            bias = [_slope_dist(hp, hh, dist, dil) for hh in (0, 1)]

            def tile(t, carry, ld=ld, nbg=nbg, bias=bias):
                r, b = t // nbg, t % nbg
                qo = pl.multiple_of(r * ld + b * BLK, BLK)
                ko = pl.multiple_of(r * (ld + BLK) + b * BLK, BLK)
                q = qd[pl.ds(qo, BLK), :]
                kk = kd[pl.ds(ko, 2 * BLK), :].astype(MXU)
                vv = vd[pl.ds(ko, 2 * BLK), :].astype(MXU)
                valid = band & ((g > 0) | (b > 0) | (ki >= BLK))
                num = jnp.zeros((BLK, BLK), F32)
                mx = jnp.zeros((BLK, BLK), F32)
                den = jnp.zeros((BLK, BLK), F32)
                for hh in (0, 1):
                    hmask = (lane < HEAD_DIM) if hh == 0 else (lane >= HEAD_DIM)
                    qm = jnp.where(hmask, q, 0.0).astype(MXU)
                    sc = jnp.where(valid, _nt(qm, kk) * scale - bias[hh], NEG)
                    m = jnp.max(sc, axis=1, keepdims=True)
                    p = jnp.exp(sc - m)
                    dn_ = jnp.sum(p, axis=1, keepdims=True)
                    o = _nn(p.astype(MXU), vv)
                    num = jnp.where(hmask, o, num)
                    mx = jnp.where(hmask, m, mx)
                    den = jnp.where(hmask, dn_, den)
                nd[pl.ds(qo, BLK), :] = num
                md[pl.ds(qo, BLK), :] = mx
                dd[pl.ds(qo, BLK), :] = den
                return carry

            lax.fori_loop(0, dil * nbg, tile, 0, unroll=4)
            for r in range(dil):
                rows = pl.ds(r, ld, stride=dil) if dil > 1 else pl.ds(0, ld)
                nn.at[pi][rows, :] = nd[r * ld:(r + 1) * ld, :]
                mn.at[pi][rows, :] = md[r * ld:(r + 1) * ld, :]
                dn.at[pi][rows, :] = dd[r * ld:(r + 1) * ld, :]

        def merge(c, carry):
            rows = pl.ds(pl.multiple_of(c * 256, 256), 256)
            ms = [mn[pi, rows, :] for pi in range(len(ATT_PATTERNS))]
            m_all = functools.reduce(jnp.maximum, ms)
            num = jnp.zeros((256, BLK), F32)
            den = jnp.zeros((256, BLK), F32)
            for pi in range(len(ATT_PATTERNS)):
                e = jnp.exp(ms[pi] - m_all)
                num = num + nn[pi, rows, :] * e
                den = den + dn[pi, rows, :] * e
            att = num / den
            att_ref[rows, :] = att
            attb_ref[rows, :] = att.astype(MXU)
            lse_ref[rows, :] = m_all + jnp.log(den)
            return carry

        lax.fori_loop(0, gsz // 256, merge, 0)
        comm.wait_at((hp == 3) & (g == ng - 1), cm)

    def cur(base):
        return pl.BlockSpec((gsz, BLK), lambda hp, g: (g, base // BLK + hp))

    def prev(base):
        return pl.BlockSpec((gsz, BLK), lambda hp, g: (jnp.maximum(g - 1, 0), base // BLK + hp))

    o_spec = pl.BlockSpec((gsz, BLK), lambda hp, g: (g, hp))
    npat = len(ATT_PATTERNS)
    res = pl.pallas_call(
        body, name=name, grid=(4, ng),
        in_specs=[cur(C_Q), prev(C_K), cur(C_K), prev(C_V), cur(C_V)] + [ANY] * comm.n,
        out_specs=[o_spec] * 3 + [ANY] * comm.n,
        out_shape=[jax.ShapeDtypeStruct((s, ATT_W), F32)] * 2 + [jax.ShapeDtypeStruct((s, ATT_W), MXU)] + comm.out_shape(),
        scratch_shapes=[pltpu.VMEM((gsz, BLK), F32), pltpu.VMEM((2 * gsz, BLK), F32), pltpu.VMEM((2 * gsz, BLK), F32)]
        + [pltpu.VMEM((gsz, BLK), F32)] * 3 + [pltpu.VMEM((npat, gsz, BLK), F32)] * 3 + comm.scratch(),
        compiler_params=_cp("arbitrary", "arbitrary"),
    )(proj, proj, proj, proj, proj, *comm.args())
    return res[0], res[1], res[2], list(res[3:])


def _att_bwd_fused(proj, datt, lse, delta, name, comm=None):
    s, npc = proj.shape
    gsz = ATT_G
    ng = s // gsz
    scale = HEAD_DIM ** -0.5
    comm = comm or _Comm()

    def body(*refs):
        (qc_ref, qn_ref, kp_ref, kc_ref, vp_ref, vc_ref, doc_ref, don_ref, lsc_ref, lsn_ref, dlc_ref, dln_ref,
         dq_ref, dk_ref, dv_ref, qd, dod, lsd, dld, kd, vd, dqd, dkd, dvd), cm = comm.split(refs, 12, 3, 9)
        hp, g = pl.program_id(0), pl.program_id(1)
        comm.start_at((hp == 0) & (g == 0), cm)
        lane = lax.broadcasted_iota(jnp.int32, (BLK, BLK), 1)
        qi = lax.broadcasted_iota(jnp.int32, (BLK, BLK), 0)
        ki = lax.broadcasted_iota(jnp.int32, (BLK, BLK), 1)
        d_far = BLK + qi - ki
        d_near = qi - ki
        for pi, (_, dil) in enumerate(ATT_PATTERNS):
            ld = gsz // dil
            nbg = ld // BLK
            reg = ld + BLK
            for dst, c_ref, n_ref in ((qd, qc_ref, qn_ref), (dod, doc_ref, don_ref), (lsd, lsc_ref, lsn_ref), (dld, dlc_ref, dln_ref)):
                _deinterleave(dst, c_ref, dil, ld, reg, 0)
                _deinterleave_edge(dst, n_ref, dil, reg, ld, 0)
            for dst, p_ref, c_ref in ((kd, kp_ref, kc_ref), (vd, vp_ref, vc_ref)):
                _deinterleave(dst, c_ref, dil, ld, reg, BLK)
                _deinterleave_edge(dst, p_ref, dil, reg, 0, gsz - BLK * dil)
            b_far = [_slope_dist(hp, hh, d_far, dil) for hh in (0, 1)]
            b_near = [_slope_dist(hp, hh, d_near, dil) for hh in (0, 1)]

            def tile(t, carry, ld=ld, nbg=nbg, reg=reg, b_far=b_far, b_near=b_near):
                r, b = t // nbg, t % nbg
                oo = pl.multiple_of(r * ld + b * BLK, BLK)
                ro = pl.multiple_of(r * reg + b * BLK, BLK)
                qn, qx = qd[pl.ds(ro, BLK), :], qd[pl.ds(ro + BLK, BLK), :]
                don, dox = dod[pl.ds(ro, BLK), :], dod[pl.ds(ro + BLK, BLK), :]
                lsn, lsx = lsd[pl.ds(ro, BLK), :], lsd[pl.ds(ro + BLK, BLK), :]
                dln, dlx = dld[pl.ds(ro, BLK), :], dld[pl.ds(ro + BLK, BLK), :]
                kp, kc = kd[pl.ds(ro, BLK), :].astype(MXU), kd[pl.ds(ro + BLK, BLK), :].astype(MXU)
                vp, vc = vd[pl.ds(ro, BLK), :].astype(MXU), vd[pl.ds(ro + BLK, BLK), :].astype(MXU)
                ok_a = (d_far <= BLK) & ((g > 0) | (b > 0))
                ok_b = d_near >= 0
                ok_c = (d_far <= BLK) & ((g < ng - 1) | (b < nbg - 1))

                def grads(qm, dom, k, v, ls, dl, bias, valid, hh):
                    c0 = hh * HEAD_DIM
                    sc = _nt(qm, k) * scale - bias
                    p = jnp.exp(jnp.where(valid, sc - ls[:, c0:c0 + 1], NEG))
                    ds = p * (_nt(dom, v) - dl[:, c0:c0 + 1])
                    return p.astype(MXU), ds.astype(MXU)

                dq = jnp.zeros((BLK, BLK), F32)
                dk = jnp.zeros((BLK, BLK), F32)
                dv = jnp.zeros((BLK, BLK), F32)
                for hh in (0, 1):
                    hmask = (lane < HEAD_DIM) if hh == 0 else (lane >= HEAD_DIM)
                    qnm = jnp.where(hmask, qn, 0.0).astype(MXU)
                    qxm = jnp.where(hmask, qx, 0.0).astype(MXU)
                    donm = jnp.where(hmask, don, 0.0).astype(MXU)
                    doxm = jnp.where(hmask, dox, 0.0).astype(MXU)
                    _, ds_a = grads(qnm, donm, kp, vp, lsn, dln, b_far[hh], ok_a, hh)
                    p_b, ds_b = grads(qnm, donm, kc, vc, lsn, dln, b_near[hh], ok_b, hh)
                    p_c, ds_c = grads(qxm, doxm, kc, vc, lsx, dlx, b_far[hh], ok_c, hh)
                    dq = jnp.where(hmask, _nn(ds_a, kp) + _nn(ds_b, kc), dq)
                    dk = dk + _tn(ds_b, qnm) + _tn(ds_c, qxm)
                    dv = dv + _tn(p_b, donm) + _tn(p_c, doxm)
                dqd[pl.ds(oo, BLK), :] = dq * scale
                dkd[pl.ds(oo, BLK), :] = dk * scale
                dvd[pl.ds(oo, BLK), :] = dv
                return carry

            lax.fori_loop(0, dil * nbg, tile, 0, unroll=4)
            for out, src in ((dq_ref, dqd), (dk_ref, dkd), (dv_ref, dvd)):
                for r in range(dil):
                    rows = pl.ds(r, ld, stride=dil) if dil > 1 else pl.ds(0, ld)
                    if pi == 0:
                        out[rows, :] = src[r * ld:(r + 1) * ld, :]
                    else:
                        out[rows, :] = out[rows, :] + src[r * ld:(r + 1) * ld, :]
        comm.wait_at((hp == 3) & (g == ng - 1), cm)

    def pspec(base, shift):
        return pl.BlockSpec((gsz, BLK), lambda hp, g: (jnp.clip(g + shift, 0, ng - 1), base // BLK + hp))

    def wspec(shift):
        return pl.BlockSpec((gsz, BLK), lambda hp, g: (jnp.clip(g + shift, 0, ng - 1), hp))

    in_specs = [pspec(C_Q, 0), pspec(C_Q, 1), pspec(C_K, -1), pspec(C_K, 0), pspec(C_V, -1), pspec(C_V, 0),
                wspec(0), wspec(1), wspec(0), wspec(1), wspec(0), wspec(1)] + [ANY] * comm.n
    res = pl.pallas_call(
        body, name=name, grid=(4, ng), in_specs=in_specs,
        out_specs=[wspec(0)] * 3 + [ANY] * comm.n,
        out_shape=[jax.ShapeDtypeStruct((s, ATT_W), F32)] * 3 + comm.out_shape(),
        scratch_shapes=[pltpu.VMEM((2 * gsz, BLK), F32)] * 6 + [pltpu.VMEM((gsz, BLK), F32)] * 3 + comm.scratch(),
        compiler_params=_cp("arbitrary", "arbitrary"),
    )(proj, proj, proj, proj, proj, proj, datt, datt, lse, lse, delta, delta, *comm.args())
    return res[0], res[1], res[2], list(res[3:])


def _att_bwd_rev(proj, datt, lse, delta, name, comm=None):
    s, npc = proj.shape
    gsz = ATT_G
    ng = s // gsz
    npat = len(ATT_PATTERNS)
    scale = HEAD_DIM ** -0.5
    comm = comm or _Comm()

    def body(*refs):
        (q_ref, kp_ref, kc_ref, vp_ref, vc_ref, do_ref, ls_ref, dl_ref, dq_out, dk_out, dv_out,
         qd, dod, lsd, dld, kd, vd, dqd, dkc, dvc, dkp, dvp, kcar, vcar, dq_ref, dk_ref, dv_ref), cm = comm.split(refs, 8, 3, 16)
        hp, gi = pl.program_id(0), pl.program_id(1)
        g = ng - 1 - gi
        comm.start_at((hp == 0) & (gi == 0), cm)

        @pl.when(gi == 0)
        def _():
            kcar[...] = jnp.zeros_like(kcar)
            vcar[...] = jnp.zeros_like(vcar)

        lane = lax.broadcasted_iota(jnp.int32, (BLK, BLK), 1)
        qi = lax.broadcasted_iota(jnp.int32, (BLK, 2 * BLK), 0)
        ki = lax.broadcasted_iota(jnp.int32, (BLK, 2 * BLK), 1)
        dist = BLK + qi - ki
        band = (dist >= 0) & (dist <= BLK)
        for pi, (_, dil) in enumerate(ATT_PATTERNS):
            ld = gsz // dil
            nbg = ld // BLK
            reg = ld + BLK
            for dst, src in ((qd, q_ref), (dod, do_ref), (lsd, ls_ref), (dld, dl_ref)):
                _deinterleave(dst, src, dil, ld, ld, 0)
            for dst, p_ref, c_ref in ((kd, kp_ref, kc_ref), (vd, vp_ref, vc_ref)):
                _deinterleave(dst, c_ref, dil, ld, reg, BLK)
                _deinterleave_edge(dst, p_ref, dil, reg, 0, gsz - BLK * dil)
            bias = [_slope_dist(hp, hh, dist, dil) for hh in (0, 1)]

            def tile(t, carry, ld=ld, nbg=nbg, reg=reg, bias=bias):
                r, b = t // nbg, t % nbg
                oo = pl.multiple_of(r * ld + b * BLK, BLK)
                ko = pl.multiple_of(r * reg + b * BLK, BLK)
                q, do = qd[pl.ds(oo, BLK), :], dod[pl.ds(oo, BLK), :]
                ls, dl = lsd[pl.ds(oo, BLK), :], dld[pl.ds(oo, BLK), :]
                kk = kd[pl.ds(ko, 2 * BLK), :].astype(MXU)
                vv = vd[pl.ds(ko, 2 * BLK), :].astype(MXU)
                valid = band & ((g > 0) | (b > 0) | (ki >= BLK))
                dq = jnp.zeros((BLK, BLK), F32)
                dkk = jnp.zeros((2 * BLK, BLK), F32)
                dvv = jnp.zeros((2 * BLK, BLK), F32)
                for hh in (0, 1):
                    c0 = hh * HEAD_DIM
                    hmask = (lane < HEAD_DIM) if hh == 0 else (lane >= HEAD_DIM)
                    qm = jnp.where(hmask, q, 0.0).astype(MXU)
                    dom = jnp.where(hmask, do, 0.0).astype(MXU)
                    sc = _nt(qm, kk) * scale - bias[hh]
                    p = jnp.exp(jnp.where(valid, sc - ls[:, c0:c0 + 1], NEG))
                    ds = (p * (_nt(dom, vv) - dl[:, c0:c0 + 1])).astype(MXU)
                    dq = jnp.where(hmask, _nn(ds, kk), dq)
                    dkk = dkk + _tn(ds, qm)
                    dvv = dvv + _tn(p.astype(MXU), dom)
                dqd[pl.ds(oo, BLK), :] = dq * scale
                dkp[pl.ds(oo, BLK), :] = dkk[:BLK] * scale
                dkc[pl.ds(oo, BLK), :] = dkk[BLK:] * scale
                dvp[pl.ds(oo, BLK), :] = dvv[:BLK]
                dvc[pl.ds(oo, BLK), :] = dvv[BLK:]
                return carry

            lax.fori_loop(0, dil * nbg, tile, 0, unroll=4)
            for r in range(dil):
                rows = pl.ds(r, ld, stride=dil) if dil > 1 else pl.ds(0, ld)
                lo, hi = r * ld, (r + 1) * ld
                edge = slice(pi * gsz + r * BLK, pi * gsz + (r + 1) * BLK)
                for out, cur, prv, car in ((dk_ref, dkc, dkp, kcar), (dv_ref, dvc, dvp, vcar)):
                    later = car[edge, :] if nbg == 1 else jnp.concatenate([prv[lo + BLK:hi, :], car[edge, :]], axis=0)
                    total = cur[lo:hi, :] + later
                    car[edge, :] = prv[lo:lo + BLK, :]
                    out[rows, :] = total if pi == 0 else out[rows, :] + total
                dq_ref[rows, :] = dqd[lo:hi, :] if pi == 0 else dq_ref[rows, :] + dqd[lo:hi, :]
        for out, acc in ((dq_out, dq_ref), (dk_out, dk_ref), (dv_out, dv_ref)):
            out[...] = acc[...].astype(out.dtype)
        comm.wait_at((hp == 3) & (gi == ng - 1), cm)

    def pspec(base, shift):
        return pl.BlockSpec((gsz, BLK), lambda hp, gi: (jnp.maximum(ng - 1 - gi + shift, 0), base // BLK + hp))

    wspec = pl.BlockSpec((gsz, BLK), lambda hp, gi: (ng - 1 - gi, hp))
    in_specs = [pspec(C_Q, 0), pspec(C_K, -1), pspec(C_K, 0), pspec(C_V, -1), pspec(C_V, 0), wspec, wspec, wspec] + [ANY] * comm.n
    res = pl.pallas_call(
        body, name=name, grid=(4, ng), in_specs=in_specs,
        out_specs=[wspec] * 3 + [ANY] * comm.n,
        out_shape=[jax.ShapeDtypeStruct((s, ATT_W), MXU)] * 3 + comm.out_shape(),
        scratch_shapes=[pltpu.VMEM((gsz, BLK), F32)] * 4 + [pltpu.VMEM((2 * gsz, BLK), F32)] * 2
        + [pltpu.VMEM((gsz, BLK), F32)] * 5 + [pltpu.VMEM((npat * gsz, BLK), F32)] * 2 + [pltpu.VMEM((gsz, BLK), F32)] * 3
        + comm.scratch(),
        compiler_params=_cp("arbitrary", "arbitrary"),
    )(proj, proj, proj, proj, proj, datt, lse, delta, *comm.args())
    return res[0], res[1], res[2], list(res[3:])


def _shift_down(cur, halo, sft):
    if sft == 0:
        return cur
    t = cur.shape[0]
    rolled = pltpu.roll(cur, sft, 0)
    hr = pltpu.roll(halo, sft, 0)
    row = lax.broadcasted_iota(jnp.int32, cur.shape, 0)
    return jnp.where(row < sft, jnp.tile(hr, (t // 8, 1)), rolled)


def _shift_up(cur, halo, sft):
    if sft == 0:
        return cur
    t = cur.shape[0]
    rolled = pltpu.roll(cur, t - sft, 0)
    hr = pltpu.roll(halo, 8 - sft, 0)
    row = lax.broadcasted_iota(jnp.int32, cur.shape, 0)
    return jnp.where(row >= t - sft, jnp.tile(hr, (t // 8, 1)), rolled)


def _conv(x, xh, w, b):
    y = b + x * w[CONV_K - 1:CONV_K]
    for k in range(CONV_K - 1):
        y = y + _shift_down(x, xh, CONV_K - 1 - k) * w[k:k + 1]
    return y


def _conv_bwd(x, xh, dy, dyh, w):
    dx = dy * w[CONV_K - 1:CONV_K]
    dws = []
    for k in range(CONV_K - 1):
        sft = CONV_K - 1 - k
        dx = dx + _shift_up(dy, dyh, sft) * w[k:k + 1]
        dws.append(jnp.sum(dy * _shift_down(x, xh, sft), axis=0, keepdims=True))
    dws.append(jnp.sum(dy * x, axis=0, keepdims=True))
    c = x.shape[1]
    dw = jnp.concatenate(dws + [jnp.zeros((8 - CONV_K, c), F32)], axis=0)
    return dx, dw, jnp.sum(dy, axis=0, keepdims=True)


def _pad8(w):
    return jnp.concatenate([w, jnp.zeros((8 - w.shape[0], w.shape[1]), w.dtype)], axis=0)


def _ssd_pre(proj, conv_w, conv_b, dt_bias128, name):
    def fn(rv, hv, cv):
        xbc, dtr = rv
        return [_silu(_conv(xbc, hv[0], cv[0], cv[1])), _softplus(dtr + cv[2])], []
    return _rows(fn, [(proj, 1024, C_XBC // 1024), (proj, BLK, C_DT // BLK)],
                 [_pad8(conv_w), conv_b.reshape(1, -1), dt_bias128],
                 [(1024, F32), (BLK, F32)], tile=256, name=name, halos=[(0, "prev")])


def _ssd_pre_bwd(proj, dxc, ddt, conv_w, conv_b, dt_bias128, name):
    def fn(rv, hv, cv):
        xbc, dtr, dxcb, ddtb = rv
        xh, dxch_raw, xnext = hv
        w, b, bias = cv
        pre = _conv(xbc, xh, w, b)
        sg = _sigmoid(pre)
        dpre = dxcb * (sg * (1.0 + pre * (1.0 - sg)))
        t = xbc.shape[0]
        tail = jnp.concatenate([xbc[t - 8:], xnext], axis=0)
        pre_n = _conv(tail[8:], tail[:8], w, b)
        sgn = _sigmoid(pre_n)
        dpre_h = dxch_raw * (sgn * (1.0 + pre_n * (1.0 - sgn)))
        dx, dw, db = _conv_bwd(xbc, xh, dpre, dpre_h, w)
        ddr = ddtb * _sigmoid(dtr + bias)
        return [dx, ddr], [dw, jnp.concatenate([db, jnp.zeros((7, db.shape[1]), F32)], axis=0), _colsum8(ddr)]
    return _rows(fn, [(proj, 1024, C_XBC // 1024), (proj, BLK, C_DT // BLK), dxc, ddt],
                 [_pad8(conv_w), conv_b.reshape(1, -1), dt_bias128],
                 [(1024, MXU), (BLK, MXU)], [(8, 1024), (8, 1024), (8, BLK)], tile=256, name=name,
                 halos=[(0, "prev"), (2, "next"), (0, "next")])


def _head_cols(v, h0):
    lane = lax.broadcasted_iota(jnp.int32, (v.shape[0], BLK), 1)
    return jnp.where(lane < HEAD_DIM, v[:, h0:h0 + 1], v[:, h0 + 1:h0 + 2])


def _ssd_scan(xc, dt, par, name):
    s = xc.shape[0]
    nc = s // BLK

    def body(x_ref, dt_ref, par_ref, y_ref, st_ref, h_ref):
        c = pl.program_id(0)

        @pl.when(c == 0)
        def _():
            h_ref[...] = jnp.zeros_like(h_ref)

        st_ref[0] = h_ref[...]
        dt = dt_ref[...]
        a_row = -jnp.exp(par_ref[0:1, :])
        d_row = par_ref[1:2, :]
        ri = lax.broadcasted_iota(jnp.int32, (BLK, BLK), 0)
        ci = lax.broadcasted_iota(jnp.int32, (BLK, BLK), 1)
        tril = ri >= ci
        cs = _nn(tril.astype(F32), dt * a_row, HI)
        cst, dtt = cs.T, dt.T
        last = cs[BLK - 1:BLK, :]
        wcol = jnp.exp(last - cs) * dt
        ecs = jnp.exp(cs)
        elast = jnp.exp(last)
        for g in (0, 1):
            bg = x_ref[:, 512 + g * BLK:512 + (g + 1) * BLK].astype(MXU)
            cg = x_ref[:, 768 + g * BLK:768 + (g + 1) * BLK].astype(MXU)
            gm = _nt(cg, bg)
            for pp in (0, 1):
                pr = 2 * g + pp
                h0 = 2 * pr
                x2 = x_ref[:, pr * BLK:(pr + 1) * BLK]
                hprev = h_ref[pr * BLK:(pr + 1) * BLK, :]
                yp = jnp.zeros((BLK, BLK), F32)
                for hh in (0, 1):
                    h = h0 + hh
                    hmask = (ci < HEAD_DIM) if hh == 0 else (ci >= HEAD_DIM)
                    lm = jnp.exp(jnp.where(tril, cs[:, h:h + 1] - cst[h:h + 1, :], NEG))
                    mm = gm * lm * dtt[h:h + 1, :]
                    yp = yp + _nn(mm.astype(MXU), jnp.where(hmask, x2, 0.0).astype(MXU))
                y0 = _nt(cg, hprev.astype(MXU))
                y_ref[:, pr * BLK:(pr + 1) * BLK] = yp + _head_cols(ecs, h0) * y0 + _head_cols(d_row, h0) * x2
                dec = jnp.where(ri < HEAD_DIM, elast[:, h0:h0 + 1], elast[:, h0 + 1:h0 + 2])
                xw = (x2 * _head_cols(wcol, h0)).astype(MXU)
                h_ref[pr * BLK:(pr + 1) * BLK, :] = dec * hprev + _tn(xw, bg)

    return pl.pallas_call(
        body, name=name, grid=(nc,),
        in_specs=[pl.BlockSpec((BLK, 1024), lambda c: (c, 0)), pl.BlockSpec((BLK, BLK), lambda c: (c, 0)),
                  pl.BlockSpec((8, BLK), lambda c: (0, 0))],
        out_specs=[pl.BlockSpec((BLK, SSD_W), lambda c: (c, 0)), pl.BlockSpec((1, SSD_W, SSD_STATE), lambda c: (c, 0, 0))],
        out_shape=[jax.ShapeDtypeStruct((s, SSD_W), F32), jax.ShapeDtypeStruct((nc, SSD_W, SSD_STATE), F32)],
        scratch_shapes=[pltpu.VMEM((SSD_W, SSD_STATE), F32)],
        compiler_params=_cp("arbitrary"),
    )(xc, dt, par)


def _ssd_scan_bwd(xc, dt, par, st, dy, name, comm=None):
    s = xc.shape[0]
    nc = s // BLK
    comm = comm or _Comm()

    def body(*refs):
        (x_ref, dt_ref, par_ref, st_ref, dy_ref, dx_ref, ddt_ref, dal_ref, dd_ref, dh_ref), cm = comm.split(refs, 5, 4, 1)
        c = pl.program_id(0)
        comm.start_at(c == 0, cm)

        @pl.when(c == 0)
        def _():
            dh_ref[...] = jnp.zeros_like(dh_ref)
            dal_ref[...] = jnp.zeros_like(dal_ref)
            dd_ref[...] = jnp.zeros_like(dd_ref)

        dt = dt_ref[...]
        a_row = -jnp.exp(par_ref[0:1, :])
        d_row = par_ref[1:2, :]
        ri = lax.broadcasted_iota(jnp.int32, (BLK, BLK), 0)
        ci = lax.broadcasted_iota(jnp.int32, (BLK, BLK), 1)
        tril = ri >= ci
        cs = _nn(tril.astype(F32), dt * a_row, HI)
        cst, dtt = cs.T, dt.T
        last = cs[BLK - 1:BLK, :]
        tolast = jnp.exp(last - cs)
        wcol = tolast * dt
        ecs = jnp.exp(cs)
        elast = jnp.exp(last)
        dcs_col = jnp.zeros((BLK, BLK), F32)
        ddt_col = jnp.zeros((BLK, BLK), F32)
        dcs_row = jnp.zeros((BLK, BLK), F32)
        ddt_row = jnp.zeros((BLK, BLK), F32)
        dlast = jnp.zeros((1, BLK), F32)
        ddsk = jnp.zeros((1, BLK), F32)
        for g in (0, 1):
            bg32 = x_ref[:, 512 + g * BLK:512 + (g + 1) * BLK]
            cg32 = x_ref[:, 768 + g * BLK:768 + (g + 1) * BLK]
            bg, cg = bg32.astype(MXU), cg32.astype(MXU)
            gm = _nt(cg, bg)
            dgm = jnp.zeros((BLK, BLK), F32)
            dbg = jnp.zeros((BLK, BLK), F32)
            dcg = jnp.zeros((BLK, BLK), F32)
            for pp in (0, 1):
                pr = 2 * g + pp
                h0 = 2 * pr
                x2 = x_ref[:, pr * BLK:(pr + 1) * BLK]
                dy2 = dy_ref[:, pr * BLK:(pr + 1) * BLK]
                hprev = st_ref[0, pr * BLK:(pr + 1) * BLK, :]
                dhn = dh_ref[pr * BLK:(pr + 1) * BLK, :]
                x2m, dhnm = x2.astype(MXU), dhn.astype(MXU)
                zb = _nt(bg, dhnm)
                y0 = _nt(cg, hprev.astype(MXU))
                esel = _head_cols(ecs, h0)
                wsel = _head_cols(wcol, h0)
                dx2 = _head_cols(d_row, h0) * dy2 + wsel * zb
                r_off = dy2 * y0
                r_w = x2 * zb
                r_d = dy2 * x2
                r_h = dhn * hprev
                for hh in (0, 1):
                    h = h0 + hh
                    hmask = (ci < HEAD_DIM) if hh == 0 else (ci >= HEAD_DIM)
                    onl = (ci == h).astype(F32)
                    ons = (ri == h).astype(F32)
                    dym = jnp.where(hmask, dy2, 0.0).astype(MXU)
                    dt_r = dtt[h:h + 1, :]
                    lm = jnp.exp(jnp.where(tril, cs[:, h:h + 1] - cst[h:h + 1, :], NEG))
                    mm = gm * lm * dt_r
                    dx2 = dx2 + _tn(mm.astype(MXU), dym)
                    dm = _nt(dym, x2m)
                    t1 = dm * lm
                    dgm = dgm + t1 * dt_r
                    tt = t1 * gm
                    ddt_row = ddt_row + ons * jnp.sum(tt, axis=0, keepdims=True)
                    t = tt * dt_r
                    dcs_col = dcs_col + onl * jnp.sum(t, axis=1, keepdims=True)
                    dcs_row = dcs_row - ons * jnp.sum(t, axis=0, keepdims=True)
                    de = jnp.sum(jnp.where(hmask, r_off, 0.0), axis=1, keepdims=True)
                    dcs_col = dcs_col + onl * (ecs[:, h:h + 1] * de)
                    hrow = (ri < HEAD_DIM) if hh == 0 else (ri >= HEAD_DIM)
                    dl_h = elast[:, h:h + 1] * jnp.sum(jnp.where(hrow, r_h, 0.0), keepdims=True)
                    dw = jnp.sum(jnp.where(hmask, r_w, 0.0), axis=1, keepdims=True)
                    ddt_col = ddt_col + onl * (dw * tolast[:, h:h + 1])
                    v = dw * wcol[:, h:h + 1]
                    dcs_col = dcs_col - onl * v
                    dl_h = dl_h + jnp.sum(v, keepdims=True)
                    dlast = dlast + onl[0:1, :] * dl_h
                    ddsk = ddsk + onl[0:1, :] * jnp.sum(jnp.where(hmask, r_d, 0.0), keepdims=True)
                dx_ref[:, pr * BLK:(pr + 1) * BLK] = dx2
                edy = (esel * dy2).astype(MXU)
                dcg = dcg + _nn(edy, hprev.astype(MXU))
                dec = jnp.where(ri < HEAD_DIM, elast[:, h0:h0 + 1], elast[:, h0 + 1:h0 + 2])
                dh_ref[pr * BLK:(pr + 1) * BLK, :] = dec * dhn + _tn(edy, cg)
                dbg = dbg + _nn((x2 * wsel).astype(MXU), dhnm)
            dgmm = dgm.astype(MXU)
            dx_ref[:, 512 + g * BLK:512 + (g + 1) * BLK] = dbg + _tn(dgmm, cg)
            dx_ref[:, 768 + g * BLK:768 + (g + 1) * BLK] = dcg + _nn(dgmm, bg)
        dcs = dcs_col + dcs_row.T + jnp.where(ri == BLK - 1, dlast, 0.0)
        dda = _nn((ri <= ci).astype(F32), dcs, HI)
        ddt_ref[...] = ddt_col + ddt_row.T + a_row * dda
        da = jnp.sum(dt * dda, axis=0, keepdims=True)
        dal_ref[0:1, :] += da * a_row
        dd_ref[0:1, :] += ddsk
        comm.wait_at(c == nc - 1, cm)

    rev = lambda c: (nc - 1 - c, 0)
    res = pl.pallas_call(
        body, name=name, grid=(nc,),
        in_specs=[pl.BlockSpec((BLK, 1024), rev), pl.BlockSpec((BLK, BLK), rev), pl.BlockSpec((8, BLK), lambda c: (0, 0)),
                  pl.BlockSpec((1, SSD_W, SSD_STATE), lambda c: (nc - 1 - c, 0, 0)), pl.BlockSpec((BLK, SSD_W), rev)]
        + [ANY] * comm.n,
        out_specs=[pl.BlockSpec((BLK, 1024), rev), pl.BlockSpec((BLK, BLK), rev),
                   pl.BlockSpec((8, BLK), lambda c: (0, 0)), pl.BlockSpec((8, BLK), lambda c: (0, 0))] + [ANY] * comm.n,
        out_shape=[jax.ShapeDtypeStruct((s, 1024), F32), jax.ShapeDtypeStruct((s, BLK), F32),
                   jax.ShapeDtypeStruct((8, BLK), F32), jax.ShapeDtypeStruct((8, BLK), F32)] + comm.out_shape(),
        scratch_shapes=[pltpu.VMEM((SSD_W, SSD_STATE), F32)] + comm.scratch(),
        compiler_params=_cp("arbitrary"),
    )(xc, dt, par, st, dy, *comm.args())
    return res[0], res[1], res[2], res[3], list(res[4:])


def _ssd_gate(y, z, w):
    t = y * _silu(z)
    outs = []
    for g in (0, 1):
        tg = t[:, g * 256:(g + 1) * 256]
        outs.append(tg * lax.rsqrt(jnp.mean(tg * tg, axis=-1, keepdims=True) + SSD_NORM_EPS))
    return jnp.concatenate(outs, axis=1) * w


def _ssd_post(y, proj, norm_w, name):
    def fn(rv, hv, cv):
        return [_ssd_gate(rv[0], rv[1], cv[0])], []
    return _rows(fn, [y, (proj, SSD_W, C_Z // SSD_W)], [norm_w.reshape(1, -1)], [(SSD_W, MXU)], tile=512, name=name)[0]


def _ssd_post_bwd(y, proj, norm_w, dout, name):
    def fn(rv, hv, cv):
        yb, zb, db = rv
        _, vjp = jax.vjp(lambda a, b: _ssd_gate(a, b, cv[0]), yb, zb)
        dy, dz = vjp(db)
        t = yb * _silu(zb)
        nrm = []
        for g in (0, 1):
            tg = t[:, g * 256:(g + 1) * 256]
            nrm.append(tg * lax.rsqrt(jnp.mean(tg * tg, axis=-1, keepdims=True) + SSD_NORM_EPS))
        return [dy, dz], [_colsum8(db * jnp.concatenate(nrm, axis=1))]
    return _rows(fn, [y, (proj, SSD_W, C_Z // SSD_W), dout], [norm_w.reshape(1, -1)],
                 [(SSD_W, F32), (SSD_W, MXU)], [(8, SSD_W)], tile=512, name=name)


LRU_T = 256


def _lru_conv(proj, conv_w, conv_b, name):
    def fn(rv, hv, cv):
        return [_conv(rv[0], hv[0], cv[0], cv[1])], []
    return _rows(fn, [(proj, LRU_W, C_XL // LRU_W)], [_pad8(conv_w), conv_b.reshape(1, -1)], [(LRU_W, F32)],
                 tile=512, name=name, halos=[(0, "prev")])[0]


def _lru_conv_bwd(proj, dxc, conv_w, name):
    def fn(rv, hv, cv):
        dx, dw, db = _conv_bwd(rv[0], hv[0], rv[1], hv[1], cv[0])
        return [dx], [dw, jnp.concatenate([db, jnp.zeros((7, db.shape[1]), F32)], axis=0)]
    return _rows(fn, [(proj, LRU_W, C_XL // LRU_W), dxc], [_pad8(conv_w)], [(LRU_W, MXU)], [(8, LRU_W), (8, LRU_W)],
                 tile=512, name=name, halos=[(0, "prev"), (1, "next")])


def _lru_au(pre_a, pre_x, xc, ba, bx, lam):
    r = _sigmoid(pre_a + ba)
    i = _sigmoid(pre_x + bx)
    log_a = -LRU_C * r * _softplus(-lam)
    a = jnp.exp(log_a)
    u = jnp.sqrt(1.0 - jnp.exp(2.0 * log_a)) * (i * xc)
    return a, u


def _lru_scan(pre, xc, proj, par, name):
    s = xc.shape[0]
    t = LRU_T

    def body(pre_ref, xc_ref, g_ref, par_ref, out_ref, h_ref, carry):
        c = pl.program_id(0)

        @pl.when(c == 0)
        def _():
            carry[...] = jnp.zeros_like(carry)

        a, u = _lru_au(pre_ref[:, :LRU_W], pre_ref[:, LRU_W:], xc_ref[...], par_ref[0:1, :], par_ref[1:2, :], par_ref[2:3, :])
        row = lax.broadcasted_iota(jnp.int32, (t, LRU_W), 0)
        sft = 1
        while sft < t:
            keep = row >= sft
            a_s = jnp.where(keep, pltpu.roll(a, sft, 0), 1.0)
            u_s = jnp.where(keep, pltpu.roll(u, sft, 0), 0.0)
            u = a * u_s + u
            a = a * a_s
            sft *= 2
        h = a * carry[0:1, :] + u
        h_ref[...] = h
        out_ref[...] = (h * _gelu(g_ref[...])).astype(out_ref.dtype)
        carry[0:1, :] = h[t - 1:t, :]

    return pl.pallas_call(
        body, name=name, grid=(s // t,),
        in_specs=[pl.BlockSpec((t, 2 * LRU_W), lambda c: (c, 0)), pl.BlockSpec((t, LRU_W), lambda c: (c, 0)),
                  pl.BlockSpec((t, LRU_W), lambda c: (c, C_G // LRU_W)), pl.BlockSpec((8, LRU_W), lambda c: (0, 0))],
        out_specs=[pl.BlockSpec((t, LRU_W), lambda c: (c, 0))] * 2,
        out_shape=[jax.ShapeDtypeStruct((s, LRU_W), MXU), jax.ShapeDtypeStruct((s, LRU_W), F32)],
        scratch_shapes=[pltpu.VMEM((8, LRU_W), F32)],
        compiler_params=_cp("arbitrary"),
    )(pre, xc, proj, par)


def _lru_scan_bwd(pre, xc, proj, par, h, dout, name):
    s = xc.shape[0]
    t = LRU_T
    n = s // t
    t8 = t // 8

    def body(pre_ref, xc_ref, g_ref, par_ref, h_ref, hh_ref, do_ref, dpre_ref, dxc_ref, dg_ref, dpar_ref, carry):
        c = pl.program_id(0)

        @pl.when(c == 0)
        def _():
            carry[...] = jnp.zeros_like(carry)
            dpar_ref[...] = jnp.zeros_like(dpar_ref)

        pa, px, xcb = pre_ref[:, :LRU_W], pre_ref[:, LRU_W:], xc_ref[...]
        ba, bx, lam = par_ref[0:1, :], par_ref[1:2, :], par_ref[2:3, :]
        (a, u), vjp = jax.vjp(_lru_au, pa, px, xcb, ba, bx, lam)
        g = g_ref[...]
        hcur = h_ref[...]
        do = do_ref[...]
        _, gvjp = jax.vjp(_gelu, g)
        dg_ref[...] = gvjp(do * hcur)[0].astype(dg_ref.dtype)
        row = lax.broadcasted_iota(jnp.int32, (t, LRU_W), 0)
        v = do * _gelu(g) + jnp.where(row == t - 1, carry[0:1, :], 0.0)
        b = jnp.where(row == t - 1, 0.0, pltpu.roll(a, t - 1, 0))
        sft = 1
        while sft < t:
            keep = row < t - sft
            b_s = jnp.where(keep, pltpu.roll(b, t - sft, 0), 1.0)
            v_s = jnp.where(keep, pltpu.roll(v, t - sft, 0), 0.0)
            v = b * v_s + v
            b = b * b_s
            sft *= 2
        dh = v
        carry[0:1, :] = a[0:1, :] * dh[0:1, :]
        hhalo = jnp.where(c == n - 1, 0.0, hh_ref[...])
        hprev = _shift_down(hcur, hhalo, 1)
        dpa, dpx, dxc, dba, dbx, dlam = vjp((dh * hprev, dh))
        dpre_ref[:, :LRU_W] = dpa
        dpre_ref[:, LRU_W:] = dpx
        dxc_ref[...] = dxc
        dpar_ref[0:1, :] += dba
        dpar_ref[1:2, :] += dbx
        dpar_ref[2:3, :] += dlam

    rev = lambda c: (n - 1 - c, 0)
    return pl.pallas_call(
        body, name=name, grid=(n,),
        in_specs=[pl.BlockSpec((t, 2 * LRU_W), rev), pl.BlockSpec((t, LRU_W), rev),
                  pl.BlockSpec((t, LRU_W), lambda c: (n - 1 - c, C_G // LRU_W)), pl.BlockSpec((8, LRU_W), lambda c: (0, 0)),
                  pl.BlockSpec((t, LRU_W), rev),
                  pl.BlockSpec((8, LRU_W), lambda c: (jnp.maximum((n - 1 - c) * t8 - 1, 0), 0)),
                  pl.BlockSpec((t, LRU_W), lambda c: (n - 1 - c, dout.shape[1] // LRU_W - 1))],
        out_specs=[pl.BlockSpec((t, 2 * LRU_W), rev), pl.BlockSpec((t, LRU_W), rev), pl.BlockSpec((t, LRU_W), rev),
                   pl.BlockSpec((8, LRU_W), lambda c: (0, 0))],
        out_shape=[jax.ShapeDtypeStruct((s, 2 * LRU_W), F32), jax.ShapeDtypeStruct((s, LRU_W), F32),
                   jax.ShapeDtypeStruct((s, LRU_W), MXU), jax.ShapeDtypeStruct((8, LRU_W), F32)],
        scratch_shapes=[pltpu.VMEM((8, LRU_W), F32)],
        compiler_params=_cp("arbitrary"),
    )(pre, xc, proj, par, h, h, dout)


def _swiglu_act(gu, name):
    def fn(rv, hv, cv):
        return [_silu(rv[0].astype(F32)) * rv[1].astype(F32)], []
    return _rows(fn, [(gu, D_FF, 0), (gu, D_FF, 1)], [], [(D_FF, MXU)], tile=256, name=name)[0]


def _swiglu_bwd(gu, da, name):
    def fn(rv, hv, cv):
        gt, up, dab = [t.astype(F32) for t in rv]
        sg = _sigmoid(gt)
        dgate = dab * up * (sg * (1.0 + gt * (1.0 - sg)))
        dup = dab * (gt * sg)
        return [jnp.concatenate([dgate, dup], axis=1)], []
    return _rows(fn, [(gu, D_FF, 0), (gu, D_FF, 1), da], [], [(2 * D_FF, MXU)], tile=256, name=name)[0]


def _loss_head(x, g, target, name):
    d = x.shape[1]

    def fn(rv, hv, cv):
        xb, tb = rv
        y, vjp = jax.vjp(_rms, xb, cv[0])
        err = y - tb
        dy = err * (1.0 / d)
        dx, _ = vjp(dy)
        rstd = lax.rsqrt(jnp.mean(xb * xb, axis=-1, keepdims=True) + NORM_EPS)
        e2 = err * err * (0.5 / d)
        e2 = functools.reduce(lambda a, b: a + b, [e2[:, k * BLK:(k + 1) * BLK] for k in range(d // BLK)])
        return [dx], [_colsum8(dy * xb * rstd), _colsum8(e2)]
    return _rows(fn, [x, target], [g.reshape(1, -1)], [(d, F32)], [(8, d), (8, BLK)], tile=512, name=name)


ANY = pl.BlockSpec(memory_space=pl.ANY)


def _coords():
    return lax.axis_index("x"), lax.axis_index("y"), lax.axis_index("c")


class _Comm:
    def __init__(self, gathers=(), scatters=()):
        self.gathers = list(gathers)
        self.scatters = list(scatters)
        self.n = len(self.gathers) + len(self.scatters)

    def args(self):
        return [g[0] for g in self.gathers] + self.scatters

    def out_shape(self):
        out = [jax.ShapeDtypeStruct((4,) + (a.shape if l is None else a.shape[1:]), a.dtype) for a, l, _ in self.gathers]
        return out + [jax.ShapeDtypeStruct((3,) + a.shape[1:], a.dtype) for a in self.scatters]

    def scratch(self):
        if not self.n:
            return []
        return [pltpu.SemaphoreType.DMA((3 * self.n,)), pltpu.SemaphoreType.DMA((3 * self.n,)),
                pltpu.SemaphoreType.DMA((max(len(self.gathers), 1),)),
                pltpu.SemaphoreType.DMA((3 * self.n,)), pltpu.SemaphoreType.DMA((3 * self.n,))]

    def split(self, refs, n_in, n_out, n_scratch):
        refs = list(refs)
        n = self.n
        own = refs[:n_in] + refs[n_in + n:n_in + n + n_out] + refs[n_in + 2 * n + n_out:n_in + 2 * n + n_out + n_scratch]
        cm = (refs[n_in:n_in + n], refs[n_in + n + n_out:n_in + 2 * n + n_out], refs[n_in + 2 * n + n_out + n_scratch:])
        return own, cm

    def _copies(self, cm, arriving):
        ins, outs, (send, recv, local, _, _) = cm
        x, y, c = _coords()
        me = 2 * x + y
        chips = [(1 - x, y), (x, 1 - y), (1 - x, 1 - y)]
        remote, locals_ = [], []
        ng = len(self.gathers)
        for i in range(self.n):
            if i < ng:
                _, l, halved = self.gathers[i]
                slab = ins[i] if l is None else ins[i].at[l]
                if not arriving:
                    locals_.append(pltpu.make_async_copy(slab, outs[i].at[me], local.at[i]))
            for j, (px, py) in enumerate(chips):
                if i < ng:
                    slot = 2 * px + py if arriving else me
                    src, dst = (slab.at[c], outs[i].at[slot, c]) if halved else (slab, outs[i].at[slot])
                else:
                    src, dst = ins[i].at[2 * px + py], outs[i].at[j]
                remote.append(pltpu.make_async_remote_copy(src, dst, send.at[3 * i + j], recv.at[3 * i + j],
                                                           device_id=(px, py, c), device_id_type=MESH))
        return remote, locals_

    def _handovers(self, cm, arriving):
        _, outs, (_, _, _, send, recv) = cm
        x, y, c = _coords()
        chips = [(1 - x, y), (x, 1 - y), (1 - x, 1 - y)]
        cps = []
        for i, (_, _, halved) in enumerate(self.gathers):
            if halved:
                for j, (px, py) in enumerate(chips):
                    src = outs[i].at[2 * px + py, c]
                    dst = outs[i].at[2 * px + py, 1 - c if arriving else c]
                    cps.append(pltpu.make_async_remote_copy(src, dst, send.at[3 * i + j], recv.at[3 * i + j],
                                                            device_id=(x, y, 1 - c), device_id_type=MESH))
        return cps

    def start_at(self, cond, cm):
        def go():
            remote, locals_ = self._copies(cm, False)
            for cp in locals_ + remote:
                cp.start()

        if self.n:
            go() if cond is True else pl.when(cond)(go)

    def wait_at(self, cond, cm):
        def go():
            for cp in self._copies(cm, True)[0]:
                cp.wait_recv()
            handed = self._handovers(cm, False)
            for cp in handed:
                cp.start()
            for cp in self._handovers(cm, True):
                cp.wait_recv()
            remote, locals_ = self._copies(cm, False)
            for cp in handed + remote:
                cp.wait_send()
            for cp in locals_:
                cp.wait()

        if self.n:
            go() if cond is True else pl.when(cond)(go)


def _comm_call(comm, name):
    def body(*refs):
        _, cm = comm.split(refs, 0, 0, 0)
        comm.start_at(True, cm)
        comm.wait_at(True, cm)

    return list(pl.pallas_call(
        body, name=name, in_specs=[ANY] * comm.n, out_specs=[ANY] * comm.n, out_shape=comm.out_shape(),
        scratch_shapes=comm.scratch(), compiler_params=pltpu.CompilerParams(has_side_effects=True),
    )(*comm.args()))


def _swap_sibling(arrs):
    n = len(arrs)

    def body(*refs):
        ins, outs, send, recv = refs[:n], refs[n:2 * n], refs[2 * n], refs[2 * n + 1]
        x, y, c = _coords()
        cps = [pltpu.make_async_remote_copy(ins[i], outs[i], send.at[i], recv.at[i], device_id=(x, y, 1 - c), device_id_type=MESH)
               for i in range(n)]
        for cp in cps:
            cp.start()
        for cp in cps:
            cp.wait_recv()
        for cp in cps:
            cp.wait_send()

    return list(pl.pallas_call(
        body, name="swap_sibling", in_specs=[ANY] * n, out_specs=[ANY] * n,
        out_shape=[jax.ShapeDtypeStruct(a.shape, a.dtype) for a in arrs],
        scratch_shapes=[pltpu.SemaphoreType.DMA((n,)), pltpu.SemaphoreType.DMA((n,))],
        compiler_params=pltpu.CompilerParams(has_side_effects=True),
    )(*arrs))


def _gather_small(gs):
    def body(g_ref, o_ref, send_sems, recv_sems, local_sem):
        x, y, c = _coords()
        me = 4 * x + 2 * y + c
        mine = pltpu.make_async_copy(g_ref, o_ref.at[me], local_sem)
        mine.start()
        sends = []
        for k in range(1, 8):
            px, py, pc = x ^ (k >> 2), y ^ ((k >> 1) & 1), c ^ (k & 1)
            sends.append((pltpu.make_async_remote_copy(g_ref, o_ref.at[me], send_sems.at[k - 1], recv_sems.at[k - 1],
                                                       device_id=(px, py, pc), device_id_type=MESH), 4 * px + 2 * py + pc, k))
        for cp, _, _ in sends:
            cp.start()
        for cp, src, k in sends:
            pltpu.make_async_remote_copy(g_ref, o_ref.at[src], send_sems.at[k - 1], recv_sems.at[k - 1],
                                         device_id=(x, y, c), device_id_type=MESH).wait_recv()
        for cp, _, _ in sends:
            cp.wait_send()
        mine.wait()

    return pl.pallas_call(
        body, name="gather_small", in_specs=[ANY], out_specs=ANY,
        out_shape=jax.ShapeDtypeStruct((8,) + gs.shape, gs.dtype),
        scratch_shapes=[pltpu.SemaphoreType.DMA((7,)), pltpu.SemaphoreType.DMA((7,)), pltpu.SemaphoreType.DMA],
        compiler_params=pltpu.CompilerParams(has_side_effects=True),
    )(gs)


def _sum_slots(own, others, name, tile):
    k, r, c = others.shape

    def body(*refs):
        if own is None:
            o_ref, out_ref = refs
            acc = o_ref[0].astype(F32)
            first = 1
        else:
            own_ref, o_ref, out_ref = refs
            acc = own_ref[...]
            first = 0
        for j in range(first, k):
            acc = acc + o_ref[j].astype(F32)
        out_ref[...] = acc

    row = pl.BlockSpec((tile, c), lambda i: (i, 0))
    specs = ([] if own is None else [row]) + [pl.BlockSpec((k, tile, c), lambda i: (0, i, 0))]
    args = ([] if own is None else [own]) + [others]
    return pl.pallas_call(body, name=name, grid=(r // tile,), in_specs=specs, out_specs=row,
                          out_shape=jax.ShapeDtypeStruct((r, c), F32), compiler_params=_cp("parallel"))(*args)


def _adamw(w, m, v, ga, gb, name, tile):
    lead = w.ndim - 2
    r, c = w.shape[-2:]

    def body(*refs):
        vals = [ref[0] if lead else ref[...] for ref in refs[:len(refs) - 4]]
        w_, m_, v_, g = vals[0], vals[1], vals[2], vals[3]
        if gb is not None:
            g = g + vals[4]
        nm = ADAM_B1 * m_ + (1.0 - ADAM_B1) * g
        nv = ADAM_B2 * v_ + (1.0 - ADAM_B2) * (g * g)
        d = -ADAM_LR * ((nm / BC1) / (jnp.sqrt(nv / BC2) + ADAM_EPS) + ADAM_WD * w_)
        for ref, val in zip(refs[len(refs) - 4:], (g, d, nm, nv)):
            if lead:
                ref[0] = val
            else:
                ref[...] = val

    if lead:
        row = pl.BlockSpec((1, tile, c), lambda l, i: (l, i, 0))
        grid = (w.shape[0], r // tile)
    else:
        row = pl.BlockSpec((tile, c), lambda i: (i, 0))
        grid = (r // tile,)
    args = [w, m, v, ga] + ([] if gb is None else [gb])
    return pl.pallas_call(body, name=name, grid=grid, in_specs=[row] * len(args), out_specs=[row] * 4,
                          out_shape=[jax.ShapeDtypeStruct(w.shape, F32)] * 4,
                          compiler_params=_cp(*(["parallel"] * len(grid))))(*args)


MATS = ("w_in", "w_out", "w_gate", "w_up", "w_down")
CONVS = ("ssd_conv_w", "lru_conv_w")
BIG = MATS + CONVS
TRANSPOSED = ("w_gate", "w_up")
COL_SHARDED = ("w_in", "ssd_conv_w", "lru_conv_w")
SMALL = ("norm_mix", "ssd_conv_b", "ssd_dt_bias", "ssd_a_log", "ssd_d", "ssd_norm", "lru_conv_b", "lru_wa", "lru_ba",
         "lru_wx", "lru_bx", "lru_lambda", "norm_ffn", "norm_final")
WEIGHTS = ("norm_mix", "w_in", "ssd_conv_w", "ssd_conv_b", "ssd_dt_bias", "ssd_a_log", "ssd_d", "ssd_norm", "lru_conv_w",
           "lru_conv_b", "lru_wa", "lru_ba", "lru_wx", "lru_bx", "lru_lambda", "w_out", "norm_ffn", "w_gate", "w_up",
           "w_down", "norm_final")
ROW_TILE = {"w_in": 256, "w_out": 128, "w_gate": 352, "w_up": 352, "w_down": 352}


def _pack(arrs, width, row_mult, dtype):
    flat = jnp.concatenate([a.reshape(-1).astype(dtype) for a in arrs])
    rows = -(-flat.shape[0] // width)
    rows = -(-rows // row_mult) * row_mult
    flat = jnp.pad(flat, (0, rows * width - flat.shape[0]))
    return flat.reshape(rows, width)


def _unpack(buf, shapes):
    flat = buf.reshape(-1)
    out, off = [], 0
    for shp in shapes:
        n = int(np.prod(shp))
        out.append(flat[off:off + n].reshape(shp))
        off += n
    return out


def _join(name, g4):
    if name in COL_SHARDED:
        return jnp.moveaxis(g4, 0, -2).reshape(g4.shape[1:-1] + (4 * g4.shape[-1],))
    return g4.reshape((4 * g4.shape[1],) + g4.shape[2:])


def _slabs(name, g):
    if name in COL_SHARDED:
        return jnp.moveaxis(g.reshape(g.shape[:-1] + (4, g.shape[-1] // 4)), -2, 0)
    return g.reshape((4, g.shape[0] // 4) + g.shape[1:])


def _perm_cols(w):
    pad = jnp.zeros(w.shape[:-1] + (NP - IN_COLS,), w.dtype)
    return jnp.concatenate([w[..., :3072], w[..., 3080:4104], w[..., 3072:3080], pad], axis=-1)


def _unperm_cols(g):
    return jnp.concatenate([g[..., :3072], g[..., C_DT:C_DT + 8], g[..., 3072:4096]], axis=-1)


def _block_diag(w):
    eye = jnp.eye(LRU_BLOCKS, dtype=w.dtype)
    return jnp.einsum("ncd,nm->ncmd", w, eye).reshape(LRU_W, LRU_W)


def _block_diag_extract(g):
    g4 = g.reshape(LRU_BLOCKS, 64, LRU_BLOCKS, 64)
    return jnp.stack([g4[n, :, n, :] for n in range(LRU_BLOCKS)], axis=0)


def _lanes128(v):
    return jnp.pad(v, (0, BLK - v.shape[0])).reshape(1, BLK)


def _layer_mixers(x, p, comm=None, h=None):
    if h is None:
        h = _rms_fwd(x, p["norm_mix"], "rms_mix")
    proj = _mm(h, p["w_in"], tm=1024, tn=1408, tk=1024, name="mm_in")
    att, lse, attb, got = _att_fwd_fused(proj, "att_fwd", comm)
    xconv, dt = _ssd_pre(proj, p["ssd_conv_w"], p["ssd_conv_b"], _lanes128(p["ssd_dt_bias"]), "ssd_pre")
    spar = jnp.concatenate([_lanes128(p["ssd_a_log"]), _lanes128(p["ssd_d"]), jnp.zeros((6, BLK), F32)], axis=0)
    y, states = _ssd_scan(xconv, dt, spar, "ssd_scan")
    ssd = _ssd_post(y, proj, p["ssd_norm"], "ssd_post")
    xc = _lru_conv(proj, p["lru_conv_w"], p["lru_conv_b"], "lru_conv")
    wab = jnp.concatenate([_block_diag(p["lru_wa"]), _block_diag(p["lru_wx"])], axis=1).astype(MXU)
    pre = _mm(xc, wab, tm=1024, tn=1024, tk=512, name="mm_lru")
    lpar = jnp.concatenate([p["lru_ba"].reshape(1, -1), p["lru_bx"].reshape(1, -1), p["lru_lambda"].reshape(1, -1),
                            jnp.zeros((5, LRU_W), F32)], axis=0)
    lru, hs = _lru_scan(pre, xc, proj, lpar, "lru_scan")
    mix = jnp.concatenate([attb, ssd, lru], axis=1)
    saved = dict(x=x, h=h, proj=proj, att=att, lse=lse, xconv=xconv, dt=dt, spar=spar, y=y, states=states, xc=xc, wab=wab,
                 pre=pre, lpar=lpar, hs=hs, mix=mix)
    return mix, saved, got


def _layer_ffn(x, mix, p, saved, comm=None):
    x1 = _mm(mix, p["w_out"], add=x, tm=1024, tn=1024, tk=1536, name="mm_out")
    h2 = _rms_fwd(x1, p["norm_ffn"], "rms_ffn")
    gu = _mm(h2, p["w_gu_t"], tb=True, out_dtype=MXU, tm=1024, tn=1408, tk=1024, name="mm_gu", comm=comm)
    gu, got = gu if comm is not None else (gu, [])
    act = _swiglu_act(gu, "swiglu_act")
    x2 = _mm(act, p["w_down"], add=x1, tm=1024, tn=1024, tk=2816, name="mm_down")
    saved.update(x1=x1, h2=h2, gu=gu, act=act)
    return x2, got


def _layer_bwd(dx2, p, sv, comm_ssd=None, comm_att=None, comm_tail=None):
    g = {}
    da = _mm(dx2, p["w_down"], tb=True, out_dtype=MXU, tm=1024, tn=1408, tk=1024, name="mm_d_act")
    g["w_down"] = _mm(sv["act"], dx2, ta=True, tm=1408, tn=1024, tk=1024, name="mm_g_down")
    dgu = _swiglu_bwd(sv["gu"], da, "swiglu_bwd")
    dh2 = _mm(dgu, p["w_gu_t"], tm=1024, tn=1024, tk=1408, name="mm_d_h2")
    g["w_gu_t"] = _mm(dgu, sv["h2"], ta=True, tm=1408, tn=1024, tk=1024, name="mm_g_gu")
    dx1, gn = _rms_bwd(sv["x1"], p["norm_ffn"], dh2, dx2, "rms_ffn_bwd")
    g["norm_ffn"] = jnp.sum(gn, axis=0)
    dmix = _mm(dx1, p["w_out"], tb=True, tm=1024, tn=1536, tk=1024, name="mm_d_mix")
    g["w_out"] = _mm(sv["mix"], dx1, ta=True, tm=1536, tn=1024, tk=1024, name="mm_g_out")
    proj = sv["proj"]
    dpre, dxc_u, dgl, dlpar = _lru_scan_bwd(sv["pre"], sv["xc"], proj, sv["lpar"], sv["hs"], dmix, "lru_scan_bwd")
    dxc = _mm(dpre, sv["wab"], tb=True, add=dxc_u, tm=1024, tn=512, tk=1024, name="mm_d_xc")
    gwab = _mm(sv["xc"], dpre, ta=True, tm=512, tn=1024, tk=1024, name="mm_g_lru")
    g["lru_wa"], g["lru_wx"] = _block_diag_extract(gwab[:, :LRU_W]), _block_diag_extract(gwab[:, LRU_W:])
    g["lru_ba"], g["lru_bx"], g["lru_lambda"] = dlpar[0], dlpar[1], dlpar[2]
    dxl, gcw, gcb = _lru_conv_bwd(proj, dxc, p["lru_conv_w"], "lru_conv_bwd")
    g["lru_conv_w"], g["lru_conv_b"] = gcw[:CONV_K], jnp.sum(gcb, axis=0)
    dy, dz, gsn = _ssd_post_bwd(sv["y"], proj, p["ssd_norm"], (dmix, SSD_W, 1), "ssd_post_bwd")
    g["ssd_norm"] = jnp.sum(gsn, axis=0)
    dxconv, ddt, dal, ddk, got_ssd = _ssd_scan_bwd(sv["xconv"], sv["dt"], sv["spar"], sv["states"], dy, "ssd_scan_bwd", comm_ssd)
    g["ssd_a_log"], g["ssd_d"] = dal[0, :8], ddk[0, :8]
    dxbc, ddtr, gsw, gsb, gdb = _ssd_pre_bwd(proj, dxconv, ddt, p["ssd_conv_w"], p["ssd_conv_b"],
                                             _lanes128(p["ssd_dt_bias"]), "ssd_pre_bwd")
    g["ssd_conv_w"], g["ssd_conv_b"], g["ssd_dt_bias"] = gsw[:CONV_K], jnp.sum(gsb, axis=0), jnp.sum(gdb, axis=0)[:8]
    delta = _att_delta((dmix, ATT_W, 0), sv["att"], "att_delta")
    dq, dk, dv, got_att = _att_bwd_rev(proj, dmix, sv["lse"], delta, "att_bwd", None if comm_att is None else comm_att(g))
    dproj = jnp.concatenate([dq, dk, dv, dz, dxbc, dgl, dxl, ddtr], axis=1)
    g["w_in"] = _mm(sv["h"], dproj, ta=True, tm=1024, tn=1408, tk=1024, name="mm_g_in")
    dh = _mm(dproj, p["w_in"], tb=True, tm=1024, tn=1024, tk=1408, name="mm_d_h", comm=None if comm_tail is None else comm_tail(g))
    dh, got_tail = dh if comm_tail is not None else (dh, [])
    dx, gm = _rms_bwd(sv["x"], p["norm_mix"], dh, dx1, "rms_mix_bwd")
    g["norm_mix"] = jnp.sum(gm, axis=0)
    return dx, g, got_ssd, got_att, got_tail


def _grad_slabs(g, names):
    out = {}
    for n in names:
        if n == "w_in":
            out[n] = _slabs(n, _unperm_cols(g["w_in"]))
        elif n == "w_gate":
            out[n] = _slabs(n, g["w_gu_t"][:D_FF])
        elif n == "w_up":
            out[n] = _slabs(n, g["w_gu_t"][D_FF:])
        else:
            out[n] = _slabs(n, g[n])
    return out


def kernel(x, norm_mix, w_in, ssd_conv_w, ssd_conv_b, ssd_dt_bias, ssd_a_log, ssd_d, ssd_norm, lru_conv_w, lru_conv_b, lru_wa, lru_ba, lru_wx, lru_bx, lru_lambda, w_out, norm_ffn, w_gate, w_up, w_down, norm_final, loss_target, m_norm_mix, m_w_in, m_ssd_conv_w, m_ssd_conv_b, m_ssd_dt_bias, m_ssd_a_log, m_ssd_d, m_ssd_norm, m_lru_conv_w, m_lru_conv_b, m_lru_wa, m_lru_ba, m_lru_wx, m_lru_bx, m_lru_lambda, m_w_out, m_norm_ffn, m_w_gate, m_w_up, m_w_down, m_norm_final, v_norm_mix, v_w_in, v_ssd_conv_w, v_ssd_conv_b, v_ssd_dt_bias, v_ssd_a_log, v_ssd_d, v_ssd_norm, v_lru_conv_w, v_lru_conv_b, v_lru_wa, v_lru_ba, v_lru_wx, v_lru_bx, v_lru_lambda, v_w_out, v_norm_ffn, v_w_gate, v_w_up, v_w_down, v_norm_final):
    loc = dict(locals())
    w = {n: loc[n] for n in WEIGHTS}
    m = {n: loc["m_" + n] for n in WEIGHTS}
    v = {n: loc["v_" + n] for n in WEIGHTS}
    for n in TRANSPOSED:
        w[n], m[n], v[n] = [jnp.transpose(t, (0, 2, 1)) for t in (w[n], m[n], v[n])]

    def halves(a):
        return a.reshape(a.shape[0], 2, a.shape[1] // 2, a.shape[2])

    def unhalve(a):
        return a.reshape(4, 2 * a.shape[2], a.shape[3])

    wb = {n: halves(w[n].astype(MXU)) for n in MATS}
    xs = x[0]
    h0, first = _rms_fwd(xs, norm_mix[0], "rms_mix", _Comm(gathers=[(wb["w_in"], 0, True), (w["ssd_conv_w"], None, False),
                                                                    (w["lru_conv_w"], None, False)]))
    convs = {"ssd_conv_w": _join("ssd_conv_w", first[1]), "lru_conv_w": _join("lru_conv_w", first[2])}
    behind_att = [(n, 0) for n in MATS[1:]] + [("w_in", 1)]
    behind_ffn = [(n, 1) for n in MATS[1:]]
    whole = {("w_in", 0): _join("w_in", unhalve(first[0]))}
    params = {}

    def layer_params(l):
        if l not in params:
            p = {n: w[n][l] for n in SMALL if n != "norm_final"}
            p.update(w_in=_perm_cols(whole["w_in", l]), ssd_conv_w=convs["ssd_conv_w"][l], lru_conv_w=convs["lru_conv_w"][l])
            params[l] = p
        if "w_out" not in params[l] and ("w_out", l) in whole:
            params[l].update(w_out=whole["w_out", l], w_down=whole["w_down", l],
                             w_gu_t=jnp.concatenate([whole["w_gate", l], whole["w_up", l]], axis=0))
        return params[l]

    saved = []
    for l in range(DEPTH):
        first_layer = l == 0
        mix, sv, got = _layer_mixers(xs, layer_params(l), _Comm(gathers=[(wb[n], k, True) for n, k in behind_att]) if first_layer else None,
                                     h0 if first_layer else None)
        whole.update({k: _join(k[0], unhalve(a)) for k, a in zip(behind_att, got)})
        xs, got = _layer_ffn(xs, mix, layer_params(l), sv, _Comm(gathers=[(wb[n], k, True) for n, k in behind_ffn]) if first_layer else None)
        whole.update({k: _join(k[0], unhalve(a)) for k, a in zip(behind_ffn, got)})
        saved.append(sv)
    dx, gnf, lsum = _loss_head(xs, norm_final, loss_target[0], "loss_head")
    loss = lax.psum(jnp.sum(lsum), ("x", "y", "c"))

    dx, g1, _, _, _ = _layer_bwd(dx, layer_params(1), saved[1])
    s1 = _grad_slabs(g1, BIG)
    att0 = ("w_gate", "w_up", "w_down", "w_out")
    s0 = {}

    def wire(s, n):
        return s[n].astype(MXU) if n in MATS else s[n]

    def comm_att(g0):
        s0.update(_grad_slabs(g0, att0))
        return _Comm(scatters=[wire(s0, n) for n in att0])

    tail0 = ("w_in",) + CONVS

    def comm_tail(g0):
        s0.update(_grad_slabs(g0, tail0))
        return _Comm(scatters=[wire(s0, n) for n in tail0])

    dx, g0, got_ssd, got_att, got_tail = _layer_bwd(dx, layer_params(0), saved[0], _Comm(scatters=[wire(s1, n) for n in BIG]),
                                                    comm_att, comm_tail)
    recv = {(n, 1): a for n, a in zip(BIG, got_ssd)}
    recv.update({(n, 0): a for n, a in zip(att0, got_att)})
    recv.update({(n, 0): a for n, a in zip(tail0, got_tail)})

    me = 2 * lax.axis_index("x") + lax.axis_index("y")
    slabs = (s0, s1)
    part = {}
    for n in BIG:
        per_layer = []
        for l in range(DEPTH):
            own = lax.dynamic_index_in_dim(slabs[l][n], me, axis=0, keepdims=False)
            per_layer.append(_sum_slots(own, recv[n, l], "sum_chips_" + n, ROW_TILE.get(n, own.shape[0])))
        part[n] = jnp.stack(per_layer, axis=0)
    sib = dict(zip(BIG, _swap_sibling([part[n] for n in BIG])))
    out_g, out_d, out_m, out_v = {}, {}, {}, {}
    for n in BIG:
        res = _adamw(w[n], m[n], v[n], part[n], sib[n], "adamw_" + n, ROW_TILE.get(n, w[n].shape[1]))
        out_g[n], out_d[n], out_m[n], out_v[n] = [jnp.transpose(t, (0, 2, 1)) for t in res] if n in TRANSPOSED else res

    gsm = {n: jnp.stack([g0[n], g1[n]], axis=0) for n in SMALL if n != "norm_final"}
    gsm["norm_final"] = jnp.sum(gnf, axis=0)
    small_shapes = [w[n].shape for n in SMALL]
    gs = _pack([gsm[n].reshape(w[n].shape) for n in SMALL], BLK, 8, F32)
    gall = _gather_small(gs)
    gsum = _sum_slots(None, gall, "sum_devices", gs.shape[0])
    ws = _pack([w[n] for n in SMALL], BLK, 8, F32)
    ms = _pack([m[n] for n in SMALL], BLK, 8, F32)
    vs = _pack([v[n] for n in SMALL], BLK, 8, F32)
    gsr, dsr, nms, nvs = _adamw(ws, ms, vs, gsum, None, "adamw_small", gs.shape[0])
    out_g.update(zip(SMALL, _unpack(gsr, small_shapes)))
    out_d.update(zip(SMALL, _unpack(dsr, small_shapes)))
    out_m.update(zip(SMALL, _unpack(nms, small_shapes)))
    out_v.update(zip(SMALL, _unpack(nvs, small_shapes)))

    return (loss, dx[None], *[out_g[n] for n in WEIGHTS], *[out_d[n] for n in WEIGHTS],
            *[out_m[n] for n in WEIGHTS], *[out_v[n] for n in WEIGHTS])
```

```python
import functools
import math

import jax
import jax.numpy as jnp
import numpy as np
from jax import lax
from jax.experimental import pallas as pl
from jax.experimental.pallas import tpu as pltpu

F32 = jnp.float32
MXU = jnp.bfloat16
HI = lax.Precision.HIGHEST
MESH = pl.DeviceIdType.MESH

D_MODEL = 1024
DEPTH = 2
HEAD_DIM = 64
ATT_W = 512
ATT_PATTERNS = ((128, 1), (512, 4), (2048, 16))
BLK = 128
SSD_W = 512
SSD_STATE = 128
LRU_W = 512
LRU_BLOCKS = 8
LRU_C = 8.0
CONV_K = 4
D_MIX = 1536
D_FF = 2816
IN_COLS = 4104
NP = 4224
NORM_EPS = 1e-6
SSD_NORM_EPS = 1e-5
LN2 = math.log(2.0)
NEG = -1e30

ADAM_LR, ADAM_B1, ADAM_B2, ADAM_EPS, ADAM_WD, ADAM_STEP = 0.001, 0.9, 0.999, 1e-08, 0.01, 10
BC1 = 1.0 - ADAM_B1 ** ADAM_STEP
BC2 = 1.0 - ADAM_B2 ** ADAM_STEP

VMEM_LIMIT = 56 * 1024 * 1024

C_Q, C_K, C_V, C_Z, C_XBC, C_G, C_XL, C_DT = 0, 512, 1024, 1536, 2048, 3072, 3584, 4096


def _cp(*sem):
    return pltpu.CompilerParams(dimension_semantics=sem, vmem_limit_bytes=VMEM_LIMIT)


def _dot(a, b, dims, prec=None):
    return lax.dot_general(a, b, (dims, ((), ())), preferred_element_type=F32, precision=prec)


def _nn(a, b, prec=None):
    return _dot(a, b, ((1,), (0,)), prec)


def _nt(a, b, prec=None):
    return _dot(a, b, ((1,), (1,)), prec)


def _tn(a, b, prec=None):
    return _dot(a, b, ((0,), (0,)), prec)


def _sigmoid(x):
    return jax.nn.sigmoid(x)


def _silu(x):
    return x * _sigmoid(x)


def _softplus(x):
    return jnp.maximum(x, 0.0) + jnp.log(1.0 + jnp.exp(-jnp.abs(x)))


def _gelu(x):
    return 0.5 * x * (1.0 + jnp.tanh(0.7978845608028654 * (x + 0.044715 * x * x * x)))


def _mm(a, b, *, ta=False, tb=False, add=None, out_dtype=F32, tm, tn, tk, name, comm=None):
    m, k = (a.shape[1], a.shape[0]) if ta else a.shape
    n = b.shape[0] if tb else b.shape[1]
    assert (b.shape[1] if tb else b.shape[0]) == k
    assert m % tm == 0 and n % tn == 0 and k % tk == 0, (name, m, n, k)
    nk = k // tk
    a_spec = pl.BlockSpec((tk, tm), lambda i, j, kk: (kk, i)) if ta else pl.BlockSpec((tm, tk), lambda i, j, kk: (i, kk))
    b_spec = pl.BlockSpec((tn, tk), lambda i, j, kk: (j, kk)) if tb else pl.BlockSpec((tk, tn), lambda i, j, kk: (kk, j))
    o_spec = pl.BlockSpec((tm, tn), lambda i, j, kk: (i, j))
    dims = ((0 if ta else 1,), (1 if tb else 0,))
    carried = comm is not None
    comm = comm or _Comm()
    ni, nj = m // tm, n // tn

    def body(*refs):
        refs, cm = comm.split(refs, 2 if add is None else 3, 1, 1)
        if add is None:
            a_ref, b_ref, o_ref, acc = refs
        else:
            a_ref, b_ref, add_ref, o_ref, acc = refs
        i, j, kk = pl.program_id(0), pl.program_id(1), pl.program_id(2)
        comm.start_at((i == 0) & (j == 0) & (kk == 0), cm)

        @pl.when(kk == 0)
        def _():
            acc[...] = jnp.zeros_like(acc)

        acc[...] += _dot(a_ref[...].astype(MXU), b_ref[...].astype(MXU), dims)

        @pl.when(kk == nk - 1)
        def _():
            r = acc[...]
            if add is not None:
                r = r + add_ref[...]
            o_ref[...] = r.astype(out_dtype)

        comm.wait_at((i == ni - 1) & (j == nj - 1) & (kk == nk - 1), cm)

    ins = [a, b] + ([] if add is None else [add])
    specs = [a_spec, b_spec] + ([] if add is None else [o_spec])
    res = pl.pallas_call(
        body, name=name, grid=(ni, nj, nk), in_specs=specs + [ANY] * comm.n, out_specs=[o_spec] + [ANY] * comm.n,
        out_shape=[jax.ShapeDtypeStruct((m, n), out_dtype)] + comm.out_shape(),
        scratch_shapes=[pltpu.VMEM((tm, tn), F32)] + comm.scratch(),
        compiler_params=_cp(*((["arbitrary"] * 3) if comm.n else ["parallel", "parallel", "arbitrary"])),
    )(*ins, *comm.args())
    return (res[0], list(res[1:])) if carried else res[0]


def _rows(fn, rows, consts=(), outs=(), accs=(), *, tile, name, halos=(), comm=None):
    rows = [r if isinstance(r, tuple) else (r, r.shape[1], 0) for r in rows]
    s = rows[0][0].shape[0]
    assert s % tile == 0 and tile % 8 == 0
    n = s // tile
    t8 = tile // 8
    nr, nh, nc_, no, na = len(rows), len(halos), len(consts), len(outs), len(accs)
    carried = comm is not None
    comm = comm or _Comm()

    def body(*refs):
        refs, cm = comm.split(refs, nr + nh + nc_, no + na, 0)
        i = pl.program_id(0)
        comm.start_at(i == 0, cm)
        rv = [r[...] for r in refs[:nr]]
        hv = []
        for (idx, kind), r in zip(halos, refs[nr:nr + nh]):
            edge = (i == 0) if kind == "prev" else (i == n - 1)
            hv.append(jnp.where(edge, 0.0, r[...]))
        cv = [r[...] for r in refs[nr + nh:nr + nh + nc_]]
        o_refs = refs[nr + nh + nc_:nr + nh + nc_ + no]
        a_refs = refs[nr + nh + nc_ + no:]
        ov, av = fn(rv, hv, cv)
        for r, v in zip(o_refs, ov):
            r[...] = v.astype(r.dtype)
        if na:
            @pl.when(i == 0)
            def _():
                for r in a_refs:
                    r[...] = jnp.zeros_like(r)
            for r, v in zip(a_refs, av):
                r[...] += v
        comm.wait_at(i == n - 1, cm)

    in_specs = [pl.BlockSpec((tile, w), functools.partial(lambda i, cb: (i, cb), cb=cb)) for (_, w, cb) in rows]
    for idx, kind in halos:
        _, w, cb = rows[idx]
        if kind == "prev":
            in_specs.append(pl.BlockSpec((8, w), functools.partial(lambda i, cb: (jnp.maximum(i * t8 - 1, 0), cb), cb=cb)))
        else:
            in_specs.append(pl.BlockSpec((8, w), functools.partial(lambda i, cb: (jnp.minimum((i + 1) * t8, n * t8 - 1), cb), cb=cb)))
    in_specs += [pl.BlockSpec(c.shape, functools.partial(lambda i, nd: (0,) * nd, nd=c.ndim)) for c in consts]
    out_specs = [pl.BlockSpec((tile, c), lambda i: (i, 0)) for (c, _) in outs]
    out_specs += [pl.BlockSpec((r, c), lambda i: (0, 0)) for (r, c) in accs]
    out_shape = [jax.ShapeDtypeStruct((s, c), dt) for (c, dt) in outs]
    out_shape += [jax.ShapeDtypeStruct((r, c), F32) for (r, c) in accs]
    args = [r[0] for r in rows] + [rows[idx][0] for idx, _ in halos] + list(consts)
    res = pl.pallas_call(
        body, name=name, grid=(n,), in_specs=in_specs + [ANY] * comm.n, out_specs=out_specs + [ANY] * comm.n,
        out_shape=out_shape + comm.out_shape(), scratch_shapes=comm.scratch(), compiler_params=_cp("arbitrary"),
    )(*args, *comm.args())
    return (list(res[:no + na]), list(res[no + na:])) if carried else list(res)


def _colsum8(v):
    t, c = v.shape
    return jnp.sum(v.reshape(t // 8, 8, c), axis=0)


def _rms(x, g):
    return x * lax.rsqrt(jnp.mean(x * x, axis=-1, keepdims=True) + NORM_EPS) * g


def _rms_fwd(x, g, name, comm=None):
    def fn(rv, hv, cv):
        return [_rms(rv[0], cv[0])], []
    res = _rows(fn, [x], [g.reshape(1, -1)], [(x.shape[1], MXU)], tile=512, name=name, comm=comm)
    return res[0] if comm is None else (res[0][0], res[1])


def _rms_bwd(x, g, dh, dres, name):
    def fn(rv, hv, cv):
        xb, dhb, drb = rv
        _, vjp = jax.vjp(_rms, xb, cv[0])
        dx, _ = vjp(dhb)
        rstd = lax.rsqrt(jnp.mean(xb * xb, axis=-1, keepdims=True) + NORM_EPS)
        return [drb + dx], [_colsum8(dhb * xb * rstd)]
    d = x.shape[1]
    return _rows(fn, [x, dh, dres], [g.reshape(1, -1)], [(d, F32)], [(8, d)], tile=512, name=name)


def _slope_dist(hp, hh, dist, dil):
    hf = (2 * hp + hh + 1).astype(F32)
    slope = jnp.exp(jnp.zeros(dist.shape, F32) - hf * LN2)
    return slope * (dist.astype(F32) * float(dil))


def _att_delta(datt, att, name):
    def fn(rv, hv, cv):
        r = lax.broadcasted_iota(jnp.int32, (ATT_W, ATT_W), 0) // HEAD_DIM
        c = lax.broadcasted_iota(jnp.int32, (ATT_W, ATT_W), 1) // HEAD_DIM
        ones = (r == c).astype(F32)
        return [_nn(rv[0] * rv[1], ones, HI)], []
    return _rows(fn, [datt, att], [], [(ATT_W, F32)], tile=512, name=name)[0]


ATT_G = 2048


def _deinterleave(dst, src, dil, ld, region, offset):
    for r in range(dil):
        rows = pl.ds(r, ld, stride=dil) if dil > 1 else pl.ds(0, ld)
        dst[r * region + offset:r * region + offset + ld, :] = src[rows, :]


def _deinterleave_edge(dst, src, dil, region, offset, first_row):
    for r in range(dil):
        rows = pl.ds(first_row + r, BLK, stride=dil) if dil > 1 else pl.ds(first_row, BLK)
        dst[r * region + offset:r * region + offset + BLK, :] = src[rows, :]


def _att_fwd_fused(proj, name, comm=None):
    s, npc = proj.shape
    gsz = ATT_G
    ng = s // gsz
    assert s % gsz == 0
    scale = HEAD_DIM ** -0.5
    comm = comm or _Comm()

    def body(*refs):
        (q_ref, kp_ref, kc_ref, vp_ref, vc_ref, att_ref, lse_ref, attb_ref, qd, kd, vd, nd, md, dd, nn, mn, dn), cm = comm.split(refs, 5, 3, 9)
        hp, g = pl.program_id(0), pl.program_id(1)
        comm.start_at((hp == 0) & (g == 0), cm)
        lane = lax.broadcasted_iota(jnp.int32, (BLK, BLK), 1)
        qi = lax.broadcasted_iota(jnp.int32, (BLK, 2 * BLK), 0)
        ki = lax.broadcasted_iota(jnp.int32, (BLK, 2 * BLK), 1)
        dist = BLK + qi - ki
        band = (dist >= 0) & (dist <= BLK)
        for pi, (_, dil) in enumerate(ATT_PATTERNS):
            ld = gsz // dil
            nbg = ld // BLK
            _deinterleave(qd, q_ref, dil, ld, ld, 0)
            _deinterleave(kd, kc_ref, dil, ld, ld + BLK, BLK)
            _deinterleave(vd, vc_ref, dil, ld, ld + BLK, BLK)
            _deinterleave_edge(kd, kp_ref, dil, ld + BLK, 0, gsz - BLK * dil)
            _deinterleave_edge(vd, vp_ref, dil, ld + BLK, 0, gsz - BLK * dil)
            bias = [_slope_dist(hp, hh, dist, dil) for hh in (0, 1)]

            def tile(t, carry, ld=ld, nbg=nbg, bias=bias):
                r, b = t // nbg, t % nbg
                qo = pl.multiple_of(r * ld + b * BLK, BLK)
                ko = pl.multiple_of(r * (ld + BLK) + b * BLK, BLK)
                q = qd[pl.ds(qo, BLK), :]
                kk = kd[pl.ds(ko, 2 * BLK), :].astype(MXU)
                vv = vd[pl.ds(ko, 2 * BLK), :].astype(MXU)
                valid = band & ((g > 0) | (b > 0) | (ki >= BLK))
                num = jnp.zeros((BLK, BLK), F32)
                mx = jnp.zeros((BLK, BLK), F32)
                den = jnp.zeros((BLK, BLK), F32)
                for hh in (0, 1):
                    hmask = (lane < HEAD_DIM) if hh == 0 else (lane >= HEAD_DIM)
                    qm = jnp.where(hmask, q, 0.0).astype(MXU)
                    sc = jnp.where(valid, _nt(qm, kk) * scale - bias[hh], NEG)
                    m = jnp.max(sc, axis=1, keepdims=True)
                    p = jnp.exp(sc - m)
                    dn_ = jnp.sum(p, axis=1, keepdims=True)
                    o = _nn(p.astype(MXU), vv)
                    num = jnp.where(hmask, o, num)
                    mx = jnp.where(hmask, m, mx)
                    den = jnp.where(hmask, dn_, den)
                nd[pl.ds(qo, BLK), :] = num
                md[pl.ds(qo, BLK), :] = mx
                dd[pl.ds(qo, BLK), :] = den
                return carry

            lax.fori_loop(0, dil * nbg, tile, 0, unroll=8)
            for r in range(dil):
                rows = pl.ds(r, ld, stride=dil) if dil > 1 else pl.ds(0, ld)
                nn.at[pi][rows, :] = nd[r * ld:(r + 1) * ld, :]
                mn.at[pi][rows, :] = md[r * ld:(r + 1) * ld, :]
                dn.at[pi][rows, :] = dd[r * ld:(r + 1) * ld, :]

        def merge(c, carry):
            rows = pl.ds(pl.multiple_of(c * 256, 256), 256)
            ms = [mn[pi, rows, :] for pi in range(len(ATT_PATTERNS))]
            m_all = functools.reduce(jnp.maximum, ms)
            num = jnp.zeros((256, BLK), F32)
            den = jnp.zeros((256, BLK), F32)
            for pi in range(len(ATT_PATTERNS)):
                e = jnp.exp(ms[pi] - m_all)
                num = num + nn[pi, rows, :] * e
                den = den + dn[pi, rows, :] * e
            att = num / den
            att_ref[rows, :] = att
            attb_ref[rows, :] = att.astype(MXU)
            lse_ref[rows, :] = m_all + jnp.log(den)
            return carry

        lax.fori_loop(0, gsz // 256, merge, 0)
        comm.wait_at((hp == 3) & (g == ng - 1), cm)

    def cur(base):
        return pl.BlockSpec((gsz, BLK), lambda hp, g: (g, base // BLK + hp))

    def prev(base):
        return pl.BlockSpec((gsz, BLK), lambda hp, g: (jnp.maximum(g - 1, 0), base // BLK + hp))

    o_spec = pl.BlockSpec((gsz, BLK), lambda hp, g: (g, hp))
    npat = len(ATT_PATTERNS)
    res = pl.pallas_call(
        body, name=name, grid=(4, ng),
        in_specs=[cur(C_Q), prev(C_K), cur(C_K), prev(C_V), cur(C_V)] + [ANY] * comm.n,
        out_specs=[o_spec] * 3 + [ANY] * comm.n,
        out_shape=[jax.ShapeDtypeStruct((s, ATT_W), F32)] * 2 + [jax.ShapeDtypeStruct((s, ATT_W), MXU)] + comm.out_shape(),
        scratch_shapes=[pltpu.VMEM((gsz, BLK), F32), pltpu.VMEM((2 * gsz, BLK), F32), pltpu.VMEM((2 * gsz, BLK), F32)]
        + [pltpu.VMEM((gsz, BLK), F32)] * 3 + [pltpu.VMEM((npat, gsz, BLK), F32)] * 3 + comm.scratch(),
        compiler_params=_cp("arbitrary", "arbitrary"),
    )(proj, proj, proj, proj, proj, *comm.args())
    return res[0], res[1], res[2], list(res[3:])


def _att_bwd_fused(proj, datt, lse, delta, name, comm=None):
    s, npc = proj.shape
    gsz = ATT_G
    ng = s // gsz
    scale = HEAD_DIM ** -0.5
    comm = comm or _Comm()

    def body(*refs):
        (qc_ref, qn_ref, kp_ref, kc_ref, vp_ref, vc_ref, doc_ref, don_ref, lsc_ref, lsn_ref, dlc_ref, dln_ref,
         dq_ref, dk_ref, dv_ref, qd, dod, lsd, dld, kd, vd, dqd, dkd, dvd), cm = comm.split(refs, 12, 3, 9)
        hp, g = pl.program_id(0), pl.program_id(1)
        comm.start_at((hp == 0) & (g == 0), cm)
        lane = lax.broadcasted_iota(jnp.int32, (BLK, BLK), 1)
        qi = lax.broadcasted_iota(jnp.int32, (BLK, BLK), 0)
        ki = lax.broadcasted_iota(jnp.int32, (BLK, BLK), 1)
        d_far = BLK + qi - ki
        d_near = qi - ki
        for pi, (_, dil) in enumerate(ATT_PATTERNS):
            ld = gsz // dil
            nbg = ld // BLK
            reg = ld + BLK
            for dst, c_ref, n_ref in ((qd, qc_ref, qn_ref), (dod, doc_ref, don_ref), (lsd, lsc_ref, lsn_ref), (dld, dlc_ref, dln_ref)):
                _deinterleave(dst, c_ref, dil, ld, reg, 0)
                _deinterleave_edge(dst, n_ref, dil, reg, ld, 0)
            for dst, p_ref, c_ref in ((kd, kp_ref, kc_ref), (vd, vp_ref, vc_ref)):
                _deinterleave(dst, c_ref, dil, ld, reg, BLK)
                _deinterleave_edge(dst, p_ref, dil, reg, 0, gsz - BLK * dil)
            b_far = [_slope_dist(hp, hh, d_far, dil) for hh in (0, 1)]
            b_near = [_slope_dist(hp, hh, d_near, dil) for hh in (0, 1)]

            def tile(t, carry, ld=ld, nbg=nbg, reg=reg, b_far=b_far, b_near=b_near):
                r, b = t // nbg, t % nbg
                oo = pl.multiple_of(r * ld + b * BLK, BLK)
                ro = pl.multiple_of(r * reg + b * BLK, BLK)
                qn, qx = qd[pl.ds(ro, BLK), :], qd[pl.ds(ro + BLK, BLK), :]
                don, dox = dod[pl.ds(ro, BLK), :], dod[pl.ds(ro + BLK, BLK), :]
                lsn, lsx = lsd[pl.ds(ro, BLK), :], lsd[pl.ds(ro + BLK, BLK), :]
                dln, dlx = dld[pl.ds(ro, BLK), :], dld[pl.ds(ro + BLK, BLK), :]
                kp, kc = kd[pl.ds(ro, BLK), :].astype(MXU), kd[pl.ds(ro + BLK, BLK), :].astype(MXU)
                vp, vc = vd[pl.ds(ro, BLK), :].astype(MXU), vd[pl.ds(ro + BLK, BLK), :].astype(MXU)
                ok_a = (d_far <= BLK) & ((g > 0) | (b > 0))
                ok_b = d_near >= 0
                ok_c = (d_far <= BLK) & ((g < ng - 1) | (b < nbg - 1))

                def grads(qm, dom, k, v, ls, dl, bias, valid, hh):
                    c0 = hh * HEAD_DIM
                    sc = _nt(qm, k) * scale - bias
                    p = jnp.exp(jnp.where(valid, sc - ls[:, c0:c0 + 1], NEG))
                    ds = p * (_nt(dom, v) - dl[:, c0:c0 + 1])
                    return p.astype(MXU), ds.astype(MXU)

                dq = jnp.zeros((BLK, BLK), F32)
                dk = jnp.zeros((BLK, BLK), F32)
                dv = jnp.zeros((BLK, BLK), F32)
                for hh in (0, 1):
                    hmask = (lane < HEAD_DIM) if hh == 0 else (lane >= HEAD_DIM)
                    qnm = jnp.where(hmask, qn, 0.0).astype(MXU)
                    qxm = jnp.where(hmask, qx, 0.0).astype(MXU)
                    donm = jnp.where(hmask, don, 0.0).astype(MXU)
                    doxm = jnp.where(hmask, dox, 0.0).astype(MXU)
                    _, ds_a = grads(qnm, donm, kp, vp, lsn, dln, b_far[hh], ok_a, hh)
                    p_b, ds_b = grads(qnm, donm, kc, vc, lsn, dln, b_near[hh], ok_b, hh)
                    p_c, ds_c = grads(qxm, doxm, kc, vc, lsx, dlx, b_far[hh], ok_c, hh)
                    dq = jnp.where(hmask, _nn(ds_a, kp) + _nn(ds_b, kc), dq)
                    dk = dk + _tn(ds_b, qnm) + _tn(ds_c, qxm)
                    dv = dv + _tn(p_b, donm) + _tn(p_c, doxm)
                dqd[pl.ds(oo, BLK), :] = dq * scale
                dkd[pl.ds(oo, BLK), :] = dk * scale
                dvd[pl.ds(oo, BLK), :] = dv
                return carry

            lax.fori_loop(0, dil * nbg, tile, 0, unroll=4)
            for out, src in ((dq_ref, dqd), (dk_ref, dkd), (dv_ref, dvd)):
                for r in range(dil):
                    rows = pl.ds(r, ld, stride=dil) if dil > 1 else pl.ds(0, ld)
                    if pi == 0:
                        out[rows, :] = src[r * ld:(r + 1) * ld, :]
                    else:
                        out[rows, :] = out[rows, :] + src[r * ld:(r + 1) * ld, :]
        comm.wait_at((hp == 3) & (g == ng - 1), cm)

    def pspec(base, shift):
        return pl.BlockSpec((gsz, BLK), lambda hp, g: (jnp.clip(g + shift, 0, ng - 1), base // BLK + hp))

    def wspec(shift):
        return pl.BlockSpec((gsz, BLK), lambda hp, g: (jnp.clip(g + shift, 0, ng - 1), hp))

    in_specs = [pspec(C_Q, 0), pspec(C_Q, 1), pspec(C_K, -1), pspec(C_K, 0), pspec(C_V, -1), pspec(C_V, 0),
                wspec(0), wspec(1), wspec(0), wspec(1), wspec(0), wspec(1)] + [ANY] * comm.n
    res = pl.pallas_call(
        body, name=name, grid=(4, ng), in_specs=in_specs,
        out_specs=[wspec(0)] * 3 + [ANY] * comm.n,
        out_shape=[jax.ShapeDtypeStruct((s, ATT_W), F32)] * 3 + comm.out_shape(),
        scratch_shapes=[pltpu.VMEM((2 * gsz, BLK), F32)] * 6 + [pltpu.VMEM((gsz, BLK), F32)] * 3 + comm.scratch(),
        compiler_params=_cp("arbitrary", "arbitrary"),
    )(proj, proj, proj, proj, proj, proj, datt, datt, lse, lse, delta, delta, *comm.args())
    return res[0], res[1], res[2], list(res[3:])


def _att_bwd_rev(proj, datt, lse, delta, name, comm=None):
    s, npc = proj.shape
    gsz = ATT_G
    ng = s // gsz
    npat = len(ATT_PATTERNS)
    scale = HEAD_DIM ** -0.5
    comm = comm or _Comm()

    def body(*refs):
        (q_ref, kp_ref, kc_ref, vp_ref, vc_ref, do_ref, ls_ref, dl_ref, dq_out, dk_out, dv_out,
         qd, dod, lsd, dld, kd, vd, dqd, dkc, dvc, dkp, dvp, kcar, vcar, dq_ref, dk_ref, dv_ref), cm = comm.split(refs, 8, 3, 16)
        hp, gi = pl.program_id(0), pl.program_id(1)
        g = ng - 1 - gi
        comm.start_at((hp == 0) & (gi == 0), cm)

        @pl.when(gi == 0)
        def _():
            kcar[...] = jnp.zeros_like(kcar)
            vcar[...] = jnp.zeros_like(vcar)

        lane = lax.broadcasted_iota(jnp.int32, (BLK, BLK), 1)
        qi = lax.broadcasted_iota(jnp.int32, (BLK, 2 * BLK), 0)
        ki = lax.broadcasted_iota(jnp.int32, (BLK, 2 * BLK), 1)
        dist = BLK + qi - ki
        band = (dist >= 0) & (dist <= BLK)
        for pi, (_, dil) in enumerate(ATT_PATTERNS):
            ld = gsz // dil
            nbg = ld // BLK
            reg = ld + BLK
            for dst, src in ((qd, q_ref), (dod, do_ref), (lsd, ls_ref), (dld, dl_ref)):
                _deinterleave(dst, src, dil, ld, ld, 0)
            for dst, p_ref, c_ref in ((kd, kp_ref, kc_ref), (vd, vp_ref, vc_ref)):
                _deinterleave(dst, c_ref, dil, ld, reg, BLK)
                _deinterleave_edge(dst, p_ref, dil, reg, 0, gsz - BLK * dil)
            bias = [_slope_dist(hp, hh, dist, dil) for hh in (0, 1)]

            def tile(t, carry, ld=ld, nbg=nbg, reg=reg, bias=bias):
                r, b = t // nbg, t % nbg
                oo = pl.multiple_of(r * ld + b * BLK, BLK)
                ko = pl.multiple_of(r * reg + b * BLK, BLK)
                q, do = qd[pl.ds(oo, BLK), :], dod[pl.ds(oo, BLK), :]
                ls, dl = lsd[pl.ds(oo, BLK), :], dld[pl.ds(oo, BLK), :]
                kk = kd[pl.ds(ko, 2 * BLK), :].astype(MXU)
                vv = vd[pl.ds(ko, 2 * BLK), :].astype(MXU)
                valid = band & ((g > 0) | (b > 0) | (ki >= BLK))
                dq = jnp.zeros((BLK, BLK), F32)
                dkk = jnp.zeros((2 * BLK, BLK), F32)
                dvv = jnp.zeros((2 * BLK, BLK), F32)
                for hh in (0, 1):
                    c0 = hh * HEAD_DIM
                    hmask = (lane < HEAD_DIM) if hh == 0 else (lane >= HEAD_DIM)
                    qm = jnp.where(hmask, q, 0.0).astype(MXU)
                    dom = jnp.where(hmask, do, 0.0).astype(MXU)
                    sc = _nt(qm, kk) * scale - bias[hh]
                    p = jnp.exp(jnp.where(valid, sc - ls[:, c0:c0 + 1], NEG))
                    ds = (p * (_nt(dom, vv) - dl[:, c0:c0 + 1])).astype(MXU)
                    dq = jnp.where(hmask, _nn(ds, kk), dq)
                    dkk = dkk + _tn(ds, qm)
                    dvv = dvv + _tn(p.astype(MXU), dom)
                dqd[pl.ds(oo, BLK), :] = dq * scale
                dkp[pl.ds(oo, BLK), :] = dkk[:BLK] * scale
                dkc[pl.ds(oo, BLK), :] = dkk[BLK:] * scale
                dvp[pl.ds(oo, BLK), :] = dvv[:BLK]
                dvc[pl.ds(oo, BLK), :] = dvv[BLK:]
                return carry

            lax.fori_loop(0, dil * nbg, tile, 0, unroll=8)
            for r in range(dil):
                rows = pl.ds(r, ld, stride=dil) if dil > 1 else pl.ds(0, ld)
                lo, hi = r * ld, (r + 1) * ld
                edge = slice(pi * gsz + r * BLK, pi * gsz + (r + 1) * BLK)
                for out, cur, prv, car in ((dk_ref, dkc, dkp, kcar), (dv_ref, dvc, dvp, vcar)):
                    later = car[edge, :] if nbg == 1 else jnp.concatenate([prv[lo + BLK:hi, :], car[edge, :]], axis=0)
                    total = cur[lo:hi, :] + later
                    car[edge, :] = prv[lo:lo + BLK, :]
                    out[rows, :] = total if pi == 0 else out[rows, :] + total
                dq_ref[rows, :] = dqd[lo:hi, :] if pi == 0 else dq_ref[rows, :] + dqd[lo:hi, :]
        for out, acc in ((dq_out, dq_ref), (dk_out, dk_ref), (dv_out, dv_ref)):
            out[...] = acc[...].astype(out.dtype)
        comm.wait_at((hp == 3) & (gi == ng - 1), cm)

    def pspec(base, shift):
        return pl.BlockSpec((gsz, BLK), lambda hp, gi: (jnp.maximum(ng - 1 - gi + shift, 0), base // BLK + hp))

    wspec = pl.BlockSpec((gsz, BLK), lambda hp, gi: (ng - 1 - gi, hp))
    in_specs = [pspec(C_Q, 0), pspec(C_K, -1), pspec(C_K, 0), pspec(C_V, -1), pspec(C_V, 0), wspec, wspec, wspec] + [ANY] * comm.n
    res = pl.pallas_call(
        body, name=name, grid=(4, ng), in_specs=in_specs,
        out_specs=[wspec] * 3 + [ANY] * comm.n,
        out_shape=[jax.ShapeDtypeStruct((s, ATT_W), MXU)] * 3 + comm.out_shape(),
        scratch_shapes=[pltpu.VMEM((gsz, BLK), F32)] * 4 + [pltpu.VMEM((2 * gsz, BLK), F32)] * 2
        + [pltpu.VMEM((gsz, BLK), F32)] * 5 + [pltpu.VMEM((npat * gsz, BLK), F32)] * 2 + [pltpu.VMEM((gsz, BLK), F32)] * 3
        + comm.scratch(),
        compiler_params=_cp("arbitrary", "arbitrary"),
    )(proj, proj, proj, proj, proj, datt, lse, delta, *comm.args())
    return res[0], res[1], res[2], list(res[3:])


def _shift_down(cur, halo, sft):
    if sft == 0:
        return cur
    t = cur.shape[0]
    rolled = pltpu.roll(cur, sft, 0)
    hr = pltpu.roll(halo, sft, 0)
    row = lax.broadcasted_iota(jnp.int32, cur.shape, 0)
    return jnp.where(row < sft, jnp.tile(hr, (t // 8, 1)), rolled)


def _shift_up(cur, halo, sft):
    if sft == 0:
        return cur
    t = cur.shape[0]
    rolled = pltpu.roll(cur, t - sft, 0)
    hr = pltpu.roll(halo, 8 - sft, 0)
    row = lax.broadcasted_iota(jnp.int32, cur.shape, 0)
    return jnp.where(row >= t - sft, jnp.tile(hr, (t // 8, 1)), rolled)


def _conv(x, xh, w, b):
    y = b + x * w[CONV_K - 1:CONV_K]
    for k in range(CONV_K - 1):
        y = y + _shift_down(x, xh, CONV_K - 1 - k) * w[k:k + 1]
    return y


def _conv_bwd(x, xh, dy, dyh, w):
    dx = dy * w[CONV_K - 1:CONV_K]
    dws = []
    for k in range(CONV_K - 1):
        sft = CONV_K - 1 - k
        dx = dx + _shift_up(dy, dyh, sft) * w[k:k + 1]
        dws.append(jnp.sum(dy * _shift_down(x, xh, sft), axis=0, keepdims=True))
    dws.append(jnp.sum(dy * x, axis=0, keepdims=True))
    c = x.shape[1]
    dw = jnp.concatenate(dws + [jnp.zeros((8 - CONV_K, c), F32)], axis=0)
    return dx, dw, jnp.sum(dy, axis=0, keepdims=True)


def _pad8(w):
    return jnp.concatenate([w, jnp.zeros((8 - w.shape[0], w.shape[1]), w.dtype)], axis=0)


def _ssd_pre(proj, conv_w, conv_b, dt_bias128, name):
    def fn(rv, hv, cv):
        xbc, dtr = rv
        return [_silu(_conv(xbc, hv[0], cv[0], cv[1])), _softplus(dtr + cv[2])], []
    return _rows(fn, [(proj, 1024, C_XBC // 1024), (proj, BLK, C_DT // BLK)],
                 [_pad8(conv_w), conv_b.reshape(1, -1), dt_bias128],
                 [(1024, F32), (BLK, F32)], tile=256, name=name, halos=[(0, "prev")])


def _ssd_pre_bwd(proj, dxc, ddt, conv_w, conv_b, dt_bias128, name):
    def fn(rv, hv, cv):
        xbc, dtr, dxcb, ddtb = rv
        xh, dxch_raw, xnext = hv
        w, b, bias = cv
        pre = _conv(xbc, xh, w, b)
        sg = _sigmoid(pre)
        dpre = dxcb * (sg * (1.0 + pre * (1.0 - sg)))
        t = xbc.shape[0]
        tail = jnp.concatenate([xbc[t - 8:], xnext], axis=0)
        pre_n = _conv(tail[8:], tail[:8], w, b)
        sgn = _sigmoid(pre_n)
        dpre_h = dxch_raw * (sgn * (1.0 + pre_n * (1.0 - sgn)))
        dx, dw, db = _conv_bwd(xbc, xh, dpre, dpre_h, w)
        ddr = ddtb * _sigmoid(dtr + bias)
        return [dx, ddr], [dw, jnp.concatenate([db, jnp.zeros((7, db.shape[1]), F32)], axis=0), _colsum8(ddr)]
    return _rows(fn, [(proj, 1024, C_XBC // 1024), (proj, BLK, C_DT // BLK), dxc, ddt],
                 [_pad8(conv_w), conv_b.reshape(1, -1), dt_bias128],
                 [(1024, MXU), (BLK, MXU)], [(8, 1024), (8, 1024), (8, BLK)], tile=256, name=name,
                 halos=[(0, "prev"), (2, "next"), (0, "next")])


SSD_CPB = 1


def _head_cols(v, h0):
    lane = lax.broadcasted_iota(jnp.int32, (v.shape[0], BLK), 1)
    return jnp.where(lane < HEAD_DIM, v[:, h0:h0 + 1], v[:, h0 + 1:h0 + 2])


def _ssd_scan(xc, dt, par, name):
    s = xc.shape[0]
    nc = s // BLK

    def body(x_ref, dt_ref, par_ref, y_ref, st_ref, h_ref):
        c = pl.program_id(0)

        @pl.when(c == 0)
        def _():
            h_ref[...] = jnp.zeros_like(h_ref)

        st_ref[0] = h_ref[...]
        dt = dt_ref[...]
        a_row = -jnp.exp(par_ref[0:1, :])
        d_row = par_ref[1:2, :]
        ri = lax.broadcasted_iota(jnp.int32, (BLK, BLK), 0)
        ci = lax.broadcasted_iota(jnp.int32, (BLK, BLK), 1)
        tril = ri >= ci
        cs = _nn(tril.astype(F32), dt * a_row, HI)
        cst, dtt = cs.T, dt.T
        last = cs[BLK - 1:BLK, :]
        wcol = jnp.exp(last - cs) * dt
        ecs = jnp.exp(cs)
        elast = jnp.exp(last)
        for g in (0, 1):
            bg = x_ref[:, 512 + g * BLK:512 + (g + 1) * BLK].astype(MXU)
            cg = x_ref[:, 768 + g * BLK:768 + (g + 1) * BLK].astype(MXU)
            gm = _nt(cg, bg)
            for pp in (0, 1):
                pr = 2 * g + pp
                h0 = 2 * pr
                x2 = x_ref[:, pr * BLK:(pr + 1) * BLK]
                hprev = h_ref[pr * BLK:(pr + 1) * BLK, :]
                yp = jnp.zeros((BLK, BLK), F32)
                for hh in (0, 1):
                    h = h0 + hh
                    hmask = (ci < HEAD_DIM) if hh == 0 else (ci >= HEAD_DIM)
                    lm = jnp.exp(jnp.where(tril, cs[:, h:h + 1] - cst[h:h + 1, :], NEG))
                    mm = gm * lm * dtt[h:h + 1, :]
                    yp = yp + _nn(mm.astype(MXU), jnp.where(hmask, x2, 0.0).astype(MXU))
                y0 = _nt(cg, hprev.astype(MXU))
                y_ref[:, pr * BLK:(pr + 1) * BLK] = yp + _head_cols(ecs, h0) * y0 + _head_cols(d_row, h0) * x2
                dec = jnp.where(ri < HEAD_DIM, elast[:, h0:h0 + 1], elast[:, h0 + 1:h0 + 2])
                xw = (x2 * _head_cols(wcol, h0)).astype(MXU)
                h_ref[pr * BLK:(pr + 1) * BLK, :] = dec * hprev + _tn(xw, bg)

    return pl.pallas_call(
        body, name=name, grid=(nc,),
        in_specs=[pl.BlockSpec((BLK, 1024), lambda c: (c, 0)), pl.BlockSpec((BLK, BLK), lambda c: (c, 0)),
                  pl.BlockSpec((8, BLK), lambda c: (0, 0))],
        out_specs=[pl.BlockSpec((BLK, SSD_W), lambda c: (c, 0)), pl.BlockSpec((1, SSD_W, SSD_STATE), lambda c: (c, 0, 0))],
        out_shape=[jax.ShapeDtypeStruct((s, SSD_W), F32), jax.ShapeDtypeStruct((nc, SSD_W, SSD_STATE), F32)],
        scratch_shapes=[pltpu.VMEM((SSD_W, SSD_STATE), F32)],
        compiler_params=_cp("arbitrary"),
    )(xc, dt, par)


def _ssd_scan_bwd(xc, dt, par, st, dy, name, comm=None):
    s = xc.shape[0]
    cpb = SSD_CPB
    nb = s // (cpb * BLK)
    comm = comm or _Comm()

    def chunk(x_ref, dt_ref, par_ref, st_ref, dy_ref, dx_ref, ddt_ref, dal_ref, dd_ref, dh_ref):
        dt = dt_ref[...]
        a_row = -jnp.exp(par_ref[0:1, :])
        d_row = par_ref[1:2, :]
        ri = lax.broadcasted_iota(jnp.int32, (BLK, BLK), 0)
        ci = lax.broadcasted_iota(jnp.int32, (BLK, BLK), 1)
        tril = ri >= ci
        cs = _nn(tril.astype(F32), dt * a_row, HI)
        cst, dtt = cs.T, dt.T
        last = cs[BLK - 1:BLK, :]
        tolast = jnp.exp(last - cs)
        wcol = tolast * dt
        ecs = jnp.exp(cs)
        elast = jnp.exp(last)
        dcs_col = jnp.zeros((BLK, BLK), F32)
        ddt_col = jnp.zeros((BLK, BLK), F32)
        dcs_row = jnp.zeros((BLK, BLK), F32)
        ddt_row = jnp.zeros((BLK, BLK), F32)
        dlast = jnp.zeros((1, BLK), F32)
        ddsk = jnp.zeros((1, BLK), F32)
        for g in (0, 1):
            bg32 = x_ref[:, 512 + g * BLK:512 + (g + 1) * BLK]
            cg32 = x_ref[:, 768 + g * BLK:768 + (g + 1) * BLK]
            bg, cg = bg32.astype(MXU), cg32.astype(MXU)
            gm = _nt(cg, bg)
            dgm = jnp.zeros((BLK, BLK), F32)
            dbg = jnp.zeros((BLK, BLK), F32)
            dcg = jnp.zeros((BLK, BLK), F32)
            for pp in (0, 1):
                pr = 2 * g + pp
                h0 = 2 * pr
                x2 = x_ref[:, pr * BLK:(pr + 1) * BLK]
                dy2 = dy_ref[:, pr * BLK:(pr + 1) * BLK]
                hprev = st_ref[0, pr * BLK:(pr + 1) * BLK, :]
                dhn = dh_ref[pr * BLK:(pr + 1) * BLK, :]
                x2m, dhnm = x2.astype(MXU), dhn.astype(MXU)
                zb = _nt(bg, dhnm)
                y0 = _nt(cg, hprev.astype(MXU))
                esel = _head_cols(ecs, h0)
                wsel = _head_cols(wcol, h0)
                dx2 = _head_cols(d_row, h0) * dy2 + wsel * zb
                r_off = dy2 * y0
                r_w = x2 * zb
                r_d = dy2 * x2
                r_h = dhn * hprev
                for hh in (0, 1):
                    h = h0 + hh
                    hmask = (ci < HEAD_DIM) if hh == 0 else (ci >= HEAD_DIM)
                    onl = (ci == h).astype(F32)
                    ons = (ri == h).astype(F32)
                    dym = jnp.where(hmask, dy2, 0.0).astype(MXU)
                    dt_r = dtt[h:h + 1, :]
                    lm = jnp.exp(jnp.where(tril, cs[:, h:h + 1] - cst[h:h + 1, :], NEG))
                    mm = gm * lm * dt_r
                    dx2 = dx2 + _tn(mm.astype(MXU), dym)
                    dm = _nt(dym, x2m)
                    t1 = dm * lm
                    dgm = dgm + t1 * dt_r
                    tt = t1 * gm
                    ddt_row = ddt_row + ons * jnp.sum(tt, axis=0, keepdims=True)
                    t = tt * dt_r
                    dcs_col = dcs_col + onl * jnp.sum(t, axis=1, keepdims=True)
                    dcs_row = dcs_row - ons * jnp.sum(t, axis=0, keepdims=True)
                    de = jnp.sum(jnp.where(hmask, r_off, 0.0), axis=1, keepdims=True)
                    dcs_col = dcs_col + onl * (ecs[:, h:h + 1] * de)
                    hrow = (ri < HEAD_DIM) if hh == 0 else (ri >= HEAD_DIM)
                    dl_h = elast[:, h:h + 1] * jnp.sum(jnp.where(hrow, r_h, 0.0), keepdims=True)
                    dw = jnp.sum(jnp.where(hmask, r_w, 0.0), axis=1, keepdims=True)
                    ddt_col = ddt_col + onl * (dw * tolast[:, h:h + 1])
                    v = dw * wcol[:, h:h + 1]
                    dcs_col = dcs_col - onl * v
                    dl_h = dl_h + jnp.sum(v, keepdims=True)
                    dlast = dlast + onl[0:1, :] * dl_h
                    ddsk = ddsk + onl[0:1, :] * jnp.sum(jnp.where(hmask, r_d, 0.0), keepdims=True)
                dx_ref[:, pr * BLK:(pr + 1) * BLK] = dx2
                edy = (esel * dy2).astype(MXU)
                dcg = dcg + _nn(edy, hprev.astype(MXU))
                dec = jnp.where(ri < HEAD_DIM, elast[:, h0:h0 + 1], elast[:, h0 + 1:h0 + 2])
                dh_ref[pr * BLK:(pr + 1) * BLK, :] = dec * dhn + _tn(edy, cg)
                dbg = dbg + _nn((x2 * wsel).astype(MXU), dhnm)
            dgmm = dgm.astype(MXU)
            dx_ref[:, 512 + g * BLK:512 + (g + 1) * BLK] = dbg + _tn(dgmm, cg)
            dx_ref[:, 768 + g * BLK:768 + (g + 1) * BLK] = dcg + _nn(dgmm, bg)
        dcs = dcs_col + dcs_row.T + jnp.where(ri == BLK - 1, dlast, 0.0)
        dda = _nn((ri <= ci).astype(F32), dcs, HI)
        ddt_ref[...] = ddt_col + ddt_row.T + a_row * dda
        da = jnp.sum(dt * dda, axis=0, keepdims=True)
        dal_ref[0:1, :] += da * a_row
        dd_ref[0:1, :] += ddsk

    def body(*refs):
        (x_ref, dt_ref, par_ref, st_ref, dy_ref, dx_ref, ddt_ref, dal_ref, dd_ref, dh_ref), cm = comm.split(refs, 5, 4, 1)
        c = pl.program_id(0)
        comm.start_at(c == 0, cm)

        @pl.when(c == 0)
        def _():
            dh_ref[...] = jnp.zeros_like(dh_ref)
            dal_ref[...] = jnp.zeros_like(dal_ref)
            dd_ref[...] = jnp.zeros_like(dd_ref)

        for cc in reversed(range(cpb)):
            rows = pl.ds(cc * BLK, BLK)
            chunk(x_ref.at[rows], dt_ref.at[rows], par_ref, st_ref.at[pl.ds(cc, 1)], dy_ref.at[rows], dx_ref.at[rows],
                  ddt_ref.at[rows], dal_ref, dd_ref, dh_ref)
        comm.wait_at(c == nb - 1, cm)

    rev = lambda c: (nb - 1 - c, 0)
    tb = cpb * BLK
    res = pl.pallas_call(
        body, name=name, grid=(nb,),
        in_specs=[pl.BlockSpec((tb, 1024), rev), pl.BlockSpec((tb, BLK), rev), pl.BlockSpec((8, BLK), lambda c: (0, 0)),
                  pl.BlockSpec((cpb, SSD_W, SSD_STATE), lambda c: (nb - 1 - c, 0, 0)), pl.BlockSpec((tb, SSD_W), rev)]
        + [ANY] * comm.n,
        out_specs=[pl.BlockSpec((tb, 1024), rev), pl.BlockSpec((tb, BLK), rev),
                   pl.BlockSpec((8, BLK), lambda c: (0, 0)), pl.BlockSpec((8, BLK), lambda c: (0, 0))] + [ANY] * comm.n,
        out_shape=[jax.ShapeDtypeStruct((s, 1024), F32), jax.ShapeDtypeStruct((s, BLK), F32),
                   jax.ShapeDtypeStruct((8, BLK), F32), jax.ShapeDtypeStruct((8, BLK), F32)] + comm.out_shape(),
        scratch_shapes=[pltpu.VMEM((SSD_W, SSD_STATE), F32)] + comm.scratch(),
        compiler_params=_cp("arbitrary"),
    )(xc, dt, par, st, dy, *comm.args())
    return res[0], res[1], res[2], res[3], list(res[4:])


def _ssd_gate(y, z, w):
    t = y * _silu(z)
    outs = []
    for g in (0, 1):
        tg = t[:, g * 256:(g + 1) * 256]
        outs.append(tg * lax.rsqrt(jnp.mean(tg * tg, axis=-1, keepdims=True) + SSD_NORM_EPS))
    return jnp.concatenate(outs, axis=1) * w


def _ssd_post(y, proj, norm_w, name):
    def fn(rv, hv, cv):
        return [_ssd_gate(rv[0], rv[1], cv[0])], []
    return _rows(fn, [y, (proj, SSD_W, C_Z // SSD_W)], [norm_w.reshape(1, -1)], [(SSD_W, MXU)], tile=512, name=name)[0]


def _ssd_post_bwd(y, proj, norm_w, dout, name):
    def fn(rv, hv, cv):
        yb, zb, db = rv
        _, vjp = jax.vjp(lambda a, b: _ssd_gate(a, b, cv[0]), yb, zb)
        dy, dz = vjp(db)
        t = yb * _silu(zb)
        nrm = []
        for g in (0, 1):
            tg = t[:, g * 256:(g + 1) * 256]
            nrm.append(tg * lax.rsqrt(jnp.mean(tg * tg, axis=-1, keepdims=True) + SSD_NORM_EPS))
        return [dy, dz], [_colsum8(db * jnp.concatenate(nrm, axis=1))]
    return _rows(fn, [y, (proj, SSD_W, C_Z // SSD_W), dout], [norm_w.reshape(1, -1)],
                 [(SSD_W, F32), (SSD_W, MXU)], [(8, SSD_W)], tile=512, name=name)


LRU_T = 256


def _lru_conv(proj, conv_w, conv_b, name):
    def fn(rv, hv, cv):
        return [_conv(rv[0], hv[0], cv[0], cv[1])], []
    return _rows(fn, [(proj, LRU_W, C_XL // LRU_W)], [_pad8(conv_w), conv_b.reshape(1, -1)], [(LRU_W, F32)],
                 tile=512, name=name, halos=[(0, "prev")])[0]


def _lru_conv_bwd(proj, dxc, conv_w, name):
    def fn(rv, hv, cv):
        dx, dw, db = _conv_bwd(rv[0], hv[0], rv[1], hv[1], cv[0])
        return [dx], [dw, jnp.concatenate([db, jnp.zeros((7, db.shape[1]), F32)], axis=0)]
    return _rows(fn, [(proj, LRU_W, C_XL // LRU_W), dxc], [_pad8(conv_w)], [(LRU_W, MXU)], [(8, LRU_W), (8, LRU_W)],
                 tile=512, name=name, halos=[(0, "prev"), (1, "next")])


def _lru_au(pre_a, pre_x, xc, ba, bx, lam):
    r = _sigmoid(pre_a + ba)
    i = _sigmoid(pre_x + bx)
    log_a = -LRU_C * r * _softplus(-lam)
    a = jnp.exp(log_a)
    u = jnp.sqrt(1.0 - jnp.exp(2.0 * log_a)) * (i * xc)
    return a, u


def _lru_scan(pre, xc, proj, par, name):
    s = xc.shape[0]
    t = LRU_T

    def body(pre_ref, xc_ref, g_ref, par_ref, out_ref, h_ref, carry):
        c = pl.program_id(0)

        @pl.when(c == 0)
        def _():
            carry[...] = jnp.zeros_like(carry)

        a, u = _lru_au(pre_ref[:, :LRU_W], pre_ref[:, LRU_W:], xc_ref[...], par_ref[0:1, :], par_ref[1:2, :], par_ref[2:3, :])
        row = lax.broadcasted_iota(jnp.int32, (t, LRU_W), 0)
        sft = 1
        while sft < t:
            keep = row >= sft
            a_s = jnp.where(keep, pltpu.roll(a, sft, 0), 1.0)
            u_s = jnp.where(keep, pltpu.roll(u, sft, 0), 0.0)
            u = a * u_s + u
            a = a * a_s
            sft *= 2
        h = a * carry[0:1, :] + u
        h_ref[...] = h
        out_ref[...] = (h * _gelu(g_ref[...])).astype(out_ref.dtype)
        carry[0:1, :] = h[t - 1:t, :]

    return pl.pallas_call(
        body, name=name, grid=(s // t,),
        in_specs=[pl.BlockSpec((t, 2 * LRU_W), lambda c: (c, 0)), pl.BlockSpec((t, LRU_W), lambda c: (c, 0)),
                  pl.BlockSpec((t, LRU_W), lambda c: (c, C_G // LRU_W)), pl.BlockSpec((8, LRU_W), lambda c: (0, 0))],
        out_specs=[pl.BlockSpec((t, LRU_W), lambda c: (c, 0))] * 2,
        out_shape=[jax.ShapeDtypeStruct((s, LRU_W), MXU), jax.ShapeDtypeStruct((s, LRU_W), F32)],
        scratch_shapes=[pltpu.VMEM((8, LRU_W), F32)],
        compiler_params=_cp("arbitrary"),
    )(pre, xc, proj, par)


def _lru_scan_bwd(pre, xc, proj, par, h, dout, name):
    s = xc.shape[0]
    t = LRU_T
    n = s // t
    t8 = t // 8

    def body(pre_ref, xc_ref, g_ref, par_ref, h_ref, hh_ref, do_ref, dpre_ref, dxc_ref, dg_ref, dpar_ref, carry):
        c = pl.program_id(0)

        @pl.when(c == 0)
        def _():
            carry[...] = jnp.zeros_like(carry)
            dpar_ref[...] = jnp.zeros_like(dpar_ref)

        pa, px, xcb = pre_ref[:, :LRU_W], pre_ref[:, LRU_W:], xc_ref[...]
        ba, bx, lam = par_ref[0:1, :], par_ref[1:2, :], par_ref[2:3, :]
        (a, u), vjp = jax.vjp(_lru_au, pa, px, xcb, ba, bx, lam)
        g = g_ref[...]
        hcur = h_ref[...]
        do = do_ref[...]
        _, gvjp = jax.vjp(_gelu, g)
        dg_ref[...] = gvjp(do * hcur)[0].astype(dg_ref.dtype)
        row = lax.broadcasted_iota(jnp.int32, (t, LRU_W), 0)
        v = do * _gelu(g) + jnp.where(row == t - 1, carry[0:1, :], 0.0)
        b = jnp.where(row == t - 1, 0.0, pltpu.roll(a, t - 1, 0))
        sft = 1
        while sft < t:
            keep = row < t - sft
            b_s = jnp.where(keep, pltpu.roll(b, t - sft, 0), 1.0)
            v_s = jnp.where(keep, pltpu.roll(v, t - sft, 0), 0.0)
            v = b * v_s + v
            b = b * b_s
            sft *= 2
        dh = v
        carry[0:1, :] = a[0:1, :] * dh[0:1, :]
        hhalo = jnp.where(c == n - 1, 0.0, hh_ref[...])
        hprev = _shift_down(hcur, hhalo, 1)
        dpa, dpx, dxc, dba, dbx, dlam = vjp((dh * hprev, dh))
        dpre_ref[:, :LRU_W] = dpa
        dpre_ref[:, LRU_W:] = dpx
        dxc_ref[...] = dxc
        dpar_ref[0:1, :] += dba
        dpar_ref[1:2, :] += dbx
        dpar_ref[2:3, :] += dlam

    rev = lambda c: (n - 1 - c, 0)
    return pl.pallas_call(
        body, name=name, grid=(n,),
        in_specs=[pl.BlockSpec((t, 2 * LRU_W), rev), pl.BlockSpec((t, LRU_W), rev),
                  pl.BlockSpec((t, LRU_W), lambda c: (n - 1 - c, C_G // LRU_W)), pl.BlockSpec((8, LRU_W), lambda c: (0, 0)),
                  pl.BlockSpec((t, LRU_W), rev),
                  pl.BlockSpec((8, LRU_W), lambda c: (jnp.maximum((n - 1 - c) * t8 - 1, 0), 0)),
                  pl.BlockSpec((t, LRU_W), lambda c: (n - 1 - c, dout.shape[1] // LRU_W - 1))],
        out_specs=[pl.BlockSpec((t, 2 * LRU_W), rev), pl.BlockSpec((t, LRU_W), rev), pl.BlockSpec((t, LRU_W), rev),
                   pl.BlockSpec((8, LRU_W), lambda c: (0, 0))],
        out_shape=[jax.ShapeDtypeStruct((s, 2 * LRU_W), F32), jax.ShapeDtypeStruct((s, LRU_W), F32),
                   jax.ShapeDtypeStruct((s, LRU_W), MXU), jax.ShapeDtypeStruct((8, LRU_W), F32)],
        scratch_shapes=[pltpu.VMEM((8, LRU_W), F32)],
        compiler_params=_cp("arbitrary"),
    )(pre, xc, proj, par, h, h, dout)


def _swiglu_act(gu, name):
    def fn(rv, hv, cv):
        return [_silu(rv[0].astype(F32)) * rv[1].astype(F32)], []
    return _rows(fn, [(gu, D_FF, 0), (gu, D_FF, 1)], [], [(D_FF, MXU)], tile=256, name=name)[0]


def _swiglu_bwd(gu, da, name):
    def fn(rv, hv, cv):
        gt, up, dab = [t.astype(F32) for t in rv]
        sg = _sigmoid(gt)
        dgate = dab * up * (sg * (1.0 + gt * (1.0 - sg)))
        dup = dab * (gt * sg)
        return [jnp.concatenate([dgate, dup], axis=1)], []
    return _rows(fn, [(gu, D_FF, 0), (gu, D_FF, 1), da], [], [(2 * D_FF, MXU)], tile=256, name=name)[0]


def _loss_head(x, g, target, name):
    d = x.shape[1]

    def fn(rv, hv, cv):
        xb, tb = rv
        y, vjp = jax.vjp(_rms, xb, cv[0])
        err = y - tb
        dy = err * (1.0 / d)
        dx, _ = vjp(dy)
        rstd = lax.rsqrt(jnp.mean(xb * xb, axis=-1, keepdims=True) + NORM_EPS)
        e2 = err * err * (0.5 / d)
        e2 = functools.reduce(lambda a, b: a + b, [e2[:, k * BLK:(k + 1) * BLK] for k in range(d // BLK)])
        return [dx], [_colsum8(dy * xb * rstd), _colsum8(e2)]
    return _rows(fn, [x, target], [g.reshape(1, -1)], [(d, F32)], [(8, d), (8, BLK)], tile=512, name=name)


ANY = pl.BlockSpec(memory_space=pl.ANY)


def _coords():
    return lax.axis_index("x"), lax.axis_index("y"), lax.axis_index("c")


class _Comm:
    def __init__(self, gathers=(), scatters=()):
        self.gathers = list(gathers)
        self.scatters = list(scatters)
        self.n = len(self.gathers) + len(self.scatters)

    def args(self):
        return [g[0] for g in self.gathers] + self.scatters

    def out_shape(self):
        out = [jax.ShapeDtypeStruct((4,) + (a.shape if l is None else a.shape[1:]), a.dtype) for a, l, _ in self.gathers]
        return out + [jax.ShapeDtypeStruct((3,) + a.shape[1:], a.dtype) for a in self.scatters]

    def scratch(self):
        if not self.n:
            return []
        return [pltpu.SemaphoreType.DMA((3 * self.n,)), pltpu.SemaphoreType.DMA((3 * self.n,)),
                pltpu.SemaphoreType.DMA((max(len(self.gathers), 1),)),
                pltpu.SemaphoreType.DMA((3 * self.n,)), pltpu.SemaphoreType.DMA((3 * self.n,))]

    def split(self, refs, n_in, n_out, n_scratch):
        refs = list(refs)
        n = self.n
        own = refs[:n_in] + refs[n_in + n:n_in + n + n_out] + refs[n_in + 2 * n + n_out:n_in + 2 * n + n_out + n_scratch]
        cm = (refs[n_in:n_in + n], refs[n_in + n + n_out:n_in + 2 * n + n_out], refs[n_in + 2 * n + n_out + n_scratch:])
        return own, cm

    def _copies(self, cm, arriving):
        ins, outs, (send, recv, local, _, _) = cm
        x, y, c = _coords()
        me = 2 * x + y
        chips = [(1 - x, y), (x, 1 - y), (1 - x, 1 - y)]
        remote, locals_ = [], []
        ng = len(self.gathers)
        for i in range(self.n):
            if i < ng:
                _, l, halved = self.gathers[i]
                slab = ins[i] if l is None else ins[i].at[l]
                if not arriving:
                    locals_.append(pltpu.make_async_copy(slab, outs[i].at[me], local.at[i]))
            for j, (px, py) in enumerate(chips):
                if i < ng:
                    slot = 2 * px + py if arriving else me
                    src, dst = (slab.at[c], outs[i].at[slot, c]) if halved else (slab, outs[i].at[slot])
                else:
                    src, dst = ins[i].at[2 * px + py], outs[i].at[j]
                remote.append(pltpu.make_async_remote_copy(src, dst, send.at[3 * i + j], recv.at[3 * i + j],
                                                           device_id=(px, py, c), device_id_type=MESH))
        return remote, locals_

    def _handovers(self, cm, arriving):
        _, outs, (_, _, _, send, recv) = cm
        x, y, c = _coords()
        chips = [(1 - x, y), (x, 1 - y), (1 - x, 1 - y)]
        cps = []
        for i, (_, _, halved) in enumerate(self.gathers):
            if halved:
                for j, (px, py) in enumerate(chips):
                    src = outs[i].at[2 * px + py, c]
                    dst = outs[i].at[2 * px + py, 1 - c if arriving else c]
                    cps.append(pltpu.make_async_remote_copy(src, dst, send.at[3 * i + j], recv.at[3 * i + j],
                                                            device_id=(x, y, 1 - c), device_id_type=MESH))
        return cps

    def start_at(self, cond, cm):
        def go():
            remote, locals_ = self._copies(cm, False)
            for cp in locals_ + remote:
                cp.start()

        if self.n:
            go() if cond is True else pl.when(cond)(go)

    def wait_at(self, cond, cm):
        def go():
            for cp in self._copies(cm, True)[0]:
                cp.wait_recv()
            handed = self._handovers(cm, False)
            for cp in handed:
                cp.start()
            for cp in self._handovers(cm, True):
                cp.wait_recv()
            remote, locals_ = self._copies(cm, False)
            for cp in handed + remote:
                cp.wait_send()
            for cp in locals_:
                cp.wait()

        if self.n:
            go() if cond is True else pl.when(cond)(go)


def _comm_call(comm, name):
    def body(*refs):
        _, cm = comm.split(refs, 0, 0, 0)
        comm.start_at(True, cm)
        comm.wait_at(True, cm)

    return list(pl.pallas_call(
        body, name=name, in_specs=[ANY] * comm.n, out_specs=[ANY] * comm.n, out_shape=comm.out_shape(),
        scratch_shapes=comm.scratch(), compiler_params=pltpu.CompilerParams(has_side_effects=True),
    )(*comm.args()))


def _swap_sibling(arrs):
    n = len(arrs)

    def body(*refs):
        ins, outs, send, recv = refs[:n], refs[n:2 * n], refs[2 * n], refs[2 * n + 1]
        x, y, c = _coords()
        cps = [pltpu.make_async_remote_copy(ins[i], outs[i], send.at[i], recv.at[i], device_id=(x, y, 1 - c), device_id_type=MESH)
               for i in range(n)]
        for cp in cps:
            cp.start()
        for cp in cps:
            cp.wait_recv()
        for cp in cps:
            cp.wait_send()

    return list(pl.pallas_call(
        body, name="swap_sibling", in_specs=[ANY] * n, out_specs=[ANY] * n,
        out_shape=[jax.ShapeDtypeStruct(a.shape, a.dtype) for a in arrs],
        scratch_shapes=[pltpu.SemaphoreType.DMA((n,)), pltpu.SemaphoreType.DMA((n,))],
        compiler_params=pltpu.CompilerParams(has_side_effects=True),
    )(*arrs))


def _gather_small(gs):
    def body(g_ref, o_ref, send_sems, recv_sems, local_sem):
        x, y, c = _coords()
        me = 4 * x + 2 * y + c
        mine = pltpu.make_async_copy(g_ref, o_ref.at[me], local_sem)
        mine.start()
        sends = []
        for k in range(1, 8):
            px, py, pc = x ^ (k >> 2), y ^ ((k >> 1) & 1), c ^ (k & 1)
            sends.append((pltpu.make_async_remote_copy(g_ref, o_ref.at[me], send_sems.at[k - 1], recv_sems.at[k - 1],
                                                       device_id=(px, py, pc), device_id_type=MESH), 4 * px + 2 * py + pc, k))
        for cp, _, _ in sends:
            cp.start()
        for cp, src, k in sends:
            pltpu.make_async_remote_copy(g_ref, o_ref.at[src], send_sems.at[k - 1], recv_sems.at[k - 1],
                                         device_id=(x, y, c), device_id_type=MESH).wait_recv()
        for cp, _, _ in sends:
            cp.wait_send()
        mine.wait()

    return pl.pallas_call(
        body, name="gather_small", in_specs=[ANY], out_specs=ANY,
        out_shape=jax.ShapeDtypeStruct((8,) + gs.shape, gs.dtype),
        scratch_shapes=[pltpu.SemaphoreType.DMA((7,)), pltpu.SemaphoreType.DMA((7,)), pltpu.SemaphoreType.DMA],
        compiler_params=pltpu.CompilerParams(has_side_effects=True),
    )(gs)


def _sum_slots(own, others, name, tile):
    k, r, c = others.shape

    def body(*refs):
        if own is None:
            o_ref, out_ref = refs
            acc = o_ref[0].astype(F32)
            first = 1
        else:
            own_ref, o_ref, out_ref = refs
            acc = own_ref[...]
            first = 0
        for j in range(first, k):
            acc = acc + o_ref[j].astype(F32)
        out_ref[...] = acc

    row = pl.BlockSpec((tile, c), lambda i: (i, 0))
    specs = ([] if own is None else [row]) + [pl.BlockSpec((k, tile, c), lambda i: (0, i, 0))]
    args = ([] if own is None else [own]) + [others]
    return pl.pallas_call(body, name=name, grid=(r // tile,), in_specs=specs, out_specs=row,
                          out_shape=jax.ShapeDtypeStruct((r, c), F32), compiler_params=_cp("parallel"))(*args)


def _adamw(w, m, v, ga, gb, name, tile, rows_first=False):
    lead = 0 if rows_first else w.ndim - 2
    r, c = w.shape[-2:]

    def body(*refs):
        vals = [ref[0] if lead else ref[...] for ref in refs[:len(refs) - 4]]
        w_, m_, v_, g = vals[0], vals[1], vals[2], vals[3]
        if gb is not None:
            g = g + vals[4]
        nm = ADAM_B1 * m_ + (1.0 - ADAM_B1) * g
        nv = ADAM_B2 * v_ + (1.0 - ADAM_B2) * (g * g)
        d = -ADAM_LR * ((nm / BC1) / (jnp.sqrt(nv / BC2) + ADAM_EPS) + ADAM_WD * w_)
        for ref, val in zip(refs[len(refs) - 4:], (g, d, nm, nv)):
            if lead:
                ref[0] = val
            else:
                ref[...] = val

    if rows_first:
        row = pl.BlockSpec((tile,) + w.shape[1:], lambda i: (i, 0, 0))
        grid = (w.shape[0] // tile,)
    elif lead:
        row = pl.BlockSpec((1, tile, c), lambda l, i: (l, i, 0))
        grid = (w.shape[0], r // tile)
    else:
        row = pl.BlockSpec((tile, c), lambda i: (i, 0))
        grid = (r // tile,)
    args = [w, m, v, ga] + ([] if gb is None else [gb])
    return pl.pallas_call(body, name=name, grid=grid, in_specs=[row] * len(args), out_specs=[row] * 4,
                          out_shape=[jax.ShapeDtypeStruct(w.shape, F32)] * 4,
                          compiler_params=_cp(*(["parallel"] * len(grid))))(*args)


MATS = ("w_in", "w_out", "w_gate", "w_up", "w_down")
CONVS = ("ssd_conv_w", "lru_conv_w")
BIG = MATS + CONVS
TRANSPOSED = ("w_gate", "w_up")
COL_SHARDED = ("ssd_conv_w", "lru_conv_w")
W_IN_SHARD = IN_COLS // 4
W_IN_PAD = 1056
SMALL = ("norm_mix", "ssd_conv_b", "ssd_dt_bias", "ssd_a_log", "ssd_d", "ssd_norm", "lru_conv_b", "lru_wa", "lru_ba",
         "lru_wx", "lru_bx", "lru_lambda", "norm_ffn", "norm_final")
WEIGHTS = ("norm_mix", "w_in", "ssd_conv_w", "ssd_conv_b", "ssd_dt_bias", "ssd_a_log", "ssd_d", "ssd_norm", "lru_conv_w",
           "lru_conv_b", "lru_wa", "lru_ba", "lru_wx", "lru_bx", "lru_lambda", "w_out", "norm_ffn", "w_gate", "w_up",
           "w_down", "norm_final")
ROW_TILE = {"w_in": W_IN_SHARD, "w_out": 128, "w_gate": 352, "w_up": 352, "w_down": 352}
W_IN_ADAM_TILE = 54


def _pack(arrs, width, row_mult, dtype):
    flat = jnp.concatenate([a.reshape(-1).astype(dtype) for a in arrs])
    rows = -(-flat.shape[0] // width)
    rows = -(-rows // row_mult) * row_mult
    flat = jnp.pad(flat, (0, rows * width - flat.shape[0]))
    return flat.reshape(rows, width)


def _unpack(buf, shapes):
    flat = buf.reshape(-1)
    out, off = [], 0
    for shp in shapes:
        n = int(np.prod(shp))
        out.append(flat[off:off + n].reshape(shp))
        off += n
    return out


def _join(name, g4):
    if name in COL_SHARDED:
        return jnp.moveaxis(g4, 0, -2).reshape(g4.shape[1:-1] + (4 * g4.shape[-1],))
    return g4.reshape((4 * g4.shape[1],) + g4.shape[2:])


def _slabs(name, g):
    if name in COL_SHARDED:
        return jnp.moveaxis(g.reshape(g.shape[:-1] + (4, g.shape[-1] // 4)), -2, 0)
    return g.reshape((4, g.shape[0] // 4) + g.shape[1:])


def _w_in_rows(g4):
    def nat(lo, hi):
        out = []
        while lo < hi:
            j = lo // W_IN_SHARD
            stop = min(hi, (j + 1) * W_IN_SHARD)
            out.append((j, lo - j * W_IN_SHARD, stop - lo))
            lo = stop
        return out
    pieces = nat(0, 3072) + nat(3080, IN_COLS) + nat(3072, 3080)

    def body(g_ref, o_ref):
        row = 0
        for j, first, n in pieces:
            o_ref[row:row + n, :] = g_ref[j, first:first + n, :]
            row += n
        o_ref[row:, :] = jnp.zeros((NP - row, o_ref.shape[1]), o_ref.dtype)

    return pl.pallas_call(body, name="w_in_rows", out_shape=jax.ShapeDtypeStruct((NP, g4.shape[-1]), g4.dtype),
                          compiler_params=pltpu.CompilerParams(vmem_limit_bytes=VMEM_LIMIT))(g4)


def _w_in_slabs(gt):
    def kern(n):
        return n if n < 3072 else (C_DT + n - 3072 if n < 3080 else n - 8)
    slabs = []
    for j in range(4):
        lo, hi = j * W_IN_SHARD, (j + 1) * W_IN_SHARD
        cuts = sorted({lo, hi} | {c for c in (3072, 3080) if lo < c < hi})
        slabs.append(jnp.concatenate([gt[kern(a):kern(a) + b - a] for a, b in zip(cuts[:-1], cuts[1:])], axis=0))
    return jnp.stack(slabs, axis=0)


def _block_diag(w):
    eye = jnp.eye(LRU_BLOCKS, dtype=w.dtype)
    return jnp.einsum("ncd,nm->ncmd", w, eye).reshape(LRU_W, LRU_W)


def _block_diag_extract(g):
    g4 = g.reshape(LRU_BLOCKS, 64, LRU_BLOCKS, 64)
    return jnp.stack([g4[n, :, n, :] for n in range(LRU_BLOCKS)], axis=0)


def _lanes128(v):
    return jnp.pad(v, (0, BLK - v.shape[0])).reshape(1, BLK)


def _layer_mixers(x, p, comm=None, h=None):
    if h is None:
        h = _rms_fwd(x, p["norm_mix"], "rms_mix")
    proj = _mm(h, p["w_in_t"], tb=True, tm=1024, tn=1408, tk=1024, name="mm_in")
    att, lse, attb, got = _att_fwd_fused(proj, "att_fwd", comm)
    xconv, dt = _ssd_pre(proj, p["ssd_conv_w"], p["ssd_conv_b"], _lanes128(p["ssd_dt_bias"]), "ssd_pre")
    spar = jnp.concatenate([_lanes128(p["ssd_a_log"]), _lanes128(p["ssd_d"]), jnp.zeros((6, BLK), F32)], axis=0)
    y, states = _ssd_scan(xconv, dt, spar, "ssd_scan")
    ssd = _ssd_post(y, proj, p["ssd_norm"], "ssd_post")
    xc = _lru_conv(proj, p["lru_conv_w"], p["lru_conv_b"], "lru_conv")
    wab = jnp.concatenate([_block_diag(p["lru_wa"]), _block_diag(p["lru_wx"])], axis=1).astype(MXU)
    pre = _mm(xc, wab, tm=1024, tn=1024, tk=512, name="mm_lru")
    lpar = jnp.concatenate([p["lru_ba"].reshape(1, -1), p["lru_bx"].reshape(1, -1), p["lru_lambda"].reshape(1, -1),
                            jnp.zeros((5, LRU_W), F32)], axis=0)
    lru, hs = _lru_scan(pre, xc, proj, lpar, "lru_scan")
    mix = jnp.concatenate([attb, ssd, lru], axis=1)
    saved = dict(x=x, h=h, proj=proj, att=att, lse=lse, xconv=xconv, dt=dt, spar=spar, y=y, states=states, xc=xc, wab=wab,
                 pre=pre, lpar=lpar, hs=hs, mix=mix)
    return mix, saved, got


def _layer_ffn(x, mix, p, saved, comm=None):
    x1 = _mm(mix, p["w_out"], add=x, tm=1024, tn=1024, tk=1536, name="mm_out")
    h2 = _rms_fwd(x1, p["norm_ffn"], "rms_ffn")
    gu = _mm(h2, p["w_gu_t"], tb=True, out_dtype=MXU, tm=1024, tn=1408, tk=1024, name="mm_gu", comm=comm)
    gu, got = gu if comm is not None else (gu, [])
    act = _swiglu_act(gu, "swiglu_act")
    x2 = _mm(act, p["w_down"], add=x1, tm=1024, tn=1024, tk=2816, name="mm_down")
    saved.update(x1=x1, h2=h2, gu=gu, act=act)
    return x2, got


def _layer_bwd(dx2, p, sv, comm_ssd=None, comm_att=None, comm_tail=None):
    g = {}
    da = _mm(dx2, p["w_down"], tb=True, out_dtype=MXU, tm=1024, tn=1408, tk=1024, name="mm_d_act")
    g["w_down"] = _mm(sv["act"], dx2, ta=True, tm=1408, tn=1024, tk=1024, name="mm_g_down")
    dgu = _swiglu_bwd(sv["gu"], da, "swiglu_bwd")
    dh2 = _mm(dgu, p["w_gu_t"], tm=1024, tn=1024, tk=1408, name="mm_d_h2")
    g["w_gu_t"] = _mm(dgu, sv["h2"], ta=True, tm=1408, tn=1024, tk=1024, name="mm_g_gu")
    dx1, gn = _rms_bwd(sv["x1"], p["norm_ffn"], dh2, dx2, "rms_ffn_bwd")
    g["norm_ffn"] = jnp.sum(gn, axis=0)
    dmix = _mm(dx1, p["w_out"], tb=True, tm=1024, tn=1536, tk=1024, name="mm_d_mix")
    g["w_out"] = _mm(sv["mix"], dx1, ta=True, tm=1536, tn=1024, tk=1024, name="mm_g_out")
    proj = sv["proj"]
    dpre, dxc_u, dgl, dlpar = _lru_scan_bwd(sv["pre"], sv["xc"], proj, sv["lpar"], sv["hs"], dmix, "lru_scan_bwd")
    dxc = _mm(dpre, sv["wab"], tb=True, add=dxc_u, tm=1024, tn=512, tk=1024, name="mm_d_xc")
    gwab = _mm(sv["xc"], dpre, ta=True, tm=512, tn=1024, tk=1024, name="mm_g_lru")
    g["lru_wa"], g["lru_wx"] = _block_diag_extract(gwab[:, :LRU_W]), _block_diag_extract(gwab[:, LRU_W:])
    g["lru_ba"], g["lru_bx"], g["lru_lambda"] = dlpar[0], dlpar[1], dlpar[2]
    dxl, gcw, gcb = _lru_conv_bwd(proj, dxc, p["lru_conv_w"], "lru_conv_bwd")
    g["lru_conv_w"], g["lru_conv_b"] = gcw[:CONV_K], jnp.sum(gcb, axis=0)
    dy, dz, gsn = _ssd_post_bwd(sv["y"], proj, p["ssd_norm"], (dmix, SSD_W, 1), "ssd_post_bwd")
    g["ssd_norm"] = jnp.sum(gsn, axis=0)
    dxconv, ddt, dal, ddk, got_ssd = _ssd_scan_bwd(sv["xconv"], sv["dt"], sv["spar"], sv["states"], dy, "ssd_scan_bwd", comm_ssd)
    g["ssd_a_log"], g["ssd_d"] = dal[0, :8], ddk[0, :8]
    dxbc, ddtr, gsw, gsb, gdb = _ssd_pre_bwd(proj, dxconv, ddt, p["ssd_conv_w"], p["ssd_conv_b"],
                                             _lanes128(p["ssd_dt_bias"]), "ssd_pre_bwd")
    g["ssd_conv_w"], g["ssd_conv_b"], g["ssd_dt_bias"] = gsw[:CONV_K], jnp.sum(gsb, axis=0), jnp.sum(gdb, axis=0)[:8]
    delta = _att_delta((dmix, ATT_W, 0), sv["att"], "att_delta")
    dq, dk, dv, got_att = _att_bwd_rev(proj, dmix, sv["lse"], delta, "att_bwd", None if comm_att is None else comm_att(g))
    dproj = jnp.concatenate([dq, dk, dv, dz, dxbc, dgl, dxl, ddtr], axis=1)
    g["w_in_t"] = _mm(dproj, sv["h"], ta=True, tm=1408, tn=1024, tk=1024, name="mm_g_in")
    dh = _mm(dproj, p["w_in_t"], tm=1024, tn=1024, tk=1408, name="mm_d_h", comm=None if comm_tail is None else comm_tail(g))
    dh, got_tail = dh if comm_tail is not None else (dh, [])
    dx, gm = _rms_bwd(sv["x"], p["norm_mix"], dh, dx1, "rms_mix_bwd")
    g["norm_mix"] = jnp.sum(gm, axis=0)
    return dx, g, got_ssd, got_att, got_tail


def _grad_slabs(g, names):
    out = {}
    for n in names:
        if n == "w_in":
            out[n] = _w_in_slabs(g["w_in_t"])
        elif n == "w_gate":
            out[n] = _slabs(n, g["w_gu_t"][:D_FF])
        elif n == "w_up":
            out[n] = _slabs(n, g["w_gu_t"][D_FF:])
        else:
            out[n] = _slabs(n, g[n])
    return out


def kernel(x, norm_mix, w_in, ssd_conv_w, ssd_conv_b, ssd_dt_bias, ssd_a_log, ssd_d, ssd_norm, lru_conv_w, lru_conv_b, lru_wa, lru_ba, lru_wx, lru_bx, lru_lambda, w_out, norm_ffn, w_gate, w_up, w_down, norm_final, loss_target, m_norm_mix, m_w_in, m_ssd_conv_w, m_ssd_conv_b, m_ssd_dt_bias, m_ssd_a_log, m_ssd_d, m_ssd_norm, m_lru_conv_w, m_lru_conv_b, m_lru_wa, m_lru_ba, m_lru_wx, m_lru_bx, m_lru_lambda, m_w_out, m_norm_ffn, m_w_gate, m_w_up, m_w_down, m_norm_final, v_norm_mix, v_w_in, v_ssd_conv_w, v_ssd_conv_b, v_ssd_dt_bias, v_ssd_a_log, v_ssd_d, v_ssd_norm, v_lru_conv_w, v_lru_conv_b, v_lru_wa, v_lru_ba, v_lru_wx, v_lru_bx, v_lru_lambda, v_w_out, v_norm_ffn, v_w_gate, v_w_up, v_w_down, v_norm_final):
    loc = dict(locals())
    w = {n: loc[n] for n in WEIGHTS}
    m = {n: loc["m_" + n] for n in WEIGHTS}
    v = {n: loc["v_" + n] for n in WEIGHTS}
    for n in TRANSPOSED:
        w[n], m[n], v[n] = [jnp.transpose(t, (0, 2, 1)) for t in (w[n], m[n], v[n])]
    wt_in, mt_in, vt_in = [jnp.transpose(t, (2, 0, 1)) for t in (w["w_in"], m["w_in"], v["w_in"])]

    def halves(a):
        return a.reshape(a.shape[0], 2, a.shape[1] // 2, a.shape[2])

    def unhalve(a):
        return a.reshape(4, 2 * a.shape[2], a.shape[3])

    def joined(name, a):
        return _w_in_rows(unhalve(a)) if name == "w_in" else _join(name, unhalve(a))

    wb = {n: halves(w[n].astype(MXU)) for n in MATS[1:]}
    wb["w_in"] = halves(jnp.pad(jnp.transpose(wt_in.astype(MXU), (1, 0, 2)), ((0, 0), (0, W_IN_PAD - W_IN_SHARD), (0, 0))))
    xs = x[0]
    h0, first = _rms_fwd(xs, norm_mix[0], "rms_mix", _Comm(gathers=[(wb["w_in"], 0, True), (w["ssd_conv_w"], None, False),
                                                                    (w["lru_conv_w"], None, False)]))
    convs = {"ssd_conv_w": _join("ssd_conv_w", first[1]), "lru_conv_w": _join("lru_conv_w", first[2])}
    behind_att = [(n, 0) for n in MATS[1:]] + [("w_in", 1)]
    behind_ffn = [(n, 1) for n in MATS[1:]]
    whole = {("w_in", 0): joined("w_in", first[0])}
    params = {}

    def layer_params(l):
        if l not in params:
            p = {n: w[n][l] for n in SMALL if n != "norm_final"}
            p.update(w_in_t=whole["w_in", l], ssd_conv_w=convs["ssd_conv_w"][l], lru_conv_w=convs["lru_conv_w"][l])
            params[l] = p
        if "w_out" not in params[l] and ("w_out", l) in whole:
            params[l].update(w_out=whole["w_out", l], w_down=whole["w_down", l],
                             w_gu_t=jnp.concatenate([whole["w_gate", l], whole["w_up", l]], axis=0))
        return params[l]

    saved = []
    for l in range(DEPTH):
        first_layer = l == 0
        mix, sv, got = _layer_mixers(xs, layer_params(l), _Comm(gathers=[(wb[n], k, True) for n, k in behind_att]) if first_layer else None,
                                     h0 if first_layer else None)
        whole.update({k: joined(k[0], a) for k, a in zip(behind_att, got)})
        xs, got = _layer_ffn(xs, mix, layer_params(l), sv, _Comm(gathers=[(wb[n], k, True) for n, k in behind_ffn]) if first_layer else None)
        whole.update({k: joined(k[0], a) for k, a in zip(behind_ffn, got)})
        saved.append(sv)
    dx, gnf, lsum = _loss_head(xs, norm_final, loss_target[0], "loss_head")
    loss = lax.psum(jnp.sum(lsum), ("x", "y", "c"))

    dx, g1, _, _, _ = _layer_bwd(dx, layer_params(1), saved[1])
    s1 = _grad_slabs(g1, BIG)
    att0 = ("w_gate", "w_up", "w_down", "w_out")
    s0 = {}

    def wire(s, n):
        return s[n].astype(MXU) if n in MATS else s[n]

    def comm_att(g0):
        s0.update(_grad_slabs(g0, att0))
        return _Comm(scatters=[wire(s0, n) for n in att0])

    tail0 = ("w_in",) + CONVS

    def comm_tail(g0):
        s0.update(_grad_slabs(g0, tail0))
        return _Comm(scatters=[wire(s0, n) for n in tail0])

    dx, g0, got_ssd, got_att, got_tail = _layer_bwd(dx, layer_params(0), saved[0], _Comm(scatters=[wire(s1, n) for n in BIG]),
                                                    comm_att, comm_tail)
    recv = {(n, 1): a for n, a in zip(BIG, got_ssd)}
    recv.update({(n, 0): a for n, a in zip(att0, got_att)})
    recv.update({(n, 0): a for n, a in zip(tail0, got_tail)})

    me = 2 * lax.axis_index("x") + lax.axis_index("y")
    slabs = (s0, s1)
    part = {}
    for n in BIG:
        per_layer = []
        for l in range(DEPTH):
            own = lax.dynamic_index_in_dim(slabs[l][n], me, axis=0, keepdims=False)
            per_layer.append(_sum_slots(own, recv[n, l], "sum_chips_" + n, ROW_TILE.get(n, own.shape[0])))
        part[n] = jnp.stack(per_layer, axis=0)
    sib = dict(zip(BIG, _swap_sibling([part[n] for n in BIG])))
    out_g, out_d, out_m, out_v = {}, {}, {}, {}
    for n in BIG:
        if n == "w_in":
            res = _adamw(wt_in, mt_in, vt_in, jnp.transpose(part[n], (1, 0, 2)), jnp.transpose(sib[n], (1, 0, 2)), "adamw_" + n,
                         W_IN_ADAM_TILE, rows_first=True)
            out_g[n], out_d[n], out_m[n], out_v[n] = [jnp.transpose(t, (1, 2, 0)) for t in res]
            continue
        res = _adamw(w[n], m[n], v[n], part[n], sib[n], "adamw_" + n, ROW_TILE.get(n, w[n].shape[1]))
        out_g[n], out_d[n], out_m[n], out_v[n] = [jnp.transpose(t, (0, 2, 1)) for t in res] if n in TRANSPOSED else res

    gsm = {n: jnp.stack([g0[n], g1[n]], axis=0) for n in SMALL if n != "norm_final"}
    gsm["norm_final"] = jnp.sum(gnf, axis=0)
    small_shapes = [w[n].shape for n in SMALL]
    gs = _pack([gsm[n].reshape(w[n].shape) for n in SMALL], BLK, 8, F32)
    gall = _gather_small(gs)
    gsum = _sum_slots(None, gall, "sum_devices", gs.shape[0])
    ws = _pack([w[n] for n in SMALL], BLK, 8, F32)
    ms = _pack([m[n] for n in SMALL], BLK, 8, F32)
    vs = _pack([v[n] for n in SMALL], BLK, 8, F32)
    gsr, dsr, nms, nvs = _adamw(ws, ms, vs, gsum, None, "adamw_small", gs.shape[0])
    out_g.update(zip(SMALL, _unpack(gsr, small_shapes)))
    out_d.update(zip(SMALL, _unpack(dsr, small_shapes)))
    out_m.update(zip(SMALL, _unpack(nms, small_shapes)))
    out_v.update(zip(SMALL, _unpack(nvs, small_shapes)))

    return (loss, dx[None], *[out_g[n] for n in WEIGHTS], *[out_d[n] for n in WEIGHTS],
            *[out_m[n] for n in WEIGHTS], *[out_v[n] for n in WEIGHTS])
```

```python
import functools
import math

import jax
import jax.numpy as jnp
import numpy as np
from jax import lax
from jax.experimental import pallas as pl
from jax.experimental.pallas import tpu as pltpu

F32 = jnp.float32
MXU = jnp.bfloat16
HI = lax.Precision.HIGHEST
MESH = pl.DeviceIdType.MESH

D_MODEL = 1024
DEPTH = 2
HEAD_DIM = 64
ATT_W = 512
ATT_PATTERNS = ((128, 1), (512, 4), (2048, 16))
BLK = 128
SSD_W = 512
SSD_STATE = 128
LRU_W = 512
LRU_BLOCKS = 8
LRU_C = 8.0
CONV_K = 4
D_MIX = 1536
D_FF = 2816
IN_COLS = 4104
NP = 4224
NORM_EPS = 1e-6
SSD_NORM_EPS = 1e-5
LN2 = math.log(2.0)
NEG = -1e30

ADAM_LR, ADAM_B1, ADAM_B2, ADAM_EPS, ADAM_WD, ADAM_STEP = 0.001, 0.9, 0.999, 1e-08, 0.01, 10
BC1 = 1.0 - ADAM_B1 ** ADAM_STEP
BC2 = 1.0 - ADAM_B2 ** ADAM_STEP

VMEM_LIMIT = 56 * 1024 * 1024

C_Q, C_K, C_V, C_Z, C_XBC, C_G, C_XL, C_DT = 0, 512, 1024, 1536, 2048, 3072, 3584, 4096


def _cp(*sem):
    return pltpu.CompilerParams(dimension_semantics=sem, vmem_limit_bytes=VMEM_LIMIT)


def _dot(a, b, dims, prec=None):
    return lax.dot_general(a, b, (dims, ((), ())), preferred_element_type=F32, precision=prec)


def _nn(a, b, prec=None):
    return _dot(a, b, ((1,), (0,)), prec)


def _nt(a, b, prec=None):
    return _dot(a, b, ((1,), (1,)), prec)


def _tn(a, b, prec=None):
    return _dot(a, b, ((0,), (0,)), prec)


def _sigmoid(x):
    return jax.nn.sigmoid(x)


def _silu(x):
    return x * _sigmoid(x)


def _softplus(x):
    return jnp.maximum(x, 0.0) + jnp.log(1.0 + jnp.exp(-jnp.abs(x)))


def _gelu(x):
    return 0.5 * x * (1.0 + jnp.tanh(0.7978845608028654 * (x + 0.044715 * x * x * x)))


def _mm(a, b, *, ta=False, tb=False, add=None, out_dtype=F32, tm, tn, tk, name, comm=None, epi=None):
    m, k = (a.shape[1], a.shape[0]) if ta else a.shape
    n = b.shape[0] if tb else b.shape[1]
    assert (b.shape[1] if tb else b.shape[0]) == k
    assert m % tm == 0 and n % tn == 0 and k % tk == 0, (name, m, n, k)
    nk = k // tk
    a_spec = pl.BlockSpec((tk, tm), lambda i, j, kk: (kk, i)) if ta else pl.BlockSpec((tm, tk), lambda i, j, kk: (i, kk))
    b_spec = pl.BlockSpec((tn, tk), lambda i, j, kk: (j, kk)) if tb else pl.BlockSpec((tk, tn), lambda i, j, kk: (kk, j))
    o_spec = pl.BlockSpec((tm, tn), lambda i, j, kk: (i, j))
    dims = ((0 if ta else 1,), (1 if tb else 0,))
    carried = comm is not None
    comm = comm or _Comm()
    ni, nj = m // tm, n // tn
    efn, erows, econsts, eouts, eaccs = epi or (None, [], [], [], [])
    assert epi is None or nj == 1
    nadd = 0 if add is None else 1
    ner, nec, neo, nea = len(erows), len(econsts), len(eouts), len(eaccs)

    def body(*refs):
        refs, cm = comm.split(refs, 2 + nadd + ner + nec, 1 + neo + nea, 1)
        a_ref, b_ref = refs[:2]
        er_refs = refs[2 + nadd:2 + nadd + ner]
        ec_refs = refs[2 + nadd + ner:2 + nadd + ner + nec]
        o_ref = refs[2 + nadd + ner + nec]
        eo_refs = refs[3 + nadd + ner + nec:3 + nadd + ner + nec + neo]
        ea_refs = refs[3 + nadd + ner + nec + neo:3 + nadd + ner + nec + neo + nea]
        acc = refs[-1]
        i, j, kk = pl.program_id(0), pl.program_id(1), pl.program_id(2)
        comm.start_at((i == 0) & (j == 0) & (kk == 0), cm)

        @pl.when(kk == 0)
        def _():
            acc[...] = jnp.zeros_like(acc)

        acc[...] += _dot(a_ref[...].astype(MXU), b_ref[...].astype(MXU), dims)

        @pl.when(kk == nk - 1)
        def _():
            r = acc[...]
            if add is not None:
                r = r + refs[2][...]
            if efn is None:
                o_ref[...] = r.astype(out_dtype)
            else:
                main, extra, sums = efn(r, [t[...] for t in er_refs], [t[...] for t in ec_refs])
                o_ref[...] = main.astype(out_dtype)
                for t, val in zip(eo_refs, extra):
                    t[...] = val.astype(t.dtype)
                @pl.when(i == 0)
                def _():
                    for t, val in zip(ea_refs, sums):
                        t[...] = val

                @pl.when(i > 0)
                def _():
                    for t, val in zip(ea_refs, sums):
                        t[...] += val

        comm.wait_at((i == ni - 1) & (j == nj - 1) & (kk == nk - 1), cm)

    def whole_rows(width):
        return pl.BlockSpec((tm, width), lambda i, j, kk: (i, 0))

    ins = [a, b] + ([] if add is None else [add]) + list(erows) + list(econsts)
    specs = [a_spec, b_spec] + ([] if add is None else [o_spec]) + [whole_rows(t.shape[1]) for t in erows]
    specs += [pl.BlockSpec(t.shape, lambda i, j, kk: (0, 0)) for t in econsts]
    out_specs = [o_spec] + [whole_rows(wd) for wd, _ in eouts] + [pl.BlockSpec((r, wd), lambda i, j, kk: (0, 0)) for r, wd in eaccs]
    out_shape = [jax.ShapeDtypeStruct((m, n), out_dtype)] + [jax.ShapeDtypeStruct((m, wd), dt) for wd, dt in eouts]
    out_shape += [jax.ShapeDtypeStruct((r, wd), F32) for r, wd in eaccs]
    serial = comm.n or nea
    res = pl.pallas_call(
        body, name=name, grid=(ni, nj, nk), in_specs=specs + [ANY] * comm.n, out_specs=out_specs + [ANY] * comm.n,
        out_shape=out_shape + comm.out_shape(),
        scratch_shapes=[pltpu.VMEM((tm, tn), F32)] + comm.scratch(),
        compiler_params=_cp(*((["arbitrary"] * 3) if serial else ["parallel", "parallel", "arbitrary"])),
    )(*ins, *comm.args())
    nown = 1 + neo + nea
    own = res[0] if epi is None else list(res[:nown])
    return (own, list(res[nown:])) if carried else own


def _rows(fn, rows, consts=(), outs=(), accs=(), *, tile, name, halos=(), comm=None):
    rows = [r if isinstance(r, tuple) else (r, r.shape[1], 0) for r in rows]
    s = rows[0][0].shape[0]
    assert s % tile == 0 and tile % 8 == 0
    n = s // tile
    t8 = tile // 8
    nr, nh, nc_, no, na = len(rows), len(halos), len(consts), len(outs), len(accs)
    carried = comm is not None
    comm = comm or _Comm()

    def body(*refs):
        refs, cm = comm.split(refs, nr + nh + nc_, no + na, 0)
        i = pl.program_id(0)
        comm.start_at(i == 0, cm)
        rv = [r[...] for r in refs[:nr]]
        hv = []
        for (idx, kind), r in zip(halos, refs[nr:nr + nh]):
            edge = (i == 0) if kind == "prev" else (i == n - 1)
            hv.append(jnp.where(edge, 0.0, r[...]))
        cv = [r[...] for r in refs[nr + nh:nr + nh + nc_]]
        o_refs = refs[nr + nh + nc_:nr + nh + nc_ + no]
        a_refs = refs[nr + nh + nc_ + no:]
        ov, av = fn(rv, hv, cv)
        for r, v in zip(o_refs, ov):
            r[...] = v.astype(r.dtype)
        if na:
            @pl.when(i == 0)
            def _():
                for r in a_refs:
                    r[...] = jnp.zeros_like(r)
            for r, v in zip(a_refs, av):
                r[...] += v
        comm.wait_at(i == n - 1, cm)

    in_specs = [pl.BlockSpec((tile, w), functools.partial(lambda i, cb: (i, cb), cb=cb)) for (_, w, cb) in rows]
    for idx, kind in halos:
        _, w, cb = rows[idx]
        if kind == "prev":
            in_specs.append(pl.BlockSpec((8, w), functools.partial(lambda i, cb: (jnp.maximum(i * t8 - 1, 0), cb), cb=cb)))
        else:
            in_specs.append(pl.BlockSpec((8, w), functools.partial(lambda i, cb: (jnp.minimum((i + 1) * t8, n * t8 - 1), cb), cb=cb)))
    in_specs += [pl.BlockSpec(c.shape, functools.partial(lambda i, nd: (0,) * nd, nd=c.ndim)) for c in consts]
    out_specs = [pl.BlockSpec((tile, c), lambda i: (i, 0)) for (c, _) in outs]
    out_specs += [pl.BlockSpec((r, c), lambda i: (0, 0)) for (r, c) in accs]
    out_shape = [jax.ShapeDtypeStruct((s, c), dt) for (c, dt) in outs]
    out_shape += [jax.ShapeDtypeStruct((r, c), F32) for (r, c) in accs]
    args = [r[0] for r in rows] + [rows[idx][0] for idx, _ in halos] + list(consts)
    res = pl.pallas_call(
        body, name=name, grid=(n,), in_specs=in_specs + [ANY] * comm.n, out_specs=out_specs + [ANY] * comm.n,
        out_shape=out_shape + comm.out_shape(), scratch_shapes=comm.scratch(), compiler_params=_cp("arbitrary"),
    )(*args, *comm.args())
    return (list(res[:no + na]), list(res[no + na:])) if carried else list(res)


def _colsum8(v):
    t, c = v.shape
    return jnp.sum(v.reshape(t // 8, 8, c), axis=0)


def _rms(x, g):
    return x * lax.rsqrt(jnp.mean(x * x, axis=-1, keepdims=True) + NORM_EPS) * g


def _epi_rms(g):
    return (lambda r, rows, consts: (r, [_rms(r, consts[0])], []), [], [g.reshape(1, -1)], [(g.shape[-1], MXU)], [])


def _epi_rms_bwd(x, g, dres):
    def fn(r, rows, consts):
        xb, drb = rows
        _, vjp = jax.vjp(_rms, xb, consts[0])
        rstd = lax.rsqrt(jnp.mean(xb * xb, axis=-1, keepdims=True) + NORM_EPS)
        return drb + vjp(r)[0], [], [_colsum8(r * xb * rstd)]
    return (fn, [x, dres], [g.reshape(1, -1)], [], [(8, g.shape[-1])])


def _epi_att_delta(att):
    def fn(r, rows, consts):
        hr = lax.broadcasted_iota(jnp.int32, (ATT_W, ATT_W), 0) // HEAD_DIM
        hc = lax.broadcasted_iota(jnp.int32, (ATT_W, ATT_W), 1) // HEAD_DIM
        return r, [_nn(r[:, :ATT_W] * rows[0], (hr == hc).astype(F32), HI)], []
    return (fn, [att], [], [(ATT_W, F32)], [])


def _rms_fwd(x, g, name, comm=None):
    def fn(rv, hv, cv):
        return [_rms(rv[0], cv[0])], []
    res = _rows(fn, [x], [g.reshape(1, -1)], [(x.shape[1], MXU)], tile=512, name=name, comm=comm)
    return res[0] if comm is None else (res[0][0], res[1])


def _rms_bwd(x, g, dh, dres, name):
    def fn(rv, hv, cv):
        xb, dhb, drb = rv
        _, vjp = jax.vjp(_rms, xb, cv[0])
        dx, _ = vjp(dhb)
        rstd = lax.rsqrt(jnp.mean(xb * xb, axis=-1, keepdims=True) + NORM_EPS)
        return [drb + dx], [_colsum8(dhb * xb * rstd)]
    d = x.shape[1]
    return _rows(fn, [x, dh, dres], [g.reshape(1, -1)], [(d, F32)], [(8, d)], tile=512, name=name)


def _slope_dist(hp, hh, dist, dil):
    hf = (2 * hp + hh + 1).astype(F32)
    slope = jnp.exp(jnp.zeros(dist.shape, F32) - hf * LN2)
    return slope * (dist.astype(F32) * float(dil))


def _att_delta(datt, att, name):
    def fn(rv, hv, cv):
        r = lax.broadcasted_iota(jnp.int32, (ATT_W, ATT_W), 0) // HEAD_DIM
        c = lax.broadcasted_iota(jnp.int32, (ATT_W, ATT_W), 1) // HEAD_DIM
        ones = (r == c).astype(F32)
        return [_nn(rv[0] * rv[1], ones, HI)], []
    return _rows(fn, [datt, att], [], [(ATT_W, F32)], tile=512, name=name)[0]


ATT_G = 2048


def _deinterleave(dst, src, dil, ld, region, offset):
    for r in range(dil):
        rows = pl.ds(r, ld, stride=dil) if dil > 1 else pl.ds(0, ld)
        dst[r * region + offset:r * region + offset + ld, :] = src[rows, :]


def _deinterleave_edge(dst, src, dil, region, offset, first_row):
    for r in range(dil):
        rows = pl.ds(first_row + r, BLK, stride=dil) if dil > 1 else pl.ds(first_row, BLK)
        dst[r * region + offset:r * region + offset + BLK, :] = src[rows, :]


def _att_fwd_fused(proj, name, comm=None):
    s, npc = proj.shape
    gsz = ATT_G
    ng = s // gsz
    assert s % gsz == 0
    scale = HEAD_DIM ** -0.5
    comm = comm or _Comm()

    def body(*refs):
        (q_ref, kp_ref, kc_ref, vp_ref, vc_ref, att_ref, lse_ref, attb_ref, qd, kd, vd, nd, md, dd, nn, mn, dn), cm = comm.split(refs, 5, 3, 9)
        hp, g = pl.program_id(0), pl.program_id(1)
        comm.start_at((hp == 0) & (g == 0), cm)
        lane = lax.broadcasted_iota(jnp.int32, (BLK, BLK), 1)
        qi = lax.broadcasted_iota(jnp.int32, (BLK, 2 * BLK), 0)
        ki = lax.broadcasted_iota(jnp.int32, (BLK, 2 * BLK), 1)
        dist = BLK + qi - ki
        band = (dist >= 0) & (dist <= BLK)
        for pi, (_, dil) in enumerate(ATT_PATTERNS):
            ld = gsz // dil
            nbg = ld // BLK
            _deinterleave(qd, q_ref, dil, ld, ld, 0)
            _deinterleave(kd, kc_ref, dil, ld, ld + BLK, BLK)
            _deinterleave(vd, vc_ref, dil, ld, ld + BLK, BLK)
            _deinterleave_edge(kd, kp_ref, dil, ld + BLK, 0, gsz - BLK * dil)
            _deinterleave_edge(vd, vp_ref, dil, ld + BLK, 0, gsz - BLK * dil)
            bias = [_slope_dist(hp, hh, dist, dil) for hh in (0, 1)]

            def tile(t, carry, ld=ld, nbg=nbg, bias=bias):
                r, b = t // nbg, t % nbg
                qo = pl.multiple_of(r * ld + b * BLK, BLK)
                ko = pl.multiple_of(r * (ld + BLK) + b * BLK, BLK)
                q = qd[pl.ds(qo, BLK), :]
                kk = kd[pl.ds(ko, 2 * BLK), :].astype(MXU)
                vv = vd[pl.ds(ko, 2 * BLK), :].astype(MXU)
                valid = band & ((g > 0) | (b > 0) | (ki >= BLK))
                num = jnp.zeros((BLK, BLK), F32)
                mx = jnp.zeros((BLK, BLK), F32)
                den = jnp.zeros((BLK, BLK), F32)
                for hh in (0, 1):
                    hmask = (lane < HEAD_DIM) if hh == 0 else (lane >= HEAD_DIM)
                    qm = jnp.where(hmask, q, 0.0).astype(MXU)
                    sc = jnp.where(valid, _nt(qm, kk) * scale - bias[hh], NEG)
                    m = jnp.max(sc, axis=1, keepdims=True)
                    p = jnp.exp(sc - m)
                    dn_ = jnp.sum(p, axis=1, keepdims=True)
                    o = _nn(p.astype(MXU), vv)
                    num = jnp.where(hmask, o, num)
                    mx = jnp.where(hmask, m, mx)
                    den = jnp.where(hmask, dn_, den)
                nd[pl.ds(qo, BLK), :] = num
                md[pl.ds(qo, BLK), :] = mx
                dd[pl.ds(qo, BLK), :] = den
                return carry

            lax.fori_loop(0, dil * nbg, tile, 0, unroll=8)
            for r in range(dil):
                rows = pl.ds(r, ld, stride=dil) if dil > 1 else pl.ds(0, ld)
                nn.at[pi][rows, :] = nd[r * ld:(r + 1) * ld, :]
                mn.at[pi][rows, :] = md[r * ld:(r + 1) * ld, :]
                dn.at[pi][rows, :] = dd[r * ld:(r + 1) * ld, :]

        def merge(c, carry):
            rows = pl.ds(pl.multiple_of(c * 256, 256), 256)
            ms = [mn[pi, rows, :] for pi in range(len(ATT_PATTERNS))]
            m_all = functools.reduce(jnp.maximum, ms)
            num = jnp.zeros((256, BLK), F32)
            den = jnp.zeros((256, BLK), F32)
            for pi in range(len(ATT_PATTERNS)):
                e = jnp.exp(ms[pi] - m_all)
                num = num + nn[pi, rows, :] * e
                den = den + dn[pi, rows, :] * e
            att = num / den
            att_ref[rows, :] = att
            attb_ref[rows, :] = att.astype(MXU)
            lse_ref[rows, :] = m_all + jnp.log(den)
            return carry

        lax.fori_loop(0, gsz // 256, merge, 0)
        comm.wait_at((hp == 3) & (g == ng - 1), cm)

    def cur(base):
        return pl.BlockSpec((gsz, BLK), lambda hp, g: (g, base // BLK + hp))

    def prev(base):
        return pl.BlockSpec((gsz, BLK), lambda hp, g: (jnp.maximum(g - 1, 0), base // BLK + hp))

    o_spec = pl.BlockSpec((gsz, BLK), lambda hp, g: (g, hp))
    npat = len(ATT_PATTERNS)
    res = pl.pallas_call(
        body, name=name, grid=(4, ng),
        in_specs=[cur(C_Q), prev(C_K), cur(C_K), prev(C_V), cur(C_V)] + [ANY] * comm.n,
        out_specs=[o_spec] * 3 + [ANY] * comm.n,
        out_shape=[jax.ShapeDtypeStruct((s, ATT_W), F32)] * 2 + [jax.ShapeDtypeStruct((s, ATT_W), MXU)] + comm.out_shape(),
        scratch_shapes=[pltpu.VMEM((gsz, BLK), F32), pltpu.VMEM((2 * gsz, BLK), F32), pltpu.VMEM((2 * gsz, BLK), F32)]
        + [pltpu.VMEM((gsz, BLK), F32)] * 3 + [pltpu.VMEM((npat, gsz, BLK), F32)] * 3 + comm.scratch(),
        compiler_params=_cp("arbitrary", "arbitrary"),
    )(proj, proj, proj, proj, proj, *comm.args())
    return res[0], res[1], res[2], list(res[3:])


def _att_bwd_fused(proj, datt, lse, delta, name, comm=None):
    s, npc = proj.shape
    gsz = ATT_G
    ng = s // gsz
    scale = HEAD_DIM ** -0.5
    comm = comm or _Comm()

    def body(*refs):
        (qc_ref, qn_ref, kp_ref, kc_ref, vp_ref, vc_ref, doc_ref, don_ref, lsc_ref, lsn_ref, dlc_ref, dln_ref,
         dq_ref, dk_ref, dv_ref, qd, dod, lsd, dld, kd, vd, dqd, dkd, dvd), cm = comm.split(refs, 12, 3, 9)
        hp, g = pl.program_id(0), pl.program_id(1)
        comm.start_at((hp == 0) & (g == 0), cm)
        lane = lax.broadcasted_iota(jnp.int32, (BLK, BLK), 1)
        qi = lax.broadcasted_iota(jnp.int32, (BLK, BLK), 0)
        ki = lax.broadcasted_iota(jnp.int32, (BLK, BLK), 1)
        d_far = BLK + qi - ki
        d_near = qi - ki
        for pi, (_, dil) in enumerate(ATT_PATTERNS):
            ld = gsz // dil
            nbg = ld // BLK
            reg = ld + BLK
            for dst, c_ref, n_ref in ((qd, qc_ref, qn_ref), (dod, doc_ref, don_ref), (lsd, lsc_ref, lsn_ref), (dld, dlc_ref, dln_ref)):
                _deinterleave(dst, c_ref, dil, ld, reg, 0)
                _deinterleave_edge(dst, n_ref, dil, reg, ld, 0)
            for dst, p_ref, c_ref in ((kd, kp_ref, kc_ref), (vd, vp_ref, vc_ref)):
                _deinterleave(dst, c_ref, dil, ld, reg, BLK)
                _deinterleave_edge(dst, p_ref, dil, reg, 0, gsz - BLK * dil)
            b_far = [_slope_dist(hp, hh, d_far, dil) for hh in (0, 1)]
            b_near = [_slope_dist(hp, hh, d_near, dil) for hh in (0, 1)]

            def tile(t, carry, ld=ld, nbg=nbg, reg=reg, b_far=b_far, b_near=b_near):
                r, b = t // nbg, t % nbg
                oo = pl.multiple_of(r * ld + b * BLK, BLK)
                ro = pl.multiple_of(r * reg + b * BLK, BLK)
                qn, qx = qd[pl.ds(ro, BLK), :], qd[pl.ds(ro + BLK, BLK), :]
                don, dox = dod[pl.ds(ro, BLK), :], dod[pl.ds(ro + BLK, BLK), :]
                lsn, lsx = lsd[pl.ds(ro, BLK), :], lsd[pl.ds(ro + BLK, BLK), :]
                dln, dlx = dld[pl.ds(ro, BLK), :], dld[pl.ds(ro + BLK, BLK), :]
                kp, kc = kd[pl.ds(ro, BLK), :].astype(MXU), kd[pl.ds(ro + BLK, BLK), :].astype(MXU)
                vp, vc = vd[pl.ds(ro, BLK), :].astype(MXU), vd[pl.ds(ro + BLK, BLK), :].astype(MXU)
                ok_a = (d_far <= BLK) & ((g > 0) | (b > 0))
                ok_b = d_near >= 0
                ok_c = (d_far <= BLK) & ((g < ng - 1) | (b < nbg - 1))

                def grads(qm, dom, k, v, ls, dl, bias, valid, hh):
                    c0 = hh * HEAD_DIM
                    sc = _nt(qm, k) * scale - bias
                    p = jnp.exp(jnp.where(valid, sc - ls[:, c0:c0 + 1], NEG))
                    ds = p * (_nt(dom, v) - dl[:, c0:c0 + 1])
                    return p.astype(MXU), ds.astype(MXU)

                dq = jnp.zeros((BLK, BLK), F32)
                dk = jnp.zeros((BLK, BLK), F32)
                dv = jnp.zeros((BLK, BLK), F32)
                for hh in (0, 1):
                    hmask = (lane < HEAD_DIM) if hh == 0 else (lane >= HEAD_DIM)
                    qnm = jnp.where(hmask, qn, 0.0).astype(MXU)
                    qxm = jnp.where(hmask, qx, 0.0).astype(MXU)
                    donm = jnp.where(hmask, don, 0.0).astype(MXU)
                    doxm = jnp.where(hmask, dox, 0.0).astype(MXU)
                    _, ds_a = grads(qnm, donm, kp, vp, lsn, dln, b_far[hh], ok_a, hh)
                    p_b, ds_b = grads(qnm, donm, kc, vc, lsn, dln, b_near[hh], ok_b, hh)
                    p_c, ds_c = grads(qxm, doxm, kc, vc, lsx, dlx, b_far[hh], ok_c, hh)
                    dq = jnp.where(hmask, _nn(ds_a, kp) + _nn(ds_b, kc), dq)
                    dk = dk + _tn(ds_b, qnm) + _tn(ds_c, qxm)
                    dv = dv + _tn(p_b, donm) + _tn(p_c, doxm)
                dqd[pl.ds(oo, BLK), :] = dq * scale
                dkd[pl.ds(oo, BLK), :] = dk * scale
                dvd[pl.ds(oo, BLK), :] = dv
                return carry

            lax.fori_loop(0, dil * nbg, tile, 0, unroll=4)
            for out, src in ((dq_ref, dqd), (dk_ref, dkd), (dv_ref, dvd)):
                for r in range(dil):
                    rows = pl.ds(r, ld, stride=dil) if dil > 1 else pl.ds(0, ld)
                    if pi == 0:
                        out[rows, :] = src[r * ld:(r + 1) * ld, :]
                    else:
                        out[rows, :] = out[rows, :] + src[r * ld:(r + 1) * ld, :]
        comm.wait_at((hp == 3) & (g == ng - 1), cm)

    def pspec(base, shift):
        return pl.BlockSpec((gsz, BLK), lambda hp, g: (jnp.clip(g + shift, 0, ng - 1), base // BLK + hp))

    def wspec(shift):
        return pl.BlockSpec((gsz, BLK), lambda hp, g: (jnp.clip(g + shift, 0, ng - 1), hp))

    in_specs = [pspec(C_Q, 0), pspec(C_Q, 1), pspec(C_K, -1), pspec(C_K, 0), pspec(C_V, -1), pspec(C_V, 0),
                wspec(0), wspec(1), wspec(0), wspec(1), wspec(0), wspec(1)] + [ANY] * comm.n
    res = pl.pallas_call(
        body, name=name, grid=(4, ng), in_specs=in_specs,
        out_specs=[wspec(0)] * 3 + [ANY] * comm.n,
        out_shape=[jax.ShapeDtypeStruct((s, ATT_W), F32)] * 3 + comm.out_shape(),
        scratch_shapes=[pltpu.VMEM((2 * gsz, BLK), F32)] * 6 + [pltpu.VMEM((gsz, BLK), F32)] * 3 + comm.scratch(),
        compiler_params=_cp("arbitrary", "arbitrary"),
    )(proj, proj, proj, proj, proj, proj, datt, datt, lse, lse, delta, delta, *comm.args())
    return res[0], res[1], res[2], list(res[3:])


def _att_bwd_rev(proj, datt, lse, delta, name, comm=None):
    s, npc = proj.shape
    gsz = ATT_G
    ng = s // gsz
    npat = len(ATT_PATTERNS)
    scale = HEAD_DIM ** -0.5
    comm = comm or _Comm()

    def body(*refs):
        (q_ref, kp_ref, kc_ref, vp_ref, vc_ref, do_ref, ls_ref, dl_ref, dq_out, dk_out, dv_out,
         qd, dod, lsd, dld, kd, vd, dqd, dkc, dvc, dkp, dvp, kcar, vcar, dq_ref, dk_ref, dv_ref), cm = comm.split(refs, 8, 3, 16)
        hp, gi = pl.program_id(0), pl.program_id(1)
        g = ng - 1 - gi
        comm.start_at((hp == 0) & (gi == 0), cm)

        @pl.when(gi == 0)
        def _():
            kcar[...] = jnp.zeros_like(kcar)
            vcar[...] = jnp.zeros_like(vcar)

        lane = lax.broadcasted_iota(jnp.int32, (BLK, BLK), 1)
        qi = lax.broadcasted_iota(jnp.int32, (BLK, 2 * BLK), 0)
        ki = lax.broadcasted_iota(jnp.int32, (BLK, 2 * BLK), 1)
        dist = BLK + qi - ki
        band = (dist >= 0) & (dist <= BLK)
        for pi, (_, dil) in enumerate(ATT_PATTERNS):
            ld = gsz // dil
            nbg = ld // BLK
            reg = ld + BLK
            for dst, src in ((qd, q_ref), (dod, do_ref), (lsd, ls_ref), (dld, dl_ref)):
                _deinterleave(dst, src, dil, ld, ld, 0)
            for dst, p_ref, c_ref in ((kd, kp_ref, kc_ref), (vd, vp_ref, vc_ref)):
                _deinterleave(dst, c_ref, dil, ld, reg, BLK)
                _deinterleave_edge(dst, p_ref, dil, reg, 0, gsz - BLK * dil)
            bias = [_slope_dist(hp, hh, dist, dil) for hh in (0, 1)]

            def tile(t, carry, ld=ld, nbg=nbg, reg=reg, bias=bias):
                r, b = t // nbg, t % nbg
                oo = pl.multiple_of(r * ld + b * BLK, BLK)
                ko = pl.multiple_of(r * reg + b * BLK, BLK)
                q, do = qd[pl.ds(oo, BLK), :], dod[pl.ds(oo, BLK), :]
                ls, dl = lsd[pl.ds(oo, BLK), :], dld[pl.ds(oo, BLK), :]
                kk = kd[pl.ds(ko, 2 * BLK), :].astype(MXU)
                vv = vd[pl.ds(ko, 2 * BLK), :].astype(MXU)
                valid = band & ((g > 0) | (b > 0) | (ki >= BLK))
                dq = jnp.zeros((BLK, BLK), F32)
                dkk = jnp.zeros((2 * BLK, BLK), F32)
                dvv = jnp.zeros((2 * BLK, BLK), F32)
                for hh in (0, 1):
                    c0 = hh * HEAD_DIM
                    hmask = (lane < HEAD_DIM) if hh == 0 else (lane >= HEAD_DIM)
                    qm = jnp.where(hmask, q, 0.0).astype(MXU)
                    dom = jnp.where(hmask, do, 0.0).astype(MXU)
                    sc = _nt(qm, kk) * scale - bias[hh]
                    p = jnp.exp(jnp.where(valid, sc - ls[:, c0:c0 + 1], NEG))
                    ds = (p * (_nt(dom, vv) - dl[:, c0:c0 + 1])).astype(MXU)
                    dq = jnp.where(hmask, _nn(ds, kk), dq)
                    dkk = dkk + _tn(ds, qm)
                    dvv = dvv + _tn(p.astype(MXU), dom)
                dqd[pl.ds(oo, BLK), :] = dq * scale
                dkp[pl.ds(oo, BLK), :] = dkk[:BLK] * scale
                dkc[pl.ds(oo, BLK), :] = dkk[BLK:] * scale
                dvp[pl.ds(oo, BLK), :] = dvv[:BLK]
                dvc[pl.ds(oo, BLK), :] = dvv[BLK:]
                return carry

            lax.fori_loop(0, dil * nbg, tile, 0, unroll=8)
            for r in range(dil):
                rows = pl.ds(r, ld, stride=dil) if dil > 1 else pl.ds(0, ld)
                lo, hi = r * ld, (r + 1) * ld
                edge = slice(pi * gsz + r * BLK, pi * gsz + (r + 1) * BLK)
                for out, cur, prv, car in ((dk_ref, dkc, dkp, kcar), (dv_ref, dvc, dvp, vcar)):
                    later = car[edge, :] if nbg == 1 else jnp.concatenate([prv[lo + BLK:hi, :], car[edge, :]], axis=0)
                    total = cur[lo:hi, :] + later
                    car[edge, :] = prv[lo:lo + BLK, :]
                    out[rows, :] = total if pi == 0 else out[rows, :] + total
                dq_ref[rows, :] = dqd[lo:hi, :] if pi == 0 else dq_ref[rows, :] + dqd[lo:hi, :]
        for out, acc in ((dq_out, dq_ref), (dk_out, dk_ref), (dv_out, dv_ref)):
            out[...] = acc[...].astype(out.dtype)
        comm.wait_at((hp == 3) & (gi == ng - 1), cm)

    def pspec(base, shift):
        return pl.BlockSpec((gsz, BLK), lambda hp, gi: (jnp.maximum(ng - 1 - gi + shift, 0), base // BLK + hp))

    wspec = pl.BlockSpec((gsz, BLK), lambda hp, gi: (ng - 1 - gi, hp))
    in_specs = [pspec(C_Q, 0), pspec(C_K, -1), pspec(C_K, 0), pspec(C_V, -1), pspec(C_V, 0), wspec, wspec, wspec] + [ANY] * comm.n
    res = pl.pallas_call(
        body, name=name, grid=(4, ng), in_specs=in_specs,
        out_specs=[wspec] * 3 + [ANY] * comm.n,
        out_shape=[jax.ShapeDtypeStruct((s, ATT_W), MXU)] * 3 + comm.out_shape(),
        scratch_shapes=[pltpu.VMEM((gsz, BLK), F32)] * 4 + [pltpu.VMEM((2 * gsz, BLK), F32)] * 2
        + [pltpu.VMEM((gsz, BLK), F32)] * 5 + [pltpu.VMEM((npat * gsz, BLK), F32)] * 2 + [pltpu.VMEM((gsz, BLK), F32)] * 3
        + comm.scratch(),
        compiler_params=_cp("arbitrary", "arbitrary"),
    )(proj, proj, proj, proj, proj, datt, lse, delta, *comm.args())
    return res[0], res[1], res[2], list(res[3:])


def _shift_down(cur, halo, sft):
    if sft == 0:
        return cur
    t = cur.shape[0]
    rolled = pltpu.roll(cur, sft, 0)
    hr = pltpu.roll(halo, sft, 0)
    row = lax.broadcasted_iota(jnp.int32, cur.shape, 0)
    return jnp.where(row < sft, jnp.tile(hr, (t // 8, 1)), rolled)


def _shift_up(cur, halo, sft):
    if sft == 0:
        return cur
    t = cur.shape[0]
    rolled = pltpu.roll(cur, t - sft, 0)
    hr = pltpu.roll(halo, 8 - sft, 0)
    row = lax.broadcasted_iota(jnp.int32, cur.shape, 0)
    return jnp.where(row >= t - sft, jnp.tile(hr, (t // 8, 1)), rolled)


def _conv(x, xh, w, b):
    y = b + x * w[CONV_K - 1:CONV_K]
    for k in range(CONV_K - 1):
        y = y + _shift_down(x, xh, CONV_K - 1 - k) * w[k:k + 1]
    return y


def _conv_bwd(x, xh, dy, dyh, w):
    dx = dy * w[CONV_K - 1:CONV_K]
    dws = []
    for k in range(CONV_K - 1):
        sft = CONV_K - 1 - k
        dx = dx + _shift_up(dy, dyh, sft) * w[k:k + 1]
        dws.append(jnp.sum(dy * _shift_down(x, xh, sft), axis=0, keepdims=True))
    dws.append(jnp.sum(dy * x, axis=0, keepdims=True))
    c = x.shape[1]
    dw = jnp.concatenate(dws + [jnp.zeros((8 - CONV_K, c), F32)], axis=0)
    return dx, dw, jnp.sum(dy, axis=0, keepdims=True)


def _pad8(w):
    return jnp.concatenate([w, jnp.zeros((8 - w.shape[0], w.shape[1]), w.dtype)], axis=0)


def _ssd_pre(proj, conv_w, conv_b, dt_bias128, name):
    def fn(rv, hv, cv):
        xbc, dtr = rv
        return [_silu(_conv(xbc, hv[0], cv[0], cv[1])), _softplus(dtr + cv[2])], []
    return _rows(fn, [(proj, 1024, C_XBC // 1024), (proj, BLK, C_DT // BLK)],
                 [_pad8(conv_w), conv_b.reshape(1, -1), dt_bias128],
                 [(1024, F32), (BLK, F32)], tile=256, name=name, halos=[(0, "prev")])


def _ssd_pre_bwd(proj, dxc, ddt, conv_w, conv_b, dt_bias128, name):
    def fn(rv, hv, cv):
        xbc, dtr, dxcb, ddtb = rv
        xh, dxch_raw, xnext = hv
        w, b, bias = cv
        pre = _conv(xbc, xh, w, b)
        sg = _sigmoid(pre)
        dpre = dxcb * (sg * (1.0 + pre * (1.0 - sg)))
        t = xbc.shape[0]
        tail = jnp.concatenate([xbc[t - 8:], xnext], axis=0)
        pre_n = _conv(tail[8:], tail[:8], w, b)
        sgn = _sigmoid(pre_n)
        dpre_h = dxch_raw * (sgn * (1.0 + pre_n * (1.0 - sgn)))
        dx, dw, db = _conv_bwd(xbc, xh, dpre, dpre_h, w)
        ddr = ddtb * _sigmoid(dtr + bias)
        return [dx, ddr], [dw, jnp.concatenate([db, jnp.zeros((7, db.shape[1]), F32)], axis=0), _colsum8(ddr)]
    return _rows(fn, [(proj, 1024, C_XBC // 1024), (proj, BLK, C_DT // BLK), dxc, ddt],
                 [_pad8(conv_w), conv_b.reshape(1, -1), dt_bias128],
                 [(1024, MXU), (BLK, MXU)], [(8, 1024), (8, 1024), (8, BLK)], tile=256, name=name,
                 halos=[(0, "prev"), (2, "next"), (0, "next")])


SSD_CPB = 1


def _head_cols(v, h0):
    lane = lax.broadcasted_iota(jnp.int32, (v.shape[0], BLK), 1)
    return jnp.where(lane < HEAD_DIM, v[:, h0:h0 + 1], v[:, h0 + 1:h0 + 2])


def _ssd_scan(xc, dt, par, name):
    s = xc.shape[0]
    nc = s // BLK

    def body(x_ref, dt_ref, par_ref, y_ref, st_ref, h_ref):
        c = pl.program_id(0)

        @pl.when(c == 0)
        def _():
            h_ref[...] = jnp.zeros_like(h_ref)

        st_ref[0] = h_ref[...]
        dt = dt_ref[...]
        a_row = -jnp.exp(par_ref[0:1, :])
        d_row = par_ref[1:2, :]
        ri = lax.broadcasted_iota(jnp.int32, (BLK, BLK), 0)
        ci = lax.broadcasted_iota(jnp.int32, (BLK, BLK), 1)
        tril = ri >= ci
        cs = _nn(tril.astype(F32), dt * a_row, HI)
        cst, dtt = cs.T, dt.T
        last = cs[BLK - 1:BLK, :]
        wcol = jnp.exp(last - cs) * dt
        ecs = jnp.exp(cs)
        elast = jnp.exp(last)
        for g in (0, 1):
            bg = x_ref[:, 512 + g * BLK:512 + (g + 1) * BLK].astype(MXU)
            cg = x_ref[:, 768 + g * BLK:768 + (g + 1) * BLK].astype(MXU)
            gm = _nt(cg, bg)
            for pp in (0, 1):
                pr = 2 * g + pp
                h0 = 2 * pr
                x2 = x_ref[:, pr * BLK:(pr + 1) * BLK]
                hprev = h_ref[pr * BLK:(pr + 1) * BLK, :]
                yp = jnp.zeros((BLK, BLK), F32)
                for hh in (0, 1):
                    h = h0 + hh
                    hmask = (ci < HEAD_DIM) if hh == 0 else (ci >= HEAD_DIM)
                    lm = jnp.exp(jnp.where(tril, cs[:, h:h + 1] - cst[h:h + 1, :], NEG))
                    mm = gm * lm * dtt[h:h + 1, :]
                    yp = yp + _nn(mm.astype(MXU), jnp.where(hmask, x2, 0.0).astype(MXU))
                y0 = _nt(cg, hprev.astype(MXU))
                y_ref[:, pr * BLK:(pr + 1) * BLK] = yp + _head_cols(ecs, h0) * y0 + _head_cols(d_row, h0) * x2
                dec = jnp.where(ri < HEAD_DIM, elast[:, h0:h0 + 1], elast[:, h0 + 1:h0 + 2])
                xw = (x2 * _head_cols(wcol, h0)).astype(MXU)
                h_ref[pr * BLK:(pr + 1) * BLK, :] = dec * hprev + _tn(xw, bg)

    return pl.pallas_call(
        body, name=name, grid=(nc,),
        in_specs=[pl.BlockSpec((BLK, 1024), lambda c: (c, 0)), pl.BlockSpec((BLK, BLK), lambda c: (c, 0)),
                  pl.BlockSpec((8, BLK), lambda c: (0, 0))],
        out_specs=[pl.BlockSpec((BLK, SSD_W), lambda c: (c, 0)), pl.BlockSpec((1, SSD_W, SSD_STATE), lambda c: (c, 0, 0))],
        out_shape=[jax.ShapeDtypeStruct((s, SSD_W), F32), jax.ShapeDtypeStruct((nc, SSD_W, SSD_STATE), F32)],
        scratch_shapes=[pltpu.VMEM((SSD_W, SSD_STATE), F32)],
        compiler_params=_cp("arbitrary"),
    )(xc, dt, par)


def _ssd_scan_bwd(xc, dt, par, st, dy, name, comm=None):
    s = xc.shape[0]
    cpb = SSD_CPB
    nb = s // (cpb * BLK)
    comm = comm or _Comm()

    def chunk(x_ref, dt_ref, par_ref, st_ref, dy_ref, dx_ref, ddt_ref, dal_ref, dd_ref, dh_ref):
        dt = dt_ref[...]
        a_row = -jnp.exp(par_ref[0:1, :])
        d_row = par_ref[1:2, :]
        ri = lax.broadcasted_iota(jnp.int32, (BLK, BLK), 0)
        ci = lax.broadcasted_iota(jnp.int32, (BLK, BLK), 1)
        tril = ri >= ci
        cs = _nn(tril.astype(F32), dt * a_row, HI)
        cst, dtt = cs.T, dt.T
        last = cs[BLK - 1:BLK, :]
        tolast = jnp.exp(last - cs)
        wcol = tolast * dt
        ecs = jnp.exp(cs)
        elast = jnp.exp(last)
        dcs_col = jnp.zeros((BLK, BLK), F32)
        ddt_col = jnp.zeros((BLK, BLK), F32)
        dcs_row = jnp.zeros((BLK, BLK), F32)
        ddt_row = jnp.zeros((BLK, BLK), F32)
        dlast = jnp.zeros((1, BLK), F32)
        ddsk = jnp.zeros((1, BLK), F32)
        for g in (0, 1):
            bg32 = x_ref[:, 512 + g * BLK:512 + (g + 1) * BLK]
            cg32 = x_ref[:, 768 + g * BLK:768 + (g + 1) * BLK]
            bg, cg = bg32.astype(MXU), cg32.astype(MXU)
            gm = _nt(cg, bg)
            dgm = jnp.zeros((BLK, BLK), F32)
            dbg = jnp.zeros((BLK, BLK), F32)
            dcg = jnp.zeros((BLK, BLK), F32)
            for pp in (0, 1):
                pr = 2 * g + pp
                h0 = 2 * pr
                x2 = x_ref[:, pr * BLK:(pr + 1) * BLK]
                dy2 = dy_ref[:, pr * BLK:(pr + 1) * BLK]
                hprev = st_ref[0, pr * BLK:(pr + 1) * BLK, :]
                dhn = dh_ref[pr * BLK:(pr + 1) * BLK, :]
                x2m, dhnm = x2.astype(MXU), dhn.astype(MXU)
                zb = _nt(bg, dhnm)
                y0 = _nt(cg, hprev.astype(MXU))
                esel = _head_cols(ecs, h0)
                wsel = _head_cols(wcol, h0)
                dx2 = _head_cols(d_row, h0) * dy2 + wsel * zb
                r_off = dy2 * y0
                r_w = x2 * zb
                r_d = dy2 * x2
                r_h = dhn * hprev
                for hh in (0, 1):
                    h = h0 + hh
                    hmask = (ci < HEAD_DIM) if hh == 0 else (ci >= HEAD_DIM)
                    onl = (ci == h).astype(F32)
                    ons = (ri == h).astype(F32)
                    dym = jnp.where(hmask, dy2, 0.0).astype(MXU)
                    dt_r = dtt[h:h + 1, :]
                    lm = jnp.exp(jnp.where(tril, cs[:, h:h + 1] - cst[h:h + 1, :], NEG))
                    mm = gm * lm * dt_r
                    dx2 = dx2 + _tn(mm.astype(MXU), dym)
                    dm = _nt(dym, x2m)
                    t1 = dm * lm
                    dgm = dgm + t1 * dt_r
                    tt = t1 * gm
                    ddt_row = ddt_row + ons * jnp.sum(tt, axis=0, keepdims=True)
                    t = tt * dt_r
                    dcs_col = dcs_col + onl * jnp.sum(t, axis=1, keepdims=True)
                    dcs_row = dcs_row - ons * jnp.sum(t, axis=0, keepdims=True)
                    de = jnp.sum(jnp.where(hmask, r_off, 0.0), axis=1, keepdims=True)
                    dcs_col = dcs_col + onl * (ecs[:, h:h + 1] * de)
                    hrow = (ri < HEAD_DIM) if hh == 0 else (ri >= HEAD_DIM)
                    dl_h = elast[:, h:h + 1] * jnp.sum(jnp.where(hrow, r_h, 0.0), keepdims=True)
                    dw = jnp.sum(jnp.where(hmask, r_w, 0.0), axis=1, keepdims=True)
                    ddt_col = ddt_col + onl * (dw * tolast[:, h:h + 1])
                    v = dw * wcol[:, h:h + 1]
                    dcs_col = dcs_col - onl * v
                    dl_h = dl_h + jnp.sum(v, keepdims=True)
                    dlast = dlast + onl[0:1, :] * dl_h
                    ddsk = ddsk + onl[0:1, :] * jnp.sum(jnp.where(hmask, r_d, 0.0), keepdims=True)
                dx_ref[:, pr * BLK:(pr + 1) * BLK] = dx2
                edy = (esel * dy2).astype(MXU)
                dcg = dcg + _nn(edy, hprev.astype(MXU))
                dec = jnp.where(ri < HEAD_DIM, elast[:, h0:h0 + 1], elast[:, h0 + 1:h0 + 2])
                dh_ref[pr * BLK:(pr + 1) * BLK, :] = dec * dhn + _tn(edy, cg)
                dbg = dbg + _nn((x2 * wsel).astype(MXU), dhnm)
            dgmm = dgm.astype(MXU)
            dx_ref[:, 512 + g * BLK:512 + (g + 1) * BLK] = dbg + _tn(dgmm, cg)
            dx_ref[:, 768 + g * BLK:768 + (g + 1) * BLK] = dcg + _nn(dgmm, bg)
        dcs = dcs_col + dcs_row.T + jnp.where(ri == BLK - 1, dlast, 0.0)
        dda = _nn((ri <= ci).astype(F32), dcs, HI)
        ddt_ref[...] = ddt_col + ddt_row.T + a_row * dda
        da = jnp.sum(dt * dda, axis=0, keepdims=True)
        dal_ref[0:1, :] += da * a_row
        dd_ref[0:1, :] += ddsk

    def body(*refs):
        (x_ref, dt_ref, par_ref, st_ref, dy_ref, dx_ref, ddt_ref, dal_ref, dd_ref, dh_ref), cm = comm.split(refs, 5, 4, 1)
        c = pl.program_id(0)
        comm.start_at(c == 0, cm)

        @pl.when(c == 0)
        def _():
            dh_ref[...] = jnp.zeros_like(dh_ref)
            dal_ref[...] = jnp.zeros_like(dal_ref)
            dd_ref[...] = jnp.zeros_like(dd_ref)

        for cc in reversed(range(cpb)):
            rows = pl.ds(cc * BLK, BLK)
            chunk(x_ref.at[rows], dt_ref.at[rows], par_ref, st_ref.at[pl.ds(cc, 1)], dy_ref.at[rows], dx_ref.at[rows],
                  ddt_ref.at[rows], dal_ref, dd_ref, dh_ref)
        comm.wait_at(c == nb - 1, cm)

    rev = lambda c: (nb - 1 - c, 0)
    tb = cpb * BLK
    res = pl.pallas_call(
        body, name=name, grid=(nb,),
        in_specs=[pl.BlockSpec((tb, 1024), rev), pl.BlockSpec((tb, BLK), rev), pl.BlockSpec((8, BLK), lambda c: (0, 0)),
                  pl.BlockSpec((cpb, SSD_W, SSD_STATE), lambda c: (nb - 1 - c, 0, 0)), pl.BlockSpec((tb, SSD_W), rev)]
        + [ANY] * comm.n,
        out_specs=[pl.BlockSpec((tb, 1024), rev), pl.BlockSpec((tb, BLK), rev),
                   pl.BlockSpec((8, BLK), lambda c: (0, 0)), pl.BlockSpec((8, BLK), lambda c: (0, 0))] + [ANY] * comm.n,
        out_shape=[jax.ShapeDtypeStruct((s, 1024), F32), jax.ShapeDtypeStruct((s, BLK), F32),
                   jax.ShapeDtypeStruct((8, BLK), F32), jax.ShapeDtypeStruct((8, BLK), F32)] + comm.out_shape(),
        scratch_shapes=[pltpu.VMEM((SSD_W, SSD_STATE), F32)] + comm.scratch(),
        compiler_params=_cp("arbitrary"),
    )(xc, dt, par, st, dy, *comm.args())
    return res[0], res[1], res[2], res[3], list(res[4:])


def _ssd_gate(y, z, w):
    t = y * _silu(z)
    outs = []
    for g in (0, 1):
        tg = t[:, g * 256:(g + 1) * 256]
        outs.append(tg * lax.rsqrt(jnp.mean(tg * tg, axis=-1, keepdims=True) + SSD_NORM_EPS))
    return jnp.concatenate(outs, axis=1) * w


def _ssd_post(y, proj, norm_w, name):
    def fn(rv, hv, cv):
        return [_ssd_gate(rv[0], rv[1], cv[0])], []
    return _rows(fn, [y, (proj, SSD_W, C_Z // SSD_W)], [norm_w.reshape(1, -1)], [(SSD_W, MXU)], tile=512, name=name)[0]


def _ssd_post_bwd(y, proj, norm_w, dout, name):
    def fn(rv, hv, cv):
        yb, zb, db = rv
        _, vjp = jax.vjp(lambda a, b: _ssd_gate(a, b, cv[0]), yb, zb)
        dy, dz = vjp(db)
        t = yb * _silu(zb)
        nrm = []
        for g in (0, 1):
            tg = t[:, g * 256:(g + 1) * 256]
            nrm.append(tg * lax.rsqrt(jnp.mean(tg * tg, axis=-1, keepdims=True) + SSD_NORM_EPS))
        return [dy, dz], [_colsum8(db * jnp.concatenate(nrm, axis=1))]
    return _rows(fn, [y, (proj, SSD_W, C_Z // SSD_W), dout], [norm_w.reshape(1, -1)],
                 [(SSD_W, F32), (SSD_W, MXU)], [(8, SSD_W)], tile=512, name=name)


LRU_T = 256


def _lru_conv(proj, conv_w, conv_b, name):
    def fn(rv, hv, cv):
        return [_conv(rv[0], hv[0], cv[0], cv[1])], []
    return _rows(fn, [(proj, LRU_W, C_XL // LRU_W)], [_pad8(conv_w), conv_b.reshape(1, -1)], [(LRU_W, F32)],
                 tile=512, name=name, halos=[(0, "prev")])[0]


def _lru_conv_bwd(proj, dxc, conv_w, name):
    def fn(rv, hv, cv):
        dx, dw, db = _conv_bwd(rv[0], hv[0], rv[1], hv[1], cv[0])
        return [dx], [dw, jnp.concatenate([db, jnp.zeros((7, db.shape[1]), F32)], axis=0)]
    return _rows(fn, [(proj, LRU_W, C_XL // LRU_W), dxc], [_pad8(conv_w)], [(LRU_W, MXU)], [(8, LRU_W), (8, LRU_W)],
                 tile=512, name=name, halos=[(0, "prev"), (1, "next")])


def _lru_au(pre_a, pre_x, xc, ba, bx, lam):
    r = _sigmoid(pre_a + ba)
    i = _sigmoid(pre_x + bx)
    log_a = -LRU_C * r * _softplus(-lam)
    a = jnp.exp(log_a)
    u = jnp.sqrt(1.0 - jnp.exp(2.0 * log_a)) * (i * xc)
    return a, u


def _lru_scan(pre, xc, proj, par, name):
    s = xc.shape[0]
    t = LRU_T

    def body(pre_ref, xc_ref, g_ref, par_ref, out_ref, h_ref, carry):
        c = pl.program_id(0)

        @pl.when(c == 0)
        def _():
            carry[...] = jnp.zeros_like(carry)

        a, u = _lru_au(pre_ref[:, :LRU_W], pre_ref[:, LRU_W:], xc_ref[...], par_ref[0:1, :], par_ref[1:2, :], par_ref[2:3, :])
        row = lax.broadcasted_iota(jnp.int32, (t, LRU_W), 0)
        sft = 1
        while sft < t:
            keep = row >= sft
            a_s = jnp.where(keep, pltpu.roll(a, sft, 0), 1.0)
            u_s = jnp.where(keep, pltpu.roll(u, sft, 0), 0.0)
            u = a * u_s + u
            a = a * a_s
            sft *= 2
        h = a * carry[0:1, :] + u
        h_ref[...] = h
        out_ref[...] = (h * _gelu(g_ref[...])).astype(out_ref.dtype)
        carry[0:1, :] = h[t - 1:t, :]

    return pl.pallas_call(
        body, name=name, grid=(s // t,),
        in_specs=[pl.BlockSpec((t, 2 * LRU_W), lambda c: (c, 0)), pl.BlockSpec((t, LRU_W), lambda c: (c, 0)),
                  pl.BlockSpec((t, LRU_W), lambda c: (c, C_G // LRU_W)), pl.BlockSpec((8, LRU_W), lambda c: (0, 0))],
        out_specs=[pl.BlockSpec((t, LRU_W), lambda c: (c, 0))] * 2,
        out_shape=[jax.ShapeDtypeStruct((s, LRU_W), MXU), jax.ShapeDtypeStruct((s, LRU_W), F32)],
        scratch_shapes=[pltpu.VMEM((8, LRU_W), F32)],
        compiler_params=_cp("arbitrary"),
    )(pre, xc, proj, par)


def _lru_scan_bwd(pre, xc, proj, par, h, dout, name):
    s = xc.shape[0]
    t = LRU_T
    n = s // t
    t8 = t // 8

    def body(pre_ref, xc_ref, g_ref, par_ref, h_ref, hh_ref, do_ref, dpre_ref, dxc_ref, dg_ref, dpar_ref, carry):
        c = pl.program_id(0)

        @pl.when(c == 0)
        def _():
            carry[...] = jnp.zeros_like(carry)
            dpar_ref[...] = jnp.zeros_like(dpar_ref)

        pa, px, xcb = pre_ref[:, :LRU_W], pre_ref[:, LRU_W:], xc_ref[...]
        ba, bx, lam = par_ref[0:1, :], par_ref[1:2, :], par_ref[2:3, :]
        (a, u), vjp = jax.vjp(_lru_au, pa, px, xcb, ba, bx, lam)
        g = g_ref[...]
        hcur = h_ref[...]
        do = do_ref[...]
        _, gvjp = jax.vjp(_gelu, g)
        dg_ref[...] = gvjp(do * hcur)[0].astype(dg_ref.dtype)
        row = lax.broadcasted_iota(jnp.int32, (t, LRU_W), 0)
        v = do * _gelu(g) + jnp.where(row == t - 1, carry[0:1, :], 0.0)
        b = jnp.where(row == t - 1, 0.0, pltpu.roll(a, t - 1, 0))
        sft = 1
        while sft < t:
            keep = row < t - sft
            b_s = jnp.where(keep, pltpu.roll(b, t - sft, 0), 1.0)
            v_s = jnp.where(keep, pltpu.roll(v, t - sft, 0), 0.0)
            v = b * v_s + v
            b = b * b_s
            sft *= 2
        dh = v
        carry[0:1, :] = a[0:1, :] * dh[0:1, :]
        hhalo = jnp.where(c == n - 1, 0.0, hh_ref[...])
        hprev = _shift_down(hcur, hhalo, 1)
        dpa, dpx, dxc, dba, dbx, dlam = vjp((dh * hprev, dh))
        dpre_ref[:, :LRU_W] = dpa
        dpre_ref[:, LRU_W:] = dpx
        dxc_ref[...] = dxc
        dpar_ref[0:1, :] += dba
        dpar_ref[1:2, :] += dbx
        dpar_ref[2:3, :] += dlam

    rev = lambda c: (n - 1 - c, 0)
    return pl.pallas_call(
        body, name=name, grid=(n,),
        in_specs=[pl.BlockSpec((t, 2 * LRU_W), rev), pl.BlockSpec((t, LRU_W), rev),
                  pl.BlockSpec((t, LRU_W), lambda c: (n - 1 - c, C_G // LRU_W)), pl.BlockSpec((8, LRU_W), lambda c: (0, 0)),
                  pl.BlockSpec((t, LRU_W), rev),
                  pl.BlockSpec((8, LRU_W), lambda c: (jnp.maximum((n - 1 - c) * t8 - 1, 0), 0)),
                  pl.BlockSpec((t, LRU_W), lambda c: (n - 1 - c, dout.shape[1] // LRU_W - 1))],
        out_specs=[pl.BlockSpec((t, 2 * LRU_W), rev), pl.BlockSpec((t, LRU_W), rev), pl.BlockSpec((t, LRU_W), rev),
                   pl.BlockSpec((8, LRU_W), lambda c: (0, 0))],
        out_shape=[jax.ShapeDtypeStruct((s, 2 * LRU_W), F32), jax.ShapeDtypeStruct((s, LRU_W), F32),
                   jax.ShapeDtypeStruct((s, LRU_W), MXU), jax.ShapeDtypeStruct((8, LRU_W), F32)],
        scratch_shapes=[pltpu.VMEM((8, LRU_W), F32)],
        compiler_params=_cp("arbitrary"),
    )(pre, xc, proj, par, h, h, dout)


def _swiglu_act(gu, name):
    def fn(rv, hv, cv):
        return [_silu(rv[0].astype(F32)) * rv[1].astype(F32)], []
    return _rows(fn, [(gu, D_FF, 0), (gu, D_FF, 1)], [], [(D_FF, MXU)], tile=256, name=name)[0]


def _swiglu_bwd(gu, da, name):
    def fn(rv, hv, cv):
        gt, up, dab = [t.astype(F32) for t in rv]
        sg = _sigmoid(gt)
        dgate = dab * up * (sg * (1.0 + gt * (1.0 - sg)))
        dup = dab * (gt * sg)
        return [jnp.concatenate([dgate, dup], axis=1)], []
    return _rows(fn, [(gu, D_FF, 0), (gu, D_FF, 1), da], [], [(2 * D_FF, MXU)], tile=256, name=name)[0]


def _loss_head(x, g, target, name):
    d = x.shape[1]

    def fn(rv, hv, cv):
        xb, tb = rv
        y, vjp = jax.vjp(_rms, xb, cv[0])
        err = y - tb
        dy = err * (1.0 / d)
        dx, _ = vjp(dy)
        rstd = lax.rsqrt(jnp.mean(xb * xb, axis=-1, keepdims=True) + NORM_EPS)
        e2 = err * err * (0.5 / d)
        e2 = functools.reduce(lambda a, b: a + b, [e2[:, k * BLK:(k + 1) * BLK] for k in range(d // BLK)])
        return [dx], [_colsum8(dy * xb * rstd), _colsum8(e2)]
    return _rows(fn, [x, target], [g.reshape(1, -1)], [(d, F32)], [(8, d), (8, BLK)], tile=512, name=name)


ANY = pl.BlockSpec(memory_space=pl.ANY)


def _coords():
    return lax.axis_index("x"), lax.axis_index("y"), lax.axis_index("c")


class _Comm:
    def __init__(self, gathers=(), scatters=()):
        self.gathers = list(gathers)
        self.scatters = list(scatters)
        self.n = len(self.gathers) + len(self.scatters)

    def args(self):
        return [g[0] for g in self.gathers] + self.scatters

    def out_shape(self):
        out = [jax.ShapeDtypeStruct((4,) + (a.shape if l is None else a.shape[1:]), a.dtype) for a, l, _ in self.gathers]
        return out + [jax.ShapeDtypeStruct((3,) + a.shape[1:], a.dtype) for a in self.scatters]

    def scratch(self):
        if not self.n:
            return []
        return [pltpu.SemaphoreType.DMA((3 * self.n,)), pltpu.SemaphoreType.DMA((3 * self.n,)),
                pltpu.SemaphoreType.DMA((max(len(self.gathers), 1),)),
                pltpu.SemaphoreType.DMA((3 * self.n,)), pltpu.SemaphoreType.DMA((3 * self.n,))]

    def split(self, refs, n_in, n_out, n_scratch):
        refs = list(refs)
        n = self.n
        own = refs[:n_in] + refs[n_in + n:n_in + n + n_out] + refs[n_in + 2 * n + n_out:n_in + 2 * n + n_out + n_scratch]
        cm = (refs[n_in:n_in + n], refs[n_in + n + n_out:n_in + 2 * n + n_out], refs[n_in + 2 * n + n_out + n_scratch:])
        return own, cm

    def _copies(self, cm, arriving):
        ins, outs, (send, recv, local, _, _) = cm
        x, y, c = _coords()
        me = 2 * x + y
        chips = [(1 - x, y), (x, 1 - y), (1 - x, 1 - y)]
        remote, locals_ = [], []
        ng = len(self.gathers)
        for i in range(self.n):
            if i < ng:
                _, l, halved = self.gathers[i]
                slab = ins[i] if l is None else ins[i].at[l]
                if not arriving:
                    locals_.append(pltpu.make_async_copy(slab, outs[i].at[me], local.at[i]))
            for j, (px, py) in enumerate(chips):
                if i < ng:
                    slot = 2 * px + py if arriving else me
                    src, dst = (slab.at[c], outs[i].at[slot, c]) if halved else (slab, outs[i].at[slot])
                else:
                    src, dst = ins[i].at[2 * px + py], outs[i].at[j]
                remote.append(pltpu.make_async_remote_copy(src, dst, send.at[3 * i + j], recv.at[3 * i + j],
                                                           device_id=(px, py, c), device_id_type=MESH))
        return remote, locals_

    def _handovers(self, cm, arriving):
        _, outs, (_, _, _, send, recv) = cm
        x, y, c = _coords()
        chips = [(1 - x, y), (x, 1 - y), (1 - x, 1 - y)]
        cps = []
        for i, (_, _, halved) in enumerate(self.gathers):
            if halved:
                for j, (px, py) in enumerate(chips):
                    src = outs[i].at[2 * px + py, c]
                    dst = outs[i].at[2 * px + py, 1 - c if arriving else c]
                    cps.append(pltpu.make_async_remote_copy(src, dst, send.at[3 * i + j], recv.at[3 * i + j],
                                                            device_id=(x, y, 1 - c), device_id_type=MESH))
        return cps

    def start_at(self, cond, cm):
        def go():
            remote, locals_ = self._copies(cm, False)
            for cp in locals_ + remote:
                cp.start()

        if self.n:
            go() if cond is True else pl.when(cond)(go)

    def wait_at(self, cond, cm):
        def go():
            for cp in self._copies(cm, True)[0]:
                cp.wait_recv()
            handed = self._handovers(cm, False)
            for cp in handed:
                cp.start()
            for cp in self._handovers(cm, True):
                cp.wait_recv()
            remote, locals_ = self._copies(cm, False)
            for cp in handed + remote:
                cp.wait_send()
            for cp in locals_:
                cp.wait()

        if self.n:
            go() if cond is True else pl.when(cond)(go)


def _comm_call(comm, name):
    def body(*refs):
        _, cm = comm.split(refs, 0, 0, 0)
        comm.start_at(True, cm)
        comm.wait_at(True, cm)

    return list(pl.pallas_call(
        body, name=name, in_specs=[ANY] * comm.n, out_specs=[ANY] * comm.n, out_shape=comm.out_shape(),
        scratch_shapes=comm.scratch(), compiler_params=pltpu.CompilerParams(has_side_effects=True),
    )(*comm.args()))


def _swap_sibling(arrs):
    n = len(arrs)

    def body(*refs):
        ins, outs, send, recv = refs[:n], refs[n:2 * n], refs[2 * n], refs[2 * n + 1]
        x, y, c = _coords()
        cps = [pltpu.make_async_remote_copy(ins[i], outs[i], send.at[i], recv.at[i], device_id=(x, y, 1 - c), device_id_type=MESH)
               for i in range(n)]
        for cp in cps:
            cp.start()
        for cp in cps:
            cp.wait_recv()
        for cp in cps:
            cp.wait_send()

    return list(pl.pallas_call(
        body, name="swap_sibling", in_specs=[ANY] * n, out_specs=[ANY] * n,
        out_shape=[jax.ShapeDtypeStruct(a.shape, a.dtype) for a in arrs],
        scratch_shapes=[pltpu.SemaphoreType.DMA((n,)), pltpu.SemaphoreType.DMA((n,))],
        compiler_params=pltpu.CompilerParams(has_side_effects=True),
    )(*arrs))


def _gather_small(gs):
    def body(g_ref, o_ref, send_sems, recv_sems, local_sem):
        x, y, c = _coords()
        me = 4 * x + 2 * y + c
        mine = pltpu.make_async_copy(g_ref, o_ref.at[me], local_sem)
        mine.start()
        sends = []
        for k in range(1, 8):
            px, py, pc = x ^ (k >> 2), y ^ ((k >> 1) & 1), c ^ (k & 1)
            sends.append((pltpu.make_async_remote_copy(g_ref, o_ref.at[me], send_sems.at[k - 1], recv_sems.at[k - 1],
                                                       device_id=(px, py, pc), device_id_type=MESH), 4 * px + 2 * py + pc, k))
        for cp, _, _ in sends:
            cp.start()
        for cp, src, k in sends:
            pltpu.make_async_remote_copy(g_ref, o_ref.at[src], send_sems.at[k - 1], recv_sems.at[k - 1],
                                         device_id=(x, y, c), device_id_type=MESH).wait_recv()
        for cp, _, _ in sends:
            cp.wait_send()
        mine.wait()

    return pl.pallas_call(
        body, name="gather_small", in_specs=[ANY], out_specs=ANY,
        out_shape=jax.ShapeDtypeStruct((8,) + gs.shape, gs.dtype),
        scratch_shapes=[pltpu.SemaphoreType.DMA((7,)), pltpu.SemaphoreType.DMA((7,)), pltpu.SemaphoreType.DMA],
        compiler_params=pltpu.CompilerParams(has_side_effects=True),
    )(gs)


def _sum_slots(own, others, name, tile):
    k, r, c = others.shape

    def body(*refs):
        if own is None:
            o_ref, out_ref = refs
            acc = o_ref[0].astype(F32)
            first = 1
        else:
            own_ref, o_ref, out_ref = refs
            acc = own_ref[...]
            first = 0
        for j in range(first, k):
            acc = acc + o_ref[j].astype(F32)
        out_ref[...] = acc

    row = pl.BlockSpec((tile, c), lambda i: (i, 0))
    specs = ([] if own is None else [row]) + [pl.BlockSpec((k, tile, c), lambda i: (0, i, 0))]
    args = ([] if own is None else [own]) + [others]
    return pl.pallas_call(body, name=name, grid=(r // tile,), in_specs=specs, out_specs=row,
                          out_shape=jax.ShapeDtypeStruct((r, c), F32), compiler_params=_cp("parallel"))(*args)


def _adamw(w, m, v, ga, gb, name, tile, rows_first=False):
    lead = 0 if rows_first else w.ndim - 2
    r, c = w.shape[-2:]

    def body(*refs):
        vals = [ref[0] if lead else ref[...] for ref in refs[:len(refs) - 4]]
        w_, m_, v_, g = vals[0], vals[1], vals[2], vals[3]
        if gb is not None:
            g = g + vals[4]
        nm = ADAM_B1 * m_ + (1.0 - ADAM_B1) * g
        nv = ADAM_B2 * v_ + (1.0 - ADAM_B2) * (g * g)
        d = -ADAM_LR * ((nm / BC1) / (jnp.sqrt(nv / BC2) + ADAM_EPS) + ADAM_WD * w_)
        for ref, val in zip(refs[len(refs) - 4:], (g, d, nm, nv)):
            if lead:
                ref[0] = val
            else:
                ref[...] = val

    if rows_first:
        row = pl.BlockSpec((tile,) + w.shape[1:], lambda i: (i, 0, 0))
        grid = (w.shape[0] // tile,)
    elif lead:
        row = pl.BlockSpec((1, tile, c), lambda l, i: (l, i, 0))
        grid = (w.shape[0], r // tile)
    else:
        row = pl.BlockSpec((tile, c), lambda i: (i, 0))
        grid = (r // tile,)
    args = [w, m, v, ga] + ([] if gb is None else [gb])
    return pl.pallas_call(body, name=name, grid=grid, in_specs=[row] * len(args), out_specs=[row] * 4,
                          out_shape=[jax.ShapeDtypeStruct(w.shape, F32)] * 4,
                          compiler_params=_cp(*(["parallel"] * len(grid))))(*args)


MATS = ("w_in", "w_out", "w_gate", "w_up", "w_down")
CONVS = ("ssd_conv_w", "lru_conv_w")
BIG = MATS + CONVS
TRANSPOSED = ("w_gate", "w_up")
COL_SHARDED = ("ssd_conv_w", "lru_conv_w")
W_IN_SHARD = IN_COLS // 4
W_IN_PAD = 1056
SMALL = ("norm_mix", "ssd_conv_b", "ssd_dt_bias", "ssd_a_log", "ssd_d", "ssd_norm", "lru_conv_b", "lru_wa", "lru_ba",
         "lru_wx", "lru_bx", "lru_lambda", "norm_ffn", "norm_final")
WEIGHTS = ("norm_mix", "w_in", "ssd_conv_w", "ssd_conv_b", "ssd_dt_bias", "ssd_a_log", "ssd_d", "ssd_norm", "lru_conv_w",
           "lru_conv_b", "lru_wa", "lru_ba", "lru_wx", "lru_bx", "lru_lambda", "w_out", "norm_ffn", "w_gate", "w_up",
           "w_down", "norm_final")
ROW_TILE = {"w_in": W_IN_SHARD, "w_out": 128, "w_gate": 352, "w_up": 352, "w_down": 352}
W_IN_ADAM_TILE = 54


def _pack(arrs, width, row_mult, dtype):
    flat = jnp.concatenate([a.reshape(-1).astype(dtype) for a in arrs])
    rows = -(-flat.shape[0] // width)
    rows = -(-rows // row_mult) * row_mult
    flat = jnp.pad(flat, (0, rows * width - flat.shape[0]))
    return flat.reshape(rows, width)


def _unpack(buf, shapes):
    flat = buf.reshape(-1)
    out, off = [], 0
    for shp in shapes:
        n = int(np.prod(shp))
        out.append(flat[off:off + n].reshape(shp))
        off += n
    return out


def _join(name, g4):
    if name in COL_SHARDED:
        return jnp.moveaxis(g4, 0, -2).reshape(g4.shape[1:-1] + (4 * g4.shape[-1],))
    return g4.reshape((4 * g4.shape[1],) + g4.shape[2:])


def _slabs(name, g):
    if name in COL_SHARDED:
        return jnp.moveaxis(g.reshape(g.shape[:-1] + (4, g.shape[-1] // 4)), -2, 0)
    return g.reshape((4, g.shape[0] // 4) + g.shape[1:])


def _w_in_rows(g4):
    def nat(lo, hi):
        out = []
        while lo < hi:
            j = lo // W_IN_SHARD
            stop = min(hi, (j + 1) * W_IN_SHARD)
            out.append((j, lo - j * W_IN_SHARD, stop - lo))
            lo = stop
        return out
    pieces = nat(0, 3072) + nat(3080, IN_COLS) + nat(3072, 3080)

    def body(g_ref, o_ref):
        row = 0
        for j, first, n in pieces:
            o_ref[row:row + n, :] = g_ref[j, first:first + n, :]
            row += n
        o_ref[row:, :] = jnp.zeros((NP - row, o_ref.shape[1]), o_ref.dtype)

    return pl.pallas_call(body, name="w_in_rows", out_shape=jax.ShapeDtypeStruct((NP, g4.shape[-1]), g4.dtype),
                          compiler_params=pltpu.CompilerParams(vmem_limit_bytes=VMEM_LIMIT))(g4)


def _w_in_slabs(gt):
    def kern(n):
        return n if n < 3072 else (C_DT + n - 3072 if n < 3080 else n - 8)
    slabs = []
    for j in range(4):
        lo, hi = j * W_IN_SHARD, (j + 1) * W_IN_SHARD
        cuts = sorted({lo, hi} | {c for c in (3072, 3080) if lo < c < hi})
        slabs.append(jnp.concatenate([gt[kern(a):kern(a) + b - a] for a, b in zip(cuts[:-1], cuts[1:])], axis=0))
    return jnp.stack(slabs, axis=0)


def _block_diag(w):
    eye = jnp.eye(LRU_BLOCKS, dtype=w.dtype)
    return jnp.einsum("ncd,nm->ncmd", w, eye).reshape(LRU_W, LRU_W)


def _block_diag_extract(g):
    g4 = g.reshape(LRU_BLOCKS, 64, LRU_BLOCKS, 64)
    return jnp.stack([g4[n, :, n, :] for n in range(LRU_BLOCKS)], axis=0)


def _lanes128(v):
    return jnp.pad(v, (0, BLK - v.shape[0])).reshape(1, BLK)


def _layer_mixers(x, p, comm=None, h=None):
    if h is None:
        h = _rms_fwd(x, p["norm_mix"], "rms_mix")
    proj = _mm(h, p["w_in_t"], tb=True, tm=1024, tn=1408, tk=1024, name="mm_in")
    att, lse, attb, got = _att_fwd_fused(proj, "att_fwd", comm)
    xconv, dt = _ssd_pre(proj, p["ssd_conv_w"], p["ssd_conv_b"], _lanes128(p["ssd_dt_bias"]), "ssd_pre")
    spar = jnp.concatenate([_lanes128(p["ssd_a_log"]), _lanes128(p["ssd_d"]), jnp.zeros((6, BLK), F32)], axis=0)
    y, states = _ssd_scan(xconv, dt, spar, "ssd_scan")
    ssd = _ssd_post(y, proj, p["ssd_norm"], "ssd_post")
    xc = _lru_conv(proj, p["lru_conv_w"], p["lru_conv_b"], "lru_conv")
    wab = jnp.concatenate([_block_diag(p["lru_wa"]), _block_diag(p["lru_wx"])], axis=1).astype(MXU)
    pre = _mm(xc, wab, tm=1024, tn=1024, tk=512, name="mm_lru")
    lpar = jnp.concatenate([p["lru_ba"].reshape(1, -1), p["lru_bx"].reshape(1, -1), p["lru_lambda"].reshape(1, -1),
                            jnp.zeros((5, LRU_W), F32)], axis=0)
    lru, hs = _lru_scan(pre, xc, proj, lpar, "lru_scan")
    mix = jnp.concatenate([attb, ssd, lru], axis=1)
    saved = dict(x=x, h=h, proj=proj, att=att, lse=lse, xconv=xconv, dt=dt, spar=spar, y=y, states=states, xc=xc, wab=wab,
                 pre=pre, lpar=lpar, hs=hs, mix=mix)
    return mix, saved, got


def _layer_ffn(x, mix, p, saved, comm=None, next_norm=None):
    x1, h2 = _mm(mix, p["w_out"], add=x, tm=1024, tn=1024, tk=1536, name="mm_out", epi=_epi_rms(p["norm_ffn"]))
    gu = _mm(h2, p["w_gu_t"], tb=True, out_dtype=MXU, tm=1024, tn=1408, tk=1024, name="mm_gu", comm=comm)
    gu, got = gu if comm is not None else (gu, [])
    act = _swiglu_act(gu, "swiglu_act")
    x2 = _mm(act, p["w_down"], add=x1, tm=1024, tn=1024, tk=2816, name="mm_down",
             epi=None if next_norm is None else _epi_rms(next_norm))
    x2, h_next = x2 if next_norm is not None else (x2, None)
    saved.update(x1=x1, h2=h2, gu=gu, act=act)
    return x2, got, h_next


def _layer_bwd(dx2, p, sv, comm_ssd=None, comm_att=None, comm_tail=None):
    g = {}
    da = _mm(dx2, p["w_down"], tb=True, out_dtype=MXU, tm=1024, tn=1408, tk=1024, name="mm_d_act")
    g["w_down"] = _mm(sv["act"], dx2, ta=True, tm=1408, tn=1024, tk=1024, name="mm_g_down")
    dgu = _swiglu_bwd(sv["gu"], da, "swiglu_bwd")
    dx1, gn = _mm(dgu, p["w_gu_t"], tm=1024, tn=1024, tk=1408, name="mm_d_h2", epi=_epi_rms_bwd(sv["x1"], p["norm_ffn"], dx2))
    g["w_gu_t"] = _mm(dgu, sv["h2"], ta=True, tm=1408, tn=1024, tk=1024, name="mm_g_gu")
    g["norm_ffn"] = jnp.sum(gn, axis=0)
    dmix, delta = _mm(dx1, p["w_out"], tb=True, tm=1024, tn=1536, tk=1024, name="mm_d_mix", epi=_epi_att_delta(sv["att"]))
    g["w_out"] = _mm(sv["mix"], dx1, ta=True, tm=1536, tn=1024, tk=1024, name="mm_g_out")
    proj = sv["proj"]
    dpre, dxc_u, dgl, dlpar = _lru_scan_bwd(sv["pre"], sv["xc"], proj, sv["lpar"], sv["hs"], dmix, "lru_scan_bwd")
    dxc = _mm(dpre, sv["wab"], tb=True, add=dxc_u, tm=1024, tn=512, tk=1024, name="mm_d_xc")
    gwab = _mm(sv["xc"], dpre, ta=True, tm=512, tn=1024, tk=1024, name="mm_g_lru")
    g["lru_wa"], g["lru_wx"] = _block_diag_extract(gwab[:, :LRU_W]), _block_diag_extract(gwab[:, LRU_W:])
    g["lru_ba"], g["lru_bx"], g["lru_lambda"] = dlpar[0], dlpar[1], dlpar[2]
    dxl, gcw, gcb = _lru_conv_bwd(proj, dxc, p["lru_conv_w"], "lru_conv_bwd")
    g["lru_conv_w"], g["lru_conv_b"] = gcw[:CONV_K], jnp.sum(gcb, axis=0)
    dy, dz, gsn = _ssd_post_bwd(sv["y"], proj, p["ssd_norm"], (dmix, SSD_W, 1), "ssd_post_bwd")
    g["ssd_norm"] = jnp.sum(gsn, axis=0)
    dxconv, ddt, dal, ddk, got_ssd = _ssd_scan_bwd(sv["xconv"], sv["dt"], sv["spar"], sv["states"], dy, "ssd_scan_bwd", comm_ssd)
    g["ssd_a_log"], g["ssd_d"] = dal[0, :8], ddk[0, :8]
    dxbc, ddtr, gsw, gsb, gdb = _ssd_pre_bwd(proj, dxconv, ddt, p["ssd_conv_w"], p["ssd_conv_b"],
                                             _lanes128(p["ssd_dt_bias"]), "ssd_pre_bwd")
    g["ssd_conv_w"], g["ssd_conv_b"], g["ssd_dt_bias"] = gsw[:CONV_K], jnp.sum(gsb, axis=0), jnp.sum(gdb, axis=0)[:8]
    dq, dk, dv, got_att = _att_bwd_rev(proj, dmix, sv["lse"], delta, "att_bwd", None if comm_att is None else comm_att(g))
    dproj = jnp.concatenate([dq, dk, dv, dz, dxbc, dgl, dxl, ddtr], axis=1)
    g["w_in_t"] = _mm(dproj, sv["h"], ta=True, tm=1408, tn=1024, tk=1024, name="mm_g_in")
    res = _mm(dproj, p["w_in_t"], tm=1024, tn=1024, tk=1408, name="mm_d_h", comm=None if comm_tail is None else comm_tail(g),
              epi=_epi_rms_bwd(sv["x"], p["norm_mix"], dx1))
    (dx, gm), got_tail = res if comm_tail is not None else (res, [])
    g["norm_mix"] = jnp.sum(gm, axis=0)
    return dx, g, got_ssd, got_att, got_tail


def _grad_slabs(g, names):
    out = {}
    for n in names:
        if n == "w_in":
            out[n] = _w_in_slabs(g["w_in_t"])
        elif n == "w_gate":
            out[n] = _slabs(n, g["w_gu_t"][:D_FF])
        elif n == "w_up":
            out[n] = _slabs(n, g["w_gu_t"][D_FF:])
        else:
            out[n] = _slabs(n, g[n])
    return out


def kernel(x, norm_mix, w_in, ssd_conv_w, ssd_conv_b, ssd_dt_bias, ssd_a_log, ssd_d, ssd_norm, lru_conv_w, lru_conv_b, lru_wa, lru_ba, lru_wx, lru_bx, lru_lambda, w_out, norm_ffn, w_gate, w_up, w_down, norm_final, loss_target, m_norm_mix, m_w_in, m_ssd_conv_w, m_ssd_conv_b, m_ssd_dt_bias, m_ssd_a_log, m_ssd_d, m_ssd_norm, m_lru_conv_w, m_lru_conv_b, m_lru_wa, m_lru_ba, m_lru_wx, m_lru_bx, m_lru_lambda, m_w_out, m_norm_ffn, m_w_gate, m_w_up, m_w_down, m_norm_final, v_norm_mix, v_w_in, v_ssd_conv_w, v_ssd_conv_b, v_ssd_dt_bias, v_ssd_a_log, v_ssd_d, v_ssd_norm, v_lru_conv_w, v_lru_conv_b, v_lru_wa, v_lru_ba, v_lru_wx, v_lru_bx, v_lru_lambda, v_w_out, v_norm_ffn, v_w_gate, v_w_up, v_w_down, v_norm_final):
    loc = dict(locals())
    w = {n: loc[n] for n in WEIGHTS}
    m = {n: loc["m_" + n] for n in WEIGHTS}
    v = {n: loc["v_" + n] for n in WEIGHTS}
    for n in TRANSPOSED:
        w[n], m[n], v[n] = [jnp.transpose(t, (0, 2, 1)) for t in (w[n], m[n], v[n])]
    wt_in, mt_in, vt_in = [jnp.transpose(t, (2, 0, 1)) for t in (w["w_in"], m["w_in"], v["w_in"])]

    def halves(a):
        return a.reshape(a.shape[0], 2, a.shape[1] // 2, a.shape[2])

    def unhalve(a):
        return a.reshape(4, 2 * a.shape[2], a.shape[3])

    def joined(name, a):
        return _w_in_rows(unhalve(a)) if name == "w_in" else _join(name, unhalve(a))

    wb = {n: halves(w[n].astype(MXU)) for n in MATS[1:]}
    wb["w_in"] = halves(jnp.pad(jnp.transpose(wt_in.astype(MXU), (1, 0, 2)), ((0, 0), (0, W_IN_PAD - W_IN_SHARD), (0, 0))))
    xs = x[0]
    h0, first = _rms_fwd(xs, norm_mix[0], "rms_mix", _Comm(gathers=[(wb["w_in"], 0, True), (w["ssd_conv_w"], None, False),
                                                                    (w["lru_conv_w"], None, False)]))
    convs = {"ssd_conv_w": _join("ssd_conv_w", first[1]), "lru_conv_w": _join("lru_conv_w", first[2])}
    behind_att = [(n, 0) for n in MATS[1:]] + [("w_in", 1)]
    behind_ffn = [(n, 1) for n in MATS[1:]]
    whole = {("w_in", 0): joined("w_in", first[0])}
    params = {}

    def layer_params(l):
        if l not in params:
            p = {n: w[n][l] for n in SMALL if n != "norm_final"}
            p.update(w_in_t=whole["w_in", l], ssd_conv_w=convs["ssd_conv_w"][l], lru_conv_w=convs["lru_conv_w"][l])
            params[l] = p
        if "w_out" not in params[l] and ("w_out", l) in whole:
            params[l].update(w_out=whole["w_out", l], w_down=whole["w_down", l],
                             w_gu_t=jnp.concatenate([whole["w_gate", l], whole["w_up", l]], axis=0))
        return params[l]

    saved = []
    h_in = h0
    for l in range(DEPTH):
        first_layer = l == 0
        mix, sv, got = _layer_mixers(xs, layer_params(l), _Comm(gathers=[(wb[n], k, True) for n, k in behind_att]) if first_layer else None,
                                     h_in)
        whole.update({k: joined(k[0], a) for k, a in zip(behind_att, got)})
        xs, got, h_in = _layer_ffn(xs, mix, layer_params(l), sv, _Comm(gathers=[(wb[n], k, True) for n, k in behind_ffn]) if first_layer else None,
                                   norm_mix[l + 1] if l + 1 < DEPTH else None)
        whole.update({k: joined(k[0], a) for k, a in zip(behind_ffn, got)})
        saved.append(sv)
    dx, gnf, lsum = _loss_head(xs, norm_final, loss_target[0], "loss_head")
    loss = lax.psum(jnp.sum(lsum), ("x", "y", "c"))

    dx, g1, _, _, _ = _layer_bwd(dx, layer_params(1), saved[1])
    s1 = _grad_slabs(g1, BIG)
    att0 = ("w_gate", "w_up", "w_down", "w_out")
    s0 = {}

    def wire(s, n):
        return s[n].astype(MXU) if n in MATS else s[n]

    def comm_att(g0):
        s0.update(_grad_slabs(g0, att0))
        return _Comm(scatters=[wire(s0, n) for n in att0])

    tail0 = ("w_in",) + CONVS

    def comm_tail(g0):
        s0.update(_grad_slabs(g0, tail0))
        return _Comm(scatters=[wire(s0, n) for n in tail0])

    dx, g0, got_ssd, got_att, got_tail = _layer_bwd(dx, layer_params(0), saved[0], _Comm(scatters=[wire(s1, n) for n in BIG]),
                                                    comm_att, comm_tail)
    recv = {(n, 1): a for n, a in zip(BIG, got_ssd)}
    recv.update({(n, 0): a for n, a in zip(att0, got_att)})
    recv.update({(n, 0): a for n, a in zip(tail0, got_tail)})

    me = 2 * lax.axis_index("x") + lax.axis_index("y")
    slabs = (s0, s1)
    part = {}
    for n in BIG:
        per_layer = []
        for l in range(DEPTH):
            own = lax.dynamic_index_in_dim(slabs[l][n], me, axis=0, keepdims=False)
            per_layer.append(_sum_slots(own, recv[n, l], "sum_chips_" + n, ROW_TILE.get(n, own.shape[0])))
        part[n] = jnp.stack(per_layer, axis=0)
    sib = dict(zip(BIG, _swap_sibling([part[n] for n in BIG])))
    out_g, out_d, out_m, out_v = {}, {}, {}, {}
    for n in BIG:
        if n == "w_in":
            res = _adamw(wt_in, mt_in, vt_in, jnp.transpose(part[n], (1, 0, 2)), jnp.transpose(sib[n], (1, 0, 2)), "adamw_" + n,
                         W_IN_ADAM_TILE, rows_first=True)
            out_g[n], out_d[n], out_m[n], out_v[n] = [jnp.transpose(t, (1, 2, 0)) for t in res]
            continue
        res = _adamw(w[n], m[n], v[n], part[n], sib[n], "adamw_" + n, ROW_TILE.get(n, w[n].shape[1]))
        out_g[n], out_d[n], out_m[n], out_v[n] = [jnp.transpose(t, (0, 2, 1)) for t in res] if n in TRANSPOSED else res

    gsm = {n: jnp.stack([g0[n], g1[n]], axis=0) for n in SMALL if n != "norm_final"}
    gsm["norm_final"] = jnp.sum(gnf, axis=0)
    small_shapes = [w[n].shape for n in SMALL]
    gs = _pack([gsm[n].reshape(w[n].shape) for n in SMALL], BLK, 8, F32)
    gall = _gather_small(gs)
    gsum = _sum_slots(None, gall, "sum_devices", gs.shape[0])
    ws = _pack([w[n] for n in SMALL], BLK, 8, F32)
    ms = _pack([m[n] for n in SMALL], BLK, 8, F32)
    vs = _pack([v[n] for n in SMALL], BLK, 8, F32)
    gsr, dsr, nms, nvs = _adamw(ws, ms, vs, gsum, None, "adamw_small", gs.shape[0])
    out_g.update(zip(SMALL, _unpack(gsr, small_shapes)))
    out_d.update(zip(SMALL, _unpack(dsr, small_shapes)))
    out_m.update(zip(SMALL, _unpack(nms, small_shapes)))
    out_v.update(zip(SMALL, _unpack(nvs, small_shapes)))

    return (loss, dx[None], *[out_g[n] for n in WEIGHTS], *[out_d[n] for n in WEIGHTS],
            *[out_m[n] for n in WEIGHTS], *[out_v[n] for n in WEIGHTS])
```

```python
import functools
import math

import jax
import jax.numpy as jnp
import numpy as np
from jax import lax
from jax.experimental import pallas as pl
from jax.experimental.pallas import tpu as pltpu

F32 = jnp.float32
MXU = jnp.bfloat16
HI = lax.Precision.HIGHEST
HIGH = lax.Precision.HIGH
MESH = pl.DeviceIdType.MESH

D_MODEL = 1024
DEPTH = 2
HEAD_DIM = 64
ATT_W = 512
ATT_PATTERNS = ((128, 1), (512, 4), (2048, 16))
BLK = 128
SSD_W = 512
SSD_STATE = 128
LRU_W = 512
LRU_BLOCKS = 8
LRU_C = 8.0
CONV_K = 4
D_MIX = 1536
D_FF = 2816
IN_COLS = 4104
NP = 4224
NORM_EPS = 1e-6
SSD_NORM_EPS = 1e-5
LN2 = math.log(2.0)
NEG = -1e30

ADAM_LR, ADAM_B1, ADAM_B2, ADAM_EPS, ADAM_WD, ADAM_STEP = 0.001, 0.9, 0.999, 1e-08, 0.01, 10
BC1 = 1.0 - ADAM_B1 ** ADAM_STEP
BC2 = 1.0 - ADAM_B2 ** ADAM_STEP

VMEM_LIMIT = 56 * 1024 * 1024

C_Q, C_K, C_V, C_Z, C_XBC, C_G, C_XL, C_DT = 0, 512, 1024, 1536, 2048, 3072, 3584, 4096


def _cp(*sem):
    return pltpu.CompilerParams(dimension_semantics=sem, vmem_limit_bytes=VMEM_LIMIT)


def _dot(a, b, dims, prec=None):
    return lax.dot_general(a, b, (dims, ((), ())), preferred_element_type=F32, precision=prec)


def _nn(a, b, prec=None):
    return _dot(a, b, ((1,), (0,)), prec)


def _nt(a, b, prec=None):
    return _dot(a, b, ((1,), (1,)), prec)


def _tn(a, b, prec=None):
    return _dot(a, b, ((0,), (0,)), prec)


def _sigmoid(x):
    return jax.nn.sigmoid(x)


def _silu(x):
    return x * _sigmoid(x)


def _softplus(x):
    return jnp.maximum(x, 0.0) + jnp.log(1.0 + jnp.exp(-jnp.abs(x)))


def _gelu(x):
    return 0.5 * x * (1.0 + jnp.tanh(0.7978845608028654 * (x + 0.044715 * x * x * x)))


def _mm(a, b, *, ta=False, tb=False, add=None, out_dtype=F32, tm, tn, tk, name, comm=None, epi=None):
    m, k = (a.shape[1], a.shape[0]) if ta else a.shape
    n = b.shape[0] if tb else b.shape[1]
    assert (b.shape[1] if tb else b.shape[0]) == k
    assert m % tm == 0 and n % tn == 0 and k % tk == 0, (name, m, n, k)
    nk = k // tk
    a_spec = pl.BlockSpec((tk, tm), lambda i, j, kk: (kk, i)) if ta else pl.BlockSpec((tm, tk), lambda i, j, kk: (i, kk))
    b_spec = pl.BlockSpec((tn, tk), lambda i, j, kk: (j, kk)) if tb else pl.BlockSpec((tk, tn), lambda i, j, kk: (kk, j))
    o_spec = pl.BlockSpec((tm, tn), lambda i, j, kk: (i, j))
    dims = ((0 if ta else 1,), (1 if tb else 0,))
    carried = comm is not None
    comm = comm or _Comm()
    ni, nj = m // tm, n // tn
    efn, erows, econsts, eouts, eaccs = epi or (None, [], [], [], [])
    assert epi is None or nj == 1
    nadd = 0 if add is None else 1
    ner, nec, neo, nea = len(erows), len(econsts), len(eouts), len(eaccs)

    def body(*refs):
        refs, cm = comm.split(refs, 2 + nadd + ner + nec, 1 + neo + nea, 1)
        a_ref, b_ref = refs[:2]
        er_refs = refs[2 + nadd:2 + nadd + ner]
        ec_refs = refs[2 + nadd + ner:2 + nadd + ner + nec]
        o_ref = refs[2 + nadd + ner + nec]
        eo_refs = refs[3 + nadd + ner + nec:3 + nadd + ner + nec + neo]
        ea_refs = refs[3 + nadd + ner + nec + neo:3 + nadd + ner + nec + neo + nea]
        acc = refs[-1]
        i, j, kk = pl.program_id(0), pl.program_id(1), pl.program_id(2)
        comm.start_at((i == 0) & (j == 0) & (kk == 0), cm)

        @pl.when(kk == 0)
        def _():
            acc[...] = jnp.zeros_like(acc)

        acc[...] += _dot(a_ref[...].astype(MXU), b_ref[...].astype(MXU), dims)

        @pl.when(kk == nk - 1)
        def _():
            r = acc[...]
            if add is not None:
                r = r + refs[2][...]
            if efn is None:
                o_ref[...] = r.astype(out_dtype)
            else:
                main, extra, sums = efn(r, [t[...] for t in er_refs], [t[...] for t in ec_refs])
                o_ref[...] = main.astype(out_dtype)
                for t, val in zip(eo_refs, extra):
                    t[...] = val.astype(t.dtype)
                @pl.when(i == 0)
                def _():
                    for t, val in zip(ea_refs, sums):
                        t[...] = val

                @pl.when(i > 0)
                def _():
                    for t, val in zip(ea_refs, sums):
                        t[...] += val

        comm.wait_at((i == ni - 1) & (j == nj - 1) & (kk == nk - 1), cm)

    def whole_rows(width):
        return pl.BlockSpec((tm, width), lambda i, j, kk: (i, 0))

    ins = [a, b] + ([] if add is None else [add]) + list(erows) + list(econsts)
    specs = [a_spec, b_spec] + ([] if add is None else [o_spec]) + [whole_rows(t.shape[1]) for t in erows]
    specs += [pl.BlockSpec(t.shape, lambda i, j, kk: (0, 0)) for t in econsts]
    out_specs = [o_spec] + [whole_rows(wd) for wd, _ in eouts] + [pl.BlockSpec((r, wd), lambda i, j, kk: (0, 0)) for r, wd in eaccs]
    out_shape = [jax.ShapeDtypeStruct((m, n), out_dtype)] + [jax.ShapeDtypeStruct((m, wd), dt) for wd, dt in eouts]
    out_shape += [jax.ShapeDtypeStruct((r, wd), F32) for r, wd in eaccs]
    serial = comm.n or nea
    res = pl.pallas_call(
        body, name=name, grid=(ni, nj, nk), in_specs=specs + [ANY] * comm.n, out_specs=out_specs + [ANY] * comm.n,
        out_shape=out_shape + comm.out_shape(),
        scratch_shapes=[pltpu.VMEM((tm, tn), F32)] + comm.scratch(),
        compiler_params=_cp(*((["arbitrary"] * 3) if serial else ["parallel", "parallel", "arbitrary"])),
    )(*ins, *comm.args())
    nown = 1 + neo + nea
    own = res[0] if epi is None else list(res[:nown])
    return (own, list(res[nown:])) if carried else own


def _rows(fn, rows, consts=(), outs=(), accs=(), *, tile, name, halos=(), comm=None):
    rows = [r if isinstance(r, tuple) else (r, r.shape[1], 0) for r in rows]
    s = rows[0][0].shape[0]
    assert s % tile == 0 and tile % 8 == 0
    n = s // tile
    t8 = tile // 8
    nr, nh, nc_, no, na = len(rows), len(halos), len(consts), len(outs), len(accs)
    carried = comm is not None
    comm = comm or _Comm()

    def body(*refs):
        refs, cm = comm.split(refs, nr + nh + nc_, no + na, 0)
        i = pl.program_id(0)
        comm.start_at(i == 0, cm)
        rv = [r[...] for r in refs[:nr]]
        hv = []
        for (idx, kind), r in zip(halos, refs[nr:nr + nh]):
            edge = (i == 0) if kind == "prev" else (i == n - 1)
            hv.append(jnp.where(edge, 0.0, r[...]))
        cv = [r[...] for r in refs[nr + nh:nr + nh + nc_]]
        o_refs = refs[nr + nh + nc_:nr + nh + nc_ + no]
        a_refs = refs[nr + nh + nc_ + no:]
        ov, av = fn(rv, hv, cv)
        for r, v in zip(o_refs, ov):
            r[...] = v.astype(r.dtype)
        if na:
            @pl.when(i == 0)
            def _():
                for r in a_refs:
                    r[...] = jnp.zeros_like(r)
            for r, v in zip(a_refs, av):
                r[...] += v
        comm.wait_at(i == n - 1, cm)

    in_specs = [pl.BlockSpec((tile, w), functools.partial(lambda i, cb: (i, cb), cb=cb)) for (_, w, cb) in rows]
    for idx, kind in halos:
        _, w, cb = rows[idx]
        if kind == "prev":
            in_specs.append(pl.BlockSpec((8, w), functools.partial(lambda i, cb: (jnp.maximum(i * t8 - 1, 0), cb), cb=cb)))
        else:
            in_specs.append(pl.BlockSpec((8, w), functools.partial(lambda i, cb: (jnp.minimum((i + 1) * t8, n * t8 - 1), cb), cb=cb)))
    in_specs += [pl.BlockSpec(c.shape, functools.partial(lambda i, nd: (0,) * nd, nd=c.ndim)) for c in consts]
    out_specs = [pl.BlockSpec((tile, c), lambda i: (i, 0)) for (c, _) in outs]
    out_specs += [pl.BlockSpec((r, c), lambda i: (0, 0)) for (r, c) in accs]
    out_shape = [jax.ShapeDtypeStruct((s, c), dt) for (c, dt) in outs]
    out_shape += [jax.ShapeDtypeStruct((r, c), F32) for (r, c) in accs]
    args = [r[0] for r in rows] + [rows[idx][0] for idx, _ in halos] + list(consts)
    res = pl.pallas_call(
        body, name=name, grid=(n,), in_specs=in_specs + [ANY] * comm.n, out_specs=out_specs + [ANY] * comm.n,
        out_shape=out_shape + comm.out_shape(), scratch_shapes=comm.scratch(), compiler_params=_cp("arbitrary"),
    )(*args, *comm.args())
    return (list(res[:no + na]), list(res[no + na:])) if carried else list(res)


def _colsum8(v):
    t, c = v.shape
    return jnp.sum(v.reshape(t // 8, 8, c), axis=0)


def _rms(x, g):
    return x * lax.rsqrt(jnp.mean(x * x, axis=-1, keepdims=True) + NORM_EPS) * g


def _epi_rms(g):
    return (lambda r, rows, consts: (r, [_rms(r, consts[0])], []), [], [g.reshape(1, -1)], [(g.shape[-1], MXU)], [])


def _epi_rms_bwd(x, g, dres):
    def fn(r, rows, consts):
        xb, drb = rows
        _, vjp = jax.vjp(_rms, xb, consts[0])
        rstd = lax.rsqrt(jnp.mean(xb * xb, axis=-1, keepdims=True) + NORM_EPS)
        return drb + vjp(r)[0], [], [_colsum8(r * xb * rstd)]
    return (fn, [x, dres], [g.reshape(1, -1)], [], [(8, g.shape[-1])])


def _epi_att_delta(att):
    def fn(r, rows, consts):
        hr = lax.broadcasted_iota(jnp.int32, (ATT_W, ATT_W), 0) // HEAD_DIM
        hc = lax.broadcasted_iota(jnp.int32, (ATT_W, ATT_W), 1) // HEAD_DIM
        return r, [_nn(r[:, :ATT_W] * rows[0], (hr == hc).astype(F32), HIGH)], []
    return (fn, [att], [], [(ATT_W, F32)], [])


def _rms_fwd(x, g, name, comm=None):
    def fn(rv, hv, cv):
        return [_rms(rv[0], cv[0])], []
    res = _rows(fn, [x], [g.reshape(1, -1)], [(x.shape[1], MXU)], tile=512, name=name, comm=comm)
    return res[0] if comm is None else (res[0][0], res[1])


def _rms_bwd(x, g, dh, dres, name):
    def fn(rv, hv, cv):
        xb, dhb, drb = rv
        _, vjp = jax.vjp(_rms, xb, cv[0])
        dx, _ = vjp(dhb)
        rstd = lax.rsqrt(jnp.mean(xb * xb, axis=-1, keepdims=True) + NORM_EPS)
        return [drb + dx], [_colsum8(dhb * xb * rstd)]
    d = x.shape[1]
    return _rows(fn, [x, dh, dres], [g.reshape(1, -1)], [(d, F32)], [(8, d)], tile=512, name=name)


def _slope_dist(hp, hh, dist, dil):
    hf = (2 * hp + hh + 1).astype(F32)
    slope = jnp.exp(jnp.zeros(dist.shape, F32) - hf * LN2)
    return slope * (dist.astype(F32) * float(dil))


def _att_delta(datt, att, name):
    def fn(rv, hv, cv):
        r = lax.broadcasted_iota(jnp.int32, (ATT_W, ATT_W), 0) // HEAD_DIM
        c = lax.broadcasted_iota(jnp.int32, (ATT_W, ATT_W), 1) // HEAD_DIM
        ones = (r == c).astype(F32)
        return [_nn(rv[0] * rv[1], ones, HI)], []
    return _rows(fn, [datt, att], [], [(ATT_W, F32)], tile=512, name=name)[0]


ATT_G = 2048


def _deinterleave(dst, src, dil, ld, region, offset):
    for r in range(dil):
        rows = pl.ds(r, ld, stride=dil) if dil > 1 else pl.ds(0, ld)
        dst[r * region + offset:r * region + offset + ld, :] = src[rows, :]


def _deinterleave_edge(dst, src, dil, region, offset, first_row):
    for r in range(dil):
        rows = pl.ds(first_row + r, BLK, stride=dil) if dil > 1 else pl.ds(first_row, BLK)
        dst[r * region + offset:r * region + offset + BLK, :] = src[rows, :]


def _att_fwd_fused(proj, name, comm=None):
    s, npc = proj.shape
    gsz = ATT_G
    ng = s // gsz
    assert s % gsz == 0
    scale = HEAD_DIM ** -0.5
    comm = comm or _Comm()

    def body(*refs):
        (q_ref, kp_ref, kc_ref, vp_ref, vc_ref, att_ref, lse_ref, attb_ref, qd, kd, vd, nd, md, dd, nn, mn, dn), cm = comm.split(refs, 5, 3, 9)
        hp, g = pl.program_id(0), pl.program_id(1)
        comm.start_at((hp == 0) & (g == 0), cm)
        lane = lax.broadcasted_iota(jnp.int32, (BLK, BLK), 1)
        qi = lax.broadcasted_iota(jnp.int32, (BLK, 2 * BLK), 0)
        ki = lax.broadcasted_iota(jnp.int32, (BLK, 2 * BLK), 1)
        dist = BLK + qi - ki
        band = (dist >= 0) & (dist <= BLK)
        for pi, (_, dil) in enumerate(ATT_PATTERNS):
            ld = gsz // dil
            nbg = ld // BLK
            _deinterleave(qd, q_ref, dil, ld, ld, 0)
            _deinterleave(kd, kc_ref, dil, ld, ld + BLK, BLK)
            _deinterleave(vd, vc_ref, dil, ld, ld + BLK, BLK)
            _deinterleave_edge(kd, kp_ref, dil, ld + BLK, 0, gsz - BLK * dil)
            _deinterleave_edge(vd, vp_ref, dil, ld + BLK, 0, gsz - BLK * dil)
            bias = [_slope_dist(hp, hh, dist, dil) for hh in (0, 1)]

            def tile(t, carry, ld=ld, nbg=nbg, bias=bias):
                r, b = t // nbg, t % nbg
                qo = pl.multiple_of(r * ld + b * BLK, BLK)
                ko = pl.multiple_of(r * (ld + BLK) + b * BLK, BLK)
                q = qd[pl.ds(qo, BLK), :]
                kk = kd[pl.ds(ko, 2 * BLK), :].astype(MXU)
                vv = vd[pl.ds(ko, 2 * BLK), :].astype(MXU)
                valid = band & ((g > 0) | (b > 0) | (ki >= BLK))
                num = jnp.zeros((BLK, BLK), F32)
                mx = jnp.zeros((BLK, BLK), F32)
                den = jnp.zeros((BLK, BLK), F32)
                for hh in (0, 1):
                    hmask = (lane < HEAD_DIM) if hh == 0 else (lane >= HEAD_DIM)
                    qm = jnp.where(hmask, q, 0.0).astype(MXU)
                    sc = jnp.where(valid, _nt(qm, kk) * scale - bias[hh], NEG)
                    m = jnp.max(sc, axis=1, keepdims=True)
                    p = jnp.exp(sc - m)
                    dn_ = jnp.sum(p, axis=1, keepdims=True)
                    o = _nn(p.astype(MXU), vv)
                    num = jnp.where(hmask, o, num)
                    mx = jnp.where(hmask, m, mx)
                    den = jnp.where(hmask, dn_, den)
                nd[pl.ds(qo, BLK), :] = num
                md[pl.ds(qo, BLK), :] = mx
                dd[pl.ds(qo, BLK), :] = den
                return carry

            lax.fori_loop(0, dil * nbg, tile, 0, unroll=8)
            for r in range(dil):
                rows = pl.ds(r, ld, stride=dil) if dil > 1 else pl.ds(0, ld)
                nn.at[pi][rows, :] = nd[r * ld:(r + 1) * ld, :]
                mn.at[pi][rows, :] = md[r * ld:(r + 1) * ld, :]
                dn.at[pi][rows, :] = dd[r * ld:(r + 1) * ld, :]

        def merge(c, carry):
            rows = pl.ds(pl.multiple_of(c * 256, 256), 256)
            ms = [mn[pi, rows, :] for pi in range(len(ATT_PATTERNS))]
            m_all = functools.reduce(jnp.maximum, ms)
            num = jnp.zeros((256, BLK), F32)
            den = jnp.zeros((256, BLK), F32)
            for pi in range(len(ATT_PATTERNS)):
                e = jnp.exp(ms[pi] - m_all)
                num = num + nn[pi, rows, :] * e
                den = den + dn[pi, rows, :] * e
            att = num / den
            att_ref[rows, :] = att
            attb_ref[rows, :] = att.astype(MXU)
            lse_ref[rows, :] = m_all + jnp.log(den)
            return carry

        lax.fori_loop(0, gsz // 256, merge, 0)
        comm.wait_at((hp == 3) & (g == ng - 1), cm)

    def cur(base):
        return pl.BlockSpec((gsz, BLK), lambda hp, g: (g, base // BLK + hp))

    def prev(base):
        return pl.BlockSpec((gsz, BLK), lambda hp, g: (jnp.maximum(g - 1, 0), base // BLK + hp))

    o_spec = pl.BlockSpec((gsz, BLK), lambda hp, g: (g, hp))
    npat = len(ATT_PATTERNS)
    res = pl.pallas_call(
        body, name=name, grid=(4, ng),
        in_specs=[cur(C_Q), prev(C_K), cur(C_K), prev(C_V), cur(C_V)] + [ANY] * comm.n,
        out_specs=[o_spec] * 3 + [ANY] * comm.n,
        out_shape=[jax.ShapeDtypeStruct((s, ATT_W), F32)] * 2 + [jax.ShapeDtypeStruct((s, ATT_W), MXU)] + comm.out_shape(),
        scratch_shapes=[pltpu.VMEM((gsz, BLK), F32), pltpu.VMEM((2 * gsz, BLK), F32), pltpu.VMEM((2 * gsz, BLK), F32)]
        + [pltpu.VMEM((gsz, BLK), F32)] * 3 + [pltpu.VMEM((npat, gsz, BLK), F32)] * 3 + comm.scratch(),
        compiler_params=_cp("arbitrary", "arbitrary"),
    )(proj, proj, proj, proj, proj, *comm.args())
    return res[0], res[1], res[2], list(res[3:])


def _att_bwd_fused(proj, datt, lse, delta, name, comm=None):
    s, npc = proj.shape
    gsz = ATT_G
    ng = s // gsz
    scale = HEAD_DIM ** -0.5
    comm = comm or _Comm()

    def body(*refs):
        (qc_ref, qn_ref, kp_ref, kc_ref, vp_ref, vc_ref, doc_ref, don_ref, lsc_ref, lsn_ref, dlc_ref, dln_ref,
         dq_ref, dk_ref, dv_ref, qd, dod, lsd, dld, kd, vd, dqd, dkd, dvd), cm = comm.split(refs, 12, 3, 9)
        hp, g = pl.program_id(0), pl.program_id(1)
        comm.start_at((hp == 0) & (g == 0), cm)
        lane = lax.broadcasted_iota(jnp.int32, (BLK, BLK), 1)
        qi = lax.broadcasted_iota(jnp.int32, (BLK, BLK), 0)
        ki = lax.broadcasted_iota(jnp.int32, (BLK, BLK), 1)
        d_far = BLK + qi - ki
        d_near = qi - ki
        for pi, (_, dil) in enumerate(ATT_PATTERNS):
            ld = gsz // dil
            nbg = ld // BLK
            reg = ld + BLK
            for dst, c_ref, n_ref in ((qd, qc_ref, qn_ref), (dod, doc_ref, don_ref), (lsd, lsc_ref, lsn_ref), (dld, dlc_ref, dln_ref)):
                _deinterleave(dst, c_ref, dil, ld, reg, 0)
                _deinterleave_edge(dst, n_ref, dil, reg, ld, 0)
            for dst, p_ref, c_ref in ((kd, kp_ref, kc_ref), (vd, vp_ref, vc_ref)):
                _deinterleave(dst, c_ref, dil, ld, reg, BLK)
                _deinterleave_edge(dst, p_ref, dil, reg, 0, gsz - BLK * dil)
            b_far = [_slope_dist(hp, hh, d_far, dil) for hh in (0, 1)]
            b_near = [_slope_dist(hp, hh, d_near, dil) for hh in (0, 1)]

            def tile(t, carry, ld=ld, nbg=nbg, reg=reg, b_far=b_far, b_near=b_near):
                r, b = t // nbg, t % nbg
                oo = pl.multiple_of(r * ld + b * BLK, BLK)
                ro = pl.multiple_of(r * reg + b * BLK, BLK)
                qn, qx = qd[pl.ds(ro, BLK), :], qd[pl.ds(ro + BLK, BLK), :]
                don, dox = dod[pl.ds(ro, BLK), :], dod[pl.ds(ro + BLK, BLK), :]
                lsn, lsx = lsd[pl.ds(ro, BLK), :], lsd[pl.ds(ro + BLK, BLK), :]
                dln, dlx = dld[pl.ds(ro, BLK), :], dld[pl.ds(ro + BLK, BLK), :]
                kp, kc = kd[pl.ds(ro, BLK), :].astype(MXU), kd[pl.ds(ro + BLK, BLK), :].astype(MXU)
                vp, vc = vd[pl.ds(ro, BLK), :].astype(MXU), vd[pl.ds(ro + BLK, BLK), :].astype(MXU)
                ok_a = (d_far <= BLK) & ((g > 0) | (b > 0))
                ok_b = d_near >= 0
                ok_c = (d_far <= BLK) & ((g < ng - 1) | (b < nbg - 1))

                def grads(qm, dom, k, v, ls, dl, bias, valid, hh):
                    c0 = hh * HEAD_DIM
                    sc = _nt(qm, k) * scale - bias
                    p = jnp.exp(jnp.where(valid, sc - ls[:, c0:c0 + 1], NEG))
                    ds = p * (_nt(dom, v) - dl[:, c0:c0 + 1])
                    return p.astype(MXU), ds.astype(MXU)

                dq = jnp.zeros((BLK, BLK), F32)
                dk = jnp.zeros((BLK, BLK), F32)
                dv = jnp.zeros((BLK, BLK), F32)
                for hh in (0, 1):
                    hmask = (lane < HEAD_DIM) if hh == 0 else (lane >= HEAD_DIM)
                    qnm = jnp.where(hmask, qn, 0.0).astype(MXU)
                    qxm = jnp.where(hmask, qx, 0.0).astype(MXU)
                    donm = jnp.where(hmask, don, 0.0).astype(MXU)
                    doxm = jnp.where(hmask, dox, 0.0).astype(MXU)
                    _, ds_a = grads(qnm, donm, kp, vp, lsn, dln, b_far[hh], ok_a, hh)
                    p_b, ds_b = grads(qnm, donm, kc, vc, lsn, dln, b_near[hh], ok_b, hh)
                    p_c, ds_c = grads(qxm, doxm, kc, vc, lsx, dlx, b_far[hh], ok_c, hh)
                    dq = jnp.where(hmask, _nn(ds_a, kp) + _nn(ds_b, kc), dq)
                    dk = dk + _tn(ds_b, qnm) + _tn(ds_c, qxm)
                    dv = dv + _tn(p_b, donm) + _tn(p_c, doxm)
                dqd[pl.ds(oo, BLK), :] = dq * scale
                dkd[pl.ds(oo, BLK), :] = dk * scale
                dvd[pl.ds(oo, BLK), :] = dv
                return carry

            lax.fori_loop(0, dil * nbg, tile, 0, unroll=4)
            for out, src in ((dq_ref, dqd), (dk_ref, dkd), (dv_ref, dvd)):
                for r in range(dil):
                    rows = pl.ds(r, ld, stride=dil) if dil > 1 else pl.ds(0, ld)
                    if pi == 0:
                        out[rows, :] = src[r * ld:(r + 1) * ld, :]
                    else:
                        out[rows, :] = out[rows, :] + src[r * ld:(r + 1) * ld, :]
        comm.wait_at((hp == 3) & (g == ng - 1), cm)

    def pspec(base, shift):
        return pl.BlockSpec((gsz, BLK), lambda hp, g: (jnp.clip(g + shift, 0, ng - 1), base // BLK + hp))

    def wspec(shift):
        return pl.BlockSpec((gsz, BLK), lambda hp, g: (jnp.clip(g + shift, 0, ng - 1), hp))

    in_specs = [pspec(C_Q, 0), pspec(C_Q, 1), pspec(C_K, -1), pspec(C_K, 0), pspec(C_V, -1), pspec(C_V, 0),
                wspec(0), wspec(1), wspec(0), wspec(1), wspec(0), wspec(1)] + [ANY] * comm.n
    res = pl.pallas_call(
        body, name=name, grid=(4, ng), in_specs=in_specs,
        out_specs=[wspec(0)] * 3 + [ANY] * comm.n,
        out_shape=[jax.ShapeDtypeStruct((s, ATT_W), F32)] * 3 + comm.out_shape(),
        scratch_shapes=[pltpu.VMEM((2 * gsz, BLK), F32)] * 6 + [pltpu.VMEM((gsz, BLK), F32)] * 3 + comm.scratch(),
        compiler_params=_cp("arbitrary", "arbitrary"),
    )(proj, proj, proj, proj, proj, proj, datt, datt, lse, lse, delta, delta, *comm.args())
    return res[0], res[1], res[2], list(res[3:])


def _att_bwd_rev(proj, datt, lse, delta, name, comm=None):
    s, npc = proj.shape
    gsz = ATT_G
    ng = s // gsz
    npat = len(ATT_PATTERNS)
    scale = HEAD_DIM ** -0.5
    comm = comm or _Comm()

    def body(*refs):
        (q_ref, kp_ref, kc_ref, vp_ref, vc_ref, do_ref, ls_ref, dl_ref, dq_out, dk_out, dv_out,
         qd, dod, lsd, dld, kd, vd, dqd, dkc, dvc, dkp, dvp, kcar, vcar, dq_ref, dk_ref, dv_ref), cm = comm.split(refs, 8, 3, 16)
        hp, gi = pl.program_id(0), pl.program_id(1)
        g = ng - 1 - gi
        comm.start_at((hp == 0) & (gi == 0), cm)

        @pl.when(gi == 0)
        def _():
            kcar[...] = jnp.zeros_like(kcar)
            vcar[...] = jnp.zeros_like(vcar)

        lane = lax.broadcasted_iota(jnp.int32, (BLK, BLK), 1)
        qi = lax.broadcasted_iota(jnp.int32, (BLK, 2 * BLK), 0)
        ki = lax.broadcasted_iota(jnp.int32, (BLK, 2 * BLK), 1)
        dist = BLK + qi - ki
        band = (dist >= 0) & (dist <= BLK)
        for pi, (_, dil) in enumerate(ATT_PATTERNS):
            ld = gsz // dil
            nbg = ld // BLK
            reg = ld + BLK
            for dst, src in ((qd, q_ref), (dod, do_ref), (lsd, ls_ref), (dld, dl_ref)):
                _deinterleave(dst, src, dil, ld, ld, 0)
            for dst, p_ref, c_ref in ((kd, kp_ref, kc_ref), (vd, vp_ref, vc_ref)):
                _deinterleave(dst, c_ref, dil, ld, reg, BLK)
                _deinterleave_edge(dst, p_ref, dil, reg, 0, gsz - BLK * dil)
            bias = [_slope_dist(hp, hh, dist, dil) for hh in (0, 1)]

            def tile(t, carry, ld=ld, nbg=nbg, reg=reg, bias=bias):
                r, b = t // nbg, t % nbg
                oo = pl.multiple_of(r * ld + b * BLK, BLK)
                ko = pl.multiple_of(r * reg + b * BLK, BLK)
                q, do = qd[pl.ds(oo, BLK), :], dod[pl.ds(oo, BLK), :]
                ls, dl = lsd[pl.ds(oo, BLK), :], dld[pl.ds(oo, BLK), :]
                kk = kd[pl.ds(ko, 2 * BLK), :].astype(MXU)
                vv = vd[pl.ds(ko, 2 * BLK), :].astype(MXU)
                valid = band & ((g > 0) | (b > 0) | (ki >= BLK))
                dq = jnp.zeros((BLK, BLK), F32)
                dkk = jnp.zeros((2 * BLK, BLK), F32)
                dvv = jnp.zeros((2 * BLK, BLK), F32)
                for hh in (0, 1):
                    c0 = hh * HEAD_DIM
                    hmask = (lane < HEAD_DIM) if hh == 0 else (lane >= HEAD_DIM)
                    qm = jnp.where(hmask, q, 0.0).astype(MXU)
                    dom = jnp.where(hmask, do, 0.0).astype(MXU)
                    sc = _nt(qm, kk) * scale - bias[hh]
                    p = jnp.exp(jnp.where(valid, sc - ls[:, c0:c0 + 1], NEG))
                    ds = (p * (_nt(dom, vv) - dl[:, c0:c0 + 1])).astype(MXU)
                    dq = jnp.where(hmask, _nn(ds, kk), dq)
                    dkk = dkk + _tn(ds, qm)
                    dvv = dvv + _tn(p.astype(MXU), dom)
                dqd[pl.ds(oo, BLK), :] = dq * scale
                dkp[pl.ds(oo, BLK), :] = dkk[:BLK] * scale
                dkc[pl.ds(oo, BLK), :] = dkk[BLK:] * scale
                dvp[pl.ds(oo, BLK), :] = dvv[:BLK]
                dvc[pl.ds(oo, BLK), :] = dvv[BLK:]
                return carry

            lax.fori_loop(0, dil * nbg, tile, 0, unroll=8)
            for r in range(dil):
                rows = pl.ds(r, ld, stride=dil) if dil > 1 else pl.ds(0, ld)
                lo, hi = r * ld, (r + 1) * ld
                edge = slice(pi * gsz + r * BLK, pi * gsz + (r + 1) * BLK)
                for out, cur, prv, car in ((dk_ref, dkc, dkp, kcar), (dv_ref, dvc, dvp, vcar)):
                    later = car[edge, :] if nbg == 1 else jnp.concatenate([prv[lo + BLK:hi, :], car[edge, :]], axis=0)
                    total = cur[lo:hi, :] + later
                    car[edge, :] = prv[lo:lo + BLK, :]
                    out[rows, :] = total if pi == 0 else out[rows, :] + total
                dq_ref[rows, :] = dqd[lo:hi, :] if pi == 0 else dq_ref[rows, :] + dqd[lo:hi, :]
        for out, acc in ((dq_out, dq_ref), (dk_out, dk_ref), (dv_out, dv_ref)):
            out[...] = acc[...].astype(out.dtype)
        comm.wait_at((hp == 3) & (gi == ng - 1), cm)

    def pspec(base, shift):
        return pl.BlockSpec((gsz, BLK), lambda hp, gi: (jnp.maximum(ng - 1 - gi + shift, 0), base // BLK + hp))

    wspec = pl.BlockSpec((gsz, BLK), lambda hp, gi: (ng - 1 - gi, hp))
    in_specs = [pspec(C_Q, 0), pspec(C_K, -1), pspec(C_K, 0), pspec(C_V, -1), pspec(C_V, 0), wspec, wspec, wspec] + [ANY] * comm.n
    res = pl.pallas_call(
        body, name=name, grid=(4, ng), in_specs=in_specs,
        out_specs=[wspec] * 3 + [ANY] * comm.n,
        out_shape=[jax.ShapeDtypeStruct((s, ATT_W), MXU)] * 3 + comm.out_shape(),
        scratch_shapes=[pltpu.VMEM((gsz, BLK), F32)] * 4 + [pltpu.VMEM((2 * gsz, BLK), F32)] * 2
        + [pltpu.VMEM((gsz, BLK), F32)] * 5 + [pltpu.VMEM((npat * gsz, BLK), F32)] * 2 + [pltpu.VMEM((gsz, BLK), F32)] * 3
        + comm.scratch(),
        compiler_params=_cp("arbitrary", "arbitrary"),
    )(proj, proj, proj, proj, proj, datt, lse, delta, *comm.args())
    return res[0], res[1], res[2], list(res[3:])


def _shift_down(cur, halo, sft):
    if sft == 0:
        return cur
    t = cur.shape[0]
    rolled = pltpu.roll(cur, sft, 0)
    hr = pltpu.roll(halo, sft, 0)
    row = lax.broadcasted_iota(jnp.int32, cur.shape, 0)
    return jnp.where(row < sft, jnp.tile(hr, (t // 8, 1)), rolled)


def _shift_up(cur, halo, sft):
    if sft == 0:
        return cur
    t = cur.shape[0]
    rolled = pltpu.roll(cur, t - sft, 0)
    hr = pltpu.roll(halo, 8 - sft, 0)
    row = lax.broadcasted_iota(jnp.int32, cur.shape, 0)
    return jnp.where(row >= t - sft, jnp.tile(hr, (t // 8, 1)), rolled)


def _conv(x, xh, w, b):
    y = b + x * w[CONV_K - 1:CONV_K]
    for k in range(CONV_K - 1):
        y = y + _shift_down(x, xh, CONV_K - 1 - k) * w[k:k + 1]
    return y


def _conv_bwd(x, xh, dy, dyh, w):
    dx = dy * w[CONV_K - 1:CONV_K]
    dws = []
    for k in range(CONV_K - 1):
        sft = CONV_K - 1 - k
        dx = dx + _shift_up(dy, dyh, sft) * w[k:k + 1]
        dws.append(jnp.sum(dy * _shift_down(x, xh, sft), axis=0, keepdims=True))
    dws.append(jnp.sum(dy * x, axis=0, keepdims=True))
    c = x.shape[1]
    dw = jnp.concatenate(dws + [jnp.zeros((8 - CONV_K, c), F32)], axis=0)
    return dx, dw, jnp.sum(dy, axis=0, keepdims=True)


def _pad8(w):
    return jnp.concatenate([w, jnp.zeros((8 - w.shape[0], w.shape[1]), w.dtype)], axis=0)


def _ssd_pre(proj, conv_w, conv_b, dt_bias128, name):
    def fn(rv, hv, cv):
        xbc, dtr = rv
        return [_silu(_conv(xbc, hv[0], cv[0], cv[1])), _softplus(dtr + cv[2])], []
    return _rows(fn, [(proj, 1024, C_XBC // 1024), (proj, BLK, C_DT // BLK)],
                 [_pad8(conv_w), conv_b.reshape(1, -1), dt_bias128],
                 [(1024, F32), (BLK, F32)], tile=256, name=name, halos=[(0, "prev")])


def _ssd_pre_bwd(proj, dxc, ddt, conv_w, conv_b, dt_bias128, name):
    def fn(rv, hv, cv):
        xbc, dtr, dxcb, ddtb = rv
        xh, dxch_raw, xnext = hv
        w, b, bias = cv
        pre = _conv(xbc, xh, w, b)
        sg = _sigmoid(pre)
        dpre = dxcb * (sg * (1.0 + pre * (1.0 - sg)))
        t = xbc.shape[0]
        tail = jnp.concatenate([xbc[t - 8:], xnext], axis=0)
        pre_n = _conv(tail[8:], tail[:8], w, b)
        sgn = _sigmoid(pre_n)
        dpre_h = dxch_raw * (sgn * (1.0 + pre_n * (1.0 - sgn)))
        dx, dw, db = _conv_bwd(xbc, xh, dpre, dpre_h, w)
        ddr = ddtb * _sigmoid(dtr + bias)
        return [dx, ddr], [dw, jnp.concatenate([db, jnp.zeros((7, db.shape[1]), F32)], axis=0), _colsum8(ddr)]
    return _rows(fn, [(proj, 1024, C_XBC // 1024), (proj, BLK, C_DT // BLK), dxc, ddt],
                 [_pad8(conv_w), conv_b.reshape(1, -1), dt_bias128],
                 [(1024, MXU), (BLK, MXU)], [(8, 1024), (8, 1024), (8, BLK)], tile=256, name=name,
                 halos=[(0, "prev"), (2, "next"), (0, "next")])


SSD_CPB = 1


def _head_cols(v, h0):
    lane = lax.broadcasted_iota(jnp.int32, (v.shape[0], BLK), 1)
    return jnp.where(lane < HEAD_DIM, v[:, h0:h0 + 1], v[:, h0 + 1:h0 + 2])


def _ssd_scan(xc, dt, par, name):
    s = xc.shape[0]
    nc = s // BLK

    def body(x_ref, dt_ref, par_ref, y_ref, st_ref, h_ref):
        c = pl.program_id(0)

        @pl.when(c == 0)
        def _():
            h_ref[...] = jnp.zeros_like(h_ref)

        st_ref[0] = h_ref[...]
        dt = dt_ref[...]
        a_row = -jnp.exp(par_ref[0:1, :])
        d_row = par_ref[1:2, :]
        ri = lax.broadcasted_iota(jnp.int32, (BLK, BLK), 0)
        ci = lax.broadcasted_iota(jnp.int32, (BLK, BLK), 1)
        tril = ri >= ci
        cs = _nn(tril.astype(F32), dt * a_row, HI)
        cst, dtt = cs.T, dt.T
        last = cs[BLK - 1:BLK, :]
        wcol = jnp.exp(last - cs) * dt
        ecs = jnp.exp(cs)
        elast = jnp.exp(last)
        for g in (0, 1):
            bg = x_ref[:, 512 + g * BLK:512 + (g + 1) * BLK].astype(MXU)
            cg = x_ref[:, 768 + g * BLK:768 + (g + 1) * BLK].astype(MXU)
            gm = _nt(cg, bg)
            for pp in (0, 1):
                pr = 2 * g + pp
                h0 = 2 * pr
                x2 = x_ref[:, pr * BLK:(pr + 1) * BLK]
                hprev = h_ref[pr * BLK:(pr + 1) * BLK, :]
                yp = jnp.zeros((BLK, BLK), F32)
                for hh in (0, 1):
                    h = h0 + hh
                    hmask = (ci < HEAD_DIM) if hh == 0 else (ci >= HEAD_DIM)
                    lm = jnp.exp(jnp.where(tril, cs[:, h:h + 1] - cst[h:h + 1, :], NEG))
                    mm = gm * lm * dtt[h:h + 1, :]
                    yp = yp + _nn(mm.astype(MXU), jnp.where(hmask, x2, 0.0).astype(MXU))
                y0 = _nt(cg, hprev.astype(MXU))
                y_ref[:, pr * BLK:(pr + 1) * BLK] = yp + _head_cols(ecs, h0) * y0 + _head_cols(d_row, h0) * x2
                dec = jnp.where(ri < HEAD_DIM, elast[:, h0:h0 + 1], elast[:, h0 + 1:h0 + 2])
                xw = (x2 * _head_cols(wcol, h0)).astype(MXU)
                h_ref[pr * BLK:(pr + 1) * BLK, :] = dec * hprev + _tn(xw, bg)

    return pl.pallas_call(
        body, name=name, grid=(nc,),
        in_specs=[pl.BlockSpec((BLK, 1024), lambda c: (c, 0)), pl.BlockSpec((BLK, BLK), lambda c: (c, 0)),
                  pl.BlockSpec((8, BLK), lambda c: (0, 0))],
        out_specs=[pl.BlockSpec((BLK, SSD_W), lambda c: (c, 0)), pl.BlockSpec((1, SSD_W, SSD_STATE), lambda c: (c, 0, 0))],
        out_shape=[jax.ShapeDtypeStruct((s, SSD_W), F32), jax.ShapeDtypeStruct((nc, SSD_W, SSD_STATE), F32)],
        scratch_shapes=[pltpu.VMEM((SSD_W, SSD_STATE), F32)],
        compiler_params=_cp("arbitrary"),
    )(xc, dt, par)


def _ssd_scan_bwd(xc, dt, par, st, dy, name, comm=None):
    s = xc.shape[0]
    cpb = SSD_CPB
    nb = s // (cpb * BLK)
    comm = comm or _Comm()

    def chunk(x_ref, dt_ref, par_ref, st_ref, dy_ref, dx_ref, ddt_ref, dal_ref, dd_ref, dh_ref):
        dt = dt_ref[...]
        a_row = -jnp.exp(par_ref[0:1, :])
        d_row = par_ref[1:2, :]
        ri = lax.broadcasted_iota(jnp.int32, (BLK, BLK), 0)
        ci = lax.broadcasted_iota(jnp.int32, (BLK, BLK), 1)
        tril = ri >= ci
        cs = _nn(tril.astype(F32), dt * a_row, HI)
        cst, dtt = cs.T, dt.T
        last = cs[BLK - 1:BLK, :]
        tolast = jnp.exp(last - cs)
        wcol = tolast * dt
        ecs = jnp.exp(cs)
        elast = jnp.exp(last)
        dcs_col = jnp.zeros((BLK, BLK), F32)
        ddt_col = jnp.zeros((BLK, BLK), F32)
        dcs_row = jnp.zeros((BLK, BLK), F32)
        ddt_row = jnp.zeros((BLK, BLK), F32)
        dlast = jnp.zeros((1, BLK), F32)
        ddsk = jnp.zeros((1, BLK), F32)
        for g in (0, 1):
            bg32 = x_ref[:, 512 + g * BLK:512 + (g + 1) * BLK]
            cg32 = x_ref[:, 768 + g * BLK:768 + (g + 1) * BLK]
            bg, cg = bg32.astype(MXU), cg32.astype(MXU)
            gm = _nt(cg, bg)
            dgm = jnp.zeros((BLK, BLK), F32)
            dbg = jnp.zeros((BLK, BLK), F32)
            dcg = jnp.zeros((BLK, BLK), F32)
            for pp in (0, 1):
                pr = 2 * g + pp
                h0 = 2 * pr
                x2 = x_ref[:, pr * BLK:(pr + 1) * BLK]
                dy2 = dy_ref[:, pr * BLK:(pr + 1) * BLK]
                hprev = st_ref[0, pr * BLK:(pr + 1) * BLK, :]
                dhn = dh_ref[pr * BLK:(pr + 1) * BLK, :]
                x2m, dhnm = x2.astype(MXU), dhn.astype(MXU)
                zb = _nt(bg, dhnm)
                y0 = _nt(cg, hprev.astype(MXU))
                esel = _head_cols(ecs, h0)
                wsel = _head_cols(wcol, h0)
                dx2 = _head_cols(d_row, h0) * dy2 + wsel * zb
                pick2 = (((ri < HEAD_DIM) & (ci == h0)) | ((ri >= HEAD_DIM) & (ci == h0 + 1))).astype(F32)
                sums = _nn(jnp.concatenate([dy2 * y0, x2 * zb, dy2 * x2], axis=0), pick2, HIGH)
                de2, dw2, dd2 = sums[:BLK], sums[BLK:2 * BLK], sums[2 * BLK:]
                v2 = dw2 * wcol
                dcs_col = dcs_col + ecs * de2 - v2
                ddt_col = ddt_col + dw2 * tolast
                hsum = _nn(dhn * hprev, jnp.ones((BLK, BLK), F32), HIGH)
                dlast = dlast + elast * jnp.sum(jnp.where(pick2 > 0.0, hsum, 0.0), axis=0, keepdims=True) \
                    + jnp.sum(v2, axis=0, keepdims=True)
                ddsk = ddsk + jnp.sum(dd2, axis=0, keepdims=True)
                ts = []
                for hh in (0, 1):
                    h = h0 + hh
                    hmask = (ci < HEAD_DIM) if hh == 0 else (ci >= HEAD_DIM)
                    ons = (ri == h).astype(F32)
                    dym = jnp.where(hmask, dy2, 0.0).astype(MXU)
                    dt_r = dtt[h:h + 1, :]
                    lm = jnp.exp(jnp.where(tril, cs[:, h:h + 1] - cst[h:h + 1, :], NEG))
                    mm = gm * lm * dt_r
                    dx2 = dx2 + _tn(mm.astype(MXU), dym)
                    dm = _nt(dym, x2m)
                    t1 = dm * lm
                    dgm = dgm + t1 * dt_r
                    tt = t1 * gm
                    ddt_row = ddt_row + ons * jnp.sum(tt, axis=0, keepdims=True)
                    t = tt * dt_r
                    dcs_row = dcs_row - ons * jnp.sum(t, axis=0, keepdims=True)
                    ts.append(t)
                rows2 = lax.broadcasted_iota(jnp.int32, (2 * BLK, BLK), 0)
                lane2 = lax.broadcasted_iota(jnp.int32, (2 * BLK, BLK), 1)
                to_lane = ((rows2 < BLK) & (lane2 == h0)) | ((rows2 >= BLK) & (lane2 == h0 + 1))
                dcs_col = dcs_col + _nn(jnp.concatenate(ts, axis=1), to_lane.astype(F32), HIGH)
                dx_ref[:, pr * BLK:(pr + 1) * BLK] = dx2
                edy = (esel * dy2).astype(MXU)
                dcg = dcg + _nn(edy, hprev.astype(MXU))
                dec = jnp.where(ri < HEAD_DIM, elast[:, h0:h0 + 1], elast[:, h0 + 1:h0 + 2])
                dh_ref[pr * BLK:(pr + 1) * BLK, :] = dec * dhn + _tn(edy, cg)
                dbg = dbg + _nn((x2 * wsel).astype(MXU), dhnm)
            dgmm = dgm.astype(MXU)
            dx_ref[:, 512 + g * BLK:512 + (g + 1) * BLK] = dbg + _tn(dgmm, cg)
            dx_ref[:, 768 + g * BLK:768 + (g + 1) * BLK] = dcg + _nn(dgmm, bg)
        dcs = dcs_col + dcs_row.T + jnp.where(ri == BLK - 1, dlast, 0.0)
        dda = _nn((ri <= ci).astype(F32), dcs, HI)
        ddt_ref[...] = ddt_col + ddt_row.T + a_row * dda
        da = jnp.sum(dt * dda, axis=0, keepdims=True)
        dal_ref[0:1, :] += da * a_row
        dd_ref[0:1, :] += ddsk

    def body(*refs):
        (x_ref, dt_ref, par_ref, st_ref, dy_ref, dx_ref, ddt_ref, dal_ref, dd_ref, dh_ref), cm = comm.split(refs, 5, 4, 1)
        c = pl.program_id(0)
        comm.start_at(c == 0, cm)

        @pl.when(c == 0)
        def _():
            dh_ref[...] = jnp.zeros_like(dh_ref)
            dal_ref[...] = jnp.zeros_like(dal_ref)
            dd_ref[...] = jnp.zeros_like(dd_ref)

        for cc in reversed(range(cpb)):
            rows = pl.ds(cc * BLK, BLK)
            chunk(x_ref.at[rows], dt_ref.at[rows], par_ref, st_ref.at[pl.ds(cc, 1)], dy_ref.at[rows], dx_ref.at[rows],
                  ddt_ref.at[rows], dal_ref, dd_ref, dh_ref)
        comm.wait_at(c == nb - 1, cm)

    rev = lambda c: (nb - 1 - c, 0)
    tb = cpb * BLK
    res = pl.pallas_call(
        body, name=name, grid=(nb,),
        in_specs=[pl.BlockSpec((tb, 1024), rev), pl.BlockSpec((tb, BLK), rev), pl.BlockSpec((8, BLK), lambda c: (0, 0)),
                  pl.BlockSpec((cpb, SSD_W, SSD_STATE), lambda c: (nb - 1 - c, 0, 0)), pl.BlockSpec((tb, SSD_W), rev)]
        + [ANY] * comm.n,
        out_specs=[pl.BlockSpec((tb, 1024), rev), pl.BlockSpec((tb, BLK), rev),
                   pl.BlockSpec((8, BLK), lambda c: (0, 0)), pl.BlockSpec((8, BLK), lambda c: (0, 0))] + [ANY] * comm.n,
        out_shape=[jax.ShapeDtypeStruct((s, 1024), F32), jax.ShapeDtypeStruct((s, BLK), F32),
                   jax.ShapeDtypeStruct((8, BLK), F32), jax.ShapeDtypeStruct((8, BLK), F32)] + comm.out_shape(),
        scratch_shapes=[pltpu.VMEM((SSD_W, SSD_STATE), F32)] + comm.scratch(),
        compiler_params=_cp("arbitrary"),
    )(xc, dt, par, st, dy, *comm.args())
    return res[0], res[1], res[2], res[3], list(res[4:])


def _ssd_gate(y, z, w):
    t = y * _silu(z)
    outs = []
    for g in (0, 1):
        tg = t[:, g * 256:(g + 1) * 256]
        outs.append(tg * lax.rsqrt(jnp.mean(tg * tg, axis=-1, keepdims=True) + SSD_NORM_EPS))
    return jnp.concatenate(outs, axis=1) * w


def _ssd_post(y, proj, norm_w, name):
    def fn(rv, hv, cv):
        return [_ssd_gate(rv[0], rv[1], cv[0])], []
    return _rows(fn, [y, (proj, SSD_W, C_Z // SSD_W)], [norm_w.reshape(1, -1)], [(SSD_W, MXU)], tile=512, name=name)[0]


def _ssd_post_bwd(y, proj, norm_w, dout, name):
    def fn(rv, hv, cv):
        yb, zb, db = rv
        _, vjp = jax.vjp(lambda a, b: _ssd_gate(a, b, cv[0]), yb, zb)
        dy, dz = vjp(db)
        t = yb * _silu(zb)
        nrm = []
        for g in (0, 1):
            tg = t[:, g * 256:(g + 1) * 256]
            nrm.append(tg * lax.rsqrt(jnp.mean(tg * tg, axis=-1, keepdims=True) + SSD_NORM_EPS))
        return [dy, dz], [_colsum8(db * jnp.concatenate(nrm, axis=1))]
    return _rows(fn, [y, (proj, SSD_W, C_Z // SSD_W), dout], [norm_w.reshape(1, -1)],
                 [(SSD_W, F32), (SSD_W, MXU)], [(8, SSD_W)], tile=512, name=name)


LRU_T = 256


def _lru_conv(proj, conv_w, conv_b, name):
    def fn(rv, hv, cv):
        return [_conv(rv[0], hv[0], cv[0], cv[1])], []
    return _rows(fn, [(proj, LRU_W, C_XL // LRU_W)], [_pad8(conv_w), conv_b.reshape(1, -1)], [(LRU_W, F32)],
                 tile=512, name=name, halos=[(0, "prev")])[0]


def _lru_conv_bwd(proj, dxc, conv_w, name):
    def fn(rv, hv, cv):
        dx, dw, db = _conv_bwd(rv[0], hv[0], rv[1], hv[1], cv[0])
        return [dx], [dw, jnp.concatenate([db, jnp.zeros((7, db.shape[1]), F32)], axis=0)]
    return _rows(fn, [(proj, LRU_W, C_XL // LRU_W), dxc], [_pad8(conv_w)], [(LRU_W, MXU)], [(8, LRU_W), (8, LRU_W)],
                 tile=512, name=name, halos=[(0, "prev"), (1, "next")])


def _lru_au(pre_a, pre_x, xc, ba, bx, lam):
    r = _sigmoid(pre_a + ba)
    i = _sigmoid(pre_x + bx)
    log_a = -LRU_C * r * _softplus(-lam)
    a = jnp.exp(log_a)
    u = jnp.sqrt(1.0 - jnp.exp(2.0 * log_a)) * (i * xc)
    return a, u


def _lru_scan(pre, xc, proj, par, name):
    s = xc.shape[0]
    t = LRU_T

    def body(pre_ref, xc_ref, g_ref, par_ref, out_ref, h_ref, carry):
        c = pl.program_id(0)

        @pl.when(c == 0)
        def _():
            carry[...] = jnp.zeros_like(carry)

        a, u = _lru_au(pre_ref[:, :LRU_W], pre_ref[:, LRU_W:], xc_ref[...], par_ref[0:1, :], par_ref[1:2, :], par_ref[2:3, :])
        row = lax.broadcasted_iota(jnp.int32, (t, LRU_W), 0)
        sft = 1
        while sft < t:
            keep = row >= sft
            a_s = jnp.where(keep, pltpu.roll(a, sft, 0), 1.0)
            u_s = jnp.where(keep, pltpu.roll(u, sft, 0), 0.0)
            u = a * u_s + u
            a = a * a_s
            sft *= 2
        h = a * carry[0:1, :] + u
        h_ref[...] = h
        out_ref[...] = (h * _gelu(g_ref[...])).astype(out_ref.dtype)
        carry[0:1, :] = h[t - 1:t, :]

    return pl.pallas_call(
        body, name=name, grid=(s // t,),
        in_specs=[pl.BlockSpec((t, 2 * LRU_W), lambda c: (c, 0)), pl.BlockSpec((t, LRU_W), lambda c: (c, 0)),
                  pl.BlockSpec((t, LRU_W), lambda c: (c, C_G // LRU_W)), pl.BlockSpec((8, LRU_W), lambda c: (0, 0))],
        out_specs=[pl.BlockSpec((t, LRU_W), lambda c: (c, 0))] * 2,
        out_shape=[jax.ShapeDtypeStruct((s, LRU_W), MXU), jax.ShapeDtypeStruct((s, LRU_W), F32)],
        scratch_shapes=[pltpu.VMEM((8, LRU_W), F32)],
        compiler_params=_cp("arbitrary"),
    )(pre, xc, proj, par)


def _lru_scan_bwd(pre, xc, proj, par, h, dout, name):
    s = xc.shape[0]
    t = LRU_T
    n = s // t
    t8 = t // 8

    def body(pre_ref, xc_ref, g_ref, par_ref, h_ref, hh_ref, do_ref, dpre_ref, dxc_ref, dg_ref, dpar_ref, carry):
        c = pl.program_id(0)

        @pl.when(c == 0)
        def _():
            carry[...] = jnp.zeros_like(carry)
            dpar_ref[...] = jnp.zeros_like(dpar_ref)

        pa, px, xcb = pre_ref[:, :LRU_W], pre_ref[:, LRU_W:], xc_ref[...]
        ba, bx, lam = par_ref[0:1, :], par_ref[1:2, :], par_ref[2:3, :]
        (a, u), vjp = jax.vjp(_lru_au, pa, px, xcb, ba, bx, lam)
        g = g_ref[...]
        hcur = h_ref[...]
        do = do_ref[...]
        _, gvjp = jax.vjp(_gelu, g)
        dg_ref[...] = gvjp(do * hcur)[0].astype(dg_ref.dtype)
        row = lax.broadcasted_iota(jnp.int32, (t, LRU_W), 0)
        v = do * _gelu(g) + jnp.where(row == t - 1, carry[0:1, :], 0.0)
        b = jnp.where(row == t - 1, 0.0, pltpu.roll(a, t - 1, 0))
        sft = 1
        while sft < t:
            keep = row < t - sft
            b_s = jnp.where(keep, pltpu.roll(b, t - sft, 0), 1.0)
            v_s = jnp.where(keep, pltpu.roll(v, t - sft, 0), 0.0)
            v = b * v_s + v
            b = b * b_s
            sft *= 2
        dh = v
        carry[0:1, :] = a[0:1, :] * dh[0:1, :]
        hhalo = jnp.where(c == n - 1, 0.0, hh_ref[...])
        hprev = _shift_down(hcur, hhalo, 1)
        dpa, dpx, dxc, dba, dbx, dlam = vjp((dh * hprev, dh))
        dpre_ref[:, :LRU_W] = dpa
        dpre_ref[:, LRU_W:] = dpx
        dxc_ref[...] = dxc
        dpar_ref[0:1, :] += dba
        dpar_ref[1:2, :] += dbx
        dpar_ref[2:3, :] += dlam

    rev = lambda c: (n - 1 - c, 0)
    return pl.pallas_call(
        body, name=name, grid=(n,),
        in_specs=[pl.BlockSpec((t, 2 * LRU_W), rev), pl.BlockSpec((t, LRU_W), rev),
                  pl.BlockSpec((t, LRU_W), lambda c: (n - 1 - c, C_G // LRU_W)), pl.BlockSpec((8, LRU_W), lambda c: (0, 0)),
                  pl.BlockSpec((t, LRU_W), rev),
                  pl.BlockSpec((8, LRU_W), lambda c: (jnp.maximum((n - 1 - c) * t8 - 1, 0), 0)),
                  pl.BlockSpec((t, LRU_W), lambda c: (n - 1 - c, dout.shape[1] // LRU_W - 1))],
        out_specs=[pl.BlockSpec((t, 2 * LRU_W), rev), pl.BlockSpec((t, LRU_W), rev), pl.BlockSpec((t, LRU_W), rev),
                   pl.BlockSpec((8, LRU_W), lambda c: (0, 0))],
        out_shape=[jax.ShapeDtypeStruct((s, 2 * LRU_W), F32), jax.ShapeDtypeStruct((s, LRU_W), F32),
                   jax.ShapeDtypeStruct((s, LRU_W), MXU), jax.ShapeDtypeStruct((8, LRU_W), F32)],
        scratch_shapes=[pltpu.VMEM((8, LRU_W), F32)],
        compiler_params=_cp("arbitrary"),
    )(pre, xc, proj, par, h, h, dout)


def _swiglu_act(gu, name):
    def fn(rv, hv, cv):
        return [_silu(rv[0].astype(F32)) * rv[1].astype(F32)], []
    return _rows(fn, [(gu, D_FF, 0), (gu, D_FF, 1)], [], [(D_FF, MXU)], tile=256, name=name)[0]


def _swiglu_bwd(gu, da, name):
    def fn(rv, hv, cv):
        gt, up, dab = [t.astype(F32) for t in rv]
        sg = _sigmoid(gt)
        dgate = dab * up * (sg * (1.0 + gt * (1.0 - sg)))
        dup = dab * (gt * sg)
        return [jnp.concatenate([dgate, dup], axis=1)], []
    return _rows(fn, [(gu, D_FF, 0), (gu, D_FF, 1), da], [], [(2 * D_FF, MXU)], tile=256, name=name)[0]


def _loss_head(x, g, target, name):
    d = x.shape[1]

    def fn(rv, hv, cv):
        xb, tb = rv
        y, vjp = jax.vjp(_rms, xb, cv[0])
        err = y - tb
        dy = err * (1.0 / d)
        dx, _ = vjp(dy)
        rstd = lax.rsqrt(jnp.mean(xb * xb, axis=-1, keepdims=True) + NORM_EPS)
        e2 = err * err * (0.5 / d)
        e2 = functools.reduce(lambda a, b: a + b, [e2[:, k * BLK:(k + 1) * BLK] for k in range(d // BLK)])
        return [dx], [_colsum8(dy * xb * rstd), _colsum8(e2)]
    return _rows(fn, [x, target], [g.reshape(1, -1)], [(d, F32)], [(8, d), (8, BLK)], tile=512, name=name)


ANY = pl.BlockSpec(memory_space=pl.ANY)


def _coords():
    return lax.axis_index("x"), lax.axis_index("y"), lax.axis_index("c")


class _Comm:
    def __init__(self, gathers=(), scatters=()):
        self.gathers = list(gathers)
        self.scatters = list(scatters)
        self.n = len(self.gathers) + len(self.scatters)

    def args(self):
        return [g[0] for g in self.gathers] + self.scatters

    def out_shape(self):
        out = [jax.ShapeDtypeStruct((4,) + (a.shape if l is None else a.shape[1:]), a.dtype) for a, l, _ in self.gathers]
        return out + [jax.ShapeDtypeStruct((3,) + a.shape[1:], a.dtype) for a in self.scatters]

    def scratch(self):
        if not self.n:
            return []
        return [pltpu.SemaphoreType.DMA((3 * self.n,)), pltpu.SemaphoreType.DMA((3 * self.n,)),
                pltpu.SemaphoreType.DMA((max(len(self.gathers), 1),)),
                pltpu.SemaphoreType.DMA((3 * self.n,)), pltpu.SemaphoreType.DMA((3 * self.n,))]

    def split(self, refs, n_in, n_out, n_scratch):
        refs = list(refs)
        n = self.n
        own = refs[:n_in] + refs[n_in + n:n_in + n + n_out] + refs[n_in + 2 * n + n_out:n_in + 2 * n + n_out + n_scratch]
        cm = (refs[n_in:n_in + n], refs[n_in + n + n_out:n_in + 2 * n + n_out], refs[n_in + 2 * n + n_out + n_scratch:])
        return own, cm

    def _copies(self, cm, arriving):
        ins, outs, (send, recv, local, _, _) = cm
        x, y, c = _coords()
        me = 2 * x + y
        chips = [(1 - x, y), (x, 1 - y), (1 - x, 1 - y)]
        remote, locals_ = [], []
        ng = len(self.gathers)
        for i in range(self.n):
            if i < ng:
                _, l, halved = self.gathers[i]
                slab = ins[i] if l is None else ins[i].at[l]
                if not arriving:
                    locals_.append(pltpu.make_async_copy(slab, outs[i].at[me], local.at[i]))
            for j, (px, py) in enumerate(chips):
                if i < ng:
                    slot = 2 * px + py if arriving else me
                    src, dst = (slab.at[c], outs[i].at[slot, c]) if halved else (slab, outs[i].at[slot])
                else:
                    src, dst = ins[i].at[2 * px + py], outs[i].at[j]
                remote.append(pltpu.make_async_remote_copy(src, dst, send.at[3 * i + j], recv.at[3 * i + j],
                                                           device_id=(px, py, c), device_id_type=MESH))
        return remote, locals_

    def _handovers(self, cm, arriving):
        _, outs, (_, _, _, send, recv) = cm
        x, y, c = _coords()
        chips = [(1 - x, y), (x, 1 - y), (1 - x, 1 - y)]
        cps = []
        for i, (_, _, halved) in enumerate(self.gathers):
            if halved:
                for j, (px, py) in enumerate(chips):
                    src = outs[i].at[2 * px + py, c]
                    dst = outs[i].at[2 * px + py, 1 - c if arriving else c]
                    cps.append(pltpu.make_async_remote_copy(src, dst, send.at[3 * i + j], recv.at[3 * i + j],
                                                            device_id=(x, y, 1 - c), device_id_type=MESH))
        return cps

    def start_at(self, cond, cm):
        def go():
            remote, locals_ = self._copies(cm, False)
            for cp in locals_ + remote:
                cp.start()

        if self.n:
            go() if cond is True else pl.when(cond)(go)

    def wait_at(self, cond, cm):
        def go():
            for cp in self._copies(cm, True)[0]:
                cp.wait_recv()
            handed = self._handovers(cm, False)
            for cp in handed:
                cp.start()
            for cp in self._handovers(cm, True):
                cp.wait_recv()
            remote, locals_ = self._copies(cm, False)
            for cp in handed + remote:
                cp.wait_send()
            for cp in locals_:
                cp.wait()

        if self.n:
            go() if cond is True else pl.when(cond)(go)


def _comm_call(comm, name):
    def body(*refs):
        _, cm = comm.split(refs, 0, 0, 0)
        comm.start_at(True, cm)
        comm.wait_at(True, cm)

    return list(pl.pallas_call(
        body, name=name, in_specs=[ANY] * comm.n, out_specs=[ANY] * comm.n, out_shape=comm.out_shape(),
        scratch_shapes=comm.scratch(), compiler_params=pltpu.CompilerParams(has_side_effects=True),
    )(*comm.args()))


def _swap_sibling(arrs):
    n = len(arrs)

    def body(*refs):
        ins, outs, send, recv = refs[:n], refs[n:2 * n], refs[2 * n], refs[2 * n + 1]
        x, y, c = _coords()
        cps = [pltpu.make_async_remote_copy(ins[i], outs[i], send.at[i], recv.at[i], device_id=(x, y, 1 - c), device_id_type=MESH)
               for i in range(n)]
        for cp in cps:
            cp.start()
        for cp in cps:
            cp.wait_recv()
        for cp in cps:
            cp.wait_send()

    return list(pl.pallas_call(
        body, name="swap_sibling", in_specs=[ANY] * n, out_specs=[ANY] * n,
        out_shape=[jax.ShapeDtypeStruct(a.shape, a.dtype) for a in arrs],
        scratch_shapes=[pltpu.SemaphoreType.DMA((n,)), pltpu.SemaphoreType.DMA((n,))],
        compiler_params=pltpu.CompilerParams(has_side_effects=True),
    )(*arrs))


def _gather_small(gs):
    def body(g_ref, o_ref, send_sems, recv_sems, local_sem):
        x, y, c = _coords()
        me = 4 * x + 2 * y + c
        mine = pltpu.make_async_copy(g_ref, o_ref.at[me], local_sem)
        mine.start()
        sends = []
        for k in range(1, 8):
            px, py, pc = x ^ (k >> 2), y ^ ((k >> 1) & 1), c ^ (k & 1)
            sends.append((pltpu.make_async_remote_copy(g_ref, o_ref.at[me], send_sems.at[k - 1], recv_sems.at[k - 1],
                                                       device_id=(px, py, pc), device_id_type=MESH), 4 * px + 2 * py + pc, k))
        for cp, _, _ in sends:
            cp.start()
        for cp, src, k in sends:
            pltpu.make_async_remote_copy(g_ref, o_ref.at[src], send_sems.at[k - 1], recv_sems.at[k - 1],
                                         device_id=(x, y, c), device_id_type=MESH).wait_recv()
        for cp, _, _ in sends:
            cp.wait_send()
        mine.wait()

    return pl.pallas_call(
        body, name="gather_small", in_specs=[ANY], out_specs=ANY,
        out_shape=jax.ShapeDtypeStruct((8,) + gs.shape, gs.dtype),
        scratch_shapes=[pltpu.SemaphoreType.DMA((7,)), pltpu.SemaphoreType.DMA((7,)), pltpu.SemaphoreType.DMA],
        compiler_params=pltpu.CompilerParams(has_side_effects=True),
    )(gs)


def _sum_slots(own, others, name, tile):
    k, r, c = others.shape

    def body(*refs):
        if own is None:
            o_ref, out_ref = refs
            acc = o_ref[0].astype(F32)
            first = 1
        else:
            own_ref, o_ref, out_ref = refs
            acc = own_ref[...]
            first = 0
        for j in range(first, k):
            acc = acc + o_ref[j].astype(F32)
        out_ref[...] = acc

    row = pl.BlockSpec((tile, c), lambda i: (i, 0))
    specs = ([] if own is None else [row]) + [pl.BlockSpec((k, tile, c), lambda i: (0, i, 0))]
    args = ([] if own is None else [own]) + [others]
    return pl.pallas_call(body, name=name, grid=(r // tile,), in_specs=specs, out_specs=row,
                          out_shape=jax.ShapeDtypeStruct((r, c), F32), compiler_params=_cp("parallel"))(*args)


def _adamw(w, m, v, ga, gb, name, tile, rows_first=False):
    lead = 0 if rows_first else w.ndim - 2
    r, c = w.shape[-2:]

    def body(*refs):
        vals = [ref[0] if lead else ref[...] for ref in refs[:len(refs) - 4]]
        w_, m_, v_, g = vals[0], vals[1], vals[2], vals[3]
        if gb is not None:
            g = g + vals[4]
        nm = ADAM_B1 * m_ + (1.0 - ADAM_B1) * g
        nv = ADAM_B2 * v_ + (1.0 - ADAM_B2) * (g * g)
        d = -ADAM_LR * ((nm / BC1) / (jnp.sqrt(nv / BC2) + ADAM_EPS) + ADAM_WD * w_)
        for ref, val in zip(refs[len(refs) - 4:], (g, d, nm, nv)):
            if lead:
                ref[0] = val
            else:
                ref[...] = val

    if rows_first:
        row = pl.BlockSpec((tile,) + w.shape[1:], lambda i: (i, 0, 0))
        grid = (w.shape[0] // tile,)
    elif lead:
        row = pl.BlockSpec((1, tile, c), lambda l, i: (l, i, 0))
        grid = (w.shape[0], r // tile)
    else:
        row = pl.BlockSpec((tile, c), lambda i: (i, 0))
        grid = (r // tile,)
    args = [w, m, v, ga] + ([] if gb is None else [gb])
    return pl.pallas_call(body, name=name, grid=grid, in_specs=[row] * len(args), out_specs=[row] * 4,
                          out_shape=[jax.ShapeDtypeStruct(w.shape, F32)] * 4,
                          compiler_params=_cp(*(["parallel"] * len(grid))))(*args)


MATS = ("w_in", "w_out", "w_gate", "w_up", "w_down")
CONVS = ("ssd_conv_w", "lru_conv_w")
BIG = MATS + CONVS
TRANSPOSED = ("w_gate", "w_up")
COL_SHARDED = ("ssd_conv_w", "lru_conv_w")
W_IN_SHARD = IN_COLS // 4
W_IN_PAD = 1056
SMALL = ("norm_mix", "ssd_conv_b", "ssd_dt_bias", "ssd_a_log", "ssd_d", "ssd_norm", "lru_conv_b", "lru_wa", "lru_ba",
         "lru_wx", "lru_bx", "lru_lambda", "norm_ffn", "norm_final")
WEIGHTS = ("norm_mix", "w_in", "ssd_conv_w", "ssd_conv_b", "ssd_dt_bias", "ssd_a_log", "ssd_d", "ssd_norm", "lru_conv_w",
           "lru_conv_b", "lru_wa", "lru_ba", "lru_wx", "lru_bx", "lru_lambda", "w_out", "norm_ffn", "w_gate", "w_up",
           "w_down", "norm_final")
ROW_TILE = {"w_in": W_IN_SHARD, "w_out": 128, "w_gate": 352, "w_up": 352, "w_down": 352}
W_IN_ADAM_TILE = 54


def _pack(arrs, width, row_mult, dtype):
    flat = jnp.concatenate([a.reshape(-1).astype(dtype) for a in arrs])
    rows = -(-flat.shape[0] // width)
    rows = -(-rows // row_mult) * row_mult
    flat = jnp.pad(flat, (0, rows * width - flat.shape[0]))
    return flat.reshape(rows, width)


def _unpack(buf, shapes):
    flat = buf.reshape(-1)
    out, off = [], 0
    for shp in shapes:
        n = int(np.prod(shp))
        out.append(flat[off:off + n].reshape(shp))
        off += n
    return out


def _join(name, g4):
    if name in COL_SHARDED:
        return jnp.moveaxis(g4, 0, -2).reshape(g4.shape[1:-1] + (4 * g4.shape[-1],))
    return g4.reshape((4 * g4.shape[1],) + g4.shape[2:])


def _slabs(name, g):
    if name in COL_SHARDED:
        return jnp.moveaxis(g.reshape(g.shape[:-1] + (4, g.shape[-1] // 4)), -2, 0)
    return g.reshape((4, g.shape[0] // 4) + g.shape[1:])


def _w_in_rows(g4):
    def nat(lo, hi):
        out = []
        while lo < hi:
            j = lo // W_IN_SHARD
            stop = min(hi, (j + 1) * W_IN_SHARD)
            out.append((j, lo - j * W_IN_SHARD, stop - lo))
            lo = stop
        return out
    pieces = nat(0, 3072) + nat(3080, IN_COLS) + nat(3072, 3080)

    def body(g_ref, o_ref):
        row = 0
        for j, first, n in pieces:
            o_ref[row:row + n, :] = g_ref[j, first:first + n, :]
            row += n
        o_ref[row:, :] = jnp.zeros((NP - row, o_ref.shape[1]), o_ref.dtype)

    return pl.pallas_call(body, name="w_in_rows", out_shape=jax.ShapeDtypeStruct((NP, g4.shape[-1]), g4.dtype),
                          compiler_params=pltpu.CompilerParams(vmem_limit_bytes=VMEM_LIMIT))(g4)


def _w_in_slabs(gt):
    def kern(n):
        return n if n < 3072 else (C_DT + n - 3072 if n < 3080 else n - 8)
    slabs = []
    for j in range(4):
        lo, hi = j * W_IN_SHARD, (j + 1) * W_IN_SHARD
        cuts = sorted({lo, hi} | {c for c in (3072, 3080) if lo < c < hi})
        slabs.append(jnp.concatenate([gt[kern(a):kern(a) + b - a] for a, b in zip(cuts[:-1], cuts[1:])], axis=0))
    return jnp.stack(slabs, axis=0)


def _block_diag(w):
    eye = jnp.eye(LRU_BLOCKS, dtype=w.dtype)
    return jnp.einsum("ncd,nm->ncmd", w, eye).reshape(LRU_W, LRU_W)


def _block_diag_extract(g):
    g4 = g.reshape(LRU_BLOCKS, 64, LRU_BLOCKS, 64)
    return jnp.stack([g4[n, :, n, :] for n in range(LRU_BLOCKS)], axis=0)


def _lanes128(v):
    return jnp.pad(v, (0, BLK - v.shape[0])).reshape(1, BLK)


def _layer_mixers(x, p, comm=None, h=None):
    if h is None:
        h = _rms_fwd(x, p["norm_mix"], "rms_mix")
    proj = _mm(h, p["w_in_t"], tb=True, tm=1024, tn=1408, tk=1024, name="mm_in")
    att, lse, attb, got = _att_fwd_fused(proj, "att_fwd", comm)
    xconv, dt = _ssd_pre(proj, p["ssd_conv_w"], p["ssd_conv_b"], _lanes128(p["ssd_dt_bias"]), "ssd_pre")
    spar = jnp.concatenate([_lanes128(p["ssd_a_log"]), _lanes128(p["ssd_d"]), jnp.zeros((6, BLK), F32)], axis=0)
    y, states = _ssd_scan(xconv, dt, spar, "ssd_scan")
    ssd = _ssd_post(y, proj, p["ssd_norm"], "ssd_post")
    xc = _lru_conv(proj, p["lru_conv_w"], p["lru_conv_b"], "lru_conv")
    wab = jnp.concatenate([_block_diag(p["lru_wa"]), _block_diag(p["lru_wx"])], axis=1).astype(MXU)
    pre = _mm(xc, wab, tm=1024, tn=1024, tk=512, name="mm_lru")
    lpar = jnp.concatenate([p["lru_ba"].reshape(1, -1), p["lru_bx"].reshape(1, -1), p["lru_lambda"].reshape(1, -1),
                            jnp.zeros((5, LRU_W), F32)], axis=0)
    lru, hs = _lru_scan(pre, xc, proj, lpar, "lru_scan")
    mix = jnp.concatenate([attb, ssd, lru], axis=1)
    saved = dict(x=x, h=h, proj=proj, att=att, lse=lse, xconv=xconv, dt=dt, spar=spar, y=y, states=states, xc=xc, wab=wab,
                 pre=pre, lpar=lpar, hs=hs, mix=mix)
    return mix, saved, got


def _layer_ffn(x, mix, p, saved, comm=None, next_norm=None):
    x1, h2 = _mm(mix, p["w_out"], add=x, tm=1024, tn=1024, tk=1536, name="mm_out", epi=_epi_rms(p["norm_ffn"]))
    gu = _mm(h2, p["w_gu_t"], tb=True, out_dtype=MXU, tm=1024, tn=1408, tk=1024, name="mm_gu", comm=comm)
    gu, got = gu if comm is not None else (gu, [])
    act = _swiglu_act(gu, "swiglu_act")
    x2 = _mm(act, p["w_down"], add=x1, tm=1024, tn=1024, tk=2816, name="mm_down",
             epi=None if next_norm is None else _epi_rms(next_norm))
    x2, h_next = x2 if next_norm is not None else (x2, None)
    saved.update(x1=x1, h2=h2, gu=gu, act=act)
    return x2, got, h_next


def _layer_bwd(dx2, p, sv, comm_ssd=None, comm_att=None, comm_tail=None):
    g = {}
    da = _mm(dx2, p["w_down"], tb=True, out_dtype=MXU, tm=1024, tn=1408, tk=1024, name="mm_d_act")
    g["w_down"] = _mm(sv["act"], dx2, ta=True, tm=1408, tn=1024, tk=1024, name="mm_g_down")
    dgu = _swiglu_bwd(sv["gu"], da, "swiglu_bwd")
    dx1, gn = _mm(dgu, p["w_gu_t"], tm=1024, tn=1024, tk=1408, name="mm_d_h2", epi=_epi_rms_bwd(sv["x1"], p["norm_ffn"], dx2))
    g["w_gu_t"] = _mm(dgu, sv["h2"], ta=True, tm=1408, tn=1024, tk=1024, name="mm_g_gu")
    g["norm_ffn"] = jnp.sum(gn, axis=0)
    dmix, delta = _mm(dx1, p["w_out"], tb=True, tm=1024, tn=1536, tk=1024, name="mm_d_mix", epi=_epi_att_delta(sv["att"]))
    g["w_out"] = _mm(sv["mix"], dx1, ta=True, tm=1536, tn=1024, tk=1024, name="mm_g_out")
    proj = sv["proj"]
    dpre, dxc_u, dgl, dlpar = _lru_scan_bwd(sv["pre"], sv["xc"], proj, sv["lpar"], sv["hs"], dmix, "lru_scan_bwd")
    dxc = _mm(dpre, sv["wab"], tb=True, add=dxc_u, tm=1024, tn=512, tk=1024, name="mm_d_xc")
    gwab = _mm(sv["xc"], dpre, ta=True, tm=512, tn=1024, tk=1024, name="mm_g_lru")
    g["lru_wa"], g["lru_wx"] = _block_diag_extract(gwab[:, :LRU_W]), _block_diag_extract(gwab[:, LRU_W:])
    g["lru_ba"], g["lru_bx"], g["lru_lambda"] = dlpar[0], dlpar[1], dlpar[2]
    dxl, gcw, gcb = _lru_conv_bwd(proj, dxc, p["lru_conv_w"], "lru_conv_bwd")
    g["lru_conv_w"], g["lru_conv_b"] = gcw[:CONV_K], jnp.sum(gcb, axis=0)
    dy, dz, gsn = _ssd_post_bwd(sv["y"], proj, p["ssd_norm"], (dmix, SSD_W, 1), "ssd_post_bwd")
    g["ssd_norm"] = jnp.sum(gsn, axis=0)
    dxconv, ddt, dal, ddk, got_ssd = _ssd_scan_bwd(sv["xconv"], sv["dt"], sv["spar"], sv["states"], dy, "ssd_scan_bwd", comm_ssd)
    g["ssd_a_log"], g["ssd_d"] = dal[0, :8], ddk[0, :8]
    dxbc, ddtr, gsw, gsb, gdb = _ssd_pre_bwd(proj, dxconv, ddt, p["ssd_conv_w"], p["ssd_conv_b"],
                                             _lanes128(p["ssd_dt_bias"]), "ssd_pre_bwd")
    g["ssd_conv_w"], g["ssd_conv_b"], g["ssd_dt_bias"] = gsw[:CONV_K], jnp.sum(gsb, axis=0), jnp.sum(gdb, axis=0)[:8]
    dq, dk, dv, got_att = _att_bwd_rev(proj, dmix, sv["lse"], delta, "att_bwd", None if comm_att is None else comm_att(g))
    dproj = jnp.concatenate([dq, dk, dv, dz, dxbc, dgl, dxl, ddtr], axis=1)
    g["w_in_t"] = _mm(dproj, sv["h"], ta=True, tm=1408, tn=1024, tk=1024, name="mm_g_in")
    res = _mm(dproj, p["w_in_t"], tm=1024, tn=1024, tk=1408, name="mm_d_h", comm=None if comm_tail is None else comm_tail(g),
              epi=_epi_rms_bwd(sv["x"], p["norm_mix"], dx1))
    (dx, gm), got_tail = res if comm_tail is not None else (res, [])
    g["norm_mix"] = jnp.sum(gm, axis=0)
    return dx, g, got_ssd, got_att, got_tail


def _grad_slabs(g, names):
    out = {}
    for n in names:
        if n == "w_in":
            out[n] = _w_in_slabs(g["w_in_t"])
        elif n == "w_gate":
            out[n] = _slabs(n, g["w_gu_t"][:D_FF])
        elif n == "w_up":
            out[n] = _slabs(n, g["w_gu_t"][D_FF:])
        else:
            out[n] = _slabs(n, g[n])
    return out


def kernel(x, norm_mix, w_in, ssd_conv_w, ssd_conv_b, ssd_dt_bias, ssd_a_log, ssd_d, ssd_norm, lru_conv_w, lru_conv_b, lru_wa, lru_ba, lru_wx, lru_bx, lru_lambda, w_out, norm_ffn, w_gate, w_up, w_down, norm_final, loss_target, m_norm_mix, m_w_in, m_ssd_conv_w, m_ssd_conv_b, m_ssd_dt_bias, m_ssd_a_log, m_ssd_d, m_ssd_norm, m_lru_conv_w, m_lru_conv_b, m_lru_wa, m_lru_ba, m_lru_wx, m_lru_bx, m_lru_lambda, m_w_out, m_norm_ffn, m_w_gate, m_w_up, m_w_down, m_norm_final, v_norm_mix, v_w_in, v_ssd_conv_w, v_ssd_conv_b, v_ssd_dt_bias, v_ssd_a_log, v_ssd_d, v_ssd_norm, v_lru_conv_w, v_lru_conv_b, v_lru_wa, v_lru_ba, v_lru_wx, v_lru_bx, v_lru_lambda, v_w_out, v_norm_ffn, v_w_gate, v_w_up, v_w_down, v_norm_final):
    loc = dict(locals())
    w = {n: loc[n] for n in WEIGHTS}
    m = {n: loc["m_" + n] for n in WEIGHTS}
    v = {n: loc["v_" + n] for n in WEIGHTS}
    for n in TRANSPOSED:
        w[n], m[n], v[n] = [jnp.transpose(t, (0, 2, 1)) for t in (w[n], m[n], v[n])]
    wt_in, mt_in, vt_in = [jnp.transpose(t, (2, 0, 1)) for t in (w["w_in"], m["w_in"], v["w_in"])]

    def halves(a):
        return a.reshape(a.shape[0], 2, a.shape[1] // 2, a.shape[2])

    def unhalve(a):
        return a.reshape(4, 2 * a.shape[2], a.shape[3])

    def joined(name, a):
        return _w_in_rows(unhalve(a)) if name == "w_in" else _join(name, unhalve(a))

    wb = {n: halves(w[n].astype(MXU)) for n in MATS[1:]}
    wb["w_in"] = halves(jnp.pad(jnp.transpose(wt_in.astype(MXU), (1, 0, 2)), ((0, 0), (0, W_IN_PAD - W_IN_SHARD), (0, 0))))
    xs = x[0]
    h0, first = _rms_fwd(xs, norm_mix[0], "rms_mix", _Comm(gathers=[(wb["w_in"], 0, True), (w["ssd_conv_w"], None, False),
                                                                    (w["lru_conv_w"], None, False)]))
    convs = {"ssd_conv_w": _join("ssd_conv_w", first[1]), "lru_conv_w": _join("lru_conv_w", first[2])}
    behind_att = [(n, 0) for n in MATS[1:]] + [("w_in", 1)]
    behind_ffn = [(n, 1) for n in MATS[1:]]
    whole = {("w_in", 0): joined("w_in", first[0])}
    params = {}

    def layer_params(l):
        if l not in params:
            p = {n: w[n][l] for n in SMALL if n != "norm_final"}
            p.update(w_in_t=whole["w_in", l], ssd_conv_w=convs["ssd_conv_w"][l], lru_conv_w=convs["lru_conv_w"][l])
            params[l] = p
        if "w_out" not in params[l] and ("w_out", l) in whole:
            params[l].update(w_out=whole["w_out", l], w_down=whole["w_down", l],
                             w_gu_t=jnp.concatenate([whole["w_gate", l], whole["w_up", l]], axis=0))
        return params[l]

    saved = []
    h_in = h0
    for l in range(DEPTH):
        first_layer = l == 0
        mix, sv, got = _layer_mixers(xs, layer_params(l), _Comm(gathers=[(wb[n], k, True) for n, k in behind_att]) if first_layer else None,
                                     h_in)
        whole.update({k: joined(k[0], a) for k, a in zip(behind_att, got)})
        xs, got, h_in = _layer_ffn(xs, mix, layer_params(l), sv, _Comm(gathers=[(wb[n], k, True) for n, k in behind_ffn]) if first_layer else None,
                                   norm_mix[l + 1] if l + 1 < DEPTH else None)
        whole.update({k: joined(k[0], a) for k, a in zip(behind_ffn, got)})
        saved.append(sv)
    dx, gnf, lsum = _loss_head(xs, norm_final, loss_target[0], "loss_head")
    loss = lax.psum(jnp.sum(lsum), ("x", "y", "c"))

    dx, g1, _, _, _ = _layer_bwd(dx, layer_params(1), saved[1])
    s1 = _grad_slabs(g1, BIG)
    att0 = ("w_gate", "w_up", "w_down", "w_out")
    s0 = {}

    def wire(s, n):
        return s[n].astype(MXU) if n in MATS else s[n]

    def comm_att(g0):
        s0.update(_grad_slabs(g0, att0))
        return _Comm(scatters=[wire(s0, n) for n in att0])

    tail0 = ("w_in",) + CONVS

    def comm_tail(g0):
        s0.update(_grad_slabs(g0, tail0))
        return _Comm(scatters=[wire(s0, n) for n in tail0])

    dx, g0, got_ssd, got_att, got_tail = _layer_bwd(dx, layer_params(0), saved[0], _Comm(scatters=[wire(s1, n) for n in BIG]),
                                                    comm_att, comm_tail)
    recv = {(n, 1): a for n, a in zip(BIG, got_ssd)}
    recv.update({(n, 0): a for n, a in zip(att0, got_att)})
    recv.update({(n, 0): a for n, a in zip(tail0, got_tail)})

    me = 2 * lax.axis_index("x") + lax.axis_index("y")
    slabs = (s0, s1)
    part = {}
    for n in BIG:
        per_layer = []
        for l in range(DEPTH):
            own = lax.dynamic_index_in_dim(slabs[l][n], me, axis=0, keepdims=False)
            per_layer.append(_sum_slots(own, recv[n, l], "sum_chips_" + n, ROW_TILE.get(n, own.shape[0])))
        part[n] = jnp.stack(per_layer, axis=0)
    sib = dict(zip(BIG, _swap_sibling([part[n] for n in BIG])))
    out_g, out_d, out_m, out_v = {}, {}, {}, {}
    for n in BIG:
        if n == "w_in":
            res = _adamw(wt_in, mt_in, vt_in, jnp.transpose(part[n], (1, 0, 2)), jnp.transpose(sib[n], (1, 0, 2)), "adamw_" + n,
                         W_IN_ADAM_TILE, rows_first=True)
            out_g[n], out_d[n], out_m[n], out_v[n] = [jnp.transpose(t, (1, 2, 0)) for t in res]
            continue
        res = _adamw(w[n], m[n], v[n], part[n], sib[n], "adamw_" + n, ROW_TILE.get(n, w[n].shape[1]))
        out_g[n], out_d[n], out_m[n], out_v[n] = [jnp.transpose(t, (0, 2, 1)) for t in res] if n in TRANSPOSED else res

    gsm = {n: jnp.stack([g0[n], g1[n]], axis=0) for n in SMALL if n != "norm_final"}
    gsm["norm_final"] = jnp.sum(gnf, axis=0)
    small_shapes = [w[n].shape for n in SMALL]
    gs = _pack([gsm[n].reshape(w[n].shape) for n in SMALL], BLK, 8, F32)
    gall = _gather_small(gs)
    gsum = _sum_slots(None, gall, "sum_devices", gs.shape[0])
    ws = _pack([w[n] for n in SMALL], BLK, 8, F32)
    ms = _pack([m[n] for n in SMALL], BLK, 8, F32)
    vs = _pack([v[n] for n in SMALL], BLK, 8, F32)
    gsr, dsr, nms, nvs = _adamw(ws, ms, vs, gsum, None, "adamw_small", gs.shape[0])
    out_g.update(zip(SMALL, _unpack(gsr, small_shapes)))
    out_d.update(zip(SMALL, _unpack(dsr, small_shapes)))
    out_m.update(zip(SMALL, _unpack(nms, small_shapes)))
    out_v.update(zip(SMALL, _unpack(nvs, small_shapes)))

    return (loss, dx[None], *[out_g[n] for n in WEIGHTS], *[out_d[n] for n in WEIGHTS],
            *[out_m[n] for n in WEIGHTS], *[out_v[n] for n in WEIGHTS])
```

```python
import functools
import math

import jax
import jax.numpy as jnp
import numpy as np
from jax import lax
from jax.experimental import pallas as pl
from jax.experimental.pallas import tpu as pltpu

F32 = jnp.float32
MXU = jnp.bfloat16
HI = lax.Precision.HIGHEST
HIGH = lax.Precision.HIGH
MESH = pl.DeviceIdType.MESH

D_MODEL = 1024
DEPTH = 2
HEAD_DIM = 64
ATT_W = 512
ATT_PATTERNS = ((128, 1), (512, 4), (2048, 16))
BLK = 128
SSD_W = 512
SSD_STATE = 128
LRU_W = 512
LRU_BLOCKS = 8
LRU_C = 8.0
CONV_K = 4
D_MIX = 1536
D_FF = 2816
IN_COLS = 4104
NP = 4224
NORM_EPS = 1e-6
SSD_NORM_EPS = 1e-5
LN2 = math.log(2.0)
NEG = -1e30

ADAM_LR, ADAM_B1, ADAM_B2, ADAM_EPS, ADAM_WD, ADAM_STEP = 0.001, 0.9, 0.999, 1e-08, 0.01, 10
BC1 = 1.0 - ADAM_B1 ** ADAM_STEP
BC2 = 1.0 - ADAM_B2 ** ADAM_STEP

VMEM_LIMIT = 56 * 1024 * 1024

C_Q, C_K, C_V, C_Z, C_XBC, C_G, C_XL, C_DT = 0, 512, 1024, 1536, 2048, 3072, 3584, 4096


def _cp(*sem):
    return pltpu.CompilerParams(dimension_semantics=sem, vmem_limit_bytes=VMEM_LIMIT)


def _dot(a, b, dims, prec=None):
    return lax.dot_general(a, b, (dims, ((), ())), preferred_element_type=F32, precision=prec)


def _nn(a, b, prec=None):
    return _dot(a, b, ((1,), (0,)), prec)


def _nt(a, b, prec=None):
    return _dot(a, b, ((1,), (1,)), prec)


def _tn(a, b, prec=None):
    return _dot(a, b, ((0,), (0,)), prec)


def _sigmoid(x):
    return jax.nn.sigmoid(x)


def _silu(x):
    return x * _sigmoid(x)


def _softplus(x):
    return jnp.maximum(x, 0.0) + jnp.log(1.0 + jnp.exp(-jnp.abs(x)))


def _gelu(x):
    return 0.5 * x * (1.0 + jnp.tanh(0.7978845608028654 * (x + 0.044715 * x * x * x)))


def _mm(a, b, *, ta=False, tb=False, add=None, out_dtype=F32, tm, tn, tk, name, comm=None, epi=None):
    m, k = (a.shape[1], a.shape[0]) if ta else a.shape
    n = b.shape[0] if tb else b.shape[1]
    assert (b.shape[1] if tb else b.shape[0]) == k
    assert m % tm == 0 and n % tn == 0 and k % tk == 0, (name, m, n, k)
    nk = k // tk
    a_spec = pl.BlockSpec((tk, tm), lambda i, j, kk: (kk, i)) if ta else pl.BlockSpec((tm, tk), lambda i, j, kk: (i, kk))
    b_spec = pl.BlockSpec((tn, tk), lambda i, j, kk: (j, kk)) if tb else pl.BlockSpec((tk, tn), lambda i, j, kk: (kk, j))
    o_spec = pl.BlockSpec((tm, tn), lambda i, j, kk: (i, j))
    dims = ((0 if ta else 1,), (1 if tb else 0,))
    carried = comm is not None
    comm = comm or _Comm()
    ni, nj = m // tm, n // tn
    efn, erows, econsts, eouts, eaccs = epi or (None, [], [], [], [])
    assert epi is None or nj == 1
    nadd = 0 if add is None else 1
    ner, nec, neo, nea = len(erows), len(econsts), len(eouts), len(eaccs)

    def body(*refs):
        refs, cm = comm.split(refs, 2 + nadd + ner + nec, 1 + neo + nea, 1)
        a_ref, b_ref = refs[:2]
        er_refs = refs[2 + nadd:2 + nadd + ner]
        ec_refs = refs[2 + nadd + ner:2 + nadd + ner + nec]
        o_ref = refs[2 + nadd + ner + nec]
        eo_refs = refs[3 + nadd + ner + nec:3 + nadd + ner + nec + neo]
        ea_refs = refs[3 + nadd + ner + nec + neo:3 + nadd + ner + nec + neo + nea]
        acc = refs[-1]
        i, j, kk = pl.program_id(0), pl.program_id(1), pl.program_id(2)
        comm.start_at((i == 0) & (j == 0) & (kk == 0), cm)

        @pl.when(kk == 0)
        def _():
            acc[...] = jnp.zeros_like(acc)

        acc[...] += _dot(a_ref[...].astype(MXU), b_ref[...].astype(MXU), dims)

        @pl.when(kk == nk - 1)
        def _():
            r = acc[...]
            if add is not None:
                r = r + refs[2][...]
            if efn is None:
                o_ref[...] = r.astype(out_dtype)
            else:
                main, extra, sums = efn(r, [t[...] for t in er_refs], [t[...] for t in ec_refs])
                o_ref[...] = main.astype(out_dtype)
                for t, val in zip(eo_refs, extra):
                    t[...] = val.astype(t.dtype)
                @pl.when(i == 0)
                def _():
                    for t, val in zip(ea_refs, sums):
                        t[...] = val

                @pl.when(i > 0)
                def _():
                    for t, val in zip(ea_refs, sums):
                        t[...] += val

        comm.wait_at((i == ni - 1) & (j == nj - 1) & (kk == nk - 1), cm)

    def whole_rows(width):
        return pl.BlockSpec((tm, width), lambda i, j, kk: (i, 0))

    ins = [a, b] + ([] if add is None else [add]) + list(erows) + list(econsts)
    specs = [a_spec, b_spec] + ([] if add is None else [o_spec]) + [whole_rows(t.shape[1]) for t in erows]
    specs += [pl.BlockSpec(t.shape, lambda i, j, kk: (0, 0)) for t in econsts]
    out_specs = [o_spec] + [whole_rows(wd) for wd, _ in eouts] + [pl.BlockSpec((r, wd), lambda i, j, kk: (0, 0)) for r, wd in eaccs]
    out_shape = [jax.ShapeDtypeStruct((m, n), out_dtype)] + [jax.ShapeDtypeStruct((m, wd), dt) for wd, dt in eouts]
    out_shape += [jax.ShapeDtypeStruct((r, wd), F32) for r, wd in eaccs]
    serial = comm.n or nea
    res = pl.pallas_call(
        body, name=name, grid=(ni, nj, nk), in_specs=specs + [ANY] * comm.n, out_specs=out_specs + [ANY] * comm.n,
        out_shape=out_shape + comm.out_shape(),
        scratch_shapes=[pltpu.VMEM((tm, tn), F32)] + comm.scratch(),
        compiler_params=_cp(*((["arbitrary"] * 3) if serial else ["parallel", "parallel", "arbitrary"])),
    )(*ins, *comm.args())
    nown = 1 + neo + nea
    own = res[0] if epi is None else list(res[:nown])
    return (own, list(res[nown:])) if carried else own


def _rows(fn, rows, consts=(), outs=(), accs=(), *, tile, name, halos=(), comm=None):
    rows = [r if isinstance(r, tuple) else (r, r.shape[1], 0) for r in rows]
    s = rows[0][0].shape[0]
    assert s % tile == 0 and tile % 8 == 0
    n = s // tile
    t8 = tile // 8
    nr, nh, nc_, no, na = len(rows), len(halos), len(consts), len(outs), len(accs)
    carried = comm is not None
    comm = comm or _Comm()

    def body(*refs):
        refs, cm = comm.split(refs, nr + nh + nc_, no + na, 0)
        i = pl.program_id(0)
        comm.start_at(i == 0, cm)
        rv = [r[...] for r in refs[:nr]]
        hv = []
        for (idx, kind), r in zip(halos, refs[nr:nr + nh]):
            edge = (i == 0) if kind == "prev" else (i == n - 1)
            hv.append(jnp.where(edge, 0.0, r[...]))
        cv = [r[...] for r in refs[nr + nh:nr + nh + nc_]]
        o_refs = refs[nr + nh + nc_:nr + nh + nc_ + no]
        a_refs = refs[nr + nh + nc_ + no:]
        ov, av = fn(rv, hv, cv)
        for r, v in zip(o_refs, ov):
            r[...] = v.astype(r.dtype)
        if na:
            @pl.when(i == 0)
            def _():
                for r in a_refs:
                    r[...] = jnp.zeros_like(r)
            for r, v in zip(a_refs, av):
                r[...] += v
        comm.wait_at(i == n - 1, cm)

    in_specs = [pl.BlockSpec((tile, w), functools.partial(lambda i, cb: (i, cb), cb=cb)) for (_, w, cb) in rows]
    for idx, kind in halos:
        _, w, cb = rows[idx]
        if kind == "prev":
            in_specs.append(pl.BlockSpec((8, w), functools.partial(lambda i, cb: (jnp.maximum(i * t8 - 1, 0), cb), cb=cb)))
        else:
            in_specs.append(pl.BlockSpec((8, w), functools.partial(lambda i, cb: (jnp.minimum((i + 1) * t8, n * t8 - 1), cb), cb=cb)))
    in_specs += [pl.BlockSpec(c.shape, functools.partial(lambda i, nd: (0,) * nd, nd=c.ndim)) for c in consts]
    out_specs = [pl.BlockSpec((tile, c), lambda i: (i, 0)) for (c, _) in outs]
    out_specs += [pl.BlockSpec((r, c), lambda i: (0, 0)) for (r, c) in accs]
    out_shape = [jax.ShapeDtypeStruct((s, c), dt) for (c, dt) in outs]
    out_shape += [jax.ShapeDtypeStruct((r, c), F32) for (r, c) in accs]
    args = [r[0] for r in rows] + [rows[idx][0] for idx, _ in halos] + list(consts)
    res = pl.pallas_call(
        body, name=name, grid=(n,), in_specs=in_specs + [ANY] * comm.n, out_specs=out_specs + [ANY] * comm.n,
        out_shape=out_shape + comm.out_shape(), scratch_shapes=comm.scratch(), compiler_params=_cp("arbitrary"),
    )(*args, *comm.args())
    return (list(res[:no + na]), list(res[no + na:])) if carried else list(res)


def _colsum8(v):
    t, c = v.shape
    return jnp.sum(v.reshape(t // 8, 8, c), axis=0)


def _rms(x, g):
    return x * lax.rsqrt(jnp.mean(x * x, axis=-1, keepdims=True) + NORM_EPS) * g


def _epi_rms(g):
    return (lambda r, rows, consts: (r, [_rms(r, consts[0])], []), [], [g.reshape(1, -1)], [(g.shape[-1], MXU)], [])


def _epi_rms_bwd(x, g, dres):
    def fn(r, rows, consts):
        xb, drb = rows
        _, vjp = jax.vjp(_rms, xb, consts[0])
        rstd = lax.rsqrt(jnp.mean(xb * xb, axis=-1, keepdims=True) + NORM_EPS)
        return drb + vjp(r)[0], [], [_colsum8(r * xb * rstd)]
    return (fn, [x, dres], [g.reshape(1, -1)], [], [(8, g.shape[-1])])


def _epi_att_delta(att):
    def fn(r, rows, consts):
        hr = lax.broadcasted_iota(jnp.int32, (ATT_W, ATT_W), 0) // HEAD_DIM
        hc = lax.broadcasted_iota(jnp.int32, (ATT_W, ATT_W), 1) // HEAD_DIM
        return r, [_nn(r[:, :ATT_W] * rows[0], (hr == hc).astype(F32), HIGH)], []
    return (fn, [att], [], [(ATT_W, F32)], [])


def _epi_wire(width):
    return (lambda r, rows, consts: (r, [r], []), [], [], [(width, MXU)], [])


def _rms_fwd(x, g, name, comm=None):
    def fn(rv, hv, cv):
        return [_rms(rv[0], cv[0])], []
    res = _rows(fn, [x], [g.reshape(1, -1)], [(x.shape[1], MXU)], tile=512, name=name, comm=comm)
    return res[0] if comm is None else (res[0][0], res[1])


def _rms_bwd(x, g, dh, dres, name):
    def fn(rv, hv, cv):
        xb, dhb, drb = rv
        _, vjp = jax.vjp(_rms, xb, cv[0])
        dx, _ = vjp(dhb)
        rstd = lax.rsqrt(jnp.mean(xb * xb, axis=-1, keepdims=True) + NORM_EPS)
        return [drb + dx], [_colsum8(dhb * xb * rstd)]
    d = x.shape[1]
    return _rows(fn, [x, dh, dres], [g.reshape(1, -1)], [(d, F32)], [(8, d)], tile=512, name=name)


def _slope_dist(hp, hh, dist, dil):
    hf = (2 * hp + hh + 1).astype(F32)
    slope = jnp.exp(jnp.zeros(dist.shape, F32) - hf * LN2)
    return slope * (dist.astype(F32) * float(dil))


def _att_delta(datt, att, name):
    def fn(rv, hv, cv):
        r = lax.broadcasted_iota(jnp.int32, (ATT_W, ATT_W), 0) // HEAD_DIM
        c = lax.broadcasted_iota(jnp.int32, (ATT_W, ATT_W), 1) // HEAD_DIM
        ones = (r == c).astype(F32)
        return [_nn(rv[0] * rv[1], ones, HI)], []
    return _rows(fn, [datt, att], [], [(ATT_W, F32)], tile=512, name=name)[0]


ATT_G = 2048


def _deinterleave(dst, src, dil, ld, region, offset):
    for r in range(dil):
        rows = pl.ds(r, ld, stride=dil) if dil > 1 else pl.ds(0, ld)
        dst[r * region + offset:r * region + offset + ld, :] = src[rows, :]


def _deinterleave_edge(dst, src, dil, region, offset, first_row):
    for r in range(dil):
        rows = pl.ds(first_row + r, BLK, stride=dil) if dil > 1 else pl.ds(first_row, BLK)
        dst[r * region + offset:r * region + offset + BLK, :] = src[rows, :]


def _att_fwd_fused(proj, name, comm=None):
    s, npc = proj.shape
    gsz = ATT_G
    ng = s // gsz
    assert s % gsz == 0
    scale = HEAD_DIM ** -0.5
    comm = comm or _Comm()

    def body(*refs):
        (q_ref, kp_ref, kc_ref, vp_ref, vc_ref, att_ref, lse_ref, attb_ref, qd, kd, vd, nd, md, dd, nn, mn, dn), cm = comm.split(refs, 5, 3, 9)
        hp, g = pl.program_id(0), pl.program_id(1)
        comm.start_at((hp == 0) & (g == 0), cm)
        lane = lax.broadcasted_iota(jnp.int32, (BLK, BLK), 1)
        qi = lax.broadcasted_iota(jnp.int32, (BLK, 2 * BLK), 0)
        ki = lax.broadcasted_iota(jnp.int32, (BLK, 2 * BLK), 1)
        dist = BLK + qi - ki
        band = (dist >= 0) & (dist <= BLK)
        for pi, (_, dil) in enumerate(ATT_PATTERNS):
            ld = gsz // dil
            nbg = ld // BLK
            _deinterleave(qd, q_ref, dil, ld, ld, 0)
            _deinterleave(kd, kc_ref, dil, ld, ld + BLK, BLK)
            _deinterleave(vd, vc_ref, dil, ld, ld + BLK, BLK)
            _deinterleave_edge(kd, kp_ref, dil, ld + BLK, 0, gsz - BLK * dil)
            _deinterleave_edge(vd, vp_ref, dil, ld + BLK, 0, gsz - BLK * dil)
            bias = [_slope_dist(hp, hh, dist, dil) for hh in (0, 1)]

            def tile(t, carry, ld=ld, nbg=nbg, bias=bias):
                r, b = t // nbg, t % nbg
                qo = pl.multiple_of(r * ld + b * BLK, BLK)
                ko = pl.multiple_of(r * (ld + BLK) + b * BLK, BLK)
                q = qd[pl.ds(qo, BLK), :]
                kk = kd[pl.ds(ko, 2 * BLK), :].astype(MXU)
                vv = vd[pl.ds(ko, 2 * BLK), :].astype(MXU)
                valid = band & ((g > 0) | (b > 0) | (ki >= BLK))
                num = jnp.zeros((BLK, BLK), F32)
                mx = jnp.zeros((BLK, BLK), F32)
                den = jnp.zeros((BLK, BLK), F32)
                for hh in (0, 1):
                    hmask = (lane < HEAD_DIM) if hh == 0 else (lane >= HEAD_DIM)
                    qm = jnp.where(hmask, q, 0.0).astype(MXU)
                    sc = jnp.where(valid, _nt(qm, kk) * scale - bias[hh], NEG)
                    m = jnp.max(sc, axis=1, keepdims=True)
                    p = jnp.exp(sc - m)
                    dn_ = jnp.sum(p, axis=1, keepdims=True)
                    o = _nn(p.astype(MXU), vv)
                    num = jnp.where(hmask, o, num)
                    mx = jnp.where(hmask, m, mx)
                    den = jnp.where(hmask, dn_, den)
                nd[pl.ds(qo, BLK), :] = num
                md[pl.ds(qo, BLK), :] = mx
                dd[pl.ds(qo, BLK), :] = den
                return carry

            lax.fori_loop(0, dil * nbg, tile, 0, unroll=8)
            for r in range(dil):
                rows = pl.ds(r, ld, stride=dil) if dil > 1 else pl.ds(0, ld)
                nn.at[pi][rows, :] = nd[r * ld:(r + 1) * ld, :]
                mn.at[pi][rows, :] = md[r * ld:(r + 1) * ld, :]
                dn.at[pi][rows, :] = dd[r * ld:(r + 1) * ld, :]

        def merge(c, carry):
            rows = pl.ds(pl.multiple_of(c * 256, 256), 256)
            ms = [mn[pi, rows, :] for pi in range(len(ATT_PATTERNS))]
            m_all = functools.reduce(jnp.maximum, ms)
            num = jnp.zeros((256, BLK), F32)
            den = jnp.zeros((256, BLK), F32)
            for pi in range(len(ATT_PATTERNS)):
                e = jnp.exp(ms[pi] - m_all)
                num = num + nn[pi, rows, :] * e
                den = den + dn[pi, rows, :] * e
            att = num / den
            att_ref[rows, :] = att
            attb_ref[rows, :] = att.astype(MXU)
            lse_ref[rows, :] = m_all + jnp.log(den)
            return carry

        lax.fori_loop(0, gsz // 256, merge, 0)
        comm.wait_at((hp == 3) & (g == ng - 1), cm)

    def cur(base):
        return pl.BlockSpec((gsz, BLK), lambda hp, g: (g, base // BLK + hp))

    def prev(base):
        return pl.BlockSpec((gsz, BLK), lambda hp, g: (jnp.maximum(g - 1, 0), base // BLK + hp))

    o_spec = pl.BlockSpec((gsz, BLK), lambda hp, g: (g, hp))
    npat = len(ATT_PATTERNS)
    res = pl.pallas_call(
        body, name=name, grid=(4, ng),
        in_specs=[cur(C_Q), prev(C_K), cur(C_K), prev(C_V), cur(C_V)] + [ANY] * comm.n,
        out_specs=[o_spec] * 3 + [ANY] * comm.n,
        out_shape=[jax.ShapeDtypeStruct((s, ATT_W), F32)] * 2 + [jax.ShapeDtypeStruct((s, ATT_W), MXU)] + comm.out_shape(),
        scratch_shapes=[pltpu.VMEM((gsz, BLK), F32), pltpu.VMEM((2 * gsz, BLK), F32), pltpu.VMEM((2 * gsz, BLK), F32)]
        + [pltpu.VMEM((gsz, BLK), F32)] * 3 + [pltpu.VMEM((npat, gsz, BLK), F32)] * 3 + comm.scratch(),
        compiler_params=_cp("arbitrary", "arbitrary"),
    )(proj, proj, proj, proj, proj, *comm.args())
    return res[0], res[1], res[2], list(res[3:])


def _att_bwd_fused(proj, datt, lse, delta, name, comm=None):
    s, npc = proj.shape
    gsz = ATT_G
    ng = s // gsz
    scale = HEAD_DIM ** -0.5
    comm = comm or _Comm()

    def body(*refs):
        (qc_ref, qn_ref, kp_ref, kc_ref, vp_ref, vc_ref, doc_ref, don_ref, lsc_ref, lsn_ref, dlc_ref, dln_ref,
         dq_ref, dk_ref, dv_ref, qd, dod, lsd, dld, kd, vd, dqd, dkd, dvd), cm = comm.split(refs, 12, 3, 9)
        hp, g = pl.program_id(0), pl.program_id(1)
        comm.start_at((hp == 0) & (g == 0), cm)
        lane = lax.broadcasted_iota(jnp.int32, (BLK, BLK), 1)
        qi = lax.broadcasted_iota(jnp.int32, (BLK, BLK), 0)
        ki = lax.broadcasted_iota(jnp.int32, (BLK, BLK), 1)
        d_far = BLK + qi - ki
        d_near = qi - ki
        for pi, (_, dil) in enumerate(ATT_PATTERNS):
            ld = gsz // dil
            nbg = ld // BLK
            reg = ld + BLK
            for dst, c_ref, n_ref in ((qd, qc_ref, qn_ref), (dod, doc_ref, don_ref), (lsd, lsc_ref, lsn_ref), (dld, dlc_ref, dln_ref)):
                _deinterleave(dst, c_ref, dil, ld, reg, 0)
                _deinterleave_edge(dst, n_ref, dil, reg, ld, 0)
            for dst, p_ref, c_ref in ((kd, kp_ref, kc_ref), (vd, vp_ref, vc_ref)):
                _deinterleave(dst, c_ref, dil, ld, reg, BLK)
                _deinterleave_edge(dst, p_ref, dil, reg, 0, gsz - BLK * dil)
            b_far = [_slope_dist(hp, hh, d_far, dil) for hh in (0, 1)]
            b_near = [_slope_dist(hp, hh, d_near, dil) for hh in (0, 1)]

            def tile(t, carry, ld=ld, nbg=nbg, reg=reg, b_far=b_far, b_near=b_near):
                r, b = t // nbg, t % nbg
                oo = pl.multiple_of(r * ld + b * BLK, BLK)
                ro = pl.multiple_of(r * reg + b * BLK, BLK)
                qn, qx = qd[pl.ds(ro, BLK), :], qd[pl.ds(ro + BLK, BLK), :]
                don, dox = dod[pl.ds(ro, BLK), :], dod[pl.ds(ro + BLK, BLK), :]
                lsn, lsx = lsd[pl.ds(ro, BLK), :], lsd[pl.ds(ro + BLK, BLK), :]
                dln, dlx = dld[pl.ds(ro, BLK), :], dld[pl.ds(ro + BLK, BLK), :]
                kp, kc = kd[pl.ds(ro, BLK), :].astype(MXU), kd[pl.ds(ro + BLK, BLK), :].astype(MXU)
                vp, vc = vd[pl.ds(ro, BLK), :].astype(MXU), vd[pl.ds(ro + BLK, BLK), :].astype(MXU)
                ok_a = (d_far <= BLK) & ((g > 0) | (b > 0))
                ok_b = d_near >= 0
                ok_c = (d_far <= BLK) & ((g < ng - 1) | (b < nbg - 1))

                def grads(qm, dom, k, v, ls, dl, bias, valid, hh):
                    c0 = hh * HEAD_DIM
                    sc = _nt(qm, k) * scale - bias
                    p = jnp.exp(jnp.where(valid, sc - ls[:, c0:c0 + 1], NEG))
                    ds = p * (_nt(dom, v) - dl[:, c0:c0 + 1])
                    return p.astype(MXU), ds.astype(MXU)

                dq = jnp.zeros((BLK, BLK), F32)
                dk = jnp.zeros((BLK, BLK), F32)
                dv = jnp.zeros((BLK, BLK), F32)
                for hh in (0, 1):
                    hmask = (lane < HEAD_DIM) if hh == 0 else (lane >= HEAD_DIM)
                    qnm = jnp.where(hmask, qn, 0.0).astype(MXU)
                    qxm = jnp.where(hmask, qx, 0.0).astype(MXU)
                    donm = jnp.where(hmask, don, 0.0).astype(MXU)
                    doxm = jnp.where(hmask, dox, 0.0).astype(MXU)
                    _, ds_a = grads(qnm, donm, kp, vp, lsn, dln, b_far[hh], ok_a, hh)
                    p_b, ds_b = grads(qnm, donm, kc, vc, lsn, dln, b_near[hh], ok_b, hh)
                    p_c, ds_c = grads(qxm, doxm, kc, vc, lsx, dlx, b_far[hh], ok_c, hh)
                    dq = jnp.where(hmask, _nn(ds_a, kp) + _nn(ds_b, kc), dq)
                    dk = dk + _tn(ds_b, qnm) + _tn(ds_c, qxm)
                    dv = dv + _tn(p_b, donm) + _tn(p_c, doxm)
                dqd[pl.ds(oo, BLK), :] = dq * scale
                dkd[pl.ds(oo, BLK), :] = dk * scale
                dvd[pl.ds(oo, BLK), :] = dv
                return carry

            lax.fori_loop(0, dil * nbg, tile, 0, unroll=4)
            for out, src in ((dq_ref, dqd), (dk_ref, dkd), (dv_ref, dvd)):
                for r in range(dil):
                    rows = pl.ds(r, ld, stride=dil) if dil > 1 else pl.ds(0, ld)
                    if pi == 0:
                        out[rows, :] = src[r * ld:(r + 1) * ld, :]
                    else:
                        out[rows, :] = out[rows, :] + src[r * ld:(r + 1) * ld, :]
        comm.wait_at((hp == 3) & (g == ng - 1), cm)

    def pspec(base, shift):
        return pl.BlockSpec((gsz, BLK), lambda hp, g: (jnp.clip(g + shift, 0, ng - 1), base // BLK + hp))

    def wspec(shift):
        return pl.BlockSpec((gsz, BLK), lambda hp, g: (jnp.clip(g + shift, 0, ng - 1), hp))

    in_specs = [pspec(C_Q, 0), pspec(C_Q, 1), pspec(C_K, -1), pspec(C_K, 0), pspec(C_V, -1), pspec(C_V, 0),
                wspec(0), wspec(1), wspec(0), wspec(1), wspec(0), wspec(1)] + [ANY] * comm.n
    res = pl.pallas_call(
        body, name=name, grid=(4, ng), in_specs=in_specs,
        out_specs=[wspec(0)] * 3 + [ANY] * comm.n,
        out_shape=[jax.ShapeDtypeStruct((s, ATT_W), F32)] * 3 + comm.out_shape(),
        scratch_shapes=[pltpu.VMEM((2 * gsz, BLK), F32)] * 6 + [pltpu.VMEM((gsz, BLK), F32)] * 3 + comm.scratch(),
        compiler_params=_cp("arbitrary", "arbitrary"),
    )(proj, proj, proj, proj, proj, proj, datt, datt, lse, lse, delta, delta, *comm.args())
    return res[0], res[1], res[2], list(res[3:])


def _att_bwd_rev(proj, datt, lse, delta, name, comm=None):
    s, npc = proj.shape
    gsz = ATT_G
    ng = s // gsz
    npat = len(ATT_PATTERNS)
    scale = HEAD_DIM ** -0.5
    comm = comm or _Comm()

    def body(*refs):
        (q_ref, kp_ref, kc_ref, vp_ref, vc_ref, do_ref, ls_ref, dl_ref, dq_out, dk_out, dv_out,
         qd, dod, lsd, dld, kd, vd, dqd, dkc, dvc, dkp, dvp, kcar, vcar, dq_ref, dk_ref, dv_ref), cm = comm.split(refs, 8, 3, 16)
        hp, gi = pl.program_id(0), pl.program_id(1)
        g = ng - 1 - gi
        comm.start_at((hp == 0) & (gi == 0), cm)

        @pl.when(gi == 0)
        def _():
            kcar[...] = jnp.zeros_like(kcar)
            vcar[...] = jnp.zeros_like(vcar)

        lane = lax.broadcasted_iota(jnp.int32, (BLK, BLK), 1)
        qi = lax.broadcasted_iota(jnp.int32, (BLK, 2 * BLK), 0)
        ki = lax.broadcasted_iota(jnp.int32, (BLK, 2 * BLK), 1)
        dist = BLK + qi - ki
        band = (dist >= 0) & (dist <= BLK)
        for pi, (_, dil) in enumerate(ATT_PATTERNS):
            ld = gsz // dil
            nbg = ld // BLK
            reg = ld + BLK
            for dst, src in ((qd, q_ref), (dod, do_ref), (lsd, ls_ref), (dld, dl_ref)):
                _deinterleave(dst, src, dil, ld, ld, 0)
            for dst, p_ref, c_ref in ((kd, kp_ref, kc_ref), (vd, vp_ref, vc_ref)):
                _deinterleave(dst, c_ref, dil, ld, reg, BLK)
                _deinterleave_edge(dst, p_ref, dil, reg, 0, gsz - BLK * dil)
            bias = [_slope_dist(hp, hh, dist, dil) for hh in (0, 1)]

            def tile(t, carry, ld=ld, nbg=nbg, reg=reg, bias=bias):
                r, b = t // nbg, t % nbg
                oo = pl.multiple_of(r * ld + b * BLK, BLK)
                ko = pl.multiple_of(r * reg + b * BLK, BLK)
                q, do = qd[pl.ds(oo, BLK), :], dod[pl.ds(oo, BLK), :]
                ls, dl = lsd[pl.ds(oo, BLK), :], dld[pl.ds(oo, BLK), :]
                kk = kd[pl.ds(ko, 2 * BLK), :].astype(MXU)
                vv = vd[pl.ds(ko, 2 * BLK), :].astype(MXU)
                valid = band & ((g > 0) | (b > 0) | (ki >= BLK))
                dq = jnp.zeros((BLK, BLK), F32)
                dkk = jnp.zeros((2 * BLK, BLK), F32)
                dvv = jnp.zeros((2 * BLK, BLK), F32)
                for hh in (0, 1):
                    c0 = hh * HEAD_DIM
                    hmask = (lane < HEAD_DIM) if hh == 0 else (lane >= HEAD_DIM)
                    qm = jnp.where(hmask, q, 0.0).astype(MXU)
                    dom = jnp.where(hmask, do, 0.0).astype(MXU)
                    sc = _nt(qm, kk) * scale - bias[hh]
                    p = jnp.exp(jnp.where(valid, sc - ls[:, c0:c0 + 1], NEG))
                    ds = (p * (_nt(dom, vv) - dl[:, c0:c0 + 1])).astype(MXU)
                    dq = jnp.where(hmask, _nn(ds, kk), dq)
                    dkk = dkk + _tn(ds, qm)
                    dvv = dvv + _tn(p.astype(MXU), dom)
                dqd[pl.ds(oo, BLK), :] = dq * scale
                dkp[pl.ds(oo, BLK), :] = dkk[:BLK] * scale
                dkc[pl.ds(oo, BLK), :] = dkk[BLK:] * scale
                dvp[pl.ds(oo, BLK), :] = dvv[:BLK]
                dvc[pl.ds(oo, BLK), :] = dvv[BLK:]
                return carry

            lax.fori_loop(0, dil * nbg, tile, 0, unroll=8)
            for r in range(dil):
                rows = pl.ds(r, ld, stride=dil) if dil > 1 else pl.ds(0, ld)
                lo, hi = r * ld, (r + 1) * ld
                edge = slice(pi * gsz + r * BLK, pi * gsz + (r + 1) * BLK)
                for out, cur, prv, car in ((dk_ref, dkc, dkp, kcar), (dv_ref, dvc, dvp, vcar)):
                    later = car[edge, :] if nbg == 1 else jnp.concatenate([prv[lo + BLK:hi, :], car[edge, :]], axis=0)
                    total = cur[lo:hi, :] + later
                    car[edge, :] = prv[lo:lo + BLK, :]
                    out[rows, :] = total if pi == 0 else out[rows, :] + total
                dq_ref[rows, :] = dqd[lo:hi, :] if pi == 0 else dq_ref[rows, :] + dqd[lo:hi, :]
        for out, acc in ((dq_out, dq_ref), (dk_out, dk_ref), (dv_out, dv_ref)):
            out[...] = acc[...].astype(out.dtype)
        comm.wait_at((hp == 3) & (gi == ng - 1), cm)

    def pspec(base, shift):
        return pl.BlockSpec((gsz, BLK), lambda hp, gi: (jnp.maximum(ng - 1 - gi + shift, 0), base // BLK + hp))

    wspec = pl.BlockSpec((gsz, BLK), lambda hp, gi: (ng - 1 - gi, hp))
    in_specs = [pspec(C_Q, 0), pspec(C_K, -1), pspec(C_K, 0), pspec(C_V, -1), pspec(C_V, 0), wspec, wspec, wspec] + [ANY] * comm.n
    res = pl.pallas_call(
        body, name=name, grid=(4, ng), in_specs=in_specs,
        out_specs=[wspec] * 3 + [ANY] * comm.n,
        out_shape=[jax.ShapeDtypeStruct((s, ATT_W), MXU)] * 3 + comm.out_shape(),
        scratch_shapes=[pltpu.VMEM((gsz, BLK), F32)] * 4 + [pltpu.VMEM((2 * gsz, BLK), F32)] * 2
        + [pltpu.VMEM((gsz, BLK), F32)] * 5 + [pltpu.VMEM((npat * gsz, BLK), F32)] * 2 + [pltpu.VMEM((gsz, BLK), F32)] * 3
        + comm.scratch(),
        compiler_params=_cp("arbitrary", "arbitrary"),
    )(proj, proj, proj, proj, proj, datt, lse, delta, *comm.args())
    return res[0], res[1], res[2], list(res[3:])


def _shift_down(cur, halo, sft):
    if sft == 0:
        return cur
    t = cur.shape[0]
    rolled = pltpu.roll(cur, sft, 0)
    hr = pltpu.roll(halo, sft, 0)
    row = lax.broadcasted_iota(jnp.int32, cur.shape, 0)
    return jnp.where(row < sft, jnp.tile(hr, (t // 8, 1)), rolled)


def _shift_up(cur, halo, sft):
    if sft == 0:
        return cur
    t = cur.shape[0]
    rolled = pltpu.roll(cur, t - sft, 0)
    hr = pltpu.roll(halo, 8 - sft, 0)
    row = lax.broadcasted_iota(jnp.int32, cur.shape, 0)
    return jnp.where(row >= t - sft, jnp.tile(hr, (t // 8, 1)), rolled)


def _conv(x, xh, w, b):
    y = b + x * w[CONV_K - 1:CONV_K]
    for k in range(CONV_K - 1):
        y = y + _shift_down(x, xh, CONV_K - 1 - k) * w[k:k + 1]
    return y


def _conv_bwd(x, xh, dy, dyh, w):
    dx = dy * w[CONV_K - 1:CONV_K]
    dws = []
    for k in range(CONV_K - 1):
        sft = CONV_K - 1 - k
        dx = dx + _shift_up(dy, dyh, sft) * w[k:k + 1]
        dws.append(jnp.sum(dy * _shift_down(x, xh, sft), axis=0, keepdims=True))
    dws.append(jnp.sum(dy * x, axis=0, keepdims=True))
    c = x.shape[1]
    dw = jnp.concatenate(dws + [jnp.zeros((8 - CONV_K, c), F32)], axis=0)
    return dx, dw, jnp.sum(dy, axis=0, keepdims=True)


def _pad8(w):
    return jnp.concatenate([w, jnp.zeros((8 - w.shape[0], w.shape[1]), w.dtype)], axis=0)


def _ssd_pre(proj, conv_w, conv_b, dt_bias128, name):
    def fn(rv, hv, cv):
        xbc, dtr = rv
        return [_silu(_conv(xbc, hv[0], cv[0], cv[1])), _softplus(dtr + cv[2])], []
    return _rows(fn, [(proj, 1024, C_XBC // 1024), (proj, BLK, C_DT // BLK)],
                 [_pad8(conv_w), conv_b.reshape(1, -1), dt_bias128],
                 [(1024, F32), (BLK, F32)], tile=256, name=name, halos=[(0, "prev")])


def _ssd_pre_bwd(proj, dxc, ddt, conv_w, conv_b, dt_bias128, name):
    def fn(rv, hv, cv):
        xbc, dtr, dxcb, ddtb = rv
        xh, dxch_raw, xnext = hv
        w, b, bias = cv
        pre = _conv(xbc, xh, w, b)
        sg = _sigmoid(pre)
        dpre = dxcb * (sg * (1.0 + pre * (1.0 - sg)))
        t = xbc.shape[0]
        tail = jnp.concatenate([xbc[t - 8:], xnext], axis=0)
        pre_n = _conv(tail[8:], tail[:8], w, b)
        sgn = _sigmoid(pre_n)
        dpre_h = dxch_raw * (sgn * (1.0 + pre_n * (1.0 - sgn)))
        dx, dw, db = _conv_bwd(xbc, xh, dpre, dpre_h, w)
        ddr = ddtb * _sigmoid(dtr + bias)
        return [dx, ddr], [dw, jnp.concatenate([db, jnp.zeros((7, db.shape[1]), F32)], axis=0), _colsum8(ddr)]
    return _rows(fn, [(proj, 1024, C_XBC // 1024), (proj, BLK, C_DT // BLK), dxc, ddt],
                 [_pad8(conv_w), conv_b.reshape(1, -1), dt_bias128],
                 [(1024, MXU), (BLK, MXU)], [(8, 1024), (8, 1024), (8, BLK)], tile=256, name=name,
                 halos=[(0, "prev"), (2, "next"), (0, "next")])


SSD_CPB = 1


def _head_cols(v, h0):
    lane = lax.broadcasted_iota(jnp.int32, (v.shape[0], BLK), 1)
    return jnp.where(lane < HEAD_DIM, v[:, h0:h0 + 1], v[:, h0 + 1:h0 + 2])


def _ssd_scan(xc, dt, par, name):
    s = xc.shape[0]
    nc = s // BLK

    def body(x_ref, dt_ref, par_ref, y_ref, st_ref, h_ref):
        c = pl.program_id(0)

        @pl.when(c == 0)
        def _():
            h_ref[...] = jnp.zeros_like(h_ref)

        st_ref[0] = h_ref[...]
        dt = dt_ref[...]
        a_row = -jnp.exp(par_ref[0:1, :])
        d_row = par_ref[1:2, :]
        ri = lax.broadcasted_iota(jnp.int32, (BLK, BLK), 0)
        ci = lax.broadcasted_iota(jnp.int32, (BLK, BLK), 1)
        tril = ri >= ci
        cs = _nn(tril.astype(F32), dt * a_row, HI)
        cst, dtt = cs.T, dt.T
        last = cs[BLK - 1:BLK, :]
        wcol = jnp.exp(last - cs) * dt
        ecs = jnp.exp(cs)
        elast = jnp.exp(last)
        for g in (0, 1):
            bg = x_ref[:, 512 + g * BLK:512 + (g + 1) * BLK].astype(MXU)
            cg = x_ref[:, 768 + g * BLK:768 + (g + 1) * BLK].astype(MXU)
            gm = _nt(cg, bg)
            for pp in (0, 1):
                pr = 2 * g + pp
                h0 = 2 * pr
                x2 = x_ref[:, pr * BLK:(pr + 1) * BLK]
                hprev = h_ref[pr * BLK:(pr + 1) * BLK, :]
                yp = jnp.zeros((BLK, BLK), F32)
                for hh in (0, 1):
                    h = h0 + hh
                    hmask = (ci < HEAD_DIM) if hh == 0 else (ci >= HEAD_DIM)
                    lm = jnp.exp(jnp.where(tril, cs[:, h:h + 1] - cst[h:h + 1, :], NEG))
                    mm = gm * lm * dtt[h:h + 1, :]
                    yp = yp + _nn(mm.astype(MXU), jnp.where(hmask, x2, 0.0).astype(MXU))
                y0 = _nt(cg, hprev.astype(MXU))
                y_ref[:, pr * BLK:(pr + 1) * BLK] = yp + _head_cols(ecs, h0) * y0 + _head_cols(d_row, h0) * x2
                dec = jnp.where(ri < HEAD_DIM, elast[:, h0:h0 + 1], elast[:, h0 + 1:h0 + 2])
                xw = (x2 * _head_cols(wcol, h0)).astype(MXU)
                h_ref[pr * BLK:(pr + 1) * BLK, :] = dec * hprev + _tn(xw, bg)

    return pl.pallas_call(
        body, name=name, grid=(nc,),
        in_specs=[pl.BlockSpec((BLK, 1024), lambda c: (c, 0)), pl.BlockSpec((BLK, BLK), lambda c: (c, 0)),
                  pl.BlockSpec((8, BLK), lambda c: (0, 0))],
        out_specs=[pl.BlockSpec((BLK, SSD_W), lambda c: (c, 0)), pl.BlockSpec((1, SSD_W, SSD_STATE), lambda c: (c, 0, 0))],
        out_shape=[jax.ShapeDtypeStruct((s, SSD_W), F32), jax.ShapeDtypeStruct((nc, SSD_W, SSD_STATE), F32)],
        scratch_shapes=[pltpu.VMEM((SSD_W, SSD_STATE), F32)],
        compiler_params=_cp("arbitrary"),
    )(xc, dt, par)


def _ssd_scan_bwd(xc, dt, par, st, dy, name, comm=None):
    s = xc.shape[0]
    cpb = SSD_CPB
    nb = s // (cpb * BLK)
    comm = comm or _Comm()

    def chunk(x_ref, dt_ref, par_ref, st_ref, dy_ref, dx_ref, ddt_ref, dal_ref, dd_ref, dh_ref):
        dt = dt_ref[...]
        a_row = -jnp.exp(par_ref[0:1, :])
        d_row = par_ref[1:2, :]
        ri = lax.broadcasted_iota(jnp.int32, (BLK, BLK), 0)
        ci = lax.broadcasted_iota(jnp.int32, (BLK, BLK), 1)
        tril = ri >= ci
        cs = _nn(tril.astype(F32), dt * a_row, HI)
        cst, dtt = cs.T, dt.T
        last = cs[BLK - 1:BLK, :]
        tolast = jnp.exp(last - cs)
        wcol = tolast * dt
        ecs = jnp.exp(cs)
        elast = jnp.exp(last)
        dcs_col = jnp.zeros((BLK, BLK), F32)
        ddt_col = jnp.zeros((BLK, BLK), F32)
        dcs_row = jnp.zeros((BLK, BLK), F32)
        ddt_row = jnp.zeros((BLK, BLK), F32)
        dlast = jnp.zeros((1, BLK), F32)
        ddsk = jnp.zeros((1, BLK), F32)
        for g in (0, 1):
            bg32 = x_ref[:, 512 + g * BLK:512 + (g + 1) * BLK]
            cg32 = x_ref[:, 768 + g * BLK:768 + (g + 1) * BLK]
            bg, cg = bg32.astype(MXU), cg32.astype(MXU)
            gm = _nt(cg, bg)
            dgm = jnp.zeros((BLK, BLK), F32)
            dbg = jnp.zeros((BLK, BLK), F32)
            dcg = jnp.zeros((BLK, BLK), F32)
            for pp in (0, 1):
                pr = 2 * g + pp
                h0 = 2 * pr
                x2 = x_ref[:, pr * BLK:(pr + 1) * BLK]
                dy2 = dy_ref[:, pr * BLK:(pr + 1) * BLK]
                hprev = st_ref[0, pr * BLK:(pr + 1) * BLK, :]
                dhn = dh_ref[pr * BLK:(pr + 1) * BLK, :]
                x2m, dhnm = x2.astype(MXU), dhn.astype(MXU)
                zb = _nt(bg, dhnm)
                y0 = _nt(cg, hprev.astype(MXU))
                esel = _head_cols(ecs, h0)
                wsel = _head_cols(wcol, h0)
                dx2 = _head_cols(d_row, h0) * dy2 + wsel * zb
                pick2 = (((ri < HEAD_DIM) & (ci == h0)) | ((ri >= HEAD_DIM) & (ci == h0 + 1))).astype(F32)
                sums = _nn(jnp.concatenate([dy2 * y0, x2 * zb, dy2 * x2], axis=0), pick2, HIGH)
                de2, dw2, dd2 = sums[:BLK], sums[BLK:2 * BLK], sums[2 * BLK:]
                v2 = dw2 * wcol
                dcs_col = dcs_col + ecs * de2 - v2
                ddt_col = ddt_col + dw2 * tolast
                hsum = _nn(dhn * hprev, jnp.ones((BLK, BLK), F32), HIGH)
                dlast = dlast + elast * jnp.sum(jnp.where(pick2 > 0.0, hsum, 0.0), axis=0, keepdims=True) \
                    + jnp.sum(v2, axis=0, keepdims=True)
                ddsk = ddsk + jnp.sum(dd2, axis=0, keepdims=True)
                ts = []
                for hh in (0, 1):
                    h = h0 + hh
                    hmask = (ci < HEAD_DIM) if hh == 0 else (ci >= HEAD_DIM)
                    ons = (ri == h).astype(F32)
                    dym = jnp.where(hmask, dy2, 0.0).astype(MXU)
                    dt_r = dtt[h:h + 1, :]
                    lm = jnp.exp(jnp.where(tril, cs[:, h:h + 1] - cst[h:h + 1, :], NEG))
                    mm = gm * lm * dt_r
                    dx2 = dx2 + _tn(mm.astype(MXU), dym)
                    dm = _nt(dym, x2m)
                    t1 = dm * lm
                    dgm = dgm + t1 * dt_r
                    tt = t1 * gm
                    ddt_row = ddt_row + ons * jnp.sum(tt, axis=0, keepdims=True)
                    t = tt * dt_r
                    dcs_row = dcs_row - ons * jnp.sum(t, axis=0, keepdims=True)
                    ts.append(t)
                rows2 = lax.broadcasted_iota(jnp.int32, (2 * BLK, BLK), 0)
                lane2 = lax.broadcasted_iota(jnp.int32, (2 * BLK, BLK), 1)
                to_lane = ((rows2 < BLK) & (lane2 == h0)) | ((rows2 >= BLK) & (lane2 == h0 + 1))
                dcs_col = dcs_col + _nn(jnp.concatenate(ts, axis=1), to_lane.astype(F32), HIGH)
                dx_ref[:, pr * BLK:(pr + 1) * BLK] = dx2
                edy = (esel * dy2).astype(MXU)
                dcg = dcg + _nn(edy, hprev.astype(MXU))
                dec = jnp.where(ri < HEAD_DIM, elast[:, h0:h0 + 1], elast[:, h0 + 1:h0 + 2])
                dh_ref[pr * BLK:(pr + 1) * BLK, :] = dec * dhn + _tn(edy, cg)
                dbg = dbg + _nn((x2 * wsel).astype(MXU), dhnm)
            dgmm = dgm.astype(MXU)
            dx_ref[:, 512 + g * BLK:512 + (g + 1) * BLK] = dbg + _tn(dgmm, cg)
            dx_ref[:, 768 + g * BLK:768 + (g + 1) * BLK] = dcg + _nn(dgmm, bg)
        dcs = dcs_col + dcs_row.T + jnp.where(ri == BLK - 1, dlast, 0.0)
        dda = _nn((ri <= ci).astype(F32), dcs, HI)
        ddt_ref[...] = ddt_col + ddt_row.T + a_row * dda
        da = jnp.sum(dt * dda, axis=0, keepdims=True)
        dal_ref[0:1, :] += da * a_row
        dd_ref[0:1, :] += ddsk

    def body(*refs):
        (x_ref, dt_ref, par_ref, st_ref, dy_ref, dx_ref, ddt_ref, dal_ref, dd_ref, dh_ref), cm = comm.split(refs, 5, 4, 1)
        c = pl.program_id(0)
        comm.start_at(c == 0, cm)

        @pl.when(c == 0)
        def _():
            dh_ref[...] = jnp.zeros_like(dh_ref)
            dal_ref[...] = jnp.zeros_like(dal_ref)
            dd_ref[...] = jnp.zeros_like(dd_ref)

        for cc in reversed(range(cpb)):
            rows = pl.ds(cc * BLK, BLK)
            chunk(x_ref.at[rows], dt_ref.at[rows], par_ref, st_ref.at[pl.ds(cc, 1)], dy_ref.at[rows], dx_ref.at[rows],
                  ddt_ref.at[rows], dal_ref, dd_ref, dh_ref)
        comm.wait_at(c == nb - 1, cm)

    rev = lambda c: (nb - 1 - c, 0)
    tb = cpb * BLK
    res = pl.pallas_call(
        body, name=name, grid=(nb,),
        in_specs=[pl.BlockSpec((tb, 1024), rev), pl.BlockSpec((tb, BLK), rev), pl.BlockSpec((8, BLK), lambda c: (0, 0)),
                  pl.BlockSpec((cpb, SSD_W, SSD_STATE), lambda c: (nb - 1 - c, 0, 0)), pl.BlockSpec((tb, SSD_W), rev)]
        + [ANY] * comm.n,
        out_specs=[pl.BlockSpec((tb, 1024), rev), pl.BlockSpec((tb, BLK), rev),
                   pl.BlockSpec((8, BLK), lambda c: (0, 0)), pl.BlockSpec((8, BLK), lambda c: (0, 0))] + [ANY] * comm.n,
        out_shape=[jax.ShapeDtypeStruct((s, 1024), F32), jax.ShapeDtypeStruct((s, BLK), F32),
                   jax.ShapeDtypeStruct((8, BLK), F32), jax.ShapeDtypeStruct((8, BLK), F32)] + comm.out_shape(),
        scratch_shapes=[pltpu.VMEM((SSD_W, SSD_STATE), F32)] + comm.scratch(),
        compiler_params=_cp("arbitrary"),
    )(xc, dt, par, st, dy, *comm.args())
    return res[0], res[1], res[2], res[3], list(res[4:])


def _ssd_gate(y, z, w):
    t = y * _silu(z)
    outs = []
    for g in (0, 1):
        tg = t[:, g * 256:(g + 1) * 256]
        outs.append(tg * lax.rsqrt(jnp.mean(tg * tg, axis=-1, keepdims=True) + SSD_NORM_EPS))
    return jnp.concatenate(outs, axis=1) * w


def _ssd_post(y, proj, norm_w, name):
    def fn(rv, hv, cv):
        return [_ssd_gate(rv[0], rv[1], cv[0])], []
    return _rows(fn, [y, (proj, SSD_W, C_Z // SSD_W)], [norm_w.reshape(1, -1)], [(SSD_W, MXU)], tile=512, name=name)[0]


def _ssd_post_bwd(y, proj, norm_w, dout, name):
    def fn(rv, hv, cv):
        yb, zb, db = rv
        _, vjp = jax.vjp(lambda a, b: _ssd_gate(a, b, cv[0]), yb, zb)
        dy, dz = vjp(db)
        t = yb * _silu(zb)
        nrm = []
        for g in (0, 1):
            tg = t[:, g * 256:(g + 1) * 256]
            nrm.append(tg * lax.rsqrt(jnp.mean(tg * tg, axis=-1, keepdims=True) + SSD_NORM_EPS))
        return [dy, dz], [_colsum8(db * jnp.concatenate(nrm, axis=1))]
    return _rows(fn, [y, (proj, SSD_W, C_Z // SSD_W), dout], [norm_w.reshape(1, -1)],
                 [(SSD_W, F32), (SSD_W, MXU)], [(8, SSD_W)], tile=512, name=name)


LRU_T = 256


def _lru_conv(proj, conv_w, conv_b, name):
    def fn(rv, hv, cv):
        return [_conv(rv[0], hv[0], cv[0], cv[1])], []
    return _rows(fn, [(proj, LRU_W, C_XL // LRU_W)], [_pad8(conv_w), conv_b.reshape(1, -1)], [(LRU_W, F32)],
                 tile=512, name=name, halos=[(0, "prev")])[0]


def _lru_conv_bwd(proj, dxc, conv_w, name):
    def fn(rv, hv, cv):
        dx, dw, db = _conv_bwd(rv[0], hv[0], rv[1], hv[1], cv[0])
        return [dx], [dw, jnp.concatenate([db, jnp.zeros((7, db.shape[1]), F32)], axis=0)]
    return _rows(fn, [(proj, LRU_W, C_XL // LRU_W), dxc], [_pad8(conv_w)], [(LRU_W, MXU)], [(8, LRU_W), (8, LRU_W)],
                 tile=512, name=name, halos=[(0, "prev"), (1, "next")])


def _lru_au(pre_a, pre_x, xc, ba, bx, lam):
    r = _sigmoid(pre_a + ba)
    i = _sigmoid(pre_x + bx)
    log_a = -LRU_C * r * _softplus(-lam)
    a = jnp.exp(log_a)
    u = jnp.sqrt(1.0 - jnp.exp(2.0 * log_a)) * (i * xc)
    return a, u


def _lru_scan(pre, xc, proj, par, name):
    s = xc.shape[0]
    t = LRU_T

    def body(pre_ref, xc_ref, g_ref, par_ref, out_ref, h_ref, carry):
        c = pl.program_id(0)

        @pl.when(c == 0)
        def _():
            carry[...] = jnp.zeros_like(carry)

        a, u = _lru_au(pre_ref[:, :LRU_W], pre_ref[:, LRU_W:], xc_ref[...], par_ref[0:1, :], par_ref[1:2, :], par_ref[2:3, :])
        row = lax.broadcasted_iota(jnp.int32, (t, LRU_W), 0)
        sft = 1
        while sft < t:
            keep = row >= sft
            a_s = jnp.where(keep, pltpu.roll(a, sft, 0), 1.0)
            u_s = jnp.where(keep, pltpu.roll(u, sft, 0), 0.0)
            u = a * u_s + u
            a = a * a_s
            sft *= 2
        h = a * carry[0:1, :] + u
        h_ref[...] = h
        out_ref[...] = (h * _gelu(g_ref[...])).astype(out_ref.dtype)
        carry[0:1, :] = h[t - 1:t, :]

    return pl.pallas_call(
        body, name=name, grid=(s // t,),
        in_specs=[pl.BlockSpec((t, 2 * LRU_W), lambda c: (c, 0)), pl.BlockSpec((t, LRU_W), lambda c: (c, 0)),
                  pl.BlockSpec((t, LRU_W), lambda c: (c, C_G // LRU_W)), pl.BlockSpec((8, LRU_W), lambda c: (0, 0))],
        out_specs=[pl.BlockSpec((t, LRU_W), lambda c: (c, 0))] * 2,
        out_shape=[jax.ShapeDtypeStruct((s, LRU_W), MXU), jax.ShapeDtypeStruct((s, LRU_W), F32)],
        scratch_shapes=[pltpu.VMEM((8, LRU_W), F32)],
        compiler_params=_cp("arbitrary"),
    )(pre, xc, proj, par)


def _lru_scan_bwd(pre, xc, proj, par, h, dout, name):
    s = xc.shape[0]
    t = LRU_T
    n = s // t
    t8 = t // 8

    def body(pre_ref, xc_ref, g_ref, par_ref, h_ref, hh_ref, do_ref, dpre_ref, dxc_ref, dg_ref, dpar_ref, carry):
        c = pl.program_id(0)

        @pl.when(c == 0)
        def _():
            carry[...] = jnp.zeros_like(carry)
            dpar_ref[...] = jnp.zeros_like(dpar_ref)

        pa, px, xcb = pre_ref[:, :LRU_W], pre_ref[:, LRU_W:], xc_ref[...]
        ba, bx, lam = par_ref[0:1, :], par_ref[1:2, :], par_ref[2:3, :]
        (a, u), vjp = jax.vjp(_lru_au, pa, px, xcb, ba, bx, lam)
        g = g_ref[...]
        hcur = h_ref[...]
        do = do_ref[...]
        _, gvjp = jax.vjp(_gelu, g)
        dg_ref[...] = gvjp(do * hcur)[0].astype(dg_ref.dtype)
        row = lax.broadcasted_iota(jnp.int32, (t, LRU_W), 0)
        v = do * _gelu(g) + jnp.where(row == t - 1, carry[0:1, :], 0.0)
        b = jnp.where(row == t - 1, 0.0, pltpu.roll(a, t - 1, 0))
        sft = 1
        while sft < t:
            keep = row < t - sft
            b_s = jnp.where(keep, pltpu.roll(b, t - sft, 0), 1.0)
            v_s = jnp.where(keep, pltpu.roll(v, t - sft, 0), 0.0)
            v = b * v_s + v
            b = b * b_s
            sft *= 2
        dh = v
        carry[0:1, :] = a[0:1, :] * dh[0:1, :]
        hhalo = jnp.where(c == n - 1, 0.0, hh_ref[...])
        hprev = _shift_down(hcur, hhalo, 1)
        dpa, dpx, dxc, dba, dbx, dlam = vjp((dh * hprev, dh))
        dpre_ref[:, :LRU_W] = dpa
        dpre_ref[:, LRU_W:] = dpx
        dxc_ref[...] = dxc
        dpar_ref[0:1, :] += dba
        dpar_ref[1:2, :] += dbx
        dpar_ref[2:3, :] += dlam

    rev = lambda c: (n - 1 - c, 0)
    return pl.pallas_call(
        body, name=name, grid=(n,),
        in_specs=[pl.BlockSpec((t, 2 * LRU_W), rev), pl.BlockSpec((t, LRU_W), rev),
                  pl.BlockSpec((t, LRU_W), lambda c: (n - 1 - c, C_G // LRU_W)), pl.BlockSpec((8, LRU_W), lambda c: (0, 0)),
                  pl.BlockSpec((t, LRU_W), rev),
                  pl.BlockSpec((8, LRU_W), lambda c: (jnp.maximum((n - 1 - c) * t8 - 1, 0), 0)),
                  pl.BlockSpec((t, LRU_W), lambda c: (n - 1 - c, dout.shape[1] // LRU_W - 1))],
        out_specs=[pl.BlockSpec((t, 2 * LRU_W), rev), pl.BlockSpec((t, LRU_W), rev), pl.BlockSpec((t, LRU_W), rev),
                   pl.BlockSpec((8, LRU_W), lambda c: (0, 0))],
        out_shape=[jax.ShapeDtypeStruct((s, 2 * LRU_W), F32), jax.ShapeDtypeStruct((s, LRU_W), F32),
                   jax.ShapeDtypeStruct((s, LRU_W), MXU), jax.ShapeDtypeStruct((8, LRU_W), F32)],
        scratch_shapes=[pltpu.VMEM((8, LRU_W), F32)],
        compiler_params=_cp("arbitrary"),
    )(pre, xc, proj, par, h, h, dout)


def _swiglu_act(gu, name):
    def fn(rv, hv, cv):
        return [_silu(rv[0].astype(F32)) * rv[1].astype(F32)], []
    return _rows(fn, [(gu, D_FF, 0), (gu, D_FF, 1)], [], [(D_FF, MXU)], tile=256, name=name)[0]


def _swiglu_bwd(gu, da, name):
    def fn(rv, hv, cv):
        gt, up, dab = [t.astype(F32) for t in rv]
        sg = _sigmoid(gt)
        dgate = dab * up * (sg * (1.0 + gt * (1.0 - sg)))
        dup = dab * (gt * sg)
        return [jnp.concatenate([dgate, dup], axis=1)], []
    return _rows(fn, [(gu, D_FF, 0), (gu, D_FF, 1), da], [], [(2 * D_FF, MXU)], tile=256, name=name)[0]


def _loss_head(x, g, target, name):
    d = x.shape[1]

    def fn(rv, hv, cv):
        xb, tb = rv
        y, vjp = jax.vjp(_rms, xb, cv[0])
        err = y - tb
        dy = err * (1.0 / d)
        dx, _ = vjp(dy)
        rstd = lax.rsqrt(jnp.mean(xb * xb, axis=-1, keepdims=True) + NORM_EPS)
        e2 = err * err * (0.5 / d)
        e2 = functools.reduce(lambda a, b: a + b, [e2[:, k * BLK:(k + 1) * BLK] for k in range(d // BLK)])
        return [dx], [_colsum8(dy * xb * rstd), _colsum8(e2)]
    return _rows(fn, [x, target], [g.reshape(1, -1)], [(d, F32)], [(8, d), (8, BLK)], tile=512, name=name)


ANY = pl.BlockSpec(memory_space=pl.ANY)


def _coords():
    return lax.axis_index("x"), lax.axis_index("y"), lax.axis_index("c")


class _Comm:
    def __init__(self, gathers=(), scatters=()):
        self.gathers = list(gathers)
        self.scatters = list(scatters)
        self.n = len(self.gathers) + len(self.scatters)

    def args(self):
        return [g[0] for g in self.gathers] + self.scatters

    def out_shape(self):
        out = [jax.ShapeDtypeStruct((4,) + (a.shape if l is None else a.shape[1:]), a.dtype) for a, l, _ in self.gathers]
        return out + [jax.ShapeDtypeStruct((3,) + a.shape[1:], a.dtype) for a in self.scatters]

    def scratch(self):
        if not self.n:
            return []
        return [pltpu.SemaphoreType.DMA((3 * self.n,)), pltpu.SemaphoreType.DMA((3 * self.n,)),
                pltpu.SemaphoreType.DMA((max(len(self.gathers), 1),)),
                pltpu.SemaphoreType.DMA((3 * self.n,)), pltpu.SemaphoreType.DMA((3 * self.n,))]

    def split(self, refs, n_in, n_out, n_scratch):
        refs = list(refs)
        n = self.n
        own = refs[:n_in] + refs[n_in + n:n_in + n + n_out] + refs[n_in + 2 * n + n_out:n_in + 2 * n + n_out + n_scratch]
        cm = (refs[n_in:n_in + n], refs[n_in + n + n_out:n_in + 2 * n + n_out], refs[n_in + 2 * n + n_out + n_scratch:])
        return own, cm

    def _copies(self, cm, arriving):
        ins, outs, (send, recv, local, _, _) = cm
        x, y, c = _coords()
        me = 2 * x + y
        chips = [(1 - x, y), (x, 1 - y), (1 - x, 1 - y)]
        remote, locals_ = [], []
        ng = len(self.gathers)
        for i in range(self.n):
            if i < ng:
                _, l, halved = self.gathers[i]
                slab = ins[i] if l is None else ins[i].at[l]
                if not arriving:
                    locals_.append(pltpu.make_async_copy(slab, outs[i].at[me], local.at[i]))
            for j, (px, py) in enumerate(chips):
                if i < ng:
                    slot = 2 * px + py if arriving else me
                    src, dst = (slab.at[c], outs[i].at[slot, c]) if halved else (slab, outs[i].at[slot])
                else:
                    src, dst = ins[i].at[2 * px + py], outs[i].at[j]
                remote.append(pltpu.make_async_remote_copy(src, dst, send.at[3 * i + j], recv.at[3 * i + j],
                                                           device_id=(px, py, c), device_id_type=MESH))
        return remote, locals_

    def _handovers(self, cm, arriving):
        _, outs, (_, _, _, send, recv) = cm
        x, y, c = _coords()
        chips = [(1 - x, y), (x, 1 - y), (1 - x, 1 - y)]
        cps = []
        for i, (_, _, halved) in enumerate(self.gathers):
            if halved:
                for j, (px, py) in enumerate(chips):
                    src = outs[i].at[2 * px + py, c]
                    dst = outs[i].at[2 * px + py, 1 - c if arriving else c]
                    cps.append(pltpu.make_async_remote_copy(src, dst, send.at[3 * i + j], recv.at[3 * i + j],
                                                            device_id=(x, y, 1 - c), device_id_type=MESH))
        return cps

    def start_at(self, cond, cm):
        def go():
            remote, locals_ = self._copies(cm, False)
            for cp in locals_ + remote:
                cp.start()

        if self.n:
            go() if cond is True else pl.when(cond)(go)

    def wait_at(self, cond, cm):
        def go():
            for cp in self._copies(cm, True)[0]:
                cp.wait_recv()
            handed = self._handovers(cm, False)
            for cp in handed:
                cp.start()
            for cp in self._handovers(cm, True):
                cp.wait_recv()
            remote, locals_ = self._copies(cm, False)
            for cp in handed + remote:
                cp.wait_send()
            for cp in locals_:
                cp.wait()

        if self.n:
            go() if cond is True else pl.when(cond)(go)


def _comm_call(comm, name):
    def body(*refs):
        _, cm = comm.split(refs, 0, 0, 0)
        comm.start_at(True, cm)
        comm.wait_at(True, cm)

    return list(pl.pallas_call(
        body, name=name, in_specs=[ANY] * comm.n, out_specs=[ANY] * comm.n, out_shape=comm.out_shape(),
        scratch_shapes=comm.scratch(), compiler_params=pltpu.CompilerParams(has_side_effects=True),
    )(*comm.args()))


def _swap_sibling(arrs):
    n = len(arrs)

    def body(*refs):
        ins, outs, send, recv = refs[:n], refs[n:2 * n], refs[2 * n], refs[2 * n + 1]
        x, y, c = _coords()
        cps = [pltpu.make_async_remote_copy(ins[i], outs[i], send.at[i], recv.at[i], device_id=(x, y, 1 - c), device_id_type=MESH)
               for i in range(n)]
        for cp in cps:
            cp.start()
        for cp in cps:
            cp.wait_recv()
        for cp in cps:
            cp.wait_send()

    return list(pl.pallas_call(
        body, name="swap_sibling", in_specs=[ANY] * n, out_specs=[ANY] * n,
        out_shape=[jax.ShapeDtypeStruct(a.shape, a.dtype) for a in arrs],
        scratch_shapes=[pltpu.SemaphoreType.DMA((n,)), pltpu.SemaphoreType.DMA((n,))],
        compiler_params=pltpu.CompilerParams(has_side_effects=True),
    )(*arrs))


def _gather_small(gs):
    def body(g_ref, o_ref, send_sems, recv_sems, local_sem):
        x, y, c = _coords()
        me = 4 * x + 2 * y + c
        mine = pltpu.make_async_copy(g_ref, o_ref.at[me], local_sem)
        mine.start()
        sends = []
        for k in range(1, 8):
            px, py, pc = x ^ (k >> 2), y ^ ((k >> 1) & 1), c ^ (k & 1)
            sends.append((pltpu.make_async_remote_copy(g_ref, o_ref.at[me], send_sems.at[k - 1], recv_sems.at[k - 1],
                                                       device_id=(px, py, pc), device_id_type=MESH), 4 * px + 2 * py + pc, k))
        for cp, _, _ in sends:
            cp.start()
        for cp, src, k in sends:
            pltpu.make_async_remote_copy(g_ref, o_ref.at[src], send_sems.at[k - 1], recv_sems.at[k - 1],
                                         device_id=(x, y, c), device_id_type=MESH).wait_recv()
        for cp, _, _ in sends:
            cp.wait_send()
        mine.wait()

    return pl.pallas_call(
        body, name="gather_small", in_specs=[ANY], out_specs=ANY,
        out_shape=jax.ShapeDtypeStruct((8,) + gs.shape, gs.dtype),
        scratch_shapes=[pltpu.SemaphoreType.DMA((7,)), pltpu.SemaphoreType.DMA((7,)), pltpu.SemaphoreType.DMA],
        compiler_params=pltpu.CompilerParams(has_side_effects=True),
    )(gs)


def _sum_slots(own, others, name, tile):
    k, r, c = others.shape

    def body(*refs):
        if own is None:
            o_ref, out_ref = refs
            acc = o_ref[0].astype(F32)
            first = 1
        else:
            own_ref, o_ref, out_ref = refs
            acc = own_ref[...]
            first = 0
        for j in range(first, k):
            acc = acc + o_ref[j].astype(F32)
        out_ref[...] = acc

    row = pl.BlockSpec((tile, c), lambda i: (i, 0))
    specs = ([] if own is None else [row]) + [pl.BlockSpec((k, tile, c), lambda i: (0, i, 0))]
    args = ([] if own is None else [own]) + [others]
    return pl.pallas_call(body, name=name, grid=(r // tile,), in_specs=specs, out_specs=row,
                          out_shape=jax.ShapeDtypeStruct((r, c), F32), compiler_params=_cp("parallel"))(*args)


def _adamw(w, m, v, ga, gb, name, tile, rows_first=False):
    lead = 0 if rows_first else w.ndim - 2
    r, c = w.shape[-2:]

    def body(*refs):
        vals = [ref[0] if lead else ref[...] for ref in refs[:len(refs) - 4]]
        w_, m_, v_, g = vals[0], vals[1], vals[2], vals[3]
        if gb is not None:
            g = g + vals[4]
        nm = ADAM_B1 * m_ + (1.0 - ADAM_B1) * g
        nv = ADAM_B2 * v_ + (1.0 - ADAM_B2) * (g * g)
        d = -ADAM_LR * ((nm / BC1) / (jnp.sqrt(nv / BC2) + ADAM_EPS) + ADAM_WD * w_)
        for ref, val in zip(refs[len(refs) - 4:], (g, d, nm, nv)):
            if lead:
                ref[0] = val
            else:
                ref[...] = val

    if rows_first:
        row = pl.BlockSpec((tile,) + w.shape[1:], lambda i: (i, 0, 0))
        grid = (w.shape[0] // tile,)
    elif lead:
        row = pl.BlockSpec((1, tile, c), lambda l, i: (l, i, 0))
        grid = (w.shape[0], r // tile)
    else:
        row = pl.BlockSpec((tile, c), lambda i: (i, 0))
        grid = (r // tile,)
    args = [w, m, v, ga] + ([] if gb is None else [gb])
    return pl.pallas_call(body, name=name, grid=grid, in_specs=[row] * len(args), out_specs=[row] * 4,
                          out_shape=[jax.ShapeDtypeStruct(w.shape, F32)] * 4,
                          compiler_params=_cp(*(["parallel"] * len(grid))))(*args)


MATS = ("w_in", "w_out", "w_gate", "w_up", "w_down")
CONVS = ("ssd_conv_w", "lru_conv_w")
BIG = MATS + CONVS
TRANSPOSED = ("w_gate", "w_up")
COL_SHARDED = ("ssd_conv_w", "lru_conv_w")
W_IN_SHARD = IN_COLS // 4
W_IN_PAD = 1056
SMALL = ("norm_mix", "ssd_conv_b", "ssd_dt_bias", "ssd_a_log", "ssd_d", "ssd_norm", "lru_conv_b", "lru_wa", "lru_ba",
         "lru_wx", "lru_bx", "lru_lambda", "norm_ffn", "norm_final")
WEIGHTS = ("norm_mix", "w_in", "ssd_conv_w", "ssd_conv_b", "ssd_dt_bias", "ssd_a_log", "ssd_d", "ssd_norm", "lru_conv_w",
           "lru_conv_b", "lru_wa", "lru_ba", "lru_wx", "lru_bx", "lru_lambda", "w_out", "norm_ffn", "w_gate", "w_up",
           "w_down", "norm_final")
ROW_TILE = {"w_in": W_IN_SHARD, "w_out": 128, "w_gate": 352, "w_up": 352, "w_down": 352}
W_IN_ADAM_TILE = 54


def _pack(arrs, width, row_mult, dtype):
    flat = jnp.concatenate([a.reshape(-1).astype(dtype) for a in arrs])
    rows = -(-flat.shape[0] // width)
    rows = -(-rows // row_mult) * row_mult
    flat = jnp.pad(flat, (0, rows * width - flat.shape[0]))
    return flat.reshape(rows, width)


def _unpack(buf, shapes):
    flat = buf.reshape(-1)
    out, off = [], 0
    for shp in shapes:
        n = int(np.prod(shp))
        out.append(flat[off:off + n].reshape(shp))
        off += n
    return out


def _join(name, g4):
    if name in COL_SHARDED:
        return jnp.moveaxis(g4, 0, -2).reshape(g4.shape[1:-1] + (4 * g4.shape[-1],))
    return g4.reshape((4 * g4.shape[1],) + g4.shape[2:])


def _slabs(name, g):
    if name in COL_SHARDED:
        return jnp.moveaxis(g.reshape(g.shape[:-1] + (4, g.shape[-1] // 4)), -2, 0)
    return g.reshape((4, g.shape[0] // 4) + g.shape[1:])


def _w_in_rows(g4):
    def nat(lo, hi):
        out = []
        while lo < hi:
            j = lo // W_IN_SHARD
            stop = min(hi, (j + 1) * W_IN_SHARD)
            out.append((j, lo - j * W_IN_SHARD, stop - lo))
            lo = stop
        return out
    pieces = nat(0, 3072) + nat(3080, IN_COLS) + nat(3072, 3080)

    def body(g_ref, o_ref):
        row = 0
        for j, first, n in pieces:
            o_ref[row:row + n, :] = g_ref[j, first:first + n, :]
            row += n
        o_ref[row:, :] = jnp.zeros((NP - row, o_ref.shape[1]), o_ref.dtype)

    return pl.pallas_call(body, name="w_in_rows", out_shape=jax.ShapeDtypeStruct((NP, g4.shape[-1]), g4.dtype),
                          compiler_params=pltpu.CompilerParams(vmem_limit_bytes=VMEM_LIMIT))(g4)


def _w_in_slabs(gt):
    def kern(n):
        return n if n < 3072 else (C_DT + n - 3072 if n < 3080 else n - 8)
    slabs = []
    for j in range(4):
        lo, hi = j * W_IN_SHARD, (j + 1) * W_IN_SHARD
        cuts = sorted({lo, hi} | {c for c in (3072, 3080) if lo < c < hi})
        slabs.append(jnp.concatenate([gt[kern(a):kern(a) + b - a] for a, b in zip(cuts[:-1], cuts[1:])], axis=0))
    return jnp.stack(slabs, axis=0)


def _block_diag(w):
    eye = jnp.eye(LRU_BLOCKS, dtype=w.dtype)
    return jnp.einsum("ncd,nm->ncmd", w, eye).reshape(LRU_W, LRU_W)


def _block_diag_extract(g):
    g4 = g.reshape(LRU_BLOCKS, 64, LRU_BLOCKS, 64)
    return jnp.stack([g4[n, :, n, :] for n in range(LRU_BLOCKS)], axis=0)


def _lanes128(v):
    return jnp.pad(v, (0, BLK - v.shape[0])).reshape(1, BLK)


def _layer_mixers(x, p, comm=None, h=None):
    if h is None:
        h = _rms_fwd(x, p["norm_mix"], "rms_mix")
    proj = _mm(h, p["w_in_t"], tb=True, tm=1024, tn=1408, tk=1024, name="mm_in")
    att, lse, attb, got = _att_fwd_fused(proj, "att_fwd", comm)
    xconv, dt = _ssd_pre(proj, p["ssd_conv_w"], p["ssd_conv_b"], _lanes128(p["ssd_dt_bias"]), "ssd_pre")
    spar = jnp.concatenate([_lanes128(p["ssd_a_log"]), _lanes128(p["ssd_d"]), jnp.zeros((6, BLK), F32)], axis=0)
    y, states = _ssd_scan(xconv, dt, spar, "ssd_scan")
    ssd = _ssd_post(y, proj, p["ssd_norm"], "ssd_post")
    xc = _lru_conv(proj, p["lru_conv_w"], p["lru_conv_b"], "lru_conv")
    wab = jnp.concatenate([_block_diag(p["lru_wa"]), _block_diag(p["lru_wx"])], axis=1).astype(MXU)
    pre = _mm(xc, wab, tm=1024, tn=1024, tk=512, name="mm_lru")
    lpar = jnp.concatenate([p["lru_ba"].reshape(1, -1), p["lru_bx"].reshape(1, -1), p["lru_lambda"].reshape(1, -1),
                            jnp.zeros((5, LRU_W), F32)], axis=0)
    lru, hs = _lru_scan(pre, xc, proj, lpar, "lru_scan")
    mix = jnp.concatenate([attb, ssd, lru], axis=1)
    saved = dict(x=x, h=h, proj=proj, att=att, lse=lse, xconv=xconv, dt=dt, spar=spar, y=y, states=states, xc=xc, wab=wab,
                 pre=pre, lpar=lpar, hs=hs, mix=mix)
    return mix, saved, got


def _layer_ffn(x, mix, p, saved, comm=None, next_norm=None):
    x1, h2 = _mm(mix, p["w_out"], add=x, tm=1024, tn=1024, tk=1536, name="mm_out", epi=_epi_rms(p["norm_ffn"]))
    gu = _mm(h2, p["w_gu_t"], tb=True, out_dtype=MXU, tm=1024, tn=1408, tk=1024, name="mm_gu", comm=comm)
    gu, got = gu if comm is not None else (gu, [])
    act = _swiglu_act(gu, "swiglu_act")
    x2 = _mm(act, p["w_down"], add=x1, tm=1024, tn=1024, tk=2816, name="mm_down",
             epi=None if next_norm is None else _epi_rms(next_norm))
    x2, h_next = x2 if next_norm is not None else (x2, None)
    saved.update(x1=x1, h2=h2, gu=gu, act=act)
    return x2, got, h_next


def _layer_bwd(dx2, p, sv, comm_ssd=None, comm_att=None, comm_tail=None):
    g = {}
    da = _mm(dx2, p["w_down"], tb=True, out_dtype=MXU, tm=1024, tn=1408, tk=1024, name="mm_d_act")
    g["w_down"], g["w_down@wire"] = _mm(sv["act"], dx2, ta=True, tm=1408, tn=1024, tk=1024, name="mm_g_down", epi=_epi_wire(D_MODEL))
    dgu = _swiglu_bwd(sv["gu"], da, "swiglu_bwd")
    dx1, gn = _mm(dgu, p["w_gu_t"], tm=1024, tn=1024, tk=1408, name="mm_d_h2", epi=_epi_rms_bwd(sv["x1"], p["norm_ffn"], dx2))
    g["w_gu_t"], g["w_gu_t@wire"] = _mm(dgu, sv["h2"], ta=True, tm=1408, tn=1024, tk=1024, name="mm_g_gu", epi=_epi_wire(D_MODEL))
    g["norm_ffn"] = jnp.sum(gn, axis=0)
    dmix, delta = _mm(dx1, p["w_out"], tb=True, tm=1024, tn=1536, tk=1024, name="mm_d_mix", epi=_epi_att_delta(sv["att"]))
    g["w_out"], g["w_out@wire"] = _mm(sv["mix"], dx1, ta=True, tm=1536, tn=1024, tk=1024, name="mm_g_out", epi=_epi_wire(D_MODEL))
    proj = sv["proj"]
    dpre, dxc_u, dgl, dlpar = _lru_scan_bwd(sv["pre"], sv["xc"], proj, sv["lpar"], sv["hs"], dmix, "lru_scan_bwd")
    dxc = _mm(dpre, sv["wab"], tb=True, add=dxc_u, tm=1024, tn=512, tk=1024, name="mm_d_xc")
    gwab = _mm(sv["xc"], dpre, ta=True, tm=512, tn=1024, tk=1024, name="mm_g_lru")
    g["lru_wa"], g["lru_wx"] = _block_diag_extract(gwab[:, :LRU_W]), _block_diag_extract(gwab[:, LRU_W:])
    g["lru_ba"], g["lru_bx"], g["lru_lambda"] = dlpar[0], dlpar[1], dlpar[2]
    dxl, gcw, gcb = _lru_conv_bwd(proj, dxc, p["lru_conv_w"], "lru_conv_bwd")
    g["lru_conv_w"], g["lru_conv_b"] = gcw[:CONV_K], jnp.sum(gcb, axis=0)
    dy, dz, gsn = _ssd_post_bwd(sv["y"], proj, p["ssd_norm"], (dmix, SSD_W, 1), "ssd_post_bwd")
    g["ssd_norm"] = jnp.sum(gsn, axis=0)
    dxconv, ddt, dal, ddk, got_ssd = _ssd_scan_bwd(sv["xconv"], sv["dt"], sv["spar"], sv["states"], dy, "ssd_scan_bwd", comm_ssd)
    g["ssd_a_log"], g["ssd_d"] = dal[0, :8], ddk[0, :8]
    dxbc, ddtr, gsw, gsb, gdb = _ssd_pre_bwd(proj, dxconv, ddt, p["ssd_conv_w"], p["ssd_conv_b"],
                                             _lanes128(p["ssd_dt_bias"]), "ssd_pre_bwd")
    g["ssd_conv_w"], g["ssd_conv_b"], g["ssd_dt_bias"] = gsw[:CONV_K], jnp.sum(gsb, axis=0), jnp.sum(gdb, axis=0)[:8]
    dq, dk, dv, got_att = _att_bwd_rev(proj, dmix, sv["lse"], delta, "att_bwd", None if comm_att is None else comm_att(g))
    dproj = jnp.concatenate([dq, dk, dv, dz, dxbc, dgl, dxl, ddtr], axis=1)
    g["w_in_t"], g["w_in_t@wire"] = _mm(dproj, sv["h"], ta=True, tm=1408, tn=1024, tk=1024, name="mm_g_in", epi=_epi_wire(D_MODEL))
    res = _mm(dproj, p["w_in_t"], tm=1024, tn=1024, tk=1408, name="mm_d_h", comm=None if comm_tail is None else comm_tail(g),
              epi=_epi_rms_bwd(sv["x"], p["norm_mix"], dx1))
    (dx, gm), got_tail = res if comm_tail is not None else (res, [])
    g["norm_mix"] = jnp.sum(gm, axis=0)
    return dx, g, got_ssd, got_att, got_tail


def _grad_slabs(g, names, suffix=""):
    out = {}
    for n in names:
        if n == "w_in":
            out[n] = _w_in_slabs(g["w_in_t" + suffix])
        elif n == "w_gate":
            out[n] = _slabs(n, g["w_gu_t" + suffix][:D_FF])
        elif n == "w_up":
            out[n] = _slabs(n, g["w_gu_t" + suffix][D_FF:])
        else:
            out[n] = _slabs(n, g[n + suffix])
    return out


def kernel(x, norm_mix, w_in, ssd_conv_w, ssd_conv_b, ssd_dt_bias, ssd_a_log, ssd_d, ssd_norm, lru_conv_w, lru_conv_b, lru_wa, lru_ba, lru_wx, lru_bx, lru_lambda, w_out, norm_ffn, w_gate, w_up, w_down, norm_final, loss_target, m_norm_mix, m_w_in, m_ssd_conv_w, m_ssd_conv_b, m_ssd_dt_bias, m_ssd_a_log, m_ssd_d, m_ssd_norm, m_lru_conv_w, m_lru_conv_b, m_lru_wa, m_lru_ba, m_lru_wx, m_lru_bx, m_lru_lambda, m_w_out, m_norm_ffn, m_w_gate, m_w_up, m_w_down, m_norm_final, v_norm_mix, v_w_in, v_ssd_conv_w, v_ssd_conv_b, v_ssd_dt_bias, v_ssd_a_log, v_ssd_d, v_ssd_norm, v_lru_conv_w, v_lru_conv_b, v_lru_wa, v_lru_ba, v_lru_wx, v_lru_bx, v_lru_lambda, v_w_out, v_norm_ffn, v_w_gate, v_w_up, v_w_down, v_norm_final):
    loc = dict(locals())
    w = {n: loc[n] for n in WEIGHTS}
    m = {n: loc["m_" + n] for n in WEIGHTS}
    v = {n: loc["v_" + n] for n in WEIGHTS}
    for n in TRANSPOSED:
        w[n], m[n], v[n] = [jnp.transpose(t, (0, 2, 1)) for t in (w[n], m[n], v[n])]
    wt_in, mt_in, vt_in = [jnp.transpose(t, (2, 0, 1)) for t in (w["w_in"], m["w_in"], v["w_in"])]

    def halves(a):
        return a.reshape(a.shape[0], 2, a.shape[1] // 2, a.shape[2])

    def unhalve(a):
        return a.reshape(4, 2 * a.shape[2], a.shape[3])

    def joined(name, a):
        return _w_in_rows(unhalve(a)) if name == "w_in" else _join(name, unhalve(a))

    wb = {n: halves(w[n].astype(MXU)) for n in MATS[1:]}
    wb["w_in"] = halves(jnp.pad(jnp.transpose(wt_in.astype(MXU), (1, 0, 2)), ((0, 0), (0, W_IN_PAD - W_IN_SHARD), (0, 0))))
    xs = x[0]
    h0, first = _rms_fwd(xs, norm_mix[0], "rms_mix", _Comm(gathers=[(wb["w_in"], 0, True), (w["ssd_conv_w"], None, False),
                                                                    (w["lru_conv_w"], None, False)]))
    convs = {"ssd_conv_w": _join("ssd_conv_w", first[1]), "lru_conv_w": _join("lru_conv_w", first[2])}
    behind_att = [(n, 0) for n in MATS[1:]] + [("w_in", 1)]
    behind_ffn = [(n, 1) for n in MATS[1:]]
    whole = {("w_in", 0): joined("w_in", first[0])}
    params = {}

    def layer_params(l):
        if l not in params:
            p = {n: w[n][l] for n in SMALL if n != "norm_final"}
            p.update(w_in_t=whole["w_in", l], ssd_conv_w=convs["ssd_conv_w"][l], lru_conv_w=convs["lru_conv_w"][l])
            params[l] = p
        if "w_out" not in params[l] and ("w_out", l) in whole:
            params[l].update(w_out=whole["w_out", l], w_down=whole["w_down", l],
                             w_gu_t=jnp.concatenate([whole["w_gate", l], whole["w_up", l]], axis=0))
        return params[l]

    saved = []
    h_in = h0
    for l in range(DEPTH):
        first_layer = l == 0
        mix, sv, got = _layer_mixers(xs, layer_params(l), _Comm(gathers=[(wb[n], k, True) for n, k in behind_att]) if first_layer else None,
                                     h_in)
        whole.update({k: joined(k[0], a) for k, a in zip(behind_att, got)})
        xs, got, h_in = _layer_ffn(xs, mix, layer_params(l), sv, _Comm(gathers=[(wb[n], k, True) for n, k in behind_ffn]) if first_layer else None,
                                   norm_mix[l + 1] if l + 1 < DEPTH else None)
        whole.update({k: joined(k[0], a) for k, a in zip(behind_ffn, got)})
        saved.append(sv)
    dx, gnf, lsum = _loss_head(xs, norm_final, loss_target[0], "loss_head")
    loss = lax.psum(jnp.sum(lsum), ("x", "y", "c"))

    dx, g1, _, _, _ = _layer_bwd(dx, layer_params(1), saved[1])
    def slabs_of(g, names):
        own = _grad_slabs(g, names)
        sent = _grad_slabs(g, [n for n in names if n in MATS], "@wire")
        sent.update({n: own[n] for n in names if n not in MATS})
        return own, sent

    s1, sent1 = slabs_of(g1, BIG)
    att0 = ("w_gate", "w_up", "w_down", "w_out")
    s0, sent0 = {}, {}

    def add0(g0, names):
        own, sent = slabs_of(g0, names)
        s0.update(own)
        sent0.update(sent)

    ssd1 = ("w_gate", "w_up")
    att1 = tuple(n for n in BIG if n not in ssd1)

    def comm_att(g0):
        add0(g0, att0)
        return _Comm(scatters=[sent1[n] for n in att1] + [sent0[n] for n in att0])

    tail0 = ("w_in",) + CONVS

    def comm_tail(g0):
        add0(g0, tail0)
        return _Comm(scatters=[sent0[n] for n in tail0])

    dx, g0, got_ssd, got_att, got_tail = _layer_bwd(dx, layer_params(0), saved[0], _Comm(scatters=[sent1[n] for n in ssd1]),
                                                    comm_att, comm_tail)
    recv = {(n, 1): a for n, a in zip(ssd1, got_ssd)}
    recv.update({(n, 1): a for n, a in zip(att1, got_att[:len(att1)])})
    recv.update({(n, 0): a for n, a in zip(att0, got_att[len(att1):])})
    recv.update({(n, 0): a for n, a in zip(tail0, got_tail)})

    me = 2 * lax.axis_index("x") + lax.axis_index("y")
    slabs = (s0, s1)
    part = {}
    for n in BIG:
        per_layer = []
        for l in range(DEPTH):
            own = lax.dynamic_index_in_dim(slabs[l][n], me, axis=0, keepdims=False)
            per_layer.append(_sum_slots(own, recv[n, l], "sum_chips_" + n, ROW_TILE.get(n, own.shape[0])))
        part[n] = jnp.stack(per_layer, axis=0)
    sib = dict(zip(BIG, _swap_sibling([part[n] for n in BIG])))
    out_g, out_d, out_m, out_v = {}, {}, {}, {}
    for n in BIG:
        if n == "w_in":
            res = _adamw(wt_in, mt_in, vt_in, jnp.transpose(part[n], (1, 0, 2)), jnp.transpose(sib[n], (1, 0, 2)), "adamw_" + n,
                         W_IN_ADAM_TILE, rows_first=True)
            out_g[n], out_d[n], out_m[n], out_v[n] = [jnp.transpose(t, (1, 2, 0)) for t in res]
            continue
        res = _adamw(w[n], m[n], v[n], part[n], sib[n], "adamw_" + n, ROW_TILE.get(n, w[n].shape[1]))
        out_g[n], out_d[n], out_m[n], out_v[n] = [jnp.transpose(t, (0, 2, 1)) for t in res] if n in TRANSPOSED else res

    gsm = {n: jnp.stack([g0[n], g1[n]], axis=0) for n in SMALL if n != "norm_final"}
    gsm["norm_final"] = jnp.sum(gnf, axis=0)
    small_shapes = [w[n].shape for n in SMALL]
    gs = _pack([gsm[n].reshape(w[n].shape) for n in SMALL], BLK, 8, F32)
    gall = _gather_small(gs)
    gsum = _sum_slots(None, gall, "sum_devices", gs.shape[0])
    ws = _pack([w[n] for n in SMALL], BLK, 8, F32)
    ms = _pack([m[n] for n in SMALL], BLK, 8, F32)
    vs = _pack([v[n] for n in SMALL], BLK, 8, F32)
    gsr, dsr, nms, nvs = _adamw(ws, ms, vs, gsum, None, "adamw_small", gs.shape[0])
    out_g.update(zip(SMALL, _unpack(gsr, small_shapes)))
    out_d.update(zip(SMALL, _unpack(dsr, small_shapes)))
    out_m.update(zip(SMALL, _unpack(nms, small_shapes)))
    out_v.update(zip(SMALL, _unpack(nvs, small_shapes)))

    return (loss, dx[None], *[out_g[n] for n in WEIGHTS], *[out_d[n] for n in WEIGHTS],
            *[out_m[n] for n in WEIGHTS], *[out_v[n] for n in WEIGHTS])
```

```python
import functools
import math

import jax
import jax.numpy as jnp
import numpy as np
from jax import lax
from jax.experimental import pallas as pl
from jax.experimental.pallas import tpu as pltpu

F32 = jnp.float32
MXU = jnp.bfloat16
HI = lax.Precision.HIGHEST
HIGH = lax.Precision.HIGH
MESH = pl.DeviceIdType.MESH

D_MODEL = 1024
DEPTH = 2
HEAD_DIM = 64
ATT_W = 512
ATT_PATTERNS = ((128, 1), (512, 4), (2048, 16))
BLK = 128
SSD_W = 512
SSD_STATE = 128
LRU_W = 512
LRU_BLOCKS = 8
LRU_C = 8.0
CONV_K = 4
D_MIX = 1536
D_FF = 2816
IN_COLS = 4104
NP = 4224
NORM_EPS = 1e-6
SSD_NORM_EPS = 1e-5
LN2 = math.log(2.0)
NEG = -1e30

ADAM_LR, ADAM_B1, ADAM_B2, ADAM_EPS, ADAM_WD, ADAM_STEP = 0.001, 0.9, 0.999, 1e-08, 0.01, 10
BC1 = 1.0 - ADAM_B1 ** ADAM_STEP
BC2 = 1.0 - ADAM_B2 ** ADAM_STEP

VMEM_LIMIT = 56 * 1024 * 1024

C_Q, C_K, C_V, C_Z, C_XBC, C_G, C_XL, C_DT = 0, 512, 1024, 1536, 2048, 3072, 3584, 4096


def _cp(*sem):
    return pltpu.CompilerParams(dimension_semantics=sem, vmem_limit_bytes=VMEM_LIMIT)


def _dot(a, b, dims, prec=None):
    return lax.dot_general(a, b, (dims, ((), ())), preferred_element_type=F32, precision=prec)


def _nn(a, b, prec=None):
    return _dot(a, b, ((1,), (0,)), prec)


def _nt(a, b, prec=None):
    return _dot(a, b, ((1,), (1,)), prec)


def _tn(a, b, prec=None):
    return _dot(a, b, ((0,), (0,)), prec)


def _sigmoid(x):
    return jax.nn.sigmoid(x)


def _silu(x):
    return x * _sigmoid(x)


def _softplus(x):
    return jnp.maximum(x, 0.0) + jnp.log(1.0 + jnp.exp(-jnp.abs(x)))


def _gelu(x):
    return 0.5 * x * (1.0 + jnp.tanh(0.7978845608028654 * (x + 0.044715 * x * x * x)))


def _mm(a, b, *, ta=False, tb=False, add=None, out_dtype=F32, tm, tn, tk, name, comm=None, epi=None):
    m, k = (a.shape[1], a.shape[0]) if ta else a.shape
    n = b.shape[0] if tb else b.shape[1]
    assert (b.shape[1] if tb else b.shape[0]) == k
    assert m % tm == 0 and n % tn == 0 and k % tk == 0, (name, m, n, k)
    nk = k // tk
    a_spec = pl.BlockSpec((tk, tm), lambda i, j, kk: (kk, i)) if ta else pl.BlockSpec((tm, tk), lambda i, j, kk: (i, kk))
    b_spec = pl.BlockSpec((tn, tk), lambda i, j, kk: (j, kk)) if tb else pl.BlockSpec((tk, tn), lambda i, j, kk: (kk, j))
    o_spec = pl.BlockSpec((tm, tn), lambda i, j, kk: (i, j))
    dims = ((0 if ta else 1,), (1 if tb else 0,))
    carried = comm is not None
    comm = comm or _Comm()
    ni, nj = m // tm, n // tn
    efn, erows, econsts, eouts, eaccs = epi or (None, [], [], [], [])
    assert epi is None or nj == 1
    nadd = 0 if add is None else 1
    ner, nec, neo, nea = len(erows), len(econsts), len(eouts), len(eaccs)

    def body(*refs):
        refs, cm = comm.split(refs, 2 + nadd + ner + nec, 1 + neo + nea, 1)
        a_ref, b_ref = refs[:2]
        er_refs = refs[2 + nadd:2 + nadd + ner]
        ec_refs = refs[2 + nadd + ner:2 + nadd + ner + nec]
        o_ref = refs[2 + nadd + ner + nec]
        eo_refs = refs[3 + nadd + ner + nec:3 + nadd + ner + nec + neo]
        ea_refs = refs[3 + nadd + ner + nec + neo:3 + nadd + ner + nec + neo + nea]
        acc = refs[-1]
        i, j, kk = pl.program_id(0), pl.program_id(1), pl.program_id(2)
        comm.start_at((i == 0) & (j == 0) & (kk == 0), cm)

        @pl.when(kk == 0)
        def _():
            acc[...] = jnp.zeros_like(acc)

        acc[...] += _dot(a_ref[...].astype(MXU), b_ref[...].astype(MXU), dims)

        @pl.when(kk == nk - 1)
        def _():
            r = acc[...]
            if add is not None:
                r = r + refs[2][...]
            if efn is None:
                o_ref[...] = r.astype(out_dtype)
            else:
                main, extra, sums = efn(r, [t[...] for t in er_refs], [t[...] for t in ec_refs])
                o_ref[...] = main.astype(out_dtype)
                for t, val in zip(eo_refs, extra):
                    t[...] = val.astype(t.dtype)
                @pl.when(i == 0)
                def _():
                    for t, val in zip(ea_refs, sums):
                        t[...] = val

                @pl.when(i > 0)
                def _():
                    for t, val in zip(ea_refs, sums):
                        t[...] += val

        comm.wait_at((i == ni - 1) & (j == nj - 1) & (kk == nk - 1), cm)

    def whole_rows(width):
        return pl.BlockSpec((tm, width), lambda i, j, kk: (i, 0))

    ins = [a, b] + ([] if add is None else [add]) + list(erows) + list(econsts)
    specs = [a_spec, b_spec] + ([] if add is None else [o_spec]) + [whole_rows(t.shape[1]) for t in erows]
    specs += [pl.BlockSpec(t.shape, lambda i, j, kk: (0, 0)) for t in econsts]
    out_specs = [o_spec] + [whole_rows(wd) for wd, _ in eouts] + [pl.BlockSpec((r, wd), lambda i, j, kk: (0, 0)) for r, wd in eaccs]
    out_shape = [jax.ShapeDtypeStruct((m, n), out_dtype)] + [jax.ShapeDtypeStruct((m, wd), dt) for wd, dt in eouts]
    out_shape += [jax.ShapeDtypeStruct((r, wd), F32) for r, wd in eaccs]
    serial = comm.n or nea
    res = pl.pallas_call(
        body, name=name, grid=(ni, nj, nk), in_specs=specs + [ANY] * comm.n, out_specs=out_specs + [ANY] * comm.n,
        out_shape=out_shape + comm.out_shape(),
        scratch_shapes=[pltpu.VMEM((tm, tn), F32)] + comm.scratch(),
        compiler_params=_cp(*((["arbitrary"] * 3) if serial else ["parallel", "parallel", "arbitrary"])),
    )(*ins, *comm.args())
    nown = 1 + neo + nea
    own = res[0] if epi is None else list(res[:nown])
    return (own, list(res[nown:])) if carried else own


def _rows(fn, rows, consts=(), outs=(), accs=(), *, tile, name, halos=(), comm=None):
    rows = [r if isinstance(r, tuple) else (r, r.shape[1], 0) for r in rows]
    s = rows[0][0].shape[0]
    assert s % tile == 0 and tile % 8 == 0
    n = s // tile
    t8 = tile // 8
    nr, nh, nc_, no, na = len(rows), len(halos), len(consts), len(outs), len(accs)
    carried = comm is not None
    comm = comm or _Comm()

    def body(*refs):
        refs, cm = comm.split(refs, nr + nh + nc_, no + na, 0)
        i = pl.program_id(0)
        comm.start_at(i == 0, cm)
        rv = [r[...] for r in refs[:nr]]
        hv = []
        for (idx, kind), r in zip(halos, refs[nr:nr + nh]):
            edge = (i == 0) if kind == "prev" else (i == n - 1)
            hv.append(jnp.where(edge, 0.0, r[...]))
        cv = [r[...] for r in refs[nr + nh:nr + nh + nc_]]
        o_refs = refs[nr + nh + nc_:nr + nh + nc_ + no]
        a_refs = refs[nr + nh + nc_ + no:]
        ov, av = fn(rv, hv, cv)
        for r, v in zip(o_refs, ov):
            r[...] = v.astype(r.dtype)
        if na:
            @pl.when(i == 0)
            def _():
                for r in a_refs:
                    r[...] = jnp.zeros_like(r)
            for r, v in zip(a_refs, av):
                r[...] += v
        comm.wait_at(i == n - 1, cm)

    in_specs = [pl.BlockSpec((tile, w), functools.partial(lambda i, cb: (i, cb), cb=cb)) for (_, w, cb) in rows]
    for idx, kind in halos:
        _, w, cb = rows[idx]
        if kind == "prev":
            in_specs.append(pl.BlockSpec((8, w), functools.partial(lambda i, cb: (jnp.maximum(i * t8 - 1, 0), cb), cb=cb)))
        else:
            in_specs.append(pl.BlockSpec((8, w), functools.partial(lambda i, cb: (jnp.minimum((i + 1) * t8, n * t8 - 1), cb), cb=cb)))
    in_specs += [pl.BlockSpec(c.shape, functools.partial(lambda i, nd: (0,) * nd, nd=c.ndim)) for c in consts]
    out_specs = [pl.BlockSpec((tile, c), lambda i: (i, 0)) for (c, _) in outs]
    out_specs += [pl.BlockSpec((r, c), lambda i: (0, 0)) for (r, c) in accs]
    out_shape = [jax.ShapeDtypeStruct((s, c), dt) for (c, dt) in outs]
    out_shape += [jax.ShapeDtypeStruct((r, c), F32) for (r, c) in accs]
    args = [r[0] for r in rows] + [rows[idx][0] for idx, _ in halos] + list(consts)
    res = pl.pallas_call(
        body, name=name, grid=(n,), in_specs=in_specs + [ANY] * comm.n, out_specs=out_specs + [ANY] * comm.n,
        out_shape=out_shape + comm.out_shape(), scratch_shapes=comm.scratch(), compiler_params=_cp("arbitrary"),
    )(*args, *comm.args())
    return (list(res[:no + na]), list(res[no + na:])) if carried else list(res)


def _colsum8(v):
    t, c = v.shape
    return jnp.sum(v.reshape(t // 8, 8, c), axis=0)


def _rms(x, g):
    return x * lax.rsqrt(jnp.mean(x * x, axis=-1, keepdims=True) + NORM_EPS) * g


def _epi_rms(g):
    return (lambda r, rows, consts: (r, [_rms(r, consts[0])], []), [], [g.reshape(1, -1)], [(g.shape[-1], MXU)], [])


def _epi_rms_bwd(x, g, dres):
    def fn(r, rows, consts):
        xb, drb = rows
        _, vjp = jax.vjp(_rms, xb, consts[0])
        rstd = lax.rsqrt(jnp.mean(xb * xb, axis=-1, keepdims=True) + NORM_EPS)
        return drb + vjp(r)[0], [], [_colsum8(r * xb * rstd)]
    return (fn, [x, dres], [g.reshape(1, -1)], [], [(8, g.shape[-1])])


def _epi_att_stats(att, lse):
    def fn(r, rows, consts):
        hr = lax.broadcasted_iota(jnp.int32, (ATT_W, ATT_W), 0) // HEAD_DIM
        hc = lax.broadcasted_iota(jnp.int32, (ATT_W, ATT_W), 1) // HEAD_DIM
        delta = _nn(r[:, :ATT_W] * rows[0], (hr == hc).astype(F32), HIGH)
        lane = lax.broadcasted_iota(jnp.int32, delta.shape, 1)
        return r, [jnp.where(lane % HEAD_DIM < HEAD_DIM // 2, rows[1], delta)], []
    return (fn, [att, lse], [], [(ATT_W, F32)], [])


def _epi_wire(width):
    return (lambda r, rows, consts: (r, [r], []), [], [], [(width, MXU)], [])


def _rms_fwd(x, g, name, comm=None):
    def fn(rv, hv, cv):
        return [_rms(rv[0], cv[0])], []
    res = _rows(fn, [x], [g.reshape(1, -1)], [(x.shape[1], MXU)], tile=512, name=name, comm=comm)
    return res[0] if comm is None else (res[0][0], res[1])


def _rms_bwd(x, g, dh, dres, name):
    def fn(rv, hv, cv):
        xb, dhb, drb = rv
        _, vjp = jax.vjp(_rms, xb, cv[0])
        dx, _ = vjp(dhb)
        rstd = lax.rsqrt(jnp.mean(xb * xb, axis=-1, keepdims=True) + NORM_EPS)
        return [drb + dx], [_colsum8(dhb * xb * rstd)]
    d = x.shape[1]
    return _rows(fn, [x, dh, dres], [g.reshape(1, -1)], [(d, F32)], [(8, d)], tile=512, name=name)


def _slope_dist(hp, hh, dist, dil):
    hf = (2 * hp + hh + 1).astype(F32)
    slope = jnp.exp(jnp.zeros(dist.shape, F32) - hf * LN2)
    return slope * (dist.astype(F32) * float(dil))


def _att_delta(datt, att, name):
    def fn(rv, hv, cv):
        r = lax.broadcasted_iota(jnp.int32, (ATT_W, ATT_W), 0) // HEAD_DIM
        c = lax.broadcasted_iota(jnp.int32, (ATT_W, ATT_W), 1) // HEAD_DIM
        ones = (r == c).astype(F32)
        return [_nn(rv[0] * rv[1], ones, HI)], []
    return _rows(fn, [datt, att], [], [(ATT_W, F32)], tile=512, name=name)[0]


ATT_G = 2048


def _deinterleave(dst, src, dil, ld, region, offset):
    for r in range(dil):
        rows = pl.ds(r, ld, stride=dil) if dil > 1 else pl.ds(0, ld)
        dst[r * region + offset:r * region + offset + ld, :] = src[rows, :]


def _deinterleave_edge(dst, src, dil, region, offset, first_row):
    for r in range(dil):
        rows = pl.ds(first_row + r, BLK, stride=dil) if dil > 1 else pl.ds(first_row, BLK)
        dst[r * region + offset:r * region + offset + BLK, :] = src[rows, :]


def _att_fwd_fused(proj, name, comm=None):
    s, npc = proj.shape
    gsz = ATT_G
    ng = s // gsz
    assert s % gsz == 0
    scale = HEAD_DIM ** -0.5
    comm = comm or _Comm()

    def body(*refs):
        (q_ref, kp_ref, kc_ref, vp_ref, vc_ref, att_ref, lse_ref, attb_ref, qd, kd, vd, nd, md, dd, nn, mn, dn), cm = comm.split(refs, 5, 3, 9)
        hp, g = pl.program_id(0), pl.program_id(1)
        comm.start_at((hp == 0) & (g == 0), cm)
        lane = lax.broadcasted_iota(jnp.int32, (BLK, BLK), 1)
        qi = lax.broadcasted_iota(jnp.int32, (BLK, 2 * BLK), 0)
        ki = lax.broadcasted_iota(jnp.int32, (BLK, 2 * BLK), 1)
        dist = BLK + qi - ki
        band = (dist >= 0) & (dist <= BLK)
        for pi, (_, dil) in enumerate(ATT_PATTERNS):
            ld = gsz // dil
            nbg = ld // BLK
            _deinterleave(qd, q_ref, dil, ld, ld, 0)
            _deinterleave(kd, kc_ref, dil, ld, ld + BLK, BLK)
            _deinterleave(vd, vc_ref, dil, ld, ld + BLK, BLK)
            _deinterleave_edge(kd, kp_ref, dil, ld + BLK, 0, gsz - BLK * dil)
            _deinterleave_edge(vd, vp_ref, dil, ld + BLK, 0, gsz - BLK * dil)
            bias = [_slope_dist(hp, hh, dist, dil) for hh in (0, 1)]

            def tile(t, carry, ld=ld, nbg=nbg, bias=bias):
                r, b = t // nbg, t % nbg
                qo = pl.multiple_of(r * ld + b * BLK, BLK)
                ko = pl.multiple_of(r * (ld + BLK) + b * BLK, BLK)
                q = qd[pl.ds(qo, BLK), :]
                kk = kd[pl.ds(ko, 2 * BLK), :].astype(MXU)
                vv = vd[pl.ds(ko, 2 * BLK), :].astype(MXU)
                valid = band & ((g > 0) | (b > 0) | (ki >= BLK))
                num = jnp.zeros((BLK, BLK), F32)
                mx = jnp.zeros((BLK, BLK), F32)
                den = jnp.zeros((BLK, BLK), F32)
                for hh in (0, 1):
                    hmask = (lane < HEAD_DIM) if hh == 0 else (lane >= HEAD_DIM)
                    qm = jnp.where(hmask, q, 0.0).astype(MXU)
                    sc = jnp.where(valid, _nt(qm, kk) * scale - bias[hh], NEG)
                    m = jnp.max(sc, axis=1, keepdims=True)
                    p = jnp.exp(sc - m)
                    dn_ = jnp.sum(p, axis=1, keepdims=True)
                    o = _nn(p.astype(MXU), vv)
                    num = jnp.where(hmask, o, num)
                    mx = jnp.where(hmask, m, mx)
                    den = jnp.where(hmask, dn_, den)
                nd[pl.ds(qo, BLK), :] = num
                md[pl.ds(qo, BLK), :] = mx
                dd[pl.ds(qo, BLK), :] = den
                return carry

            lax.fori_loop(0, dil * nbg, tile, 0, unroll=8)
            for r in range(dil):
                rows = pl.ds(r, ld, stride=dil) if dil > 1 else pl.ds(0, ld)
                nn.at[pi][rows, :] = nd[r * ld:(r + 1) * ld, :]
                mn.at[pi][rows, :] = md[r * ld:(r + 1) * ld, :]
                dn.at[pi][rows, :] = dd[r * ld:(r + 1) * ld, :]

        def merge(c, carry):
            rows = pl.ds(pl.multiple_of(c * 256, 256), 256)
            ms = [mn[pi, rows, :] for pi in range(len(ATT_PATTERNS))]
            m_all = functools.reduce(jnp.maximum, ms)
            num = jnp.zeros((256, BLK), F32)
            den = jnp.zeros((256, BLK), F32)
            for pi in range(len(ATT_PATTERNS)):
                e = jnp.exp(ms[pi] - m_all)
                num = num + nn[pi, rows, :] * e
                den = den + dn[pi, rows, :] * e
            att = num / den
            att_ref[rows, :] = att
            attb_ref[rows, :] = att.astype(MXU)
            lse_ref[rows, :] = m_all + jnp.log(den)
            return carry

        lax.fori_loop(0, gsz // 256, merge, 0)
        comm.wait_at((hp == 3) & (g == ng - 1), cm)

    def cur(base):
        return pl.BlockSpec((gsz, BLK), lambda hp, g: (g, base // BLK + hp))

    def prev(base):
        return pl.BlockSpec((gsz, BLK), lambda hp, g: (jnp.maximum(g - 1, 0), base // BLK + hp))

    o_spec = pl.BlockSpec((gsz, BLK), lambda hp, g: (g, hp))
    npat = len(ATT_PATTERNS)
    res = pl.pallas_call(
        body, name=name, grid=(4, ng),
        in_specs=[cur(C_Q), prev(C_K), cur(C_K), prev(C_V), cur(C_V)] + [ANY] * comm.n,
        out_specs=[o_spec] * 3 + [ANY] * comm.n,
        out_shape=[jax.ShapeDtypeStruct((s, ATT_W), F32)] * 2 + [jax.ShapeDtypeStruct((s, ATT_W), MXU)] + comm.out_shape(),
        scratch_shapes=[pltpu.VMEM((gsz, BLK), F32), pltpu.VMEM((2 * gsz, BLK), F32), pltpu.VMEM((2 * gsz, BLK), F32)]
        + [pltpu.VMEM((gsz, BLK), F32)] * 3 + [pltpu.VMEM((npat, gsz, BLK), F32)] * 3 + comm.scratch(),
        compiler_params=_cp("arbitrary", "arbitrary"),
    )(proj, proj, proj, proj, proj, *comm.args())
    return res[0], res[1], res[2], list(res[3:])


def _att_bwd_fused(proj, datt, lse, delta, name, comm=None):
    s, npc = proj.shape
    gsz = ATT_G
    ng = s // gsz
    scale = HEAD_DIM ** -0.5
    comm = comm or _Comm()

    def body(*refs):
        (qc_ref, qn_ref, kp_ref, kc_ref, vp_ref, vc_ref, doc_ref, don_ref, lsc_ref, lsn_ref, dlc_ref, dln_ref,
         dq_ref, dk_ref, dv_ref, qd, dod, lsd, dld, kd, vd, dqd, dkd, dvd), cm = comm.split(refs, 12, 3, 9)
        hp, g = pl.program_id(0), pl.program_id(1)
        comm.start_at((hp == 0) & (g == 0), cm)
        lane = lax.broadcasted_iota(jnp.int32, (BLK, BLK), 1)
        qi = lax.broadcasted_iota(jnp.int32, (BLK, BLK), 0)
        ki = lax.broadcasted_iota(jnp.int32, (BLK, BLK), 1)
        d_far = BLK + qi - ki
        d_near = qi - ki
        for pi, (_, dil) in enumerate(ATT_PATTERNS):
            ld = gsz // dil
            nbg = ld // BLK
            reg = ld + BLK
            for dst, c_ref, n_ref in ((qd, qc_ref, qn_ref), (dod, doc_ref, don_ref), (lsd, lsc_ref, lsn_ref), (dld, dlc_ref, dln_ref)):
                _deinterleave(dst, c_ref, dil, ld, reg, 0)
                _deinterleave_edge(dst, n_ref, dil, reg, ld, 0)
            for dst, p_ref, c_ref in ((kd, kp_ref, kc_ref), (vd, vp_ref, vc_ref)):
                _deinterleave(dst, c_ref, dil, ld, reg, BLK)
                _deinterleave_edge(dst, p_ref, dil, reg, 0, gsz - BLK * dil)
            b_far = [_slope_dist(hp, hh, d_far, dil) for hh in (0, 1)]
            b_near = [_slope_dist(hp, hh, d_near, dil) for hh in (0, 1)]

            def tile(t, carry, ld=ld, nbg=nbg, reg=reg, b_far=b_far, b_near=b_near):
                r, b = t // nbg, t % nbg
                oo = pl.multiple_of(r * ld + b * BLK, BLK)
                ro = pl.multiple_of(r * reg + b * BLK, BLK)
                qn, qx = qd[pl.ds(ro, BLK), :], qd[pl.ds(ro + BLK, BLK), :]
                don, dox = dod[pl.ds(ro, BLK), :], dod[pl.ds(ro + BLK, BLK), :]
                lsn, lsx = lsd[pl.ds(ro, BLK), :], lsd[pl.ds(ro + BLK, BLK), :]
                dln, dlx = dld[pl.ds(ro, BLK), :], dld[pl.ds(ro + BLK, BLK), :]
                kp, kc = kd[pl.ds(ro, BLK), :].astype(MXU), kd[pl.ds(ro + BLK, BLK), :].astype(MXU)
                vp, vc = vd[pl.ds(ro, BLK), :].astype(MXU), vd[pl.ds(ro + BLK, BLK), :].astype(MXU)
                ok_a = (d_far <= BLK) & ((g > 0) | (b > 0))
                ok_b = d_near >= 0
                ok_c = (d_far <= BLK) & ((g < ng - 1) | (b < nbg - 1))

                def grads(qm, dom, k, v, ls, dl, bias, valid, hh):
                    c0 = hh * HEAD_DIM
                    sc = _nt(qm, k) * scale - bias
                    p = jnp.exp(jnp.where(valid, sc - ls[:, c0:c0 + 1], NEG))
                    ds = p * (_nt(dom, v) - dl[:, c0:c0 + 1])
                    return p.astype(MXU), ds.astype(MXU)

                dq = jnp.zeros((BLK, BLK), F32)
                dk = jnp.zeros((BLK, BLK), F32)
                dv = jnp.zeros((BLK, BLK), F32)
                for hh in (0, 1):
                    hmask = (lane < HEAD_DIM) if hh == 0 else (lane >= HEAD_DIM)
                    qnm = jnp.where(hmask, qn, 0.0).astype(MXU)
                    qxm = jnp.where(hmask, qx, 0.0).astype(MXU)
                    donm = jnp.where(hmask, don, 0.0).astype(MXU)
                    doxm = jnp.where(hmask, dox, 0.0).astype(MXU)
                    _, ds_a = grads(qnm, donm, kp, vp, lsn, dln, b_far[hh], ok_a, hh)
                    p_b, ds_b = grads(qnm, donm, kc, vc, lsn, dln, b_near[hh], ok_b, hh)
                    p_c, ds_c = grads(qxm, doxm, kc, vc, lsx, dlx, b_far[hh], ok_c, hh)
                    dq = jnp.where(hmask, _nn(ds_a, kp) + _nn(ds_b, kc), dq)
                    dk = dk + _tn(ds_b, qnm) + _tn(ds_c, qxm)
                    dv = dv + _tn(p_b, donm) + _tn(p_c, doxm)
                dqd[pl.ds(oo, BLK), :] = dq * scale
                dkd[pl.ds(oo, BLK), :] = dk * scale
                dvd[pl.ds(oo, BLK), :] = dv
                return carry

            lax.fori_loop(0, dil * nbg, tile, 0, unroll=4)
            for out, src in ((dq_ref, dqd), (dk_ref, dkd), (dv_ref, dvd)):
                for r in range(dil):
                    rows = pl.ds(r, ld, stride=dil) if dil > 1 else pl.ds(0, ld)
                    if pi == 0:
                        out[rows, :] = src[r * ld:(r + 1) * ld, :]
                    else:
                        out[rows, :] = out[rows, :] + src[r * ld:(r + 1) * ld, :]
        comm.wait_at((hp == 3) & (g == ng - 1), cm)

    def pspec(base, shift):
        return pl.BlockSpec((gsz, BLK), lambda hp, g: (jnp.clip(g + shift, 0, ng - 1), base // BLK + hp))

    def wspec(shift):
        return pl.BlockSpec((gsz, BLK), lambda hp, g: (jnp.clip(g + shift, 0, ng - 1), hp))

    in_specs = [pspec(C_Q, 0), pspec(C_Q, 1), pspec(C_K, -1), pspec(C_K, 0), pspec(C_V, -1), pspec(C_V, 0),
                wspec(0), wspec(1), wspec(0), wspec(1), wspec(0), wspec(1)] + [ANY] * comm.n
    res = pl.pallas_call(
        body, name=name, grid=(4, ng), in_specs=in_specs,
        out_specs=[wspec(0)] * 3 + [ANY] * comm.n,
        out_shape=[jax.ShapeDtypeStruct((s, ATT_W), F32)] * 3 + comm.out_shape(),
        scratch_shapes=[pltpu.VMEM((2 * gsz, BLK), F32)] * 6 + [pltpu.VMEM((gsz, BLK), F32)] * 3 + comm.scratch(),
        compiler_params=_cp("arbitrary", "arbitrary"),
    )(proj, proj, proj, proj, proj, proj, datt, datt, lse, lse, delta, delta, *comm.args())
    return res[0], res[1], res[2], list(res[3:])


def _att_bwd_rev(proj, datt, stats, name, comm=None):
    s, npc = proj.shape
    gsz = ATT_G
    ng = s // gsz
    npat = len(ATT_PATTERNS)
    scale = HEAD_DIM ** -0.5
    comm = comm or _Comm()

    def body(*refs):
        (q_ref, kp_ref, kc_ref, vp_ref, vc_ref, do_ref, st_ref, dq_out, dk_out, dv_out,
         qd, dod, std, kd, vd, dqd, dkc, dvc, dkp, dvp, kcar, vcar, dq_ref, dk_ref, dv_ref), cm = comm.split(refs, 7, 3, 15)
        hp, gi = pl.program_id(0), pl.program_id(1)
        g = ng - 1 - gi
        comm.start_at((hp == 0) & (gi == 0), cm)

        @pl.when(gi == 0)
        def _():
            kcar[...] = jnp.zeros_like(kcar)
            vcar[...] = jnp.zeros_like(vcar)

        lane = lax.broadcasted_iota(jnp.int32, (BLK, BLK), 1)
        qi = lax.broadcasted_iota(jnp.int32, (BLK, 2 * BLK), 0)
        ki = lax.broadcasted_iota(jnp.int32, (BLK, 2 * BLK), 1)
        dist = BLK + qi - ki
        band = (dist >= 0) & (dist <= BLK)
        for pi, (_, dil) in enumerate(ATT_PATTERNS):
            ld = gsz // dil
            nbg = ld // BLK
            reg = ld + BLK
            for dst, src in ((qd, q_ref), (dod, do_ref), (std, st_ref)):
                _deinterleave(dst, src, dil, ld, ld, 0)
            for dst, p_ref, c_ref in ((kd, kp_ref, kc_ref), (vd, vp_ref, vc_ref)):
                _deinterleave(dst, c_ref, dil, ld, reg, BLK)
                _deinterleave_edge(dst, p_ref, dil, reg, 0, gsz - BLK * dil)
            bias = [_slope_dist(hp, hh, dist, dil) for hh in (0, 1)]

            def tile(t, carry, ld=ld, nbg=nbg, reg=reg, bias=bias):
                r, b = t // nbg, t % nbg
                oo = pl.multiple_of(r * ld + b * BLK, BLK)
                ko = pl.multiple_of(r * reg + b * BLK, BLK)
                q, do = qd[pl.ds(oo, BLK), :], dod[pl.ds(oo, BLK), :]
                st = std[pl.ds(oo, BLK), :]
                kk = kd[pl.ds(ko, 2 * BLK), :].astype(MXU)
                vv = vd[pl.ds(ko, 2 * BLK), :].astype(MXU)
                valid = band & ((g > 0) | (b > 0) | (ki >= BLK))
                dq = jnp.zeros((BLK, BLK), F32)
                dkk = jnp.zeros((2 * BLK, BLK), F32)
                dvv = jnp.zeros((2 * BLK, BLK), F32)
                for hh in (0, 1):
                    c0 = hh * HEAD_DIM
                    hmask = (lane < HEAD_DIM) if hh == 0 else (lane >= HEAD_DIM)
                    qm = jnp.where(hmask, q, 0.0).astype(MXU)
                    dom = jnp.where(hmask, do, 0.0).astype(MXU)
                    sc = _nt(qm, kk) * scale - bias[hh]
                    p = jnp.exp(jnp.where(valid, sc - st[:, c0:c0 + 1], NEG))
                    ds = (p * (_nt(dom, vv) - st[:, c0 + HEAD_DIM // 2:c0 + HEAD_DIM // 2 + 1])).astype(MXU)
                    dq = jnp.where(hmask, _nn(ds, kk), dq)
                    dkk = dkk + _tn(ds, qm)
                    dvv = dvv + _tn(p.astype(MXU), dom)
                dqd[pl.ds(oo, BLK), :] = dq * scale
                dkp[pl.ds(oo, BLK), :] = dkk[:BLK] * scale
                dkc[pl.ds(oo, BLK), :] = dkk[BLK:] * scale
                dvp[pl.ds(oo, BLK), :] = dvv[:BLK]
                dvc[pl.ds(oo, BLK), :] = dvv[BLK:]
                return carry

            lax.fori_loop(0, dil * nbg, tile, 0, unroll=8)
            for r in range(dil):
                rows = pl.ds(r, ld, stride=dil) if dil > 1 else pl.ds(0, ld)
                lo, hi = r * ld, (r + 1) * ld
                edge = slice(pi * gsz + r * BLK, pi * gsz + (r + 1) * BLK)
                for out, cur, prv, car in ((dk_ref, dkc, dkp, kcar), (dv_ref, dvc, dvp, vcar)):
                    later = car[edge, :] if nbg == 1 else jnp.concatenate([prv[lo + BLK:hi, :], car[edge, :]], axis=0)
                    total = cur[lo:hi, :] + later
                    car[edge, :] = prv[lo:lo + BLK, :]
                    out[rows, :] = total if pi == 0 else out[rows, :] + total
                dq_ref[rows, :] = dqd[lo:hi, :] if pi == 0 else dq_ref[rows, :] + dqd[lo:hi, :]
        for out, acc in ((dq_out, dq_ref), (dk_out, dk_ref), (dv_out, dv_ref)):
            out[...] = acc[...].astype(out.dtype)
        comm.wait_at((hp == 3) & (gi == ng - 1), cm)

    def pspec(base, shift):
        return pl.BlockSpec((gsz, BLK), lambda hp, gi: (jnp.maximum(ng - 1 - gi + shift, 0), base // BLK + hp))

    wspec = pl.BlockSpec((gsz, BLK), lambda hp, gi: (ng - 1 - gi, hp))
    in_specs = [pspec(C_Q, 0), pspec(C_K, -1), pspec(C_K, 0), pspec(C_V, -1), pspec(C_V, 0), wspec, wspec] + [ANY] * comm.n
    res = pl.pallas_call(
        body, name=name, grid=(4, ng), in_specs=in_specs,
        out_specs=[wspec] * 3 + [ANY] * comm.n,
        out_shape=[jax.ShapeDtypeStruct((s, ATT_W), MXU)] * 3 + comm.out_shape(),
        scratch_shapes=[pltpu.VMEM((gsz, BLK), F32)] * 3 + [pltpu.VMEM((2 * gsz, BLK), F32)] * 2
        + [pltpu.VMEM((gsz, BLK), F32)] * 5 + [pltpu.VMEM((npat * gsz, BLK), F32)] * 2 + [pltpu.VMEM((gsz, BLK), F32)] * 3
        + comm.scratch(),
        compiler_params=_cp("arbitrary", "arbitrary"),
    )(proj, proj, proj, proj, proj, datt, stats, *comm.args())
    return res[0], res[1], res[2], list(res[3:])


def _shift_down(cur, halo, sft):
    if sft == 0:
        return cur
    t = cur.shape[0]
    rolled = pltpu.roll(cur, sft, 0)
    hr = pltpu.roll(halo, sft, 0)
    row = lax.broadcasted_iota(jnp.int32, cur.shape, 0)
    return jnp.where(row < sft, jnp.tile(hr, (t // 8, 1)), rolled)


def _shift_up(cur, halo, sft):
    if sft == 0:
        return cur
    t = cur.shape[0]
    rolled = pltpu.roll(cur, t - sft, 0)
    hr = pltpu.roll(halo, 8 - sft, 0)
    row = lax.broadcasted_iota(jnp.int32, cur.shape, 0)
    return jnp.where(row >= t - sft, jnp.tile(hr, (t // 8, 1)), rolled)


def _conv(x, xh, w, b):
    y = b + x * w[CONV_K - 1:CONV_K]
    for k in range(CONV_K - 1):
        y = y + _shift_down(x, xh, CONV_K - 1 - k) * w[k:k + 1]
    return y


def _conv_bwd(x, xh, dy, dyh, w):
    dx = dy * w[CONV_K - 1:CONV_K]
    dws = []
    for k in range(CONV_K - 1):
        sft = CONV_K - 1 - k
        dx = dx + _shift_up(dy, dyh, sft) * w[k:k + 1]
        dws.append(jnp.sum(dy * _shift_down(x, xh, sft), axis=0, keepdims=True))
    dws.append(jnp.sum(dy * x, axis=0, keepdims=True))
    c = x.shape[1]
    dw = jnp.concatenate(dws + [jnp.zeros((8 - CONV_K, c), F32)], axis=0)
    return dx, dw, jnp.sum(dy, axis=0, keepdims=True)


def _pad8(w):
    return jnp.concatenate([w, jnp.zeros((8 - w.shape[0], w.shape[1]), w.dtype)], axis=0)


def _ssd_pre(proj, conv_w, conv_b, dt_bias128, name):
    def fn(rv, hv, cv):
        xbc, dtr = rv
        return [_silu(_conv(xbc, hv[0], cv[0], cv[1])), _softplus(dtr + cv[2])], []
    return _rows(fn, [(proj, 1024, C_XBC // 1024), (proj, BLK, C_DT // BLK)],
                 [_pad8(conv_w), conv_b.reshape(1, -1), dt_bias128],
                 [(1024, F32), (BLK, F32)], tile=256, name=name, halos=[(0, "prev")])


def _ssd_pre_bwd(proj, dxc, ddt, conv_w, conv_b, dt_bias128, name):
    def fn(rv, hv, cv):
        xbc, dtr, dxcb, ddtb = rv
        xh, dxch_raw, xnext = hv
        w, b, bias = cv
        pre = _conv(xbc, xh, w, b)
        sg = _sigmoid(pre)
        dpre = dxcb * (sg * (1.0 + pre * (1.0 - sg)))
        t = xbc.shape[0]
        tail = jnp.concatenate([xbc[t - 8:], xnext], axis=0)
        pre_n = _conv(tail[8:], tail[:8], w, b)
        sgn = _sigmoid(pre_n)
        dpre_h = dxch_raw * (sgn * (1.0 + pre_n * (1.0 - sgn)))
        dx, dw, db = _conv_bwd(xbc, xh, dpre, dpre_h, w)
        ddr = ddtb * _sigmoid(dtr + bias)
        return [dx, ddr], [dw, jnp.concatenate([db, jnp.zeros((7, db.shape[1]), F32)], axis=0), _colsum8(ddr)]
    return _rows(fn, [(proj, 1024, C_XBC // 1024), (proj, BLK, C_DT // BLK), dxc, ddt],
                 [_pad8(conv_w), conv_b.reshape(1, -1), dt_bias128],
                 [(1024, MXU), (BLK, MXU)], [(8, 1024), (8, 1024), (8, BLK)], tile=256, name=name,
                 halos=[(0, "prev"), (2, "next"), (0, "next")])


SSD_CPB = 1


def _head_cols(v, h0):
    lane = lax.broadcasted_iota(jnp.int32, (v.shape[0], BLK), 1)
    return jnp.where(lane < HEAD_DIM, v[:, h0:h0 + 1], v[:, h0 + 1:h0 + 2])


def _ssd_scan(xc, dt, par, name):
    s = xc.shape[0]
    nc = s // BLK

    def body(x_ref, dt_ref, par_ref, y_ref, st_ref, h_ref):
        c = pl.program_id(0)

        @pl.when(c == 0)
        def _():
            h_ref[...] = jnp.zeros_like(h_ref)

        st_ref[0] = h_ref[...]
        dt = dt_ref[...]
        a_row = -jnp.exp(par_ref[0:1, :])
        d_row = par_ref[1:2, :]
        ri = lax.broadcasted_iota(jnp.int32, (BLK, BLK), 0)
        ci = lax.broadcasted_iota(jnp.int32, (BLK, BLK), 1)
        tril = ri >= ci
        cs = _nn(tril.astype(F32), dt * a_row, HI)
        cst, dtt = cs.T, dt.T
        last = cs[BLK - 1:BLK, :]
        wcol = jnp.exp(last - cs) * dt
        ecs = jnp.exp(cs)
        elast = jnp.exp(last)
        for g in (0, 1):
            bg = x_ref[:, 512 + g * BLK:512 + (g + 1) * BLK].astype(MXU)
            cg = x_ref[:, 768 + g * BLK:768 + (g + 1) * BLK].astype(MXU)
            gm = _nt(cg, bg)
            for pp in (0, 1):
                pr = 2 * g + pp
                h0 = 2 * pr
                x2 = x_ref[:, pr * BLK:(pr + 1) * BLK]
                hprev = h_ref[pr * BLK:(pr + 1) * BLK, :]
                yp = jnp.zeros((BLK, BLK), F32)
                for hh in (0, 1):
                    h = h0 + hh
                    hmask = (ci < HEAD_DIM) if hh == 0 else (ci >= HEAD_DIM)
                    lm = jnp.exp(jnp.where(tril, cs[:, h:h + 1] - cst[h:h + 1, :], NEG))
                    mm = gm * lm * dtt[h:h + 1, :]
                    yp = yp + _nn(mm.astype(MXU), jnp.where(hmask, x2, 0.0).astype(MXU))
                y0 = _nt(cg, hprev.astype(MXU))
                y_ref[:, pr * BLK:(pr + 1) * BLK] = yp + _head_cols(ecs, h0) * y0 + _head_cols(d_row, h0) * x2
                dec = jnp.where(ri < HEAD_DIM, elast[:, h0:h0 + 1], elast[:, h0 + 1:h0 + 2])
                xw = (x2 * _head_cols(wcol, h0)).astype(MXU)
                h_ref[pr * BLK:(pr + 1) * BLK, :] = dec * hprev + _tn(xw, bg)

    return pl.pallas_call(
        body, name=name, grid=(nc,),
        in_specs=[pl.BlockSpec((BLK, 1024), lambda c: (c, 0)), pl.BlockSpec((BLK, BLK), lambda c: (c, 0)),
                  pl.BlockSpec((8, BLK), lambda c: (0, 0))],
        out_specs=[pl.BlockSpec((BLK, SSD_W), lambda c: (c, 0)), pl.BlockSpec((1, SSD_W, SSD_STATE), lambda c: (c, 0, 0))],
        out_shape=[jax.ShapeDtypeStruct((s, SSD_W), F32), jax.ShapeDtypeStruct((nc, SSD_W, SSD_STATE), F32)],
        scratch_shapes=[pltpu.VMEM((SSD_W, SSD_STATE), F32)],
        compiler_params=_cp("arbitrary"),
    )(xc, dt, par)


def _ssd_scan_bwd(xc, dt, par, st, dy, name, comm=None):
    s = xc.shape[0]
    cpb = SSD_CPB
    nb = s // (cpb * BLK)
    comm = comm or _Comm()

    def chunk(x_ref, dt_ref, par_ref, st_ref, dy_ref, dx_ref, ddt_ref, dal_ref, dd_ref, dh_ref):
        dt = dt_ref[...]
        a_row = -jnp.exp(par_ref[0:1, :])
        d_row = par_ref[1:2, :]
        ri = lax.broadcasted_iota(jnp.int32, (BLK, BLK), 0)
        ci = lax.broadcasted_iota(jnp.int32, (BLK, BLK), 1)
        tril = ri >= ci
        cs = _nn(tril.astype(F32), dt * a_row, HI)
        cst, dtt = cs.T, dt.T
        last = cs[BLK - 1:BLK, :]
        tolast = jnp.exp(last - cs)
        wcol = tolast * dt
        ecs = jnp.exp(cs)
        elast = jnp.exp(last)
        dcs_col = jnp.zeros((BLK, BLK), F32)
        ddt_col = jnp.zeros((BLK, BLK), F32)
        dcs_row = jnp.zeros((BLK, BLK), F32)
        ddt_row = jnp.zeros((BLK, BLK), F32)
        dlast = jnp.zeros((1, BLK), F32)
        ddsk = jnp.zeros((1, BLK), F32)
        for g in (0, 1):
            bg32 = x_ref[:, 512 + g * BLK:512 + (g + 1) * BLK]
            cg32 = x_ref[:, 768 + g * BLK:768 + (g + 1) * BLK]
            bg, cg = bg32.astype(MXU), cg32.astype(MXU)
            gm = _nt(cg, bg)
            dgm = jnp.zeros((BLK, BLK), F32)
            dbg = jnp.zeros((BLK, BLK), F32)
            dcg = jnp.zeros((BLK, BLK), F32)
            for pp in (0, 1):
                pr = 2 * g + pp
                h0 = 2 * pr
                x2 = x_ref[:, pr * BLK:(pr + 1) * BLK]
                dy2 = dy_ref[:, pr * BLK:(pr + 1) * BLK]
                hprev = st_ref[0, pr * BLK:(pr + 1) * BLK, :]
                dhn = dh_ref[pr * BLK:(pr + 1) * BLK, :]
                x2m, dhnm = x2.astype(MXU), dhn.astype(MXU)
                zb = _nt(bg, dhnm)
                y0 = _nt(cg, hprev.astype(MXU))
                esel = _head_cols(ecs, h0)
                wsel = _head_cols(wcol, h0)
                dx2 = _head_cols(d_row, h0) * dy2 + wsel * zb
                pick2 = (((ri < HEAD_DIM) & (ci == h0)) | ((ri >= HEAD_DIM) & (ci == h0 + 1))).astype(F32)
                sums = _nn(jnp.concatenate([dy2 * y0, x2 * zb, dy2 * x2], axis=0), pick2, HIGH)
                de2, dw2, dd2 = sums[:BLK], sums[BLK:2 * BLK], sums[2 * BLK:]
                v2 = dw2 * wcol
                dcs_col = dcs_col + ecs * de2 - v2
                ddt_col = ddt_col + dw2 * tolast
                hsum = _nn(dhn * hprev, jnp.ones((BLK, BLK), F32), HIGH)
                dlast = dlast + elast * jnp.sum(jnp.where(pick2 > 0.0, hsum, 0.0), axis=0, keepdims=True) \
                    + jnp.sum(v2, axis=0, keepdims=True)
                ddsk = ddsk + jnp.sum(dd2, axis=0, keepdims=True)
                ts = []
                for hh in (0, 1):
                    h = h0 + hh
                    hmask = (ci < HEAD_DIM) if hh == 0 else (ci >= HEAD_DIM)
                    ons = (ri == h).astype(F32)
                    dym = jnp.where(hmask, dy2, 0.0).astype(MXU)
                    dt_r = dtt[h:h + 1, :]
                    lm = jnp.exp(jnp.where(tril, cs[:, h:h + 1] - cst[h:h + 1, :], NEG))
                    mm = gm * lm * dt_r
                    dx2 = dx2 + _tn(mm.astype(MXU), dym)
                    dm = _nt(dym, x2m)
                    t1 = dm * lm
                    dgm = dgm + t1 * dt_r
                    tt = t1 * gm
                    ddt_row = ddt_row + ons * jnp.sum(tt, axis=0, keepdims=True)
                    t = tt * dt_r
                    dcs_row = dcs_row - ons * jnp.sum(t, axis=0, keepdims=True)
                    ts.append(t)
                rows2 = lax.broadcasted_iota(jnp.int32, (2 * BLK, BLK), 0)
                lane2 = lax.broadcasted_iota(jnp.int32, (2 * BLK, BLK), 1)
                to_lane = ((rows2 < BLK) & (lane2 == h0)) | ((rows2 >= BLK) & (lane2 == h0 + 1))
                dcs_col = dcs_col + _nn(jnp.concatenate(ts, axis=1), to_lane.astype(F32), HIGH)
                dx_ref[:, pr * BLK:(pr + 1) * BLK] = dx2
                edy = (esel * dy2).astype(MXU)
                dcg = dcg + _nn(edy, hprev.astype(MXU))
                dec = jnp.where(ri < HEAD_DIM, elast[:, h0:h0 + 1], elast[:, h0 + 1:h0 + 2])
                dh_ref[pr * BLK:(pr + 1) * BLK, :] = dec * dhn + _tn(edy, cg)
                dbg = dbg + _nn((x2 * wsel).astype(MXU), dhnm)
            dgmm = dgm.astype(MXU)
            dx_ref[:, 512 + g * BLK:512 + (g + 1) * BLK] = dbg + _tn(dgmm, cg)
            dx_ref[:, 768 + g * BLK:768 + (g + 1) * BLK] = dcg + _nn(dgmm, bg)
        dcs = dcs_col + dcs_row.T + jnp.where(ri == BLK - 1, dlast, 0.0)
        dda = _nn((ri <= ci).astype(F32), dcs, HI)
        ddt_ref[...] = ddt_col + ddt_row.T + a_row * dda
        da = jnp.sum(dt * dda, axis=0, keepdims=True)
        dal_ref[0:1, :] += da * a_row
        dd_ref[0:1, :] += ddsk

    def body(*refs):
        (x_ref, dt_ref, par_ref, st_ref, dy_ref, dx_ref, ddt_ref, dal_ref, dd_ref, dh_ref), cm = comm.split(refs, 5, 4, 1)
        c = pl.program_id(0)
        comm.start_at(c == 0, cm)

        @pl.when(c == 0)
        def _():
            dh_ref[...] = jnp.zeros_like(dh_ref)
            dal_ref[...] = jnp.zeros_like(dal_ref)
            dd_ref[...] = jnp.zeros_like(dd_ref)

        for cc in reversed(range(cpb)):
            rows = pl.ds(cc * BLK, BLK)
            chunk(x_ref.at[rows], dt_ref.at[rows], par_ref, st_ref.at[pl.ds(cc, 1)], dy_ref.at[rows], dx_ref.at[rows],
                  ddt_ref.at[rows], dal_ref, dd_ref, dh_ref)
        comm.wait_at(c == nb - 1, cm)

    rev = lambda c: (nb - 1 - c, 0)
    tb = cpb * BLK
    res = pl.pallas_call(
        body, name=name, grid=(nb,),
        in_specs=[pl.BlockSpec((tb, 1024), rev), pl.BlockSpec((tb, BLK), rev), pl.BlockSpec((8, BLK), lambda c: (0, 0)),
                  pl.BlockSpec((cpb, SSD_W, SSD_STATE), lambda c: (nb - 1 - c, 0, 0)), pl.BlockSpec((tb, SSD_W), rev)]
        + [ANY] * comm.n,
        out_specs=[pl.BlockSpec((tb, 1024), rev), pl.BlockSpec((tb, BLK), rev),
                   pl.BlockSpec((8, BLK), lambda c: (0, 0)), pl.BlockSpec((8, BLK), lambda c: (0, 0))] + [ANY] * comm.n,
        out_shape=[jax.ShapeDtypeStruct((s, 1024), F32), jax.ShapeDtypeStruct((s, BLK), F32),
                   jax.ShapeDtypeStruct((8, BLK), F32), jax.ShapeDtypeStruct((8, BLK), F32)] + comm.out_shape(),
        scratch_shapes=[pltpu.VMEM((SSD_W, SSD_STATE), F32)] + comm.scratch(),
        compiler_params=_cp("arbitrary"),
    )(xc, dt, par, st, dy, *comm.args())
    return res[0], res[1], res[2], res[3], list(res[4:])


def _ssd_gate(y, z, w):
    t = y * _silu(z)
    outs = []
    for g in (0, 1):
        tg = t[:, g * 256:(g + 1) * 256]
        outs.append(tg * lax.rsqrt(jnp.mean(tg * tg, axis=-1, keepdims=True) + SSD_NORM_EPS))
    return jnp.concatenate(outs, axis=1) * w


def _ssd_post(y, proj, norm_w, name):
    def fn(rv, hv, cv):
        return [_ssd_gate(rv[0], rv[1], cv[0])], []
    return _rows(fn, [y, (proj, SSD_W, C_Z // SSD_W)], [norm_w.reshape(1, -1)], [(SSD_W, MXU)], tile=512, name=name)[0]


def _ssd_post_bwd(y, proj, norm_w, dout, name):
    def fn(rv, hv, cv):
        yb, zb, db = rv
        _, vjp = jax.vjp(lambda a, b: _ssd_gate(a, b, cv[0]), yb, zb)
        dy, dz = vjp(db)
        t = yb * _silu(zb)
        nrm = []
        for g in (0, 1):
            tg = t[:, g * 256:(g + 1) * 256]
            nrm.append(tg * lax.rsqrt(jnp.mean(tg * tg, axis=-1, keepdims=True) + SSD_NORM_EPS))
        return [dy, dz], [_colsum8(db * jnp.concatenate(nrm, axis=1))]
    return _rows(fn, [y, (proj, SSD_W, C_Z // SSD_W), dout], [norm_w.reshape(1, -1)],
                 [(SSD_W, F32), (SSD_W, MXU)], [(8, SSD_W)], tile=512, name=name)


LRU_T = 256


def _lru_conv(proj, conv_w, conv_b, name):
    def fn(rv, hv, cv):
        return [_conv(rv[0], hv[0], cv[0], cv[1])], []
    return _rows(fn, [(proj, LRU_W, C_XL // LRU_W)], [_pad8(conv_w), conv_b.reshape(1, -1)], [(LRU_W, F32)],
                 tile=512, name=name, halos=[(0, "prev")])[0]


def _lru_conv_bwd(proj, dxc, conv_w, name):
    def fn(rv, hv, cv):
        dx, dw, db = _conv_bwd(rv[0], hv[0], rv[1], hv[1], cv[0])
        return [dx], [dw, jnp.concatenate([db, jnp.zeros((7, db.shape[1]), F32)], axis=0)]
    return _rows(fn, [(proj, LRU_W, C_XL // LRU_W), dxc], [_pad8(conv_w)], [(LRU_W, MXU)], [(8, LRU_W), (8, LRU_W)],
                 tile=512, name=name, halos=[(0, "prev"), (1, "next")])


def _lru_au(pre_a, pre_x, xc, ba, bx, lam):
    r = _sigmoid(pre_a + ba)
    i = _sigmoid(pre_x + bx)
    log_a = -LRU_C * r * _softplus(-lam)
    a = jnp.exp(log_a)
    u = jnp.sqrt(1.0 - jnp.exp(2.0 * log_a)) * (i * xc)
    return a, u


def _lru_scan(pre, xc, proj, par, name):
    s = xc.shape[0]
    t = LRU_T

    def body(pre_ref, xc_ref, g_ref, par_ref, out_ref, h_ref, carry):
        c = pl.program_id(0)

        @pl.when(c == 0)
        def _():
            carry[...] = jnp.zeros_like(carry)

        a, u = _lru_au(pre_ref[:, :LRU_W], pre_ref[:, LRU_W:], xc_ref[...], par_ref[0:1, :], par_ref[1:2, :], par_ref[2:3, :])
        row = lax.broadcasted_iota(jnp.int32, (t, LRU_W), 0)
        sft = 1
        while sft < t:
            keep = row >= sft
            a_s = jnp.where(keep, pltpu.roll(a, sft, 0), 1.0)
            u_s = jnp.where(keep, pltpu.roll(u, sft, 0), 0.0)
            u = a * u_s + u
            a = a * a_s
            sft *= 2
        h = a * carry[0:1, :] + u
        h_ref[...] = h
        out_ref[...] = (h * _gelu(g_ref[...])).astype(out_ref.dtype)
        carry[0:1, :] = h[t - 1:t, :]

    return pl.pallas_call(
        body, name=name, grid=(s // t,),
        in_specs=[pl.BlockSpec((t, 2 * LRU_W), lambda c: (c, 0)), pl.BlockSpec((t, LRU_W), lambda c: (c, 0)),
                  pl.BlockSpec((t, LRU_W), lambda c: (c, C_G // LRU_W)), pl.BlockSpec((8, LRU_W), lambda c: (0, 0))],
        out_specs=[pl.BlockSpec((t, LRU_W), lambda c: (c, 0))] * 2,
        out_shape=[jax.ShapeDtypeStruct((s, LRU_W), MXU), jax.ShapeDtypeStruct((s, LRU_W), F32)],
        scratch_shapes=[pltpu.VMEM((8, LRU_W), F32)],
        compiler_params=_cp("arbitrary"),
    )(pre, xc, proj, par)


def _lru_scan_bwd(pre, xc, proj, par, h, dout, name):
    s = xc.shape[0]
    t = LRU_T
    n = s // t
    t8 = t // 8

    def body(pre_ref, xc_ref, g_ref, par_ref, h_ref, hh_ref, do_ref, dpre_ref, dxc_ref, dg_ref, dpar_ref, carry):
        c = pl.program_id(0)

        @pl.when(c == 0)
        def _():
            carry[...] = jnp.zeros_like(carry)
            dpar_ref[...] = jnp.zeros_like(dpar_ref)

        pa, px, xcb = pre_ref[:, :LRU_W], pre_ref[:, LRU_W:], xc_ref[...]
        ba, bx, lam = par_ref[0:1, :], par_ref[1:2, :], par_ref[2:3, :]
        (a, u), vjp = jax.vjp(_lru_au, pa, px, xcb, ba, bx, lam)
        g = g_ref[...]
        hcur = h_ref[...]
        do = do_ref[...]
        _, gvjp = jax.vjp(_gelu, g)
        dg_ref[...] = gvjp(do * hcur)[0].astype(dg_ref.dtype)
        row = lax.broadcasted_iota(jnp.int32, (t, LRU_W), 0)
        v = do * _gelu(g) + jnp.where(row == t - 1, carry[0:1, :], 0.0)
        b = jnp.where(row == t - 1, 0.0, pltpu.roll(a, t - 1, 0))
        sft = 1
        while sft < t:
            keep = row < t - sft
            b_s = jnp.where(keep, pltpu.roll(b, t - sft, 0), 1.0)
            v_s = jnp.where(keep, pltpu.roll(v, t - sft, 0), 0.0)
            v = b * v_s + v
            b = b * b_s
            sft *= 2
        dh = v
        carry[0:1, :] = a[0:1, :] * dh[0:1, :]
        hhalo = jnp.where(c == n - 1, 0.0, hh_ref[...])
        hprev = _shift_down(hcur, hhalo, 1)
        dpa, dpx, dxc, dba, dbx, dlam = vjp((dh * hprev, dh))
        dpre_ref[:, :LRU_W] = dpa
        dpre_ref[:, LRU_W:] = dpx
        dxc_ref[...] = dxc
        dpar_ref[0:1, :] += dba
        dpar_ref[1:2, :] += dbx
        dpar_ref[2:3, :] += dlam

    rev = lambda c: (n - 1 - c, 0)
    return pl.pallas_call(
        body, name=name, grid=(n,),
        in_specs=[pl.BlockSpec((t, 2 * LRU_W), rev), pl.BlockSpec((t, LRU_W), rev),
                  pl.BlockSpec((t, LRU_W), lambda c: (n - 1 - c, C_G // LRU_W)), pl.BlockSpec((8, LRU_W), lambda c: (0, 0)),
                  pl.BlockSpec((t, LRU_W), rev),
                  pl.BlockSpec((8, LRU_W), lambda c: (jnp.maximum((n - 1 - c) * t8 - 1, 0), 0)),
                  pl.BlockSpec((t, LRU_W), lambda c: (n - 1 - c, dout.shape[1] // LRU_W - 1))],
        out_specs=[pl.BlockSpec((t, 2 * LRU_W), rev), pl.BlockSpec((t, LRU_W), rev), pl.BlockSpec((t, LRU_W), rev),
                   pl.BlockSpec((8, LRU_W), lambda c: (0, 0))],
        out_shape=[jax.ShapeDtypeStruct((s, 2 * LRU_W), F32), jax.ShapeDtypeStruct((s, LRU_W), F32),
                   jax.ShapeDtypeStruct((s, LRU_W), MXU), jax.ShapeDtypeStruct((8, LRU_W), F32)],
        scratch_shapes=[pltpu.VMEM((8, LRU_W), F32)],
        compiler_params=_cp("arbitrary"),
    )(pre, xc, proj, par, h, h, dout)


def _swiglu_act(gu, name):
    def fn(rv, hv, cv):
        return [_silu(rv[0].astype(F32)) * rv[1].astype(F32)], []
    return _rows(fn, [(gu, D_FF, 0), (gu, D_FF, 1)], [], [(D_FF, MXU)], tile=256, name=name)[0]


def _swiglu_bwd(gu, da, name):
    def fn(rv, hv, cv):
        gt, up, dab = [t.astype(F32) for t in rv]
        sg = _sigmoid(gt)
        dgate = dab * up * (sg * (1.0 + gt * (1.0 - sg)))
        dup = dab * (gt * sg)
        return [jnp.concatenate([dgate, dup], axis=1)], []
    return _rows(fn, [(gu, D_FF, 0), (gu, D_FF, 1), da], [], [(2 * D_FF, MXU)], tile=256, name=name)[0]


def _loss_head(x, g, target, name):
    d = x.shape[1]

    def fn(rv, hv, cv):
        xb, tb = rv
        y, vjp = jax.vjp(_rms, xb, cv[0])
        err = y - tb
        dy = err * (1.0 / d)
        dx, _ = vjp(dy)
        rstd = lax.rsqrt(jnp.mean(xb * xb, axis=-1, keepdims=True) + NORM_EPS)
        e2 = err * err * (0.5 / d)
        e2 = functools.reduce(lambda a, b: a + b, [e2[:, k * BLK:(k + 1) * BLK] for k in range(d // BLK)])
        return [dx], [_colsum8(dy * xb * rstd), _colsum8(e2)]
    return _rows(fn, [x, target], [g.reshape(1, -1)], [(d, F32)], [(8, d), (8, BLK)], tile=512, name=name)


ANY = pl.BlockSpec(memory_space=pl.ANY)


def _coords():
    return lax.axis_index("x"), lax.axis_index("y"), lax.axis_index("c")


class _Comm:
    def __init__(self, gathers=(), scatters=()):
        self.gathers = list(gathers)
        self.scatters = list(scatters)
        self.n = len(self.gathers) + len(self.scatters)

    def args(self):
        return [g[0] for g in self.gathers] + self.scatters

    def out_shape(self):
        out = [jax.ShapeDtypeStruct((4,) + (a.shape if l is None else a.shape[1:]), a.dtype) for a, l, _ in self.gathers]
        return out + [jax.ShapeDtypeStruct((3,) + a.shape[1:], a.dtype) for a in self.scatters]

    def scratch(self):
        if not self.n:
            return []
        return [pltpu.SemaphoreType.DMA((3 * self.n,)), pltpu.SemaphoreType.DMA((3 * self.n,)),
                pltpu.SemaphoreType.DMA((max(len(self.gathers), 1),)),
                pltpu.SemaphoreType.DMA((3 * self.n,)), pltpu.SemaphoreType.DMA((3 * self.n,))]

    def split(self, refs, n_in, n_out, n_scratch):
        refs = list(refs)
        n = self.n
        own = refs[:n_in] + refs[n_in + n:n_in + n + n_out] + refs[n_in + 2 * n + n_out:n_in + 2 * n + n_out + n_scratch]
        cm = (refs[n_in:n_in + n], refs[n_in + n + n_out:n_in + 2 * n + n_out], refs[n_in + 2 * n + n_out + n_scratch:])
        return own, cm

    def _copies(self, cm, arriving):
        ins, outs, (send, recv, local, _, _) = cm
        x, y, c = _coords()
        me = 2 * x + y
        chips = [(1 - x, y), (x, 1 - y), (1 - x, 1 - y)]
        remote, locals_ = [], []
        ng = len(self.gathers)
        for i in range(self.n):
            if i < ng:
                _, l, halved = self.gathers[i]
                slab = ins[i] if l is None else ins[i].at[l]
                if not arriving:
                    locals_.append(pltpu.make_async_copy(slab, outs[i].at[me], local.at[i]))
            for j, (px, py) in enumerate(chips):
                if i < ng:
                    slot = 2 * px + py if arriving else me
                    src, dst = (slab.at[c], outs[i].at[slot, c]) if halved else (slab, outs[i].at[slot])
                else:
                    src, dst = ins[i].at[2 * px + py], outs[i].at[j]
                remote.append(pltpu.make_async_remote_copy(src, dst, send.at[3 * i + j], recv.at[3 * i + j],
                                                           device_id=(px, py, c), device_id_type=MESH))
        return remote, locals_

    def _handovers(self, cm, arriving):
        _, outs, (_, _, _, send, recv) = cm
        x, y, c = _coords()
        chips = [(1 - x, y), (x, 1 - y), (1 - x, 1 - y)]
        cps = []
        for i, (_, _, halved) in enumerate(self.gathers):
            if halved:
                for j, (px, py) in enumerate(chips):
                    src = outs[i].at[2 * px + py, c]
                    dst = outs[i].at[2 * px + py, 1 - c if arriving else c]
                    cps.append(pltpu.make_async_remote_copy(src, dst, send.at[3 * i + j], recv.at[3 * i + j],
                                                            device_id=(x, y, 1 - c), device_id_type=MESH))
        return cps

    def start_at(self, cond, cm):
        def go():
            remote, locals_ = self._copies(cm, False)
            for cp in locals_ + remote:
                cp.start()

        if self.n:
            go() if cond is True else pl.when(cond)(go)

    def wait_at(self, cond, cm):
        def go():
            for cp in self._copies(cm, True)[0]:
                cp.wait_recv()
            handed = self._handovers(cm, False)
            for cp in handed:
                cp.start()
            for cp in self._handovers(cm, True):
                cp.wait_recv()
            remote, locals_ = self._copies(cm, False)
            for cp in handed + remote:
                cp.wait_send()
            for cp in locals_:
                cp.wait()

        if self.n:
            go() if cond is True else pl.when(cond)(go)


def _comm_call(comm, name):
    def body(*refs):
        _, cm = comm.split(refs, 0, 0, 0)
        comm.start_at(True, cm)
        comm.wait_at(True, cm)

    return list(pl.pallas_call(
        body, name=name, in_specs=[ANY] * comm.n, out_specs=[ANY] * comm.n, out_shape=comm.out_shape(),
        scratch_shapes=comm.scratch(), compiler_params=pltpu.CompilerParams(has_side_effects=True),
    )(*comm.args()))


def _swap_sibling(arrs):
    n = len(arrs)

    def body(*refs):
        ins, outs, send, recv = refs[:n], refs[n:2 * n], refs[2 * n], refs[2 * n + 1]
        x, y, c = _coords()
        cps = [pltpu.make_async_remote_copy(ins[i], outs[i], send.at[i], recv.at[i], device_id=(x, y, 1 - c), device_id_type=MESH)
               for i in range(n)]
        for cp in cps:
            cp.start()
        for cp in cps:
            cp.wait_recv()
        for cp in cps:
            cp.wait_send()

    return list(pl.pallas_call(
        body, name="swap_sibling", in_specs=[ANY] * n, out_specs=[ANY] * n,
        out_shape=[jax.ShapeDtypeStruct(a.shape, a.dtype) for a in arrs],
        scratch_shapes=[pltpu.SemaphoreType.DMA((n,)), pltpu.SemaphoreType.DMA((n,))],
        compiler_params=pltpu.CompilerParams(has_side_effects=True),
    )(*arrs))


def _gather_small(gs):
    def body(g_ref, o_ref, send_sems, recv_sems, local_sem):
        x, y, c = _coords()
        me = 4 * x + 2 * y + c
        mine = pltpu.make_async_copy(g_ref, o_ref.at[me], local_sem)
        mine.start()
        sends = []
        for k in range(1, 8):
            px, py, pc = x ^ (k >> 2), y ^ ((k >> 1) & 1), c ^ (k & 1)
            sends.append((pltpu.make_async_remote_copy(g_ref, o_ref.at[me], send_sems.at[k - 1], recv_sems.at[k - 1],
                                                       device_id=(px, py, pc), device_id_type=MESH), 4 * px + 2 * py + pc, k))
        for cp, _, _ in sends:
            cp.start()
        for cp, src, k in sends:
            pltpu.make_async_remote_copy(g_ref, o_ref.at[src], send_sems.at[k - 1], recv_sems.at[k - 1],
                                         device_id=(x, y, c), device_id_type=MESH).wait_recv()
        for cp, _, _ in sends:
            cp.wait_send()
        mine.wait()

    return pl.pallas_call(
        body, name="gather_small", in_specs=[ANY], out_specs=ANY,
        out_shape=jax.ShapeDtypeStruct((8,) + gs.shape, gs.dtype),
        scratch_shapes=[pltpu.SemaphoreType.DMA((7,)), pltpu.SemaphoreType.DMA((7,)), pltpu.SemaphoreType.DMA],
        compiler_params=pltpu.CompilerParams(has_side_effects=True),
    )(gs)


def _sum_slots(own, others, name, tile):
    k, r, c = others.shape

    def body(*refs):
        if own is None:
            o_ref, out_ref = refs
            acc = o_ref[0].astype(F32)
            first = 1
        else:
            own_ref, o_ref, out_ref = refs
            acc = own_ref[...]
            first = 0
        for j in range(first, k):
            acc = acc + o_ref[j].astype(F32)
        out_ref[...] = acc

    row = pl.BlockSpec((tile, c), lambda i: (i, 0))
    specs = ([] if own is None else [row]) + [pl.BlockSpec((k, tile, c), lambda i: (0, i, 0))]
    args = ([] if own is None else [own]) + [others]
    return pl.pallas_call(body, name=name, grid=(r // tile,), in_specs=specs, out_specs=row,
                          out_shape=jax.ShapeDtypeStruct((r, c), F32), compiler_params=_cp("parallel"))(*args)


def _adamw(w, m, v, ga, gb, name, tile, rows_first=False):
    lead = 0 if rows_first else w.ndim - 2
    r, c = w.shape[-2:]

    def body(*refs):
        vals = [ref[0] if lead else ref[...] for ref in refs[:len(refs) - 4]]
        w_, m_, v_, g = vals[0], vals[1], vals[2], vals[3]
        if gb is not None:
            g = g + vals[4]
        nm = ADAM_B1 * m_ + (1.0 - ADAM_B1) * g
        nv = ADAM_B2 * v_ + (1.0 - ADAM_B2) * (g * g)
        d = -ADAM_LR * ((nm / BC1) / (jnp.sqrt(nv / BC2) + ADAM_EPS) + ADAM_WD * w_)
        for ref, val in zip(refs[len(refs) - 4:], (g, d, nm, nv)):
            if lead:
                ref[0] = val
            else:
                ref[...] = val

    if rows_first:
        row = pl.BlockSpec((tile,) + w.shape[1:], lambda i: (i, 0, 0))
        grid = (w.shape[0] // tile,)
    elif lead:
        row = pl.BlockSpec((1, tile, c), lambda l, i: (l, i, 0))
        grid = (w.shape[0], r // tile)
    else:
        row = pl.BlockSpec((tile, c), lambda i: (i, 0))
        grid = (r // tile,)
    args = [w, m, v, ga] + ([] if gb is None else [gb])
    return pl.pallas_call(body, name=name, grid=grid, in_specs=[row] * len(args), out_specs=[row] * 4,
                          out_shape=[jax.ShapeDtypeStruct(w.shape, F32)] * 4,
                          compiler_params=_cp(*(["parallel"] * len(grid))))(*args)


MATS = ("w_in", "w_out", "w_gate", "w_up", "w_down")
CONVS = ("ssd_conv_w", "lru_conv_w")
BIG = MATS + CONVS
TRANSPOSED = ("w_gate", "w_up")
COL_SHARDED = ("ssd_conv_w", "lru_conv_w")
W_IN_SHARD = IN_COLS // 4
W_IN_PAD = 1056
SMALL = ("norm_mix", "ssd_conv_b", "ssd_dt_bias", "ssd_a_log", "ssd_d", "ssd_norm", "lru_conv_b", "lru_wa", "lru_ba",
         "lru_wx", "lru_bx", "lru_lambda", "norm_ffn", "norm_final")
WEIGHTS = ("norm_mix", "w_in", "ssd_conv_w", "ssd_conv_b", "ssd_dt_bias", "ssd_a_log", "ssd_d", "ssd_norm", "lru_conv_w",
           "lru_conv_b", "lru_wa", "lru_ba", "lru_wx", "lru_bx", "lru_lambda", "w_out", "norm_ffn", "w_gate", "w_up",
           "w_down", "norm_final")
ROW_TILE = {"w_in": W_IN_SHARD, "w_out": 128, "w_gate": 352, "w_up": 352, "w_down": 352}
W_IN_ADAM_TILE = 54


def _pack(arrs, width, row_mult, dtype):
    flat = jnp.concatenate([a.reshape(-1).astype(dtype) for a in arrs])
    rows = -(-flat.shape[0] // width)
    rows = -(-rows // row_mult) * row_mult
    flat = jnp.pad(flat, (0, rows * width - flat.shape[0]))
    return flat.reshape(rows, width)


def _unpack(buf, shapes):
    flat = buf.reshape(-1)
    out, off = [], 0
    for shp in shapes:
        n = int(np.prod(shp))
        out.append(flat[off:off + n].reshape(shp))
        off += n
    return out


def _join(name, g4):
    if name in COL_SHARDED:
        return jnp.moveaxis(g4, 0, -2).reshape(g4.shape[1:-1] + (4 * g4.shape[-1],))
    return g4.reshape((4 * g4.shape[1],) + g4.shape[2:])


def _slabs(name, g):
    if name in COL_SHARDED:
        return jnp.moveaxis(g.reshape(g.shape[:-1] + (4, g.shape[-1] // 4)), -2, 0)
    return g.reshape((4, g.shape[0] // 4) + g.shape[1:])


def _w_in_rows(g4):
    def nat(lo, hi):
        out = []
        while lo < hi:
            j = lo // W_IN_SHARD
            stop = min(hi, (j + 1) * W_IN_SHARD)
            out.append((j, lo - j * W_IN_SHARD, stop - lo))
            lo = stop
        return out
    pieces = nat(0, 3072) + nat(3080, IN_COLS) + nat(3072, 3080)

    def body(g_ref, o_ref):
        row = 0
        for j, first, n in pieces:
            o_ref[row:row + n, :] = g_ref[j, first:first + n, :]
            row += n
        o_ref[row:, :] = jnp.zeros((NP - row, o_ref.shape[1]), o_ref.dtype)

    return pl.pallas_call(body, name="w_in_rows", out_shape=jax.ShapeDtypeStruct((NP, g4.shape[-1]), g4.dtype),
                          compiler_params=pltpu.CompilerParams(vmem_limit_bytes=VMEM_LIMIT))(g4)


def _w_in_slabs(gt):
    def kern(n):
        return n if n < 3072 else (C_DT + n - 3072 if n < 3080 else n - 8)
    slabs = []
    for j in range(4):
        lo, hi = j * W_IN_SHARD, (j + 1) * W_IN_SHARD
        cuts = sorted({lo, hi} | {c for c in (3072, 3080) if lo < c < hi})
        slabs.append(jnp.concatenate([gt[kern(a):kern(a) + b - a] for a, b in zip(cuts[:-1], cuts[1:])], axis=0))
    return jnp.stack(slabs, axis=0)


def _block_diag(w):
    eye = jnp.eye(LRU_BLOCKS, dtype=w.dtype)
    return jnp.einsum("ncd,nm->ncmd", w, eye).reshape(LRU_W, LRU_W)


def _block_diag_extract(g):
    g4 = g.reshape(LRU_BLOCKS, 64, LRU_BLOCKS, 64)
    return jnp.stack([g4[n, :, n, :] for n in range(LRU_BLOCKS)], axis=0)


def _lanes128(v):
    return jnp.pad(v, (0, BLK - v.shape[0])).reshape(1, BLK)


def _layer_mixers(x, p, comm=None, h=None):
    if h is None:
        h = _rms_fwd(x, p["norm_mix"], "rms_mix")
    proj = _mm(h, p["w_in_t"], tb=True, tm=1024, tn=1408, tk=1024, name="mm_in")
    att, lse, attb, got = _att_fwd_fused(proj, "att_fwd", comm)
    xconv, dt = _ssd_pre(proj, p["ssd_conv_w"], p["ssd_conv_b"], _lanes128(p["ssd_dt_bias"]), "ssd_pre")
    spar = jnp.concatenate([_lanes128(p["ssd_a_log"]), _lanes128(p["ssd_d"]), jnp.zeros((6, BLK), F32)], axis=0)
    y, states = _ssd_scan(xconv, dt, spar, "ssd_scan")
    ssd = _ssd_post(y, proj, p["ssd_norm"], "ssd_post")
    xc = _lru_conv(proj, p["lru_conv_w"], p["lru_conv_b"], "lru_conv")
    wab = jnp.concatenate([_block_diag(p["lru_wa"]), _block_diag(p["lru_wx"])], axis=1).astype(MXU)
    pre = _mm(xc, wab, tm=1024, tn=1024, tk=512, name="mm_lru")
    lpar = jnp.concatenate([p["lru_ba"].reshape(1, -1), p["lru_bx"].reshape(1, -1), p["lru_lambda"].reshape(1, -1),
                            jnp.zeros((5, LRU_W), F32)], axis=0)
    lru, hs = _lru_scan(pre, xc, proj, lpar, "lru_scan")
    mix = jnp.concatenate([attb, ssd, lru], axis=1)
    saved = dict(x=x, h=h, proj=proj, att=att, lse=lse, xconv=xconv, dt=dt, spar=spar, y=y, states=states, xc=xc, wab=wab,
                 pre=pre, lpar=lpar, hs=hs, mix=mix)
    return mix, saved, got


def _layer_ffn(x, mix, p, saved, comm=None, next_norm=None):
    x1, h2 = _mm(mix, p["w_out"], add=x, tm=1024, tn=1024, tk=1536, name="mm_out", epi=_epi_rms(p["norm_ffn"]))
    gu = _mm(h2, p["w_gu_t"], tb=True, out_dtype=MXU, tm=1024, tn=1408, tk=1024, name="mm_gu", comm=comm)
    gu, got = gu if comm is not None else (gu, [])
    act = _swiglu_act(gu, "swiglu_act")
    x2 = _mm(act, p["w_down"], add=x1, tm=1024, tn=1024, tk=2816, name="mm_down",
             epi=None if next_norm is None else _epi_rms(next_norm))
    x2, h_next = x2 if next_norm is not None else (x2, None)
    saved.update(x1=x1, h2=h2, gu=gu, act=act)
    return x2, got, h_next


def _layer_bwd(dx2, p, sv, comm_ssd=None, comm_att=None, comm_tail=None):
    g = {}
    da = _mm(dx2, p["w_down"], tb=True, out_dtype=MXU, tm=1024, tn=1408, tk=1024, name="mm_d_act")
    g["w_down"], g["w_down@wire"] = _mm(sv["act"], dx2, ta=True, tm=1408, tn=1024, tk=1024, name="mm_g_down", epi=_epi_wire(D_MODEL))
    dgu = _swiglu_bwd(sv["gu"], da, "swiglu_bwd")
    dx1, gn = _mm(dgu, p["w_gu_t"], tm=1024, tn=1024, tk=1408, name="mm_d_h2", epi=_epi_rms_bwd(sv["x1"], p["norm_ffn"], dx2))
    g["w_gu_t"], g["w_gu_t@wire"] = _mm(dgu, sv["h2"], ta=True, tm=1408, tn=1024, tk=1024, name="mm_g_gu", epi=_epi_wire(D_MODEL))
    g["norm_ffn"] = jnp.sum(gn, axis=0)
    dmix, stats = _mm(dx1, p["w_out"], tb=True, tm=1024, tn=1536, tk=1024, name="mm_d_mix", epi=_epi_att_stats(sv["att"], sv["lse"]))
    g["w_out"], g["w_out@wire"] = _mm(sv["mix"], dx1, ta=True, tm=1536, tn=1024, tk=1024, name="mm_g_out", epi=_epi_wire(D_MODEL))
    proj = sv["proj"]
    dpre, dxc_u, dgl, dlpar = _lru_scan_bwd(sv["pre"], sv["xc"], proj, sv["lpar"], sv["hs"], dmix, "lru_scan_bwd")
    dxc = _mm(dpre, sv["wab"], tb=True, add=dxc_u, tm=1024, tn=512, tk=1024, name="mm_d_xc")
    gwab = _mm(sv["xc"], dpre, ta=True, tm=512, tn=1024, tk=1024, name="mm_g_lru")
    g["lru_wa"], g["lru_wx"] = _block_diag_extract(gwab[:, :LRU_W]), _block_diag_extract(gwab[:, LRU_W:])
    g["lru_ba"], g["lru_bx"], g["lru_lambda"] = dlpar[0], dlpar[1], dlpar[2]
    dxl, gcw, gcb = _lru_conv_bwd(proj, dxc, p["lru_conv_w"], "lru_conv_bwd")
    g["lru_conv_w"], g["lru_conv_b"] = gcw[:CONV_K], jnp.sum(gcb, axis=0)
    dy, dz, gsn = _ssd_post_bwd(sv["y"], proj, p["ssd_norm"], (dmix, SSD_W, 1), "ssd_post_bwd")
    g["ssd_norm"] = jnp.sum(gsn, axis=0)
    dxconv, ddt, dal, ddk, got_ssd = _ssd_scan_bwd(sv["xconv"], sv["dt"], sv["spar"], sv["states"], dy, "ssd_scan_bwd", comm_ssd)
    g["ssd_a_log"], g["ssd_d"] = dal[0, :8], ddk[0, :8]
    dxbc, ddtr, gsw, gsb, gdb = _ssd_pre_bwd(proj, dxconv, ddt, p["ssd_conv_w"], p["ssd_conv_b"],
                                             _lanes128(p["ssd_dt_bias"]), "ssd_pre_bwd")
    g["ssd_conv_w"], g["ssd_conv_b"], g["ssd_dt_bias"] = gsw[:CONV_K], jnp.sum(gsb, axis=0), jnp.sum(gdb, axis=0)[:8]
    dq, dk, dv, got_att = _att_bwd_rev(proj, dmix, stats, "att_bwd", None if comm_att is None else comm_att(g))
    dproj = jnp.concatenate([dq, dk, dv, dz, dxbc, dgl, dxl, ddtr], axis=1)
    g["w_in_t"], g["w_in_t@wire"] = _mm(dproj, sv["h"], ta=True, tm=1408, tn=1024, tk=1024, name="mm_g_in", epi=_epi_wire(D_MODEL))
    res = _mm(dproj, p["w_in_t"], tm=1024, tn=1024, tk=1408, name="mm_d_h", comm=None if comm_tail is None else comm_tail(g),
              epi=_epi_rms_bwd(sv["x"], p["norm_mix"], dx1))
    (dx, gm), got_tail = res if comm_tail is not None else (res, [])
    g["norm_mix"] = jnp.sum(gm, axis=0)
    return dx, g, got_ssd, got_att, got_tail


def _grad_slabs(g, names, suffix=""):
    out = {}
    for n in names:
        if n == "w_in":
            out[n] = _w_in_slabs(g["w_in_t" + suffix])
        elif n == "w_gate":
            out[n] = _slabs(n, g["w_gu_t" + suffix][:D_FF])
        elif n == "w_up":
            out[n] = _slabs(n, g["w_gu_t" + suffix][D_FF:])
        else:
            out[n] = _slabs(n, g[n + suffix])
    return out


def kernel(x, norm_mix, w_in, ssd_conv_w, ssd_conv_b, ssd_dt_bias, ssd_a_log, ssd_d, ssd_norm, lru_conv_w, lru_conv_b, lru_wa, lru_ba, lru_wx, lru_bx, lru_lambda, w_out, norm_ffn, w_gate, w_up, w_down, norm_final, loss_target, m_norm_mix, m_w_in, m_ssd_conv_w, m_ssd_conv_b, m_ssd_dt_bias, m_ssd_a_log, m_ssd_d, m_ssd_norm, m_lru_conv_w, m_lru_conv_b, m_lru_wa, m_lru_ba, m_lru_wx, m_lru_bx, m_lru_lambda, m_w_out, m_norm_ffn, m_w_gate, m_w_up, m_w_down, m_norm_final, v_norm_mix, v_w_in, v_ssd_conv_w, v_ssd_conv_b, v_ssd_dt_bias, v_ssd_a_log, v_ssd_d, v_ssd_norm, v_lru_conv_w, v_lru_conv_b, v_lru_wa, v_lru_ba, v_lru_wx, v_lru_bx, v_lru_lambda, v_w_out, v_norm_ffn, v_w_gate, v_w_up, v_w_down, v_norm_final):
    loc = dict(locals())
    w = {n: loc[n] for n in WEIGHTS}
    m = {n: loc["m_" + n] for n in WEIGHTS}
    v = {n: loc["v_" + n] for n in WEIGHTS}
    for n in TRANSPOSED:
        w[n], m[n], v[n] = [jnp.transpose(t, (0, 2, 1)) for t in (w[n], m[n], v[n])]
    wt_in, mt_in, vt_in = [jnp.transpose(t, (2, 0, 1)) for t in (w["w_in"], m["w_in"], v["w_in"])]

    def halves(a):
        return a.reshape(a.shape[0], 2, a.shape[1] // 2, a.shape[2])

    def unhalve(a):
        return a.reshape(4, 2 * a.shape[2], a.shape[3])

    def joined(name, a):
        return _w_in_rows(unhalve(a)) if name == "w_in" else _join(name, unhalve(a))

    wb = {n: halves(w[n].astype(MXU)) for n in MATS[1:]}
    wb["w_in"] = halves(jnp.pad(jnp.transpose(wt_in.astype(MXU), (1, 0, 2)), ((0, 0), (0, W_IN_PAD - W_IN_SHARD), (0, 0))))
    xs = x[0]
    h0, first = _rms_fwd(xs, norm_mix[0], "rms_mix", _Comm(gathers=[(wb["w_in"], 0, True), (w["ssd_conv_w"], None, False),
                                                                    (w["lru_conv_w"], None, False)]))
    convs = {"ssd_conv_w": _join("ssd_conv_w", first[1]), "lru_conv_w": _join("lru_conv_w", first[2])}
    behind_att = [(n, 0) for n in MATS[1:]] + [("w_in", 1)]
    behind_ffn = [(n, 1) for n in MATS[1:]]
    whole = {("w_in", 0): joined("w_in", first[0])}
    params = {}

    def layer_params(l):
        if l not in params:
            p = {n: w[n][l] for n in SMALL if n != "norm_final"}
            p.update(w_in_t=whole["w_in", l], ssd_conv_w=convs["ssd_conv_w"][l], lru_conv_w=convs["lru_conv_w"][l])
            params[l] = p
        if "w_out" not in params[l] and ("w_out", l) in whole:
            params[l].update(w_out=whole["w_out", l], w_down=whole["w_down", l],
                             w_gu_t=jnp.concatenate([whole["w_gate", l], whole["w_up", l]], axis=0))
        return params[l]

    saved = []
    h_in = h0
    for l in range(DEPTH):
        first_layer = l == 0
        mix, sv, got = _layer_mixers(xs, layer_params(l), _Comm(gathers=[(wb[n], k, True) for n, k in behind_att]) if first_layer else None,
                                     h_in)
        whole.update({k: joined(k[0], a) for k, a in zip(behind_att, got)})
        xs, got, h_in = _layer_ffn(xs, mix, layer_params(l), sv, _Comm(gathers=[(wb[n], k, True) for n, k in behind_ffn]) if first_layer else None,
                                   norm_mix[l + 1] if l + 1 < DEPTH else None)
        whole.update({k: joined(k[0], a) for k, a in zip(behind_ffn, got)})
        saved.append(sv)
    dx, gnf, lsum = _loss_head(xs, norm_final, loss_target[0], "loss_head")
    loss = lax.psum(jnp.sum(lsum), ("x", "y", "c"))

    dx, g1, _, _, _ = _layer_bwd(dx, layer_params(1), saved[1])
    def slabs_of(g, names):
        own = _grad_slabs(g, names)
        sent = _grad_slabs(g, [n for n in names if n in MATS], "@wire")
        sent.update({n: own[n] for n in names if n not in MATS})
        return own, sent

    s1, sent1 = slabs_of(g1, BIG)
    att0 = ("w_gate", "w_up", "w_down", "w_out")
    s0, sent0 = {}, {}

    def add0(g0, names):
        own, sent = slabs_of(g0, names)
        s0.update(own)
        sent0.update(sent)

    ssd1 = ("w_gate", "w_up")
    att1 = tuple(n for n in BIG if n not in ssd1)

    def comm_att(g0):
        add0(g0, att0)
        return _Comm(scatters=[sent1[n] for n in att1] + [sent0[n] for n in att0])

    tail0 = ("w_in",) + CONVS

    def comm_tail(g0):
        add0(g0, tail0)
        return _Comm(scatters=[sent0[n] for n in tail0])

    dx, g0, got_ssd, got_att, got_tail = _layer_bwd(dx, layer_params(0), saved[0], _Comm(scatters=[sent1[n] for n in ssd1]),
                                                    comm_att, comm_tail)
    recv = {(n, 1): a for n, a in zip(ssd1, got_ssd)}
    recv.update({(n, 1): a for n, a in zip(att1, got_att[:len(att1)])})
    recv.update({(n, 0): a for n, a in zip(att0, got_att[len(att1):])})
    recv.update({(n, 0): a for n, a in zip(tail0, got_tail)})

    me = 2 * lax.axis_index("x") + lax.axis_index("y")
    slabs = (s0, s1)
    part = {}
    for n in BIG:
        per_layer = []
        for l in range(DEPTH):
            own = lax.dynamic_index_in_dim(slabs[l][n], me, axis=0, keepdims=False)
            per_layer.append(_sum_slots(own, recv[n, l], "sum_chips_" + n, ROW_TILE.get(n, own.shape[0])))
        part[n] = jnp.stack(per_layer, axis=0)
    sib = dict(zip(BIG, _swap_sibling([part[n] for n in BIG])))
    out_g, out_d, out_m, out_v = {}, {}, {}, {}
    for n in BIG:
        if n == "w_in":
            res = _adamw(wt_in, mt_in, vt_in, jnp.transpose(part[n], (1, 0, 2)), jnp.transpose(sib[n], (1, 0, 2)), "adamw_" + n,
                         W_IN_ADAM_TILE, rows_first=True)
            out_g[n], out_d[n], out_m[n], out_v[n] = [jnp.transpose(t, (1, 2, 0)) for t in res]
            continue
        res = _adamw(w[n], m[n], v[n], part[n], sib[n], "adamw_" + n, ROW_TILE.get(n, w[n].shape[1]))
        out_g[n], out_d[n], out_m[n], out_v[n] = [jnp.transpose(t, (0, 2, 1)) for t in res] if n in TRANSPOSED else res

    gsm = {n: jnp.stack([g0[n], g1[n]], axis=0) for n in SMALL if n != "norm_final"}
    gsm["norm_final"] = jnp.sum(gnf, axis=0)
    small_shapes = [w[n].shape for n in SMALL]
    gs = _pack([gsm[n].reshape(w[n].shape) for n in SMALL], BLK, 8, F32)
    gall = _gather_small(gs)
    gsum = _sum_slots(None, gall, "sum_devices", gs.shape[0])
    ws = _pack([w[n] for n in SMALL], BLK, 8, F32)
    ms = _pack([m[n] for n in SMALL], BLK, 8, F32)
    vs = _pack([v[n] for n in SMALL], BLK, 8, F32)
    gsr, dsr, nms, nvs = _adamw(ws, ms, vs, gsum, None, "adamw_small", gs.shape[0])
    out_g.update(zip(SMALL, _unpack(gsr, small_shapes)))
    out_d.update(zip(SMALL, _unpack(dsr, small_shapes)))
    out_m.update(zip(SMALL, _unpack(nms, small_shapes)))
    out_v.update(zip(SMALL, _unpack(nvs, small_shapes)))

    return (loss, dx[None], *[out_g[n] for n in WEIGHTS], *[out_d[n] for n in WEIGHTS],
            *[out_m[n] for n in WEIGHTS], *[out_v[n] for n in WEIGHTS])
```

```python
import functools
import math

import jax
import jax.numpy as jnp
import numpy as np
from jax import lax
from jax.experimental import pallas as pl
from jax.experimental.pallas import tpu as pltpu

F32 = jnp.float32
MXU = jnp.bfloat16
HI = lax.Precision.HIGHEST
HIGH = lax.Precision.HIGH
MESH = pl.DeviceIdType.MESH

D_MODEL = 1024
DEPTH = 2
HEAD_DIM = 64
ATT_W = 512
ATT_PATTERNS = ((128, 1), (512, 4), (2048, 16))
BLK = 128
SSD_W = 512
SSD_STATE = 128
LRU_W = 512
LRU_BLOCKS = 8
LRU_C = 8.0
CONV_K = 4
D_MIX = 1536
D_FF = 2816
IN_COLS = 4104
NP = 4224
NORM_EPS = 1e-6
SSD_NORM_EPS = 1e-5
LN2 = math.log(2.0)
NEG = -1e30

ADAM_LR, ADAM_B1, ADAM_B2, ADAM_EPS, ADAM_WD, ADAM_STEP = 0.001, 0.9, 0.999, 1e-08, 0.01, 10
BC1 = 1.0 - ADAM_B1 ** ADAM_STEP
BC2 = 1.0 - ADAM_B2 ** ADAM_STEP

VMEM_LIMIT = 56 * 1024 * 1024

C_Q, C_K, C_V, C_Z, C_XBC, C_G, C_XL, C_DT = 0, 512, 1024, 1536, 2048, 3072, 3584, 4096


def _cp(*sem):
    return pltpu.CompilerParams(dimension_semantics=sem, vmem_limit_bytes=VMEM_LIMIT)


def _dot(a, b, dims, prec=None):
    return lax.dot_general(a, b, (dims, ((), ())), preferred_element_type=F32, precision=prec)


def _nn(a, b, prec=None):
    return _dot(a, b, ((1,), (0,)), prec)


def _nt(a, b, prec=None):
    return _dot(a, b, ((1,), (1,)), prec)


def _tn(a, b, prec=None):
    return _dot(a, b, ((0,), (0,)), prec)


def _sigmoid(x):
    return jax.nn.sigmoid(x)


def _silu(x):
    return x * _sigmoid(x)


def _softplus(x):
    return jnp.maximum(x, 0.0) + jnp.log(1.0 + jnp.exp(-jnp.abs(x)))


def _gelu(x):
    return 0.5 * x * (1.0 + jnp.tanh(0.7978845608028654 * (x + 0.044715 * x * x * x)))


def _mm(a, b, *, ta=False, tb=False, add=None, out_dtype=F32, tm, tn, tk, name, comm=None, epi=None):
    m, k = (a.shape[1], a.shape[0]) if ta else a.shape
    n = b.shape[0] if tb else b.shape[1]
    assert (b.shape[1] if tb else b.shape[0]) == k
    assert m % tm == 0 and n % tn == 0 and k % tk == 0, (name, m, n, k)
    nk = k // tk
    a_spec = pl.BlockSpec((tk, tm), lambda i, j, kk: (kk, i)) if ta else pl.BlockSpec((tm, tk), lambda i, j, kk: (i, kk))
    b_spec = pl.BlockSpec((tn, tk), lambda i, j, kk: (j, kk)) if tb else pl.BlockSpec((tk, tn), lambda i, j, kk: (kk, j))
    o_spec = pl.BlockSpec((tm, tn), lambda i, j, kk: (i, j))
    dims = ((0 if ta else 1,), (1 if tb else 0,))
    carried = comm is not None
    comm = comm or _Comm()
    ni, nj = m // tm, n // tn
    efn, erows, econsts, eouts, eaccs = epi or (None, [], [], [], [])
    assert epi is None or nj == 1
    nadd = 0 if add is None else 1
    ner, nec, neo, nea = len(erows), len(econsts), len(eouts), len(eaccs)

    def body(*refs):
        refs, cm = comm.split(refs, 2 + nadd + ner + nec, 1 + neo + nea, 1)
        a_ref, b_ref = refs[:2]
        er_refs = refs[2 + nadd:2 + nadd + ner]
        ec_refs = refs[2 + nadd + ner:2 + nadd + ner + nec]
        o_ref = refs[2 + nadd + ner + nec]
        eo_refs = refs[3 + nadd + ner + nec:3 + nadd + ner + nec + neo]
        ea_refs = refs[3 + nadd + ner + nec + neo:3 + nadd + ner + nec + neo + nea]
        acc = refs[-1]
        i, j, kk = pl.program_id(0), pl.program_id(1), pl.program_id(2)
        comm.start_at((i == 0) & (j == 0) & (kk == 0), cm)

        @pl.when(kk == 0)
        def _():
            acc[...] = jnp.zeros_like(acc)

        acc[...] += _dot(a_ref[...].astype(MXU), b_ref[...].astype(MXU), dims)

        @pl.when(kk == nk - 1)
        def _():
            r = acc[...]
            if add is not None:
                r = r + refs[2][...]
            if efn is None:
                o_ref[...] = r.astype(out_dtype)
            else:
                main, extra, sums = efn(r, [t[...] for t in er_refs], [t[...] for t in ec_refs])
                o_ref[...] = main.astype(out_dtype)
                for t, val in zip(eo_refs, extra):
                    t[...] = val.astype(t.dtype)
                @pl.when(i == 0)
                def _():
                    for t, val in zip(ea_refs, sums):
                        t[...] = val

                @pl.when(i > 0)
                def _():
                    for t, val in zip(ea_refs, sums):
                        t[...] += val

        comm.wait_at((i == ni - 1) & (j == nj - 1) & (kk == nk - 1), cm)

    def whole_rows(width):
        return pl.BlockSpec((tm, width), lambda i, j, kk: (i, 0))

    ins = [a, b] + ([] if add is None else [add]) + list(erows) + list(econsts)
    specs = [a_spec, b_spec] + ([] if add is None else [o_spec]) + [whole_rows(t.shape[1]) for t in erows]
    specs += [pl.BlockSpec(t.shape, lambda i, j, kk: (0, 0)) for t in econsts]
    out_specs = [o_spec] + [whole_rows(wd) for wd, _ in eouts] + [pl.BlockSpec((r, wd), lambda i, j, kk: (0, 0)) for r, wd in eaccs]
    out_shape = [jax.ShapeDtypeStruct((m, n), out_dtype)] + [jax.ShapeDtypeStruct((m, wd), dt) for wd, dt in eouts]
    out_shape += [jax.ShapeDtypeStruct((r, wd), F32) for r, wd in eaccs]
    serial = comm.n or nea
    res = pl.pallas_call(
        body, name=name, grid=(ni, nj, nk), in_specs=specs + [ANY] * comm.n, out_specs=out_specs + [ANY] * comm.n,
        out_shape=out_shape + comm.out_shape(),
        scratch_shapes=[pltpu.VMEM((tm, tn), F32)] + comm.scratch(),
        compiler_params=_cp(*((["arbitrary"] * 3) if serial else ["parallel", "parallel", "arbitrary"])),
    )(*ins, *comm.args())
    nown = 1 + neo + nea
    own = res[0] if epi is None else list(res[:nown])
    return (own, list(res[nown:])) if carried else own


def _rows(fn, rows, consts=(), outs=(), accs=(), *, tile, name, halos=(), comm=None):
    rows = [r if isinstance(r, tuple) else (r, r.shape[1], 0) for r in rows]
    s = rows[0][0].shape[0]
    assert s % tile == 0 and tile % 8 == 0
    n = s // tile
    t8 = tile // 8
    nr, nh, nc_, no, na = len(rows), len(halos), len(consts), len(outs), len(accs)
    carried = comm is not None
    comm = comm or _Comm()

    def body(*refs):
        refs, cm = comm.split(refs, nr + nh + nc_, no + na, 0)
        i = pl.program_id(0)
        comm.start_at(i == 0, cm)
        rv = [r[...] for r in refs[:nr]]
        hv = []
        for (idx, kind), r in zip(halos, refs[nr:nr + nh]):
            edge = (i == 0) if kind == "prev" else (i == n - 1)
            hv.append(jnp.where(edge, 0.0, r[...]))
        cv = [r[...] for r in refs[nr + nh:nr + nh + nc_]]
        o_refs = refs[nr + nh + nc_:nr + nh + nc_ + no]
        a_refs = refs[nr + nh + nc_ + no:]
        ov, av = fn(rv, hv, cv)
        for r, v in zip(o_refs, ov):
            r[...] = v.astype(r.dtype)
        if na:
            @pl.when(i == 0)
            def _():
                for r in a_refs:
                    r[...] = jnp.zeros_like(r)
            for r, v in zip(a_refs, av):
                r[...] += v
        comm.wait_at(i == n - 1, cm)

    in_specs = [pl.BlockSpec((tile, w), functools.partial(lambda i, cb: (i, cb), cb=cb)) for (_, w, cb) in rows]
    for idx, kind in halos:
        _, w, cb = rows[idx]
        if kind == "prev":
            in_specs.append(pl.BlockSpec((8, w), functools.partial(lambda i, cb: (jnp.maximum(i * t8 - 1, 0), cb), cb=cb)))
        else:
            in_specs.append(pl.BlockSpec((8, w), functools.partial(lambda i, cb: (jnp.minimum((i + 1) * t8, n * t8 - 1), cb), cb=cb)))
    in_specs += [pl.BlockSpec(c.shape, functools.partial(lambda i, nd: (0,) * nd, nd=c.ndim)) for c in consts]
    out_specs = [pl.BlockSpec((tile, c), lambda i: (i, 0)) for (c, _) in outs]
    out_specs += [pl.BlockSpec((r, c), lambda i: (0, 0)) for (r, c) in accs]
    out_shape = [jax.ShapeDtypeStruct((s, c), dt) for (c, dt) in outs]
    out_shape += [jax.ShapeDtypeStruct((r, c), F32) for (r, c) in accs]
    args = [r[0] for r in rows] + [rows[idx][0] for idx, _ in halos] + list(consts)
    res = pl.pallas_call(
        body, name=name, grid=(n,), in_specs=in_specs + [ANY] * comm.n, out_specs=out_specs + [ANY] * comm.n,
        out_shape=out_shape + comm.out_shape(), scratch_shapes=comm.scratch(), compiler_params=_cp("arbitrary"),
    )(*args, *comm.args())
    return (list(res[:no + na]), list(res[no + na:])) if carried else list(res)


def _colsum8(v):
    t, c = v.shape
    return jnp.sum(v.reshape(t // 8, 8, c), axis=0)


def _rms(x, g):
    return x * lax.rsqrt(jnp.mean(x * x, axis=-1, keepdims=True) + NORM_EPS) * g


def _epi_rms(g):
    return (lambda r, rows, consts: (r, [_rms(r, consts[0])], []), [], [g.reshape(1, -1)], [(g.shape[-1], MXU)], [])


def _epi_rms_bwd(x, g, dres):
    def fn(r, rows, consts):
        xb, drb = rows
        _, vjp = jax.vjp(_rms, xb, consts[0])
        rstd = lax.rsqrt(jnp.mean(xb * xb, axis=-1, keepdims=True) + NORM_EPS)
        return drb + vjp(r)[0], [], [_colsum8(r * xb * rstd)]
    return (fn, [x, dres], [g.reshape(1, -1)], [], [(8, g.shape[-1])])


def _epi_att_stats(att, lse):
    def fn(r, rows, consts):
        hr = lax.broadcasted_iota(jnp.int32, (ATT_W, ATT_W), 0) // HEAD_DIM
        hc = lax.broadcasted_iota(jnp.int32, (ATT_W, ATT_W), 1) // HEAD_DIM
        delta = _nn(r[:, :ATT_W] * rows[0], (hr == hc).astype(F32), HIGH)
        lane = lax.broadcasted_iota(jnp.int32, delta.shape, 1)
        return r, [jnp.where(lane % HEAD_DIM < HEAD_DIM // 2, rows[1], delta)], []
    return (fn, [att, lse], [], [(ATT_W, F32)], [])


def _epi_wire(width):
    return (lambda r, rows, consts: (r, [r], []), [], [], [(width, MXU)], [])


def _rms_fwd(x, g, name, comm=None):
    def fn(rv, hv, cv):
        return [_rms(rv[0], cv[0])], []
    res = _rows(fn, [x], [g.reshape(1, -1)], [(x.shape[1], MXU)], tile=512, name=name, comm=comm)
    return res[0] if comm is None else (res[0][0], res[1])


def _slope_dist(hp, hh, dist, dil):
    hf = (2 * hp + hh + 1).astype(F32)
    slope = jnp.exp(jnp.zeros(dist.shape, F32) - hf * LN2)
    return slope * (dist.astype(F32) * float(dil))


ATT_G = 2048


def _deinterleave(dst, src, dil, ld, region, offset):
    for r in range(dil):
        rows = pl.ds(r, ld, stride=dil) if dil > 1 else pl.ds(0, ld)
        dst[r * region + offset:r * region + offset + ld, :] = src[rows, :]


def _deinterleave_edge(dst, src, dil, region, offset, first_row):
    for r in range(dil):
        rows = pl.ds(first_row + r, BLK, stride=dil) if dil > 1 else pl.ds(first_row, BLK)
        dst[r * region + offset:r * region + offset + BLK, :] = src[rows, :]


def _att_fwd_fused(proj, name, comm=None):
    s, npc = proj.shape
    gsz = ATT_G
    ng = s // gsz
    assert s % gsz == 0
    scale = HEAD_DIM ** -0.5
    comm = comm or _Comm()

    def body(*refs):
        (q_ref, kp_ref, kc_ref, vp_ref, vc_ref, att_ref, lse_ref, attb_ref, qd, kd, vd, nd, md, dd, nn, mn, dn), cm = comm.split(refs, 5, 3, 9)
        hp, g = pl.program_id(0), pl.program_id(1)
        comm.start_at((hp == 0) & (g == 0), cm)
        lane = lax.broadcasted_iota(jnp.int32, (BLK, BLK), 1)
        qi = lax.broadcasted_iota(jnp.int32, (BLK, 2 * BLK), 0)
        ki = lax.broadcasted_iota(jnp.int32, (BLK, 2 * BLK), 1)
        dist = BLK + qi - ki
        band = (dist >= 0) & (dist <= BLK)
        for pi, (_, dil) in enumerate(ATT_PATTERNS):
            ld = gsz // dil
            nbg = ld // BLK
            _deinterleave(qd, q_ref, dil, ld, ld, 0)
            _deinterleave(kd, kc_ref, dil, ld, ld + BLK, BLK)
            _deinterleave(vd, vc_ref, dil, ld, ld + BLK, BLK)
            _deinterleave_edge(kd, kp_ref, dil, ld + BLK, 0, gsz - BLK * dil)
            _deinterleave_edge(vd, vp_ref, dil, ld + BLK, 0, gsz - BLK * dil)
            bias = [_slope_dist(hp, hh, dist, dil) for hh in (0, 1)]

            def tile(t, carry, ld=ld, nbg=nbg, bias=bias):
                r, b = t // nbg, t % nbg
                qo = pl.multiple_of(r * ld + b * BLK, BLK)
                ko = pl.multiple_of(r * (ld + BLK) + b * BLK, BLK)
                q = qd[pl.ds(qo, BLK), :]
                kk = kd[pl.ds(ko, 2 * BLK), :].astype(MXU)
                vv = vd[pl.ds(ko, 2 * BLK), :].astype(MXU)
                valid = band & ((g > 0) | (b > 0) | (ki >= BLK))
                num = jnp.zeros((BLK, BLK), F32)
                mx = jnp.zeros((BLK, BLK), F32)
                den = jnp.zeros((BLK, BLK), F32)
                for hh in (0, 1):
                    hmask = (lane < HEAD_DIM) if hh == 0 else (lane >= HEAD_DIM)
                    qm = jnp.where(hmask, q, 0.0).astype(MXU)
                    sc = jnp.where(valid, _nt(qm, kk) * scale - bias[hh], NEG)
                    m = jnp.max(sc, axis=1, keepdims=True)
                    p = jnp.exp(sc - m)
                    dn_ = jnp.sum(p, axis=1, keepdims=True)
                    o = _nn(p.astype(MXU), vv)
                    num = jnp.where(hmask, o, num)
                    mx = jnp.where(hmask, m, mx)
                    den = jnp.where(hmask, dn_, den)
                nd[pl.ds(qo, BLK), :] = num
                md[pl.ds(qo, BLK), :] = mx
                dd[pl.ds(qo, BLK), :] = den
                return carry

            lax.fori_loop(0, dil * nbg, tile, 0, unroll=8)
            for r in range(dil):
                rows = pl.ds(r, ld, stride=dil) if dil > 1 else pl.ds(0, ld)
                nn.at[pi][rows, :] = nd[r * ld:(r + 1) * ld, :]
                mn.at[pi][rows, :] = md[r * ld:(r + 1) * ld, :]
                dn.at[pi][rows, :] = dd[r * ld:(r + 1) * ld, :]

        def merge(c, carry):
            rows = pl.ds(pl.multiple_of(c * 256, 256), 256)
            ms = [mn[pi, rows, :] for pi in range(len(ATT_PATTERNS))]
            m_all = functools.reduce(jnp.maximum, ms)
            num = jnp.zeros((256, BLK), F32)
            den = jnp.zeros((256, BLK), F32)
            for pi in range(len(ATT_PATTERNS)):
                e = jnp.exp(ms[pi] - m_all)
                num = num + nn[pi, rows, :] * e
                den = den + dn[pi, rows, :] * e
            att = num / den
            att_ref[rows, :] = att
            attb_ref[rows, :] = att.astype(MXU)
            lse_ref[rows, :] = m_all + jnp.log(den)
            return carry

        lax.fori_loop(0, gsz // 256, merge, 0)
        comm.wait_at((hp == 3) & (g == ng - 1), cm)

    def cur(base):
        return pl.BlockSpec((gsz, BLK), lambda hp, g: (g, base // BLK + hp))

    def prev(base):
        return pl.BlockSpec((gsz, BLK), lambda hp, g: (jnp.maximum(g - 1, 0), base // BLK + hp))

    o_spec = pl.BlockSpec((gsz, BLK), lambda hp, g: (g, hp))
    npat = len(ATT_PATTERNS)
    res = pl.pallas_call(
        body, name=name, grid=(4, ng),
        in_specs=[cur(C_Q), prev(C_K), cur(C_K), prev(C_V), cur(C_V)] + [ANY] * comm.n,
        out_specs=[o_spec] * 3 + [ANY] * comm.n,
        out_shape=[jax.ShapeDtypeStruct((s, ATT_W), F32)] * 2 + [jax.ShapeDtypeStruct((s, ATT_W), MXU)] + comm.out_shape(),
        scratch_shapes=[pltpu.VMEM((gsz, BLK), F32), pltpu.VMEM((2 * gsz, BLK), F32), pltpu.VMEM((2 * gsz, BLK), F32)]
        + [pltpu.VMEM((gsz, BLK), F32)] * 3 + [pltpu.VMEM((npat, gsz, BLK), F32)] * 3 + comm.scratch(),
        compiler_params=_cp("arbitrary", "arbitrary"),
    )(proj, proj, proj, proj, proj, *comm.args())
    return res[0], res[1], res[2], list(res[3:])


def _att_bwd_rev(proj, datt, stats, name, comm=None):
    s, npc = proj.shape
    gsz = ATT_G
    ng = s // gsz
    npat = len(ATT_PATTERNS)
    scale = HEAD_DIM ** -0.5
    comm = comm or _Comm()

    def body(*refs):
        (q_ref, kp_ref, kc_ref, vp_ref, vc_ref, do_ref, st_ref, dq_out, dk_out, dv_out,
         qd, dod, std, kd, vd, dqd, dkc, dvc, dkp, dvp, kcar, vcar, dq_ref, dk_ref, dv_ref), cm = comm.split(refs, 7, 3, 15)
        hp, gi = pl.program_id(0), pl.program_id(1)
        g = ng - 1 - gi
        comm.start_at((hp == 0) & (gi == 0), cm)

        @pl.when(gi == 0)
        def _():
            kcar[...] = jnp.zeros_like(kcar)
            vcar[...] = jnp.zeros_like(vcar)

        lane = lax.broadcasted_iota(jnp.int32, (BLK, BLK), 1)
        qi = lax.broadcasted_iota(jnp.int32, (BLK, 2 * BLK), 0)
        ki = lax.broadcasted_iota(jnp.int32, (BLK, 2 * BLK), 1)
        dist = BLK + qi - ki
        band = (dist >= 0) & (dist <= BLK)
        for pi, (_, dil) in enumerate(ATT_PATTERNS):
            ld = gsz // dil
            nbg = ld // BLK
            reg = ld + BLK
            for dst, src in ((qd, q_ref), (dod, do_ref), (std, st_ref)):
                _deinterleave(dst, src, dil, ld, ld, 0)
            for dst, p_ref, c_ref in ((kd, kp_ref, kc_ref), (vd, vp_ref, vc_ref)):
                _deinterleave(dst, c_ref, dil, ld, reg, BLK)
                _deinterleave_edge(dst, p_ref, dil, reg, 0, gsz - BLK * dil)
            bias = [_slope_dist(hp, hh, dist, dil) for hh in (0, 1)]

            def tile(t, carry, ld=ld, nbg=nbg, reg=reg, bias=bias):
                r, b = t // nbg, t % nbg
                oo = pl.multiple_of(r * ld + b * BLK, BLK)
                ko = pl.multiple_of(r * reg + b * BLK, BLK)
                q, do = qd[pl.ds(oo, BLK), :], dod[pl.ds(oo, BLK), :]
                st = std[pl.ds(oo, BLK), :]
                kk = kd[pl.ds(ko, 2 * BLK), :].astype(MXU)
                vv = vd[pl.ds(ko, 2 * BLK), :].astype(MXU)
                valid = band & ((g > 0) | (b > 0) | (ki >= BLK))
                dq = jnp.zeros((BLK, BLK), F32)
                dkk = jnp.zeros((2 * BLK, BLK), F32)
                dvv = jnp.zeros((2 * BLK, BLK), F32)
                for hh in (0, 1):
                    c0 = hh * HEAD_DIM
                    hmask = (lane < HEAD_DIM) if hh == 0 else (lane >= HEAD_DIM)
                    qm = jnp.where(hmask, q, 0.0).astype(MXU)
                    dom = jnp.where(hmask, do, 0.0).astype(MXU)
                    sc = _nt(qm, kk) * scale - bias[hh]
                    p = jnp.exp(jnp.where(valid, sc - st[:, c0:c0 + 1], NEG))
                    ds = (p * (_nt(dom, vv) - st[:, c0 + HEAD_DIM // 2:c0 + HEAD_DIM // 2 + 1])).astype(MXU)
                    dq = jnp.where(hmask, _nn(ds, kk), dq)
                    dkk = dkk + _tn(ds, qm)
                    dvv = dvv + _tn(p.astype(MXU), dom)
                dqd[pl.ds(oo, BLK), :] = dq * scale
                dkp[pl.ds(oo, BLK), :] = dkk[:BLK] * scale
                dkc[pl.ds(oo, BLK), :] = dkk[BLK:] * scale
                dvp[pl.ds(oo, BLK), :] = dvv[:BLK]
                dvc[pl.ds(oo, BLK), :] = dvv[BLK:]
                return carry

            lax.fori_loop(0, dil * nbg, tile, 0, unroll=8)
            for r in range(dil):
                rows = pl.ds(r, ld, stride=dil) if dil > 1 else pl.ds(0, ld)
                lo, hi = r * ld, (r + 1) * ld
                edge = slice(pi * gsz + r * BLK, pi * gsz + (r + 1) * BLK)
                for out, cur, prv, car in ((dk_ref, dkc, dkp, kcar), (dv_ref, dvc, dvp, vcar)):
                    later = car[edge, :] if nbg == 1 else jnp.concatenate([prv[lo + BLK:hi, :], car[edge, :]], axis=0)
                    total = cur[lo:hi, :] + later
                    car[edge, :] = prv[lo:lo + BLK, :]
                    out[rows, :] = total if pi == 0 else out[rows, :] + total
                dq_ref[rows, :] = dqd[lo:hi, :] if pi == 0 else dq_ref[rows, :] + dqd[lo:hi, :]
        for out, acc in ((dq_out, dq_ref), (dk_out, dk_ref), (dv_out, dv_ref)):
            out[...] = acc[...].astype(out.dtype)
        comm.wait_at((hp == 3) & (gi == ng - 1), cm)

    def pspec(base, shift):
        return pl.BlockSpec((gsz, BLK), lambda hp, gi: (jnp.maximum(ng - 1 - gi + shift, 0), base // BLK + hp))

    wspec = pl.BlockSpec((gsz, BLK), lambda hp, gi: (ng - 1 - gi, hp))
    in_specs = [pspec(C_Q, 0), pspec(C_K, -1), pspec(C_K, 0), pspec(C_V, -1), pspec(C_V, 0), wspec, wspec] + [ANY] * comm.n
    res = pl.pallas_call(
        body, name=name, grid=(4, ng), in_specs=in_specs,
        out_specs=[wspec] * 3 + [ANY] * comm.n,
        out_shape=[jax.ShapeDtypeStruct((s, ATT_W), MXU)] * 3 + comm.out_shape(),
        scratch_shapes=[pltpu.VMEM((gsz, BLK), F32)] * 3 + [pltpu.VMEM((2 * gsz, BLK), F32)] * 2
        + [pltpu.VMEM((gsz, BLK), F32)] * 5 + [pltpu.VMEM((npat * gsz, BLK), F32)] * 2 + [pltpu.VMEM((gsz, BLK), F32)] * 3
        + comm.scratch(),
        compiler_params=_cp("arbitrary", "arbitrary"),
    )(proj, proj, proj, proj, proj, datt, stats, *comm.args())
    return res[0], res[1], res[2], list(res[3:])


def _shift_down(cur, halo, sft):
    if sft == 0:
        return cur
    t = cur.shape[0]
    rolled = pltpu.roll(cur, sft, 0)
    hr = pltpu.roll(halo, sft, 0)
    row = lax.broadcasted_iota(jnp.int32, cur.shape, 0)
    return jnp.where(row < sft, jnp.tile(hr, (t // 8, 1)), rolled)


def _shift_up(cur, halo, sft):
    if sft == 0:
        return cur
    t = cur.shape[0]
    rolled = pltpu.roll(cur, t - sft, 0)
    hr = pltpu.roll(halo, 8 - sft, 0)
    row = lax.broadcasted_iota(jnp.int32, cur.shape, 0)
    return jnp.where(row >= t - sft, jnp.tile(hr, (t // 8, 1)), rolled)


def _conv(x, xh, w, b):
    y = b + x * w[CONV_K - 1:CONV_K]
    for k in range(CONV_K - 1):
        y = y + _shift_down(x, xh, CONV_K - 1 - k) * w[k:k + 1]
    return y


def _conv_bwd(x, xh, dy, dyh, w):
    dx = dy * w[CONV_K - 1:CONV_K]
    dws = []
    for k in range(CONV_K - 1):
        sft = CONV_K - 1 - k
        dx = dx + _shift_up(dy, dyh, sft) * w[k:k + 1]
        dws.append(jnp.sum(dy * _shift_down(x, xh, sft), axis=0, keepdims=True))
    dws.append(jnp.sum(dy * x, axis=0, keepdims=True))
    c = x.shape[1]
    dw = jnp.concatenate(dws + [jnp.zeros((8 - CONV_K, c), F32)], axis=0)
    return dx, dw, jnp.sum(dy, axis=0, keepdims=True)


def _pad8(w):
    return jnp.concatenate([w, jnp.zeros((8 - w.shape[0], w.shape[1]), w.dtype)], axis=0)


def _ssd_pre(proj, conv_w, conv_b, dt_bias128, name):
    def fn(rv, hv, cv):
        xbc, dtr = rv
        return [_silu(_conv(xbc, hv[0], cv[0], cv[1])), _softplus(dtr + cv[2])], []
    return _rows(fn, [(proj, 1024, C_XBC // 1024), (proj, BLK, C_DT // BLK)],
                 [_pad8(conv_w), conv_b.reshape(1, -1), dt_bias128],
                 [(1024, F32), (BLK, F32)], tile=256, name=name, halos=[(0, "prev")])


def _ssd_pre_bwd(proj, dxc, ddt, conv_w, conv_b, dt_bias128, name):
    def fn(rv, hv, cv):
        xbc, dtr, dxcb, ddtb = rv
        xh, dxch_raw, xnext = hv
        w, b, bias = cv
        pre = _conv(xbc, xh, w, b)
        sg = _sigmoid(pre)
        dpre = dxcb * (sg * (1.0 + pre * (1.0 - sg)))
        t = xbc.shape[0]
        tail = jnp.concatenate([xbc[t - 8:], xnext], axis=0)
        pre_n = _conv(tail[8:], tail[:8], w, b)
        sgn = _sigmoid(pre_n)
        dpre_h = dxch_raw * (sgn * (1.0 + pre_n * (1.0 - sgn)))
        dx, dw, db = _conv_bwd(xbc, xh, dpre, dpre_h, w)
        ddr = ddtb * _sigmoid(dtr + bias)
        return [dx, ddr], [dw, jnp.concatenate([db, jnp.zeros((7, db.shape[1]), F32)], axis=0), _colsum8(ddr)]
    return _rows(fn, [(proj, 1024, C_XBC // 1024), (proj, BLK, C_DT // BLK), dxc, ddt],
                 [_pad8(conv_w), conv_b.reshape(1, -1), dt_bias128],
                 [(1024, MXU), (BLK, MXU)], [(8, 1024), (8, 1024), (8, BLK)], tile=256, name=name,
                 halos=[(0, "prev"), (2, "next"), (0, "next")])


SSD_CPB = 1


def _head_cols(v, h0):
    lane = lax.broadcasted_iota(jnp.int32, (v.shape[0], BLK), 1)
    return jnp.where(lane < HEAD_DIM, v[:, h0:h0 + 1], v[:, h0 + 1:h0 + 2])


def _ssd_scan(xc, dt, par, name):
    s = xc.shape[0]
    nc = s // BLK

    def body(x_ref, dt_ref, par_ref, y_ref, st_ref, h_ref):
        c = pl.program_id(0)

        @pl.when(c == 0)
        def _():
            h_ref[...] = jnp.zeros_like(h_ref)

        st_ref[0] = h_ref[...]
        dt = dt_ref[...]
        a_row = -jnp.exp(par_ref[0:1, :])
        d_row = par_ref[1:2, :]
        ri = lax.broadcasted_iota(jnp.int32, (BLK, BLK), 0)
        ci = lax.broadcasted_iota(jnp.int32, (BLK, BLK), 1)
        tril = ri >= ci
        cs = _nn(tril.astype(F32), dt * a_row, HI)
        cst, dtt = cs.T, dt.T
        last = cs[BLK - 1:BLK, :]
        wcol = jnp.exp(last - cs) * dt
        ecs = jnp.exp(cs)
        elast = jnp.exp(last)
        for g in (0, 1):
            bg = x_ref[:, 512 + g * BLK:512 + (g + 1) * BLK].astype(MXU)
            cg = x_ref[:, 768 + g * BLK:768 + (g + 1) * BLK].astype(MXU)
            gm = _nt(cg, bg)
            for pp in (0, 1):
                pr = 2 * g + pp
                h0 = 2 * pr
                x2 = x_ref[:, pr * BLK:(pr + 1) * BLK]
                hprev = h_ref[pr * BLK:(pr + 1) * BLK, :]
                yp = jnp.zeros((BLK, BLK), F32)
                for hh in (0, 1):
                    h = h0 + hh
                    hmask = (ci < HEAD_DIM) if hh == 0 else (ci >= HEAD_DIM)
                    lm = jnp.exp(jnp.where(tril, cs[:, h:h + 1] - cst[h:h + 1, :], NEG))
                    mm = gm * lm * dtt[h:h + 1, :]
                    yp = yp + _nn(mm.astype(MXU), jnp.where(hmask, x2, 0.0).astype(MXU))
                y0 = _nt(cg, hprev.astype(MXU))
                y_ref[:, pr * BLK:(pr + 1) * BLK] = yp + _head_cols(ecs, h0) * y0 + _head_cols(d_row, h0) * x2
                dec = jnp.where(ri < HEAD_DIM, elast[:, h0:h0 + 1], elast[:, h0 + 1:h0 + 2])
                xw = (x2 * _head_cols(wcol, h0)).astype(MXU)
                h_ref[pr * BLK:(pr + 1) * BLK, :] = dec * hprev + _tn(xw, bg)

    return pl.pallas_call(
        body, name=name, grid=(nc,),
        in_specs=[pl.BlockSpec((BLK, 1024), lambda c: (c, 0)), pl.BlockSpec((BLK, BLK), lambda c: (c, 0)),
                  pl.BlockSpec((8, BLK), lambda c: (0, 0))],
        out_specs=[pl.BlockSpec((BLK, SSD_W), lambda c: (c, 0)), pl.BlockSpec((1, SSD_W, SSD_STATE), lambda c: (c, 0, 0))],
        out_shape=[jax.ShapeDtypeStruct((s, SSD_W), F32), jax.ShapeDtypeStruct((nc, SSD_W, SSD_STATE), F32)],
        scratch_shapes=[pltpu.VMEM((SSD_W, SSD_STATE), F32)],
        compiler_params=_cp("arbitrary"),
    )(xc, dt, par)


def _ssd_scan_bwd(xc, dt, par, st, dy, name, comm=None):
    s = xc.shape[0]
    cpb = SSD_CPB
    nb = s // (cpb * BLK)
    comm = comm or _Comm()

    def chunk(x_ref, dt_ref, par_ref, st_ref, dy_ref, dx_ref, ddt_ref, dal_ref, dd_ref, dh_ref):
        dt = dt_ref[...]
        a_row = -jnp.exp(par_ref[0:1, :])
        d_row = par_ref[1:2, :]
        ri = lax.broadcasted_iota(jnp.int32, (BLK, BLK), 0)
        ci = lax.broadcasted_iota(jnp.int32, (BLK, BLK), 1)
        tril = ri >= ci
        cs = _nn(tril.astype(F32), dt * a_row, HI)
        cst, dtt = cs.T, dt.T
        last = cs[BLK - 1:BLK, :]
        tolast = jnp.exp(last - cs)
        wcol = tolast * dt
        ecs = jnp.exp(cs)
        elast = jnp.exp(last)
        dcs_col = jnp.zeros((BLK, BLK), F32)
        ddt_col = jnp.zeros((BLK, BLK), F32)
        dcs_row = jnp.zeros((BLK, BLK), F32)
        ddt_row = jnp.zeros((BLK, BLK), F32)
        dlast = jnp.zeros((1, BLK), F32)
        ddsk = jnp.zeros((1, BLK), F32)
        for g in (0, 1):
            bg32 = x_ref[:, 512 + g * BLK:512 + (g + 1) * BLK]
            cg32 = x_ref[:, 768 + g * BLK:768 + (g + 1) * BLK]
            bg, cg = bg32.astype(MXU), cg32.astype(MXU)
            gm = _nt(cg, bg)
            dgm = jnp.zeros((BLK, BLK), F32)
            dbg = jnp.zeros((BLK, BLK), F32)
            dcg = jnp.zeros((BLK, BLK), F32)
            for pp in (0, 1):
                pr = 2 * g + pp
                h0 = 2 * pr
                x2 = x_ref[:, pr * BLK:(pr + 1) * BLK]
                dy2 = dy_ref[:, pr * BLK:(pr + 1) * BLK]
                hprev = st_ref[0, pr * BLK:(pr + 1) * BLK, :]
                dhn = dh_ref[pr * BLK:(pr + 1) * BLK, :]
                x2m, dhnm = x2.astype(MXU), dhn.astype(MXU)
                zb = _nt(bg, dhnm)
                y0 = _nt(cg, hprev.astype(MXU))
                esel = _head_cols(ecs, h0)
                wsel = _head_cols(wcol, h0)
                dx2 = _head_cols(d_row, h0) * dy2 + wsel * zb
                pick2 = (((ri < HEAD_DIM) & (ci == h0)) | ((ri >= HEAD_DIM) & (ci == h0 + 1))).astype(F32)
                sums = _nn(jnp.concatenate([dy2 * y0, x2 * zb, dy2 * x2], axis=0), pick2, HIGH)
                de2, dw2, dd2 = sums[:BLK], sums[BLK:2 * BLK], sums[2 * BLK:]
                v2 = dw2 * wcol
                dcs_col = dcs_col + ecs * de2 - v2
                ddt_col = ddt_col + dw2 * tolast
                hsum = _nn(dhn * hprev, jnp.ones((BLK, BLK), F32), HIGH)
                dlast = dlast + elast * jnp.sum(jnp.where(pick2 > 0.0, hsum, 0.0), axis=0, keepdims=True) \
                    + jnp.sum(v2, axis=0, keepdims=True)
                ddsk = ddsk + jnp.sum(dd2, axis=0, keepdims=True)
                ts = []
                for hh in (0, 1):
                    h = h0 + hh
                    hmask = (ci < HEAD_DIM) if hh == 0 else (ci >= HEAD_DIM)
                    ons = (ri == h).astype(F32)
                    dym = jnp.where(hmask, dy2, 0.0).astype(MXU)
                    dt_r = dtt[h:h + 1, :]
                    lm = jnp.exp(jnp.where(tril, cs[:, h:h + 1] - cst[h:h + 1, :], NEG))
                    mm = gm * lm * dt_r
                    dx2 = dx2 + _tn(mm.astype(MXU), dym)
                    dm = _nt(dym, x2m)
                    t1 = dm * lm
                    dgm = dgm + t1 * dt_r
                    tt = t1 * gm
                    ddt_row = ddt_row + ons * jnp.sum(tt, axis=0, keepdims=True)
                    t = tt * dt_r
                    dcs_row = dcs_row - ons * jnp.sum(t, axis=0, keepdims=True)
                    ts.append(t)
                rows2 = lax.broadcasted_iota(jnp.int32, (2 * BLK, BLK), 0)
                lane2 = lax.broadcasted_iota(jnp.int32, (2 * BLK, BLK), 1)
                to_lane = ((rows2 < BLK) & (lane2 == h0)) | ((rows2 >= BLK) & (lane2 == h0 + 1))
                dcs_col = dcs_col + _nn(jnp.concatenate(ts, axis=1), to_lane.astype(F32), HIGH)
                dx_ref[:, pr * BLK:(pr + 1) * BLK] = dx2
                edy = (esel * dy2).astype(MXU)
                dcg = dcg + _nn(edy, hprev.astype(MXU))
                dec = jnp.where(ri < HEAD_DIM, elast[:, h0:h0 + 1], elast[:, h0 + 1:h0 + 2])
                dh_ref[pr * BLK:(pr + 1) * BLK, :] = dec * dhn + _tn(edy, cg)
                dbg = dbg + _nn((x2 * wsel).astype(MXU), dhnm)
            dgmm = dgm.astype(MXU)
            dx_ref[:, 512 + g * BLK:512 + (g + 1) * BLK] = dbg + _tn(dgmm, cg)
            dx_ref[:, 768 + g * BLK:768 + (g + 1) * BLK] = dcg + _nn(dgmm, bg)
        dcs = dcs_col + dcs_row.T + jnp.where(ri == BLK - 1, dlast, 0.0)
        dda = _nn((ri <= ci).astype(F32), dcs, HI)
        ddt_ref[...] = ddt_col + ddt_row.T + a_row * dda
        da = jnp.sum(dt * dda, axis=0, keepdims=True)
        dal_ref[0:1, :] += da * a_row
        dd_ref[0:1, :] += ddsk

    def body(*refs):
        (x_ref, dt_ref, par_ref, st_ref, dy_ref, dx_ref, ddt_ref, dal_ref, dd_ref, dh_ref), cm = comm.split(refs, 5, 4, 1)
        c = pl.program_id(0)
        comm.start_at(c == 0, cm)

        @pl.when(c == 0)
        def _():
            dh_ref[...] = jnp.zeros_like(dh_ref)
            dal_ref[...] = jnp.zeros_like(dal_ref)
            dd_ref[...] = jnp.zeros_like(dd_ref)

        for cc in reversed(range(cpb)):
            rows = pl.ds(cc * BLK, BLK)
            chunk(x_ref.at[rows], dt_ref.at[rows], par_ref, st_ref.at[pl.ds(cc, 1)], dy_ref.at[rows], dx_ref.at[rows],
                  ddt_ref.at[rows], dal_ref, dd_ref, dh_ref)
        comm.wait_at(c == nb - 1, cm)

    rev = lambda c: (nb - 1 - c, 0)
    tb = cpb * BLK
    res = pl.pallas_call(
        body, name=name, grid=(nb,),
        in_specs=[pl.BlockSpec((tb, 1024), rev), pl.BlockSpec((tb, BLK), rev), pl.BlockSpec((8, BLK), lambda c: (0, 0)),
                  pl.BlockSpec((cpb, SSD_W, SSD_STATE), lambda c: (nb - 1 - c, 0, 0)), pl.BlockSpec((tb, SSD_W), rev)]
        + [ANY] * comm.n,
        out_specs=[pl.BlockSpec((tb, 1024), rev), pl.BlockSpec((tb, BLK), rev),
                   pl.BlockSpec((8, BLK), lambda c: (0, 0)), pl.BlockSpec((8, BLK), lambda c: (0, 0))] + [ANY] * comm.n,
        out_shape=[jax.ShapeDtypeStruct((s, 1024), F32), jax.ShapeDtypeStruct((s, BLK), F32),
                   jax.ShapeDtypeStruct((8, BLK), F32), jax.ShapeDtypeStruct((8, BLK), F32)] + comm.out_shape(),
        scratch_shapes=[pltpu.VMEM((SSD_W, SSD_STATE), F32)] + comm.scratch(),
        compiler_params=_cp("arbitrary"),
    )(xc, dt, par, st, dy, *comm.args())
    return res[0], res[1], res[2], res[3], list(res[4:])


def _ssd_gate(y, z, w):
    t = y * _silu(z)
    outs = []
    for g in (0, 1):
        tg = t[:, g * 256:(g + 1) * 256]
        outs.append(tg * lax.rsqrt(jnp.mean(tg * tg, axis=-1, keepdims=True) + SSD_NORM_EPS))
    return jnp.concatenate(outs, axis=1) * w


def _ssd_post(y, proj, norm_w, name):
    def fn(rv, hv, cv):
        return [_ssd_gate(rv[0], rv[1], cv[0])], []
    return _rows(fn, [y, (proj, SSD_W, C_Z // SSD_W)], [norm_w.reshape(1, -1)], [(SSD_W, MXU)], tile=512, name=name)[0]


def _ssd_post_bwd(y, proj, norm_w, dout, name):
    def fn(rv, hv, cv):
        yb, zb, db = rv
        _, vjp = jax.vjp(lambda a, b: _ssd_gate(a, b, cv[0]), yb, zb)
        dy, dz = vjp(db)
        t = yb * _silu(zb)
        nrm = []
        for g in (0, 1):
            tg = t[:, g * 256:(g + 1) * 256]
            nrm.append(tg * lax.rsqrt(jnp.mean(tg * tg, axis=-1, keepdims=True) + SSD_NORM_EPS))
        return [dy, dz], [_colsum8(db * jnp.concatenate(nrm, axis=1))]
    return _rows(fn, [y, (proj, SSD_W, C_Z // SSD_W), dout], [norm_w.reshape(1, -1)],
                 [(SSD_W, F32), (SSD_W, MXU)], [(8, SSD_W)], tile=512, name=name)


LRU_T = 256


def _lru_conv(proj, conv_w, conv_b, name):
    def fn(rv, hv, cv):
        return [_conv(rv[0], hv[0], cv[0], cv[1])], []
    return _rows(fn, [(proj, LRU_W, C_XL // LRU_W)], [_pad8(conv_w), conv_b.reshape(1, -1)], [(LRU_W, F32)],
                 tile=512, name=name, halos=[(0, "prev")])[0]


def _lru_conv_bwd(proj, dxc, conv_w, name):
    def fn(rv, hv, cv):
        dx, dw, db = _conv_bwd(rv[0], hv[0], rv[1], hv[1], cv[0])
        return [dx], [dw, jnp.concatenate([db, jnp.zeros((7, db.shape[1]), F32)], axis=0)]
    return _rows(fn, [(proj, LRU_W, C_XL // LRU_W), dxc], [_pad8(conv_w)], [(LRU_W, MXU)], [(8, LRU_W), (8, LRU_W)],
                 tile=512, name=name, halos=[(0, "prev"), (1, "next")])


def _lru_au(pre_a, pre_x, xc, ba, bx, lam):
    r = _sigmoid(pre_a + ba)
    i = _sigmoid(pre_x + bx)
    log_a = -LRU_C * r * _softplus(-lam)
    a = jnp.exp(log_a)
    u = jnp.sqrt(1.0 - jnp.exp(2.0 * log_a)) * (i * xc)
    return a, u


def _lru_scan(pre, xc, proj, par, name):
    s = xc.shape[0]
    t = LRU_T

    def body(pre_ref, xc_ref, g_ref, par_ref, out_ref, h_ref, carry):
        c = pl.program_id(0)

        @pl.when(c == 0)
        def _():
            carry[...] = jnp.zeros_like(carry)

        a, u = _lru_au(pre_ref[:, :LRU_W], pre_ref[:, LRU_W:], xc_ref[...], par_ref[0:1, :], par_ref[1:2, :], par_ref[2:3, :])
        row = lax.broadcasted_iota(jnp.int32, (t, LRU_W), 0)
        sft = 1
        while sft < t:
            keep = row >= sft
            a_s = jnp.where(keep, pltpu.roll(a, sft, 0), 1.0)
            u_s = jnp.where(keep, pltpu.roll(u, sft, 0), 0.0)
            u = a * u_s + u
            a = a * a_s
            sft *= 2
        h = a * carry[0:1, :] + u
        h_ref[...] = h
        out_ref[...] = (h * _gelu(g_ref[...])).astype(out_ref.dtype)
        carry[0:1, :] = h[t - 1:t, :]

    return pl.pallas_call(
        body, name=name, grid=(s // t,),
        in_specs=[pl.BlockSpec((t, 2 * LRU_W), lambda c: (c, 0)), pl.BlockSpec((t, LRU_W), lambda c: (c, 0)),
                  pl.BlockSpec((t, LRU_W), lambda c: (c, C_G // LRU_W)), pl.BlockSpec((8, LRU_W), lambda c: (0, 0))],
        out_specs=[pl.BlockSpec((t, LRU_W), lambda c: (c, 0))] * 2,
        out_shape=[jax.ShapeDtypeStruct((s, LRU_W), MXU), jax.ShapeDtypeStruct((s, LRU_W), F32)],
        scratch_shapes=[pltpu.VMEM((8, LRU_W), F32)],
        compiler_params=_cp("arbitrary"),
    )(pre, xc, proj, par)


def _lru_scan_bwd(pre, xc, proj, par, h, dout, name):
    s = xc.shape[0]
    t = LRU_T
    n = s // t
    t8 = t // 8

    def body(pre_ref, xc_ref, g_ref, par_ref, h_ref, hh_ref, do_ref, dpre_ref, dxc_ref, dg_ref, dpar_ref, carry):
        c = pl.program_id(0)

        @pl.when(c == 0)
        def _():
            carry[...] = jnp.zeros_like(carry)
            dpar_ref[...] = jnp.zeros_like(dpar_ref)

        pa, px, xcb = pre_ref[:, :LRU_W], pre_ref[:, LRU_W:], xc_ref[...]
        ba, bx, lam = par_ref[0:1, :], par_ref[1:2, :], par_ref[2:3, :]
        (a, u), vjp = jax.vjp(_lru_au, pa, px, xcb, ba, bx, lam)
        g = g_ref[...]
        hcur = h_ref[...]
        do = do_ref[...]
        _, gvjp = jax.vjp(_gelu, g)
        dg_ref[...] = gvjp(do * hcur)[0].astype(dg_ref.dtype)
        row = lax.broadcasted_iota(jnp.int32, (t, LRU_W), 0)
        v = do * _gelu(g) + jnp.where(row == t - 1, carry[0:1, :], 0.0)
        b = jnp.where(row == t - 1, 0.0, pltpu.roll(a, t - 1, 0))
        sft = 1
        while sft < t:
            keep = row < t - sft
            b_s = jnp.where(keep, pltpu.roll(b, t - sft, 0), 1.0)
            v_s = jnp.where(keep, pltpu.roll(v, t - sft, 0), 0.0)
            v = b * v_s + v
            b = b * b_s
            sft *= 2
        dh = v
        carry[0:1, :] = a[0:1, :] * dh[0:1, :]
        hhalo = jnp.where(c == n - 1, 0.0, hh_ref[...])
        hprev = _shift_down(hcur, hhalo, 1)
        dpa, dpx, dxc, dba, dbx, dlam = vjp((dh * hprev, dh))
        dpre_ref[:, :LRU_W] = dpa
        dpre_ref[:, LRU_W:] = dpx
        dxc_ref[...] = dxc
        dpar_ref[0:1, :] += dba
        dpar_ref[1:2, :] += dbx
        dpar_ref[2:3, :] += dlam

    rev = lambda c: (n - 1 - c, 0)
    return pl.pallas_call(
        body, name=name, grid=(n,),
        in_specs=[pl.BlockSpec((t, 2 * LRU_W), rev), pl.BlockSpec((t, LRU_W), rev),
                  pl.BlockSpec((t, LRU_W), lambda c: (n - 1 - c, C_G // LRU_W)), pl.BlockSpec((8, LRU_W), lambda c: (0, 0)),
                  pl.BlockSpec((t, LRU_W), rev),
                  pl.BlockSpec((8, LRU_W), lambda c: (jnp.maximum((n - 1 - c) * t8 - 1, 0), 0)),
                  pl.BlockSpec((t, LRU_W), lambda c: (n - 1 - c, dout.shape[1] // LRU_W - 1))],
        out_specs=[pl.BlockSpec((t, 2 * LRU_W), rev), pl.BlockSpec((t, LRU_W), rev), pl.BlockSpec((t, LRU_W), rev),
                   pl.BlockSpec((8, LRU_W), lambda c: (0, 0))],
        out_shape=[jax.ShapeDtypeStruct((s, 2 * LRU_W), F32), jax.ShapeDtypeStruct((s, LRU_W), F32),
                   jax.ShapeDtypeStruct((s, LRU_W), MXU), jax.ShapeDtypeStruct((8, LRU_W), F32)],
        scratch_shapes=[pltpu.VMEM((8, LRU_W), F32)],
        compiler_params=_cp("arbitrary"),
    )(pre, xc, proj, par, h, h, dout)


def _swiglu_act(gu, name):
    def fn(rv, hv, cv):
        return [_silu(rv[0].astype(F32)) * rv[1].astype(F32)], []
    return _rows(fn, [(gu, D_FF, 0), (gu, D_FF, 1)], [], [(D_FF, MXU)], tile=256, name=name)[0]


def _swiglu_bwd(gu, da, name):
    def fn(rv, hv, cv):
        gt, up, dab = [t.astype(F32) for t in rv]
        sg = _sigmoid(gt)
        dgate = dab * up * (sg * (1.0 + gt * (1.0 - sg)))
        dup = dab * (gt * sg)
        return [jnp.concatenate([dgate, dup], axis=1)], []
    return _rows(fn, [(gu, D_FF, 0), (gu, D_FF, 1), da], [], [(2 * D_FF, MXU)], tile=256, name=name)[0]


def _loss_head(x, g, target, name):
    d = x.shape[1]

    def fn(rv, hv, cv):
        xb, tb = rv
        y, vjp = jax.vjp(_rms, xb, cv[0])
        err = y - tb
        dy = err * (1.0 / d)
        dx, _ = vjp(dy)
        rstd = lax.rsqrt(jnp.mean(xb * xb, axis=-1, keepdims=True) + NORM_EPS)
        e2 = err * err * (0.5 / d)
        e2 = functools.reduce(lambda a, b: a + b, [e2[:, k * BLK:(k + 1) * BLK] for k in range(d // BLK)])
        return [dx], [_colsum8(dy * xb * rstd), _colsum8(e2)]
    return _rows(fn, [x, target], [g.reshape(1, -1)], [(d, F32)], [(8, d), (8, BLK)], tile=512, name=name)


ANY = pl.BlockSpec(memory_space=pl.ANY)


def _coords():
    return lax.axis_index("x"), lax.axis_index("y"), lax.axis_index("c")


class _Comm:
    def __init__(self, gathers=(), scatters=()):
        self.gathers = list(gathers)
        self.scatters = list(scatters)
        self.n = len(self.gathers) + len(self.scatters)

    def args(self):
        return [g[0] for g in self.gathers] + self.scatters

    def out_shape(self):
        out = [jax.ShapeDtypeStruct((4,) + (a.shape if l is None else a.shape[1:]), a.dtype) for a, l, _ in self.gathers]
        return out + [jax.ShapeDtypeStruct((3,) + a.shape[1:], a.dtype) for a in self.scatters]

    def scratch(self):
        if not self.n:
            return []
        return [pltpu.SemaphoreType.DMA((3 * self.n,)), pltpu.SemaphoreType.DMA((3 * self.n,)),
                pltpu.SemaphoreType.DMA((max(len(self.gathers), 1),)),
                pltpu.SemaphoreType.DMA((3 * self.n,)), pltpu.SemaphoreType.DMA((3 * self.n,))]

    def split(self, refs, n_in, n_out, n_scratch):
        refs = list(refs)
        n = self.n
        own = refs[:n_in] + refs[n_in + n:n_in + n + n_out] + refs[n_in + 2 * n + n_out:n_in + 2 * n + n_out + n_scratch]
        cm = (refs[n_in:n_in + n], refs[n_in + n + n_out:n_in + 2 * n + n_out], refs[n_in + 2 * n + n_out + n_scratch:])
        return own, cm

    def _copies(self, cm, arriving):
        ins, outs, (send, recv, local, _, _) = cm
        x, y, c = _coords()
        me = 2 * x + y
        chips = [(1 - x, y), (x, 1 - y), (1 - x, 1 - y)]
        remote, locals_ = [], []
        ng = len(self.gathers)
        for i in range(self.n):
            if i < ng:
                _, l, halved = self.gathers[i]
                slab = ins[i] if l is None else ins[i].at[l]
                if not arriving:
                    locals_.append(pltpu.make_async_copy(slab, outs[i].at[me], local.at[i]))
            for j, (px, py) in enumerate(chips):
                if i < ng:
                    slot = 2 * px + py if arriving else me
                    src, dst = (slab.at[c], outs[i].at[slot, c]) if halved else (slab, outs[i].at[slot])
                else:
                    src, dst = ins[i].at[2 * px + py], outs[i].at[j]
                remote.append(pltpu.make_async_remote_copy(src, dst, send.at[3 * i + j], recv.at[3 * i + j],
                                                           device_id=(px, py, c), device_id_type=MESH))
        return remote, locals_

    def _handovers(self, cm, arriving):
        _, outs, (_, _, _, send, recv) = cm
        x, y, c = _coords()
        chips = [(1 - x, y), (x, 1 - y), (1 - x, 1 - y)]
        cps = []
        for i, (_, _, halved) in enumerate(self.gathers):
            if halved:
                for j, (px, py) in enumerate(chips):
                    src = outs[i].at[2 * px + py, c]
                    dst = outs[i].at[2 * px + py, 1 - c if arriving else c]
                    cps.append(pltpu.make_async_remote_copy(src, dst, send.at[3 * i + j], recv.at[3 * i + j],
                                                            device_id=(x, y, 1 - c), device_id_type=MESH))
        return cps

    def start_at(self, cond, cm):
        def go():
            remote, locals_ = self._copies(cm, False)
            for cp in locals_ + remote:
                cp.start()

        if self.n:
            go() if cond is True else pl.when(cond)(go)

    def wait_at(self, cond, cm):
        def go():
            for cp in self._copies(cm, True)[0]:
                cp.wait_recv()
            handed = self._handovers(cm, False)
            for cp in handed:
                cp.start()
            for cp in self._handovers(cm, True):
                cp.wait_recv()
            remote, locals_ = self._copies(cm, False)
            for cp in handed + remote:
                cp.wait_send()
            for cp in locals_:
                cp.wait()

        if self.n:
            go() if cond is True else pl.when(cond)(go)


def _swap_sibling(arrs):
    n = len(arrs)

    def body(*refs):
        ins, outs, send, recv = refs[:n], refs[n:2 * n], refs[2 * n], refs[2 * n + 1]
        x, y, c = _coords()
        cps = [pltpu.make_async_remote_copy(ins[i], outs[i], send.at[i], recv.at[i], device_id=(x, y, 1 - c), device_id_type=MESH)
               for i in range(n)]
        for cp in cps:
            cp.start()
        for cp in cps:
            cp.wait_recv()
        for cp in cps:
            cp.wait_send()

    return list(pl.pallas_call(
        body, name="swap_sibling", in_specs=[ANY] * n, out_specs=[ANY] * n,
        out_shape=[jax.ShapeDtypeStruct(a.shape, a.dtype) for a in arrs],
        scratch_shapes=[pltpu.SemaphoreType.DMA((n,)), pltpu.SemaphoreType.DMA((n,))],
        compiler_params=pltpu.CompilerParams(has_side_effects=True),
    )(*arrs))


def _gather_small(gs):
    def body(g_ref, o_ref, send_sems, recv_sems, local_sem):
        x, y, c = _coords()
        me = 4 * x + 2 * y + c
        mine = pltpu.make_async_copy(g_ref, o_ref.at[me], local_sem)
        mine.start()
        sends = []
        for k in range(1, 8):
            px, py, pc = x ^ (k >> 2), y ^ ((k >> 1) & 1), c ^ (k & 1)
            sends.append((pltpu.make_async_remote_copy(g_ref, o_ref.at[me], send_sems.at[k - 1], recv_sems.at[k - 1],
                                                       device_id=(px, py, pc), device_id_type=MESH), 4 * px + 2 * py + pc, k))
        for cp, _, _ in sends:
            cp.start()
        for cp, src, k in sends:
            pltpu.make_async_remote_copy(g_ref, o_ref.at[src], send_sems.at[k - 1], recv_sems.at[k - 1],
                                         device_id=(x, y, c), device_id_type=MESH).wait_recv()
        for cp, _, _ in sends:
            cp.wait_send()
        mine.wait()

    return pl.pallas_call(
        body, name="gather_small", in_specs=[ANY], out_specs=ANY,
        out_shape=jax.ShapeDtypeStruct((8,) + gs.shape, gs.dtype),
        scratch_shapes=[pltpu.SemaphoreType.DMA((7,)), pltpu.SemaphoreType.DMA((7,)), pltpu.SemaphoreType.DMA],
        compiler_params=pltpu.CompilerParams(has_side_effects=True),
    )(gs)


def _sum_slots(own, others, name, tile):
    k, r, c = others.shape

    def body(*refs):
        if own is None:
            o_ref, out_ref = refs
            acc = o_ref[0].astype(F32)
            first = 1
        else:
            own_ref, o_ref, out_ref = refs
            acc = own_ref[...]
            first = 0
        for j in range(first, k):
            acc = acc + o_ref[j].astype(F32)
        out_ref[...] = acc

    row = pl.BlockSpec((tile, c), lambda i: (i, 0))
    specs = ([] if own is None else [row]) + [pl.BlockSpec((k, tile, c), lambda i: (0, i, 0))]
    args = ([] if own is None else [own]) + [others]
    return pl.pallas_call(body, name=name, grid=(r // tile,), in_specs=specs, out_specs=row,
                          out_shape=jax.ShapeDtypeStruct((r, c), F32), compiler_params=_cp("parallel"))(*args)


def _adamw(w, m, v, ga, gb, name, tile, rows_first=False):
    lead = 0 if rows_first else w.ndim - 2
    r, c = w.shape[-2:]

    def body(*refs):
        vals = [ref[0] if lead else ref[...] for ref in refs[:len(refs) - 4]]
        w_, m_, v_, g = vals[0], vals[1], vals[2], vals[3]
        if gb is not None:
            g = g + vals[4]
        nm = ADAM_B1 * m_ + (1.0 - ADAM_B1) * g
        nv = ADAM_B2 * v_ + (1.0 - ADAM_B2) * (g * g)
        d = -ADAM_LR * ((nm / BC1) / (jnp.sqrt(nv / BC2) + ADAM_EPS) + ADAM_WD * w_)
        for ref, val in zip(refs[len(refs) - 4:], (g, d, nm, nv)):
            if lead:
                ref[0] = val
            else:
                ref[...] = val

    if rows_first:
        row = pl.BlockSpec((tile,) + w.shape[1:], lambda i: (i, 0, 0))
        grid = (w.shape[0] // tile,)
    elif lead:
        row = pl.BlockSpec((1, tile, c), lambda l, i: (l, i, 0))
        grid = (w.shape[0], r // tile)
    else:
        row = pl.BlockSpec((tile, c), lambda i: (i, 0))
        grid = (r // tile,)
    args = [w, m, v, ga] + ([] if gb is None else [gb])
    return pl.pallas_call(body, name=name, grid=grid, in_specs=[row] * len(args), out_specs=[row] * 4,
                          out_shape=[jax.ShapeDtypeStruct(w.shape, F32)] * 4,
                          compiler_params=_cp(*(["parallel"] * len(grid))))(*args)


MATS = ("w_in", "w_out", "w_gate", "w_up", "w_down")
CONVS = ("ssd_conv_w", "lru_conv_w")
BIG = MATS + CONVS
TRANSPOSED = ("w_gate", "w_up")
COL_SHARDED = ("ssd_conv_w", "lru_conv_w")
W_IN_SHARD = IN_COLS // 4
W_IN_PAD = 1056
SMALL = ("norm_mix", "ssd_conv_b", "ssd_dt_bias", "ssd_a_log", "ssd_d", "ssd_norm", "lru_conv_b", "lru_wa", "lru_ba",
         "lru_wx", "lru_bx", "lru_lambda", "norm_ffn", "norm_final")
WEIGHTS = ("norm_mix", "w_in", "ssd_conv_w", "ssd_conv_b", "ssd_dt_bias", "ssd_a_log", "ssd_d", "ssd_norm", "lru_conv_w",
           "lru_conv_b", "lru_wa", "lru_ba", "lru_wx", "lru_bx", "lru_lambda", "w_out", "norm_ffn", "w_gate", "w_up",
           "w_down", "norm_final")
ROW_TILE = {"w_in": W_IN_SHARD, "w_out": 128, "w_gate": 352, "w_up": 352, "w_down": 352}
W_IN_ADAM_TILE = 54


def _pack(arrs, width, row_mult, dtype):
    flat = jnp.concatenate([a.reshape(-1).astype(dtype) for a in arrs])
    rows = -(-flat.shape[0] // width)
    rows = -(-rows // row_mult) * row_mult
    flat = jnp.pad(flat, (0, rows * width - flat.shape[0]))
    return flat.reshape(rows, width)


def _unpack(buf, shapes):
    flat = buf.reshape(-1)
    out, off = [], 0
    for shp in shapes:
        n = int(np.prod(shp))
        out.append(flat[off:off + n].reshape(shp))
        off += n
    return out


def _join(name, g4):
    if name in COL_SHARDED:
        return jnp.moveaxis(g4, 0, -2).reshape(g4.shape[1:-1] + (4 * g4.shape[-1],))
    return g4.reshape((4 * g4.shape[1],) + g4.shape[2:])


def _slabs(name, g):
    if name in COL_SHARDED:
        return jnp.moveaxis(g.reshape(g.shape[:-1] + (4, g.shape[-1] // 4)), -2, 0)
    return g.reshape((4, g.shape[0] // 4) + g.shape[1:])


def _w_in_rows(g4):
    def nat(lo, hi):
        out = []
        while lo < hi:
            j = lo // W_IN_SHARD
            stop = min(hi, (j + 1) * W_IN_SHARD)
            out.append((j, lo - j * W_IN_SHARD, stop - lo))
            lo = stop
        return out
    pieces = nat(0, 3072) + nat(3080, IN_COLS) + nat(3072, 3080)

    def body(g_ref, o_ref):
        row = 0
        for j, first, n in pieces:
            o_ref[row:row + n, :] = g_ref[j, first:first + n, :]
            row += n
        o_ref[row:, :] = jnp.zeros((NP - row, o_ref.shape[1]), o_ref.dtype)

    return pl.pallas_call(body, name="w_in_rows", out_shape=jax.ShapeDtypeStruct((NP, g4.shape[-1]), g4.dtype),
                          compiler_params=pltpu.CompilerParams(vmem_limit_bytes=VMEM_LIMIT))(g4)


def _w_in_slabs(gt):
    def kern(n):
        return n if n < 3072 else (C_DT + n - 3072 if n < 3080 else n - 8)
    slabs = []
    for j in range(4):
        lo, hi = j * W_IN_SHARD, (j + 1) * W_IN_SHARD
        cuts = sorted({lo, hi} | {c for c in (3072, 3080) if lo < c < hi})
        slabs.append(jnp.concatenate([gt[kern(a):kern(a) + b - a] for a, b in zip(cuts[:-1], cuts[1:])], axis=0))
    return jnp.stack(slabs, axis=0)


def _block_diag(w):
    eye = jnp.eye(LRU_BLOCKS, dtype=w.dtype)
    return jnp.einsum("ncd,nm->ncmd", w, eye).reshape(LRU_W, LRU_W)


def _block_diag_extract(g):
    g4 = g.reshape(LRU_BLOCKS, 64, LRU_BLOCKS, 64)
    return jnp.stack([g4[n, :, n, :] for n in range(LRU_BLOCKS)], axis=0)


def _lanes128(v):
    return jnp.pad(v, (0, BLK - v.shape[0])).reshape(1, BLK)


def _layer_mixers(x, p, comm=None, h=None):
    if h is None:
        h = _rms_fwd(x, p["norm_mix"], "rms_mix")
    proj = _mm(h, p["w_in_t"], tb=True, tm=1024, tn=1408, tk=1024, name="mm_in")
    att, lse, attb, got = _att_fwd_fused(proj, "att_fwd", comm)
    xconv, dt = _ssd_pre(proj, p["ssd_conv_w"], p["ssd_conv_b"], _lanes128(p["ssd_dt_bias"]), "ssd_pre")
    spar = jnp.concatenate([_lanes128(p["ssd_a_log"]), _lanes128(p["ssd_d"]), jnp.zeros((6, BLK), F32)], axis=0)
    y, states = _ssd_scan(xconv, dt, spar, "ssd_scan")
    ssd = _ssd_post(y, proj, p["ssd_norm"], "ssd_post")
    xc = _lru_conv(proj, p["lru_conv_w"], p["lru_conv_b"], "lru_conv")
    wab = jnp.concatenate([_block_diag(p["lru_wa"]), _block_diag(p["lru_wx"])], axis=1).astype(MXU)
    pre = _mm(xc, wab, tm=1024, tn=1024, tk=512, name="mm_lru")
    lpar = jnp.concatenate([p["lru_ba"].reshape(1, -1), p["lru_bx"].reshape(1, -1), p["lru_lambda"].reshape(1, -1),
                            jnp.zeros((5, LRU_W), F32)], axis=0)
    lru, hs = _lru_scan(pre, xc, proj, lpar, "lru_scan")
    mix = jnp.concatenate([attb, ssd, lru], axis=1)
    saved = dict(x=x, h=h, proj=proj, att=att, lse=lse, xconv=xconv, dt=dt, spar=spar, y=y, states=states, xc=xc, wab=wab,
                 pre=pre, lpar=lpar, hs=hs, mix=mix)
    return mix, saved, got


def _layer_ffn(x, mix, p, saved, comm=None, next_norm=None, comm_down=None):
    x1, h2 = _mm(mix, p["w_out"], add=x, tm=1024, tn=1024, tk=1536, name="mm_out", epi=_epi_rms(p["norm_ffn"]))
    gu = _mm(h2, p["w_gu_t"], tb=True, out_dtype=MXU, tm=1024, tn=1408, tk=1024, name="mm_gu", comm=comm)
    gu, got = gu if comm is not None else (gu, [])
    act = _swiglu_act(gu, "swiglu_act")
    x2 = _mm(act, p["w_down"], add=x1, tm=1024, tn=1024, tk=2816, name="mm_down", comm=comm_down,
             epi=None if next_norm is None else _epi_rms(next_norm))
    x2, got_down = x2 if comm_down is not None else (x2, [])
    x2, h_next = x2 if next_norm is not None else (x2, None)
    saved.update(x1=x1, h2=h2, gu=gu, act=act)
    return x2, got + got_down, h_next


def _layer_bwd(dx2, p, sv, comm_ssd=None, comm_att=None, comm_tail=None):
    g = {}
    da = _mm(dx2, p["w_down"], tb=True, out_dtype=MXU, tm=1024, tn=1408, tk=1024, name="mm_d_act")
    g["w_down"], g["w_down@wire"] = _mm(sv["act"], dx2, ta=True, tm=1408, tn=1024, tk=1024, name="mm_g_down", epi=_epi_wire(D_MODEL))
    dgu = _swiglu_bwd(sv["gu"], da, "swiglu_bwd")
    dx1, gn = _mm(dgu, p["w_gu_t"], tm=1024, tn=1024, tk=1408, name="mm_d_h2", epi=_epi_rms_bwd(sv["x1"], p["norm_ffn"], dx2))
    g["w_gu_t"], g["w_gu_t@wire"] = _mm(dgu, sv["h2"], ta=True, tm=1408, tn=1024, tk=1024, name="mm_g_gu", epi=_epi_wire(D_MODEL))
    g["norm_ffn"] = jnp.sum(gn, axis=0)
    dmix, stats = _mm(dx1, p["w_out"], tb=True, tm=1024, tn=1536, tk=1024, name="mm_d_mix", epi=_epi_att_stats(sv["att"], sv["lse"]))
    g["w_out"], g["w_out@wire"] = _mm(sv["mix"], dx1, ta=True, tm=1536, tn=1024, tk=1024, name="mm_g_out", epi=_epi_wire(D_MODEL))
    proj = sv["proj"]
    dpre, dxc_u, dgl, dlpar = _lru_scan_bwd(sv["pre"], sv["xc"], proj, sv["lpar"], sv["hs"], dmix, "lru_scan_bwd")
    dxc = _mm(dpre, sv["wab"], tb=True, add=dxc_u, tm=1024, tn=512, tk=1024, name="mm_d_xc")
    gwab = _mm(sv["xc"], dpre, ta=True, tm=512, tn=1024, tk=1024, name="mm_g_lru")
    g["lru_wa"], g["lru_wx"] = _block_diag_extract(gwab[:, :LRU_W]), _block_diag_extract(gwab[:, LRU_W:])
    g["lru_ba"], g["lru_bx"], g["lru_lambda"] = dlpar[0], dlpar[1], dlpar[2]
    dxl, gcw, gcb = _lru_conv_bwd(proj, dxc, p["lru_conv_w"], "lru_conv_bwd")
    g["lru_conv_w"], g["lru_conv_b"] = gcw[:CONV_K], jnp.sum(gcb, axis=0)
    dy, dz, gsn = _ssd_post_bwd(sv["y"], proj, p["ssd_norm"], (dmix, SSD_W, 1), "ssd_post_bwd")
    g["ssd_norm"] = jnp.sum(gsn, axis=0)
    dxconv, ddt, dal, ddk, got_ssd = _ssd_scan_bwd(sv["xconv"], sv["dt"], sv["spar"], sv["states"], dy, "ssd_scan_bwd", comm_ssd)
    g["ssd_a_log"], g["ssd_d"] = dal[0, :8], ddk[0, :8]
    dxbc, ddtr, gsw, gsb, gdb = _ssd_pre_bwd(proj, dxconv, ddt, p["ssd_conv_w"], p["ssd_conv_b"],
                                             _lanes128(p["ssd_dt_bias"]), "ssd_pre_bwd")
    g["ssd_conv_w"], g["ssd_conv_b"], g["ssd_dt_bias"] = gsw[:CONV_K], jnp.sum(gsb, axis=0), jnp.sum(gdb, axis=0)[:8]
    dq, dk, dv, got_att = _att_bwd_rev(proj, dmix, stats, "att_bwd", None if comm_att is None else comm_att(g))
    dproj = jnp.concatenate([dq, dk, dv, dz, dxbc, dgl, dxl, ddtr], axis=1)
    g["w_in_t"], g["w_in_t@wire"] = _mm(dproj, sv["h"], ta=True, tm=1408, tn=1024, tk=1024, name="mm_g_in", epi=_epi_wire(D_MODEL))
    res = _mm(dproj, p["w_in_t"], tm=1024, tn=1024, tk=1408, name="mm_d_h", comm=None if comm_tail is None else comm_tail(g),
              epi=_epi_rms_bwd(sv["x"], p["norm_mix"], dx1))
    (dx, gm), got_tail = res if comm_tail is not None else (res, [])
    g["norm_mix"] = jnp.sum(gm, axis=0)
    return dx, g, got_ssd, got_att, got_tail


def _grad_slabs(g, names, suffix=""):
    out = {}
    for n in names:
        if n == "w_in":
            out[n] = _w_in_slabs(g["w_in_t" + suffix])
        elif n == "w_gate":
            out[n] = _slabs(n, g["w_gu_t" + suffix][:D_FF])
        elif n == "w_up":
            out[n] = _slabs(n, g["w_gu_t" + suffix][D_FF:])
        else:
            out[n] = _slabs(n, g[n + suffix])
    return out


def kernel(x, norm_mix, w_in, ssd_conv_w, ssd_conv_b, ssd_dt_bias, ssd_a_log, ssd_d, ssd_norm, lru_conv_w, lru_conv_b, lru_wa, lru_ba, lru_wx, lru_bx, lru_lambda, w_out, norm_ffn, w_gate, w_up, w_down, norm_final, loss_target, m_norm_mix, m_w_in, m_ssd_conv_w, m_ssd_conv_b, m_ssd_dt_bias, m_ssd_a_log, m_ssd_d, m_ssd_norm, m_lru_conv_w, m_lru_conv_b, m_lru_wa, m_lru_ba, m_lru_wx, m_lru_bx, m_lru_lambda, m_w_out, m_norm_ffn, m_w_gate, m_w_up, m_w_down, m_norm_final, v_norm_mix, v_w_in, v_ssd_conv_w, v_ssd_conv_b, v_ssd_dt_bias, v_ssd_a_log, v_ssd_d, v_ssd_norm, v_lru_conv_w, v_lru_conv_b, v_lru_wa, v_lru_ba, v_lru_wx, v_lru_bx, v_lru_lambda, v_w_out, v_norm_ffn, v_w_gate, v_w_up, v_w_down, v_norm_final):
    loc = dict(locals())
    w = {n: loc[n] for n in WEIGHTS}
    m = {n: loc["m_" + n] for n in WEIGHTS}
    v = {n: loc["v_" + n] for n in WEIGHTS}
    for n in TRANSPOSED:
        w[n], m[n], v[n] = [jnp.transpose(t, (0, 2, 1)) for t in (w[n], m[n], v[n])]
    wt_in, mt_in, vt_in = [jnp.transpose(t, (2, 0, 1)) for t in (w["w_in"], m["w_in"], v["w_in"])]

    def halves(a):
        return a.reshape(a.shape[0], 2, a.shape[1] // 2, a.shape[2])

    def unhalve(a):
        return a.reshape(4, 2 * a.shape[2], a.shape[3])

    def joined(name, a):
        return _w_in_rows(unhalve(a)) if name == "w_in" else _join(name, unhalve(a))

    wb = {n: halves(w[n].astype(MXU)) for n in MATS[1:]}
    wb["w_in"] = halves(jnp.pad(jnp.transpose(wt_in.astype(MXU), (1, 0, 2)), ((0, 0), (0, W_IN_PAD - W_IN_SHARD), (0, 0))))
    xs = x[0]
    h0, first = _rms_fwd(xs, norm_mix[0], "rms_mix", _Comm(gathers=[(wb["w_in"], 0, True), (w["ssd_conv_w"], None, False),
                                                                    (w["lru_conv_w"], None, False)]))
    convs = {"ssd_conv_w": _join("ssd_conv_w", first[1]), "lru_conv_w": _join("lru_conv_w", first[2])}
    behind_att = [(n, 0) for n in MATS[1:]] + [("w_in", 1)]
    behind_ffn = [(n, 1) for n in MATS[1:]]
    whole = {("w_in", 0): joined("w_in", first[0])}
    params = {}

    def layer_params(l):
        if l not in params:
            p = {n: w[n][l] for n in SMALL if n != "norm_final"}
            p.update(w_in_t=whole["w_in", l], ssd_conv_w=convs["ssd_conv_w"][l], lru_conv_w=convs["lru_conv_w"][l])
            params[l] = p
        if "w_out" not in params[l] and ("w_out", l) in whole:
            params[l].update(w_out=whole["w_out", l], w_down=whole["w_down", l],
                             w_gu_t=jnp.concatenate([whole["w_gate", l], whole["w_up", l]], axis=0))
        return params[l]

    saved = []
    h_in = h0
    for l in range(DEPTH):
        first_layer = l == 0
        mix, sv, got = _layer_mixers(xs, layer_params(l), _Comm(gathers=[(wb[n], k, True) for n, k in behind_att]) if first_layer else None,
                                     h_in)
        whole.update({k: joined(k[0], a) for k, a in zip(behind_att, got)})
        xs, got, h_in = _layer_ffn(xs, mix, layer_params(l), sv, _Comm(gathers=[(wb[n], k, True) for n, k in behind_ffn[:-1]]) if first_layer else None,
                                   norm_mix[l + 1] if l + 1 < DEPTH else None,
                                   _Comm(gathers=[(wb[n], k, True) for n, k in behind_ffn[-1:]]) if first_layer else None)
        whole.update({k: joined(k[0], a) for k, a in zip(behind_ffn, got)})
        saved.append(sv)
    dx, gnf, lsum = _loss_head(xs, norm_final, loss_target[0], "loss_head")
    loss = lax.psum(jnp.sum(lsum), ("x", "y", "c"))

    dx, g1, _, _, _ = _layer_bwd(dx, layer_params(1), saved[1])
    def slabs_of(g, names):
        own = _grad_slabs(g, names)
        sent = _grad_slabs(g, [n for n in names if n in MATS], "@wire")
        sent.update({n: own[n] for n in names if n not in MATS})
        return own, sent

    s1, sent1 = slabs_of(g1, BIG)
    att0 = ("w_gate", "w_up", "w_down", "w_out")
    s0, sent0 = {}, {}

    def add0(g0, names):
        own, sent = slabs_of(g0, names)
        s0.update(own)
        sent0.update(sent)

    ssd1 = ("w_gate", "w_up")
    att1 = tuple(n for n in BIG if n not in ssd1)

    def comm_att(g0):
        add0(g0, att0)
        return _Comm(scatters=[sent1[n] for n in att1] + [sent0[n] for n in att0])

    tail0 = ("w_in",) + CONVS

    def comm_tail(g0):
        add0(g0, tail0)
        return _Comm(scatters=[sent0[n] for n in tail0])

    dx, g0, got_ssd, got_att, got_tail = _layer_bwd(dx, layer_params(0), saved[0], _Comm(scatters=[sent1[n] for n in ssd1]),
                                                    comm_att, comm_tail)
    recv = {(n, 1): a for n, a in zip(ssd1, got_ssd)}
    recv.update({(n, 1): a for n, a in zip(att1, got_att[:len(att1)])})
    recv.update({(n, 0): a for n, a in zip(att0, got_att[len(att1):])})
    recv.update({(n, 0): a for n, a in zip(tail0, got_tail)})

    me = 2 * lax.axis_index("x") + lax.axis_index("y")
    slabs = (s0, s1)
    part = {}
    for n in BIG:
        per_layer = []
        for l in range(DEPTH):
            own = lax.dynamic_index_in_dim(slabs[l][n], me, axis=0, keepdims=False)
            per_layer.append(_sum_slots(own, recv[n, l], "sum_chips_" + n, ROW_TILE.get(n, own.shape[0])))
        part[n] = jnp.stack(per_layer, axis=0)
    sib = dict(zip(BIG, _swap_sibling([part[n] for n in BIG])))
    out_g, out_d, out_m, out_v = {}, {}, {}, {}
    for n in BIG:
        if n == "w_in":
            res = _adamw(wt_in, mt_in, vt_in, jnp.transpose(part[n], (1, 0, 2)), jnp.transpose(sib[n], (1, 0, 2)), "adamw_" + n,
                         W_IN_ADAM_TILE, rows_first=True)
            out_g[n], out_d[n], out_m[n], out_v[n] = [jnp.transpose(t, (1, 2, 0)) for t in res]
            continue
        res = _adamw(w[n], m[n], v[n], part[n], sib[n], "adamw_" + n, ROW_TILE.get(n, w[n].shape[1]))
        out_g[n], out_d[n], out_m[n], out_v[n] = [jnp.transpose(t, (0, 2, 1)) for t in res] if n in TRANSPOSED else res

    gsm = {n: jnp.stack([g0[n], g1[n]], axis=0) for n in SMALL if n != "norm_final"}
    gsm["norm_final"] = jnp.sum(gnf, axis=0)
    small_shapes = [w[n].shape for n in SMALL]
    gs = _pack([gsm[n].reshape(w[n].shape) for n in SMALL], BLK, 8, F32)
    gall = _gather_small(gs)
    gsum = _sum_slots(None, gall, "sum_devices", gs.shape[0])
    ws = _pack([w[n] for n in SMALL], BLK, 8, F32)
    ms = _pack([m[n] for n in SMALL], BLK, 8, F32)
    vs = _pack([v[n] for n in SMALL], BLK, 8, F32)
    gsr, dsr, nms, nvs = _adamw(ws, ms, vs, gsum, None, "adamw_small", gs.shape[0])
    out_g.update(zip(SMALL, _unpack(gsr, small_shapes)))
    out_d.update(zip(SMALL, _unpack(dsr, small_shapes)))
    out_m.update(zip(SMALL, _unpack(nms, small_shapes)))
    out_v.update(zip(SMALL, _unpack(nvs, small_shapes)))

    return (loss, dx[None], *[out_g[n] for n in WEIGHTS], *[out_d[n] for n in WEIGHTS],
            *[out_m[n] for n in WEIGHTS], *[out_v[n] for n in WEIGHTS])
```

```python
import functools
import math

import jax
import jax.numpy as jnp
import numpy as np
from jax import lax
from jax.experimental import pallas as pl
from jax.experimental.pallas import tpu as pltpu

F32 = jnp.float32
MXU = jnp.bfloat16
HI = lax.Precision.HIGHEST
HIGH = lax.Precision.HIGH
MESH = pl.DeviceIdType.MESH

D_MODEL = 1024
DEPTH = 2
HEAD_DIM = 64
ATT_W = 512
ATT_PATTERNS = ((128, 1), (512, 4), (2048, 16))
BLK = 128
SSD_W = 512
SSD_STATE = 128
LRU_W = 512
LRU_BLOCKS = 8
LRU_C = 8.0
CONV_K = 4
D_MIX = 1536
D_FF = 2816
IN_COLS = 4104
NP = 4224
NORM_EPS = 1e-6
SSD_NORM_EPS = 1e-5
LN2 = math.log(2.0)
NEG = -1e30

ADAM_LR, ADAM_B1, ADAM_B2, ADAM_EPS, ADAM_WD, ADAM_STEP = 0.001, 0.9, 0.999, 1e-08, 0.01, 10
BC1 = 1.0 - ADAM_B1 ** ADAM_STEP
BC2 = 1.0 - ADAM_B2 ** ADAM_STEP

VMEM_LIMIT = 56 * 1024 * 1024

C_Q, C_K, C_V, C_Z, C_XBC, C_G, C_XL, C_DT = 0, 512, 1024, 1536, 2048, 3072, 3584, 4096


def _cp(*sem):
    return pltpu.CompilerParams(dimension_semantics=sem, vmem_limit_bytes=VMEM_LIMIT)


def _dot(a, b, dims, prec=None):
    return lax.dot_general(a, b, (dims, ((), ())), preferred_element_type=F32, precision=prec)


def _nn(a, b, prec=None):
    return _dot(a, b, ((1,), (0,)), prec)


def _nt(a, b, prec=None):
    return _dot(a, b, ((1,), (1,)), prec)


def _tn(a, b, prec=None):
    return _dot(a, b, ((0,), (0,)), prec)


def _sigmoid(x):
    return jax.nn.sigmoid(x)


def _silu(x):
    return x * _sigmoid(x)


def _softplus(x):
    return jnp.maximum(x, 0.0) + jnp.log(1.0 + jnp.exp(-jnp.abs(x)))


def _gelu(x):
    return 0.5 * x * (1.0 + jnp.tanh(0.7978845608028654 * (x + 0.044715 * x * x * x)))


def _mm(a, b, *, ta=False, tb=False, add=None, out_dtype=F32, tm, tn, tk, name, comm=None, epi=None):
    m, k = (a.shape[1], a.shape[0]) if ta else a.shape
    n = b.shape[0] if tb else b.shape[1]
    assert (b.shape[1] if tb else b.shape[0]) == k
    assert m % tm == 0 and n % tn == 0 and k % tk == 0, (name, m, n, k)
    nk = k // tk
    a_spec = pl.BlockSpec((tk, tm), lambda i, j, kk: (kk, i)) if ta else pl.BlockSpec((tm, tk), lambda i, j, kk: (i, kk))
    b_spec = pl.BlockSpec((tn, tk), lambda i, j, kk: (j, kk)) if tb else pl.BlockSpec((tk, tn), lambda i, j, kk: (kk, j))
    o_spec = pl.BlockSpec((tm, tn), lambda i, j, kk: (i, j))
    dims = ((0 if ta else 1,), (1 if tb else 0,))
    carried = comm is not None
    comm = comm or _Comm()
    ni, nj = m // tm, n // tn
    efn, erows, econsts, eouts, eaccs = epi or (None, [], [], [], [])
    assert epi is None or nj == 1
    nadd = 0 if add is None else 1
    ner, nec, neo, nea = len(erows), len(econsts), len(eouts), len(eaccs)

    def body(*refs):
        refs, cm = comm.split(refs, 2 + nadd + ner + nec, 1 + neo + nea, 1)
        a_ref, b_ref = refs[:2]
        er_refs = refs[2 + nadd:2 + nadd + ner]
        ec_refs = refs[2 + nadd + ner:2 + nadd + ner + nec]
        o_ref = refs[2 + nadd + ner + nec]
        eo_refs = refs[3 + nadd + ner + nec:3 + nadd + ner + nec + neo]
        ea_refs = refs[3 + nadd + ner + nec + neo:3 + nadd + ner + nec + neo + nea]
        acc = refs[-1]
        i, j, kk = pl.program_id(0), pl.program_id(1), pl.program_id(2)
        comm.start_at((i == 0) & (j == 0) & (kk == 0), cm)

        @pl.when(kk == 0)
        def _():
            acc[...] = jnp.zeros_like(acc)

        acc[...] += _dot(a_ref[...].astype(MXU), b_ref[...].astype(MXU), dims)

        @pl.when(kk == nk - 1)
        def _():
            r = acc[...]
            if add is not None:
                r = r + refs[2][...]
            if efn is None:
                o_ref[...] = r.astype(out_dtype)
            else:
                main, extra, sums = efn(r, [t[...] for t in er_refs], [t[...] for t in ec_refs])
                o_ref[...] = main.astype(out_dtype)
                for t, val in zip(eo_refs, extra):
                    t[...] = val.astype(t.dtype)
                @pl.when(i == 0)
                def _():
                    for t, val in zip(ea_refs, sums):
                        t[...] = val

                @pl.when(i > 0)
                def _():
                    for t, val in zip(ea_refs, sums):
                        t[...] += val

        comm.wait_at((i == ni - 1) & (j == nj - 1) & (kk == nk - 1), cm)

    def whole_rows(width):
        return pl.BlockSpec((tm, width), lambda i, j, kk: (i, 0))

    ins = [a, b] + ([] if add is None else [add]) + list(erows) + list(econsts)
    specs = [a_spec, b_spec] + ([] if add is None else [o_spec]) + [whole_rows(t.shape[1]) for t in erows]
    specs += [pl.BlockSpec(t.shape, lambda i, j, kk: (0, 0)) for t in econsts]
    out_specs = [o_spec] + [whole_rows(wd) for wd, _ in eouts] + [pl.BlockSpec((r, wd), lambda i, j, kk: (0, 0)) for r, wd in eaccs]
    out_shape = [jax.ShapeDtypeStruct((m, n), out_dtype)] + [jax.ShapeDtypeStruct((m, wd), dt) for wd, dt in eouts]
    out_shape += [jax.ShapeDtypeStruct((r, wd), F32) for r, wd in eaccs]
    serial = comm.n or nea
    res = pl.pallas_call(
        body, name=name, grid=(ni, nj, nk), in_specs=specs + [ANY] * comm.n, out_specs=out_specs + [ANY] * comm.n,
        out_shape=out_shape + comm.out_shape(),
        scratch_shapes=[pltpu.VMEM((tm, tn), F32)] + comm.scratch(),
        compiler_params=_cp(*((["arbitrary"] * 3) if serial else ["parallel", "parallel", "arbitrary"])),
    )(*ins, *comm.args())
    nown = 1 + neo + nea
    own = res[0] if epi is None else list(res[:nown])
    return (own, list(res[nown:])) if carried else own


def _rows(fn, rows, consts=(), outs=(), accs=(), *, tile, name, halos=(), comm=None):
    rows = [r if isinstance(r, tuple) else (r, r.shape[1], 0) for r in rows]
    s = rows[0][0].shape[0]
    assert s % tile == 0 and tile % 8 == 0
    n = s // tile
    t8 = tile // 8
    nr, nh, nc_, no, na = len(rows), len(halos), len(consts), len(outs), len(accs)
    carried = comm is not None
    comm = comm or _Comm()

    def body(*refs):
        refs, cm = comm.split(refs, nr + nh + nc_, no + na, 0)
        i = pl.program_id(0)
        comm.start_at(i == 0, cm)
        rv = [r[...] for r in refs[:nr]]
        hv = []
        for (idx, kind), r in zip(halos, refs[nr:nr + nh]):
            edge = (i == 0) if kind == "prev" else (i == n - 1)
            hv.append(jnp.where(edge, 0.0, r[...]))
        cv = [r[...] for r in refs[nr + nh:nr + nh + nc_]]
        o_refs = refs[nr + nh + nc_:nr + nh + nc_ + no]
        a_refs = refs[nr + nh + nc_ + no:]
        ov, av = fn(rv, hv, cv)
        for r, v in zip(o_refs, ov):
            r[...] = v.astype(r.dtype)
        if na:
            @pl.when(i == 0)
            def _():
                for r in a_refs:
                    r[...] = jnp.zeros_like(r)
            for r, v in zip(a_refs, av):
                r[...] += v
        comm.wait_at(i == n - 1, cm)

    in_specs = [pl.BlockSpec((tile, w), functools.partial(lambda i, cb: (i, cb), cb=cb)) for (_, w, cb) in rows]
    for idx, kind in halos:
        _, w, cb = rows[idx]
        if kind == "prev":
            in_specs.append(pl.BlockSpec((8, w), functools.partial(lambda i, cb: (jnp.maximum(i * t8 - 1, 0), cb), cb=cb)))
        else:
            in_specs.append(pl.BlockSpec((8, w), functools.partial(lambda i, cb: (jnp.minimum((i + 1) * t8, n * t8 - 1), cb), cb=cb)))
    in_specs += [pl.BlockSpec(c.shape, functools.partial(lambda i, nd: (0,) * nd, nd=c.ndim)) for c in consts]
    out_specs = [pl.BlockSpec((tile, c), lambda i: (i, 0)) for (c, _) in outs]
    out_specs += [pl.BlockSpec((r, c), lambda i: (0, 0)) for (r, c) in accs]
    out_shape = [jax.ShapeDtypeStruct((s, c), dt) for (c, dt) in outs]
    out_shape += [jax.ShapeDtypeStruct((r, c), F32) for (r, c) in accs]
    args = [r[0] for r in rows] + [rows[idx][0] for idx, _ in halos] + list(consts)
    res = pl.pallas_call(
        body, name=name, grid=(n,), in_specs=in_specs + [ANY] * comm.n, out_specs=out_specs + [ANY] * comm.n,
        out_shape=out_shape + comm.out_shape(), scratch_shapes=comm.scratch(), compiler_params=_cp("arbitrary"),
    )(*args, *comm.args())
    return (list(res[:no + na]), list(res[no + na:])) if carried else list(res)


def _colsum8(v):
    t, c = v.shape
    return jnp.sum(v.reshape(t // 8, 8, c), axis=0)


def _rms(x, g):
    return x * lax.rsqrt(jnp.mean(x * x, axis=-1, keepdims=True) + NORM_EPS) * g


def _epi_rms(g):
    return (lambda r, rows, consts: (r, [_rms(r, consts[0])], []), [], [g.reshape(1, -1)], [(g.shape[-1], MXU)], [])


def _epi_rms_bwd(x, g, dres):
    def fn(r, rows, consts):
        xb, drb = rows
        _, vjp = jax.vjp(_rms, xb, consts[0])
        rstd = lax.rsqrt(jnp.mean(xb * xb, axis=-1, keepdims=True) + NORM_EPS)
        return drb + vjp(r)[0], [], [_colsum8(r * xb * rstd)]
    return (fn, [x, dres], [g.reshape(1, -1)], [], [(8, g.shape[-1])])


def _epi_att_stats(att, lse):
    def fn(r, rows, consts):
        hr = lax.broadcasted_iota(jnp.int32, (ATT_W, ATT_W), 0) // HEAD_DIM
        hc = lax.broadcasted_iota(jnp.int32, (ATT_W, ATT_W), 1) // HEAD_DIM
        delta = _nn(r[:, :ATT_W] * rows[0], (hr == hc).astype(F32), HIGH)
        lane = lax.broadcasted_iota(jnp.int32, delta.shape, 1)
        return r, [jnp.where(lane % HEAD_DIM < HEAD_DIM // 2, rows[1], delta)], []
    return (fn, [att, lse], [], [(ATT_W, F32)], [])


def _epi_wire(width):
    return (lambda r, rows, consts: (r, [r], []), [], [], [(width, MXU)], [])


def _rms_fwd(x, g, name, comm=None):
    def fn(rv, hv, cv):
        return [_rms(rv[0], cv[0])], []
    res = _rows(fn, [x], [g.reshape(1, -1)], [(x.shape[1], MXU)], tile=512, name=name, comm=comm)
    return res[0] if comm is None else (res[0][0], res[1])


def _slope_dist(hp, hh, dist, dil):
    hf = (2 * hp + hh + 1).astype(F32)
    slope = jnp.exp(jnp.zeros(dist.shape, F32) - hf * LN2)
    return slope * (dist.astype(F32) * float(dil))


ATT_G = 2048


def _att_fwd_fused(proj, name, comm=None):
    s, npc = proj.shape
    gsz = ATT_G
    ng = s // gsz
    assert s % gsz == 0
    scale = HEAD_DIM ** -0.5
    comm = comm or _Comm()

    def body(*refs):
        (q_ref, kp_ref, kc_ref, vp_ref, vc_ref, att_ref, lse_ref, attb_ref, nn, mn, dn), cm = comm.split(refs, 5, 3, 3)
        hp, g = pl.program_id(0), pl.program_id(1)
        comm.start_at((hp == 0) & (g == 0), cm)
        lane = lax.broadcasted_iota(jnp.int32, (BLK, BLK), 1)
        qi = lax.broadcasted_iota(jnp.int32, (BLK, 2 * BLK), 0)
        ki = lax.broadcasted_iota(jnp.int32, (BLK, 2 * BLK), 1)
        dist = BLK + qi - ki
        band = (dist >= 0) & (dist <= BLK)
        for pi, (_, dil) in enumerate(ATT_PATTERNS):
            nbg = gsz // dil // BLK
            bias = [_slope_dist(hp, hh, dist, dil) for hh in (0, 1)]
            for r in range(dil):
                for b in range(nbg):
                    def rows(blk):
                        return pl.ds(blk * BLK * dil + r, BLK, stride=dil) if dil > 1 else pl.ds(blk * BLK, BLK)
                    q = q_ref[rows(b), :]
                    k_prev = kp_ref[rows(nbg - 1), :] if b == 0 else kc_ref[rows(b - 1), :]
                    v_prev = vp_ref[rows(nbg - 1), :] if b == 0 else vc_ref[rows(b - 1), :]
                    kk = jnp.concatenate([k_prev, kc_ref[rows(b), :]], axis=0).astype(MXU)
                    vv = jnp.concatenate([v_prev, vc_ref[rows(b), :]], axis=0).astype(MXU)
                    valid = (band & ((g > 0) | (ki >= BLK))) if b == 0 else band
                    num = jnp.zeros((BLK, BLK), F32)
                    mx = jnp.zeros((BLK, BLK), F32)
                    den = jnp.zeros((BLK, BLK), F32)
                    for hh in (0, 1):
                        hmask = (lane < HEAD_DIM) if hh == 0 else (lane >= HEAD_DIM)
                        qm = jnp.where(hmask, q, 0.0).astype(MXU)
                        sc = jnp.where(valid, _nt(qm, kk) * scale - bias[hh], NEG)
                        m = jnp.max(sc, axis=1, keepdims=True)
                        p = jnp.exp(sc - m)
                        dn_ = jnp.sum(p, axis=1, keepdims=True)
                        o = _nn(p.astype(MXU), vv)
                        num = jnp.where(hmask, o, num)
                        mx = jnp.where(hmask, m, mx)
                        den = jnp.where(hmask, dn_, den)
                    nn.at[pi][rows(b), :] = num
                    mn.at[pi][rows(b), :] = mx
                    dn.at[pi][rows(b), :] = den

        def merge(c, carry):
            rows = pl.ds(pl.multiple_of(c * 256, 256), 256)
            ms = [mn[pi, rows, :] for pi in range(len(ATT_PATTERNS))]
            m_all = functools.reduce(jnp.maximum, ms)
            num = jnp.zeros((256, BLK), F32)
            den = jnp.zeros((256, BLK), F32)
            for pi in range(len(ATT_PATTERNS)):
                e = jnp.exp(ms[pi] - m_all)
                num = num + nn[pi, rows, :] * e
                den = den + dn[pi, rows, :] * e
            att = num / den
            att_ref[rows, :] = att
            attb_ref[rows, :] = att.astype(MXU)
            lse_ref[rows, :] = m_all + jnp.log(den)
            return carry

        lax.fori_loop(0, gsz // 256, merge, 0)
        comm.wait_at((hp == 3) & (g == ng - 1), cm)

    def cur(base):
        return pl.BlockSpec((gsz, BLK), lambda hp, g: (g, base // BLK + hp))

    def prev(base):
        return pl.BlockSpec((gsz, BLK), lambda hp, g: (jnp.maximum(g - 1, 0), base // BLK + hp))

    o_spec = pl.BlockSpec((gsz, BLK), lambda hp, g: (g, hp))
    npat = len(ATT_PATTERNS)
    res = pl.pallas_call(
        body, name=name, grid=(4, ng),
        in_specs=[cur(C_Q), prev(C_K), cur(C_K), prev(C_V), cur(C_V)] + [ANY] * comm.n,
        out_specs=[o_spec] * 3 + [ANY] * comm.n,
        out_shape=[jax.ShapeDtypeStruct((s, ATT_W), F32)] * 2 + [jax.ShapeDtypeStruct((s, ATT_W), MXU)] + comm.out_shape(),
        scratch_shapes=[pltpu.VMEM((npat, gsz, BLK), F32)] * 3 + comm.scratch(),
        compiler_params=_cp("arbitrary", "arbitrary"),
    )(proj, proj, proj, proj, proj, *comm.args())
    return res[0], res[1], res[2], list(res[3:])


def _att_bwd_rev(proj, datt, stats, name, comm=None):
    s, npc = proj.shape
    gsz = ATT_G
    ng = s // gsz
    npat = len(ATT_PATTERNS)
    scale = HEAD_DIM ** -0.5
    comm = comm or _Comm()

    def body(*refs):
        (q_ref, kp_ref, kc_ref, vp_ref, vc_ref, do_ref, st_ref, dq_out, dk_out, dv_out,
         kcar, vcar, dq_ref, dk_ref, dv_ref), cm = comm.split(refs, 7, 3, 5)
        hp, gi = pl.program_id(0), pl.program_id(1)
        g = ng - 1 - gi
        comm.start_at((hp == 0) & (gi == 0), cm)

        @pl.when(gi == 0)
        def _():
            kcar[...] = jnp.zeros_like(kcar)
            vcar[...] = jnp.zeros_like(vcar)

        lane = lax.broadcasted_iota(jnp.int32, (BLK, BLK), 1)
        qi = lax.broadcasted_iota(jnp.int32, (BLK, 2 * BLK), 0)
        ki = lax.broadcasted_iota(jnp.int32, (BLK, 2 * BLK), 1)
        dist = BLK + qi - ki
        band = (dist >= 0) & (dist <= BLK)
        for acc in (dq_ref, dk_ref, dv_ref):
            acc[...] = jnp.zeros_like(acc)
        for pi, (_, dil) in enumerate(ATT_PATTERNS):
            nbg = gsz // dil // BLK
            bias = [_slope_dist(hp, hh, dist, dil) for hh in (0, 1)]
            for r in range(dil):
                edge = slice(pi * gsz + r * BLK, pi * gsz + (r + 1) * BLK)
                for b in reversed(range(nbg)):
                    def rows(blk):
                        return pl.ds(blk * BLK * dil + r, BLK, stride=dil) if dil > 1 else pl.ds(blk * BLK, BLK)
                    q, do, st = q_ref[rows(b), :], do_ref[rows(b), :], st_ref[rows(b), :]
                    k_prev = kp_ref[rows(nbg - 1), :] if b == 0 else kc_ref[rows(b - 1), :]
                    v_prev = vp_ref[rows(nbg - 1), :] if b == 0 else vc_ref[rows(b - 1), :]
                    kk = jnp.concatenate([k_prev, kc_ref[rows(b), :]], axis=0).astype(MXU)
                    vv = jnp.concatenate([v_prev, vc_ref[rows(b), :]], axis=0).astype(MXU)
                    valid = (band & ((g > 0) | (ki >= BLK))) if b == 0 else band
                    dq = jnp.zeros((BLK, BLK), F32)
                    dkk = jnp.zeros((2 * BLK, BLK), F32)
                    dvv = jnp.zeros((2 * BLK, BLK), F32)
                    for hh in (0, 1):
                        c0 = hh * HEAD_DIM
                        hmask = (lane < HEAD_DIM) if hh == 0 else (lane >= HEAD_DIM)
                        qm = jnp.where(hmask, q, 0.0).astype(MXU)
                        dom = jnp.where(hmask, do, 0.0).astype(MXU)
                        sc = _nt(qm, kk) * scale - bias[hh]
                        p = jnp.exp(jnp.where(valid, sc - st[:, c0:c0 + 1], NEG))
                        ds = (p * (_nt(dom, vv) - st[:, c0 + HEAD_DIM // 2:c0 + HEAD_DIM // 2 + 1])).astype(MXU)
                        dq = jnp.where(hmask, _nn(ds, kk), dq)
                        dkk = dkk + _tn(ds, qm)
                        dvv = dvv + _tn(p.astype(MXU), dom)
                    dq_ref[rows(b), :] += dq * scale
                    own_k, own_v = dkk[BLK:] * scale, dvv[BLK:]
                    if b == nbg - 1:
                        own_k, own_v = own_k + kcar[edge, :], own_v + vcar[edge, :]
                    dk_ref[rows(b), :] += own_k
                    dv_ref[rows(b), :] += own_v
                    if b > 0:
                        dk_ref[rows(b - 1), :] += dkk[:BLK] * scale
                        dv_ref[rows(b - 1), :] += dvv[:BLK]
                    else:
                        kcar[edge, :] = dkk[:BLK] * scale
                        vcar[edge, :] = dvv[:BLK]
        for out, acc in ((dq_out, dq_ref), (dk_out, dk_ref), (dv_out, dv_ref)):
            out[...] = acc[...].astype(out.dtype)
        comm.wait_at((hp == 3) & (gi == ng - 1), cm)

    def pspec(base, shift):
        return pl.BlockSpec((gsz, BLK), lambda hp, gi: (jnp.maximum(ng - 1 - gi + shift, 0), base // BLK + hp))

    wspec = pl.BlockSpec((gsz, BLK), lambda hp, gi: (ng - 1 - gi, hp))
    in_specs = [pspec(C_Q, 0), pspec(C_K, -1), pspec(C_K, 0), pspec(C_V, -1), pspec(C_V, 0), wspec, wspec] + [ANY] * comm.n
    res = pl.pallas_call(
        body, name=name, grid=(4, ng), in_specs=in_specs,
        out_specs=[wspec] * 3 + [ANY] * comm.n,
        out_shape=[jax.ShapeDtypeStruct((s, ATT_W), MXU)] * 3 + comm.out_shape(),
        scratch_shapes=[pltpu.VMEM((npat * gsz, BLK), F32)] * 2 + [pltpu.VMEM((gsz, BLK), F32)] * 3 + comm.scratch(),
        compiler_params=_cp("arbitrary", "arbitrary"),
    )(proj, proj, proj, proj, proj, datt, stats, *comm.args())
    return res[0], res[1], res[2], list(res[3:])


def _shift_down(cur, halo, sft):
    if sft == 0:
        return cur
    t = cur.shape[0]
    rolled = pltpu.roll(cur, sft, 0)
    hr = pltpu.roll(halo, sft, 0)
    row = lax.broadcasted_iota(jnp.int32, cur.shape, 0)
    return jnp.where(row < sft, jnp.tile(hr, (t // 8, 1)), rolled)


def _shift_up(cur, halo, sft):
    if sft == 0:
        return cur
    t = cur.shape[0]
    rolled = pltpu.roll(cur, t - sft, 0)
    hr = pltpu.roll(halo, 8 - sft, 0)
    row = lax.broadcasted_iota(jnp.int32, cur.shape, 0)
    return jnp.where(row >= t - sft, jnp.tile(hr, (t // 8, 1)), rolled)


def _conv(x, xh, w, b):
    y = b + x * w[CONV_K - 1:CONV_K]
    for k in range(CONV_K - 1):
        y = y + _shift_down(x, xh, CONV_K - 1 - k) * w[k:k + 1]
    return y


def _conv_bwd(x, xh, dy, dyh, w):
    dx = dy * w[CONV_K - 1:CONV_K]
    dws = []
    for k in range(CONV_K - 1):
        sft = CONV_K - 1 - k
        dx = dx + _shift_up(dy, dyh, sft) * w[k:k + 1]
        dws.append(jnp.sum(dy * _shift_down(x, xh, sft), axis=0, keepdims=True))
    dws.append(jnp.sum(dy * x, axis=0, keepdims=True))
    c = x.shape[1]
    dw = jnp.concatenate(dws + [jnp.zeros((8 - CONV_K, c), F32)], axis=0)
    return dx, dw, jnp.sum(dy, axis=0, keepdims=True)


def _pad8(w):
    return jnp.concatenate([w, jnp.zeros((8 - w.shape[0], w.shape[1]), w.dtype)], axis=0)


def _ssd_pre(proj, conv_w, conv_b, dt_bias128, name):
    def fn(rv, hv, cv):
        xbc, dtr = rv
        return [_silu(_conv(xbc, hv[0], cv[0], cv[1])), _softplus(dtr + cv[2])], []
    return _rows(fn, [(proj, 1024, C_XBC // 1024), (proj, BLK, C_DT // BLK)],
                 [_pad8(conv_w), conv_b.reshape(1, -1), dt_bias128],
                 [(1024, F32), (BLK, F32)], tile=256, name=name, halos=[(0, "prev")])


def _ssd_pre_bwd(proj, dxc, ddt, conv_w, conv_b, dt_bias128, name):
    def fn(rv, hv, cv):
        xbc, dtr, dxcb, ddtb = rv
        xh, dxch_raw, xnext = hv
        w, b, bias = cv
        pre = _conv(xbc, xh, w, b)
        sg = _sigmoid(pre)
        dpre = dxcb * (sg * (1.0 + pre * (1.0 - sg)))
        t = xbc.shape[0]
        tail = jnp.concatenate([xbc[t - 8:], xnext], axis=0)
        pre_n = _conv(tail[8:], tail[:8], w, b)
        sgn = _sigmoid(pre_n)
        dpre_h = dxch_raw * (sgn * (1.0 + pre_n * (1.0 - sgn)))
        dx, dw, db = _conv_bwd(xbc, xh, dpre, dpre_h, w)
        ddr = ddtb * _sigmoid(dtr + bias)
        return [dx, ddr], [dw, jnp.concatenate([db, jnp.zeros((7, db.shape[1]), F32)], axis=0), _colsum8(ddr)]
    return _rows(fn, [(proj, 1024, C_XBC // 1024), (proj, BLK, C_DT // BLK), dxc, ddt],
                 [_pad8(conv_w), conv_b.reshape(1, -1), dt_bias128],
                 [(1024, MXU), (BLK, MXU)], [(8, 1024), (8, 1024), (8, BLK)], tile=256, name=name,
                 halos=[(0, "prev"), (2, "next"), (0, "next")])


SSD_CPB = 1


def _head_cols(v, h0):
    lane = lax.broadcasted_iota(jnp.int32, (v.shape[0], BLK), 1)
    return jnp.where(lane < HEAD_DIM, v[:, h0:h0 + 1], v[:, h0 + 1:h0 + 2])


def _ssd_scan(xc, dt, par, name):
    s = xc.shape[0]
    nc = s // BLK

    def body(x_ref, dt_ref, par_ref, y_ref, st_ref, h_ref):
        c = pl.program_id(0)

        @pl.when(c == 0)
        def _():
            h_ref[...] = jnp.zeros_like(h_ref)

        st_ref[0] = h_ref[...]
        dt = dt_ref[...]
        a_row = -jnp.exp(par_ref[0:1, :])
        d_row = par_ref[1:2, :]
        ri = lax.broadcasted_iota(jnp.int32, (BLK, BLK), 0)
        ci = lax.broadcasted_iota(jnp.int32, (BLK, BLK), 1)
        tril = ri >= ci
        cs = _nn(tril.astype(F32), dt * a_row, HI)
        cst, dtt = cs.T, dt.T
        last = cs[BLK - 1:BLK, :]
        wcol = jnp.exp(last - cs) * dt
        ecs = jnp.exp(cs)
        elast = jnp.exp(last)
        for g in (0, 1):
            bg = x_ref[:, 512 + g * BLK:512 + (g + 1) * BLK].astype(MXU)
            cg = x_ref[:, 768 + g * BLK:768 + (g + 1) * BLK].astype(MXU)
            gm = _nt(cg, bg)
            for pp in (0, 1):
                pr = 2 * g + pp
                h0 = 2 * pr
                x2 = x_ref[:, pr * BLK:(pr + 1) * BLK]
                hprev = h_ref[pr * BLK:(pr + 1) * BLK, :]
                yp = jnp.zeros((BLK, BLK), F32)
                for hh in (0, 1):
                    h = h0 + hh
                    hmask = (ci < HEAD_DIM) if hh == 0 else (ci >= HEAD_DIM)
                    lm = jnp.exp(jnp.where(tril, cs[:, h:h + 1] - cst[h:h + 1, :], NEG))
                    mm = gm * lm * dtt[h:h + 1, :]
                    yp = yp + _nn(mm.astype(MXU), jnp.where(hmask, x2, 0.0).astype(MXU))
                y0 = _nt(cg, hprev.astype(MXU))
                y_ref[:, pr * BLK:(pr + 1) * BLK] = yp + _head_cols(ecs, h0) * y0 + _head_cols(d_row, h0) * x2
                dec = jnp.where(ri < HEAD_DIM, elast[:, h0:h0 + 1], elast[:, h0 + 1:h0 + 2])
                xw = (x2 * _head_cols(wcol, h0)).astype(MXU)
                h_ref[pr * BLK:(pr + 1) * BLK, :] = dec * hprev + _tn(xw, bg)

    return pl.pallas_call(
        body, name=name, grid=(nc,),
        in_specs=[pl.BlockSpec((BLK, 1024), lambda c: (c, 0)), pl.BlockSpec((BLK, BLK), lambda c: (c, 0)),
                  pl.BlockSpec((8, BLK), lambda c: (0, 0))],
        out_specs=[pl.BlockSpec((BLK, SSD_W), lambda c: (c, 0)), pl.BlockSpec((1, SSD_W, SSD_STATE), lambda c: (c, 0, 0))],
        out_shape=[jax.ShapeDtypeStruct((s, SSD_W), F32), jax.ShapeDtypeStruct((nc, SSD_W, SSD_STATE), F32)],
        scratch_shapes=[pltpu.VMEM((SSD_W, SSD_STATE), F32)],
        compiler_params=_cp("arbitrary"),
    )(xc, dt, par)


def _ssd_scan_bwd(xc, dt, par, st, dy, name, comm=None):
    s = xc.shape[0]
    cpb = SSD_CPB
    nb = s // (cpb * BLK)
    comm = comm or _Comm()

    def chunk(x_ref, dt_ref, par_ref, st_ref, dy_ref, dx_ref, ddt_ref, dal_ref, dd_ref, dh_ref):
        dt = dt_ref[...]
        a_row = -jnp.exp(par_ref[0:1, :])
        d_row = par_ref[1:2, :]
        ri = lax.broadcasted_iota(jnp.int32, (BLK, BLK), 0)
        ci = lax.broadcasted_iota(jnp.int32, (BLK, BLK), 1)
        tril = ri >= ci
        cs = _nn(tril.astype(F32), dt * a_row, HI)
        cst, dtt = cs.T, dt.T
        last = cs[BLK - 1:BLK, :]
        tolast = jnp.exp(last - cs)
        wcol = tolast * dt
        ecs = jnp.exp(cs)
        elast = jnp.exp(last)
        dcs_col = jnp.zeros((BLK, BLK), F32)
        ddt_col = jnp.zeros((BLK, BLK), F32)
        dcs_row = jnp.zeros((BLK, BLK), F32)
        ddt_row = jnp.zeros((BLK, BLK), F32)
        dlast = jnp.zeros((1, BLK), F32)
        ddsk = jnp.zeros((1, BLK), F32)
        for g in (0, 1):
            bg32 = x_ref[:, 512 + g * BLK:512 + (g + 1) * BLK]
            cg32 = x_ref[:, 768 + g * BLK:768 + (g + 1) * BLK]
            bg, cg = bg32.astype(MXU), cg32.astype(MXU)
            gm = _nt(cg, bg)
            dgm = jnp.zeros((BLK, BLK), F32)
            dbg = jnp.zeros((BLK, BLK), F32)
            dcg = jnp.zeros((BLK, BLK), F32)
            for pp in (0, 1):
                pr = 2 * g + pp
                h0 = 2 * pr
                x2 = x_ref[:, pr * BLK:(pr + 1) * BLK]
                dy2 = dy_ref[:, pr * BLK:(pr + 1) * BLK]
                hprev = st_ref[0, pr * BLK:(pr + 1) * BLK, :]
                dhn = dh_ref[pr * BLK:(pr + 1) * BLK, :]
                x2m, dhnm = x2.astype(MXU), dhn.astype(MXU)
                zb = _nt(bg, dhnm)
                y0 = _nt(cg, hprev.astype(MXU))
                esel = _head_cols(ecs, h0)
                wsel = _head_cols(wcol, h0)
                dx2 = _head_cols(d_row, h0) * dy2 + wsel * zb
                pick2 = (((ri < HEAD_DIM) & (ci == h0)) | ((ri >= HEAD_DIM) & (ci == h0 + 1))).astype(F32)
                sums = _nn(jnp.concatenate([dy2 * y0, x2 * zb, dy2 * x2], axis=0), pick2, HIGH)
                de2, dw2, dd2 = sums[:BLK], sums[BLK:2 * BLK], sums[2 * BLK:]
                v2 = dw2 * wcol
                dcs_col = dcs_col + ecs * de2 - v2
                ddt_col = ddt_col + dw2 * tolast
                hsum = _nn(dhn * hprev, jnp.ones((BLK, BLK), F32), HIGH)
                dlast = dlast + elast * jnp.sum(jnp.where(pick2 > 0.0, hsum, 0.0), axis=0, keepdims=True) \
                    + jnp.sum(v2, axis=0, keepdims=True)
                ddsk = ddsk + jnp.sum(dd2, axis=0, keepdims=True)
                ts = []
                for hh in (0, 1):
                    h = h0 + hh
                    hmask = (ci < HEAD_DIM) if hh == 0 else (ci >= HEAD_DIM)
                    ons = (ri == h).astype(F32)
                    dym = jnp.where(hmask, dy2, 0.0).astype(MXU)
                    dt_r = dtt[h:h + 1, :]
                    lm = jnp.exp(jnp.where(tril, cs[:, h:h + 1] - cst[h:h + 1, :], NEG))
                    mm = gm * lm * dt_r
                    dx2 = dx2 + _tn(mm.astype(MXU), dym)
                    dm = _nt(dym, x2m)
                    t1 = dm * lm
                    dgm = dgm + t1 * dt_r
                    tt = t1 * gm
                    ddt_row = ddt_row + ons * jnp.sum(tt, axis=0, keepdims=True)
                    t = tt * dt_r
                    dcs_row = dcs_row - ons * jnp.sum(t, axis=0, keepdims=True)
                    ts.append(t)
                rows2 = lax.broadcasted_iota(jnp.int32, (2 * BLK, BLK), 0)
                lane2 = lax.broadcasted_iota(jnp.int32, (2 * BLK, BLK), 1)
                to_lane = ((rows2 < BLK) & (lane2 == h0)) | ((rows2 >= BLK) & (lane2 == h0 + 1))
                dcs_col = dcs_col + _nn(jnp.concatenate(ts, axis=1), to_lane.astype(F32), HIGH)
                dx_ref[:, pr * BLK:(pr + 1) * BLK] = dx2
                edy = (esel * dy2).astype(MXU)
                dcg = dcg + _nn(edy, hprev.astype(MXU))
                dec = jnp.where(ri < HEAD_DIM, elast[:, h0:h0 + 1], elast[:, h0 + 1:h0 + 2])
                dh_ref[pr * BLK:(pr + 1) * BLK, :] = dec * dhn + _tn(edy, cg)
                dbg = dbg + _nn((x2 * wsel).astype(MXU), dhnm)
            dgmm = dgm.astype(MXU)
            dx_ref[:, 512 + g * BLK:512 + (g + 1) * BLK] = dbg + _tn(dgmm, cg)
            dx_ref[:, 768 + g * BLK:768 + (g + 1) * BLK] = dcg + _nn(dgmm, bg)
        dcs = dcs_col + dcs_row.T + jnp.where(ri == BLK - 1, dlast, 0.0)
        dda = _nn((ri <= ci).astype(F32), dcs, HI)
        ddt_ref[...] = ddt_col + ddt_row.T + a_row * dda
        da = jnp.sum(dt * dda, axis=0, keepdims=True)
        dal_ref[0:1, :] += da * a_row
        dd_ref[0:1, :] += ddsk

    def body(*refs):
        (x_ref, dt_ref, par_ref, st_ref, dy_ref, dx_ref, ddt_ref, dal_ref, dd_ref, dh_ref), cm = comm.split(refs, 5, 4, 1)
        c = pl.program_id(0)
        comm.start_at(c == 0, cm)

        @pl.when(c == 0)
        def _():
            dh_ref[...] = jnp.zeros_like(dh_ref)
            dal_ref[...] = jnp.zeros_like(dal_ref)
            dd_ref[...] = jnp.zeros_like(dd_ref)

        for cc in reversed(range(cpb)):
            rows = pl.ds(cc * BLK, BLK)
            chunk(x_ref.at[rows], dt_ref.at[rows], par_ref, st_ref.at[pl.ds(cc, 1)], dy_ref.at[rows], dx_ref.at[rows],
                  ddt_ref.at[rows], dal_ref, dd_ref, dh_ref)
        comm.wait_at(c == nb - 1, cm)

    rev = lambda c: (nb - 1 - c, 0)
    tb = cpb * BLK
    res = pl.pallas_call(
        body, name=name, grid=(nb,),
        in_specs=[pl.BlockSpec((tb, 1024), rev), pl.BlockSpec((tb, BLK), rev), pl.BlockSpec((8, BLK), lambda c: (0, 0)),
                  pl.BlockSpec((cpb, SSD_W, SSD_STATE), lambda c: (nb - 1 - c, 0, 0)), pl.BlockSpec((tb, SSD_W), rev)]
        + [ANY] * comm.n,
        out_specs=[pl.BlockSpec((tb, 1024), rev), pl.BlockSpec((tb, BLK), rev),
                   pl.BlockSpec((8, BLK), lambda c: (0, 0)), pl.BlockSpec((8, BLK), lambda c: (0, 0))] + [ANY] * comm.n,
        out_shape=[jax.ShapeDtypeStruct((s, 1024), F32), jax.ShapeDtypeStruct((s, BLK), F32),
                   jax.ShapeDtypeStruct((8, BLK), F32), jax.ShapeDtypeStruct((8, BLK), F32)] + comm.out_shape(),
        scratch_shapes=[pltpu.VMEM((SSD_W, SSD_STATE), F32)] + comm.scratch(),
        compiler_params=_cp("arbitrary"),
    )(xc, dt, par, st, dy, *comm.args())
    return res[0], res[1], res[2], res[3], list(res[4:])


def _ssd_gate(y, z, w):
    t = y * _silu(z)
    outs = []
    for g in (0, 1):
        tg = t[:, g * 256:(g + 1) * 256]
        outs.append(tg * lax.rsqrt(jnp.mean(tg * tg, axis=-1, keepdims=True) + SSD_NORM_EPS))
    return jnp.concatenate(outs, axis=1) * w


def _ssd_post(y, proj, norm_w, name):
    def fn(rv, hv, cv):
        return [_ssd_gate(rv[0], rv[1], cv[0])], []
    return _rows(fn, [y, (proj, SSD_W, C_Z // SSD_W)], [norm_w.reshape(1, -1)], [(SSD_W, MXU)], tile=512, name=name)[0]


def _ssd_post_bwd(y, proj, norm_w, dout, name):
    def fn(rv, hv, cv):
        yb, zb, db = rv
        _, vjp = jax.vjp(lambda a, b: _ssd_gate(a, b, cv[0]), yb, zb)
        dy, dz = vjp(db)
        t = yb * _silu(zb)
        nrm = []
        for g in (0, 1):
            tg = t[:, g * 256:(g + 1) * 256]
            nrm.append(tg * lax.rsqrt(jnp.mean(tg * tg, axis=-1, keepdims=True) + SSD_NORM_EPS))
        return [dy, dz], [_colsum8(db * jnp.concatenate(nrm, axis=1))]
    return _rows(fn, [y, (proj, SSD_W, C_Z // SSD_W), dout], [norm_w.reshape(1, -1)],
                 [(SSD_W, F32), (SSD_W, MXU)], [(8, SSD_W)], tile=512, name=name)


LRU_T = 256


def _lru_conv(proj, conv_w, conv_b, name):
    def fn(rv, hv, cv):
        return [_conv(rv[0], hv[0], cv[0], cv[1])], []
    return _rows(fn, [(proj, LRU_W, C_XL // LRU_W)], [_pad8(conv_w), conv_b.reshape(1, -1)], [(LRU_W, F32)],
                 tile=512, name=name, halos=[(0, "prev")])[0]


def _lru_conv_bwd(proj, dxc, conv_w, name):
    def fn(rv, hv, cv):
        dx, dw, db = _conv_bwd(rv[0], hv[0], rv[1], hv[1], cv[0])
        return [dx], [dw, jnp.concatenate([db, jnp.zeros((7, db.shape[1]), F32)], axis=0)]
    return _rows(fn, [(proj, LRU_W, C_XL // LRU_W), dxc], [_pad8(conv_w)], [(LRU_W, MXU)], [(8, LRU_W), (8, LRU_W)],
                 tile=512, name=name, halos=[(0, "prev"), (1, "next")])


def _lru_au(pre_a, pre_x, xc, ba, bx, lam):
    r = _sigmoid(pre_a + ba)
    i = _sigmoid(pre_x + bx)
    log_a = -LRU_C * r * _softplus(-lam)
    a = jnp.exp(log_a)
    u = jnp.sqrt(1.0 - jnp.exp(2.0 * log_a)) * (i * xc)
    return a, u


def _lru_scan(pre, xc, proj, par, name):
    s = xc.shape[0]
    t = LRU_T

    def body(pre_ref, xc_ref, g_ref, par_ref, out_ref, h_ref, carry):
        c = pl.program_id(0)

        @pl.when(c == 0)
        def _():
            carry[...] = jnp.zeros_like(carry)

        a, u = _lru_au(pre_ref[:, :LRU_W], pre_ref[:, LRU_W:], xc_ref[...], par_ref[0:1, :], par_ref[1:2, :], par_ref[2:3, :])
        row = lax.broadcasted_iota(jnp.int32, (t, LRU_W), 0)
        sft = 1
        while sft < t:
            keep = row >= sft
            a_s = jnp.where(keep, pltpu.roll(a, sft, 0), 1.0)
            u_s = jnp.where(keep, pltpu.roll(u, sft, 0), 0.0)
            u = a * u_s + u
            a = a * a_s
            sft *= 2
        h = a * carry[0:1, :] + u
        h_ref[...] = h
        out_ref[...] = (h * _gelu(g_ref[...])).astype(out_ref.dtype)
        carry[0:1, :] = h[t - 1:t, :]

    return pl.pallas_call(
        body, name=name, grid=(s // t,),
        in_specs=[pl.BlockSpec((t, 2 * LRU_W), lambda c: (c, 0)), pl.BlockSpec((t, LRU_W), lambda c: (c, 0)),
                  pl.BlockSpec((t, LRU_W), lambda c: (c, C_G // LRU_W)), pl.BlockSpec((8, LRU_W), lambda c: (0, 0))],
        out_specs=[pl.BlockSpec((t, LRU_W), lambda c: (c, 0))] * 2,
        out_shape=[jax.ShapeDtypeStruct((s, LRU_W), MXU), jax.ShapeDtypeStruct((s, LRU_W), F32)],
        scratch_shapes=[pltpu.VMEM((8, LRU_W), F32)],
        compiler_params=_cp("arbitrary"),
    )(pre, xc, proj, par)


def _lru_scan_bwd(pre, xc, proj, par, h, dout, name):
    s = xc.shape[0]
    t = LRU_T
    n = s // t
    t8 = t // 8

    def body(pre_ref, xc_ref, g_ref, par_ref, h_ref, hh_ref, do_ref, dpre_ref, dxc_ref, dg_ref, dpar_ref, carry):
        c = pl.program_id(0)

        @pl.when(c == 0)
        def _():
            carry[...] = jnp.zeros_like(carry)
            dpar_ref[...] = jnp.zeros_like(dpar_ref)

        pa, px, xcb = pre_ref[:, :LRU_W], pre_ref[:, LRU_W:], xc_ref[...]
        ba, bx, lam = par_ref[0:1, :], par_ref[1:2, :], par_ref[2:3, :]
        (a, u), vjp = jax.vjp(_lru_au, pa, px, xcb, ba, bx, lam)
        g = g_ref[...]
        hcur = h_ref[...]
        do = do_ref[...]
        _, gvjp = jax.vjp(_gelu, g)
        dg_ref[...] = gvjp(do * hcur)[0].astype(dg_ref.dtype)
        row = lax.broadcasted_iota(jnp.int32, (t, LRU_W), 0)
        v = do * _gelu(g) + jnp.where(row == t - 1, carry[0:1, :], 0.0)
        b = jnp.where(row == t - 1, 0.0, pltpu.roll(a, t - 1, 0))
        sft = 1
        while sft < t:
            keep = row < t - sft
            b_s = jnp.where(keep, pltpu.roll(b, t - sft, 0), 1.0)
            v_s = jnp.where(keep, pltpu.roll(v, t - sft, 0), 0.0)
            v = b * v_s + v
            b = b * b_s
            sft *= 2
        dh = v
        carry[0:1, :] = a[0:1, :] * dh[0:1, :]
        hhalo = jnp.where(c == n - 1, 0.0, hh_ref[...])
        hprev = _shift_down(hcur, hhalo, 1)
        dpa, dpx, dxc, dba, dbx, dlam = vjp((dh * hprev, dh))
        dpre_ref[:, :LRU_W] = dpa
        dpre_ref[:, LRU_W:] = dpx
        dxc_ref[...] = dxc
        dpar_ref[0:1, :] += dba
        dpar_ref[1:2, :] += dbx
        dpar_ref[2:3, :] += dlam

    rev = lambda c: (n - 1 - c, 0)
    return pl.pallas_call(
        body, name=name, grid=(n,),
        in_specs=[pl.BlockSpec((t, 2 * LRU_W), rev), pl.BlockSpec((t, LRU_W), rev),
                  pl.BlockSpec((t, LRU_W), lambda c: (n - 1 - c, C_G // LRU_W)), pl.BlockSpec((8, LRU_W), lambda c: (0, 0)),
                  pl.BlockSpec((t, LRU_W), rev),
                  pl.BlockSpec((8, LRU_W), lambda c: (jnp.maximum((n - 1 - c) * t8 - 1, 0), 0)),
                  pl.BlockSpec((t, LRU_W), lambda c: (n - 1 - c, dout.shape[1] // LRU_W - 1))],
        out_specs=[pl.BlockSpec((t, 2 * LRU_W), rev), pl.BlockSpec((t, LRU_W), rev), pl.BlockSpec((t, LRU_W), rev),
                   pl.BlockSpec((8, LRU_W), lambda c: (0, 0))],
        out_shape=[jax.ShapeDtypeStruct((s, 2 * LRU_W), F32), jax.ShapeDtypeStruct((s, LRU_W), F32),
                   jax.ShapeDtypeStruct((s, LRU_W), MXU), jax.ShapeDtypeStruct((8, LRU_W), F32)],
        scratch_shapes=[pltpu.VMEM((8, LRU_W), F32)],
        compiler_params=_cp("arbitrary"),
    )(pre, xc, proj, par, h, h, dout)


def _swiglu_act(gu, name):
    def fn(rv, hv, cv):
        return [_silu(rv[0].astype(F32)) * rv[1].astype(F32)], []
    return _rows(fn, [(gu, D_FF, 0), (gu, D_FF, 1)], [], [(D_FF, MXU)], tile=256, name=name)[0]


def _swiglu_bwd(gu, da, name):
    def fn(rv, hv, cv):
        gt, up, dab = [t.astype(F32) for t in rv]
        sg = _sigmoid(gt)
        dgate = dab * up * (sg * (1.0 + gt * (1.0 - sg)))
        dup = dab * (gt * sg)
        return [jnp.concatenate([dgate, dup], axis=1)], []
    return _rows(fn, [(gu, D_FF, 0), (gu, D_FF, 1), da], [], [(2 * D_FF, MXU)], tile=256, name=name)[0]


def _loss_head(x, g, target, name):
    d = x.shape[1]

    def fn(rv, hv, cv):
        xb, tb = rv
        y, vjp = jax.vjp(_rms, xb, cv[0])
        err = y - tb
        dy = err * (1.0 / d)
        dx, _ = vjp(dy)
        rstd = lax.rsqrt(jnp.mean(xb * xb, axis=-1, keepdims=True) + NORM_EPS)
        e2 = err * err * (0.5 / d)
        e2 = functools.reduce(lambda a, b: a + b, [e2[:, k * BLK:(k + 1) * BLK] for k in range(d // BLK)])
        return [dx], [_colsum8(dy * xb * rstd), _colsum8(e2)]
    return _rows(fn, [x, target], [g.reshape(1, -1)], [(d, F32)], [(8, d), (8, BLK)], tile=512, name=name)


ANY = pl.BlockSpec(memory_space=pl.ANY)


def _coords():
    return lax.axis_index("x"), lax.axis_index("y"), lax.axis_index("c")


class _Comm:
    def __init__(self, gathers=(), scatters=()):
        self.gathers = list(gathers)
        self.scatters = list(scatters)
        self.n = len(self.gathers) + len(self.scatters)

    def args(self):
        return [g[0] for g in self.gathers] + self.scatters

    def out_shape(self):
        out = [jax.ShapeDtypeStruct((4,) + (a.shape if l is None else a.shape[1:]), a.dtype) for a, l, _ in self.gathers]
        return out + [jax.ShapeDtypeStruct((3,) + a.shape[1:], a.dtype) for a in self.scatters]

    def scratch(self):
        if not self.n:
            return []
        return [pltpu.SemaphoreType.DMA((3 * self.n,)), pltpu.SemaphoreType.DMA((3 * self.n,)),
                pltpu.SemaphoreType.DMA((max(len(self.gathers), 1),)),
                pltpu.SemaphoreType.DMA((3 * self.n,)), pltpu.SemaphoreType.DMA((3 * self.n,))]

    def split(self, refs, n_in, n_out, n_scratch):
        refs = list(refs)
        n = self.n
        own = refs[:n_in] + refs[n_in + n:n_in + n + n_out] + refs[n_in + 2 * n + n_out:n_in + 2 * n + n_out + n_scratch]
        cm = (refs[n_in:n_in + n], refs[n_in + n + n_out:n_in + 2 * n + n_out], refs[n_in + 2 * n + n_out + n_scratch:])
        return own, cm

    def _copies(self, cm, arriving):
        ins, outs, (send, recv, local, _, _) = cm
        x, y, c = _coords()
        me = 2 * x + y
        chips = [(1 - x, y), (x, 1 - y), (1 - x, 1 - y)]
        remote, locals_ = [], []
        ng = len(self.gathers)
        for i in range(self.n):
            if i < ng:
                _, l, halved = self.gathers[i]
                slab = ins[i] if l is None else ins[i].at[l]
                if not arriving:
                    locals_.append(pltpu.make_async_copy(slab, outs[i].at[me], local.at[i]))
            for j, (px, py) in enumerate(chips):
                if i < ng:
                    slot = 2 * px + py if arriving else me
                    src, dst = (slab.at[c], outs[i].at[slot, c]) if halved else (slab, outs[i].at[slot])
                else:
                    src, dst = ins[i].at[2 * px + py], outs[i].at[j]
                remote.append(pltpu.make_async_remote_copy(src, dst, send.at[3 * i + j], recv.at[3 * i + j],
                                                           device_id=(px, py, c), device_id_type=MESH))
        return remote, locals_

    def _handovers(self, cm, arriving):
        _, outs, (_, _, _, send, recv) = cm
        x, y, c = _coords()
        chips = [(1 - x, y), (x, 1 - y), (1 - x, 1 - y)]
        cps = []
        for i, (_, _, halved) in enumerate(self.gathers):
            if halved:
                for j, (px, py) in enumerate(chips):
                    src = outs[i].at[2 * px + py, c]
                    dst = outs[i].at[2 * px + py, 1 - c if arriving else c]
                    cps.append(pltpu.make_async_remote_copy(src, dst, send.at[3 * i + j], recv.at[3 * i + j],
                                                            device_id=(x, y, 1 - c), device_id_type=MESH))
        return cps

    def start_at(self, cond, cm):
        def go():
            remote, locals_ = self._copies(cm, False)
            for cp in locals_ + remote:
                cp.start()

        if self.n:
            go() if cond is True else pl.when(cond)(go)

    def wait_at(self, cond, cm):
        def go():
            for cp in self._copies(cm, True)[0]:
                cp.wait_recv()
            handed = self._handovers(cm, False)
            for cp in handed:
                cp.start()
            for cp in self._handovers(cm, True):
                cp.wait_recv()
            remote, locals_ = self._copies(cm, False)
            for cp in handed + remote:
                cp.wait_send()
            for cp in locals_:
                cp.wait()

        if self.n:
            go() if cond is True else pl.when(cond)(go)


def _swap_sibling(arrs):
    n = len(arrs)

    def body(*refs):
        ins, outs, send, recv = refs[:n], refs[n:2 * n], refs[2 * n], refs[2 * n + 1]
        x, y, c = _coords()
        cps = [pltpu.make_async_remote_copy(ins[i], outs[i], send.at[i], recv.at[i], device_id=(x, y, 1 - c), device_id_type=MESH)
               for i in range(n)]
        for cp in cps:
            cp.start()
        for cp in cps:
            cp.wait_recv()
        for cp in cps:
            cp.wait_send()

    return list(pl.pallas_call(
        body, name="swap_sibling", in_specs=[ANY] * n, out_specs=[ANY] * n,
        out_shape=[jax.ShapeDtypeStruct(a.shape, a.dtype) for a in arrs],
        scratch_shapes=[pltpu.SemaphoreType.DMA((n,)), pltpu.SemaphoreType.DMA((n,))],
        compiler_params=pltpu.CompilerParams(has_side_effects=True),
    )(*arrs))


def _gather_small(gs):
    def body(g_ref, o_ref, send_sems, recv_sems, local_sem):
        x, y, c = _coords()
        me = 4 * x + 2 * y + c
        mine = pltpu.make_async_copy(g_ref, o_ref.at[me], local_sem)
        mine.start()
        sends = []
        for k in range(1, 8):
            px, py, pc = x ^ (k >> 2), y ^ ((k >> 1) & 1), c ^ (k & 1)
            sends.append((pltpu.make_async_remote_copy(g_ref, o_ref.at[me], send_sems.at[k - 1], recv_sems.at[k - 1],
                                                       device_id=(px, py, pc), device_id_type=MESH), 4 * px + 2 * py + pc, k))
        for cp, _, _ in sends:
            cp.start()
        for cp, src, k in sends:
            pltpu.make_async_remote_copy(g_ref, o_ref.at[src], send_sems.at[k - 1], recv_sems.at[k - 1],
                                         device_id=(x, y, c), device_id_type=MESH).wait_recv()
        for cp, _, _ in sends:
            cp.wait_send()
        mine.wait()

    return pl.pallas_call(
        body, name="gather_small", in_specs=[ANY], out_specs=ANY,
        out_shape=jax.ShapeDtypeStruct((8,) + gs.shape, gs.dtype),
        scratch_shapes=[pltpu.SemaphoreType.DMA((7,)), pltpu.SemaphoreType.DMA((7,)), pltpu.SemaphoreType.DMA],
        compiler_params=pltpu.CompilerParams(has_side_effects=True),
    )(gs)


def _sum_slots(own, others, name, tile):
    k, r, c = others.shape

    def body(*refs):
        if own is None:
            o_ref, out_ref = refs
            acc = o_ref[0].astype(F32)
            first = 1
        else:
            own_ref, o_ref, out_ref = refs
            acc = own_ref[...]
            first = 0
        for j in range(first, k):
            acc = acc + o_ref[j].astype(F32)
        out_ref[...] = acc

    row = pl.BlockSpec((tile, c), lambda i: (i, 0))
    specs = ([] if own is None else [row]) + [pl.BlockSpec((k, tile, c), lambda i: (0, i, 0))]
    args = ([] if own is None else [own]) + [others]
    return pl.pallas_call(body, name=name, grid=(r // tile,), in_specs=specs, out_specs=row,
                          out_shape=jax.ShapeDtypeStruct((r, c), F32), compiler_params=_cp("parallel"))(*args)


def _adamw(w, m, v, ga, gb, name, tile, rows_first=False):
    lead = 0 if rows_first else w.ndim - 2
    r, c = w.shape[-2:]

    def body(*refs):
        vals = [ref[0] if lead else ref[...] for ref in refs[:len(refs) - 4]]
        w_, m_, v_, g = vals[0], vals[1], vals[2], vals[3]
        if gb is not None:
            g = g + vals[4]
        nm = ADAM_B1 * m_ + (1.0 - ADAM_B1) * g
        nv = ADAM_B2 * v_ + (1.0 - ADAM_B2) * (g * g)
        d = -ADAM_LR * ((nm / BC1) / (jnp.sqrt(nv / BC2) + ADAM_EPS) + ADAM_WD * w_)
        for ref, val in zip(refs[len(refs) - 4:], (g, d, nm, nv)):
            if lead:
                ref[0] = val
            else:
                ref[...] = val

    if rows_first:
        row = pl.BlockSpec((tile,) + w.shape[1:], lambda i: (i, 0, 0))
        grid = (w.shape[0] // tile,)
    elif lead:
        row = pl.BlockSpec((1, tile, c), lambda l, i: (l, i, 0))
        grid = (w.shape[0], r // tile)
    else:
        row = pl.BlockSpec((tile, c), lambda i: (i, 0))
        grid = (r // tile,)
    args = [w, m, v, ga] + ([] if gb is None else [gb])
    return pl.pallas_call(body, name=name, grid=grid, in_specs=[row] * len(args), out_specs=[row] * 4,
                          out_shape=[jax.ShapeDtypeStruct(w.shape, F32)] * 4,
                          compiler_params=_cp(*(["parallel"] * len(grid))))(*args)


MATS = ("w_in", "w_out", "w_gate", "w_up", "w_down")
CONVS = ("ssd_conv_w", "lru_conv_w")
BIG = MATS + CONVS
TRANSPOSED = ("w_gate", "w_up")
COL_SHARDED = ("ssd_conv_w", "lru_conv_w")
W_IN_SHARD = IN_COLS // 4
W_IN_PAD = 1056
SMALL = ("norm_mix", "ssd_conv_b", "ssd_dt_bias", "ssd_a_log", "ssd_d", "ssd_norm", "lru_conv_b", "lru_wa", "lru_ba",
         "lru_wx", "lru_bx", "lru_lambda", "norm_ffn", "norm_final")
WEIGHTS = ("norm_mix", "w_in", "ssd_conv_w", "ssd_conv_b", "ssd_dt_bias", "ssd_a_log", "ssd_d", "ssd_norm", "lru_conv_w",
           "lru_conv_b", "lru_wa", "lru_ba", "lru_wx", "lru_bx", "lru_lambda", "w_out", "norm_ffn", "w_gate", "w_up",
           "w_down", "norm_final")
ROW_TILE = {"w_in": W_IN_SHARD, "w_out": 128, "w_gate": 352, "w_up": 352, "w_down": 352}
W_IN_ADAM_TILE = 54


def _pack(arrs, width, row_mult, dtype):
    flat = jnp.concatenate([a.reshape(-1).astype(dtype) for a in arrs])
    rows = -(-flat.shape[0] // width)
    rows = -(-rows // row_mult) * row_mult
    flat = jnp.pad(flat, (0, rows * width - flat.shape[0]))
    return flat.reshape(rows, width)


def _unpack(buf, shapes):
    flat = buf.reshape(-1)
    out, off = [], 0
    for shp in shapes:
        n = int(np.prod(shp))
        out.append(flat[off:off + n].reshape(shp))
        off += n
    return out


def _join(name, g4):
    if name in COL_SHARDED:
        return jnp.moveaxis(g4, 0, -2).reshape(g4.shape[1:-1] + (4 * g4.shape[-1],))
    return g4.reshape((4 * g4.shape[1],) + g4.shape[2:])


def _slabs(name, g):
    if name in COL_SHARDED:
        return jnp.moveaxis(g.reshape(g.shape[:-1] + (4, g.shape[-1] // 4)), -2, 0)
    return g.reshape((4, g.shape[0] // 4) + g.shape[1:])


def _w_in_rows(g4):
    def nat(lo, hi):
        out = []
        while lo < hi:
            j = lo // W_IN_SHARD
            stop = min(hi, (j + 1) * W_IN_SHARD)
            out.append((j, lo - j * W_IN_SHARD, stop - lo))
            lo = stop
        return out
    pieces = nat(0, 3072) + nat(3080, IN_COLS) + nat(3072, 3080)

    def body(g_ref, o_ref):
        row = 0
        for j, first, n in pieces:
            o_ref[row:row + n, :] = g_ref[j, first:first + n, :]
            row += n
        o_ref[row:, :] = jnp.zeros((NP - row, o_ref.shape[1]), o_ref.dtype)

    return pl.pallas_call(body, name="w_in_rows", out_shape=jax.ShapeDtypeStruct((NP, g4.shape[-1]), g4.dtype),
                          compiler_params=pltpu.CompilerParams(vmem_limit_bytes=VMEM_LIMIT))(g4)


def _w_in_slabs(gt):
    def kern(n):
        return n if n < 3072 else (C_DT + n - 3072 if n < 3080 else n - 8)
    slabs = []
    for j in range(4):
        lo, hi = j * W_IN_SHARD, (j + 1) * W_IN_SHARD
        cuts = sorted({lo, hi} | {c for c in (3072, 3080) if lo < c < hi})
        slabs.append(jnp.concatenate([gt[kern(a):kern(a) + b - a] for a, b in zip(cuts[:-1], cuts[1:])], axis=0))
    return jnp.stack(slabs, axis=0)


def _block_diag(w):
    eye = jnp.eye(LRU_BLOCKS, dtype=w.dtype)
    return jnp.einsum("ncd,nm->ncmd", w, eye).reshape(LRU_W, LRU_W)


def _block_diag_extract(g):
    g4 = g.reshape(LRU_BLOCKS, 64, LRU_BLOCKS, 64)
    return jnp.stack([g4[n, :, n, :] for n in range(LRU_BLOCKS)], axis=0)


def _lanes128(v):
    return jnp.pad(v, (0, BLK - v.shape[0])).reshape(1, BLK)


def _layer_mixers(x, p, comm=None, h=None):
    if h is None:
        h = _rms_fwd(x, p["norm_mix"], "rms_mix")
    proj = _mm(h, p["w_in_t"], tb=True, tm=1024, tn=1408, tk=1024, name="mm_in")
    att, lse, attb, got = _att_fwd_fused(proj, "att_fwd", comm)
    xconv, dt = _ssd_pre(proj, p["ssd_conv_w"], p["ssd_conv_b"], _lanes128(p["ssd_dt_bias"]), "ssd_pre")
    spar = jnp.concatenate([_lanes128(p["ssd_a_log"]), _lanes128(p["ssd_d"]), jnp.zeros((6, BLK), F32)], axis=0)
    y, states = _ssd_scan(xconv, dt, spar, "ssd_scan")
    ssd = _ssd_post(y, proj, p["ssd_norm"], "ssd_post")
    xc = _lru_conv(proj, p["lru_conv_w"], p["lru_conv_b"], "lru_conv")
    wab = jnp.concatenate([_block_diag(p["lru_wa"]), _block_diag(p["lru_wx"])], axis=1).astype(MXU)
    pre = _mm(xc, wab, tm=1024, tn=1024, tk=512, name="mm_lru")
    lpar = jnp.concatenate([p["lru_ba"].reshape(1, -1), p["lru_bx"].reshape(1, -1), p["lru_lambda"].reshape(1, -1),
                            jnp.zeros((5, LRU_W), F32)], axis=0)
    lru, hs = _lru_scan(pre, xc, proj, lpar, "lru_scan")
    mix = jnp.concatenate([attb, ssd, lru], axis=1)
    saved = dict(x=x, h=h, proj=proj, att=att, lse=lse, xconv=xconv, dt=dt, spar=spar, y=y, states=states, xc=xc, wab=wab,
                 pre=pre, lpar=lpar, hs=hs, mix=mix)
    return mix, saved, got


def _layer_ffn(x, mix, p, saved, comm=None, next_norm=None, comm_down=None):
    x1, h2 = _mm(mix, p["w_out"], add=x, tm=1024, tn=1024, tk=1536, name="mm_out", epi=_epi_rms(p["norm_ffn"]))
    gu = _mm(h2, p["w_gu_t"], tb=True, out_dtype=MXU, tm=1024, tn=1408, tk=1024, name="mm_gu", comm=comm)
    gu, got = gu if comm is not None else (gu, [])
    act = _swiglu_act(gu, "swiglu_act")
    x2 = _mm(act, p["w_down"], add=x1, tm=1024, tn=1024, tk=2816, name="mm_down", comm=comm_down,
             epi=None if next_norm is None else _epi_rms(next_norm))
    x2, got_down = x2 if comm_down is not None else (x2, [])
    x2, h_next = x2 if next_norm is not None else (x2, None)
    saved.update(x1=x1, h2=h2, gu=gu, act=act)
    return x2, got + got_down, h_next


def _layer_bwd(dx2, p, sv, comm_ssd=None, comm_att=None, comm_tail=None):
    g = {}
    da = _mm(dx2, p["w_down"], tb=True, out_dtype=MXU, tm=1024, tn=1408, tk=1024, name="mm_d_act")
    g["w_down"], g["w_down@wire"] = _mm(sv["act"], dx2, ta=True, tm=1408, tn=1024, tk=1024, name="mm_g_down", epi=_epi_wire(D_MODEL))
    dgu = _swiglu_bwd(sv["gu"], da, "swiglu_bwd")
    dx1, gn = _mm(dgu, p["w_gu_t"], tm=1024, tn=1024, tk=1408, name="mm_d_h2", epi=_epi_rms_bwd(sv["x1"], p["norm_ffn"], dx2))
    g["w_gu_t"], g["w_gu_t@wire"] = _mm(dgu, sv["h2"], ta=True, tm=1408, tn=1024, tk=1024, name="mm_g_gu", epi=_epi_wire(D_MODEL))
    g["norm_ffn"] = jnp.sum(gn, axis=0)
    dmix, stats = _mm(dx1, p["w_out"], tb=True, tm=1024, tn=1536, tk=1024, name="mm_d_mix", epi=_epi_att_stats(sv["att"], sv["lse"]))
    g["w_out"], g["w_out@wire"] = _mm(sv["mix"], dx1, ta=True, tm=1536, tn=1024, tk=1024, name="mm_g_out", epi=_epi_wire(D_MODEL))
    proj = sv["proj"]
    dpre, dxc_u, dgl, dlpar = _lru_scan_bwd(sv["pre"], sv["xc"], proj, sv["lpar"], sv["hs"], dmix, "lru_scan_bwd")
    dxc = _mm(dpre, sv["wab"], tb=True, add=dxc_u, tm=1024, tn=512, tk=1024, name="mm_d_xc")
    gwab = _mm(sv["xc"], dpre, ta=True, tm=512, tn=1024, tk=1024, name="mm_g_lru")
    g["lru_wa"], g["lru_wx"] = _block_diag_extract(gwab[:, :LRU_W]), _block_diag_extract(gwab[:, LRU_W:])
    g["lru_ba"], g["lru_bx"], g["lru_lambda"] = dlpar[0], dlpar[1], dlpar[2]
    dxl, gcw, gcb = _lru_conv_bwd(proj, dxc, p["lru_conv_w"], "lru_conv_bwd")
    g["lru_conv_w"], g["lru_conv_b"] = gcw[:CONV_K], jnp.sum(gcb, axis=0)
    dy, dz, gsn = _ssd_post_bwd(sv["y"], proj, p["ssd_norm"], (dmix, SSD_W, 1), "ssd_post_bwd")
    g["ssd_norm"] = jnp.sum(gsn, axis=0)
    dxconv, ddt, dal, ddk, got_ssd = _ssd_scan_bwd(sv["xconv"], sv["dt"], sv["spar"], sv["states"], dy, "ssd_scan_bwd", comm_ssd)
    g["ssd_a_log"], g["ssd_d"] = dal[0, :8], ddk[0, :8]
    dxbc, ddtr, gsw, gsb, gdb = _ssd_pre_bwd(proj, dxconv, ddt, p["ssd_conv_w"], p["ssd_conv_b"],
                                             _lanes128(p["ssd_dt_bias"]), "ssd_pre_bwd")
    g["ssd_conv_w"], g["ssd_conv_b"], g["ssd_dt_bias"] = gsw[:CONV_K], jnp.sum(gsb, axis=0), jnp.sum(gdb, axis=0)[:8]
    dq, dk, dv, got_att = _att_bwd_rev(proj, dmix, stats, "att_bwd", None if comm_att is None else comm_att(g))
    dproj = jnp.concatenate([dq, dk, dv, dz, dxbc, dgl, dxl, ddtr], axis=1)
    g["w_in_t"], g["w_in_t@wire"] = _mm(dproj, sv["h"], ta=True, tm=1408, tn=1024, tk=1024, name="mm_g_in", epi=_epi_wire(D_MODEL))
    res = _mm(dproj, p["w_in_t"], tm=1024, tn=1024, tk=1408, name="mm_d_h", comm=None if comm_tail is None else comm_tail(g),
              epi=_epi_rms_bwd(sv["x"], p["norm_mix"], dx1))
    (dx, gm), got_tail = res if comm_tail is not None else (res, [])
    g["norm_mix"] = jnp.sum(gm, axis=0)
    return dx, g, got_ssd, got_att, got_tail


def _grad_slabs(g, names, suffix=""):
    out = {}
    for n in names:
        if n == "w_in":
            out[n] = _w_in_slabs(g["w_in_t" + suffix])
        elif n == "w_gate":
            out[n] = _slabs(n, g["w_gu_t" + suffix][:D_FF])
        elif n == "w_up":
            out[n] = _slabs(n, g["w_gu_t" + suffix][D_FF:])
        else:
            out[n] = _slabs(n, g[n + suffix])
    return out


def kernel(x, norm_mix, w_in, ssd_conv_w, ssd_conv_b, ssd_dt_bias, ssd_a_log, ssd_d, ssd_norm, lru_conv_w, lru_conv_b, lru_wa, lru_ba, lru_wx, lru_bx, lru_lambda, w_out, norm_ffn, w_gate, w_up, w_down, norm_final, loss_target, m_norm_mix, m_w_in, m_ssd_conv_w, m_ssd_conv_b, m_ssd_dt_bias, m_ssd_a_log, m_ssd_d, m_ssd_norm, m_lru_conv_w, m_lru_conv_b, m_lru_wa, m_lru_ba, m_lru_wx, m_lru_bx, m_lru_lambda, m_w_out, m_norm_ffn, m_w_gate, m_w_up, m_w_down, m_norm_final, v_norm_mix, v_w_in, v_ssd_conv_w, v_ssd_conv_b, v_ssd_dt_bias, v_ssd_a_log, v_ssd_d, v_ssd_norm, v_lru_conv_w, v_lru_conv_b, v_lru_wa, v_lru_ba, v_lru_wx, v_lru_bx, v_lru_lambda, v_w_out, v_norm_ffn, v_w_gate, v_w_up, v_w_down, v_norm_final):
    loc = dict(locals())
    w = {n: loc[n] for n in WEIGHTS}
    m = {n: loc["m_" + n] for n in WEIGHTS}
    v = {n: loc["v_" + n] for n in WEIGHTS}
    for n in TRANSPOSED:
        w[n], m[n], v[n] = [jnp.transpose(t, (0, 2, 1)) for t in (w[n], m[n], v[n])]
    wt_in, mt_in, vt_in = [jnp.transpose(t, (2, 0, 1)) for t in (w["w_in"], m["w_in"], v["w_in"])]

    def halves(a):
        return a.reshape(a.shape[0], 2, a.shape[1] // 2, a.shape[2])

    def unhalve(a):
        return a.reshape(4, 2 * a.shape[2], a.shape[3])

    def joined(name, a):
        return _w_in_rows(unhalve(a)) if name == "w_in" else _join(name, unhalve(a))

    wb = {n: halves(w[n].astype(MXU)) for n in MATS[1:]}
    wb["w_in"] = halves(jnp.pad(jnp.transpose(wt_in.astype(MXU), (1, 0, 2)), ((0, 0), (0, W_IN_PAD - W_IN_SHARD), (0, 0))))
    xs = x[0]
    h0, first = _rms_fwd(xs, norm_mix[0], "rms_mix", _Comm(gathers=[(wb["w_in"], 0, True), (w["ssd_conv_w"], None, False),
                                                                    (w["lru_conv_w"], None, False)]))
    convs = {"ssd_conv_w": _join("ssd_conv_w", first[1]), "lru_conv_w": _join("lru_conv_w", first[2])}
    behind_att = [(n, 0) for n in MATS[1:]] + [("w_in", 1)]
    behind_ffn = [(n, 1) for n in MATS[1:]]
    whole = {("w_in", 0): joined("w_in", first[0])}
    params = {}

    def layer_params(l):
        if l not in params:
            p = {n: w[n][l] for n in SMALL if n != "norm_final"}
            p.update(w_in_t=whole["w_in", l], ssd_conv_w=convs["ssd_conv_w"][l], lru_conv_w=convs["lru_conv_w"][l])
            params[l] = p
        if "w_out" not in params[l] and ("w_out", l) in whole:
            params[l].update(w_out=whole["w_out", l], w_down=whole["w_down", l],
                             w_gu_t=jnp.concatenate([whole["w_gate", l], whole["w_up", l]], axis=0))
        return params[l]

    saved = []
    h_in = h0
    for l in range(DEPTH):
        first_layer = l == 0
        mix, sv, got = _layer_mixers(xs, layer_params(l), _Comm(gathers=[(wb[n], k, True) for n, k in behind_att]) if first_layer else None,
                                     h_in)
        whole.update({k: joined(k[0], a) for k, a in zip(behind_att, got)})
        xs, got, h_in = _layer_ffn(xs, mix, layer_params(l), sv, _Comm(gathers=[(wb[n], k, True) for n, k in behind_ffn[:-1]]) if first_layer else None,
                                   norm_mix[l + 1] if l + 1 < DEPTH else None,
                                   _Comm(gathers=[(wb[n], k, True) for n, k in behind_ffn[-1:]]) if first_layer else None)
        whole.update({k: joined(k[0], a) for k, a in zip(behind_ffn, got)})
        saved.append(sv)
    dx, gnf, lsum = _loss_head(xs, norm_final, loss_target[0], "loss_head")
    loss = lax.psum(jnp.sum(lsum), ("x", "y", "c"))

    dx, g1, _, _, _ = _layer_bwd(dx, layer_params(1), saved[1])
    def slabs_of(g, names):
        own = _grad_slabs(g, names)
        sent = _grad_slabs(g, [n for n in names if n in MATS], "@wire")
        sent.update({n: own[n] for n in names if n not in MATS})
        return own, sent

    s1, sent1 = slabs_of(g1, BIG)
    att0 = ("w_gate", "w_up", "w_down", "w_out")
    s0, sent0 = {}, {}

    def add0(g0, names):
        own, sent = slabs_of(g0, names)
        s0.update(own)
        sent0.update(sent)

    ssd1 = ("w_gate", "w_up")
    att1 = tuple(n for n in BIG if n not in ssd1)

    def comm_att(g0):
        add0(g0, att0)
        return _Comm(scatters=[sent1[n] for n in att1] + [sent0[n] for n in att0])

    tail0 = ("w_in",) + CONVS

    def comm_tail(g0):
        add0(g0, tail0)
        return _Comm(scatters=[sent0[n] for n in tail0])

    dx, g0, got_ssd, got_att, got_tail = _layer_bwd(dx, layer_params(0), saved[0], _Comm(scatters=[sent1[n] for n in ssd1]),
                                                    comm_att, comm_tail)
    recv = {(n, 1): a for n, a in zip(ssd1, got_ssd)}
    recv.update({(n, 1): a for n, a in zip(att1, got_att[:len(att1)])})
    recv.update({(n, 0): a for n, a in zip(att0, got_att[len(att1):])})
    recv.update({(n, 0): a for n, a in zip(tail0, got_tail)})

    me = 2 * lax.axis_index("x") + lax.axis_index("y")
    slabs = (s0, s1)
    part = {}
    for n in BIG:
        per_layer = []
        for l in range(DEPTH):
            own = lax.dynamic_index_in_dim(slabs[l][n], me, axis=0, keepdims=False)
            per_layer.append(_sum_slots(own, recv[n, l], "sum_chips_" + n, ROW_TILE.get(n, own.shape[0])))
        part[n] = jnp.stack(per_layer, axis=0)
    sib = dict(zip(BIG, _swap_sibling([part[n] for n in BIG])))
    out_g, out_d, out_m, out_v = {}, {}, {}, {}
    for n in BIG:
        if n == "w_in":
            res = _adamw(wt_in, mt_in, vt_in, jnp.transpose(part[n], (1, 0, 2)), jnp.transpose(sib[n], (1, 0, 2)), "adamw_" + n,
                         W_IN_ADAM_TILE, rows_first=True)
            out_g[n], out_d[n], out_m[n], out_v[n] = [jnp.transpose(t, (1, 2, 0)) for t in res]
            continue
        res = _adamw(w[n], m[n], v[n], part[n], sib[n], "adamw_" + n, ROW_TILE.get(n, w[n].shape[1]))
        out_g[n], out_d[n], out_m[n], out_v[n] = [jnp.transpose(t, (0, 2, 1)) for t in res] if n in TRANSPOSED else res

    gsm = {n: jnp.stack([g0[n], g1[n]], axis=0) for n in SMALL if n != "norm_final"}
    gsm["norm_final"] = jnp.sum(gnf, axis=0)
    small_shapes = [w[n].shape for n in SMALL]
    gs = _pack([gsm[n].reshape(w[n].shape) for n in SMALL], BLK, 8, F32)
    gall = _gather_small(gs)
    gsum = _sum_slots(None, gall, "sum_devices", gs.shape[0])
    ws = _pack([w[n] for n in SMALL], BLK, 8, F32)
    ms = _pack([m[n] for n in SMALL], BLK, 8, F32)
    vs = _pack([v[n] for n in SMALL], BLK, 8, F32)
    gsr, dsr, nms, nvs = _adamw(ws, ms, vs, gsum, None, "adamw_small", gs.shape[0])
    out_g.update(zip(SMALL, _unpack(gsr, small_shapes)))
    out_d.update(zip(SMALL, _unpack(dsr, small_shapes)))
    out_m.update(zip(SMALL, _unpack(nms, small_shapes)))
    out_v.update(zip(SMALL, _unpack(nvs, small_shapes)))

    return (loss, dx[None], *[out_g[n] for n in WEIGHTS], *[out_d[n] for n in WEIGHTS],
            *[out_m[n] for n in WEIGHTS], *[out_v[n] for n in WEIGHTS])
```

```python
import functools
import math

import jax
import jax.numpy as jnp
import numpy as np
from jax import lax
from jax.experimental import pallas as pl
from jax.experimental.pallas import tpu as pltpu

F32 = jnp.float32
MXU = jnp.bfloat16
HI = lax.Precision.HIGHEST
HIGH = lax.Precision.HIGH
MESH = pl.DeviceIdType.MESH

D_MODEL = 1024
DEPTH = 2
HEAD_DIM = 64
ATT_W = 512
ATT_PATTERNS = ((128, 1), (512, 4), (2048, 16))
BLK = 128
SSD_W = 512
SSD_STATE = 128
LRU_W = 512
LRU_BLOCKS = 8
LRU_C = 8.0
CONV_K = 4
D_MIX = 1536
D_FF = 2816
IN_COLS = 4104
NP = 4224
NORM_EPS = 1e-6
SSD_NORM_EPS = 1e-5
LN2 = math.log(2.0)
NEG = -1e30

ADAM_LR, ADAM_B1, ADAM_B2, ADAM_EPS, ADAM_WD, ADAM_STEP = 0.001, 0.9, 0.999, 1e-08, 0.01, 10
BC1 = 1.0 - ADAM_B1 ** ADAM_STEP
BC2 = 1.0 - ADAM_B2 ** ADAM_STEP

VMEM_LIMIT = 56 * 1024 * 1024

C_Q, C_K, C_V, C_Z, C_XBC, C_G, C_XL, C_DT = 0, 512, 1024, 1536, 2048, 3072, 3584, 4096


def _cp(*sem):
    return pltpu.CompilerParams(dimension_semantics=sem, vmem_limit_bytes=VMEM_LIMIT)


def _dot(a, b, dims, prec=None):
    return lax.dot_general(a, b, (dims, ((), ())), preferred_element_type=F32, precision=prec)


def _nn(a, b, prec=None):
    return _dot(a, b, ((1,), (0,)), prec)


def _nt(a, b, prec=None):
    return _dot(a, b, ((1,), (1,)), prec)


def _tn(a, b, prec=None):
    return _dot(a, b, ((0,), (0,)), prec)


def _sigmoid(x):
    return jax.nn.sigmoid(x)


def _silu(x):
    return x * _sigmoid(x)


def _softplus(x):
    return jnp.maximum(x, 0.0) + jnp.log(1.0 + jnp.exp(-jnp.abs(x)))


def _gelu(x):
    return 0.5 * x * (1.0 + jnp.tanh(0.7978845608028654 * (x + 0.044715 * x * x * x)))


def _mm(a, b, *, ta=False, tb=False, add=None, out_dtype=F32, tm, tn, tk, name, comm=None, epi=None):
    m, k = (a.shape[1], a.shape[0]) if ta else a.shape
    n = b.shape[0] if tb else b.shape[1]
    assert (b.shape[1] if tb else b.shape[0]) == k
    assert m % tm == 0 and n % tn == 0 and k % tk == 0, (name, m, n, k)
    nk = k // tk
    a_spec = pl.BlockSpec((tk, tm), lambda i, j, kk: (kk, i)) if ta else pl.BlockSpec((tm, tk), lambda i, j, kk: (i, kk))
    b_spec = pl.BlockSpec((tn, tk), lambda i, j, kk: (j, kk)) if tb else pl.BlockSpec((tk, tn), lambda i, j, kk: (kk, j))
    o_spec = pl.BlockSpec((tm, tn), lambda i, j, kk: (i, j))
    dims = ((0 if ta else 1,), (1 if tb else 0,))
    carried = comm is not None
    comm = comm or _Comm()
    ni, nj = m // tm, n // tn
    efn, erows, econsts, eouts, eaccs = epi or (None, [], [], [], [])
    assert epi is None or nj == 1
    nadd = 0 if add is None else 1
    ner, nec, neo, nea = len(erows), len(econsts), len(eouts), len(eaccs)

    def body(*refs):
        refs, cm = comm.split(refs, 2 + nadd + ner + nec, 1 + neo + nea, 1)
        a_ref, b_ref = refs[:2]
        er_refs = refs[2 + nadd:2 + nadd + ner]
        ec_refs = refs[2 + nadd + ner:2 + nadd + ner + nec]
        o_ref = refs[2 + nadd + ner + nec]
        eo_refs = refs[3 + nadd + ner + nec:3 + nadd + ner + nec + neo]
        ea_refs = refs[3 + nadd + ner + nec + neo:3 + nadd + ner + nec + neo + nea]
        acc = refs[-1]
        i, j, kk = pl.program_id(0), pl.program_id(1), pl.program_id(2)
        comm.start_at((i == 0) & (j == 0) & (kk == 0), cm)

        @pl.when(kk == 0)
        def _():
            acc[...] = jnp.zeros_like(acc)

        acc[...] += _dot(a_ref[...].astype(MXU), b_ref[...].astype(MXU), dims)

        @pl.when(kk == nk - 1)
        def _():
            r = acc[...]
            if add is not None:
                r = r + refs[2][...]
            if efn is None:
                o_ref[...] = r.astype(out_dtype)
            else:
                main, extra, sums = efn(r, [t[...] for t in er_refs], [t[...] for t in ec_refs])
                o_ref[...] = main.astype(out_dtype)
                for t, val in zip(eo_refs, extra):
                    t[...] = val.astype(t.dtype)
                @pl.when(i == 0)
                def _():
                    for t, val in zip(ea_refs, sums):
                        t[...] = val

                @pl.when(i > 0)
                def _():
                    for t, val in zip(ea_refs, sums):
                        t[...] += val

        comm.wait_at((i == ni - 1) & (j == nj - 1) & (kk == nk - 1), cm)

    def whole_rows(width):
        return pl.BlockSpec((tm, width), lambda i, j, kk: (i, 0))

    ins = [a, b] + ([] if add is None else [add]) + list(erows) + list(econsts)
    specs = [a_spec, b_spec] + ([] if add is None else [o_spec]) + [whole_rows(t.shape[1]) for t in erows]
    specs += [pl.BlockSpec(t.shape, lambda i, j, kk: (0, 0)) for t in econsts]
    out_specs = [o_spec] + [whole_rows(wd) for wd, _ in eouts] + [pl.BlockSpec((r, wd), lambda i, j, kk: (0, 0)) for r, wd in eaccs]
    out_shape = [jax.ShapeDtypeStruct((m, n), out_dtype)] + [jax.ShapeDtypeStruct((m, wd), dt) for wd, dt in eouts]
    out_shape += [jax.ShapeDtypeStruct((r, wd), F32) for r, wd in eaccs]
    serial = comm.n or nea
    res = pl.pallas_call(
        body, name=name, grid=(ni, nj, nk), in_specs=specs + [ANY] * comm.n, out_specs=out_specs + [ANY] * comm.n,
        out_shape=out_shape + comm.out_shape(),
        scratch_shapes=[pltpu.VMEM((tm, tn), F32)] + comm.scratch(),
        compiler_params=_cp(*((["arbitrary"] * 3) if serial else ["parallel", "parallel", "arbitrary"])),
    )(*ins, *comm.args())
    nown = 1 + neo + nea
    own = res[0] if epi is None else list(res[:nown])
    return (own, list(res[nown:])) if carried else own


def _rows(fn, rows, consts=(), outs=(), accs=(), *, tile, name, halos=(), comm=None):
    rows = [r if isinstance(r, tuple) else (r, r.shape[1], 0) for r in rows]
    s = rows[0][0].shape[0]
    assert s % tile == 0 and tile % 8 == 0
    n = s // tile
    t8 = tile // 8
    nr, nh, nc_, no, na = len(rows), len(halos), len(consts), len(outs), len(accs)
    carried = comm is not None
    comm = comm or _Comm()

    def body(*refs):
        refs, cm = comm.split(refs, nr + nh + nc_, no + na, 0)
        i = pl.program_id(0)
        comm.start_at(i == 0, cm)
        rv = [r[...] for r in refs[:nr]]
        hv = []
        for (idx, kind), r in zip(halos, refs[nr:nr + nh]):
            edge = (i == 0) if kind == "prev" else (i == n - 1)
            hv.append(jnp.where(edge, 0.0, r[...]))
        cv = [r[...] for r in refs[nr + nh:nr + nh + nc_]]
        o_refs = refs[nr + nh + nc_:nr + nh + nc_ + no]
        a_refs = refs[nr + nh + nc_ + no:]
        ov, av = fn(rv, hv, cv)
        for r, v in zip(o_refs, ov):
            r[...] = v.astype(r.dtype)
        if na:
            @pl.when(i == 0)
            def _():
                for r in a_refs:
                    r[...] = jnp.zeros_like(r)
            for r, v in zip(a_refs, av):
                r[...] += v
        comm.wait_at(i == n - 1, cm)

    in_specs = [pl.BlockSpec((tile, w), functools.partial(lambda i, cb: (i, cb), cb=cb)) for (_, w, cb) in rows]
    for idx, kind in halos:
        _, w, cb = rows[idx]
        if kind == "prev":
            in_specs.append(pl.BlockSpec((8, w), functools.partial(lambda i, cb: (jnp.maximum(i * t8 - 1, 0), cb), cb=cb)))
        else:
            in_specs.append(pl.BlockSpec((8, w), functools.partial(lambda i, cb: (jnp.minimum((i + 1) * t8, n * t8 - 1), cb), cb=cb)))
    in_specs += [pl.BlockSpec(c.shape, functools.partial(lambda i, nd: (0,) * nd, nd=c.ndim)) for c in consts]
    out_specs = [pl.BlockSpec((tile, c), lambda i: (i, 0)) for (c, _) in outs]
    out_specs += [pl.BlockSpec((r, c), lambda i: (0, 0)) for (r, c) in accs]
    out_shape = [jax.ShapeDtypeStruct((s, c), dt) for (c, dt) in outs]
    out_shape += [jax.ShapeDtypeStruct((r, c), F32) for (r, c) in accs]
    args = [r[0] for r in rows] + [rows[idx][0] for idx, _ in halos] + list(consts)
    res = pl.pallas_call(
        body, name=name, grid=(n,), in_specs=in_specs + [ANY] * comm.n, out_specs=out_specs + [ANY] * comm.n,
        out_shape=out_shape + comm.out_shape(), scratch_shapes=comm.scratch(), compiler_params=_cp("arbitrary"),
    )(*args, *comm.args())
    return (list(res[:no + na]), list(res[no + na:])) if carried else list(res)


def _colsum8(v):
    t, c = v.shape
    return jnp.sum(v.reshape(t // 8, 8, c), axis=0)


def _rms(x, g):
    return x * lax.rsqrt(jnp.mean(x * x, axis=-1, keepdims=True) + NORM_EPS) * g


def _epi_rms(g):
    return (lambda r, rows, consts: (r, [_rms(r, consts[0])], []), [], [g.reshape(1, -1)], [(g.shape[-1], MXU)], [])


def _epi_rms_bwd(x, g, dres):
    def fn(r, rows, consts):
        xb, drb = rows
        _, vjp = jax.vjp(_rms, xb, consts[0])
        rstd = lax.rsqrt(jnp.mean(xb * xb, axis=-1, keepdims=True) + NORM_EPS)
        return drb + vjp(r)[0], [], [_colsum8(r * xb * rstd)]
    return (fn, [x, dres], [g.reshape(1, -1)], [], [(8, g.shape[-1])])


def _epi_att_stats(att, lse):
    def fn(r, rows, consts):
        hr = lax.broadcasted_iota(jnp.int32, (ATT_W, ATT_W), 0) // HEAD_DIM
        hc = lax.broadcasted_iota(jnp.int32, (ATT_W, ATT_W), 1) // HEAD_DIM
        delta = _nn(r[:, :ATT_W] * rows[0], (hr == hc).astype(F32), HIGH)
        lane = lax.broadcasted_iota(jnp.int32, delta.shape, 1)
        return r, [jnp.where(lane % HEAD_DIM < HEAD_DIM // 2, rows[1], delta)], []
    return (fn, [att, lse], [], [(ATT_W, F32)], [])


def _epi_wire(width):
    return (lambda r, rows, consts: (r, [r], []), [], [], [(width, MXU)], [])


def _rms_fwd(x, g, name, comm=None):
    def fn(rv, hv, cv):
        return [_rms(rv[0], cv[0])], []
    res = _rows(fn, [x], [g.reshape(1, -1)], [(x.shape[1], MXU)], tile=512, name=name, comm=comm)
    return res[0] if comm is None else (res[0][0], res[1])


def _slope_dist(hp, hh, dist, dil):
    hf = (2 * hp + hh + 1).astype(F32)
    slope = jnp.exp(jnp.zeros(dist.shape, F32) - hf * LN2)
    return slope * (dist.astype(F32) * float(dil))


ATT_G = 2048


def _att_fwd_fused(proj, name, comm=None):
    s, npc = proj.shape
    gsz = ATT_G
    ng = s // gsz
    assert s % gsz == 0
    scale = HEAD_DIM ** -0.5
    comm = comm or _Comm()

    def body(*refs):
        (q_ref, kp_ref, kc_ref, vp_ref, vc_ref, att_ref, lse_ref, attb_ref, nn, mn, dn), cm = comm.split(refs, 5, 3, 3)
        hp, g = pl.program_id(0), pl.program_id(1)
        comm.start_at((hp == 0) & (g == 0), cm)
        lane = lax.broadcasted_iota(jnp.int32, (BLK, BLK), 1)
        qi = lax.broadcasted_iota(jnp.int32, (BLK, 2 * BLK), 0)
        ki = lax.broadcasted_iota(jnp.int32, (BLK, 2 * BLK), 1)
        dist = BLK + qi - ki
        band = (dist >= 0) & (dist <= BLK)
        for pi, (_, dil) in enumerate(ATT_PATTERNS):
            nbg = gsz // dil // BLK
            bias = [_slope_dist(hp, hh, dist, dil) for hh in (0, 1)]
            for r in range(dil):
                for b in range(nbg):
                    def rows(blk):
                        return pl.ds(blk * BLK * dil + r, BLK, stride=dil) if dil > 1 else pl.ds(blk * BLK, BLK)
                    q = q_ref[rows(b), :]
                    k_prev = kp_ref[rows(nbg - 1), :] if b == 0 else kc_ref[rows(b - 1), :]
                    v_prev = vp_ref[rows(nbg - 1), :] if b == 0 else vc_ref[rows(b - 1), :]
                    kk = jnp.concatenate([k_prev, kc_ref[rows(b), :]], axis=0).astype(MXU)
                    vv = jnp.concatenate([v_prev, vc_ref[rows(b), :]], axis=0).astype(MXU)
                    valid = (band & ((g > 0) | (ki >= BLK))) if b == 0 else band
                    num = jnp.zeros((BLK, BLK), F32)
                    mx = jnp.zeros((BLK, BLK), F32)
                    den = jnp.zeros((BLK, BLK), F32)
                    for hh in (0, 1):
                        hmask = (lane < HEAD_DIM) if hh == 0 else (lane >= HEAD_DIM)
                        qm = jnp.where(hmask, q, 0.0).astype(MXU)
                        sc = jnp.where(valid, _nt(qm, kk) * scale - bias[hh], NEG)
                        m = jnp.max(sc, axis=1, keepdims=True)
                        p = jnp.exp(sc - m)
                        dn_ = jnp.sum(p, axis=1, keepdims=True)
                        o = _nn(p.astype(MXU), vv)
                        num = jnp.where(hmask, o, num)
                        mx = jnp.where(hmask, m, mx)
                        den = jnp.where(hmask, dn_, den)
                    nn.at[pi][rows(b), :] = num
                    mn.at[pi][rows(b), :] = mx
                    dn.at[pi][rows(b), :] = den

        def merge(c, carry):
            rows = pl.ds(pl.multiple_of(c * 256, 256), 256)
            ms = [mn[pi, rows, :] for pi in range(len(ATT_PATTERNS))]
            m_all = functools.reduce(jnp.maximum, ms)
            num = jnp.zeros((256, BLK), F32)
            den = jnp.zeros((256, BLK), F32)
            for pi in range(len(ATT_PATTERNS)):
                e = jnp.exp(ms[pi] - m_all)
                num = num + nn[pi, rows, :] * e
                den = den + dn[pi, rows, :] * e
            att = num / den
            att_ref[rows, :] = att
            attb_ref[rows, :] = att.astype(MXU)
            lse_ref[rows, :] = m_all + jnp.log(den)
            return carry

        lax.fori_loop(0, gsz // 256, merge, 0)
        comm.wait_at((hp == 3) & (g == ng - 1), cm)

    def cur(base):
        return pl.BlockSpec((gsz, BLK), lambda hp, g: (g, base // BLK + hp))

    def prev(base):
        return pl.BlockSpec((gsz, BLK), lambda hp, g: (jnp.maximum(g - 1, 0), base // BLK + hp))

    o_spec = pl.BlockSpec((gsz, BLK), lambda hp, g: (g, hp))
    npat = len(ATT_PATTERNS)
    res = pl.pallas_call(
        body, name=name, grid=(4, ng),
        in_specs=[cur(C_Q), prev(C_K), cur(C_K), prev(C_V), cur(C_V)] + [ANY] * comm.n,
        out_specs=[o_spec] * 3 + [ANY] * comm.n,
        out_shape=[jax.ShapeDtypeStruct((s, ATT_W), F32)] * 2 + [jax.ShapeDtypeStruct((s, ATT_W), MXU)] + comm.out_shape(),
        scratch_shapes=[pltpu.VMEM((npat, gsz, BLK), F32)] * 3 + comm.scratch(),
        compiler_params=_cp("arbitrary", "arbitrary"),
    )(proj, proj, proj, proj, proj, *comm.args())
    return res[0], res[1], res[2], list(res[3:])


def _att_bwd_rev(proj, datt, stats, name, comm=None):
    s, npc = proj.shape
    gsz = ATT_G
    ng = s // gsz
    npat = len(ATT_PATTERNS)
    scale = HEAD_DIM ** -0.5
    comm = comm or _Comm()

    def body(*refs):
        (q_ref, kp_ref, kc_ref, vp_ref, vc_ref, do_ref, st_ref, dq_out, dk_out, dv_out,
         kcar, vcar, dq_ref, dk_ref, dv_ref), cm = comm.split(refs, 7, 3, 5)
        hp, gi = pl.program_id(0), pl.program_id(1)
        g = ng - 1 - gi
        comm.start_at((hp == 0) & (gi == 0), cm)

        @pl.when(gi == 0)
        def _():
            kcar[...] = jnp.zeros_like(kcar)
            vcar[...] = jnp.zeros_like(vcar)

        lane = lax.broadcasted_iota(jnp.int32, (BLK, BLK), 1)
        qi = lax.broadcasted_iota(jnp.int32, (BLK, 2 * BLK), 0)
        ki = lax.broadcasted_iota(jnp.int32, (BLK, 2 * BLK), 1)
        dist = BLK + qi - ki
        band = (dist >= 0) & (dist <= BLK)
        for acc in (dq_ref, dk_ref, dv_ref):
            acc[...] = jnp.zeros_like(acc)
        for pi, (_, dil) in enumerate(ATT_PATTERNS):
            nbg = gsz // dil // BLK
            bias = [_slope_dist(hp, hh, dist, dil) for hh in (0, 1)]
            for r in range(dil):
                edge = slice(pi * gsz + r * BLK, pi * gsz + (r + 1) * BLK)
                for b in reversed(range(nbg)):
                    def rows(blk):
                        return pl.ds(blk * BLK * dil + r, BLK, stride=dil) if dil > 1 else pl.ds(blk * BLK, BLK)
                    q, do, st = q_ref[rows(b), :], do_ref[rows(b), :], st_ref[rows(b), :]
                    k_prev = kp_ref[rows(nbg - 1), :] if b == 0 else kc_ref[rows(b - 1), :]
                    v_prev = vp_ref[rows(nbg - 1), :] if b == 0 else vc_ref[rows(b - 1), :]
                    kk = jnp.concatenate([k_prev, kc_ref[rows(b), :]], axis=0).astype(MXU)
                    vv = jnp.concatenate([v_prev, vc_ref[rows(b), :]], axis=0).astype(MXU)
                    valid = (band & ((g > 0) | (ki >= BLK))) if b == 0 else band
                    dq = jnp.zeros((BLK, BLK), F32)
                    dkk = jnp.zeros((2 * BLK, BLK), F32)
                    dvv = jnp.zeros((2 * BLK, BLK), F32)
                    for hh in (0, 1):
                        c0 = hh * HEAD_DIM
                        hmask = (lane < HEAD_DIM) if hh == 0 else (lane >= HEAD_DIM)
                        qm = jnp.where(hmask, q, 0.0).astype(MXU)
                        dom = jnp.where(hmask, do, 0.0).astype(MXU)
                        sc = _nt(qm, kk) * scale - bias[hh]
                        p = jnp.exp(jnp.where(valid, sc - st[:, c0:c0 + 1], NEG))
                        ds = (p * (_nt(dom, vv) - st[:, c0 + HEAD_DIM // 2:c0 + HEAD_DIM // 2 + 1])).astype(MXU)
                        dq = jnp.where(hmask, _nn(ds, kk), dq)
                        dkk = dkk + _tn(ds, qm)
                        dvv = dvv + _tn(p.astype(MXU), dom)
                    dq_ref[rows(b), :] += dq * scale
                    own_k, own_v = dkk[BLK:] * scale, dvv[BLK:]
                    if b == nbg - 1:
                        own_k, own_v = own_k + kcar[edge, :], own_v + vcar[edge, :]
                    dk_ref[rows(b), :] += own_k
                    dv_ref[rows(b), :] += own_v
                    if b > 0:
                        dk_ref[rows(b - 1), :] += dkk[:BLK] * scale
                        dv_ref[rows(b - 1), :] += dvv[:BLK]
                    else:
                        kcar[edge, :] = dkk[:BLK] * scale
                        vcar[edge, :] = dvv[:BLK]
        for out, acc in ((dq_out, dq_ref), (dk_out, dk_ref), (dv_out, dv_ref)):
            out[...] = acc[...].astype(out.dtype)
        comm.wait_at((hp == 3) & (gi == ng - 1), cm)

    def pspec(base, shift):
        return pl.BlockSpec((gsz, BLK), lambda hp, gi: (jnp.maximum(ng - 1 - gi + shift, 0), base // BLK + hp))

    wspec = pl.BlockSpec((gsz, BLK), lambda hp, gi: (ng - 1 - gi, hp))
    in_specs = [pspec(C_Q, 0), pspec(C_K, -1), pspec(C_K, 0), pspec(C_V, -1), pspec(C_V, 0), wspec, wspec] + [ANY] * comm.n
    res = pl.pallas_call(
        body, name=name, grid=(4, ng), in_specs=in_specs,
        out_specs=[wspec] * 3 + [ANY] * comm.n,
        out_shape=[jax.ShapeDtypeStruct((s, ATT_W), MXU)] * 3 + comm.out_shape(),
        scratch_shapes=[pltpu.VMEM((npat * gsz, BLK), F32)] * 2 + [pltpu.VMEM((gsz, BLK), F32)] * 3 + comm.scratch(),
        compiler_params=_cp("arbitrary", "arbitrary"),
    )(proj, proj, proj, proj, proj, datt, stats, *comm.args())
    return res[0], res[1], res[2], list(res[3:])


def _shift_down(cur, halo, sft):
    if sft == 0:
        return cur
    t = cur.shape[0]
    rolled = pltpu.roll(cur, sft, 0)
    hr = pltpu.roll(halo, sft, 0)
    row = lax.broadcasted_iota(jnp.int32, cur.shape, 0)
    return jnp.where(row < sft, jnp.tile(hr, (t // 8, 1)), rolled)


def _shift_up(cur, halo, sft):
    if sft == 0:
        return cur
    t = cur.shape[0]
    rolled = pltpu.roll(cur, t - sft, 0)
    hr = pltpu.roll(halo, 8 - sft, 0)
    row = lax.broadcasted_iota(jnp.int32, cur.shape, 0)
    return jnp.where(row >= t - sft, jnp.tile(hr, (t // 8, 1)), rolled)


def _conv(x, xh, w, b):
    y = b + x * w[CONV_K - 1:CONV_K]
    for k in range(CONV_K - 1):
        y = y + _shift_down(x, xh, CONV_K - 1 - k) * w[k:k + 1]
    return y


def _conv_bwd(x, xh, dy, dyh, w):
    dx = dy * w[CONV_K - 1:CONV_K]
    dws = []
    for k in range(CONV_K - 1):
        sft = CONV_K - 1 - k
        dx = dx + _shift_up(dy, dyh, sft) * w[k:k + 1]
        dws.append(jnp.sum(dy * _shift_down(x, xh, sft), axis=0, keepdims=True))
    dws.append(jnp.sum(dy * x, axis=0, keepdims=True))
    c = x.shape[1]
    dw = jnp.concatenate(dws + [jnp.zeros((8 - CONV_K, c), F32)], axis=0)
    return dx, dw, jnp.sum(dy, axis=0, keepdims=True)


def _pad8(w):
    return jnp.concatenate([w, jnp.zeros((8 - w.shape[0], w.shape[1]), w.dtype)], axis=0)


def _ssd_pre(proj, conv_w, conv_b, dt_bias128, name):
    def fn(rv, hv, cv):
        xbc, dtr = rv
        return [_silu(_conv(xbc, hv[0], cv[0], cv[1])), _softplus(dtr + cv[2])], []
    return _rows(fn, [(proj, 1024, C_XBC // 1024), (proj, BLK, C_DT // BLK)],
                 [_pad8(conv_w), conv_b.reshape(1, -1), dt_bias128],
                 [(1024, F32), (BLK, F32)], tile=256, name=name, halos=[(0, "prev")])


def _ssd_pre_bwd(proj, dxc, ddt, conv_w, conv_b, dt_bias128, name):
    def fn(rv, hv, cv):
        xbc, dtr, dxcb, ddtb = rv
        xh, dxch_raw, xnext = hv
        w, b, bias = cv
        pre = _conv(xbc, xh, w, b)
        sg = _sigmoid(pre)
        dpre = dxcb * (sg * (1.0 + pre * (1.0 - sg)))
        t = xbc.shape[0]
        tail = jnp.concatenate([xbc[t - 8:], xnext], axis=0)
        pre_n = _conv(tail[8:], tail[:8], w, b)
        sgn = _sigmoid(pre_n)
        dpre_h = dxch_raw * (sgn * (1.0 + pre_n * (1.0 - sgn)))
        dx, dw, db = _conv_bwd(xbc, xh, dpre, dpre_h, w)
        ddr = ddtb * _sigmoid(dtr + bias)
        return [dx, ddr], [dw, jnp.concatenate([db, jnp.zeros((7, db.shape[1]), F32)], axis=0), _colsum8(ddr)]
    return _rows(fn, [(proj, 1024, C_XBC // 1024), (proj, BLK, C_DT // BLK), dxc, ddt],
                 [_pad8(conv_w), conv_b.reshape(1, -1), dt_bias128],
                 [(1024, MXU), (BLK, MXU)], [(8, 1024), (8, 1024), (8, BLK)], tile=256, name=name,
                 halos=[(0, "prev"), (2, "next"), (0, "next")])


SSD_CPB = 1


def _head_cols(v, h0):
    lane = lax.broadcasted_iota(jnp.int32, (v.shape[0], BLK), 1)
    return jnp.where(lane < HEAD_DIM, v[:, h0:h0 + 1], v[:, h0 + 1:h0 + 2])


def _ssd_scan(xc, dt, par, name):
    s = xc.shape[0]
    nc = s // BLK

    def body(x_ref, dt_ref, par_ref, y_ref, st_ref, h_ref):
        c = pl.program_id(0)

        @pl.when(c == 0)
        def _():
            h_ref[...] = jnp.zeros_like(h_ref)

        st_ref[0] = h_ref[...]
        dt = dt_ref[...]
        a_row = -jnp.exp(par_ref[0:1, :])
        d_row = par_ref[1:2, :]
        ri = lax.broadcasted_iota(jnp.int32, (BLK, BLK), 0)
        ci = lax.broadcasted_iota(jnp.int32, (BLK, BLK), 1)
        tril = ri >= ci
        cs = _nn(tril.astype(F32), dt * a_row, HI)
        cst, dtt = cs.T, dt.T
        last = cs[BLK - 1:BLK, :]
        wcol = jnp.exp(last - cs) * dt
        ecs = jnp.exp(cs)
        elast = jnp.exp(last)
        for g in (0, 1):
            bg = x_ref[:, 512 + g * BLK:512 + (g + 1) * BLK].astype(MXU)
            cg = x_ref[:, 768 + g * BLK:768 + (g + 1) * BLK].astype(MXU)
            gm = _nt(cg, bg)
            for pp in (0, 1):
                pr = 2 * g + pp
                h0 = 2 * pr
                x2 = x_ref[:, pr * BLK:(pr + 1) * BLK]
                hprev = h_ref[pr * BLK:(pr + 1) * BLK, :]
                yp = jnp.zeros((BLK, BLK), F32)
                for hh in (0, 1):
                    h = h0 + hh
                    hmask = (ci < HEAD_DIM) if hh == 0 else (ci >= HEAD_DIM)
                    lm = jnp.exp(jnp.where(tril, cs[:, h:h + 1] - cst[h:h + 1, :], NEG))
                    mm = gm * lm * dtt[h:h + 1, :]
                    yp = yp + _nn(mm.astype(MXU), jnp.where(hmask, x2, 0.0).astype(MXU))
                y0 = _nt(cg, hprev.astype(MXU))
                y_ref[:, pr * BLK:(pr + 1) * BLK] = yp + _head_cols(ecs, h0) * y0 + _head_cols(d_row, h0) * x2
                dec = jnp.where(ri < HEAD_DIM, elast[:, h0:h0 + 1], elast[:, h0 + 1:h0 + 2])
                xw = (x2 * _head_cols(wcol, h0)).astype(MXU)
                h_ref[pr * BLK:(pr + 1) * BLK, :] = dec * hprev + _tn(xw, bg)

    return pl.pallas_call(
        body, name=name, grid=(nc,),
        in_specs=[pl.BlockSpec((BLK, 1024), lambda c: (c, 0)), pl.BlockSpec((BLK, BLK), lambda c: (c, 0)),
                  pl.BlockSpec((8, BLK), lambda c: (0, 0))],
        out_specs=[pl.BlockSpec((BLK, SSD_W), lambda c: (c, 0)), pl.BlockSpec((1, SSD_W, SSD_STATE), lambda c: (c, 0, 0))],
        out_shape=[jax.ShapeDtypeStruct((s, SSD_W), F32), jax.ShapeDtypeStruct((nc, SSD_W, SSD_STATE), F32)],
        scratch_shapes=[pltpu.VMEM((SSD_W, SSD_STATE), F32)],
        compiler_params=_cp("arbitrary"),
    )(xc, dt, par)


def _ssd_scan_bwd(xc, dt, par, st, dy, name, comm=None):
    s = xc.shape[0]
    cpb = SSD_CPB
    nb = s // (cpb * BLK)
    comm = comm or _Comm()

    def chunk(x_ref, dt_ref, par_ref, st_ref, dy_ref, dx_ref, ddt_ref, dal_ref, dd_ref, dh_ref):
        dt = dt_ref[...]
        a_row = -jnp.exp(par_ref[0:1, :])
        d_row = par_ref[1:2, :]
        ri = lax.broadcasted_iota(jnp.int32, (BLK, BLK), 0)
        ci = lax.broadcasted_iota(jnp.int32, (BLK, BLK), 1)
        tril = ri >= ci
        cs = _nn(tril.astype(F32), dt * a_row, HI)
        cst, dtt = cs.T, dt.T
        last = cs[BLK - 1:BLK, :]
        tolast = jnp.exp(last - cs)
        wcol = tolast * dt
        ecs = jnp.exp(cs)
        elast = jnp.exp(last)
        dcs_col = jnp.zeros((BLK, BLK), F32)
        ddt_col = jnp.zeros((BLK, BLK), F32)
        dcs_row = jnp.zeros((BLK, BLK), F32)
        ddt_row = jnp.zeros((BLK, BLK), F32)
        dlast = jnp.zeros((1, BLK), F32)
        ddsk = jnp.zeros((1, BLK), F32)
        for g in (0, 1):
            bg32 = x_ref[:, 512 + g * BLK:512 + (g + 1) * BLK]
            cg32 = x_ref[:, 768 + g * BLK:768 + (g + 1) * BLK]
            bg, cg = bg32.astype(MXU), cg32.astype(MXU)
            gm = _nt(cg, bg)
            dgm = jnp.zeros((BLK, BLK), F32)
            dbg = jnp.zeros((BLK, BLK), F32)
            dcg = jnp.zeros((BLK, BLK), F32)
            for pp in (0, 1):
                pr = 2 * g + pp
                h0 = 2 * pr
                x2 = x_ref[:, pr * BLK:(pr + 1) * BLK]
                dy2 = dy_ref[:, pr * BLK:(pr + 1) * BLK]
                hprev = st_ref[0, pr * BLK:(pr + 1) * BLK, :]
                dhn = dh_ref[pr * BLK:(pr + 1) * BLK, :]
                x2m, dhnm = x2.astype(MXU), dhn.astype(MXU)
                zb = _nt(bg, dhnm)
                y0 = _nt(cg, hprev.astype(MXU))
                esel = _head_cols(ecs, h0)
                wsel = _head_cols(wcol, h0)
                dx2 = _head_cols(d_row, h0) * dy2 + wsel * zb
                pick2 = (((ri < HEAD_DIM) & (ci == h0)) | ((ri >= HEAD_DIM) & (ci == h0 + 1))).astype(F32)
                sums = _nn(jnp.concatenate([dy2 * y0, x2 * zb, dy2 * x2], axis=0), pick2, HIGH)
                de2, dw2, dd2 = sums[:BLK], sums[BLK:2 * BLK], sums[2 * BLK:]
                v2 = dw2 * wcol
                dcs_col = dcs_col + ecs * de2 - v2
                ddt_col = ddt_col + dw2 * tolast
                hsum = _nn(dhn * hprev, jnp.ones((BLK, BLK), F32), HIGH)
                dlast = dlast + elast * jnp.sum(jnp.where(pick2 > 0.0, hsum, 0.0), axis=0, keepdims=True) \
                    + jnp.sum(v2, axis=0, keepdims=True)
                ddsk = ddsk + jnp.sum(dd2, axis=0, keepdims=True)
                ts = []
                for hh in (0, 1):
                    h = h0 + hh
                    hmask = (ci < HEAD_DIM) if hh == 0 else (ci >= HEAD_DIM)
                    ons = (ri == h).astype(F32)
                    dym = jnp.where(hmask, dy2, 0.0).astype(MXU)
                    dt_r = dtt[h:h + 1, :]
                    lm = jnp.exp(jnp.where(tril, cs[:, h:h + 1] - cst[h:h + 1, :], NEG))
                    mm = gm * lm * dt_r
                    dx2 = dx2 + _tn(mm.astype(MXU), dym)
                    dm = _nt(dym, x2m)
                    t1 = dm * lm
                    dgm = dgm + t1 * dt_r
                    tt = t1 * gm
                    ddt_row = ddt_row + ons * jnp.sum(tt, axis=0, keepdims=True)
                    t = tt * dt_r
                    dcs_row = dcs_row - ons * jnp.sum(t, axis=0, keepdims=True)
                    ts.append(t)
                rows2 = lax.broadcasted_iota(jnp.int32, (2 * BLK, BLK), 0)
                lane2 = lax.broadcasted_iota(jnp.int32, (2 * BLK, BLK), 1)
                to_lane = ((rows2 < BLK) & (lane2 == h0)) | ((rows2 >= BLK) & (lane2 == h0 + 1))
                dcs_col = dcs_col + _nn(jnp.concatenate(ts, axis=1), to_lane.astype(F32), HIGH)
                dx_ref[:, pr * BLK:(pr + 1) * BLK] = dx2
                edy = (esel * dy2).astype(MXU)
                dcg = dcg + _nn(edy, hprev.astype(MXU))
                dec = jnp.where(ri < HEAD_DIM, elast[:, h0:h0 + 1], elast[:, h0 + 1:h0 + 2])
                dh_ref[pr * BLK:(pr + 1) * BLK, :] = dec * dhn + _tn(edy, cg)
                dbg = dbg + _nn((x2 * wsel).astype(MXU), dhnm)
            dgmm = dgm.astype(MXU)
            dx_ref[:, 512 + g * BLK:512 + (g + 1) * BLK] = dbg + _tn(dgmm, cg)
            dx_ref[:, 768 + g * BLK:768 + (g + 1) * BLK] = dcg + _nn(dgmm, bg)
        dcs = dcs_col + dcs_row.T + jnp.where(ri == BLK - 1, dlast, 0.0)
        dda = _nn((ri <= ci).astype(F32), dcs, HI)
        ddt_ref[...] = ddt_col + ddt_row.T + a_row * dda
        da = jnp.sum(dt * dda, axis=0, keepdims=True)
        dal_ref[0:1, :] += da * a_row
        dd_ref[0:1, :] += ddsk

    def body(*refs):
        (x_ref, dt_ref, par_ref, st_ref, dy_ref, dx_ref, ddt_ref, dal_ref, dd_ref, dh_ref), cm = comm.split(refs, 5, 4, 1)
        c = pl.program_id(0)
        comm.start_at(c == 0, cm)

        @pl.when(c == 0)
        def _():
            dh_ref[...] = jnp.zeros_like(dh_ref)
            dal_ref[...] = jnp.zeros_like(dal_ref)
            dd_ref[...] = jnp.zeros_like(dd_ref)

        for cc in reversed(range(cpb)):
            rows = pl.ds(cc * BLK, BLK)
            chunk(x_ref.at[rows], dt_ref.at[rows], par_ref, st_ref.at[pl.ds(cc, 1)], dy_ref.at[rows], dx_ref.at[rows],
                  ddt_ref.at[rows], dal_ref, dd_ref, dh_ref)
        comm.wait_at(c == nb - 1, cm)

    rev = lambda c: (nb - 1 - c, 0)
    tb = cpb * BLK
    res = pl.pallas_call(
        body, name=name, grid=(nb,),
        in_specs=[pl.BlockSpec((tb, 1024), rev), pl.BlockSpec((tb, BLK), rev), pl.BlockSpec((8, BLK), lambda c: (0, 0)),
                  pl.BlockSpec((cpb, SSD_W, SSD_STATE), lambda c: (nb - 1 - c, 0, 0)), pl.BlockSpec((tb, SSD_W), rev)]
        + [ANY] * comm.n,
        out_specs=[pl.BlockSpec((tb, 1024), rev), pl.BlockSpec((tb, BLK), rev),
                   pl.BlockSpec((8, BLK), lambda c: (0, 0)), pl.BlockSpec((8, BLK), lambda c: (0, 0))] + [ANY] * comm.n,
        out_shape=[jax.ShapeDtypeStruct((s, 1024), F32), jax.ShapeDtypeStruct((s, BLK), F32),
                   jax.ShapeDtypeStruct((8, BLK), F32), jax.ShapeDtypeStruct((8, BLK), F32)] + comm.out_shape(),
        scratch_shapes=[pltpu.VMEM((SSD_W, SSD_STATE), F32)] + comm.scratch(),
        compiler_params=_cp("arbitrary"),
    )(xc, dt, par, st, dy, *comm.args())
    return res[0], res[1], res[2], res[3], list(res[4:])


def _ssd_gate(y, z, w):
    t = y * _silu(z)
    outs = []
    for g in (0, 1):
        tg = t[:, g * 256:(g + 1) * 256]
        outs.append(tg * lax.rsqrt(jnp.mean(tg * tg, axis=-1, keepdims=True) + SSD_NORM_EPS))
    return jnp.concatenate(outs, axis=1) * w


def _ssd_post(y, proj, norm_w, name):
    def fn(rv, hv, cv):
        return [_ssd_gate(rv[0], rv[1], cv[0])], []
    return _rows(fn, [y, (proj, SSD_W, C_Z // SSD_W)], [norm_w.reshape(1, -1)], [(SSD_W, MXU)], tile=512, name=name)[0]


def _ssd_post_bwd(y, proj, norm_w, dout, name):
    def fn(rv, hv, cv):
        yb, zb, db = rv
        _, vjp = jax.vjp(lambda a, b: _ssd_gate(a, b, cv[0]), yb, zb)
        dy, dz = vjp(db)
        t = yb * _silu(zb)
        nrm = []
        for g in (0, 1):
            tg = t[:, g * 256:(g + 1) * 256]
            nrm.append(tg * lax.rsqrt(jnp.mean(tg * tg, axis=-1, keepdims=True) + SSD_NORM_EPS))
        return [dy, dz], [_colsum8(db * jnp.concatenate(nrm, axis=1))]
    return _rows(fn, [y, (proj, SSD_W, C_Z // SSD_W), dout], [norm_w.reshape(1, -1)],
                 [(SSD_W, F32), (SSD_W, MXU)], [(8, SSD_W)], tile=512, name=name)


LRU_T = 256


def _lru_conv(proj, conv_w, conv_b, name):
    def fn(rv, hv, cv):
        return [_conv(rv[0], hv[0], cv[0], cv[1])], []
    return _rows(fn, [(proj, LRU_W, C_XL // LRU_W)], [_pad8(conv_w), conv_b.reshape(1, -1)], [(LRU_W, F32)],
                 tile=512, name=name, halos=[(0, "prev")])[0]


def _lru_conv_bwd(proj, dxc, conv_w, name):
    def fn(rv, hv, cv):
        dx, dw, db = _conv_bwd(rv[0], hv[0], rv[1], hv[1], cv[0])
        return [dx], [dw, jnp.concatenate([db, jnp.zeros((7, db.shape[1]), F32)], axis=0)]
    return _rows(fn, [(proj, LRU_W, C_XL // LRU_W), dxc], [_pad8(conv_w)], [(LRU_W, MXU)], [(8, LRU_W), (8, LRU_W)],
                 tile=512, name=name, halos=[(0, "prev"), (1, "next")])


def _lru_au(pre_a, pre_x, xc, ba, bx, lam):
    r = _sigmoid(pre_a + ba)
    i = _sigmoid(pre_x + bx)
    log_a = -LRU_C * r * _softplus(-lam)
    a = jnp.exp(log_a)
    u = jnp.sqrt(1.0 - jnp.exp(2.0 * log_a)) * (i * xc)
    return a, u


def _lru_scan(pre, xc, proj, par, name):
    s = xc.shape[0]
    t = LRU_T

    def body(pre_ref, xc_ref, g_ref, par_ref, out_ref, h_ref, carry):
        c = pl.program_id(0)

        @pl.when(c == 0)
        def _():
            carry[...] = jnp.zeros_like(carry)

        a, u = _lru_au(pre_ref[:, :LRU_W], pre_ref[:, LRU_W:], xc_ref[...], par_ref[0:1, :], par_ref[1:2, :], par_ref[2:3, :])
        row = lax.broadcasted_iota(jnp.int32, (t, LRU_W), 0)
        sft = 1
        while sft < t:
            keep = row >= sft
            a_s = jnp.where(keep, pltpu.roll(a, sft, 0), 1.0)
            u_s = jnp.where(keep, pltpu.roll(u, sft, 0), 0.0)
            u = a * u_s + u
            a = a * a_s
            sft *= 2
        h = a * carry[0:1, :] + u
        h_ref[...] = h
        out_ref[...] = (h * _gelu(g_ref[...])).astype(out_ref.dtype)
        carry[0:1, :] = h[t - 1:t, :]

    return pl.pallas_call(
        body, name=name, grid=(s // t,),
        in_specs=[pl.BlockSpec((t, 2 * LRU_W), lambda c: (c, 0)), pl.BlockSpec((t, LRU_W), lambda c: (c, 0)),
                  pl.BlockSpec((t, LRU_W), lambda c: (c, C_G // LRU_W)), pl.BlockSpec((8, LRU_W), lambda c: (0, 0))],
        out_specs=[pl.BlockSpec((t, LRU_W), lambda c: (c, 0))] * 2,
        out_shape=[jax.ShapeDtypeStruct((s, LRU_W), MXU), jax.ShapeDtypeStruct((s, LRU_W), F32)],
        scratch_shapes=[pltpu.VMEM((8, LRU_W), F32)],
        compiler_params=_cp("arbitrary"),
    )(pre, xc, proj, par)


def _lru_scan_bwd(pre, xc, proj, par, h, dout, name):
    s = xc.shape[0]
    t = LRU_T
    n = s // t
    t8 = t // 8

    def body(pre_ref, xc_ref, g_ref, par_ref, h_ref, hh_ref, do_ref, dpre_ref, dxc_ref, dg_ref, dpar_ref, carry):
        c = pl.program_id(0)

        @pl.when(c == 0)
        def _():
            carry[...] = jnp.zeros_like(carry)
            dpar_ref[...] = jnp.zeros_like(dpar_ref)

        pa, px, xcb = pre_ref[:, :LRU_W], pre_ref[:, LRU_W:], xc_ref[...]
        ba, bx, lam = par_ref[0:1, :], par_ref[1:2, :], par_ref[2:3, :]
        (a, u), vjp = jax.vjp(_lru_au, pa, px, xcb, ba, bx, lam)
        g = g_ref[...]
        hcur = h_ref[...]
        do = do_ref[...]
        _, gvjp = jax.vjp(_gelu, g)
        dg_ref[...] = gvjp(do * hcur)[0].astype(dg_ref.dtype)
        row = lax.broadcasted_iota(jnp.int32, (t, LRU_W), 0)
        v = do * _gelu(g) + jnp.where(row == t - 1, carry[0:1, :], 0.0)
        b = jnp.where(row == t - 1, 0.0, pltpu.roll(a, t - 1, 0))
        sft = 1
        while sft < t:
            keep = row < t - sft
            b_s = jnp.where(keep, pltpu.roll(b, t - sft, 0), 1.0)
            v_s = jnp.where(keep, pltpu.roll(v, t - sft, 0), 0.0)
            v = b * v_s + v
            b = b * b_s
            sft *= 2
        dh = v
        carry[0:1, :] = a[0:1, :] * dh[0:1, :]
        hhalo = jnp.where(c == n - 1, 0.0, hh_ref[...])
        hprev = _shift_down(hcur, hhalo, 1)
        dpa, dpx, dxc, dba, dbx, dlam = vjp((dh * hprev, dh))
        dpre_ref[:, :LRU_W] = dpa
        dpre_ref[:, LRU_W:] = dpx
        dxc_ref[...] = dxc
        dpar_ref[0:1, :] += dba
        dpar_ref[1:2, :] += dbx
        dpar_ref[2:3, :] += dlam

    rev = lambda c: (n - 1 - c, 0)
    return pl.pallas_call(
        body, name=name, grid=(n,),
        in_specs=[pl.BlockSpec((t, 2 * LRU_W), rev), pl.BlockSpec((t, LRU_W), rev),
                  pl.BlockSpec((t, LRU_W), lambda c: (n - 1 - c, C_G // LRU_W)), pl.BlockSpec((8, LRU_W), lambda c: (0, 0)),
                  pl.BlockSpec((t, LRU_W), rev),
                  pl.BlockSpec((8, LRU_W), lambda c: (jnp.maximum((n - 1 - c) * t8 - 1, 0), 0)),
                  pl.BlockSpec((t, LRU_W), lambda c: (n - 1 - c, dout.shape[1] // LRU_W - 1))],
        out_specs=[pl.BlockSpec((t, 2 * LRU_W), rev), pl.BlockSpec((t, LRU_W), rev), pl.BlockSpec((t, LRU_W), rev),
                   pl.BlockSpec((8, LRU_W), lambda c: (0, 0))],
        out_shape=[jax.ShapeDtypeStruct((s, 2 * LRU_W), F32), jax.ShapeDtypeStruct((s, LRU_W), F32),
                   jax.ShapeDtypeStruct((s, LRU_W), MXU), jax.ShapeDtypeStruct((8, LRU_W), F32)],
        scratch_shapes=[pltpu.VMEM((8, LRU_W), F32)],
        compiler_params=_cp("arbitrary"),
    )(pre, xc, proj, par, h, h, dout)


def _swiglu_act(gu, name):
    def fn(rv, hv, cv):
        return [_silu(rv[0].astype(F32)) * rv[1].astype(F32)], []
    return _rows(fn, [(gu, D_FF, 0), (gu, D_FF, 1)], [], [(D_FF, MXU)], tile=256, name=name)[0]


def _swiglu_bwd(gu, da, name):
    def fn(rv, hv, cv):
        gt, up, dab = [t.astype(F32) for t in rv]
        sg = _sigmoid(gt)
        dgate = dab * up * (sg * (1.0 + gt * (1.0 - sg)))
        dup = dab * (gt * sg)
        return [jnp.concatenate([dgate, dup], axis=1)], []
    return _rows(fn, [(gu, D_FF, 0), (gu, D_FF, 1), da], [], [(2 * D_FF, MXU)], tile=256, name=name)[0]


def _loss_head(x, g, target, name):
    d = x.shape[1]

    def fn(rv, hv, cv):
        xb, tb = rv
        y, vjp = jax.vjp(_rms, xb, cv[0])
        err = y - tb
        dy = err * (1.0 / d)
        dx, _ = vjp(dy)
        rstd = lax.rsqrt(jnp.mean(xb * xb, axis=-1, keepdims=True) + NORM_EPS)
        e2 = err * err * (0.5 / d)
        e2 = functools.reduce(lambda a, b: a + b, [e2[:, k * BLK:(k + 1) * BLK] for k in range(d // BLK)])
        return [dx], [_colsum8(dy * xb * rstd), _colsum8(e2)]
    return _rows(fn, [x, target], [g.reshape(1, -1)], [(d, F32)], [(8, d), (8, BLK)], tile=512, name=name)


ANY = pl.BlockSpec(memory_space=pl.ANY)


def _coords():
    return lax.axis_index("x"), lax.axis_index("y"), lax.axis_index("c")


class _Comm:
    def __init__(self, gathers=(), scatters=()):
        self.gathers = list(gathers)
        self.scatters = list(scatters)
        self.n = len(self.gathers) + len(self.scatters)

    def args(self):
        return [g[0] for g in self.gathers] + self.scatters

    def out_shape(self):
        out = [jax.ShapeDtypeStruct((4,) + (a.shape if l is None else a.shape[1:]), a.dtype) for a, l, _ in self.gathers]
        return out + [jax.ShapeDtypeStruct((3,) + a.shape[1:], a.dtype) for a in self.scatters]

    def scratch(self):
        if not self.n:
            return []
        return [pltpu.SemaphoreType.DMA((3 * self.n,)), pltpu.SemaphoreType.DMA((3 * self.n,)),
                pltpu.SemaphoreType.DMA((max(len(self.gathers), 1),)),
                pltpu.SemaphoreType.DMA((3 * self.n,)), pltpu.SemaphoreType.DMA((3 * self.n,))]

    def split(self, refs, n_in, n_out, n_scratch):
        refs = list(refs)
        n = self.n
        own = refs[:n_in] + refs[n_in + n:n_in + n + n_out] + refs[n_in + 2 * n + n_out:n_in + 2 * n + n_out + n_scratch]
        cm = (refs[n_in:n_in + n], refs[n_in + n + n_out:n_in + 2 * n + n_out], refs[n_in + 2 * n + n_out + n_scratch:])
        return own, cm

    def _copies(self, cm, arriving):
        ins, outs, (send, recv, local, _, _) = cm
        x, y, c = _coords()
        me = 2 * x + y
        chips = [(1 - x, y), (x, 1 - y), (1 - x, 1 - y)]
        remote, locals_ = [], []
        ng = len(self.gathers)
        for i in range(self.n):
            if i < ng:
                _, l, halved = self.gathers[i]
                slab = ins[i] if l is None else ins[i].at[l]
                if not arriving:
                    locals_.append(pltpu.make_async_copy(slab, outs[i].at[me], local.at[i]))
            for j, (px, py) in enumerate(chips):
                if i < ng:
                    slot = 2 * px + py if arriving else me
                    src, dst = (slab.at[c], outs[i].at[slot, c]) if halved else (slab, outs[i].at[slot])
                else:
                    src, dst = ins[i].at[2 * px + py], outs[i].at[j]
                remote.append(pltpu.make_async_remote_copy(src, dst, send.at[3 * i + j], recv.at[3 * i + j],
                                                           device_id=(px, py, c), device_id_type=MESH))
        return remote, locals_

    def _handovers(self, cm, arriving):
        _, outs, (_, _, _, send, recv) = cm
        x, y, c = _coords()
        chips = [(1 - x, y), (x, 1 - y), (1 - x, 1 - y)]
        cps = []
        for i, (_, _, halved) in enumerate(self.gathers):
            if halved:
                for j, (px, py) in enumerate(chips):
                    src = outs[i].at[2 * px + py, c]
                    dst = outs[i].at[2 * px + py, 1 - c if arriving else c]
                    cps.append(pltpu.make_async_remote_copy(src, dst, send.at[3 * i + j], recv.at[3 * i + j],
                                                            device_id=(x, y, 1 - c), device_id_type=MESH))
        return cps

    def start_at(self, cond, cm):
        def go():
            remote, locals_ = self._copies(cm, False)
            for cp in locals_ + remote:
                cp.start()

        if self.n:
            go() if cond is True else pl.when(cond)(go)

    def wait_at(self, cond, cm):
        def go():
            for cp in self._copies(cm, True)[0]:
                cp.wait_recv()
            handed = self._handovers(cm, False)
            for cp in handed:
                cp.start()
            for cp in self._handovers(cm, True):
                cp.wait_recv()
            remote, locals_ = self._copies(cm, False)
            for cp in handed + remote:
                cp.wait_send()
            for cp in locals_:
                cp.wait()

        if self.n:
            go() if cond is True else pl.when(cond)(go)


def _swap_sibling(arrs):
    n = len(arrs)

    def body(*refs):
        ins, outs, send, recv = refs[:n], refs[n:2 * n], refs[2 * n], refs[2 * n + 1]
        x, y, c = _coords()
        cps = [pltpu.make_async_remote_copy(ins[i], outs[i], send.at[i], recv.at[i], device_id=(x, y, 1 - c), device_id_type=MESH)
               for i in range(n)]
        for cp in cps:
            cp.start()
        for cp in cps:
            cp.wait_recv()
        for cp in cps:
            cp.wait_send()

    return list(pl.pallas_call(
        body, name="swap_sibling", in_specs=[ANY] * n, out_specs=[ANY] * n,
        out_shape=[jax.ShapeDtypeStruct(a.shape, a.dtype) for a in arrs],
        scratch_shapes=[pltpu.SemaphoreType.DMA((n,)), pltpu.SemaphoreType.DMA((n,))],
        compiler_params=pltpu.CompilerParams(has_side_effects=True),
    )(*arrs))


def _gather_small(gs):
    def body(g_ref, o_ref, send_sems, recv_sems, local_sem):
        x, y, c = _coords()
        me = 4 * x + 2 * y + c
        mine = pltpu.make_async_copy(g_ref, o_ref.at[me], local_sem)
        mine.start()
        sends = []
        for k in range(1, 8):
            px, py, pc = x ^ (k >> 2), y ^ ((k >> 1) & 1), c ^ (k & 1)
            sends.append((pltpu.make_async_remote_copy(g_ref, o_ref.at[me], send_sems.at[k - 1], recv_sems.at[k - 1],
                                                       device_id=(px, py, pc), device_id_type=MESH), 4 * px + 2 * py + pc, k))
        for cp, _, _ in sends:
            cp.start()
        for cp, src, k in sends:
            pltpu.make_async_remote_copy(g_ref, o_ref.at[src], send_sems.at[k - 1], recv_sems.at[k - 1],
                                         device_id=(x, y, c), device_id_type=MESH).wait_recv()
        for cp, _, _ in sends:
            cp.wait_send()
        mine.wait()

    return pl.pallas_call(
        body, name="gather_small", in_specs=[ANY], out_specs=ANY,
        out_shape=jax.ShapeDtypeStruct((8,) + gs.shape, gs.dtype),
        scratch_shapes=[pltpu.SemaphoreType.DMA((7,)), pltpu.SemaphoreType.DMA((7,)), pltpu.SemaphoreType.DMA],
        compiler_params=pltpu.CompilerParams(has_side_effects=True),
    )(gs)


def _sum_slots(own, others, name, tile):
    k, r, c = others.shape

    def body(*refs):
        if own is None:
            o_ref, out_ref = refs
            acc = o_ref[0].astype(F32)
            first = 1
        else:
            own_ref, o_ref, out_ref = refs
            acc = own_ref[...]
            first = 0
        for j in range(first, k):
            acc = acc + o_ref[j].astype(F32)
        out_ref[...] = acc

    row = pl.BlockSpec((tile, c), lambda i: (i, 0))
    specs = ([] if own is None else [row]) + [pl.BlockSpec((k, tile, c), lambda i: (0, i, 0))]
    args = ([] if own is None else [own]) + [others]
    return pl.pallas_call(body, name=name, grid=(r // tile,), in_specs=specs, out_specs=row,
                          out_shape=jax.ShapeDtypeStruct((r, c), F32), compiler_params=_cp("parallel"))(*args)


def _adamw(w, m, v, ga, gb, name, tile, rows_first=False):
    lead = 0 if rows_first else w.ndim - 2
    r, c = w.shape[-2:]

    def body(*refs):
        vals = [ref[0] if lead else ref[...] for ref in refs[:len(refs) - 4]]
        w_, m_, v_, g = vals[0], vals[1], vals[2], vals[3]
        if gb is not None:
            g = g + vals[4]
        nm = ADAM_B1 * m_ + (1.0 - ADAM_B1) * g
        nv = ADAM_B2 * v_ + (1.0 - ADAM_B2) * (g * g)
        d = -ADAM_LR * ((nm / BC1) / (jnp.sqrt(nv / BC2) + ADAM_EPS) + ADAM_WD * w_)
        for ref, val in zip(refs[len(refs) - 4:], (g, d, nm, nv)):
            if lead:
                ref[0] = val
            else:
                ref[...] = val

    if rows_first:
        row = pl.BlockSpec((tile,) + w.shape[1:], lambda i: (i, 0, 0))
        grid = (w.shape[0] // tile,)
    elif lead:
        row = pl.BlockSpec((1, tile, c), lambda l, i: (l, i, 0))
        grid = (w.shape[0], r // tile)
    else:
        row = pl.BlockSpec((tile, c), lambda i: (i, 0))
        grid = (r // tile,)
    args = [w, m, v, ga] + ([] if gb is None else [gb])
    return pl.pallas_call(body, name=name, grid=grid, in_specs=[row] * len(args), out_specs=[row] * 4,
                          out_shape=[jax.ShapeDtypeStruct(w.shape, F32)] * 4,
                          compiler_params=_cp(*(["parallel"] * len(grid))))(*args)


MATS = ("w_in", "w_out", "w_gate", "w_up", "w_down")
CONVS = ("ssd_conv_w", "lru_conv_w")
BIG = MATS + CONVS
TRANSPOSED = ("w_gate", "w_up")
COL_SHARDED = ("ssd_conv_w", "lru_conv_w")
W_IN_SHARD = IN_COLS // 4
W_IN_PAD = 1056
SMALL = ("norm_mix", "ssd_conv_b", "ssd_dt_bias", "ssd_a_log", "ssd_d", "ssd_norm", "lru_conv_b", "lru_wa", "lru_ba",
         "lru_wx", "lru_bx", "lru_lambda", "norm_ffn", "norm_final")
WEIGHTS = ("norm_mix", "w_in", "ssd_conv_w", "ssd_conv_b", "ssd_dt_bias", "ssd_a_log", "ssd_d", "ssd_norm", "lru_conv_w",
           "lru_conv_b", "lru_wa", "lru_ba", "lru_wx", "lru_bx", "lru_lambda", "w_out", "norm_ffn", "w_gate", "w_up",
           "w_down", "norm_final")
ROW_TILE = {"w_in": W_IN_SHARD, "w_out": 128, "w_gate": 352, "w_up": 352, "w_down": 352}
W_IN_ADAM_TILE = 54


def _pack(arrs, width, row_mult, dtype):
    flat = jnp.concatenate([a.reshape(-1).astype(dtype) for a in arrs])
    rows = -(-flat.shape[0] // width)
    rows = -(-rows // row_mult) * row_mult
    flat = jnp.pad(flat, (0, rows * width - flat.shape[0]))
    return flat.reshape(rows, width)


def _unpack(buf, shapes):
    flat = buf.reshape(-1)
    out, off = [], 0
    for shp in shapes:
        n = int(np.prod(shp))
        out.append(flat[off:off + n].reshape(shp))
        off += n
    return out


def _join(name, g4):
    if name in COL_SHARDED:
        return jnp.moveaxis(g4, 0, -2).reshape(g4.shape[1:-1] + (4 * g4.shape[-1],))
    return g4.reshape((4 * g4.shape[1],) + g4.shape[2:])


def _slabs(name, g):
    if name in COL_SHARDED:
        return jnp.moveaxis(g.reshape(g.shape[:-1] + (4, g.shape[-1] // 4)), -2, 0)
    return g.reshape((4, g.shape[0] // 4) + g.shape[1:])


def _w_in_rows(g4):
    def nat(lo, hi):
        out = []
        while lo < hi:
            j = lo // W_IN_SHARD
            stop = min(hi, (j + 1) * W_IN_SHARD)
            out.append((j, lo - j * W_IN_SHARD, stop - lo))
            lo = stop
        return out
    pieces = nat(0, 3072) + nat(3080, IN_COLS) + nat(3072, 3080)

    def body(g_ref, o_ref):
        row = 0
        for j, first, n in pieces:
            o_ref[row:row + n, :] = g_ref[j, first:first + n, :]
            row += n
        o_ref[row:, :] = jnp.zeros((NP - row, o_ref.shape[1]), o_ref.dtype)

    return pl.pallas_call(body, name="w_in_rows", out_shape=jax.ShapeDtypeStruct((NP, g4.shape[-1]), g4.dtype),
                          compiler_params=pltpu.CompilerParams(vmem_limit_bytes=VMEM_LIMIT))(g4)


def _w_in_slabs(gt):
    def kern(n):
        return n if n < 3072 else (C_DT + n - 3072 if n < 3080 else n - 8)
    slabs = []
    for j in range(4):
        lo, hi = j * W_IN_SHARD, (j + 1) * W_IN_SHARD
        cuts = sorted({lo, hi} | {c for c in (3072, 3080) if lo < c < hi})
        slabs.append(jnp.concatenate([gt[kern(a):kern(a) + b - a] for a, b in zip(cuts[:-1], cuts[1:])], axis=0))
    return jnp.stack(slabs, axis=0)


def _block_diag(w):
    eye = jnp.eye(LRU_BLOCKS, dtype=w.dtype)
    return jnp.einsum("ncd,nm->ncmd", w, eye).reshape(LRU_W, LRU_W)


def _block_diag_extract(g):
    g4 = g.reshape(LRU_BLOCKS, 64, LRU_BLOCKS, 64)
    return jnp.stack([g4[n, :, n, :] for n in range(LRU_BLOCKS)], axis=0)


def _lanes128(v):
    return jnp.pad(v, (0, BLK - v.shape[0])).reshape(1, BLK)


def _layer_mixers(x, p, comm=None, h=None):
    if h is None:
        h = _rms_fwd(x, p["norm_mix"], "rms_mix")
    proj = _mm(h, p["w_in_t"], tb=True, tm=1024, tn=1408, tk=1024, name="mm_in")
    att, lse, attb, got = _att_fwd_fused(proj, "att_fwd", comm)
    xconv, dt = _ssd_pre(proj, p["ssd_conv_w"], p["ssd_conv_b"], _lanes128(p["ssd_dt_bias"]), "ssd_pre")
    spar = jnp.concatenate([_lanes128(p["ssd_a_log"]), _lanes128(p["ssd_d"]), jnp.zeros((6, BLK), F32)], axis=0)
    y, states = _ssd_scan(xconv, dt, spar, "ssd_scan")
    ssd = _ssd_post(y, proj, p["ssd_norm"], "ssd_post")
    xc = _lru_conv(proj, p["lru_conv_w"], p["lru_conv_b"], "lru_conv")
    wab = jnp.concatenate([_block_diag(p["lru_wa"]), _block_diag(p["lru_wx"])], axis=1).astype(MXU)
    pre = _mm(xc, wab, tm=1024, tn=1024, tk=512, name="mm_lru")
    lpar = jnp.concatenate([p["lru_ba"].reshape(1, -1), p["lru_bx"].reshape(1, -1), p["lru_lambda"].reshape(1, -1),
                            jnp.zeros((5, LRU_W), F32)], axis=0)
    lru, hs = _lru_scan(pre, xc, proj, lpar, "lru_scan")
    mix = jnp.concatenate([attb, ssd, lru], axis=1)
    saved = dict(x=x, h=h, proj=proj, att=att, lse=lse, xconv=xconv, dt=dt, spar=spar, y=y, states=states, xc=xc, wab=wab,
                 pre=pre, lpar=lpar, hs=hs, mix=mix)
    return mix, saved, got


def _layer_ffn(x, mix, p, saved, comms=(None, None, None), next_norm=None):
    comm_out, comm, comm_down = comms
    x1 = _mm(mix, p["w_out"], add=x, tm=1024, tn=1024, tk=1536, name="mm_out", epi=_epi_rms(p["norm_ffn"]), comm=comm_out)
    (x1, h2), got_out = x1 if comm_out is not None else (x1, [])
    gu = _mm(h2, p["w_gu_t"], tb=True, out_dtype=MXU, tm=1024, tn=1408, tk=1024, name="mm_gu", comm=comm)
    gu, got = gu if comm is not None else (gu, [])
    act = _swiglu_act(gu, "swiglu_act")
    x2 = _mm(act, p["w_down"], add=x1, tm=1024, tn=1024, tk=2816, name="mm_down", comm=comm_down,
             epi=None if next_norm is None else _epi_rms(next_norm))
    x2, got_down = x2 if comm_down is not None else (x2, [])
    x2, h_next = x2 if next_norm is not None else (x2, None)
    saved.update(x1=x1, h2=h2, gu=gu, act=act)
    return x2, got_out + got + got_down, h_next


def _layer_bwd(dx2, p, sv, comm_ssd=None, comm_att=None, comm_tail=None):
    g = {}
    da = _mm(dx2, p["w_down"], tb=True, out_dtype=MXU, tm=1024, tn=1408, tk=1024, name="mm_d_act")
    g["w_down"], g["w_down@wire"] = _mm(sv["act"], dx2, ta=True, tm=1408, tn=1024, tk=1024, name="mm_g_down", epi=_epi_wire(D_MODEL))
    dgu = _swiglu_bwd(sv["gu"], da, "swiglu_bwd")
    dx1, gn = _mm(dgu, p["w_gu_t"], tm=1024, tn=1024, tk=1408, name="mm_d_h2", epi=_epi_rms_bwd(sv["x1"], p["norm_ffn"], dx2))
    g["w_gu_t"], g["w_gu_t@wire"] = _mm(dgu, sv["h2"], ta=True, tm=1408, tn=1024, tk=1024, name="mm_g_gu", epi=_epi_wire(D_MODEL))
    g["norm_ffn"] = jnp.sum(gn, axis=0)
    dmix, stats = _mm(dx1, p["w_out"], tb=True, tm=1024, tn=1536, tk=1024, name="mm_d_mix", epi=_epi_att_stats(sv["att"], sv["lse"]))
    g["w_out"], g["w_out@wire"] = _mm(sv["mix"], dx1, ta=True, tm=1536, tn=1024, tk=1024, name="mm_g_out", epi=_epi_wire(D_MODEL))
    proj = sv["proj"]
    dpre, dxc_u, dgl, dlpar = _lru_scan_bwd(sv["pre"], sv["xc"], proj, sv["lpar"], sv["hs"], dmix, "lru_scan_bwd")
    dxc = _mm(dpre, sv["wab"], tb=True, add=dxc_u, tm=1024, tn=512, tk=1024, name="mm_d_xc")
    gwab = _mm(sv["xc"], dpre, ta=True, tm=512, tn=1024, tk=1024, name="mm_g_lru")
    g["lru_wa"], g["lru_wx"] = _block_diag_extract(gwab[:, :LRU_W]), _block_diag_extract(gwab[:, LRU_W:])
    g["lru_ba"], g["lru_bx"], g["lru_lambda"] = dlpar[0], dlpar[1], dlpar[2]
    dxl, gcw, gcb = _lru_conv_bwd(proj, dxc, p["lru_conv_w"], "lru_conv_bwd")
    g["lru_conv_w"], g["lru_conv_b"] = gcw[:CONV_K], jnp.sum(gcb, axis=0)
    dy, dz, gsn = _ssd_post_bwd(sv["y"], proj, p["ssd_norm"], (dmix, SSD_W, 1), "ssd_post_bwd")
    g["ssd_norm"] = jnp.sum(gsn, axis=0)
    dxconv, ddt, dal, ddk, got_ssd = _ssd_scan_bwd(sv["xconv"], sv["dt"], sv["spar"], sv["states"], dy, "ssd_scan_bwd", comm_ssd)
    g["ssd_a_log"], g["ssd_d"] = dal[0, :8], ddk[0, :8]
    dxbc, ddtr, gsw, gsb, gdb = _ssd_pre_bwd(proj, dxconv, ddt, p["ssd_conv_w"], p["ssd_conv_b"],
                                             _lanes128(p["ssd_dt_bias"]), "ssd_pre_bwd")
    g["ssd_conv_w"], g["ssd_conv_b"], g["ssd_dt_bias"] = gsw[:CONV_K], jnp.sum(gsb, axis=0), jnp.sum(gdb, axis=0)[:8]
    dq, dk, dv, got_att = _att_bwd_rev(proj, dmix, stats, "att_bwd", None if comm_att is None else comm_att(g))
    dproj = jnp.concatenate([dq, dk, dv, dz, dxbc, dgl, dxl, ddtr], axis=1)
    g["w_in_t"], g["w_in_t@wire"] = _mm(dproj, sv["h"], ta=True, tm=1408, tn=1024, tk=1024, name="mm_g_in", epi=_epi_wire(D_MODEL))
    res = _mm(dproj, p["w_in_t"], tm=1024, tn=1024, tk=1408, name="mm_d_h", comm=None if comm_tail is None else comm_tail(g),
              epi=_epi_rms_bwd(sv["x"], p["norm_mix"], dx1))
    (dx, gm), got_tail = res if comm_tail is not None else (res, [])
    g["norm_mix"] = jnp.sum(gm, axis=0)
    return dx, g, got_ssd, got_att, got_tail


def _grad_slabs(g, names, suffix=""):
    out = {}
    for n in names:
        if n == "w_in":
            out[n] = _w_in_slabs(g["w_in_t" + suffix])
        elif n == "w_gate":
            out[n] = _slabs(n, g["w_gu_t" + suffix][:D_FF])
        elif n == "w_up":
            out[n] = _slabs(n, g["w_gu_t" + suffix][D_FF:])
        else:
            out[n] = _slabs(n, g[n + suffix])
    return out


def kernel(x, norm_mix, w_in, ssd_conv_w, ssd_conv_b, ssd_dt_bias, ssd_a_log, ssd_d, ssd_norm, lru_conv_w, lru_conv_b, lru_wa, lru_ba, lru_wx, lru_bx, lru_lambda, w_out, norm_ffn, w_gate, w_up, w_down, norm_final, loss_target, m_norm_mix, m_w_in, m_ssd_conv_w, m_ssd_conv_b, m_ssd_dt_bias, m_ssd_a_log, m_ssd_d, m_ssd_norm, m_lru_conv_w, m_lru_conv_b, m_lru_wa, m_lru_ba, m_lru_wx, m_lru_bx, m_lru_lambda, m_w_out, m_norm_ffn, m_w_gate, m_w_up, m_w_down, m_norm_final, v_norm_mix, v_w_in, v_ssd_conv_w, v_ssd_conv_b, v_ssd_dt_bias, v_ssd_a_log, v_ssd_d, v_ssd_norm, v_lru_conv_w, v_lru_conv_b, v_lru_wa, v_lru_ba, v_lru_wx, v_lru_bx, v_lru_lambda, v_w_out, v_norm_ffn, v_w_gate, v_w_up, v_w_down, v_norm_final):
    loc = dict(locals())
    w = {n: loc[n] for n in WEIGHTS}
    m = {n: loc["m_" + n] for n in WEIGHTS}
    v = {n: loc["v_" + n] for n in WEIGHTS}
    for n in TRANSPOSED:
        w[n], m[n], v[n] = [jnp.transpose(t, (0, 2, 1)) for t in (w[n], m[n], v[n])]
    wt_in, mt_in, vt_in = [jnp.transpose(t, (2, 0, 1)) for t in (w["w_in"], m["w_in"], v["w_in"])]

    def halves(a):
        return a.reshape(a.shape[0], 2, a.shape[1] // 2, a.shape[2])

    def unhalve(a):
        return a.reshape(4, 2 * a.shape[2], a.shape[3])

    def joined(name, a):
        return _w_in_rows(unhalve(a)) if name == "w_in" else _join(name, unhalve(a))

    wb = {n: halves(w[n].astype(MXU)) for n in MATS[1:]}
    wb["w_in"] = halves(jnp.pad(jnp.transpose(wt_in.astype(MXU), (1, 0, 2)), ((0, 0), (0, W_IN_PAD - W_IN_SHARD), (0, 0))))
    xs = x[0]
    h0, first = _rms_fwd(xs, norm_mix[0], "rms_mix", _Comm(gathers=[(wb["w_in"], 0, True), (w["ssd_conv_w"], None, False),
                                                                    (w["lru_conv_w"], None, False)]))
    convs = {"ssd_conv_w": _join("ssd_conv_w", first[1]), "lru_conv_w": _join("lru_conv_w", first[2])}
    behind_att = [(n, 0) for n in MATS[1:]]
    behind_ffn = [[("w_in", 1)], [("w_out", 1), ("w_gate", 1), ("w_up", 1)], [("w_down", 1)]]
    whole = {("w_in", 0): joined("w_in", first[0])}
    params = {}

    def layer_params(l):
        if l not in params:
            p = {n: w[n][l] for n in SMALL if n != "norm_final"}
            p.update(w_in_t=whole["w_in", l], ssd_conv_w=convs["ssd_conv_w"][l], lru_conv_w=convs["lru_conv_w"][l])
            params[l] = p
        if "w_out" not in params[l] and ("w_out", l) in whole:
            params[l].update(w_out=whole["w_out", l], w_down=whole["w_down", l],
                             w_gu_t=jnp.concatenate([whole["w_gate", l], whole["w_up", l]], axis=0))
        return params[l]

    saved = []
    h_in = h0
    for l in range(DEPTH):
        first_layer = l == 0
        mix, sv, got = _layer_mixers(xs, layer_params(l), _Comm(gathers=[(wb[n], k, True) for n, k in behind_att]) if first_layer else None,
                                     h_in)
        whole.update({k: joined(k[0], a) for k, a in zip(behind_att, got)})
        comms = [_Comm(gathers=[(wb[n], k, True) for n, k in part]) if first_layer else None for part in behind_ffn]
        xs, got, h_in = _layer_ffn(xs, mix, layer_params(l), sv, comms, norm_mix[l + 1] if l + 1 < DEPTH else None)
        whole.update({k: joined(k[0], a) for k, a in zip([k for part in behind_ffn for k in part], got)})
        saved.append(sv)
    dx, gnf, lsum = _loss_head(xs, norm_final, loss_target[0], "loss_head")
    loss = lax.psum(jnp.sum(lsum), ("x", "y", "c"))

    dx, g1, _, _, _ = _layer_bwd(dx, layer_params(1), saved[1])
    def slabs_of(g, names):
        own = _grad_slabs(g, names)
        sent = _grad_slabs(g, [n for n in names if n in MATS], "@wire")
        sent.update({n: own[n] for n in names if n not in MATS})
        return own, sent

    s1, sent1 = slabs_of(g1, BIG)
    att0 = ("w_gate", "w_up", "w_down", "w_out")
    s0, sent0 = {}, {}

    def add0(g0, names):
        own, sent = slabs_of(g0, names)
        s0.update(own)
        sent0.update(sent)

    ssd1 = ("w_gate", "w_up")
    att1 = tuple(n for n in BIG if n not in ssd1)

    def comm_att(g0):
        add0(g0, att0)
        return _Comm(scatters=[sent1[n] for n in att1] + [sent0[n] for n in att0])

    tail0 = ("w_in",) + CONVS

    def comm_tail(g0):
        add0(g0, tail0)
        return _Comm(scatters=[sent0[n] for n in tail0])

    dx, g0, got_ssd, got_att, got_tail = _layer_bwd(dx, layer_params(0), saved[0], _Comm(scatters=[sent1[n] for n in ssd1]),
                                                    comm_att, comm_tail)
    recv = {(n, 1): a for n, a in zip(ssd1, got_ssd)}
    recv.update({(n, 1): a for n, a in zip(att1, got_att[:len(att1)])})
    recv.update({(n, 0): a for n, a in zip(att0, got_att[len(att1):])})
    recv.update({(n, 0): a for n, a in zip(tail0, got_tail)})

    me = 2 * lax.axis_index("x") + lax.axis_index("y")
    slabs = (s0, s1)
    part = {}
    for n in BIG:
        per_layer = []
        for l in range(DEPTH):
            own = lax.dynamic_index_in_dim(slabs[l][n], me, axis=0, keepdims=False)
            per_layer.append(_sum_slots(own, recv[n, l], "sum_chips_" + n, ROW_TILE.get(n, own.shape[0])))
        part[n] = jnp.stack(per_layer, axis=0)
    sib = dict(zip(BIG, _swap_sibling([part[n] for n in BIG])))
    out_g, out_d, out_m, out_v = {}, {}, {}, {}
    for n in BIG:
        if n == "w_in":
            res = _adamw(wt_in, mt_in, vt_in, jnp.transpose(part[n], (1, 0, 2)), jnp.transpose(sib[n], (1, 0, 2)), "adamw_" + n,
                         W_IN_ADAM_TILE, rows_first=True)
            out_g[n], out_d[n], out_m[n], out_v[n] = [jnp.transpose(t, (1, 2, 0)) for t in res]
            continue
        res = _adamw(w[n], m[n], v[n], part[n], sib[n], "adamw_" + n, ROW_TILE.get(n, w[n].shape[1]))
        out_g[n], out_d[n], out_m[n], out_v[n] = [jnp.transpose(t, (0, 2, 1)) for t in res] if n in TRANSPOSED else res

    gsm = {n: jnp.stack([g0[n], g1[n]], axis=0) for n in SMALL if n != "norm_final"}
    gsm["norm_final"] = jnp.sum(gnf, axis=0)
    small_shapes = [w[n].shape for n in SMALL]
    gs = _pack([gsm[n].reshape(w[n].shape) for n in SMALL], BLK, 8, F32)
    gall = _gather_small(gs)
    gsum = _sum_slots(None, gall, "sum_devices", gs.shape[0])
    ws = _pack([w[n] for n in SMALL], BLK, 8, F32)
    ms = _pack([m[n] for n in SMALL], BLK, 8, F32)
    vs = _pack([v[n] for n in SMALL], BLK, 8, F32)
    gsr, dsr, nms, nvs = _adamw(ws, ms, vs, gsum, None, "adamw_small", gs.shape[0])
    out_g.update(zip(SMALL, _unpack(gsr, small_shapes)))
    out_d.update(zip(SMALL, _unpack(dsr, small_shapes)))
    out_m.update(zip(SMALL, _unpack(nms, small_shapes)))
    out_v.update(zip(SMALL, _unpack(nvs, small_shapes)))

    return (loss, dx[None], *[out_g[n] for n in WEIGHTS], *[out_d[n] for n in WEIGHTS],
            *[out_m[n] for n in WEIGHTS], *[out_v[n] for n in WEIGHTS])
```

```python
import functools
import math

import jax
import jax.numpy as jnp
import numpy as np
from jax import lax
from jax.experimental import pallas as pl
from jax.experimental.pallas import tpu as pltpu

F32 = jnp.float32
MXU = jnp.bfloat16
HI = lax.Precision.HIGHEST
HIGH = lax.Precision.HIGH
MESH = pl.DeviceIdType.MESH

D_MODEL = 1024
DEPTH = 2
HEAD_DIM = 64
ATT_W = 512
ATT_PATTERNS = ((128, 1), (512, 4), (2048, 16))
BLK = 128
SSD_W = 512
SSD_STATE = 128
LRU_W = 512
LRU_BLOCKS = 8
LRU_C = 8.0
CONV_K = 4
D_MIX = 1536
D_FF = 2816
IN_COLS = 4104
NP = 4224
NORM_EPS = 1e-6
SSD_NORM_EPS = 1e-5
LN2 = math.log(2.0)
NEG = -1e30

ADAM_LR, ADAM_B1, ADAM_B2, ADAM_EPS, ADAM_WD, ADAM_STEP = 0.001, 0.9, 0.999, 1e-08, 0.01, 10
BC1 = 1.0 - ADAM_B1 ** ADAM_STEP
BC2 = 1.0 - ADAM_B2 ** ADAM_STEP

VMEM_LIMIT = 56 * 1024 * 1024

C_Q, C_K, C_V, C_Z, C_XBC, C_G, C_XL, C_DT = 0, 512, 1024, 1536, 2048, 3072, 3584, 4096


def _cp(*sem):
    return pltpu.CompilerParams(dimension_semantics=sem, vmem_limit_bytes=VMEM_LIMIT)


def _dot(a, b, dims, prec=None):
    return lax.dot_general(a, b, (dims, ((), ())), preferred_element_type=F32, precision=prec)


def _nn(a, b, prec=None):
    return _dot(a, b, ((1,), (0,)), prec)


def _nt(a, b, prec=None):
    return _dot(a, b, ((1,), (1,)), prec)


def _tn(a, b, prec=None):
    return _dot(a, b, ((0,), (0,)), prec)


def _sigmoid(x):
    return jax.nn.sigmoid(x)


def _silu(x):
    return x * _sigmoid(x)


def _softplus(x):
    return jnp.maximum(x, 0.0) + jnp.log(1.0 + jnp.exp(-jnp.abs(x)))


def _gelu(x):
    return 0.5 * x * (1.0 + jnp.tanh(0.7978845608028654 * (x + 0.044715 * x * x * x)))


def _mm(a, b, *, ta=False, tb=False, add=None, out_dtype=F32, tm, tn, tk, name, comm=None, epi=None):
    m, k = (a.shape[1], a.shape[0]) if ta else a.shape
    n = b.shape[0] if tb else b.shape[1]
    assert (b.shape[1] if tb else b.shape[0]) == k
    assert m % tm == 0 and n % tn == 0 and k % tk == 0, (name, m, n, k)
    nk = k // tk
    a_spec = pl.BlockSpec((tk, tm), lambda i, j, kk: (kk, i)) if ta else pl.BlockSpec((tm, tk), lambda i, j, kk: (i, kk))
    b_spec = pl.BlockSpec((tn, tk), lambda i, j, kk: (j, kk)) if tb else pl.BlockSpec((tk, tn), lambda i, j, kk: (kk, j))
    o_spec = pl.BlockSpec((tm, tn), lambda i, j, kk: (i, j))
    dims = ((0 if ta else 1,), (1 if tb else 0,))
    carried = comm is not None
    comm = comm or _Comm()
    ni, nj = m // tm, n // tn
    efn, erows, econsts, eouts, eaccs = epi or (None, [], [], [], [])
    assert epi is None or nj == 1
    nadd = 0 if add is None else 1
    ner, nec, neo, nea = len(erows), len(econsts), len(eouts), len(eaccs)

    def body(*refs):
        refs, cm = comm.split(refs, 2 + nadd + ner + nec, 1 + neo + nea, 1)
        a_ref, b_ref = refs[:2]
        er_refs = refs[2 + nadd:2 + nadd + ner]
        ec_refs = refs[2 + nadd + ner:2 + nadd + ner + nec]
        o_ref = refs[2 + nadd + ner + nec]
        eo_refs = refs[3 + nadd + ner + nec:3 + nadd + ner + nec + neo]
        ea_refs = refs[3 + nadd + ner + nec + neo:3 + nadd + ner + nec + neo + nea]
        acc = refs[-1]
        i, j, kk = pl.program_id(0), pl.program_id(1), pl.program_id(2)
        comm.start_at((i == 0) & (j == 0) & (kk == 0), cm)

        @pl.when(kk == 0)
        def _():
            acc[...] = jnp.zeros_like(acc)

        acc[...] += _dot(a_ref[...].astype(MXU), b_ref[...].astype(MXU), dims)

        @pl.when(kk == nk - 1)
        def _():
            r = acc[...]
            if add is not None:
                r = r + refs[2][...]
            if efn is None:
                o_ref[...] = r.astype(out_dtype)
            else:
                main, extra, sums = efn(r, [t[...] for t in er_refs], [t[...] for t in ec_refs])
                o_ref[...] = main.astype(out_dtype)
                for t, val in zip(eo_refs, extra):
                    t[...] = val.astype(t.dtype)
                @pl.when(i == 0)
                def _():
                    for t, val in zip(ea_refs, sums):
                        t[...] = val

                @pl.when(i > 0)
                def _():
                    for t, val in zip(ea_refs, sums):
                        t[...] += val

        comm.wait_at((i == ni - 1) & (j == nj - 1) & (kk == nk - 1), cm)

    def whole_rows(width):
        return pl.BlockSpec((tm, width), lambda i, j, kk: (i, 0))

    ins = [a, b] + ([] if add is None else [add]) + list(erows) + list(econsts)
    specs = [a_spec, b_spec] + ([] if add is None else [o_spec]) + [whole_rows(t.shape[1]) for t in erows]
    specs += [pl.BlockSpec(t.shape, lambda i, j, kk: (0, 0)) for t in econsts]
    out_specs = [o_spec] + [whole_rows(wd) for wd, _ in eouts] + [pl.BlockSpec((r, wd), lambda i, j, kk: (0, 0)) for r, wd in eaccs]
    out_shape = [jax.ShapeDtypeStruct((m, n), out_dtype)] + [jax.ShapeDtypeStruct((m, wd), dt) for wd, dt in eouts]
    out_shape += [jax.ShapeDtypeStruct((r, wd), F32) for r, wd in eaccs]
    serial = comm.n or nea
    res = pl.pallas_call(
        body, name=name, grid=(ni, nj, nk), in_specs=specs + [ANY] * comm.n, out_specs=out_specs + [ANY] * comm.n,
        out_shape=out_shape + comm.out_shape(),
        scratch_shapes=[pltpu.VMEM((tm, tn), F32)] + comm.scratch(),
        compiler_params=_cp(*((["arbitrary"] * 3) if serial else ["parallel", "parallel", "arbitrary"])),
    )(*ins, *comm.args())
    nown = 1 + neo + nea
    own = res[0] if epi is None else list(res[:nown])
    return (own, list(res[nown:])) if carried else own


def _rows(fn, rows, consts=(), outs=(), accs=(), *, tile, name, halos=(), comm=None):
    rows = [r if isinstance(r, tuple) else (r, r.shape[1], 0) for r in rows]
    s = rows[0][0].shape[0]
    assert s % tile == 0 and tile % 8 == 0
    n = s // tile
    t8 = tile // 8
    nr, nh, nc_, no, na = len(rows), len(halos), len(consts), len(outs), len(accs)
    carried = comm is not None
    comm = comm or _Comm()

    def body(*refs):
        refs, cm = comm.split(refs, nr + nh + nc_, no + na, 0)
        i = pl.program_id(0)
        comm.start_at(i == 0, cm)
        rv = [r[...] for r in refs[:nr]]
        hv = []
        for (idx, kind), r in zip(halos, refs[nr:nr + nh]):
            edge = (i == 0) if kind == "prev" else (i == n - 1)
            hv.append(jnp.where(edge, 0.0, r[...]))
        cv = [r[...] for r in refs[nr + nh:nr + nh + nc_]]
        o_refs = refs[nr + nh + nc_:nr + nh + nc_ + no]
        a_refs = refs[nr + nh + nc_ + no:]
        ov, av = fn(rv, hv, cv)
        for r, v in zip(o_refs, ov):
            r[...] = v.astype(r.dtype)
        if na:
            @pl.when(i == 0)
            def _():
                for r in a_refs:
                    r[...] = jnp.zeros_like(r)
            for r, v in zip(a_refs, av):
                r[...] += v
        comm.wait_at(i == n - 1, cm)

    in_specs = [pl.BlockSpec((tile, w), functools.partial(lambda i, cb: (i, cb), cb=cb)) for (_, w, cb) in rows]
    for idx, kind in halos:
        _, w, cb = rows[idx]
        if kind == "prev":
            in_specs.append(pl.BlockSpec((8, w), functools.partial(lambda i, cb: (jnp.maximum(i * t8 - 1, 0), cb), cb=cb)))
        else:
            in_specs.append(pl.BlockSpec((8, w), functools.partial(lambda i, cb: (jnp.minimum((i + 1) * t8, n * t8 - 1), cb), cb=cb)))
    in_specs += [pl.BlockSpec(c.shape, functools.partial(lambda i, nd: (0,) * nd, nd=c.ndim)) for c in consts]
    out_specs = [pl.BlockSpec((tile, c), lambda i: (i, 0)) for (c, _) in outs]
    out_specs += [pl.BlockSpec((r, c), lambda i: (0, 0)) for (r, c) in accs]
    out_shape = [jax.ShapeDtypeStruct((s, c), dt) for (c, dt) in outs]
    out_shape += [jax.ShapeDtypeStruct((r, c), F32) for (r, c) in accs]
    args = [r[0] for r in rows] + [rows[idx][0] for idx, _ in halos] + list(consts)
    res = pl.pallas_call(
        body, name=name, grid=(n,), in_specs=in_specs + [ANY] * comm.n, out_specs=out_specs + [ANY] * comm.n,
        out_shape=out_shape + comm.out_shape(), scratch_shapes=comm.scratch(), compiler_params=_cp("arbitrary"),
    )(*args, *comm.args())
    return (list(res[:no + na]), list(res[no + na:])) if carried else list(res)


def _colsum8(v):
    t, c = v.shape
    return jnp.sum(v.reshape(t // 8, 8, c), axis=0)


def _rms(x, g):
    return x * lax.rsqrt(jnp.mean(x * x, axis=-1, keepdims=True) + NORM_EPS) * g


def _epi_rms(g):
    return (lambda r, rows, consts: (r, [_rms(r, consts[0])], []), [], [g.reshape(1, -1)], [(g.shape[-1], MXU)], [])


def _epi_rms_bwd(x, g, dres):
    def fn(r, rows, consts):
        xb, drb = rows
        _, vjp = jax.vjp(_rms, xb, consts[0])
        rstd = lax.rsqrt(jnp.mean(xb * xb, axis=-1, keepdims=True) + NORM_EPS)
        return drb + vjp(r)[0], [], [_colsum8(r * xb * rstd)]
    return (fn, [x, dres], [g.reshape(1, -1)], [], [(8, g.shape[-1])])


def _epi_att_stats(att, lse):
    def fn(r, rows, consts):
        hr = lax.broadcasted_iota(jnp.int32, (ATT_W, ATT_W), 0) // HEAD_DIM
        hc = lax.broadcasted_iota(jnp.int32, (ATT_W, ATT_W), 1) // HEAD_DIM
        delta = _nn(r[:, :ATT_W] * rows[0], (hr == hc).astype(F32), HIGH)
        lane = lax.broadcasted_iota(jnp.int32, delta.shape, 1)
        return r, [jnp.where(lane % HEAD_DIM < HEAD_DIM // 2, rows[1], delta)], []
    return (fn, [att, lse], [], [(ATT_W, F32)], [])


def _epi_wire(width):
    return (lambda r, rows, consts: (r, [r], []), [], [], [(width, MXU)], [])


def _rms_fwd(x, g, name, comm=None):
    def fn(rv, hv, cv):
        return [_rms(rv[0], cv[0])], []
    res = _rows(fn, [x], [g.reshape(1, -1)], [(x.shape[1], MXU)], tile=512, name=name, comm=comm)
    return res[0] if comm is None else (res[0][0], res[1])


def _slope_dist(hp, hh, dist, dil):
    hf = (2 * hp + hh + 1).astype(F32)
    slope = jnp.exp(jnp.zeros(dist.shape, F32) - hf * LN2)
    return slope * (dist.astype(F32) * float(dil))


ATT_G = 2048


def _att_fwd_fused(proj, name, comm=None):
    s, npc = proj.shape
    gsz = ATT_G
    ng = s // gsz
    assert s % gsz == 0
    scale = HEAD_DIM ** -0.5
    comm = comm or _Comm()

    def body(*refs):
        (q_ref, kp_ref, kc_ref, vp_ref, vc_ref, att_ref, lse_ref, attb_ref, nn, mn, dn), cm = comm.split(refs, 5, 3, 3)
        hp, g = pl.program_id(0), pl.program_id(1)
        comm.start_at((hp == 0) & (g == 0), cm)
        lane = lax.broadcasted_iota(jnp.int32, (BLK, BLK), 1)
        qi = lax.broadcasted_iota(jnp.int32, (BLK, 2 * BLK), 0)
        ki = lax.broadcasted_iota(jnp.int32, (BLK, 2 * BLK), 1)
        dist = BLK + qi - ki
        band = (dist >= 0) & (dist <= BLK)
        for pi, (_, dil) in enumerate(ATT_PATTERNS):
            nbg = gsz // dil // BLK
            bias = [_slope_dist(hp, hh, dist, dil) for hh in (0, 1)]
            for r in range(dil):
                for b in range(nbg):
                    def rows(blk):
                        return pl.ds(blk * BLK * dil + r, BLK, stride=dil) if dil > 1 else pl.ds(blk * BLK, BLK)
                    q = q_ref[rows(b), :]
                    k_prev = kp_ref[rows(nbg - 1), :] if b == 0 else kc_ref[rows(b - 1), :]
                    v_prev = vp_ref[rows(nbg - 1), :] if b == 0 else vc_ref[rows(b - 1), :]
                    kk = jnp.concatenate([k_prev, kc_ref[rows(b), :]], axis=0).astype(MXU)
                    vv = jnp.concatenate([v_prev, vc_ref[rows(b), :]], axis=0).astype(MXU)
                    valid = (band & ((g > 0) | (ki >= BLK))) if b == 0 else band
                    num = jnp.zeros((BLK, BLK), F32)
                    mx = jnp.zeros((BLK, BLK), F32)
                    den = jnp.zeros((BLK, BLK), F32)
                    for hh in (0, 1):
                        hmask = (lane < HEAD_DIM) if hh == 0 else (lane >= HEAD_DIM)
                        qm = jnp.where(hmask, q, 0.0).astype(MXU)
                        sc = jnp.where(valid, _nt(qm, kk) * scale - bias[hh], NEG)
                        m = jnp.max(sc, axis=1, keepdims=True)
                        p = jnp.exp(sc - m)
                        dn_ = jnp.sum(p, axis=1, keepdims=True)
                        o = _nn(p.astype(MXU), vv)
                        num = jnp.where(hmask, o, num)
                        mx = jnp.where(hmask, m, mx)
                        den = jnp.where(hmask, dn_, den)
                    nn.at[pi][rows(b), :] = num
                    mn.at[pi][rows(b), :] = mx
                    dn.at[pi][rows(b), :] = den

        def merge(c, carry):
            rows = pl.ds(pl.multiple_of(c * 256, 256), 256)
            ms = [mn[pi, rows, :] for pi in range(len(ATT_PATTERNS))]
            m_all = functools.reduce(jnp.maximum, ms)
            num = jnp.zeros((256, BLK), F32)
            den = jnp.zeros((256, BLK), F32)
            for pi in range(len(ATT_PATTERNS)):
                e = jnp.exp(ms[pi] - m_all)
                num = num + nn[pi, rows, :] * e
                den = den + dn[pi, rows, :] * e
            att = num / den
            att_ref[rows, :] = att
            attb_ref[rows, :] = att.astype(MXU)
            lse_ref[rows, :] = m_all + jnp.log(den)
            return carry

        lax.fori_loop(0, gsz // 256, merge, 0)
        comm.wait_at((hp == 3) & (g == ng - 1), cm)

    def cur(base):
        return pl.BlockSpec((gsz, BLK), lambda hp, g: (g, base // BLK + hp))

    def prev(base):
        return pl.BlockSpec((gsz, BLK), lambda hp, g: (jnp.maximum(g - 1, 0), base // BLK + hp))

    o_spec = pl.BlockSpec((gsz, BLK), lambda hp, g: (g, hp))
    npat = len(ATT_PATTERNS)
    res = pl.pallas_call(
        body, name=name, grid=(4, ng),
        in_specs=[cur(C_Q), prev(C_K), cur(C_K), prev(C_V), cur(C_V)] + [ANY] * comm.n,
        out_specs=[o_spec] * 3 + [ANY] * comm.n,
        out_shape=[jax.ShapeDtypeStruct((s, ATT_W), F32)] * 2 + [jax.ShapeDtypeStruct((s, ATT_W), MXU)] + comm.out_shape(),
        scratch_shapes=[pltpu.VMEM((npat, gsz, BLK), F32)] * 3 + comm.scratch(),
        compiler_params=_cp("arbitrary", "arbitrary"),
    )(proj, proj, proj, proj, proj, *comm.args())
    return res[0], res[1], res[2], list(res[3:])


def _att_bwd_rev(proj, datt, stats, name, comm=None):
    s, npc = proj.shape
    gsz = ATT_G
    ng = s // gsz
    npat = len(ATT_PATTERNS)
    scale = HEAD_DIM ** -0.5
    comm = comm or _Comm()

    def body(*refs):
        (q_ref, kp_ref, kc_ref, vp_ref, vc_ref, do_ref, st_ref, dq_out, dk_out, dv_out,
         kcar, vcar, dq_ref, dk_ref, dv_ref), cm = comm.split(refs, 7, 3, 5)
        hp, gi = pl.program_id(0), pl.program_id(1)
        g = ng - 1 - gi
        comm.start_at((hp == 0) & (gi == 0), cm)

        @pl.when(gi == 0)
        def _():
            kcar[...] = jnp.zeros_like(kcar)
            vcar[...] = jnp.zeros_like(vcar)

        lane = lax.broadcasted_iota(jnp.int32, (BLK, BLK), 1)
        qi = lax.broadcasted_iota(jnp.int32, (BLK, 2 * BLK), 0)
        ki = lax.broadcasted_iota(jnp.int32, (BLK, 2 * BLK), 1)
        dist = BLK + qi - ki
        band = (dist >= 0) & (dist <= BLK)
        for acc in (dq_ref, dk_ref, dv_ref):
            acc[...] = jnp.zeros_like(acc)
        for pi, (_, dil) in enumerate(ATT_PATTERNS):
            nbg = gsz // dil // BLK
            bias = [_slope_dist(hp, hh, dist, dil) for hh in (0, 1)]
            for r in range(dil):
                edge = slice(pi * gsz + r * BLK, pi * gsz + (r + 1) * BLK)
                for b in reversed(range(nbg)):
                    def rows(blk):
                        return pl.ds(blk * BLK * dil + r, BLK, stride=dil) if dil > 1 else pl.ds(blk * BLK, BLK)
                    q, do, st = q_ref[rows(b), :], do_ref[rows(b), :], st_ref[rows(b), :]
                    k_prev = kp_ref[rows(nbg - 1), :] if b == 0 else kc_ref[rows(b - 1), :]
                    v_prev = vp_ref[rows(nbg - 1), :] if b == 0 else vc_ref[rows(b - 1), :]
                    kk = jnp.concatenate([k_prev, kc_ref[rows(b), :]], axis=0).astype(MXU)
                    vv = jnp.concatenate([v_prev, vc_ref[rows(b), :]], axis=0).astype(MXU)
                    valid = (band & ((g > 0) | (ki >= BLK))) if b == 0 else band
                    dq = jnp.zeros((BLK, BLK), F32)
                    dkk = jnp.zeros((2 * BLK, BLK), F32)
                    dvv = jnp.zeros((2 * BLK, BLK), F32)
                    for hh in (0, 1):
                        c0 = hh * HEAD_DIM
                        hmask = (lane < HEAD_DIM) if hh == 0 else (lane >= HEAD_DIM)
                        qm = jnp.where(hmask, q, 0.0).astype(MXU)
                        dom = jnp.where(hmask, do, 0.0).astype(MXU)
                        sc = _nt(qm, kk) * scale - bias[hh]
                        p = jnp.exp(jnp.where(valid, sc - st[:, c0:c0 + 1], NEG))
                        ds = (p * (_nt(dom, vv) - st[:, c0 + HEAD_DIM // 2:c0 + HEAD_DIM // 2 + 1])).astype(MXU)
                        dq = jnp.where(hmask, _nn(ds, kk), dq)
                        dkk = dkk + _tn(ds, qm)
                        dvv = dvv + _tn(p.astype(MXU), dom)
                    dq_ref[rows(b), :] += dq * scale
                    own_k, own_v = dkk[BLK:] * scale, dvv[BLK:]
                    if b == nbg - 1:
                        own_k, own_v = own_k + kcar[edge, :], own_v + vcar[edge, :]
                    dk_ref[rows(b), :] += own_k
                    dv_ref[rows(b), :] += own_v
                    if b > 0:
                        dk_ref[rows(b - 1), :] += dkk[:BLK] * scale
                        dv_ref[rows(b - 1), :] += dvv[:BLK]
                    else:
                        kcar[edge, :] = dkk[:BLK] * scale
                        vcar[edge, :] = dvv[:BLK]
        for out, acc in ((dq_out, dq_ref), (dk_out, dk_ref), (dv_out, dv_ref)):
            out[...] = acc[...].astype(out.dtype)
        comm.wait_at((hp == 3) & (gi == ng - 1), cm)

    def pspec(base, shift):
        return pl.BlockSpec((gsz, BLK), lambda hp, gi: (jnp.maximum(ng - 1 - gi + shift, 0), base // BLK + hp))

    wspec = pl.BlockSpec((gsz, BLK), lambda hp, gi: (ng - 1 - gi, hp))
    in_specs = [pspec(C_Q, 0), pspec(C_K, -1), pspec(C_K, 0), pspec(C_V, -1), pspec(C_V, 0), wspec, wspec] + [ANY] * comm.n
    res = pl.pallas_call(
        body, name=name, grid=(4, ng), in_specs=in_specs,
        out_specs=[wspec] * 3 + [ANY] * comm.n,
        out_shape=[jax.ShapeDtypeStruct((s, ATT_W), MXU)] * 3 + comm.out_shape(),
        scratch_shapes=[pltpu.VMEM((npat * gsz, BLK), F32)] * 2 + [pltpu.VMEM((gsz, BLK), F32)] * 3 + comm.scratch(),
        compiler_params=_cp("arbitrary", "arbitrary"),
    )(proj, proj, proj, proj, proj, datt, stats, *comm.args())
    return res[0], res[1], res[2], list(res[3:])


def _shift_down(cur, halo, sft):
    if sft == 0:
        return cur
    t = cur.shape[0]
    rolled = pltpu.roll(cur, sft, 0)
    hr = pltpu.roll(halo, sft, 0)
    row = lax.broadcasted_iota(jnp.int32, cur.shape, 0)
    return jnp.where(row < sft, jnp.tile(hr, (t // 8, 1)), rolled)


def _shift_up(cur, halo, sft):
    if sft == 0:
        return cur
    t = cur.shape[0]
    rolled = pltpu.roll(cur, t - sft, 0)
    hr = pltpu.roll(halo, 8 - sft, 0)
    row = lax.broadcasted_iota(jnp.int32, cur.shape, 0)
    return jnp.where(row >= t - sft, jnp.tile(hr, (t // 8, 1)), rolled)


def _conv(x, xh, w, b):
    y = b + x * w[CONV_K - 1:CONV_K]
    for k in range(CONV_K - 1):
        y = y + _shift_down(x, xh, CONV_K - 1 - k) * w[k:k + 1]
    return y


def _conv_bwd(x, xh, dy, dyh, w):
    dx = dy * w[CONV_K - 1:CONV_K]
    dws = []
    for k in range(CONV_K - 1):
        sft = CONV_K - 1 - k
        dx = dx + _shift_up(dy, dyh, sft) * w[k:k + 1]
        dws.append(jnp.sum(dy * _shift_down(x, xh, sft), axis=0, keepdims=True))
    dws.append(jnp.sum(dy * x, axis=0, keepdims=True))
    c = x.shape[1]
    dw = jnp.concatenate(dws + [jnp.zeros((8 - CONV_K, c), F32)], axis=0)
    return dx, dw, jnp.sum(dy, axis=0, keepdims=True)


def _pad8(w):
    return jnp.concatenate([w, jnp.zeros((8 - w.shape[0], w.shape[1]), w.dtype)], axis=0)


def _ssd_pre(proj, conv_w, conv_b, dt_bias128, name):
    def fn(rv, hv, cv):
        xbc, dtr = rv
        return [_silu(_conv(xbc, hv[0], cv[0], cv[1])), _softplus(dtr + cv[2])], []
    return _rows(fn, [(proj, 1024, C_XBC // 1024), (proj, BLK, C_DT // BLK)],
                 [_pad8(conv_w), conv_b.reshape(1, -1), dt_bias128],
                 [(1024, F32), (BLK, F32)], tile=256, name=name, halos=[(0, "prev")])


def _ssd_pre_bwd(proj, dxc, ddt, conv_w, conv_b, dt_bias128, name):
    def fn(rv, hv, cv):
        xbc, dtr, dxcb, ddtb = rv
        xh, dxch_raw, xnext = hv
        w, b, bias = cv
        pre = _conv(xbc, xh, w, b)
        sg = _sigmoid(pre)
        dpre = dxcb * (sg * (1.0 + pre * (1.0 - sg)))
        t = xbc.shape[0]
        tail = jnp.concatenate([xbc[t - 8:], xnext], axis=0)
        pre_n = _conv(tail[8:], tail[:8], w, b)
        sgn = _sigmoid(pre_n)
        dpre_h = dxch_raw * (sgn * (1.0 + pre_n * (1.0 - sgn)))
        dx, dw, db = _conv_bwd(xbc, xh, dpre, dpre_h, w)
        ddr = ddtb * _sigmoid(dtr + bias)
        return [dx, ddr], [dw, jnp.concatenate([db, jnp.zeros((7, db.shape[1]), F32)], axis=0), _colsum8(ddr)]
    return _rows(fn, [(proj, 1024, C_XBC // 1024), (proj, BLK, C_DT // BLK), dxc, ddt],
                 [_pad8(conv_w), conv_b.reshape(1, -1), dt_bias128],
                 [(1024, MXU), (BLK, MXU)], [(8, 1024), (8, 1024), (8, BLK)], tile=256, name=name,
                 halos=[(0, "prev"), (2, "next"), (0, "next")])


SSD_CPB = 1


def _head_cols(v, h0):
    lane = lax.broadcasted_iota(jnp.int32, (v.shape[0], BLK), 1)
    return jnp.where(lane < HEAD_DIM, v[:, h0:h0 + 1], v[:, h0 + 1:h0 + 2])


def _ssd_scan(xc, dt, par, name):
    s = xc.shape[0]
    nc = s // BLK

    def body(x_ref, dt_ref, par_ref, y_ref, st_ref, h_ref):
        c = pl.program_id(0)

        @pl.when(c == 0)
        def _():
            h_ref[...] = jnp.zeros_like(h_ref)

        st_ref[0] = h_ref[...]
        dt = dt_ref[...]
        a_row = -jnp.exp(par_ref[0:1, :])
        d_row = par_ref[1:2, :]
        ri = lax.broadcasted_iota(jnp.int32, (BLK, BLK), 0)
        ci = lax.broadcasted_iota(jnp.int32, (BLK, BLK), 1)
        tril = ri >= ci
        cs = _nn(tril.astype(F32), dt * a_row, HI)
        cst, dtt = cs.T, dt.T
        last = cs[BLK - 1:BLK, :]
        wcol = jnp.exp(last - cs) * dt
        ecs = jnp.exp(cs)
        elast = jnp.exp(last)
        for g in (0, 1):
            bg = x_ref[:, 512 + g * BLK:512 + (g + 1) * BLK].astype(MXU)
            cg = x_ref[:, 768 + g * BLK:768 + (g + 1) * BLK].astype(MXU)
            gm = _nt(cg, bg)
            for pp in (0, 1):
                pr = 2 * g + pp
                h0 = 2 * pr
                x2 = x_ref[:, pr * BLK:(pr + 1) * BLK]
                hprev = h_ref[pr * BLK:(pr + 1) * BLK, :]
                yp = jnp.zeros((BLK, BLK), F32)
                for hh in (0, 1):
                    h = h0 + hh
                    hmask = (ci < HEAD_DIM) if hh == 0 else (ci >= HEAD_DIM)
                    lm = jnp.exp(jnp.where(tril, cs[:, h:h + 1] - cst[h:h + 1, :], NEG))
                    mm = gm * lm * dtt[h:h + 1, :]
                    yp = yp + _nn(mm.astype(MXU), jnp.where(hmask, x2, 0.0).astype(MXU))
                y0 = _nt(cg, hprev.astype(MXU))
                y_ref[:, pr * BLK:(pr + 1) * BLK] = yp + _head_cols(ecs, h0) * y0 + _head_cols(d_row, h0) * x2
                dec = jnp.where(ri < HEAD_DIM, elast[:, h0:h0 + 1], elast[:, h0 + 1:h0 + 2])
                xw = (x2 * _head_cols(wcol, h0)).astype(MXU)
                h_ref[pr * BLK:(pr + 1) * BLK, :] = dec * hprev + _tn(xw, bg)

    return pl.pallas_call(
        body, name=name, grid=(nc,),
        in_specs=[pl.BlockSpec((BLK, 1024), lambda c: (c, 0)), pl.BlockSpec((BLK, BLK), lambda c: (c, 0)),
                  pl.BlockSpec((8, BLK), lambda c: (0, 0))],
        out_specs=[pl.BlockSpec((BLK, SSD_W), lambda c: (c, 0)), pl.BlockSpec((1, SSD_W, SSD_STATE), lambda c: (c, 0, 0))],
        out_shape=[jax.ShapeDtypeStruct((s, SSD_W), F32), jax.ShapeDtypeStruct((nc, SSD_W, SSD_STATE), F32)],
        scratch_shapes=[pltpu.VMEM((SSD_W, SSD_STATE), F32)],
        compiler_params=_cp("arbitrary"),
    )(xc, dt, par)


def _ssd_scan_bwd(xc, dt, par, st, dy, name, comm=None):
    s = xc.shape[0]
    cpb = SSD_CPB
    nb = s // (cpb * BLK)
    comm = comm or _Comm()

    def chunk(x_ref, dt_ref, par_ref, st_ref, dy_ref, dx_ref, ddt_ref, dal_ref, dd_ref, dh_ref):
        dt = dt_ref[...]
        a_row = -jnp.exp(par_ref[0:1, :])
        d_row = par_ref[1:2, :]
        ri = lax.broadcasted_iota(jnp.int32, (BLK, BLK), 0)
        ci = lax.broadcasted_iota(jnp.int32, (BLK, BLK), 1)
        tril = ri >= ci
        cs = _nn(tril.astype(F32), dt * a_row, HI)
        cst, dtt = cs.T, dt.T
        last = cs[BLK - 1:BLK, :]
        tolast = jnp.exp(last - cs)
        wcol = tolast * dt
        ecs = jnp.exp(cs)
        elast = jnp.exp(last)
        dcs_col = jnp.zeros((BLK, BLK), F32)
        ddt_col = jnp.zeros((BLK, BLK), F32)
        dcs_row = jnp.zeros((BLK, BLK), F32)
        ddt_row = jnp.zeros((BLK, BLK), F32)
        dlast = jnp.zeros((1, BLK), F32)
        ddsk = jnp.zeros((1, BLK), F32)
        for g in (0, 1):
            bg32 = x_ref[:, 512 + g * BLK:512 + (g + 1) * BLK]
            cg32 = x_ref[:, 768 + g * BLK:768 + (g + 1) * BLK]
            bg, cg = bg32.astype(MXU), cg32.astype(MXU)
            gm = _nt(cg, bg)
            dgm = jnp.zeros((BLK, BLK), F32)
            dbg = jnp.zeros((BLK, BLK), F32)
            dcg = jnp.zeros((BLK, BLK), F32)
            for pp in (0, 1):
                pr = 2 * g + pp
                h0 = 2 * pr
                x2 = x_ref[:, pr * BLK:(pr + 1) * BLK]
                dy2 = dy_ref[:, pr * BLK:(pr + 1) * BLK]
                hprev = st_ref[0, pr * BLK:(pr + 1) * BLK, :]
                dhn = dh_ref[pr * BLK:(pr + 1) * BLK, :]
                x2m, dhnm = x2.astype(MXU), dhn.astype(MXU)
                zb = _nt(bg, dhnm)
                y0 = _nt(cg, hprev.astype(MXU))
                esel = _head_cols(ecs, h0)
                wsel = _head_cols(wcol, h0)
                dx2 = _head_cols(d_row, h0) * dy2 + wsel * zb
                pick2 = (((ri < HEAD_DIM) & (ci == h0)) | ((ri >= HEAD_DIM) & (ci == h0 + 1))).astype(F32)
                sums = _nn(jnp.concatenate([dy2 * y0, x2 * zb, dy2 * x2], axis=0), pick2, HIGH)
                de2, dw2, dd2 = sums[:BLK], sums[BLK:2 * BLK], sums[2 * BLK:]
                v2 = dw2 * wcol
                dcs_col = dcs_col + ecs * de2 - v2
                ddt_col = ddt_col + dw2 * tolast
                hsum = _nn(dhn * hprev, jnp.ones((BLK, BLK), F32), HIGH)
                dlast = dlast + elast * jnp.sum(jnp.where(pick2 > 0.0, hsum, 0.0), axis=0, keepdims=True) \
                    + jnp.sum(v2, axis=0, keepdims=True)
                ddsk = ddsk + jnp.sum(dd2, axis=0, keepdims=True)
                ts = []
                for hh in (0, 1):
                    h = h0 + hh
                    hmask = (ci < HEAD_DIM) if hh == 0 else (ci >= HEAD_DIM)
                    ons = (ri == h).astype(F32)
                    dym = jnp.where(hmask, dy2, 0.0).astype(MXU)
                    dt_r = dtt[h:h + 1, :]
                    lm = jnp.exp(jnp.where(tril, cs[:, h:h + 1] - cst[h:h + 1, :], NEG))
                    mm = gm * lm * dt_r
                    dx2 = dx2 + _tn(mm.astype(MXU), dym)
                    dm = _nt(dym, x2m)
                    t1 = dm * lm
                    dgm = dgm + t1 * dt_r
                    tt = t1 * gm
                    ddt_row = ddt_row + ons * jnp.sum(tt, axis=0, keepdims=True)
                    t = tt * dt_r
                    dcs_row = dcs_row - ons * jnp.sum(t, axis=0, keepdims=True)
                    ts.append(t)
                rows2 = lax.broadcasted_iota(jnp.int32, (2 * BLK, BLK), 0)
                lane2 = lax.broadcasted_iota(jnp.int32, (2 * BLK, BLK), 1)
                to_lane = ((rows2 < BLK) & (lane2 == h0)) | ((rows2 >= BLK) & (lane2 == h0 + 1))
                dcs_col = dcs_col + _nn(jnp.concatenate(ts, axis=1), to_lane.astype(F32), HIGH)
                dx_ref[:, pr * BLK:(pr + 1) * BLK] = dx2
                edy = (esel * dy2).astype(MXU)
                dcg = dcg + _nn(edy, hprev.astype(MXU))
                dec = jnp.where(ri < HEAD_DIM, elast[:, h0:h0 + 1], elast[:, h0 + 1:h0 + 2])
                dh_ref[pr * BLK:(pr + 1) * BLK, :] = dec * dhn + _tn(edy, cg)
                dbg = dbg + _nn((x2 * wsel).astype(MXU), dhnm)
            dgmm = dgm.astype(MXU)
            dx_ref[:, 512 + g * BLK:512 + (g + 1) * BLK] = dbg + _tn(dgmm, cg)
            dx_ref[:, 768 + g * BLK:768 + (g + 1) * BLK] = dcg + _nn(dgmm, bg)
        dcs = dcs_col + dcs_row.T + jnp.where(ri == BLK - 1, dlast, 0.0)
        dda = _nn((ri <= ci).astype(F32), dcs, HI)
        ddt_ref[...] = ddt_col + ddt_row.T + a_row * dda
        da = jnp.sum(dt * dda, axis=0, keepdims=True)
        dal_ref[0:1, :] += da * a_row
        dd_ref[0:1, :] += ddsk

    def body(*refs):
        (x_ref, dt_ref, par_ref, st_ref, dy_ref, dx_ref, ddt_ref, dal_ref, dd_ref, dh_ref), cm = comm.split(refs, 5, 4, 1)
        c = pl.program_id(0)
        comm.start_at(c == 0, cm)

        @pl.when(c == 0)
        def _():
            dh_ref[...] = jnp.zeros_like(dh_ref)
            dal_ref[...] = jnp.zeros_like(dal_ref)
            dd_ref[...] = jnp.zeros_like(dd_ref)

        for cc in reversed(range(cpb)):
            rows = pl.ds(cc * BLK, BLK)
            chunk(x_ref.at[rows], dt_ref.at[rows], par_ref, st_ref.at[pl.ds(cc, 1)], dy_ref.at[rows], dx_ref.at[rows],
                  ddt_ref.at[rows], dal_ref, dd_ref, dh_ref)
        comm.wait_at(c == nb - 1, cm)

    rev = lambda c: (nb - 1 - c, 0)
    tb = cpb * BLK
    res = pl.pallas_call(
        body, name=name, grid=(nb,),
        in_specs=[pl.BlockSpec((tb, 1024), rev), pl.BlockSpec((tb, BLK), rev), pl.BlockSpec((8, BLK), lambda c: (0, 0)),
                  pl.BlockSpec((cpb, SSD_W, SSD_STATE), lambda c: (nb - 1 - c, 0, 0)), pl.BlockSpec((tb, SSD_W), rev)]
        + [ANY] * comm.n,
        out_specs=[pl.BlockSpec((tb, 1024), rev), pl.BlockSpec((tb, BLK), rev),
                   pl.BlockSpec((8, BLK), lambda c: (0, 0)), pl.BlockSpec((8, BLK), lambda c: (0, 0))] + [ANY] * comm.n,
        out_shape=[jax.ShapeDtypeStruct((s, 1024), F32), jax.ShapeDtypeStruct((s, BLK), F32),
                   jax.ShapeDtypeStruct((8, BLK), F32), jax.ShapeDtypeStruct((8, BLK), F32)] + comm.out_shape(),
        scratch_shapes=[pltpu.VMEM((SSD_W, SSD_STATE), F32)] + comm.scratch(),
        compiler_params=_cp("arbitrary"),
    )(xc, dt, par, st, dy, *comm.args())
    return res[0], res[1], res[2], res[3], list(res[4:])


def _ssd_gate(y, z, w):
    t = y * _silu(z)
    outs = []
    for g in (0, 1):
        tg = t[:, g * 256:(g + 1) * 256]
        outs.append(tg * lax.rsqrt(jnp.mean(tg * tg, axis=-1, keepdims=True) + SSD_NORM_EPS))
    return jnp.concatenate(outs, axis=1) * w


def _ssd_post(y, proj, norm_w, name):
    def fn(rv, hv, cv):
        return [_ssd_gate(rv[0], rv[1], cv[0])], []
    return _rows(fn, [y, (proj, SSD_W, C_Z // SSD_W)], [norm_w.reshape(1, -1)], [(SSD_W, MXU)], tile=512, name=name)[0]


def _ssd_post_bwd(y, proj, norm_w, dout, name):
    def fn(rv, hv, cv):
        yb, zb, db = rv
        _, vjp = jax.vjp(lambda a, b: _ssd_gate(a, b, cv[0]), yb, zb)
        dy, dz = vjp(db)
        t = yb * _silu(zb)
        nrm = []
        for g in (0, 1):
            tg = t[:, g * 256:(g + 1) * 256]
            nrm.append(tg * lax.rsqrt(jnp.mean(tg * tg, axis=-1, keepdims=True) + SSD_NORM_EPS))
        return [dy, dz], [_colsum8(db * jnp.concatenate(nrm, axis=1))]
    return _rows(fn, [y, (proj, SSD_W, C_Z // SSD_W), dout], [norm_w.reshape(1, -1)],
                 [(SSD_W, F32), (SSD_W, MXU)], [(8, SSD_W)], tile=512, name=name)


LRU_T = 256


def _lru_conv(proj, conv_w, conv_b, name):
    def fn(rv, hv, cv):
        return [_conv(rv[0], hv[0], cv[0], cv[1])], []
    return _rows(fn, [(proj, LRU_W, C_XL // LRU_W)], [_pad8(conv_w), conv_b.reshape(1, -1)], [(LRU_W, F32)],
                 tile=512, name=name, halos=[(0, "prev")])[0]


def _lru_conv_bwd(proj, dxc, conv_w, name):
    def fn(rv, hv, cv):
        dx, dw, db = _conv_bwd(rv[0], hv[0], rv[1], hv[1], cv[0])
        return [dx], [dw, jnp.concatenate([db, jnp.zeros((7, db.shape[1]), F32)], axis=0)]
    return _rows(fn, [(proj, LRU_W, C_XL // LRU_W), dxc], [_pad8(conv_w)], [(LRU_W, MXU)], [(8, LRU_W), (8, LRU_W)],
                 tile=512, name=name, halos=[(0, "prev"), (1, "next")])


def _lru_au(pre_a, pre_x, xc, ba, bx, lam):
    r = _sigmoid(pre_a + ba)
    i = _sigmoid(pre_x + bx)
    log_a = -LRU_C * r * _softplus(-lam)
    a = jnp.exp(log_a)
    u = jnp.sqrt(1.0 - jnp.exp(2.0 * log_a)) * (i * xc)
    return a, u


def _lru_scan(pre, xc, proj, par, name):
    s = xc.shape[0]
    t = LRU_T

    def body(pre_ref, xc_ref, g_ref, par_ref, out_ref, h_ref, carry):
        c = pl.program_id(0)

        @pl.when(c == 0)
        def _():
            carry[...] = jnp.zeros_like(carry)

        a, u = _lru_au(pre_ref[:, :LRU_W], pre_ref[:, LRU_W:], xc_ref[...], par_ref[0:1, :], par_ref[1:2, :], par_ref[2:3, :])
        row = lax.broadcasted_iota(jnp.int32, (t, LRU_W), 0)
        sft = 1
        while sft < t:
            keep = row >= sft
            a_s = jnp.where(keep, pltpu.roll(a, sft, 0), 1.0)
            u_s = jnp.where(keep, pltpu.roll(u, sft, 0), 0.0)
            u = a * u_s + u
            a = a * a_s
            sft *= 2
        h = a * carry[0:1, :] + u
        h_ref[...] = h
        out_ref[...] = (h * _gelu(g_ref[...])).astype(out_ref.dtype)
        carry[0:1, :] = h[t - 1:t, :]

    return pl.pallas_call(
        body, name=name, grid=(s // t,),
        in_specs=[pl.BlockSpec((t, 2 * LRU_W), lambda c: (c, 0)), pl.BlockSpec((t, LRU_W), lambda c: (c, 0)),
                  pl.BlockSpec((t, LRU_W), lambda c: (c, C_G // LRU_W)), pl.BlockSpec((8, LRU_W), lambda c: (0, 0))],
        out_specs=[pl.BlockSpec((t, LRU_W), lambda c: (c, 0))] * 2,
        out_shape=[jax.ShapeDtypeStruct((s, LRU_W), MXU), jax.ShapeDtypeStruct((s, LRU_W), F32)],
        scratch_shapes=[pltpu.VMEM((8, LRU_W), F32)],
        compiler_params=_cp("arbitrary"),
    )(pre, xc, proj, par)


def _lru_scan_bwd(pre, xc, proj, par, h, dout, name):
    s = xc.shape[0]
    t = LRU_T
    n = s // t
    t8 = t // 8

    def body(pre_ref, xc_ref, g_ref, par_ref, h_ref, hh_ref, do_ref, dpre_ref, dxc_ref, dg_ref, dpar_ref, carry):
        c = pl.program_id(0)

        @pl.when(c == 0)
        def _():
            carry[...] = jnp.zeros_like(carry)
            dpar_ref[...] = jnp.zeros_like(dpar_ref)

        pa, px, xcb = pre_ref[:, :LRU_W], pre_ref[:, LRU_W:], xc_ref[...]
        ba, bx, lam = par_ref[0:1, :], par_ref[1:2, :], par_ref[2:3, :]
        (a, u), vjp = jax.vjp(_lru_au, pa, px, xcb, ba, bx, lam)
        g = g_ref[...]
        hcur = h_ref[...]
        do = do_ref[...]
        _, gvjp = jax.vjp(_gelu, g)
        dg_ref[...] = gvjp(do * hcur)[0].astype(dg_ref.dtype)
        row = lax.broadcasted_iota(jnp.int32, (t, LRU_W), 0)
        v = do * _gelu(g) + jnp.where(row == t - 1, carry[0:1, :], 0.0)
        b = jnp.where(row == t - 1, 0.0, pltpu.roll(a, t - 1, 0))
        sft = 1
        while sft < t:
            keep = row < t - sft
            b_s = jnp.where(keep, pltpu.roll(b, t - sft, 0), 1.0)
            v_s = jnp.where(keep, pltpu.roll(v, t - sft, 0), 0.0)
            v = b * v_s + v
            b = b * b_s
            sft *= 2
        dh = v
        carry[0:1, :] = a[0:1, :] * dh[0:1, :]
        hhalo = jnp.where(c == n - 1, 0.0, hh_ref[...])
        hprev = _shift_down(hcur, hhalo, 1)
        dpa, dpx, dxc, dba, dbx, dlam = vjp((dh * hprev, dh))
        dpre_ref[:, :LRU_W] = dpa
        dpre_ref[:, LRU_W:] = dpx
        dxc_ref[...] = dxc
        dpar_ref[0:1, :] += dba
        dpar_ref[1:2, :] += dbx
        dpar_ref[2:3, :] += dlam

    rev = lambda c: (n - 1 - c, 0)
    return pl.pallas_call(
        body, name=name, grid=(n,),
        in_specs=[pl.BlockSpec((t, 2 * LRU_W), rev), pl.BlockSpec((t, LRU_W), rev),
                  pl.BlockSpec((t, LRU_W), lambda c: (n - 1 - c, C_G // LRU_W)), pl.BlockSpec((8, LRU_W), lambda c: (0, 0)),
                  pl.BlockSpec((t, LRU_W), rev),
                  pl.BlockSpec((8, LRU_W), lambda c: (jnp.maximum((n - 1 - c) * t8 - 1, 0), 0)),
                  pl.BlockSpec((t, LRU_W), lambda c: (n - 1 - c, dout.shape[1] // LRU_W - 1))],
        out_specs=[pl.BlockSpec((t, 2 * LRU_W), rev), pl.BlockSpec((t, LRU_W), rev), pl.BlockSpec((t, LRU_W), rev),
                   pl.BlockSpec((8, LRU_W), lambda c: (0, 0))],
        out_shape=[jax.ShapeDtypeStruct((s, 2 * LRU_W), F32), jax.ShapeDtypeStruct((s, LRU_W), F32),
                   jax.ShapeDtypeStruct((s, LRU_W), MXU), jax.ShapeDtypeStruct((8, LRU_W), F32)],
        scratch_shapes=[pltpu.VMEM((8, LRU_W), F32)],
        compiler_params=_cp("arbitrary"),
    )(pre, xc, proj, par, h, h, dout)


def _swiglu_act(gu, name):
    def fn(rv, hv, cv):
        return [_silu(rv[0].astype(F32)) * rv[1].astype(F32)], []
    return _rows(fn, [(gu, D_FF, 0), (gu, D_FF, 1)], [], [(D_FF, MXU)], tile=256, name=name)[0]


def _epi_swiglu_bwd(gu):
    def fn(r, rows, consts):
        gt, up = rows[0][:, :D_FF].astype(F32), rows[0][:, D_FF:].astype(F32)
        sg = _sigmoid(gt)
        dgate = r * up * (sg * (1.0 + gt * (1.0 - sg)))
        dup = r * (gt * sg)
        return r, [jnp.concatenate([dgate, dup], axis=1)], []
    return (fn, [gu], [], [(2 * D_FF, MXU)], [])


def _loss_head(x, g, target, name):
    d = x.shape[1]

    def fn(rv, hv, cv):
        xb, tb = rv
        y, vjp = jax.vjp(_rms, xb, cv[0])
        err = y - tb
        dy = err * (1.0 / d)
        dx, _ = vjp(dy)
        rstd = lax.rsqrt(jnp.mean(xb * xb, axis=-1, keepdims=True) + NORM_EPS)
        e2 = err * err * (0.5 / d)
        e2 = functools.reduce(lambda a, b: a + b, [e2[:, k * BLK:(k + 1) * BLK] for k in range(d // BLK)])
        return [dx], [_colsum8(dy * xb * rstd), _colsum8(e2)]
    return _rows(fn, [x, target], [g.reshape(1, -1)], [(d, F32)], [(8, d), (8, BLK)], tile=512, name=name)


ANY = pl.BlockSpec(memory_space=pl.ANY)


def _coords():
    return lax.axis_index("x"), lax.axis_index("y"), lax.axis_index("c")


class _Comm:
    def __init__(self, gathers=(), scatters=()):
        self.gathers = list(gathers)
        self.scatters = list(scatters)
        self.n = len(self.gathers) + len(self.scatters)

    def args(self):
        return [g[0] for g in self.gathers] + self.scatters

    def out_shape(self):
        out = [jax.ShapeDtypeStruct((4,) + (a.shape if l is None else a.shape[1:]), a.dtype) for a, l, _ in self.gathers]
        return out + [jax.ShapeDtypeStruct((3,) + a.shape[1:], a.dtype) for a in self.scatters]

    def scratch(self):
        if not self.n:
            return []
        return [pltpu.SemaphoreType.DMA((3 * self.n,)), pltpu.SemaphoreType.DMA((3 * self.n,)),
                pltpu.SemaphoreType.DMA((max(len(self.gathers), 1),)),
                pltpu.SemaphoreType.DMA((3 * self.n,)), pltpu.SemaphoreType.DMA((3 * self.n,))]

    def split(self, refs, n_in, n_out, n_scratch):
        refs = list(refs)
        n = self.n
        own = refs[:n_in] + refs[n_in + n:n_in + n + n_out] + refs[n_in + 2 * n + n_out:n_in + 2 * n + n_out + n_scratch]
        cm = (refs[n_in:n_in + n], refs[n_in + n + n_out:n_in + 2 * n + n_out], refs[n_in + 2 * n + n_out + n_scratch:])
        return own, cm

    def _copies(self, cm, arriving):
        ins, outs, (send, recv, local, _, _) = cm
        x, y, c = _coords()
        me = 2 * x + y
        chips = [(1 - x, y), (x, 1 - y), (1 - x, 1 - y)]
        remote, locals_ = [], []
        ng = len(self.gathers)
        for i in range(self.n):
            if i < ng:
                _, l, halved = self.gathers[i]
                slab = ins[i] if l is None else ins[i].at[l]
                if not arriving:
                    locals_.append(pltpu.make_async_copy(slab, outs[i].at[me], local.at[i]))
            for j, (px, py) in enumerate(chips):
                if i < ng:
                    slot = 2 * px + py if arriving else me
                    src, dst = (slab.at[c], outs[i].at[slot, c]) if halved else (slab, outs[i].at[slot])
                else:
                    src, dst = ins[i].at[2 * px + py], outs[i].at[j]
                remote.append(pltpu.make_async_remote_copy(src, dst, send.at[3 * i + j], recv.at[3 * i + j],
                                                           device_id=(px, py, c), device_id_type=MESH))
        return remote, locals_

    def _handovers(self, cm, arriving):
        _, outs, (_, _, _, send, recv) = cm
        x, y, c = _coords()
        chips = [(1 - x, y), (x, 1 - y), (1 - x, 1 - y)]
        cps = []
        for i, (_, _, halved) in enumerate(self.gathers):
            if halved:
                for j, (px, py) in enumerate(chips):
                    src = outs[i].at[2 * px + py, c]
                    dst = outs[i].at[2 * px + py, 1 - c if arriving else c]
                    cps.append(pltpu.make_async_remote_copy(src, dst, send.at[3 * i + j], recv.at[3 * i + j],
                                                            device_id=(x, y, 1 - c), device_id_type=MESH))
        return cps

    def start_at(self, cond, cm):
        def go():
            remote, locals_ = self._copies(cm, False)
            for cp in locals_ + remote:
                cp.start()

        if self.n:
            go() if cond is True else pl.when(cond)(go)

    def wait_at(self, cond, cm):
        def go():
            for cp in self._copies(cm, True)[0]:
                cp.wait_recv()
            handed = self._handovers(cm, False)
            for cp in handed:
                cp.start()
            for cp in self._handovers(cm, True):
                cp.wait_recv()
            remote, locals_ = self._copies(cm, False)
            for cp in handed + remote:
                cp.wait_send()
            for cp in locals_:
                cp.wait()

        if self.n:
            go() if cond is True else pl.when(cond)(go)


def _swap_sibling(arrs):
    n = len(arrs)

    def body(*refs):
        ins, outs, send, recv = refs[:n], refs[n:2 * n], refs[2 * n], refs[2 * n + 1]
        x, y, c = _coords()
        cps = [pltpu.make_async_remote_copy(ins[i], outs[i], send.at[i], recv.at[i], device_id=(x, y, 1 - c), device_id_type=MESH)
               for i in range(n)]
        for cp in cps:
            cp.start()
        for cp in cps:
            cp.wait_recv()
        for cp in cps:
            cp.wait_send()

    return list(pl.pallas_call(
        body, name="swap_sibling", in_specs=[ANY] * n, out_specs=[ANY] * n,
        out_shape=[jax.ShapeDtypeStruct(a.shape, a.dtype) for a in arrs],
        scratch_shapes=[pltpu.SemaphoreType.DMA((n,)), pltpu.SemaphoreType.DMA((n,))],
        compiler_params=pltpu.CompilerParams(has_side_effects=True),
    )(*arrs))


def _gather_small(gs):
    def body(g_ref, o_ref, send_sems, recv_sems, local_sem):
        x, y, c = _coords()
        me = 4 * x + 2 * y + c
        mine = pltpu.make_async_copy(g_ref, o_ref.at[me], local_sem)
        mine.start()
        sends = []
        for k in range(1, 8):
            px, py, pc = x ^ (k >> 2), y ^ ((k >> 1) & 1), c ^ (k & 1)
            sends.append((pltpu.make_async_remote_copy(g_ref, o_ref.at[me], send_sems.at[k - 1], recv_sems.at[k - 1],
                                                       device_id=(px, py, pc), device_id_type=MESH), 4 * px + 2 * py + pc, k))
        for cp, _, _ in sends:
            cp.start()
        for cp, src, k in sends:
            pltpu.make_async_remote_copy(g_ref, o_ref.at[src], send_sems.at[k - 1], recv_sems.at[k - 1],
                                         device_id=(x, y, c), device_id_type=MESH).wait_recv()
        for cp, _, _ in sends:
            cp.wait_send()
        mine.wait()

    return pl.pallas_call(
        body, name="gather_small", in_specs=[ANY], out_specs=ANY,
        out_shape=jax.ShapeDtypeStruct((8,) + gs.shape, gs.dtype),
        scratch_shapes=[pltpu.SemaphoreType.DMA((7,)), pltpu.SemaphoreType.DMA((7,)), pltpu.SemaphoreType.DMA],
        compiler_params=pltpu.CompilerParams(has_side_effects=True),
    )(gs)


def _sum_slots(own, others, name, tile):
    k, r, c = others.shape

    def body(*refs):
        if own is None:
            o_ref, out_ref = refs
            acc = o_ref[0].astype(F32)
            first = 1
        else:
            own_ref, o_ref, out_ref = refs
            acc = own_ref[...]
            first = 0
        for j in range(first, k):
            acc = acc + o_ref[j].astype(F32)
        out_ref[...] = acc

    row = pl.BlockSpec((tile, c), lambda i: (i, 0))
    specs = ([] if own is None else [row]) + [pl.BlockSpec((k, tile, c), lambda i: (0, i, 0))]
    args = ([] if own is None else [own]) + [others]
    return pl.pallas_call(body, name=name, grid=(r // tile,), in_specs=specs, out_specs=row,
                          out_shape=jax.ShapeDtypeStruct((r, c), F32), compiler_params=_cp("parallel"))(*args)


def _adamw(w, m, v, ga, gb, name, tile, rows_first=False):
    lead = 0 if rows_first else w.ndim - 2
    r, c = w.shape[-2:]

    def body(*refs):
        vals = [ref[0] if lead else ref[...] for ref in refs[:len(refs) - 4]]
        w_, m_, v_, g = vals[0], vals[1], vals[2], vals[3]
        if gb is not None:
            g = g + vals[4]
        nm = ADAM_B1 * m_ + (1.0 - ADAM_B1) * g
        nv = ADAM_B2 * v_ + (1.0 - ADAM_B2) * (g * g)
        d = -ADAM_LR * ((nm / BC1) / (jnp.sqrt(nv / BC2) + ADAM_EPS) + ADAM_WD * w_)
        for ref, val in zip(refs[len(refs) - 4:], (g, d, nm, nv)):
            if lead:
                ref[0] = val
            else:
                ref[...] = val

    if rows_first:
        row = pl.BlockSpec((tile,) + w.shape[1:], lambda i: (i, 0, 0))
        grid = (w.shape[0] // tile,)
    elif lead:
        row = pl.BlockSpec((1, tile, c), lambda l, i: (l, i, 0))
        grid = (w.shape[0], r // tile)
    else:
        row = pl.BlockSpec((tile, c), lambda i: (i, 0))
        grid = (r // tile,)
    args = [w, m, v, ga] + ([] if gb is None else [gb])
    return pl.pallas_call(body, name=name, grid=grid, in_specs=[row] * len(args), out_specs=[row] * 4,
                          out_shape=[jax.ShapeDtypeStruct(w.shape, F32)] * 4,
                          compiler_params=_cp(*(["parallel"] * len(grid))))(*args)


MATS = ("w_in", "w_out", "w_gate", "w_up", "w_down")
CONVS = ("ssd_conv_w", "lru_conv_w")
BIG = MATS + CONVS
TRANSPOSED = ("w_gate", "w_up")
COL_SHARDED = ("ssd_conv_w", "lru_conv_w")
W_IN_SHARD = IN_COLS // 4
W_IN_PAD = 1056
SMALL = ("norm_mix", "ssd_conv_b", "ssd_dt_bias", "ssd_a_log", "ssd_d", "ssd_norm", "lru_conv_b", "lru_wa", "lru_ba",
         "lru_wx", "lru_bx", "lru_lambda", "norm_ffn", "norm_final")
WEIGHTS = ("norm_mix", "w_in", "ssd_conv_w", "ssd_conv_b", "ssd_dt_bias", "ssd_a_log", "ssd_d", "ssd_norm", "lru_conv_w",
           "lru_conv_b", "lru_wa", "lru_ba", "lru_wx", "lru_bx", "lru_lambda", "w_out", "norm_ffn", "w_gate", "w_up",
           "w_down", "norm_final")
ROW_TILE = {"w_in": W_IN_SHARD, "w_out": 128, "w_gate": 352, "w_up": 352, "w_down": 352}
W_IN_ADAM_TILE = 54


def _pack(arrs, width, row_mult, dtype):
    flat = jnp.concatenate([a.reshape(-1).astype(dtype) for a in arrs])
    rows = -(-flat.shape[0] // width)
    rows = -(-rows // row_mult) * row_mult
    flat = jnp.pad(flat, (0, rows * width - flat.shape[0]))
    return flat.reshape(rows, width)


def _unpack(buf, shapes):
    flat = buf.reshape(-1)
    out, off = [], 0
    for shp in shapes:
        n = int(np.prod(shp))
        out.append(flat[off:off + n].reshape(shp))
        off += n
    return out


def _join(name, g4):
    if name in COL_SHARDED:
        return jnp.moveaxis(g4, 0, -2).reshape(g4.shape[1:-1] + (4 * g4.shape[-1],))
    return g4.reshape((4 * g4.shape[1],) + g4.shape[2:])


def _slabs(name, g):
    if name in COL_SHARDED:
        return jnp.moveaxis(g.reshape(g.shape[:-1] + (4, g.shape[-1] // 4)), -2, 0)
    return g.reshape((4, g.shape[0] // 4) + g.shape[1:])


def _w_in_rows(g4):
    def nat(lo, hi):
        out = []
        while lo < hi:
            j = lo // W_IN_SHARD
            stop = min(hi, (j + 1) * W_IN_SHARD)
            out.append((j, lo - j * W_IN_SHARD, stop - lo))
            lo = stop
        return out
    pieces = nat(0, 3072) + nat(3080, IN_COLS) + nat(3072, 3080)

    def body(g_ref, o_ref):
        row = 0
        for j, first, n in pieces:
            o_ref[row:row + n, :] = g_ref[j, first:first + n, :]
            row += n
        o_ref[row:, :] = jnp.zeros((NP - row, o_ref.shape[1]), o_ref.dtype)

    return pl.pallas_call(body, name="w_in_rows", out_shape=jax.ShapeDtypeStruct((NP, g4.shape[-1]), g4.dtype),
                          compiler_params=pltpu.CompilerParams(vmem_limit_bytes=VMEM_LIMIT))(g4)


def _w_in_slabs(gt):
    def kern(n):
        return n if n < 3072 else (C_DT + n - 3072 if n < 3080 else n - 8)
    slabs = []
    for j in range(4):
        lo, hi = j * W_IN_SHARD, (j + 1) * W_IN_SHARD
        cuts = sorted({lo, hi} | {c for c in (3072, 3080) if lo < c < hi})
        slabs.append(jnp.concatenate([gt[kern(a):kern(a) + b - a] for a, b in zip(cuts[:-1], cuts[1:])], axis=0))
    return jnp.stack(slabs, axis=0)


def _block_diag(w):
    eye = jnp.eye(LRU_BLOCKS, dtype=w.dtype)
    return jnp.einsum("ncd,nm->ncmd", w, eye).reshape(LRU_W, LRU_W)


def _block_diag_extract(g):
    g4 = g.reshape(LRU_BLOCKS, 64, LRU_BLOCKS, 64)
    return jnp.stack([g4[n, :, n, :] for n in range(LRU_BLOCKS)], axis=0)


def _lanes128(v):
    return jnp.pad(v, (0, BLK - v.shape[0])).reshape(1, BLK)


def _layer_mixers(x, p, comm=None, h=None):
    if h is None:
        h = _rms_fwd(x, p["norm_mix"], "rms_mix")
    proj = _mm(h, p["w_in_t"], tb=True, tm=1024, tn=1408, tk=1024, name="mm_in")
    att, lse, attb, got = _att_fwd_fused(proj, "att_fwd", comm)
    xconv, dt = _ssd_pre(proj, p["ssd_conv_w"], p["ssd_conv_b"], _lanes128(p["ssd_dt_bias"]), "ssd_pre")
    spar = jnp.concatenate([_lanes128(p["ssd_a_log"]), _lanes128(p["ssd_d"]), jnp.zeros((6, BLK), F32)], axis=0)
    y, states = _ssd_scan(xconv, dt, spar, "ssd_scan")
    ssd = _ssd_post(y, proj, p["ssd_norm"], "ssd_post")
    xc = _lru_conv(proj, p["lru_conv_w"], p["lru_conv_b"], "lru_conv")
    wab = jnp.concatenate([_block_diag(p["lru_wa"]), _block_diag(p["lru_wx"])], axis=1).astype(MXU)
    pre = _mm(xc, wab, tm=1024, tn=1024, tk=512, name="mm_lru")
    lpar = jnp.concatenate([p["lru_ba"].reshape(1, -1), p["lru_bx"].reshape(1, -1), p["lru_lambda"].reshape(1, -1),
                            jnp.zeros((5, LRU_W), F32)], axis=0)
    lru, hs = _lru_scan(pre, xc, proj, lpar, "lru_scan")
    mix = jnp.concatenate([attb, ssd, lru], axis=1)
    saved = dict(x=x, h=h, proj=proj, att=att, lse=lse, xconv=xconv, dt=dt, spar=spar, y=y, states=states, xc=xc, wab=wab,
                 pre=pre, lpar=lpar, hs=hs, mix=mix)
    return mix, saved, got


def _layer_ffn(x, mix, p, saved, comms=(None, None, None), next_norm=None):
    comm_out, comm, comm_down = comms
    x1 = _mm(mix, p["w_out"], add=x, tm=1024, tn=1024, tk=1536, name="mm_out", epi=_epi_rms(p["norm_ffn"]), comm=comm_out)
    (x1, h2), got_out = x1 if comm_out is not None else (x1, [])
    gu = _mm(h2, p["w_gu_t"], tb=True, out_dtype=MXU, tm=1024, tn=1408, tk=1024, name="mm_gu", comm=comm)
    gu, got = gu if comm is not None else (gu, [])
    act = _swiglu_act(gu, "swiglu_act")
    x2 = _mm(act, p["w_down"], add=x1, tm=1024, tn=1024, tk=2816, name="mm_down", comm=comm_down,
             epi=None if next_norm is None else _epi_rms(next_norm))
    x2, got_down = x2 if comm_down is not None else (x2, [])
    x2, h_next = x2 if next_norm is not None else (x2, None)
    saved.update(x1=x1, h2=h2, gu=gu, act=act)
    return x2, got_out + got + got_down, h_next


def _layer_bwd(dx2, p, sv, comm_ssd=None, comm_att=None, comm_tail=None):
    g = {}
    _, dgu = _mm(dx2, p["w_down"], tb=True, out_dtype=MXU, tm=512, tn=D_FF, tk=1024, name="mm_d_act", epi=_epi_swiglu_bwd(sv["gu"]))
    g["w_down"], g["w_down@wire"] = _mm(sv["act"], dx2, ta=True, tm=1408, tn=1024, tk=1024, name="mm_g_down", epi=_epi_wire(D_MODEL))
    dx1, gn = _mm(dgu, p["w_gu_t"], tm=1024, tn=1024, tk=1408, name="mm_d_h2", epi=_epi_rms_bwd(sv["x1"], p["norm_ffn"], dx2))
    g["w_gu_t"], g["w_gu_t@wire"] = _mm(dgu, sv["h2"], ta=True, tm=1408, tn=1024, tk=1024, name="mm_g_gu", epi=_epi_wire(D_MODEL))
    g["norm_ffn"] = jnp.sum(gn, axis=0)
    dmix, stats = _mm(dx1, p["w_out"], tb=True, tm=1024, tn=1536, tk=1024, name="mm_d_mix", epi=_epi_att_stats(sv["att"], sv["lse"]))
    g["w_out"], g["w_out@wire"] = _mm(sv["mix"], dx1, ta=True, tm=1536, tn=1024, tk=1024, name="mm_g_out", epi=_epi_wire(D_MODEL))
    proj = sv["proj"]
    dpre, dxc_u, dgl, dlpar = _lru_scan_bwd(sv["pre"], sv["xc"], proj, sv["lpar"], sv["hs"], dmix, "lru_scan_bwd")
    dxc = _mm(dpre, sv["wab"], tb=True, add=dxc_u, tm=1024, tn=512, tk=1024, name="mm_d_xc")
    gwab = _mm(sv["xc"], dpre, ta=True, tm=512, tn=1024, tk=1024, name="mm_g_lru")
    g["lru_wa"], g["lru_wx"] = _block_diag_extract(gwab[:, :LRU_W]), _block_diag_extract(gwab[:, LRU_W:])
    g["lru_ba"], g["lru_bx"], g["lru_lambda"] = dlpar[0], dlpar[1], dlpar[2]
    dxl, gcw, gcb = _lru_conv_bwd(proj, dxc, p["lru_conv_w"], "lru_conv_bwd")
    g["lru_conv_w"], g["lru_conv_b"] = gcw[:CONV_K], jnp.sum(gcb, axis=0)
    dy, dz, gsn = _ssd_post_bwd(sv["y"], proj, p["ssd_norm"], (dmix, SSD_W, 1), "ssd_post_bwd")
    g["ssd_norm"] = jnp.sum(gsn, axis=0)
    dxconv, ddt, dal, ddk, got_ssd = _ssd_scan_bwd(sv["xconv"], sv["dt"], sv["spar"], sv["states"], dy, "ssd_scan_bwd", comm_ssd)
    g["ssd_a_log"], g["ssd_d"] = dal[0, :8], ddk[0, :8]
    dxbc, ddtr, gsw, gsb, gdb = _ssd_pre_bwd(proj, dxconv, ddt, p["ssd_conv_w"], p["ssd_conv_b"],
                                             _lanes128(p["ssd_dt_bias"]), "ssd_pre_bwd")
    g["ssd_conv_w"], g["ssd_conv_b"], g["ssd_dt_bias"] = gsw[:CONV_K], jnp.sum(gsb, axis=0), jnp.sum(gdb, axis=0)[:8]
    dq, dk, dv, got_att = _att_bwd_rev(proj, dmix, stats, "att_bwd", None if comm_att is None else comm_att(g))
    dproj = jnp.concatenate([dq, dk, dv, dz, dxbc, dgl, dxl, ddtr], axis=1)
    g["w_in_t"], g["w_in_t@wire"] = _mm(dproj, sv["h"], ta=True, tm=1408, tn=1024, tk=1024, name="mm_g_in", epi=_epi_wire(D_MODEL))
    res = _mm(dproj, p["w_in_t"], tm=1024, tn=1024, tk=1408, name="mm_d_h", comm=None if comm_tail is None else comm_tail(g),
              epi=_epi_rms_bwd(sv["x"], p["norm_mix"], dx1))
    (dx, gm), got_tail = res if comm_tail is not None else (res, [])
    g["norm_mix"] = jnp.sum(gm, axis=0)
    return dx, g, got_ssd, got_att, got_tail


def _grad_slabs(g, names, suffix=""):
    out = {}
    for n in names:
        if n == "w_in":
            out[n] = _w_in_slabs(g["w_in_t" + suffix])
        elif n == "w_gate":
            out[n] = _slabs(n, g["w_gu_t" + suffix][:D_FF])
        elif n == "w_up":
            out[n] = _slabs(n, g["w_gu_t" + suffix][D_FF:])
        else:
            out[n] = _slabs(n, g[n + suffix])
    return out


def kernel(x, norm_mix, w_in, ssd_conv_w, ssd_conv_b, ssd_dt_bias, ssd_a_log, ssd_d, ssd_norm, lru_conv_w, lru_conv_b, lru_wa, lru_ba, lru_wx, lru_bx, lru_lambda, w_out, norm_ffn, w_gate, w_up, w_down, norm_final, loss_target, m_norm_mix, m_w_in, m_ssd_conv_w, m_ssd_conv_b, m_ssd_dt_bias, m_ssd_a_log, m_ssd_d, m_ssd_norm, m_lru_conv_w, m_lru_conv_b, m_lru_wa, m_lru_ba, m_lru_wx, m_lru_bx, m_lru_lambda, m_w_out, m_norm_ffn, m_w_gate, m_w_up, m_w_down, m_norm_final, v_norm_mix, v_w_in, v_ssd_conv_w, v_ssd_conv_b, v_ssd_dt_bias, v_ssd_a_log, v_ssd_d, v_ssd_norm, v_lru_conv_w, v_lru_conv_b, v_lru_wa, v_lru_ba, v_lru_wx, v_lru_bx, v_lru_lambda, v_w_out, v_norm_ffn, v_w_gate, v_w_up, v_w_down, v_norm_final):
    loc = dict(locals())
    w = {n: loc[n] for n in WEIGHTS}
    m = {n: loc["m_" + n] for n in WEIGHTS}
    v = {n: loc["v_" + n] for n in WEIGHTS}
    for n in TRANSPOSED:
        w[n], m[n], v[n] = [jnp.transpose(t, (0, 2, 1)) for t in (w[n], m[n], v[n])]
    wt_in, mt_in, vt_in = [jnp.transpose(t, (2, 0, 1)) for t in (w["w_in"], m["w_in"], v["w_in"])]

    def halves(a):
        return a.reshape(a.shape[0], 2, a.shape[1] // 2, a.shape[2])

    def unhalve(a):
        return a.reshape(4, 2 * a.shape[2], a.shape[3])

    def joined(name, a):
        return _w_in_rows(unhalve(a)) if name == "w_in" else _join(name, unhalve(a))

    wb = {n: halves(w[n].astype(MXU)) for n in MATS[1:]}
    wb["w_in"] = halves(jnp.pad(jnp.transpose(wt_in.astype(MXU), (1, 0, 2)), ((0, 0), (0, W_IN_PAD - W_IN_SHARD), (0, 0))))
    xs = x[0]
    h0, first = _rms_fwd(xs, norm_mix[0], "rms_mix", _Comm(gathers=[(wb["w_in"], 0, True), (w["ssd_conv_w"], None, False),
                                                                    (w["lru_conv_w"], None, False)]))
    convs = {"ssd_conv_w": _join("ssd_conv_w", first[1]), "lru_conv_w": _join("lru_conv_w", first[2])}
    behind_att = [(n, 0) for n in MATS[1:]]
    behind_ffn = [[("w_in", 1)], [("w_out", 1), ("w_gate", 1), ("w_up", 1)], [("w_down", 1)]]
    whole = {("w_in", 0): joined("w_in", first[0])}
    params = {}

    def layer_params(l):
        if l not in params:
            p = {n: w[n][l] for n in SMALL if n != "norm_final"}
            p.update(w_in_t=whole["w_in", l], ssd_conv_w=convs["ssd_conv_w"][l], lru_conv_w=convs["lru_conv_w"][l])
            params[l] = p
        if "w_out" not in params[l] and ("w_out", l) in whole:
            params[l].update(w_out=whole["w_out", l], w_down=whole["w_down", l],
                             w_gu_t=jnp.concatenate([whole["w_gate", l], whole["w_up", l]], axis=0))
        return params[l]

    saved = []
    h_in = h0
    for l in range(DEPTH):
        first_layer = l == 0
        mix, sv, got = _layer_mixers(xs, layer_params(l), _Comm(gathers=[(wb[n], k, True) for n, k in behind_att]) if first_layer else None,
                                     h_in)
        whole.update({k: joined(k[0], a) for k, a in zip(behind_att, got)})
        comms = [_Comm(gathers=[(wb[n], k, True) for n, k in part]) if first_layer else None for part in behind_ffn]
        xs, got, h_in = _layer_ffn(xs, mix, layer_params(l), sv, comms, norm_mix[l + 1] if l + 1 < DEPTH else None)
        whole.update({k: joined(k[0], a) for k, a in zip([k for part in behind_ffn for k in part], got)})
        saved.append(sv)
    dx, gnf, lsum = _loss_head(xs, norm_final, loss_target[0], "loss_head")
    loss = lax.psum(jnp.sum(lsum), ("x", "y", "c"))

    dx, g1, _, _, _ = _layer_bwd(dx, layer_params(1), saved[1])
    def slabs_of(g, names):
        own = _grad_slabs(g, names)
        sent = _grad_slabs(g, [n for n in names if n in MATS], "@wire")
        sent.update({n: own[n] for n in names if n not in MATS})
        return own, sent

    s1, sent1 = slabs_of(g1, BIG)
    att0 = ("w_gate", "w_up", "w_down", "w_out")
    s0, sent0 = {}, {}

    def add0(g0, names):
        own, sent = slabs_of(g0, names)
        s0.update(own)
        sent0.update(sent)

    ssd1 = ("w_gate", "w_up")
    att1 = tuple(n for n in BIG if n not in ssd1)

    def comm_att(g0):
        add0(g0, att0)
        return _Comm(scatters=[sent1[n] for n in att1] + [sent0[n] for n in att0])

    tail0 = ("w_in",) + CONVS

    def comm_tail(g0):
        add0(g0, tail0)
        return _Comm(scatters=[sent0[n] for n in tail0])

    dx, g0, got_ssd, got_att, got_tail = _layer_bwd(dx, layer_params(0), saved[0], _Comm(scatters=[sent1[n] for n in ssd1]),
                                                    comm_att, comm_tail)
    recv = {(n, 1): a for n, a in zip(ssd1, got_ssd)}
    recv.update({(n, 1): a for n, a in zip(att1, got_att[:len(att1)])})
    recv.update({(n, 0): a for n, a in zip(att0, got_att[len(att1):])})
    recv.update({(n, 0): a for n, a in zip(tail0, got_tail)})

    me = 2 * lax.axis_index("x") + lax.axis_index("y")
    slabs = (s0, s1)
    part = {}
    for n in BIG:
        per_layer = []
        for l in range(DEPTH):
            own = lax.dynamic_index_in_dim(slabs[l][n], me, axis=0, keepdims=False)
            per_layer.append(_sum_slots(own, recv[n, l], "sum_chips_" + n, ROW_TILE.get(n, own.shape[0])))
        part[n] = jnp.stack(per_layer, axis=0)
    sib = dict(zip(BIG, _swap_sibling([part[n] for n in BIG])))
    out_g, out_d, out_m, out_v = {}, {}, {}, {}
    for n in BIG:
        if n == "w_in":
            res = _adamw(wt_in, mt_in, vt_in, jnp.transpose(part[n], (1, 0, 2)), jnp.transpose(sib[n], (1, 0, 2)), "adamw_" + n,
                         W_IN_ADAM_TILE, rows_first=True)
            out_g[n], out_d[n], out_m[n], out_v[n] = [jnp.transpose(t, (1, 2, 0)) for t in res]
            continue
        res = _adamw(w[n], m[n], v[n], part[n], sib[n], "adamw_" + n, ROW_TILE.get(n, w[n].shape[1]))
        out_g[n], out_d[n], out_m[n], out_v[n] = [jnp.transpose(t, (0, 2, 1)) for t in res] if n in TRANSPOSED else res

    gsm = {n: jnp.stack([g0[n], g1[n]], axis=0) for n in SMALL if n != "norm_final"}
    gsm["norm_final"] = jnp.sum(gnf, axis=0)
    small_shapes = [w[n].shape for n in SMALL]
    gs = _pack([gsm[n].reshape(w[n].shape) for n in SMALL], BLK, 8, F32)
    gall = _gather_small(gs)
    gsum = _sum_slots(None, gall, "sum_devices", gs.shape[0])
    ws = _pack([w[n] for n in SMALL], BLK, 8, F32)
    ms = _pack([m[n] for n in SMALL], BLK, 8, F32)
    vs = _pack([v[n] for n in SMALL], BLK, 8, F32)
    gsr, dsr, nms, nvs = _adamw(ws, ms, vs, gsum, None, "adamw_small", gs.shape[0])
    out_g.update(zip(SMALL, _unpack(gsr, small_shapes)))
    out_d.update(zip(SMALL, _unpack(dsr, small_shapes)))
    out_m.update(zip(SMALL, _unpack(nms, small_shapes)))
    out_v.update(zip(SMALL, _unpack(nvs, small_shapes)))

    return (loss, dx[None], *[out_g[n] for n in WEIGHTS], *[out_d[n] for n in WEIGHTS],
            *[out_m[n] for n in WEIGHTS], *[out_v[n] for n in WEIGHTS])
```

```python
import functools
import math

import jax
import jax.numpy as jnp
import numpy as np
from jax import lax
from jax.experimental import pallas as pl
from jax.experimental.pallas import tpu as pltpu

F32 = jnp.float32
MXU = jnp.bfloat16
HI = lax.Precision.HIGHEST
HIGH = lax.Precision.HIGH
MESH = pl.DeviceIdType.MESH

D_MODEL = 1024
DEPTH = 2
HEAD_DIM = 64
ATT_W = 512
ATT_PATTERNS = ((128, 1), (512, 4), (2048, 16))
BLK = 128
SSD_W = 512
SSD_STATE = 128
LRU_W = 512
LRU_BLOCKS = 8
LRU_C = 8.0
CONV_K = 4
D_MIX = 1536
D_FF = 2816
IN_COLS = 4104
NP = 4224
NORM_EPS = 1e-6
SSD_NORM_EPS = 1e-5
LN2 = math.log(2.0)
NEG = -1e30

ADAM_LR, ADAM_B1, ADAM_B2, ADAM_EPS, ADAM_WD, ADAM_STEP = 0.001, 0.9, 0.999, 1e-08, 0.01, 10
BC1 = 1.0 - ADAM_B1 ** ADAM_STEP
BC2 = 1.0 - ADAM_B2 ** ADAM_STEP

VMEM_LIMIT = 56 * 1024 * 1024

C_Q, C_K, C_V, C_Z, C_XBC, C_G, C_XL, C_DT = 0, 512, 1024, 1536, 2048, 3072, 3584, 4096


def _cp(*sem):
    return pltpu.CompilerParams(dimension_semantics=sem, vmem_limit_bytes=VMEM_LIMIT)


def _dot(a, b, dims, prec=None):
    return lax.dot_general(a, b, (dims, ((), ())), preferred_element_type=F32, precision=prec)


def _nn(a, b, prec=None):
    return _dot(a, b, ((1,), (0,)), prec)


def _nt(a, b, prec=None):
    return _dot(a, b, ((1,), (1,)), prec)


def _tn(a, b, prec=None):
    return _dot(a, b, ((0,), (0,)), prec)


def _sigmoid(x):
    return jax.nn.sigmoid(x)


def _silu(x):
    return x * _sigmoid(x)


def _softplus(x):
    return jnp.maximum(x, 0.0) + jnp.log(1.0 + jnp.exp(-jnp.abs(x)))


def _gelu(x):
    return 0.5 * x * (1.0 + jnp.tanh(0.7978845608028654 * (x + 0.044715 * x * x * x)))


def _mm(a, b, *, ta=False, tb=False, add=None, out_dtype=F32, tm, tn, tk, name, comm=None, epi=None, a_pro=None):
    m, k = (a.shape[1], a.shape[0]) if ta else a.shape
    n = b.shape[0] if tb else b.shape[1]
    if a_pro is not None:
        k = b.shape[1] if tb else b.shape[0]
        assert not ta and tk == k and tn == n
    assert (b.shape[1] if tb else b.shape[0]) == k
    assert m % tm == 0 and n % tn == 0 and k % tk == 0, (name, m, n, k)
    nk = k // tk
    a_spec = pl.BlockSpec((tk, tm), lambda i, j, kk: (kk, i)) if ta else pl.BlockSpec((tm, tk), lambda i, j, kk: (i, kk))
    if a_pro is not None:
        a_spec = pl.BlockSpec((tm, a.shape[1]), lambda i, j, kk: (i, 0))
    b_spec = pl.BlockSpec((tn, tk), lambda i, j, kk: (j, kk)) if tb else pl.BlockSpec((tk, tn), lambda i, j, kk: (kk, j))
    o_spec = pl.BlockSpec((tm, tn), lambda i, j, kk: (i, j))
    dims = ((0 if ta else 1,), (1 if tb else 0,))
    carried = comm is not None
    comm = comm or _Comm()
    ni, nj = m // tm, n // tn
    efn, erows, econsts, eouts, eaccs = epi or (None, [], [], [], [])
    assert epi is None or nj == 1
    nadd = 0 if add is None else 1
    ner, nec, neo, nea = len(erows), len(econsts), len(eouts), len(eaccs)
    npa = 0 if a_pro is None else 1

    def body(*refs):
        refs, cm = comm.split(refs, 2 + nadd + ner + nec, 1 + neo + nea + npa, 1)
        a_ref, b_ref = refs[:2]
        er_refs = refs[2 + nadd:2 + nadd + ner]
        ec_refs = refs[2 + nadd + ner:2 + nadd + ner + nec]
        o_ref = refs[2 + nadd + ner + nec]
        eo_refs = refs[3 + nadd + ner + nec:3 + nadd + ner + nec + neo]
        ea_refs = refs[3 + nadd + ner + nec + neo:3 + nadd + ner + nec + neo + nea]
        acc = refs[-1]
        i, j, kk = pl.program_id(0), pl.program_id(1), pl.program_id(2)
        comm.start_at((i == 0) & (j == 0) & (kk == 0), cm)

        @pl.when(kk == 0)
        def _():
            acc[...] = jnp.zeros_like(acc)

        a_val = a_ref[...] if a_pro is None else a_pro(a_ref[...]).astype(MXU)
        acc[...] += _dot(a_val.astype(MXU), b_ref[...].astype(MXU), dims)
        if a_pro is not None:
            refs[3 + nadd + ner + nec + neo + nea][...] = a_val

        @pl.when(kk == nk - 1)
        def _():
            r = acc[...]
            if add is not None:
                r = r + refs[2][...]
            if efn is None:
                o_ref[...] = r.astype(out_dtype)
            else:
                main, extra, sums = efn(r, [t[...] for t in er_refs], [t[...] for t in ec_refs])
                o_ref[...] = main.astype(out_dtype)
                for t, val in zip(eo_refs, extra):
                    t[...] = val.astype(t.dtype)
                @pl.when(i == 0)
                def _():
                    for t, val in zip(ea_refs, sums):
                        t[...] = val

                @pl.when(i > 0)
                def _():
                    for t, val in zip(ea_refs, sums):
                        t[...] += val

        comm.wait_at((i == ni - 1) & (j == nj - 1) & (kk == nk - 1), cm)

    def whole_rows(width):
        return pl.BlockSpec((tm, width), lambda i, j, kk: (i, 0))

    ins = [a, b] + ([] if add is None else [add]) + list(erows) + list(econsts)
    specs = [a_spec, b_spec] + ([] if add is None else [o_spec]) + [whole_rows(t.shape[1]) for t in erows]
    specs += [pl.BlockSpec(t.shape, lambda i, j, kk: (0, 0)) for t in econsts]
    out_specs = [o_spec] + [whole_rows(wd) for wd, _ in eouts] + [pl.BlockSpec((r, wd), lambda i, j, kk: (0, 0)) for r, wd in eaccs]
    out_shape = [jax.ShapeDtypeStruct((m, n), out_dtype)] + [jax.ShapeDtypeStruct((m, wd), dt) for wd, dt in eouts]
    out_shape += [jax.ShapeDtypeStruct((r, wd), F32) for r, wd in eaccs]
    if a_pro is not None:
        out_specs.append(whole_rows(k))
        out_shape.append(jax.ShapeDtypeStruct((m, k), MXU))
    serial = comm.n or nea
    res = pl.pallas_call(
        body, name=name, grid=(ni, nj, nk), in_specs=specs + [ANY] * comm.n, out_specs=out_specs + [ANY] * comm.n,
        out_shape=out_shape + comm.out_shape(),
        scratch_shapes=[pltpu.VMEM((tm, tn), F32)] + comm.scratch(),
        compiler_params=_cp(*((["arbitrary"] * 3) if serial else ["parallel", "parallel", "arbitrary"])),
    )(*ins, *comm.args())
    nown = 1 + neo + nea + npa
    own = res[0] if nown == 1 else list(res[:nown])
    return (own, list(res[nown:])) if carried else own


def _rows(fn, rows, consts=(), outs=(), accs=(), *, tile, name, halos=(), comm=None):
    rows = [r if isinstance(r, tuple) else (r, r.shape[1], 0) for r in rows]
    s = rows[0][0].shape[0]
    assert s % tile == 0 and tile % 8 == 0
    n = s // tile
    t8 = tile // 8
    nr, nh, nc_, no, na = len(rows), len(halos), len(consts), len(outs), len(accs)
    carried = comm is not None
    comm = comm or _Comm()

    def body(*refs):
        refs, cm = comm.split(refs, nr + nh + nc_, no + na, 0)
        i = pl.program_id(0)
        comm.start_at(i == 0, cm)
        rv = [r[...] for r in refs[:nr]]
        hv = []
        for (idx, kind), r in zip(halos, refs[nr:nr + nh]):
            edge = (i == 0) if kind == "prev" else (i == n - 1)
            hv.append(jnp.where(edge, 0.0, r[...]))
        cv = [r[...] for r in refs[nr + nh:nr + nh + nc_]]
        o_refs = refs[nr + nh + nc_:nr + nh + nc_ + no]
        a_refs = refs[nr + nh + nc_ + no:]
        ov, av = fn(rv, hv, cv)
        for r, v in zip(o_refs, ov):
            r[...] = v.astype(r.dtype)
        if na:
            @pl.when(i == 0)
            def _():
                for r in a_refs:
                    r[...] = jnp.zeros_like(r)
            for r, v in zip(a_refs, av):
                r[...] += v
        comm.wait_at(i == n - 1, cm)

    in_specs = [pl.BlockSpec((tile, w), functools.partial(lambda i, cb: (i, cb), cb=cb)) for (_, w, cb) in rows]
    for idx, kind in halos:
        _, w, cb = rows[idx]
        if kind == "prev":
            in_specs.append(pl.BlockSpec((8, w), functools.partial(lambda i, cb: (jnp.maximum(i * t8 - 1, 0), cb), cb=cb)))
        else:
            in_specs.append(pl.BlockSpec((8, w), functools.partial(lambda i, cb: (jnp.minimum((i + 1) * t8, n * t8 - 1), cb), cb=cb)))
    in_specs += [pl.BlockSpec(c.shape, functools.partial(lambda i, nd: (0,) * nd, nd=c.ndim)) for c in consts]
    out_specs = [pl.BlockSpec((tile, c), lambda i: (i, 0)) for (c, _) in outs]
    out_specs += [pl.BlockSpec((r, c), lambda i: (0, 0)) for (r, c) in accs]
    out_shape = [jax.ShapeDtypeStruct((s, c), dt) for (c, dt) in outs]
    out_shape += [jax.ShapeDtypeStruct((r, c), F32) for (r, c) in accs]
    args = [r[0] for r in rows] + [rows[idx][0] for idx, _ in halos] + list(consts)
    res = pl.pallas_call(
        body, name=name, grid=(n,), in_specs=in_specs + [ANY] * comm.n, out_specs=out_specs + [ANY] * comm.n,
        out_shape=out_shape + comm.out_shape(), scratch_shapes=comm.scratch(), compiler_params=_cp("arbitrary"),
    )(*args, *comm.args())
    return (list(res[:no + na]), list(res[no + na:])) if carried else list(res)


def _colsum8(v):
    t, c = v.shape
    return jnp.sum(v.reshape(t // 8, 8, c), axis=0)


def _rms(x, g):
    return x * lax.rsqrt(jnp.mean(x * x, axis=-1, keepdims=True) + NORM_EPS) * g


def _epi_rms(g):
    return (lambda r, rows, consts: (r, [_rms(r, consts[0])], []), [], [g.reshape(1, -1)], [(g.shape[-1], MXU)], [])


def _epi_rms_bwd(x, g, dres):
    def fn(r, rows, consts):
        xb, drb = rows
        _, vjp = jax.vjp(_rms, xb, consts[0])
        rstd = lax.rsqrt(jnp.mean(xb * xb, axis=-1, keepdims=True) + NORM_EPS)
        return drb + vjp(r)[0], [], [_colsum8(r * xb * rstd)]
    return (fn, [x, dres], [g.reshape(1, -1)], [], [(8, g.shape[-1])])


def _epi_att_stats(att, lse):
    def fn(r, rows, consts):
        hr = lax.broadcasted_iota(jnp.int32, (ATT_W, ATT_W), 0) // HEAD_DIM
        hc = lax.broadcasted_iota(jnp.int32, (ATT_W, ATT_W), 1) // HEAD_DIM
        delta = _nn(r[:, :ATT_W] * rows[0], (hr == hc).astype(F32), HIGH)
        lane = lax.broadcasted_iota(jnp.int32, delta.shape, 1)
        return r, [jnp.where(lane % HEAD_DIM < HEAD_DIM // 2, rows[1], delta)], []
    return (fn, [att, lse], [], [(ATT_W, F32)], [])


def _epi_wire(width):
    return (lambda r, rows, consts: (r, [r], []), [], [], [(width, MXU)], [])


def _rms_fwd(x, g, name, comm=None):
    def fn(rv, hv, cv):
        return [_rms(rv[0], cv[0])], []
    res = _rows(fn, [x], [g.reshape(1, -1)], [(x.shape[1], MXU)], tile=512, name=name, comm=comm)
    return res[0] if comm is None else (res[0][0], res[1])


def _slope_dist(hp, hh, dist, dil):
    hf = (2 * hp + hh + 1).astype(F32)
    slope = jnp.exp(jnp.zeros(dist.shape, F32) - hf * LN2)
    return slope * (dist.astype(F32) * float(dil))


ATT_G = 2048


def _att_fwd_fused(proj, name, comm=None):
    s, npc = proj.shape
    gsz = ATT_G
    ng = s // gsz
    assert s % gsz == 0
    scale = HEAD_DIM ** -0.5
    comm = comm or _Comm()

    def body(*refs):
        (q_ref, kp_ref, kc_ref, vp_ref, vc_ref, att_ref, lse_ref, attb_ref, nn, mn, dn), cm = comm.split(refs, 5, 3, 3)
        hp, g = pl.program_id(0), pl.program_id(1)
        comm.start_at((hp == 0) & (g == 0), cm)
        lane = lax.broadcasted_iota(jnp.int32, (BLK, BLK), 1)
        qi = lax.broadcasted_iota(jnp.int32, (BLK, 2 * BLK), 0)
        ki = lax.broadcasted_iota(jnp.int32, (BLK, 2 * BLK), 1)
        dist = BLK + qi - ki
        band = (dist >= 0) & (dist <= BLK)
        for pi, (_, dil) in enumerate(ATT_PATTERNS):
            nbg = gsz // dil // BLK
            bias = [_slope_dist(hp, hh, dist, dil) for hh in (0, 1)]
            for r in range(dil):
                for b in range(nbg):
                    def rows(blk):
                        return pl.ds(blk * BLK * dil + r, BLK, stride=dil) if dil > 1 else pl.ds(blk * BLK, BLK)
                    q = q_ref[rows(b), :]
                    k_prev = kp_ref[rows(nbg - 1), :] if b == 0 else kc_ref[rows(b - 1), :]
                    v_prev = vp_ref[rows(nbg - 1), :] if b == 0 else vc_ref[rows(b - 1), :]
                    kk = jnp.concatenate([k_prev, kc_ref[rows(b), :]], axis=0).astype(MXU)
                    vv = jnp.concatenate([v_prev, vc_ref[rows(b), :]], axis=0).astype(MXU)
                    valid = (band & ((g > 0) | (ki >= BLK))) if b == 0 else band
                    num = jnp.zeros((BLK, BLK), F32)
                    mx = jnp.zeros((BLK, BLK), F32)
                    den = jnp.zeros((BLK, BLK), F32)
                    for hh in (0, 1):
                        hmask = (lane < HEAD_DIM) if hh == 0 else (lane >= HEAD_DIM)
                        qm = jnp.where(hmask, q, 0.0).astype(MXU)
                        sc = jnp.where(valid, _nt(qm, kk) * scale - bias[hh], NEG)
                        m = jnp.max(sc, axis=1, keepdims=True)
                        p = jnp.exp(sc - m)
                        dn_ = jnp.sum(p, axis=1, keepdims=True)
                        o = _nn(p.astype(MXU), vv)
                        num = jnp.where(hmask, o, num)
                        mx = jnp.where(hmask, m, mx)
                        den = jnp.where(hmask, dn_, den)
                    nn.at[pi][rows(b), :] = num
                    mn.at[pi][rows(b), :] = mx
                    dn.at[pi][rows(b), :] = den

        def merge(c, carry):
            rows = pl.ds(pl.multiple_of(c * 256, 256), 256)
            ms = [mn[pi, rows, :] for pi in range(len(ATT_PATTERNS))]
            m_all = functools.reduce(jnp.maximum, ms)
            num = jnp.zeros((256, BLK), F32)
            den = jnp.zeros((256, BLK), F32)
            for pi in range(len(ATT_PATTERNS)):
                e = jnp.exp(ms[pi] - m_all)
                num = num + nn[pi, rows, :] * e
                den = den + dn[pi, rows, :] * e
            att = num / den
            att_ref[rows, :] = att
            attb_ref[rows, :] = att.astype(MXU)
            lse_ref[rows, :] = m_all + jnp.log(den)
            return carry

        lax.fori_loop(0, gsz // 256, merge, 0)
        comm.wait_at((hp == 3) & (g == ng - 1), cm)

    def cur(base):
        return pl.BlockSpec((gsz, BLK), lambda hp, g: (g, base // BLK + hp))

    def prev(base):
        return pl.BlockSpec((gsz, BLK), lambda hp, g: (jnp.maximum(g - 1, 0), base // BLK + hp))

    o_spec = pl.BlockSpec((gsz, BLK), lambda hp, g: (g, hp))
    npat = len(ATT_PATTERNS)
    res = pl.pallas_call(
        body, name=name, grid=(4, ng),
        in_specs=[cur(C_Q), prev(C_K), cur(C_K), prev(C_V), cur(C_V)] + [ANY] * comm.n,
        out_specs=[o_spec] * 3 + [ANY] * comm.n,
        out_shape=[jax.ShapeDtypeStruct((s, ATT_W), F32)] * 2 + [jax.ShapeDtypeStruct((s, ATT_W), MXU)] + comm.out_shape(),
        scratch_shapes=[pltpu.VMEM((npat, gsz, BLK), F32)] * 3 + comm.scratch(),
        compiler_params=_cp("arbitrary", "arbitrary"),
    )(proj, proj, proj, proj, proj, *comm.args())
    return res[0], res[1], res[2], list(res[3:])


def _att_bwd_rev(proj, datt, stats, name, comm=None):
    s, npc = proj.shape
    gsz = ATT_G
    ng = s // gsz
    npat = len(ATT_PATTERNS)
    scale = HEAD_DIM ** -0.5
    comm = comm or _Comm()

    def body(*refs):
        (q_ref, kp_ref, kc_ref, vp_ref, vc_ref, do_ref, st_ref, dq_out, dk_out, dv_out,
         kcar, vcar, dq_ref, dk_ref, dv_ref), cm = comm.split(refs, 7, 3, 5)
        hp, gi = pl.program_id(0), pl.program_id(1)
        g = ng - 1 - gi
        comm.start_at((hp == 0) & (gi == 0), cm)

        @pl.when(gi == 0)
        def _():
            kcar[...] = jnp.zeros_like(kcar)
            vcar[...] = jnp.zeros_like(vcar)

        lane = lax.broadcasted_iota(jnp.int32, (BLK, BLK), 1)
        qi = lax.broadcasted_iota(jnp.int32, (BLK, 2 * BLK), 0)
        ki = lax.broadcasted_iota(jnp.int32, (BLK, 2 * BLK), 1)
        dist = BLK + qi - ki
        band = (dist >= 0) & (dist <= BLK)
        for acc in (dq_ref, dk_ref, dv_ref):
            acc[...] = jnp.zeros_like(acc)
        for pi, (_, dil) in enumerate(ATT_PATTERNS):
            nbg = gsz // dil // BLK
            bias = [_slope_dist(hp, hh, dist, dil) for hh in (0, 1)]
            for r in range(dil):
                edge = slice(pi * gsz + r * BLK, pi * gsz + (r + 1) * BLK)
                for b in reversed(range(nbg)):
                    def rows(blk):
                        return pl.ds(blk * BLK * dil + r, BLK, stride=dil) if dil > 1 else pl.ds(blk * BLK, BLK)
                    q, do, st = q_ref[rows(b), :], do_ref[rows(b), :], st_ref[rows(b), :]
                    k_prev = kp_ref[rows(nbg - 1), :] if b == 0 else kc_ref[rows(b - 1), :]
                    v_prev = vp_ref[rows(nbg - 1), :] if b == 0 else vc_ref[rows(b - 1), :]
                    kk = jnp.concatenate([k_prev, kc_ref[rows(b), :]], axis=0).astype(MXU)
                    vv = jnp.concatenate([v_prev, vc_ref[rows(b), :]], axis=0).astype(MXU)
                    valid = (band & ((g > 0) | (ki >= BLK))) if b == 0 else band
                    dq = jnp.zeros((BLK, BLK), F32)
                    dkk = jnp.zeros((2 * BLK, BLK), F32)
                    dvv = jnp.zeros((2 * BLK, BLK), F32)
                    for hh in (0, 1):
                        c0 = hh * HEAD_DIM
                        hmask = (lane < HEAD_DIM) if hh == 0 else (lane >= HEAD_DIM)
                        qm = jnp.where(hmask, q, 0.0).astype(MXU)
                        dom = jnp.where(hmask, do, 0.0).astype(MXU)
                        sc = _nt(qm, kk) * scale - bias[hh]
                        p = jnp.exp(jnp.where(valid, sc - st[:, c0:c0 + 1], NEG))
                        ds = (p * (_nt(dom, vv) - st[:, c0 + HEAD_DIM // 2:c0 + HEAD_DIM // 2 + 1])).astype(MXU)
                        dq = jnp.where(hmask, _nn(ds, kk), dq)
                        dkk = dkk + _tn(ds, qm)
                        dvv = dvv + _tn(p.astype(MXU), dom)
                    dq_ref[rows(b), :] += dq * scale
                    own_k, own_v = dkk[BLK:] * scale, dvv[BLK:]
                    if b == nbg - 1:
                        own_k, own_v = own_k + kcar[edge, :], own_v + vcar[edge, :]
                    dk_ref[rows(b), :] += own_k
                    dv_ref[rows(b), :] += own_v
                    if b > 0:
                        dk_ref[rows(b - 1), :] += dkk[:BLK] * scale
                        dv_ref[rows(b - 1), :] += dvv[:BLK]
                    else:
                        kcar[edge, :] = dkk[:BLK] * scale
                        vcar[edge, :] = dvv[:BLK]
        for out, acc in ((dq_out, dq_ref), (dk_out, dk_ref), (dv_out, dv_ref)):
            out[...] = acc[...].astype(out.dtype)
        comm.wait_at((hp == 3) & (gi == ng - 1), cm)

    def pspec(base, shift):
        return pl.BlockSpec((gsz, BLK), lambda hp, gi: (jnp.maximum(ng - 1 - gi + shift, 0), base // BLK + hp))

    wspec = pl.BlockSpec((gsz, BLK), lambda hp, gi: (ng - 1 - gi, hp))
    in_specs = [pspec(C_Q, 0), pspec(C_K, -1), pspec(C_K, 0), pspec(C_V, -1), pspec(C_V, 0), wspec, wspec] + [ANY] * comm.n
    res = pl.pallas_call(
        body, name=name, grid=(4, ng), in_specs=in_specs,
        out_specs=[wspec] * 3 + [ANY] * comm.n,
        out_shape=[jax.ShapeDtypeStruct((s, ATT_W), MXU)] * 3 + comm.out_shape(),
        scratch_shapes=[pltpu.VMEM((npat * gsz, BLK), F32)] * 2 + [pltpu.VMEM((gsz, BLK), F32)] * 3 + comm.scratch(),
        compiler_params=_cp("arbitrary", "arbitrary"),
    )(proj, proj, proj, proj, proj, datt, stats, *comm.args())
    return res[0], res[1], res[2], list(res[3:])


def _shift_down(cur, halo, sft):
    if sft == 0:
        return cur
    t = cur.shape[0]
    rolled = pltpu.roll(cur, sft, 0)
    hr = pltpu.roll(halo, sft, 0)
    row = lax.broadcasted_iota(jnp.int32, cur.shape, 0)
    return jnp.where(row < sft, jnp.tile(hr, (t // 8, 1)), rolled)


def _shift_up(cur, halo, sft):
    if sft == 0:
        return cur
    t = cur.shape[0]
    rolled = pltpu.roll(cur, t - sft, 0)
    hr = pltpu.roll(halo, 8 - sft, 0)
    row = lax.broadcasted_iota(jnp.int32, cur.shape, 0)
    return jnp.where(row >= t - sft, jnp.tile(hr, (t // 8, 1)), rolled)


def _conv(x, xh, w, b):
    y = b + x * w[CONV_K - 1:CONV_K]
    for k in range(CONV_K - 1):
        y = y + _shift_down(x, xh, CONV_K - 1 - k) * w[k:k + 1]
    return y


def _conv_bwd(x, xh, dy, dyh, w):
    dx = dy * w[CONV_K - 1:CONV_K]
    dws = []
    for k in range(CONV_K - 1):
        sft = CONV_K - 1 - k
        dx = dx + _shift_up(dy, dyh, sft) * w[k:k + 1]
        dws.append(jnp.sum(dy * _shift_down(x, xh, sft), axis=0, keepdims=True))
    dws.append(jnp.sum(dy * x, axis=0, keepdims=True))
    c = x.shape[1]
    dw = jnp.concatenate(dws + [jnp.zeros((8 - CONV_K, c), F32)], axis=0)
    return dx, dw, jnp.sum(dy, axis=0, keepdims=True)


def _pad8(w):
    return jnp.concatenate([w, jnp.zeros((8 - w.shape[0], w.shape[1]), w.dtype)], axis=0)


def _ssd_pre(proj, conv_w, conv_b, dt_bias128, name):
    def fn(rv, hv, cv):
        xbc, dtr = rv
        return [_silu(_conv(xbc, hv[0], cv[0], cv[1])), _softplus(dtr + cv[2])], []
    return _rows(fn, [(proj, 1024, C_XBC // 1024), (proj, BLK, C_DT // BLK)],
                 [_pad8(conv_w), conv_b.reshape(1, -1), dt_bias128],
                 [(1024, F32), (BLK, F32)], tile=256, name=name, halos=[(0, "prev")])


def _ssd_pre_bwd(proj, dxc, ddt, conv_w, conv_b, dt_bias128, name):
    def fn(rv, hv, cv):
        xbc, dtr, dxcb, ddtb = rv
        xh, dxch_raw, xnext = hv
        w, b, bias = cv
        pre = _conv(xbc, xh, w, b)
        sg = _sigmoid(pre)
        dpre = dxcb * (sg * (1.0 + pre * (1.0 - sg)))
        t = xbc.shape[0]
        tail = jnp.concatenate([xbc[t - 8:], xnext], axis=0)
        pre_n = _conv(tail[8:], tail[:8], w, b)
        sgn = _sigmoid(pre_n)
        dpre_h = dxch_raw * (sgn * (1.0 + pre_n * (1.0 - sgn)))
        dx, dw, db = _conv_bwd(xbc, xh, dpre, dpre_h, w)
        ddr = ddtb * _sigmoid(dtr + bias)
        return [dx, ddr], [dw, jnp.concatenate([db, jnp.zeros((7, db.shape[1]), F32)], axis=0), _colsum8(ddr)]
    return _rows(fn, [(proj, 1024, C_XBC // 1024), (proj, BLK, C_DT // BLK), dxc, ddt],
                 [_pad8(conv_w), conv_b.reshape(1, -1), dt_bias128],
                 [(1024, MXU), (BLK, MXU)], [(8, 1024), (8, 1024), (8, BLK)], tile=256, name=name,
                 halos=[(0, "prev"), (2, "next"), (0, "next")])


SSD_CPB = 1


def _head_cols(v, h0):
    lane = lax.broadcasted_iota(jnp.int32, (v.shape[0], BLK), 1)
    return jnp.where(lane < HEAD_DIM, v[:, h0:h0 + 1], v[:, h0 + 1:h0 + 2])


def _ssd_scan(xc, dt, par, name):
    s = xc.shape[0]
    nc = s // BLK

    def body(x_ref, dt_ref, par_ref, y_ref, st_ref, h_ref):
        c = pl.program_id(0)

        @pl.when(c == 0)
        def _():
            h_ref[...] = jnp.zeros_like(h_ref)

        st_ref[0] = h_ref[...]
        dt = dt_ref[...]
        a_row = -jnp.exp(par_ref[0:1, :])
        d_row = par_ref[1:2, :]
        ri = lax.broadcasted_iota(jnp.int32, (BLK, BLK), 0)
        ci = lax.broadcasted_iota(jnp.int32, (BLK, BLK), 1)
        tril = ri >= ci
        cs = _nn(tril.astype(F32), dt * a_row, HI)
        cst, dtt = cs.T, dt.T
        last = cs[BLK - 1:BLK, :]
        wcol = jnp.exp(last - cs) * dt
        ecs = jnp.exp(cs)
        elast = jnp.exp(last)
        for g in (0, 1):
            bg = x_ref[:, 512 + g * BLK:512 + (g + 1) * BLK].astype(MXU)
            cg = x_ref[:, 768 + g * BLK:768 + (g + 1) * BLK].astype(MXU)
            gm = _nt(cg, bg)
            for pp in (0, 1):
                pr = 2 * g + pp
                h0 = 2 * pr
                x2 = x_ref[:, pr * BLK:(pr + 1) * BLK]
                hprev = h_ref[pr * BLK:(pr + 1) * BLK, :]
                yp = jnp.zeros((BLK, BLK), F32)
                for hh in (0, 1):
                    h = h0 + hh
                    hmask = (ci < HEAD_DIM) if hh == 0 else (ci >= HEAD_DIM)
                    lm = jnp.exp(jnp.where(tril, cs[:, h:h + 1] - cst[h:h + 1, :], NEG))
                    mm = gm * lm * dtt[h:h + 1, :]
                    yp = yp + _nn(mm.astype(MXU), jnp.where(hmask, x2, 0.0).astype(MXU))
                y0 = _nt(cg, hprev.astype(MXU))
                y_ref[:, pr * BLK:(pr + 1) * BLK] = yp + _head_cols(ecs, h0) * y0 + _head_cols(d_row, h0) * x2
                dec = jnp.where(ri < HEAD_DIM, elast[:, h0:h0 + 1], elast[:, h0 + 1:h0 + 2])
                xw = (x2 * _head_cols(wcol, h0)).astype(MXU)
                h_ref[pr * BLK:(pr + 1) * BLK, :] = dec * hprev + _tn(xw, bg)

    return pl.pallas_call(
        body, name=name, grid=(nc,),
        in_specs=[pl.BlockSpec((BLK, 1024), lambda c: (c, 0)), pl.BlockSpec((BLK, BLK), lambda c: (c, 0)),
                  pl.BlockSpec((8, BLK), lambda c: (0, 0))],
        out_specs=[pl.BlockSpec((BLK, SSD_W), lambda c: (c, 0)), pl.BlockSpec((1, SSD_W, SSD_STATE), lambda c: (c, 0, 0))],
        out_shape=[jax.ShapeDtypeStruct((s, SSD_W), F32), jax.ShapeDtypeStruct((nc, SSD_W, SSD_STATE), F32)],
        scratch_shapes=[pltpu.VMEM((SSD_W, SSD_STATE), F32)],
        compiler_params=_cp("arbitrary"),
    )(xc, dt, par)


def _ssd_scan_bwd(xc, dt, par, st, dy, name, comm=None):
    s = xc.shape[0]
    cpb = SSD_CPB
    nb = s // (cpb * BLK)
    comm = comm or _Comm()

    def chunk(x_ref, dt_ref, par_ref, st_ref, dy_ref, dx_ref, ddt_ref, dal_ref, dd_ref, dh_ref):
        dt = dt_ref[...]
        a_row = -jnp.exp(par_ref[0:1, :])
        d_row = par_ref[1:2, :]
        ri = lax.broadcasted_iota(jnp.int32, (BLK, BLK), 0)
        ci = lax.broadcasted_iota(jnp.int32, (BLK, BLK), 1)
        tril = ri >= ci
        cs = _nn(tril.astype(F32), dt * a_row, HI)
        cst, dtt = cs.T, dt.T
        last = cs[BLK - 1:BLK, :]
        tolast = jnp.exp(last - cs)
        wcol = tolast * dt
        ecs = jnp.exp(cs)
        elast = jnp.exp(last)
        dcs_col = jnp.zeros((BLK, BLK), F32)
        ddt_col = jnp.zeros((BLK, BLK), F32)
        dcs_row = jnp.zeros((BLK, BLK), F32)
        ddt_row = jnp.zeros((BLK, BLK), F32)
        dlast = jnp.zeros((1, BLK), F32)
        ddsk = jnp.zeros((1, BLK), F32)
        for g in (0, 1):
            bg32 = x_ref[:, 512 + g * BLK:512 + (g + 1) * BLK]
            cg32 = x_ref[:, 768 + g * BLK:768 + (g + 1) * BLK]
            bg, cg = bg32.astype(MXU), cg32.astype(MXU)
            gm = _nt(cg, bg)
            dgm = jnp.zeros((BLK, BLK), F32)
            dbg = jnp.zeros((BLK, BLK), F32)
            dcg = jnp.zeros((BLK, BLK), F32)
            for pp in (0, 1):
                pr = 2 * g + pp
                h0 = 2 * pr
                x2 = x_ref[:, pr * BLK:(pr + 1) * BLK]
                dy2 = dy_ref[:, pr * BLK:(pr + 1) * BLK]
                hprev = st_ref[0, pr * BLK:(pr + 1) * BLK, :]
                dhn = dh_ref[pr * BLK:(pr + 1) * BLK, :]
                x2m, dhnm = x2.astype(MXU), dhn.astype(MXU)
                zb = _nt(bg, dhnm)
                y0 = _nt(cg, hprev.astype(MXU))
                esel = _head_cols(ecs, h0)
                wsel = _head_cols(wcol, h0)
                dx2 = _head_cols(d_row, h0) * dy2 + wsel * zb
                pick2 = (((ri < HEAD_DIM) & (ci == h0)) | ((ri >= HEAD_DIM) & (ci == h0 + 1))).astype(F32)
                sums = _nn(jnp.concatenate([dy2 * y0, x2 * zb, dy2 * x2], axis=0), pick2, HIGH)
                de2, dw2, dd2 = sums[:BLK], sums[BLK:2 * BLK], sums[2 * BLK:]
                v2 = dw2 * wcol
                dcs_col = dcs_col + ecs * de2 - v2
                ddt_col = ddt_col + dw2 * tolast
                hsum = _nn(dhn * hprev, jnp.ones((BLK, BLK), F32), HIGH)
                dlast = dlast + elast * jnp.sum(jnp.where(pick2 > 0.0, hsum, 0.0), axis=0, keepdims=True) \
                    + jnp.sum(v2, axis=0, keepdims=True)
                ddsk = ddsk + jnp.sum(dd2, axis=0, keepdims=True)
                ts = []
                for hh in (0, 1):
                    h = h0 + hh
                    hmask = (ci < HEAD_DIM) if hh == 0 else (ci >= HEAD_DIM)
                    ons = (ri == h).astype(F32)
                    dym = jnp.where(hmask, dy2, 0.0).astype(MXU)
                    dt_r = dtt[h:h + 1, :]
                    lm = jnp.exp(jnp.where(tril, cs[:, h:h + 1] - cst[h:h + 1, :], NEG))
                    mm = gm * lm * dt_r
                    dx2 = dx2 + _tn(mm.astype(MXU), dym)
                    dm = _nt(dym, x2m)
                    t1 = dm * lm
                    dgm = dgm + t1 * dt_r
                    tt = t1 * gm
                    ddt_row = ddt_row + ons * jnp.sum(tt, axis=0, keepdims=True)
                    t = tt * dt_r
                    dcs_row = dcs_row - ons * jnp.sum(t, axis=0, keepdims=True)
                    ts.append(t)
                rows2 = lax.broadcasted_iota(jnp.int32, (2 * BLK, BLK), 0)
                lane2 = lax.broadcasted_iota(jnp.int32, (2 * BLK, BLK), 1)
                to_lane = ((rows2 < BLK) & (lane2 == h0)) | ((rows2 >= BLK) & (lane2 == h0 + 1))
                dcs_col = dcs_col + _nn(jnp.concatenate(ts, axis=1), to_lane.astype(F32), HIGH)
                dx_ref[:, pr * BLK:(pr + 1) * BLK] = dx2
                edy = (esel * dy2).astype(MXU)
                dcg = dcg + _nn(edy, hprev.astype(MXU))
                dec = jnp.where(ri < HEAD_DIM, elast[:, h0:h0 + 1], elast[:, h0 + 1:h0 + 2])
                dh_ref[pr * BLK:(pr + 1) * BLK, :] = dec * dhn + _tn(edy, cg)
                dbg = dbg + _nn((x2 * wsel).astype(MXU), dhnm)
            dgmm = dgm.astype(MXU)
            dx_ref[:, 512 + g * BLK:512 + (g + 1) * BLK] = dbg + _tn(dgmm, cg)
            dx_ref[:, 768 + g * BLK:768 + (g + 1) * BLK] = dcg + _nn(dgmm, bg)
        dcs = dcs_col + dcs_row.T + jnp.where(ri == BLK - 1, dlast, 0.0)
        dda = _nn((ri <= ci).astype(F32), dcs, HI)
        ddt_ref[...] = ddt_col + ddt_row.T + a_row * dda
        da = jnp.sum(dt * dda, axis=0, keepdims=True)
        dal_ref[0:1, :] += da * a_row
        dd_ref[0:1, :] += ddsk

    def body(*refs):
        (x_ref, dt_ref, par_ref, st_ref, dy_ref, dx_ref, ddt_ref, dal_ref, dd_ref, dh_ref), cm = comm.split(refs, 5, 4, 1)
        c = pl.program_id(0)
        comm.start_at(c == 0, cm)

        @pl.when(c == 0)
        def _():
            dh_ref[...] = jnp.zeros_like(dh_ref)
            dal_ref[...] = jnp.zeros_like(dal_ref)
            dd_ref[...] = jnp.zeros_like(dd_ref)

        for cc in reversed(range(cpb)):
            rows = pl.ds(cc * BLK, BLK)
            chunk(x_ref.at[rows], dt_ref.at[rows], par_ref, st_ref.at[pl.ds(cc, 1)], dy_ref.at[rows], dx_ref.at[rows],
                  ddt_ref.at[rows], dal_ref, dd_ref, dh_ref)
        comm.wait_at(c == nb - 1, cm)

    rev = lambda c: (nb - 1 - c, 0)
    tb = cpb * BLK
    res = pl.pallas_call(
        body, name=name, grid=(nb,),
        in_specs=[pl.BlockSpec((tb, 1024), rev), pl.BlockSpec((tb, BLK), rev), pl.BlockSpec((8, BLK), lambda c: (0, 0)),
                  pl.BlockSpec((cpb, SSD_W, SSD_STATE), lambda c: (nb - 1 - c, 0, 0)), pl.BlockSpec((tb, SSD_W), rev)]
        + [ANY] * comm.n,
        out_specs=[pl.BlockSpec((tb, 1024), rev), pl.BlockSpec((tb, BLK), rev),
                   pl.BlockSpec((8, BLK), lambda c: (0, 0)), pl.BlockSpec((8, BLK), lambda c: (0, 0))] + [ANY] * comm.n,
        out_shape=[jax.ShapeDtypeStruct((s, 1024), F32), jax.ShapeDtypeStruct((s, BLK), F32),
                   jax.ShapeDtypeStruct((8, BLK), F32), jax.ShapeDtypeStruct((8, BLK), F32)] + comm.out_shape(),
        scratch_shapes=[pltpu.VMEM((SSD_W, SSD_STATE), F32)] + comm.scratch(),
        compiler_params=_cp("arbitrary"),
    )(xc, dt, par, st, dy, *comm.args())
    return res[0], res[1], res[2], res[3], list(res[4:])


def _ssd_gate(y, z, w):
    t = y * _silu(z)
    outs = []
    for g in (0, 1):
        tg = t[:, g * 256:(g + 1) * 256]
        outs.append(tg * lax.rsqrt(jnp.mean(tg * tg, axis=-1, keepdims=True) + SSD_NORM_EPS))
    return jnp.concatenate(outs, axis=1) * w


def _ssd_post(y, proj, norm_w, name):
    def fn(rv, hv, cv):
        return [_ssd_gate(rv[0], rv[1], cv[0])], []
    return _rows(fn, [y, (proj, SSD_W, C_Z // SSD_W)], [norm_w.reshape(1, -1)], [(SSD_W, MXU)], tile=512, name=name)[0]


def _ssd_post_bwd(y, proj, norm_w, dout, name):
    def fn(rv, hv, cv):
        yb, zb, db = rv
        _, vjp = jax.vjp(lambda a, b: _ssd_gate(a, b, cv[0]), yb, zb)
        dy, dz = vjp(db)
        t = yb * _silu(zb)
        nrm = []
        for g in (0, 1):
            tg = t[:, g * 256:(g + 1) * 256]
            nrm.append(tg * lax.rsqrt(jnp.mean(tg * tg, axis=-1, keepdims=True) + SSD_NORM_EPS))
        return [dy, dz], [_colsum8(db * jnp.concatenate(nrm, axis=1))]
    return _rows(fn, [y, (proj, SSD_W, C_Z // SSD_W), dout], [norm_w.reshape(1, -1)],
                 [(SSD_W, F32), (SSD_W, MXU)], [(8, SSD_W)], tile=512, name=name)


LRU_T = 256


def _lru_conv(proj, conv_w, conv_b, name):
    def fn(rv, hv, cv):
        return [_conv(rv[0], hv[0], cv[0], cv[1])], []
    return _rows(fn, [(proj, LRU_W, C_XL // LRU_W)], [_pad8(conv_w), conv_b.reshape(1, -1)], [(LRU_W, F32)],
                 tile=512, name=name, halos=[(0, "prev")])[0]


def _lru_conv_bwd(proj, dxc, conv_w, name):
    def fn(rv, hv, cv):
        dx, dw, db = _conv_bwd(rv[0], hv[0], rv[1], hv[1], cv[0])
        return [dx], [dw, jnp.concatenate([db, jnp.zeros((7, db.shape[1]), F32)], axis=0)]
    return _rows(fn, [(proj, LRU_W, C_XL // LRU_W), dxc], [_pad8(conv_w)], [(LRU_W, MXU)], [(8, LRU_W), (8, LRU_W)],
                 tile=512, name=name, halos=[(0, "prev"), (1, "next")])


def _lru_au(pre_a, pre_x, xc, ba, bx, lam):
    r = _sigmoid(pre_a + ba)
    i = _sigmoid(pre_x + bx)
    log_a = -LRU_C * r * _softplus(-lam)
    a = jnp.exp(log_a)
    u = jnp.sqrt(1.0 - jnp.exp(2.0 * log_a)) * (i * xc)
    return a, u


def _lru_scan(pre, xc, proj, par, name):
    s = xc.shape[0]
    t = LRU_T

    def body(pre_ref, xc_ref, g_ref, par_ref, out_ref, h_ref, carry):
        c = pl.program_id(0)

        @pl.when(c == 0)
        def _():
            carry[...] = jnp.zeros_like(carry)

        a, u = _lru_au(pre_ref[:, :LRU_W], pre_ref[:, LRU_W:], xc_ref[...], par_ref[0:1, :], par_ref[1:2, :], par_ref[2:3, :])
        row = lax.broadcasted_iota(jnp.int32, (t, LRU_W), 0)
        sft = 1
        while sft < t:
            keep = row >= sft
            a_s = jnp.where(keep, pltpu.roll(a, sft, 0), 1.0)
            u_s = jnp.where(keep, pltpu.roll(u, sft, 0), 0.0)
            u = a * u_s + u
            a = a * a_s
            sft *= 2
        h = a * carry[0:1, :] + u
        h_ref[...] = h
        out_ref[...] = (h * _gelu(g_ref[...])).astype(out_ref.dtype)
        carry[0:1, :] = h[t - 1:t, :]

    return pl.pallas_call(
        body, name=name, grid=(s // t,),
        in_specs=[pl.BlockSpec((t, 2 * LRU_W), lambda c: (c, 0)), pl.BlockSpec((t, LRU_W), lambda c: (c, 0)),
                  pl.BlockSpec((t, LRU_W), lambda c: (c, C_G // LRU_W)), pl.BlockSpec((8, LRU_W), lambda c: (0, 0))],
        out_specs=[pl.BlockSpec((t, LRU_W), lambda c: (c, 0))] * 2,
        out_shape=[jax.ShapeDtypeStruct((s, LRU_W), MXU), jax.ShapeDtypeStruct((s, LRU_W), F32)],
        scratch_shapes=[pltpu.VMEM((8, LRU_W), F32)],
        compiler_params=_cp("arbitrary"),
    )(pre, xc, proj, par)


def _lru_scan_bwd(pre, xc, proj, par, h, dout, name):
    s = xc.shape[0]
    t = LRU_T
    n = s // t
    t8 = t // 8

    def body(pre_ref, xc_ref, g_ref, par_ref, h_ref, hh_ref, do_ref, dpre_ref, dxc_ref, dg_ref, dpar_ref, carry):
        c = pl.program_id(0)

        @pl.when(c == 0)
        def _():
            carry[...] = jnp.zeros_like(carry)
            dpar_ref[...] = jnp.zeros_like(dpar_ref)

        pa, px, xcb = pre_ref[:, :LRU_W], pre_ref[:, LRU_W:], xc_ref[...]
        ba, bx, lam = par_ref[0:1, :], par_ref[1:2, :], par_ref[2:3, :]
        (a, u), vjp = jax.vjp(_lru_au, pa, px, xcb, ba, bx, lam)
        g = g_ref[...]
        hcur = h_ref[...]
        do = do_ref[...]
        _, gvjp = jax.vjp(_gelu, g)
        dg_ref[...] = gvjp(do * hcur)[0].astype(dg_ref.dtype)
        row = lax.broadcasted_iota(jnp.int32, (t, LRU_W), 0)
        v = do * _gelu(g) + jnp.where(row == t - 1, carry[0:1, :], 0.0)
        b = jnp.where(row == t - 1, 0.0, pltpu.roll(a, t - 1, 0))
        sft = 1
        while sft < t:
            keep = row < t - sft
            b_s = jnp.where(keep, pltpu.roll(b, t - sft, 0), 1.0)
            v_s = jnp.where(keep, pltpu.roll(v, t - sft, 0), 0.0)
            v = b * v_s + v
            b = b * b_s
            sft *= 2
        dh = v
        carry[0:1, :] = a[0:1, :] * dh[0:1, :]
        hhalo = jnp.where(c == n - 1, 0.0, hh_ref[...])
        hprev = _shift_down(hcur, hhalo, 1)
        dpa, dpx, dxc, dba, dbx, dlam = vjp((dh * hprev, dh))
        dpre_ref[:, :LRU_W] = dpa
        dpre_ref[:, LRU_W:] = dpx
        dxc_ref[...] = dxc
        dpar_ref[0:1, :] += dba
        dpar_ref[1:2, :] += dbx
        dpar_ref[2:3, :] += dlam

    rev = lambda c: (n - 1 - c, 0)
    return pl.pallas_call(
        body, name=name, grid=(n,),
        in_specs=[pl.BlockSpec((t, 2 * LRU_W), rev), pl.BlockSpec((t, LRU_W), rev),
                  pl.BlockSpec((t, LRU_W), lambda c: (n - 1 - c, C_G // LRU_W)), pl.BlockSpec((8, LRU_W), lambda c: (0, 0)),
                  pl.BlockSpec((t, LRU_W), rev),
                  pl.BlockSpec((8, LRU_W), lambda c: (jnp.maximum((n - 1 - c) * t8 - 1, 0), 0)),
                  pl.BlockSpec((t, LRU_W), lambda c: (n - 1 - c, dout.shape[1] // LRU_W - 1))],
        out_specs=[pl.BlockSpec((t, 2 * LRU_W), rev), pl.BlockSpec((t, LRU_W), rev), pl.BlockSpec((t, LRU_W), rev),
                   pl.BlockSpec((8, LRU_W), lambda c: (0, 0))],
        out_shape=[jax.ShapeDtypeStruct((s, 2 * LRU_W), F32), jax.ShapeDtypeStruct((s, LRU_W), F32),
                   jax.ShapeDtypeStruct((s, LRU_W), MXU), jax.ShapeDtypeStruct((8, LRU_W), F32)],
        scratch_shapes=[pltpu.VMEM((8, LRU_W), F32)],
        compiler_params=_cp("arbitrary"),
    )(pre, xc, proj, par, h, h, dout)


def _swiglu_act(gu, name):
    def fn(rv, hv, cv):
        return [_silu(rv[0].astype(F32)) * rv[1].astype(F32)], []
    return _rows(fn, [(gu, D_FF, 0), (gu, D_FF, 1)], [], [(D_FF, MXU)], tile=256, name=name)[0]


def _epi_swiglu_bwd(gu):
    def fn(r, rows, consts):
        gt, up = rows[0][:, :D_FF].astype(F32), rows[0][:, D_FF:].astype(F32)
        sg = _sigmoid(gt)
        dgate = r * up * (sg * (1.0 + gt * (1.0 - sg)))
        dup = r * (gt * sg)
        return r, [jnp.concatenate([dgate, dup], axis=1)], []
    return (fn, [gu], [], [(2 * D_FF, MXU)], [])


def _loss_head(x, g, target, name):
    d = x.shape[1]

    def fn(rv, hv, cv):
        xb, tb = rv
        y, vjp = jax.vjp(_rms, xb, cv[0])
        err = y - tb
        dy = err * (1.0 / d)
        dx, _ = vjp(dy)
        rstd = lax.rsqrt(jnp.mean(xb * xb, axis=-1, keepdims=True) + NORM_EPS)
        e2 = err * err * (0.5 / d)
        e2 = functools.reduce(lambda a, b: a + b, [e2[:, k * BLK:(k + 1) * BLK] for k in range(d // BLK)])
        return [dx], [_colsum8(dy * xb * rstd), _colsum8(e2)]
    return _rows(fn, [x, target], [g.reshape(1, -1)], [(d, F32)], [(8, d), (8, BLK)], tile=512, name=name)


ANY = pl.BlockSpec(memory_space=pl.ANY)


def _coords():
    return lax.axis_index("x"), lax.axis_index("y"), lax.axis_index("c")


class _Comm:
    def __init__(self, gathers=(), scatters=()):
        self.gathers = list(gathers)
        self.scatters = list(scatters)
        self.n = len(self.gathers) + len(self.scatters)

    def args(self):
        return [g[0] for g in self.gathers] + self.scatters

    def out_shape(self):
        out = [jax.ShapeDtypeStruct((4,) + (a.shape if l is None else a.shape[1:]), a.dtype) for a, l, _ in self.gathers]
        return out + [jax.ShapeDtypeStruct((3,) + a.shape[1:], a.dtype) for a in self.scatters]

    def scratch(self):
        if not self.n:
            return []
        return [pltpu.SemaphoreType.DMA((3 * self.n,)), pltpu.SemaphoreType.DMA((3 * self.n,)),
                pltpu.SemaphoreType.DMA((max(len(self.gathers), 1),)),
                pltpu.SemaphoreType.DMA((3 * self.n,)), pltpu.SemaphoreType.DMA((3 * self.n,))]

    def split(self, refs, n_in, n_out, n_scratch):
        refs = list(refs)
        n = self.n
        own = refs[:n_in] + refs[n_in + n:n_in + n + n_out] + refs[n_in + 2 * n + n_out:n_in + 2 * n + n_out + n_scratch]
        cm = (refs[n_in:n_in + n], refs[n_in + n + n_out:n_in + 2 * n + n_out], refs[n_in + 2 * n + n_out + n_scratch:])
        return own, cm

    def _copies(self, cm, arriving):
        ins, outs, (send, recv, local, _, _) = cm
        x, y, c = _coords()
        me = 2 * x + y
        chips = [(1 - x, y), (x, 1 - y), (1 - x, 1 - y)]
        remote, locals_ = [], []
        ng = len(self.gathers)
        for i in range(self.n):
            if i < ng:
                _, l, halved = self.gathers[i]
                slab = ins[i] if l is None else ins[i].at[l]
                if not arriving:
                    locals_.append(pltpu.make_async_copy(slab, outs[i].at[me], local.at[i]))
            for j, (px, py) in enumerate(chips):
                if i < ng:
                    slot = 2 * px + py if arriving else me
                    src, dst = (slab.at[c], outs[i].at[slot, c]) if halved else (slab, outs[i].at[slot])
                else:
                    src, dst = ins[i].at[2 * px + py], outs[i].at[j]
                remote.append(pltpu.make_async_remote_copy(src, dst, send.at[3 * i + j], recv.at[3 * i + j],
                                                           device_id=(px, py, c), device_id_type=MESH))
        return remote, locals_

    def _handovers(self, cm, arriving):
        _, outs, (_, _, _, send, recv) = cm
        x, y, c = _coords()
        chips = [(1 - x, y), (x, 1 - y), (1 - x, 1 - y)]
        cps = []
        for i, (_, _, halved) in enumerate(self.gathers):
            if halved:
                for j, (px, py) in enumerate(chips):
                    src = outs[i].at[2 * px + py, c]
                    dst = outs[i].at[2 * px + py, 1 - c if arriving else c]
                    cps.append(pltpu.make_async_remote_copy(src, dst, send.at[3 * i + j], recv.at[3 * i + j],
                                                            device_id=(x, y, 1 - c), device_id_type=MESH))
        return cps

    def start_at(self, cond, cm):
        def go():
            remote, locals_ = self._copies(cm, False)
            for cp in locals_ + remote:
                cp.start()

        if self.n:
            go() if cond is True else pl.when(cond)(go)

    def wait_at(self, cond, cm):
        def go():
            for cp in self._copies(cm, True)[0]:
                cp.wait_recv()
            handed = self._handovers(cm, False)
            for cp in handed:
                cp.start()
            for cp in self._handovers(cm, True):
                cp.wait_recv()
            remote, locals_ = self._copies(cm, False)
            for cp in handed + remote:
                cp.wait_send()
            for cp in locals_:
                cp.wait()

        if self.n:
            go() if cond is True else pl.when(cond)(go)


def _swap_sibling(arrs):
    n = len(arrs)

    def body(*refs):
        ins, outs, send, recv = refs[:n], refs[n:2 * n], refs[2 * n], refs[2 * n + 1]
        x, y, c = _coords()
        cps = [pltpu.make_async_remote_copy(ins[i], outs[i], send.at[i], recv.at[i], device_id=(x, y, 1 - c), device_id_type=MESH)
               for i in range(n)]
        for cp in cps:
            cp.start()
        for cp in cps:
            cp.wait_recv()
        for cp in cps:
            cp.wait_send()

    return list(pl.pallas_call(
        body, name="swap_sibling", in_specs=[ANY] * n, out_specs=[ANY] * n,
        out_shape=[jax.ShapeDtypeStruct(a.shape, a.dtype) for a in arrs],
        scratch_shapes=[pltpu.SemaphoreType.DMA((n,)), pltpu.SemaphoreType.DMA((n,))],
        compiler_params=pltpu.CompilerParams(has_side_effects=True),
    )(*arrs))


def _gather_small(gs):
    def body(g_ref, o_ref, send_sems, recv_sems, local_sem):
        x, y, c = _coords()
        me = 4 * x + 2 * y + c
        mine = pltpu.make_async_copy(g_ref, o_ref.at[me], local_sem)
        mine.start()
        sends = []
        for k in range(1, 8):
            px, py, pc = x ^ (k >> 2), y ^ ((k >> 1) & 1), c ^ (k & 1)
            sends.append((pltpu.make_async_remote_copy(g_ref, o_ref.at[me], send_sems.at[k - 1], recv_sems.at[k - 1],
                                                       device_id=(px, py, pc), device_id_type=MESH), 4 * px + 2 * py + pc, k))
        for cp, _, _ in sends:
            cp.start()
        for cp, src, k in sends:
            pltpu.make_async_remote_copy(g_ref, o_ref.at[src], send_sems.at[k - 1], recv_sems.at[k - 1],
                                         device_id=(x, y, c), device_id_type=MESH).wait_recv()
        for cp, _, _ in sends:
            cp.wait_send()
        mine.wait()

    return pl.pallas_call(
        body, name="gather_small", in_specs=[ANY], out_specs=ANY,
        out_shape=jax.ShapeDtypeStruct((8,) + gs.shape, gs.dtype),
        scratch_shapes=[pltpu.SemaphoreType.DMA((7,)), pltpu.SemaphoreType.DMA((7,)), pltpu.SemaphoreType.DMA],
        compiler_params=pltpu.CompilerParams(has_side_effects=True),
    )(gs)


def _sum_slots(own, others, name, tile):
    k, r, c = others.shape

    def body(*refs):
        if own is None:
            o_ref, out_ref = refs
            acc = o_ref[0].astype(F32)
            first = 1
        else:
            own_ref, o_ref, out_ref = refs
            acc = own_ref[...]
            first = 0
        for j in range(first, k):
            acc = acc + o_ref[j].astype(F32)
        out_ref[...] = acc

    row = pl.BlockSpec((tile, c), lambda i: (i, 0))
    specs = ([] if own is None else [row]) + [pl.BlockSpec((k, tile, c), lambda i: (0, i, 0))]
    args = ([] if own is None else [own]) + [others]
    return pl.pallas_call(body, name=name, grid=(r // tile,), in_specs=specs, out_specs=row,
                          out_shape=jax.ShapeDtypeStruct((r, c), F32), compiler_params=_cp("parallel"))(*args)


def _adamw(w, m, v, ga, gb, name, tile, rows_first=False):
    lead = 0 if rows_first else w.ndim - 2
    r, c = w.shape[-2:]

    def body(*refs):
        vals = [ref[0] if lead else ref[...] for ref in refs[:len(refs) - 4]]
        w_, m_, v_, g = vals[0], vals[1], vals[2], vals[3]
        if gb is not None:
            g = g + vals[4]
        nm = ADAM_B1 * m_ + (1.0 - ADAM_B1) * g
        nv = ADAM_B2 * v_ + (1.0 - ADAM_B2) * (g * g)
        d = -ADAM_LR * ((nm / BC1) / (jnp.sqrt(nv / BC2) + ADAM_EPS) + ADAM_WD * w_)
        for ref, val in zip(refs[len(refs) - 4:], (g, d, nm, nv)):
            if lead:
                ref[0] = val
            else:
                ref[...] = val

    if rows_first:
        row = pl.BlockSpec((tile,) + w.shape[1:], lambda i: (i, 0, 0))
        grid = (w.shape[0] // tile,)
    elif lead:
        row = pl.BlockSpec((1, tile, c), lambda l, i: (l, i, 0))
        grid = (w.shape[0], r // tile)
    else:
        row = pl.BlockSpec((tile, c), lambda i: (i, 0))
        grid = (r // tile,)
    args = [w, m, v, ga] + ([] if gb is None else [gb])
    return pl.pallas_call(body, name=name, grid=grid, in_specs=[row] * len(args), out_specs=[row] * 4,
                          out_shape=[jax.ShapeDtypeStruct(w.shape, F32)] * 4,
                          compiler_params=_cp(*(["parallel"] * len(grid))))(*args)


MATS = ("w_in", "w_out", "w_gate", "w_up", "w_down")
CONVS = ("ssd_conv_w", "lru_conv_w")
BIG = MATS + CONVS
TRANSPOSED = ("w_gate", "w_up")
COL_SHARDED = ("ssd_conv_w", "lru_conv_w")
W_IN_SHARD = IN_COLS // 4
W_IN_PAD = 1056
SMALL = ("norm_mix", "ssd_conv_b", "ssd_dt_bias", "ssd_a_log", "ssd_d", "ssd_norm", "lru_conv_b", "lru_wa", "lru_ba",
         "lru_wx", "lru_bx", "lru_lambda", "norm_ffn", "norm_final")
WEIGHTS = ("norm_mix", "w_in", "ssd_conv_w", "ssd_conv_b", "ssd_dt_bias", "ssd_a_log", "ssd_d", "ssd_norm", "lru_conv_w",
           "lru_conv_b", "lru_wa", "lru_ba", "lru_wx", "lru_bx", "lru_lambda", "w_out", "norm_ffn", "w_gate", "w_up",
           "w_down", "norm_final")
ROW_TILE = {"w_in": W_IN_SHARD, "w_out": 128, "w_gate": 352, "w_up": 352, "w_down": 352}
W_IN_ADAM_TILE = 54


def _pack(arrs, width, row_mult, dtype):
    flat = jnp.concatenate([a.reshape(-1).astype(dtype) for a in arrs])
    rows = -(-flat.shape[0] // width)
    rows = -(-rows // row_mult) * row_mult
    flat = jnp.pad(flat, (0, rows * width - flat.shape[0]))
    return flat.reshape(rows, width)


def _unpack(buf, shapes):
    flat = buf.reshape(-1)
    out, off = [], 0
    for shp in shapes:
        n = int(np.prod(shp))
        out.append(flat[off:off + n].reshape(shp))
        off += n
    return out


def _join(name, g4):
    if name in COL_SHARDED:
        return jnp.moveaxis(g4, 0, -2).reshape(g4.shape[1:-1] + (4 * g4.shape[-1],))
    return g4.reshape((4 * g4.shape[1],) + g4.shape[2:])


def _slabs(name, g):
    if name in COL_SHARDED:
        return jnp.moveaxis(g.reshape(g.shape[:-1] + (4, g.shape[-1] // 4)), -2, 0)
    return g.reshape((4, g.shape[0] // 4) + g.shape[1:])


def _w_in_rows(g4):
    def nat(lo, hi):
        out = []
        while lo < hi:
            j = lo // W_IN_SHARD
            stop = min(hi, (j + 1) * W_IN_SHARD)
            out.append((j, lo - j * W_IN_SHARD, stop - lo))
            lo = stop
        return out
    pieces = nat(0, 3072) + nat(3080, IN_COLS) + nat(3072, 3080)

    def body(g_ref, o_ref):
        row = 0
        for j, first, n in pieces:
            o_ref[row:row + n, :] = g_ref[j, first:first + n, :]
            row += n
        o_ref[row:, :] = jnp.zeros((NP - row, o_ref.shape[1]), o_ref.dtype)

    return pl.pallas_call(body, name="w_in_rows", out_shape=jax.ShapeDtypeStruct((NP, g4.shape[-1]), g4.dtype),
                          compiler_params=pltpu.CompilerParams(vmem_limit_bytes=VMEM_LIMIT))(g4)


def _w_in_slabs(gt):
    def kern(n):
        return n if n < 3072 else (C_DT + n - 3072 if n < 3080 else n - 8)
    slabs = []
    for j in range(4):
        lo, hi = j * W_IN_SHARD, (j + 1) * W_IN_SHARD
        cuts = sorted({lo, hi} | {c for c in (3072, 3080) if lo < c < hi})
        slabs.append(jnp.concatenate([gt[kern(a):kern(a) + b - a] for a, b in zip(cuts[:-1], cuts[1:])], axis=0))
    return jnp.stack(slabs, axis=0)


def _block_diag(w):
    eye = jnp.eye(LRU_BLOCKS, dtype=w.dtype)
    return jnp.einsum("ncd,nm->ncmd", w, eye).reshape(LRU_W, LRU_W)


def _block_diag_extract(g):
    g4 = g.reshape(LRU_BLOCKS, 64, LRU_BLOCKS, 64)
    return jnp.stack([g4[n, :, n, :] for n in range(LRU_BLOCKS)], axis=0)


def _lanes128(v):
    return jnp.pad(v, (0, BLK - v.shape[0])).reshape(1, BLK)


def _layer_mixers(x, p, comm=None, h=None):
    if h is None:
        h = _rms_fwd(x, p["norm_mix"], "rms_mix")
    proj = _mm(h, p["w_in_t"], tb=True, tm=1024, tn=1408, tk=1024, name="mm_in")
    att, lse, attb, got = _att_fwd_fused(proj, "att_fwd", comm)
    xconv, dt = _ssd_pre(proj, p["ssd_conv_w"], p["ssd_conv_b"], _lanes128(p["ssd_dt_bias"]), "ssd_pre")
    spar = jnp.concatenate([_lanes128(p["ssd_a_log"]), _lanes128(p["ssd_d"]), jnp.zeros((6, BLK), F32)], axis=0)
    y, states = _ssd_scan(xconv, dt, spar, "ssd_scan")
    ssd = _ssd_post(y, proj, p["ssd_norm"], "ssd_post")
    xc = _lru_conv(proj, p["lru_conv_w"], p["lru_conv_b"], "lru_conv")
    wab = jnp.concatenate([_block_diag(p["lru_wa"]), _block_diag(p["lru_wx"])], axis=1).astype(MXU)
    pre = _mm(xc, wab, tm=1024, tn=1024, tk=512, name="mm_lru")
    lpar = jnp.concatenate([p["lru_ba"].reshape(1, -1), p["lru_bx"].reshape(1, -1), p["lru_lambda"].reshape(1, -1),
                            jnp.zeros((5, LRU_W), F32)], axis=0)
    lru, hs = _lru_scan(pre, xc, proj, lpar, "lru_scan")
    mix = jnp.concatenate([attb, ssd, lru], axis=1)
    saved = dict(x=x, h=h, proj=proj, att=att, lse=lse, xconv=xconv, dt=dt, spar=spar, y=y, states=states, xc=xc, wab=wab,
                 pre=pre, lpar=lpar, hs=hs, mix=mix)
    return mix, saved, got


def _layer_ffn(x, mix, p, saved, comms=(None, None, None), next_norm=None):
    comm_out, comm, comm_down = comms
    x1 = _mm(mix, p["w_out"], add=x, tm=1024, tn=1024, tk=1536, name="mm_out", epi=_epi_rms(p["norm_ffn"]), comm=comm_out)
    (x1, h2), got_out = x1 if comm_out is not None else (x1, [])
    gu = _mm(h2, p["w_gu_t"], tb=True, out_dtype=MXU, tm=1024, tn=1408, tk=1024, name="mm_gu", comm=comm)
    gu, got = gu if comm is not None else (gu, [])
    x2 = _mm(gu, p["w_down"], add=x1, tm=512, tn=1024, tk=D_FF, name="mm_down", comm=comm_down,
             epi=None if next_norm is None else _epi_rms(next_norm),
             a_pro=lambda t: _silu(t[:, :D_FF].astype(F32)) * t[:, D_FF:].astype(F32))
    x2, got_down = x2 if comm_down is not None else (x2, [])
    x2, h_next, act = x2 if next_norm is not None else (x2[0], None, x2[1])
    saved.update(x1=x1, h2=h2, gu=gu, act=act)
    return x2, got_out + got + got_down, h_next


def _layer_bwd(dx2, p, sv, comm_ssd=None, comm_att=None, comm_tail=None):
    g = {}
    _, dgu = _mm(dx2, p["w_down"], tb=True, out_dtype=MXU, tm=512, tn=D_FF, tk=1024, name="mm_d_act", epi=_epi_swiglu_bwd(sv["gu"]))
    g["w_down"], g["w_down@wire"] = _mm(sv["act"], dx2, ta=True, tm=1408, tn=1024, tk=1024, name="mm_g_down", epi=_epi_wire(D_MODEL))
    dx1, gn = _mm(dgu, p["w_gu_t"], tm=1024, tn=1024, tk=1408, name="mm_d_h2", epi=_epi_rms_bwd(sv["x1"], p["norm_ffn"], dx2))
    g["w_gu_t"], g["w_gu_t@wire"] = _mm(dgu, sv["h2"], ta=True, tm=1408, tn=1024, tk=1024, name="mm_g_gu", epi=_epi_wire(D_MODEL))
    g["norm_ffn"] = jnp.sum(gn, axis=0)
    dmix, stats = _mm(dx1, p["w_out"], tb=True, tm=1024, tn=1536, tk=1024, name="mm_d_mix", epi=_epi_att_stats(sv["att"], sv["lse"]))
    g["w_out"], g["w_out@wire"] = _mm(sv["mix"], dx1, ta=True, tm=1536, tn=1024, tk=1024, name="mm_g_out", epi=_epi_wire(D_MODEL))
    proj = sv["proj"]
    dpre, dxc_u, dgl, dlpar = _lru_scan_bwd(sv["pre"], sv["xc"], proj, sv["lpar"], sv["hs"], dmix, "lru_scan_bwd")
    dxc = _mm(dpre, sv["wab"], tb=True, add=dxc_u, tm=1024, tn=512, tk=1024, name="mm_d_xc")
    gwab = _mm(sv["xc"], dpre, ta=True, tm=512, tn=1024, tk=1024, name="mm_g_lru")
    g["lru_wa"], g["lru_wx"] = _block_diag_extract(gwab[:, :LRU_W]), _block_diag_extract(gwab[:, LRU_W:])
    g["lru_ba"], g["lru_bx"], g["lru_lambda"] = dlpar[0], dlpar[1], dlpar[2]
    dxl, gcw, gcb = _lru_conv_bwd(proj, dxc, p["lru_conv_w"], "lru_conv_bwd")
    g["lru_conv_w"], g["lru_conv_b"] = gcw[:CONV_K], jnp.sum(gcb, axis=0)
    dy, dz, gsn = _ssd_post_bwd(sv["y"], proj, p["ssd_norm"], (dmix, SSD_W, 1), "ssd_post_bwd")
    g["ssd_norm"] = jnp.sum(gsn, axis=0)
    dxconv, ddt, dal, ddk, got_ssd = _ssd_scan_bwd(sv["xconv"], sv["dt"], sv["spar"], sv["states"], dy, "ssd_scan_bwd", comm_ssd)
    g["ssd_a_log"], g["ssd_d"] = dal[0, :8], ddk[0, :8]
    dxbc, ddtr, gsw, gsb, gdb = _ssd_pre_bwd(proj, dxconv, ddt, p["ssd_conv_w"], p["ssd_conv_b"],
                                             _lanes128(p["ssd_dt_bias"]), "ssd_pre_bwd")
    g["ssd_conv_w"], g["ssd_conv_b"], g["ssd_dt_bias"] = gsw[:CONV_K], jnp.sum(gsb, axis=0), jnp.sum(gdb, axis=0)[:8]
    dq, dk, dv, got_att = _att_bwd_rev(proj, dmix, stats, "att_bwd", None if comm_att is None else comm_att(g))
    dproj = jnp.concatenate([dq, dk, dv, dz, dxbc, dgl, dxl, ddtr], axis=1)
    g["w_in_t"], g["w_in_t@wire"] = _mm(dproj, sv["h"], ta=True, tm=1408, tn=1024, tk=1024, name="mm_g_in", epi=_epi_wire(D_MODEL))
    res = _mm(dproj, p["w_in_t"], tm=1024, tn=1024, tk=1408, name="mm_d_h", comm=None if comm_tail is None else comm_tail(g),
              epi=_epi_rms_bwd(sv["x"], p["norm_mix"], dx1))
    (dx, gm), got_tail = res if comm_tail is not None else (res, [])
    g["norm_mix"] = jnp.sum(gm, axis=0)
    return dx, g, got_ssd, got_att, got_tail


def _grad_slabs(g, names, suffix=""):
    out = {}
    for n in names:
        if n == "w_in":
            out[n] = _w_in_slabs(g["w_in_t" + suffix])
        elif n == "w_gate":
            out[n] = _slabs(n, g["w_gu_t" + suffix][:D_FF])
        elif n == "w_up":
            out[n] = _slabs(n, g["w_gu_t" + suffix][D_FF:])
        else:
            out[n] = _slabs(n, g[n + suffix])
    return out


def kernel(x, norm_mix, w_in, ssd_conv_w, ssd_conv_b, ssd_dt_bias, ssd_a_log, ssd_d, ssd_norm, lru_conv_w, lru_conv_b, lru_wa, lru_ba, lru_wx, lru_bx, lru_lambda, w_out, norm_ffn, w_gate, w_up, w_down, norm_final, loss_target, m_norm_mix, m_w_in, m_ssd_conv_w, m_ssd_conv_b, m_ssd_dt_bias, m_ssd_a_log, m_ssd_d, m_ssd_norm, m_lru_conv_w, m_lru_conv_b, m_lru_wa, m_lru_ba, m_lru_wx, m_lru_bx, m_lru_lambda, m_w_out, m_norm_ffn, m_w_gate, m_w_up, m_w_down, m_norm_final, v_norm_mix, v_w_in, v_ssd_conv_w, v_ssd_conv_b, v_ssd_dt_bias, v_ssd_a_log, v_ssd_d, v_ssd_norm, v_lru_conv_w, v_lru_conv_b, v_lru_wa, v_lru_ba, v_lru_wx, v_lru_bx, v_lru_lambda, v_w_out, v_norm_ffn, v_w_gate, v_w_up, v_w_down, v_norm_final):
    loc = dict(locals())
    w = {n: loc[n] for n in WEIGHTS}
    m = {n: loc["m_" + n] for n in WEIGHTS}
    v = {n: loc["v_" + n] for n in WEIGHTS}
    for n in TRANSPOSED:
        w[n], m[n], v[n] = [jnp.transpose(t, (0, 2, 1)) for t in (w[n], m[n], v[n])]
    wt_in, mt_in, vt_in = [jnp.transpose(t, (2, 0, 1)) for t in (w["w_in"], m["w_in"], v["w_in"])]

    def halves(a):
        return a.reshape(a.shape[0], 2, a.shape[1] // 2, a.shape[2])

    def unhalve(a):
        return a.reshape(4, 2 * a.shape[2], a.shape[3])

    def joined(name, a):
        return _w_in_rows(unhalve(a)) if name == "w_in" else _join(name, unhalve(a))

    wb = {n: halves(w[n].astype(MXU)) for n in MATS[1:]}
    wb["w_in"] = halves(jnp.pad(jnp.transpose(wt_in.astype(MXU), (1, 0, 2)), ((0, 0), (0, W_IN_PAD - W_IN_SHARD), (0, 0))))
    xs = x[0]
    h0, first = _rms_fwd(xs, norm_mix[0], "rms_mix", _Comm(gathers=[(wb["w_in"], 0, True), (w["ssd_conv_w"], None, False),
                                                                    (w["lru_conv_w"], None, False)]))
    convs = {"ssd_conv_w": _join("ssd_conv_w", first[1]), "lru_conv_w": _join("lru_conv_w", first[2])}
    behind_att = [(n, 0) for n in MATS[1:]]
    behind_ffn = [[("w_in", 1)], [("w_out", 1), ("w_gate", 1), ("w_up", 1)], [("w_down", 1)]]
    whole = {("w_in", 0): joined("w_in", first[0])}
    params = {}

    def layer_params(l):
        if l not in params:
            p = {n: w[n][l] for n in SMALL if n != "norm_final"}
            p.update(w_in_t=whole["w_in", l], ssd_conv_w=convs["ssd_conv_w"][l], lru_conv_w=convs["lru_conv_w"][l])
            params[l] = p
        if "w_out" not in params[l] and ("w_out", l) in whole:
            params[l].update(w_out=whole["w_out", l], w_down=whole["w_down", l],
                             w_gu_t=jnp.concatenate([whole["w_gate", l], whole["w_up", l]], axis=0))
        return params[l]

    saved = []
    h_in = h0
    for l in range(DEPTH):
        first_layer = l == 0
        mix, sv, got = _layer_mixers(xs, layer_params(l), _Comm(gathers=[(wb[n], k, True) for n, k in behind_att]) if first_layer else None,
                                     h_in)
        whole.update({k: joined(k[0], a) for k, a in zip(behind_att, got)})
        comms = [_Comm(gathers=[(wb[n], k, True) for n, k in part]) if first_layer else None for part in behind_ffn]
        xs, got, h_in = _layer_ffn(xs, mix, layer_params(l), sv, comms, norm_mix[l + 1] if l + 1 < DEPTH else None)
        whole.update({k: joined(k[0], a) for k, a in zip([k for part in behind_ffn for k in part], got)})
        saved.append(sv)
    dx, gnf, lsum = _loss_head(xs, norm_final, loss_target[0], "loss_head")
    loss = lax.psum(jnp.sum(lsum), ("x", "y", "c"))

    dx, g1, _, _, _ = _layer_bwd(dx, layer_params(1), saved[1])
    def slabs_of(g, names):
        own = _grad_slabs(g, names)
        sent = _grad_slabs(g, [n for n in names if n in MATS], "@wire")
        sent.update({n: own[n] for n in names if n not in MATS})
        return own, sent

    s1, sent1 = slabs_of(g1, BIG)
    att0 = ("w_gate", "w_up", "w_down", "w_out")
    s0, sent0 = {}, {}

    def add0(g0, names):
        own, sent = slabs_of(g0, names)
        s0.update(own)
        sent0.update(sent)

    ssd1 = ("w_gate", "w_up")
    att1 = tuple(n for n in BIG if n not in ssd1)

    def comm_att(g0):
        add0(g0, att0)
        return _Comm(scatters=[sent1[n] for n in att1] + [sent0[n] for n in att0])

    tail0 = ("w_in",) + CONVS

    def comm_tail(g0):
        add0(g0, tail0)
        return _Comm(scatters=[sent0[n] for n in tail0])

    dx, g0, got_ssd, got_att, got_tail = _layer_bwd(dx, layer_params(0), saved[0], _Comm(scatters=[sent1[n] for n in ssd1]),
                                                    comm_att, comm_tail)
    recv = {(n, 1): a for n, a in zip(ssd1, got_ssd)}
    recv.update({(n, 1): a for n, a in zip(att1, got_att[:len(att1)])})
    recv.update({(n, 0): a for n, a in zip(att0, got_att[len(att1):])})
    recv.update({(n, 0): a for n, a in zip(tail0, got_tail)})

    me = 2 * lax.axis_index("x") + lax.axis_index("y")
    slabs = (s0, s1)
    part = {}
    for n in BIG:
        per_layer = []
        for l in range(DEPTH):
            own = lax.dynamic_index_in_dim(slabs[l][n], me, axis=0, keepdims=False)
            per_layer.append(_sum_slots(own, recv[n, l], "sum_chips_" + n, ROW_TILE.get(n, own.shape[0])))
        part[n] = jnp.stack(per_layer, axis=0)
    sib = dict(zip(BIG, _swap_sibling([part[n] for n in BIG])))
    out_g, out_d, out_m, out_v = {}, {}, {}, {}
    for n in BIG:
        if n == "w_in":
            res = _adamw(wt_in, mt_in, vt_in, jnp.transpose(part[n], (1, 0, 2)), jnp.transpose(sib[n], (1, 0, 2)), "adamw_" + n,
                         W_IN_ADAM_TILE, rows_first=True)
            out_g[n], out_d[n], out_m[n], out_v[n] = [jnp.transpose(t, (1, 2, 0)) for t in res]
            continue
        res = _adamw(w[n], m[n], v[n], part[n], sib[n], "adamw_" + n, ROW_TILE.get(n, w[n].shape[1]))
        out_g[n], out_d[n], out_m[n], out_v[n] = [jnp.transpose(t, (0, 2, 1)) for t in res] if n in TRANSPOSED else res

    gsm = {n: jnp.stack([g0[n], g1[n]], axis=0) for n in SMALL if n != "norm_final"}
    gsm["norm_final"] = jnp.sum(gnf, axis=0)
    small_shapes = [w[n].shape for n in SMALL]
    gs = _pack([gsm[n].reshape(w[n].shape) for n in SMALL], BLK, 8, F32)
    gall = _gather_small(gs)
    gsum = _sum_slots(None, gall, "sum_devices", gs.shape[0])
    ws = _pack([w[n] for n in SMALL], BLK, 8, F32)
    ms = _pack([m[n] for n in SMALL], BLK, 8, F32)
    vs = _pack([v[n] for n in SMALL], BLK, 8, F32)
    gsr, dsr, nms, nvs = _adamw(ws, ms, vs, gsum, None, "adamw_small", gs.shape[0])
    out_g.update(zip(SMALL, _unpack(gsr, small_shapes)))
    out_d.update(zip(SMALL, _unpack(dsr, small_shapes)))
    out_m.update(zip(SMALL, _unpack(nms, small_shapes)))
    out_v.update(zip(SMALL, _unpack(nvs, small_shapes)))

    return (loss, dx[None], *[out_g[n] for n in WEIGHTS], *[out_d[n] for n in WEIGHTS],
            *[out_m[n] for n in WEIGHTS], *[out_v[n] for n in WEIGHTS])
```

```python
import functools
import math

import jax
import jax.numpy as jnp
import numpy as np
from jax import lax
from jax.experimental import pallas as pl
from jax.experimental.pallas import tpu as pltpu

F32 = jnp.float32
MXU = jnp.bfloat16
HI = lax.Precision.HIGHEST
HIGH = lax.Precision.HIGH
MESH = pl.DeviceIdType.MESH

D_MODEL = 1024
DEPTH = 2
HEAD_DIM = 64
ATT_W = 512
ATT_PATTERNS = ((128, 1), (512, 4), (2048, 16))
BLK = 128
SSD_W = 512
SSD_STATE = 128
LRU_W = 512
LRU_BLOCKS = 8
LRU_C = 8.0
CONV_K = 4
D_MIX = 1536
D_FF = 2816
IN_COLS = 4104
NP = 4224
NORM_EPS = 1e-6
SSD_NORM_EPS = 1e-5
LN2 = math.log(2.0)
NEG = -1e30

ADAM_LR, ADAM_B1, ADAM_B2, ADAM_EPS, ADAM_WD, ADAM_STEP = 0.001, 0.9, 0.999, 1e-08, 0.01, 10
BC1 = 1.0 - ADAM_B1 ** ADAM_STEP
BC2 = 1.0 - ADAM_B2 ** ADAM_STEP

VMEM_LIMIT = 56 * 1024 * 1024

C_Q, C_K, C_V, C_Z, C_XBC, C_G, C_XL, C_DT = 0, 512, 1024, 1536, 2048, 3072, 3584, 4096


def _cp(*sem):
    return pltpu.CompilerParams(dimension_semantics=sem, vmem_limit_bytes=VMEM_LIMIT)


def _dot(a, b, dims, prec=None):
    return lax.dot_general(a, b, (dims, ((), ())), preferred_element_type=F32, precision=prec)


def _nn(a, b, prec=None):
    return _dot(a, b, ((1,), (0,)), prec)


def _nt(a, b, prec=None):
    return _dot(a, b, ((1,), (1,)), prec)


def _tn(a, b, prec=None):
    return _dot(a, b, ((0,), (0,)), prec)


def _sigmoid(x):
    return jax.nn.sigmoid(x)


def _silu(x):
    return x * _sigmoid(x)


def _softplus(x):
    return jnp.maximum(x, 0.0) + jnp.log(1.0 + jnp.exp(-jnp.abs(x)))


def _gelu(x):
    return 0.5 * x * (1.0 + jnp.tanh(0.7978845608028654 * (x + 0.044715 * x * x * x)))


def _mm(a, b, *, ta=False, tb=False, add=None, out_dtype=F32, tm, tn, tk, name, comm=None, epi=None, a_pro=None):
    m, k = (a.shape[1], a.shape[0]) if ta else a.shape
    n = b.shape[0] if tb else b.shape[1]
    if a_pro is not None:
        k = b.shape[1] if tb else b.shape[0]
        assert not ta and tk == k and tn == n
    assert (b.shape[1] if tb else b.shape[0]) == k
    assert m % tm == 0 and n % tn == 0 and k % tk == 0, (name, m, n, k)
    nk = k // tk
    a_spec = pl.BlockSpec((tk, tm), lambda i, j, kk: (kk, i)) if ta else pl.BlockSpec((tm, tk), lambda i, j, kk: (i, kk))
    if a_pro is not None:
        a_spec = pl.BlockSpec((tm, a.shape[1]), lambda i, j, kk: (i, 0))
    b_spec = pl.BlockSpec((tn, tk), lambda i, j, kk: (j, kk)) if tb else pl.BlockSpec((tk, tn), lambda i, j, kk: (kk, j))
    o_spec = pl.BlockSpec((tm, tn), lambda i, j, kk: (i, j))
    dims = ((0 if ta else 1,), (1 if tb else 0,))
    carried = comm is not None
    comm = comm or _Comm()
    ni, nj = m // tm, n // tn
    efn, erows, econsts, eouts, eaccs = epi or (None, [], [], [], [])
    assert epi is None or nj == 1
    nadd = 0 if add is None else 1
    ner, nec, neo, nea = len(erows), len(econsts), len(eouts), len(eaccs)
    npa = 0 if a_pro is None else 1

    def body(*refs):
        refs, cm = comm.split(refs, 2 + nadd + ner + nec, 1 + neo + nea + npa, 1)
        a_ref, b_ref = refs[:2]
        er_refs = refs[2 + nadd:2 + nadd + ner]
        ec_refs = refs[2 + nadd + ner:2 + nadd + ner + nec]
        o_ref = refs[2 + nadd + ner + nec]
        eo_refs = refs[3 + nadd + ner + nec:3 + nadd + ner + nec + neo]
        ea_refs = refs[3 + nadd + ner + nec + neo:3 + nadd + ner + nec + neo + nea]
        acc = refs[-1]
        i, j, kk = pl.program_id(0), pl.program_id(1), pl.program_id(2)
        comm.start_at((i == 0) & (j == 0) & (kk == 0), cm)

        @pl.when(kk == 0)
        def _():
            acc[...] = jnp.zeros_like(acc)

        a_val = a_ref[...] if a_pro is None else a_pro(a_ref[...]).astype(MXU)
        acc[...] += _dot(a_val.astype(MXU), b_ref[...].astype(MXU), dims)
        if a_pro is not None:
            refs[3 + nadd + ner + nec + neo + nea][...] = a_val

        @pl.when(kk == nk - 1)
        def _():
            r = acc[...]
            if add is not None:
                r = r + refs[2][...]
            if efn is None:
                o_ref[...] = r.astype(out_dtype)
            else:
                main, extra, sums = efn(r, [t[...] for t in er_refs], [t[...] for t in ec_refs])
                o_ref[...] = main.astype(out_dtype)
                for t, val in zip(eo_refs, extra):
                    t[...] = val.astype(t.dtype)
                @pl.when(i == 0)
                def _():
                    for t, val in zip(ea_refs, sums):
                        t[...] = val

                @pl.when(i > 0)
                def _():
                    for t, val in zip(ea_refs, sums):
                        t[...] += val

        comm.wait_at((i == ni - 1) & (j == nj - 1) & (kk == nk - 1), cm)

    def whole_rows(width):
        return pl.BlockSpec((tm, width), lambda i, j, kk: (i, 0))

    ins = [a, b] + ([] if add is None else [add]) + list(erows) + list(econsts)
    specs = [a_spec, b_spec] + ([] if add is None else [o_spec]) + [whole_rows(t.shape[1]) for t in erows]
    specs += [pl.BlockSpec(t.shape, lambda i, j, kk: (0, 0)) for t in econsts]
    out_specs = [o_spec] + [whole_rows(wd) for wd, _ in eouts] + [pl.BlockSpec((r, wd), lambda i, j, kk: (0, 0)) for r, wd in eaccs]
    out_shape = [jax.ShapeDtypeStruct((m, n), out_dtype)] + [jax.ShapeDtypeStruct((m, wd), dt) for wd, dt in eouts]
    out_shape += [jax.ShapeDtypeStruct((r, wd), F32) for r, wd in eaccs]
    if a_pro is not None:
        out_specs.append(whole_rows(k))
        out_shape.append(jax.ShapeDtypeStruct((m, k), MXU))
    serial = comm.n or nea
    res = pl.pallas_call(
        body, name=name, grid=(ni, nj, nk), in_specs=specs + [ANY] * comm.n, out_specs=out_specs + [ANY] * comm.n,
        out_shape=out_shape + comm.out_shape(),
        scratch_shapes=[pltpu.VMEM((tm, tn), F32)] + comm.scratch(),
        compiler_params=_cp(*((["arbitrary"] * 3) if serial else ["parallel", "parallel", "arbitrary"])),
    )(*ins, *comm.args())
    nown = 1 + neo + nea + npa
    own = res[0] if nown == 1 else list(res[:nown])
    return (own, list(res[nown:])) if carried else own


def _rows(fn, rows, consts=(), outs=(), accs=(), *, tile, name, halos=(), comm=None):
    rows = [r if isinstance(r, tuple) else (r, r.shape[1], 0) for r in rows]
    s = rows[0][0].shape[0]
    assert s % tile == 0 and tile % 8 == 0
    n = s // tile
    t8 = tile // 8
    nr, nh, nc_, no, na = len(rows), len(halos), len(consts), len(outs), len(accs)
    carried = comm is not None
    comm = comm or _Comm()

    def body(*refs):
        refs, cm = comm.split(refs, nr + nh + nc_, no + na, 0)
        i = pl.program_id(0)
        comm.start_at(i == 0, cm)
        rv = [r[...] for r in refs[:nr]]
        hv = []
        for (idx, kind), r in zip(halos, refs[nr:nr + nh]):
            edge = (i == 0) if kind == "prev" else (i == n - 1)
            hv.append(jnp.where(edge, 0.0, r[...]))
        cv = [r[...] for r in refs[nr + nh:nr + nh + nc_]]
        o_refs = refs[nr + nh + nc_:nr + nh + nc_ + no]
        a_refs = refs[nr + nh + nc_ + no:]
        ov, av = fn(rv, hv, cv)
        for r, v in zip(o_refs, ov):
            r[...] = v.astype(r.dtype)
        if na:
            @pl.when(i == 0)
            def _():
                for r in a_refs:
                    r[...] = jnp.zeros_like(r)
            for r, v in zip(a_refs, av):
                r[...] += v
        comm.wait_at(i == n - 1, cm)

    in_specs = [pl.BlockSpec((tile, w), functools.partial(lambda i, cb: (i, cb), cb=cb)) for (_, w, cb) in rows]
    for idx, kind in halos:
        _, w, cb = rows[idx]
        if kind == "prev":
            in_specs.append(pl.BlockSpec((8, w), functools.partial(lambda i, cb: (jnp.maximum(i * t8 - 1, 0), cb), cb=cb)))
        else:
            in_specs.append(pl.BlockSpec((8, w), functools.partial(lambda i, cb: (jnp.minimum((i + 1) * t8, n * t8 - 1), cb), cb=cb)))
    in_specs += [pl.BlockSpec(c.shape, functools.partial(lambda i, nd: (0,) * nd, nd=c.ndim)) for c in consts]
    out_specs = [pl.BlockSpec((tile, c), lambda i: (i, 0)) for (c, _) in outs]
    out_specs += [pl.BlockSpec((r, c), lambda i: (0, 0)) for (r, c) in accs]
    out_shape = [jax.ShapeDtypeStruct((s, c), dt) for (c, dt) in outs]
    out_shape += [jax.ShapeDtypeStruct((r, c), F32) for (r, c) in accs]
    args = [r[0] for r in rows] + [rows[idx][0] for idx, _ in halos] + list(consts)
    res = pl.pallas_call(
        body, name=name, grid=(n,), in_specs=in_specs + [ANY] * comm.n, out_specs=out_specs + [ANY] * comm.n,
        out_shape=out_shape + comm.out_shape(), scratch_shapes=comm.scratch(), compiler_params=_cp("arbitrary"),
    )(*args, *comm.args())
    return (list(res[:no + na]), list(res[no + na:])) if carried else list(res)


def _colsum8(v):
    t, c = v.shape
    return jnp.sum(v.reshape(t // 8, 8, c), axis=0)


def _rms(x, g):
    return x * lax.rsqrt(jnp.mean(x * x, axis=-1, keepdims=True) + NORM_EPS) * g


def _epi_rms(g):
    return (lambda r, rows, consts: (r, [_rms(r, consts[0])], []), [], [g.reshape(1, -1)], [(g.shape[-1], MXU)], [])


def _epi_rms_bwd(x, g, dres):
    def fn(r, rows, consts):
        xb, drb = rows
        _, vjp = jax.vjp(_rms, xb, consts[0])
        rstd = lax.rsqrt(jnp.mean(xb * xb, axis=-1, keepdims=True) + NORM_EPS)
        return drb + vjp(r)[0], [], [_colsum8(r * xb * rstd)]
    return (fn, [x, dres], [g.reshape(1, -1)], [], [(8, g.shape[-1])])


def _epi_att_stats(att, lse):
    def fn(r, rows, consts):
        hr = lax.broadcasted_iota(jnp.int32, (ATT_W, ATT_W), 0) // HEAD_DIM
        hc = lax.broadcasted_iota(jnp.int32, (ATT_W, ATT_W), 1) // HEAD_DIM
        delta = _nn(r[:, :ATT_W] * rows[0], (hr == hc).astype(F32), HIGH)
        lane = lax.broadcasted_iota(jnp.int32, delta.shape, 1)
        return r, [jnp.where(lane % HEAD_DIM < HEAD_DIM // 2, rows[1], delta)], []
    return (fn, [att, lse], [], [(ATT_W, F32)], [])


def _epi_wire(width):
    return (lambda r, rows, consts: (r, [r], []), [], [], [(width, MXU)], [])


def _rms_fwd(x, g, name, comm=None):
    def fn(rv, hv, cv):
        return [_rms(rv[0], cv[0])], []
    res = _rows(fn, [x], [g.reshape(1, -1)], [(x.shape[1], MXU)], tile=512, name=name, comm=comm)
    return res[0] if comm is None else (res[0][0], res[1])


def _slope_dist(hp, hh, dist, dil):
    hf = (2 * hp + hh + 1).astype(F32)
    slope = jnp.exp(jnp.zeros(dist.shape, F32) - hf * LN2)
    return slope * (dist.astype(F32) * float(dil))


ATT_G = 2048


def _att_fwd_fused(proj, name, comm=None):
    s, npc = proj.shape
    gsz = ATT_G
    ng = s // gsz
    assert s % gsz == 0
    scale = HEAD_DIM ** -0.5
    comm = comm or _Comm()

    def body(*refs):
        (q_ref, kp_ref, kc_ref, vp_ref, vc_ref, att_ref, lse_ref, attb_ref, nn, mn, dn), cm = comm.split(refs, 5, 3, 3)
        hp, g = pl.program_id(0), pl.program_id(1)
        comm.start_at((hp == 0) & (g == 0), cm)
        lane = lax.broadcasted_iota(jnp.int32, (BLK, BLK), 1)
        qi = lax.broadcasted_iota(jnp.int32, (BLK, 2 * BLK), 0)
        ki = lax.broadcasted_iota(jnp.int32, (BLK, 2 * BLK), 1)
        dist = BLK + qi - ki
        band = (dist >= 0) & (dist <= BLK)
        for pi, (_, dil) in enumerate(ATT_PATTERNS):
            nbg = gsz // dil // BLK
            bias = [_slope_dist(hp, hh, dist, dil) for hh in (0, 1)]
            for r in range(dil):
                for b in range(nbg):
                    def rows(blk):
                        return pl.ds(blk * BLK * dil + r, BLK, stride=dil) if dil > 1 else pl.ds(blk * BLK, BLK)
                    q = q_ref[rows(b), :]
                    k_prev = kp_ref[rows(nbg - 1), :] if b == 0 else kc_ref[rows(b - 1), :]
                    v_prev = vp_ref[rows(nbg - 1), :] if b == 0 else vc_ref[rows(b - 1), :]
                    kk = jnp.concatenate([k_prev, kc_ref[rows(b), :]], axis=0).astype(MXU)
                    vv = jnp.concatenate([v_prev, vc_ref[rows(b), :]], axis=0).astype(MXU)
                    valid = (band & ((g > 0) | (ki >= BLK))) if b == 0 else band
                    num = jnp.zeros((BLK, BLK), F32)
                    mx = jnp.zeros((BLK, BLK), F32)
                    den = jnp.zeros((BLK, BLK), F32)
                    for hh in (0, 1):
                        hmask = (lane < HEAD_DIM) if hh == 0 else (lane >= HEAD_DIM)
                        qm = jnp.where(hmask, q, 0.0).astype(MXU)
                        sc = jnp.where(valid, _nt(qm, kk) * scale - bias[hh], NEG)
                        m = jnp.max(sc, axis=1, keepdims=True)
                        p = jnp.exp(sc - m)
                        dn_ = jnp.sum(p, axis=1, keepdims=True)
                        o = _nn(p.astype(MXU), vv)
                        num = jnp.where(hmask, o, num)
                        mx = jnp.where(hmask, m, mx)
                        den = jnp.where(hmask, dn_, den)
                    nn.at[pi][rows(b), :] = num
                    mn.at[pi][rows(b), :] = mx
                    dn.at[pi][rows(b), :] = den

        def merge(c, carry):
            rows = pl.ds(pl.multiple_of(c * 256, 256), 256)
            ms = [mn[pi, rows, :] for pi in range(len(ATT_PATTERNS))]
            m_all = functools.reduce(jnp.maximum, ms)
            num = jnp.zeros((256, BLK), F32)
            den = jnp.zeros((256, BLK), F32)
            for pi in range(len(ATT_PATTERNS)):
                e = jnp.exp(ms[pi] - m_all)
                num = num + nn[pi, rows, :] * e
                den = den + dn[pi, rows, :] * e
            att = num / den
            att_ref[rows, :] = att
            attb_ref[rows, :] = att.astype(MXU)
            lse_ref[rows, :] = m_all + jnp.log(den)
            return carry

        lax.fori_loop(0, gsz // 256, merge, 0)
        comm.wait_at((hp == 3) & (g == ng - 1), cm)

    def cur(base):
        return pl.BlockSpec((gsz, BLK), lambda hp, g: (g, base // BLK + hp))

    def prev(base):
        return pl.BlockSpec((gsz, BLK), lambda hp, g: (jnp.maximum(g - 1, 0), base // BLK + hp))

    o_spec = pl.BlockSpec((gsz, BLK), lambda hp, g: (g, hp))
    npat = len(ATT_PATTERNS)
    res = pl.pallas_call(
        body, name=name, grid=(4, ng),
        in_specs=[cur(C_Q), prev(C_K), cur(C_K), prev(C_V), cur(C_V)] + [ANY] * comm.n,
        out_specs=[o_spec] * 3 + [ANY] * comm.n,
        out_shape=[jax.ShapeDtypeStruct((s, ATT_W), F32)] * 2 + [jax.ShapeDtypeStruct((s, ATT_W), MXU)] + comm.out_shape(),
        scratch_shapes=[pltpu.VMEM((npat, gsz, BLK), F32)] * 3 + comm.scratch(),
        compiler_params=_cp("arbitrary", "arbitrary"),
    )(proj, proj, proj, proj, proj, *comm.args())
    return res[0], res[1], res[2], list(res[3:])


def _att_bwd_rev(proj, datt, stats, name, comm=None):
    s, npc = proj.shape
    gsz = ATT_G
    ng = s // gsz
    npat = len(ATT_PATTERNS)
    scale = HEAD_DIM ** -0.5
    comm = comm or _Comm()

    def body(*refs):
        (q_ref, kp_ref, kc_ref, vp_ref, vc_ref, do_ref, st_ref, dq_out, dk_out, dv_out,
         kcar, vcar, dq_ref, dk_ref, dv_ref), cm = comm.split(refs, 7, 3, 5)
        hp, gi = pl.program_id(0), pl.program_id(1)
        g = ng - 1 - gi
        comm.start_at((hp == 0) & (gi == 0), cm)

        @pl.when(gi == 0)
        def _():
            kcar[...] = jnp.zeros_like(kcar)
            vcar[...] = jnp.zeros_like(vcar)

        lane = lax.broadcasted_iota(jnp.int32, (BLK, BLK), 1)
        qi = lax.broadcasted_iota(jnp.int32, (BLK, 2 * BLK), 0)
        ki = lax.broadcasted_iota(jnp.int32, (BLK, 2 * BLK), 1)
        dist = BLK + qi - ki
        band = (dist >= 0) & (dist <= BLK)
        for acc in (dq_ref, dk_ref, dv_ref):
            acc[...] = jnp.zeros_like(acc)
        for pi, (_, dil) in enumerate(ATT_PATTERNS):
            nbg = gsz // dil // BLK
            bias = [_slope_dist(hp, hh, dist, dil) for hh in (0, 1)]
            for r in range(dil):
                edge = slice(pi * gsz + r * BLK, pi * gsz + (r + 1) * BLK)
                for b in reversed(range(nbg)):
                    def rows(blk):
                        return pl.ds(blk * BLK * dil + r, BLK, stride=dil) if dil > 1 else pl.ds(blk * BLK, BLK)
                    q, do, st = q_ref[rows(b), :], do_ref[rows(b), :], st_ref[rows(b), :]
                    k_prev = kp_ref[rows(nbg - 1), :] if b == 0 else kc_ref[rows(b - 1), :]
                    v_prev = vp_ref[rows(nbg - 1), :] if b == 0 else vc_ref[rows(b - 1), :]
                    kk = jnp.concatenate([k_prev, kc_ref[rows(b), :]], axis=0).astype(MXU)
                    vv = jnp.concatenate([v_prev, vc_ref[rows(b), :]], axis=0).astype(MXU)
                    valid = (band & ((g > 0) | (ki >= BLK))) if b == 0 else band
                    dq = jnp.zeros((BLK, BLK), F32)
                    dkk = jnp.zeros((2 * BLK, BLK), F32)
                    dvv = jnp.zeros((2 * BLK, BLK), F32)
                    for hh in (0, 1):
                        c0 = hh * HEAD_DIM
                        hmask = (lane < HEAD_DIM) if hh == 0 else (lane >= HEAD_DIM)
                        qm = jnp.where(hmask, q, 0.0).astype(MXU)
                        dom = jnp.where(hmask, do, 0.0).astype(MXU)
                        sc = _nt(qm, kk) * scale - bias[hh]
                        p = jnp.exp(jnp.where(valid, sc - st[:, c0:c0 + 1], NEG))
                        ds = (p * (_nt(dom, vv) - st[:, c0 + HEAD_DIM // 2:c0 + HEAD_DIM // 2 + 1])).astype(MXU)
                        dq = jnp.where(hmask, _nn(ds, kk), dq)
                        dkk = dkk + _tn(ds, qm)
                        dvv = dvv + _tn(p.astype(MXU), dom)
                    dq_ref[rows(b), :] += dq * scale
                    own_k, own_v = dkk[BLK:] * scale, dvv[BLK:]
                    if b == nbg - 1:
                        own_k, own_v = own_k + kcar[edge, :], own_v + vcar[edge, :]
                    dk_ref[rows(b), :] += own_k
                    dv_ref[rows(b), :] += own_v
                    if b > 0:
                        dk_ref[rows(b - 1), :] += dkk[:BLK] * scale
                        dv_ref[rows(b - 1), :] += dvv[:BLK]
                    else:
                        kcar[edge, :] = dkk[:BLK] * scale
                        vcar[edge, :] = dvv[:BLK]
        for out, acc in ((dq_out, dq_ref), (dk_out, dk_ref), (dv_out, dv_ref)):
            out[...] = acc[...].astype(out.dtype)
        comm.wait_at((hp == 3) & (gi == ng - 1), cm)

    def pspec(base, shift):
        return pl.BlockSpec((gsz, BLK), lambda hp, gi: (jnp.maximum(ng - 1 - gi + shift, 0), base // BLK + hp))

    wspec = pl.BlockSpec((gsz, BLK), lambda hp, gi: (ng - 1 - gi, hp))
    in_specs = [pspec(C_Q, 0), pspec(C_K, -1), pspec(C_K, 0), pspec(C_V, -1), pspec(C_V, 0), wspec, wspec] + [ANY] * comm.n
    res = pl.pallas_call(
        body, name=name, grid=(4, ng), in_specs=in_specs,
        out_specs=[wspec] * 3 + [ANY] * comm.n,
        out_shape=[jax.ShapeDtypeStruct((s, ATT_W), MXU)] * 3 + comm.out_shape(),
        scratch_shapes=[pltpu.VMEM((npat * gsz, BLK), F32)] * 2 + [pltpu.VMEM((gsz, BLK), F32)] * 3 + comm.scratch(),
        compiler_params=_cp("arbitrary", "arbitrary"),
    )(proj, proj, proj, proj, proj, datt, stats, *comm.args())
    return res[0], res[1], res[2], list(res[3:])


def _shift_down(cur, halo, sft):
    if sft == 0:
        return cur
    t = cur.shape[0]
    rolled = pltpu.roll(cur, sft, 0)
    hr = pltpu.roll(halo, sft, 0)
    row = lax.broadcasted_iota(jnp.int32, cur.shape, 0)
    return jnp.where(row < sft, jnp.tile(hr, (t // 8, 1)), rolled)


def _shift_up(cur, halo, sft):
    if sft == 0:
        return cur
    t = cur.shape[0]
    rolled = pltpu.roll(cur, t - sft, 0)
    hr = pltpu.roll(halo, 8 - sft, 0)
    row = lax.broadcasted_iota(jnp.int32, cur.shape, 0)
    return jnp.where(row >= t - sft, jnp.tile(hr, (t // 8, 1)), rolled)


def _conv(x, xh, w, b):
    y = b + x * w[CONV_K - 1:CONV_K]
    for k in range(CONV_K - 1):
        y = y + _shift_down(x, xh, CONV_K - 1 - k) * w[k:k + 1]
    return y


def _conv_bwd(x, xh, dy, dyh, w):
    dx = dy * w[CONV_K - 1:CONV_K]
    dws = []
    for k in range(CONV_K - 1):
        sft = CONV_K - 1 - k
        dx = dx + _shift_up(dy, dyh, sft) * w[k:k + 1]
        dws.append(jnp.sum(dy * _shift_down(x, xh, sft), axis=0, keepdims=True))
    dws.append(jnp.sum(dy * x, axis=0, keepdims=True))
    c = x.shape[1]
    dw = jnp.concatenate(dws + [jnp.zeros((8 - CONV_K, c), F32)], axis=0)
    return dx, dw, jnp.sum(dy, axis=0, keepdims=True)


def _pad8(w):
    return jnp.concatenate([w, jnp.zeros((8 - w.shape[0], w.shape[1]), w.dtype)], axis=0)


def _ssd_pre(proj, conv_w, conv_b, dt_bias128, name):
    def fn(rv, hv, cv):
        xbc, dtr = rv
        return [_silu(_conv(xbc, hv[0], cv[0], cv[1])), _softplus(dtr + cv[2])], []
    return _rows(fn, [(proj, 1024, C_XBC // 1024), (proj, BLK, C_DT // BLK)],
                 [_pad8(conv_w), conv_b.reshape(1, -1), dt_bias128],
                 [(1024, F32), (BLK, F32)], tile=256, name=name, halos=[(0, "prev")])


def _ssd_pre_bwd(proj, dxc, ddt, conv_w, conv_b, dt_bias128, name):
    def fn(rv, hv, cv):
        xbc, dtr, dxcb, ddtb = rv
        xh, dxch_raw, xnext = hv
        w, b, bias = cv
        pre = _conv(xbc, xh, w, b)
        sg = _sigmoid(pre)
        dpre = dxcb * (sg * (1.0 + pre * (1.0 - sg)))
        t = xbc.shape[0]
        tail = jnp.concatenate([xbc[t - 8:], xnext], axis=0)
        pre_n = _conv(tail[8:], tail[:8], w, b)
        sgn = _sigmoid(pre_n)
        dpre_h = dxch_raw * (sgn * (1.0 + pre_n * (1.0 - sgn)))
        dx, dw, db = _conv_bwd(xbc, xh, dpre, dpre_h, w)
        ddr = ddtb * _sigmoid(dtr + bias)
        return [dx, ddr], [dw, jnp.concatenate([db, jnp.zeros((7, db.shape[1]), F32)], axis=0), _colsum8(ddr)]
    return _rows(fn, [(proj, 1024, C_XBC // 1024), (proj, BLK, C_DT // BLK), dxc, ddt],
                 [_pad8(conv_w), conv_b.reshape(1, -1), dt_bias128],
                 [(1024, MXU), (BLK, MXU)], [(8, 1024), (8, 1024), (8, BLK)], tile=256, name=name,
                 halos=[(0, "prev"), (2, "next"), (0, "next")])


SSD_CPB = 1


def _head_cols(v, h0):
    lane = lax.broadcasted_iota(jnp.int32, (v.shape[0], BLK), 1)
    return jnp.where(lane < HEAD_DIM, v[:, h0:h0 + 1], v[:, h0 + 1:h0 + 2])


def _ssd_scan(xc, dt, par, name):
    s = xc.shape[0]
    nc = s // BLK

    def body(x_ref, dt_ref, par_ref, y_ref, st_ref, h_ref):
        c = pl.program_id(0)

        @pl.when(c == 0)
        def _():
            h_ref[...] = jnp.zeros_like(h_ref)

        st_ref[0] = h_ref[...]
        dt = dt_ref[...]
        a_row = -jnp.exp(par_ref[0:1, :])
        d_row = par_ref[1:2, :]
        ri = lax.broadcasted_iota(jnp.int32, (BLK, BLK), 0)
        ci = lax.broadcasted_iota(jnp.int32, (BLK, BLK), 1)
        tril = ri >= ci
        cs = _nn(tril.astype(F32), dt * a_row, HI)
        cst, dtt = cs.T, dt.T
        last = cs[BLK - 1:BLK, :]
        wcol = jnp.exp(last - cs) * dt
        ecs = jnp.exp(cs)
        elast = jnp.exp(last)
        for g in (0, 1):
            bg = x_ref[:, 512 + g * BLK:512 + (g + 1) * BLK].astype(MXU)
            cg = x_ref[:, 768 + g * BLK:768 + (g + 1) * BLK].astype(MXU)
            gm = _nt(cg, bg)
            for pp in (0, 1):
                pr = 2 * g + pp
                h0 = 2 * pr
                x2 = x_ref[:, pr * BLK:(pr + 1) * BLK]
                hprev = h_ref[pr * BLK:(pr + 1) * BLK, :]
                yp = jnp.zeros((BLK, BLK), F32)
                for hh in (0, 1):
                    h = h0 + hh
                    hmask = (ci < HEAD_DIM) if hh == 0 else (ci >= HEAD_DIM)
                    lm = jnp.exp(jnp.where(tril, cs[:, h:h + 1] - cst[h:h + 1, :], NEG))
                    mm = gm * lm * dtt[h:h + 1, :]
                    yp = yp + _nn(mm.astype(MXU), jnp.where(hmask, x2, 0.0).astype(MXU))
                y0 = _nt(cg, hprev.astype(MXU))
                y_ref[:, pr * BLK:(pr + 1) * BLK] = yp + _head_cols(ecs, h0) * y0 + _head_cols(d_row, h0) * x2
                dec = jnp.where(ri < HEAD_DIM, elast[:, h0:h0 + 1], elast[:, h0 + 1:h0 + 2])
                xw = (x2 * _head_cols(wcol, h0)).astype(MXU)
                h_ref[pr * BLK:(pr + 1) * BLK, :] = dec * hprev + _tn(xw, bg)

    return pl.pallas_call(
        body, name=name, grid=(nc,),
        in_specs=[pl.BlockSpec((BLK, 1024), lambda c: (c, 0)), pl.BlockSpec((BLK, BLK), lambda c: (c, 0)),
                  pl.BlockSpec((8, BLK), lambda c: (0, 0))],
        out_specs=[pl.BlockSpec((BLK, SSD_W), lambda c: (c, 0)), pl.BlockSpec((1, SSD_W, SSD_STATE), lambda c: (c, 0, 0))],
        out_shape=[jax.ShapeDtypeStruct((s, SSD_W), F32), jax.ShapeDtypeStruct((nc, SSD_W, SSD_STATE), F32)],
        scratch_shapes=[pltpu.VMEM((SSD_W, SSD_STATE), F32)],
        compiler_params=_cp("arbitrary"),
    )(xc, dt, par)


def _ssd_scan_bwd(xc, dt, par, st, dy, name, comm=None):
    s = xc.shape[0]
    cpb = SSD_CPB
    nb = s // (cpb * BLK)
    comm = comm or _Comm()

    def chunk(x_ref, dt_ref, par_ref, st_ref, dy_ref, dx_ref, ddt_ref, dal_ref, dd_ref, dh_ref):
        dt = dt_ref[...]
        a_row = -jnp.exp(par_ref[0:1, :])
        d_row = par_ref[1:2, :]
        ri = lax.broadcasted_iota(jnp.int32, (BLK, BLK), 0)
        ci = lax.broadcasted_iota(jnp.int32, (BLK, BLK), 1)
        tril = ri >= ci
        cs = _nn(tril.astype(F32), dt * a_row, HI)
        cst, dtt = cs.T, dt.T
        last = cs[BLK - 1:BLK, :]
        tolast = jnp.exp(last - cs)
        wcol = tolast * dt
        ecs = jnp.exp(cs)
        elast = jnp.exp(last)
        dcs_col = jnp.zeros((BLK, BLK), F32)
        ddt_col = jnp.zeros((BLK, BLK), F32)
        dcs_row = jnp.zeros((BLK, BLK), F32)
        ddt_row = jnp.zeros((BLK, BLK), F32)
        dlast = jnp.zeros((1, BLK), F32)
        ddsk = jnp.zeros((1, BLK), F32)
        for g in (0, 1):
            bg32 = x_ref[:, 512 + g * BLK:512 + (g + 1) * BLK]
            cg32 = x_ref[:, 768 + g * BLK:768 + (g + 1) * BLK]
            bg, cg = bg32.astype(MXU), cg32.astype(MXU)
            gm = _nt(cg, bg)
            dgm = jnp.zeros((BLK, BLK), F32)
            dbg = jnp.zeros((BLK, BLK), F32)
            dcg = jnp.zeros((BLK, BLK), F32)
            for pp in (0, 1):
                pr = 2 * g + pp
                h0 = 2 * pr
                x2 = x_ref[:, pr * BLK:(pr + 1) * BLK]
                dy2 = dy_ref[:, pr * BLK:(pr + 1) * BLK]
                hprev = st_ref[0, pr * BLK:(pr + 1) * BLK, :]
                dhn = dh_ref[pr * BLK:(pr + 1) * BLK, :]
                x2m, dhnm = x2.astype(MXU), dhn.astype(MXU)
                zb = _nt(bg, dhnm)
                y0 = _nt(cg, hprev.astype(MXU))
                esel = _head_cols(ecs, h0)
                wsel = _head_cols(wcol, h0)
                dx2 = _head_cols(d_row, h0) * dy2 + wsel * zb
                pick2 = (((ri < HEAD_DIM) & (ci == h0)) | ((ri >= HEAD_DIM) & (ci == h0 + 1))).astype(F32)
                sums = _nn(jnp.concatenate([dy2 * y0, x2 * zb, dy2 * x2], axis=0), pick2, HIGH)
                de2, dw2, dd2 = sums[:BLK], sums[BLK:2 * BLK], sums[2 * BLK:]
                v2 = dw2 * wcol
                dcs_col = dcs_col + ecs * de2 - v2
                ddt_col = ddt_col + dw2 * tolast
                hsum = _nn(dhn * hprev, jnp.ones((BLK, BLK), F32), HIGH)
                dlast = dlast + elast * jnp.sum(jnp.where(pick2 > 0.0, hsum, 0.0), axis=0, keepdims=True) \
                    + jnp.sum(v2, axis=0, keepdims=True)
                ddsk = ddsk + jnp.sum(dd2, axis=0, keepdims=True)
                ts = []
                for hh in (0, 1):
                    h = h0 + hh
                    hmask = (ci < HEAD_DIM) if hh == 0 else (ci >= HEAD_DIM)
                    ons = (ri == h).astype(F32)
                    dym = jnp.where(hmask, dy2, 0.0).astype(MXU)
                    dt_r = dtt[h:h + 1, :]
                    lm = jnp.exp(jnp.where(tril, cs[:, h:h + 1] - cst[h:h + 1, :], NEG))
                    mm = gm * lm * dt_r
                    dx2 = dx2 + _tn(mm.astype(MXU), dym)
                    dm = _nt(dym, x2m)
                    t1 = dm * lm
                    dgm = dgm + t1 * dt_r
                    tt = t1 * gm
                    ddt_row = ddt_row + ons * jnp.sum(tt, axis=0, keepdims=True)
                    t = tt * dt_r
                    dcs_row = dcs_row - ons * jnp.sum(t, axis=0, keepdims=True)
                    ts.append(t)
                rows2 = lax.broadcasted_iota(jnp.int32, (2 * BLK, BLK), 0)
                lane2 = lax.broadcasted_iota(jnp.int32, (2 * BLK, BLK), 1)
                to_lane = ((rows2 < BLK) & (lane2 == h0)) | ((rows2 >= BLK) & (lane2 == h0 + 1))
                dcs_col = dcs_col + _nn(jnp.concatenate(ts, axis=1), to_lane.astype(F32), HIGH)
                dx_ref[:, pr * BLK:(pr + 1) * BLK] = dx2
                edy = (esel * dy2).astype(MXU)
                dcg = dcg + _nn(edy, hprev.astype(MXU))
                dec = jnp.where(ri < HEAD_DIM, elast[:, h0:h0 + 1], elast[:, h0 + 1:h0 + 2])
                dh_ref[pr * BLK:(pr + 1) * BLK, :] = dec * dhn + _tn(edy, cg)
                dbg = dbg + _nn((x2 * wsel).astype(MXU), dhnm)
            dgmm = dgm.astype(MXU)
            dx_ref[:, 512 + g * BLK:512 + (g + 1) * BLK] = dbg + _tn(dgmm, cg)
            dx_ref[:, 768 + g * BLK:768 + (g + 1) * BLK] = dcg + _nn(dgmm, bg)
        dcs = dcs_col + dcs_row.T + jnp.where(ri == BLK - 1, dlast, 0.0)
        dda = _nn((ri <= ci).astype(F32), dcs, HI)
        ddt_ref[...] = ddt_col + ddt_row.T + a_row * dda
        da = jnp.sum(dt * dda, axis=0, keepdims=True)
        dal_ref[0:1, :] += da * a_row
        dd_ref[0:1, :] += ddsk

    def body(*refs):
        (x_ref, dt_ref, par_ref, st_ref, dy_ref, dx_ref, ddt_ref, dal_ref, dd_ref, dh_ref), cm = comm.split(refs, 5, 4, 1)
        c = pl.program_id(0)
        comm.start_at(c == 0, cm)

        @pl.when(c == 0)
        def _():
            dh_ref[...] = jnp.zeros_like(dh_ref)
            dal_ref[...] = jnp.zeros_like(dal_ref)
            dd_ref[...] = jnp.zeros_like(dd_ref)

        for cc in reversed(range(cpb)):
            rows = pl.ds(cc * BLK, BLK)
            chunk(x_ref.at[rows], dt_ref.at[rows], par_ref, st_ref.at[pl.ds(cc, 1)], dy_ref.at[rows], dx_ref.at[rows],
                  ddt_ref.at[rows], dal_ref, dd_ref, dh_ref)
        comm.wait_at(c == nb - 1, cm)

    rev = lambda c: (nb - 1 - c, 0)
    tb = cpb * BLK
    res = pl.pallas_call(
        body, name=name, grid=(nb,),
        in_specs=[pl.BlockSpec((tb, 1024), rev), pl.BlockSpec((tb, BLK), rev), pl.BlockSpec((8, BLK), lambda c: (0, 0)),
                  pl.BlockSpec((cpb, SSD_W, SSD_STATE), lambda c: (nb - 1 - c, 0, 0)), pl.BlockSpec((tb, SSD_W), rev)]
        + [ANY] * comm.n,
        out_specs=[pl.BlockSpec((tb, 1024), rev), pl.BlockSpec((tb, BLK), rev),
                   pl.BlockSpec((8, BLK), lambda c: (0, 0)), pl.BlockSpec((8, BLK), lambda c: (0, 0))] + [ANY] * comm.n,
        out_shape=[jax.ShapeDtypeStruct((s, 1024), F32), jax.ShapeDtypeStruct((s, BLK), F32),
                   jax.ShapeDtypeStruct((8, BLK), F32), jax.ShapeDtypeStruct((8, BLK), F32)] + comm.out_shape(),
        scratch_shapes=[pltpu.VMEM((SSD_W, SSD_STATE), F32)] + comm.scratch(),
        compiler_params=_cp("arbitrary"),
    )(xc, dt, par, st, dy, *comm.args())
    return res[0], res[1], res[2], res[3], list(res[4:])


def _ssd_gate(y, z, w):
    t = y * _silu(z)
    outs = []
    for g in (0, 1):
        tg = t[:, g * 256:(g + 1) * 256]
        outs.append(tg * lax.rsqrt(jnp.mean(tg * tg, axis=-1, keepdims=True) + SSD_NORM_EPS))
    return jnp.concatenate(outs, axis=1) * w


def _ssd_post(y, proj, norm_w, name):
    def fn(rv, hv, cv):
        return [_ssd_gate(rv[0], rv[1], cv[0])], []
    return _rows(fn, [y, (proj, SSD_W, C_Z // SSD_W)], [norm_w.reshape(1, -1)], [(SSD_W, MXU)], tile=512, name=name)[0]


def _ssd_post_bwd(y, proj, norm_w, dout, name):
    def fn(rv, hv, cv):
        yb, zb, db = rv
        _, vjp = jax.vjp(lambda a, b: _ssd_gate(a, b, cv[0]), yb, zb)
        dy, dz = vjp(db)
        t = yb * _silu(zb)
        nrm = []
        for g in (0, 1):
            tg = t[:, g * 256:(g + 1) * 256]
            nrm.append(tg * lax.rsqrt(jnp.mean(tg * tg, axis=-1, keepdims=True) + SSD_NORM_EPS))
        return [dy, dz], [_colsum8(db * jnp.concatenate(nrm, axis=1))]
    return _rows(fn, [y, (proj, SSD_W, C_Z // SSD_W), dout], [norm_w.reshape(1, -1)],
                 [(SSD_W, F32), (SSD_W, MXU)], [(8, SSD_W)], tile=512, name=name)


LRU_T = 256


def _lru_conv(proj, conv_w, conv_b, name):
    def fn(rv, hv, cv):
        return [_conv(rv[0], hv[0], cv[0], cv[1])], []
    return _rows(fn, [(proj, LRU_W, C_XL // LRU_W)], [_pad8(conv_w), conv_b.reshape(1, -1)], [(LRU_W, F32)],
                 tile=512, name=name, halos=[(0, "prev")])[0]


def _lru_conv_bwd(proj, dxc, conv_w, name):
    def fn(rv, hv, cv):
        dx, dw, db = _conv_bwd(rv[0], hv[0], rv[1], hv[1], cv[0])
        return [dx], [dw, jnp.concatenate([db, jnp.zeros((7, db.shape[1]), F32)], axis=0)]
    return _rows(fn, [(proj, LRU_W, C_XL // LRU_W), dxc], [_pad8(conv_w)], [(LRU_W, MXU)], [(8, LRU_W), (8, LRU_W)],
                 tile=512, name=name, halos=[(0, "prev"), (1, "next")])


def _lru_au(pre_a, pre_x, xc, ba, bx, lam):
    r = _sigmoid(pre_a + ba)
    i = _sigmoid(pre_x + bx)
    log_a = -LRU_C * r * _softplus(-lam)
    a = jnp.exp(log_a)
    u = jnp.sqrt(1.0 - jnp.exp(2.0 * log_a)) * (i * xc)
    return a, u


def _lru_scan(pre, xc, proj, par, name):
    s = xc.shape[0]
    t = LRU_T

    def body(pre_ref, xc_ref, g_ref, par_ref, out_ref, h_ref, carry):
        c = pl.program_id(0)

        @pl.when(c == 0)
        def _():
            carry[...] = jnp.zeros_like(carry)

        a, u = _lru_au(pre_ref[:, :LRU_W], pre_ref[:, LRU_W:], xc_ref[...], par_ref[0:1, :], par_ref[1:2, :], par_ref[2:3, :])
        row = lax.broadcasted_iota(jnp.int32, (t, LRU_W), 0)
        sft = 1
        while sft < t:
            keep = row >= sft
            a_s = jnp.where(keep, pltpu.roll(a, sft, 0), 1.0)
            u_s = jnp.where(keep, pltpu.roll(u, sft, 0), 0.0)
            u = a * u_s + u
            a = a * a_s
            sft *= 2
        h = a * carry[0:1, :] + u
        h_ref[...] = h
        out_ref[...] = (h * _gelu(g_ref[...])).astype(out_ref.dtype)
        carry[0:1, :] = h[t - 1:t, :]

    return pl.pallas_call(
        body, name=name, grid=(s // t,),
        in_specs=[pl.BlockSpec((t, 2 * LRU_W), lambda c: (c, 0)), pl.BlockSpec((t, LRU_W), lambda c: (c, 0)),
                  pl.BlockSpec((t, LRU_W), lambda c: (c, C_G // LRU_W)), pl.BlockSpec((8, LRU_W), lambda c: (0, 0))],
        out_specs=[pl.BlockSpec((t, LRU_W), lambda c: (c, 0))] * 2,
        out_shape=[jax.ShapeDtypeStruct((s, LRU_W), MXU), jax.ShapeDtypeStruct((s, LRU_W), F32)],
        scratch_shapes=[pltpu.VMEM((8, LRU_W), F32)],
        compiler_params=_cp("arbitrary"),
    )(pre, xc, proj, par)


def _lru_scan_bwd(pre, xc, proj, par, h, dout, name):
    s = xc.shape[0]
    t = LRU_T
    n = s // t
    t8 = t // 8

    def body(pre_ref, xc_ref, g_ref, par_ref, h_ref, hh_ref, do_ref, dpre_ref, dxc_ref, dg_ref, dpar_ref, carry):
        c = pl.program_id(0)

        @pl.when(c == 0)
        def _():
            carry[...] = jnp.zeros_like(carry)
            dpar_ref[...] = jnp.zeros_like(dpar_ref)

        pa, px, xcb = pre_ref[:, :LRU_W], pre_ref[:, LRU_W:], xc_ref[...]
        ba, bx, lam = par_ref[0:1, :], par_ref[1:2, :], par_ref[2:3, :]
        (a, u), vjp = jax.vjp(_lru_au, pa, px, xcb, ba, bx, lam)
        g = g_ref[...]
        hcur = h_ref[...]
        do = do_ref[...]
        _, gvjp = jax.vjp(_gelu, g)
        dg_ref[...] = gvjp(do * hcur)[0].astype(dg_ref.dtype)
        row = lax.broadcasted_iota(jnp.int32, (t, LRU_W), 0)
        v = do * _gelu(g) + jnp.where(row == t - 1, carry[0:1, :], 0.0)
        b = jnp.where(row == t - 1, 0.0, pltpu.roll(a, t - 1, 0))
        sft = 1
        while sft < t:
            keep = row < t - sft
            b_s = jnp.where(keep, pltpu.roll(b, t - sft, 0), 1.0)
            v_s = jnp.where(keep, pltpu.roll(v, t - sft, 0), 0.0)
            v = b * v_s + v
            b = b * b_s
            sft *= 2
        dh = v
        carry[0:1, :] = a[0:1, :] * dh[0:1, :]
        hhalo = jnp.where(c == n - 1, 0.0, hh_ref[...])
        hprev = _shift_down(hcur, hhalo, 1)
        dpa, dpx, dxc, dba, dbx, dlam = vjp((dh * hprev, dh))
        dpre_ref[:, :LRU_W] = dpa
        dpre_ref[:, LRU_W:] = dpx
        dxc_ref[...] = dxc
        dpar_ref[0:1, :] += dba
        dpar_ref[1:2, :] += dbx
        dpar_ref[2:3, :] += dlam

    rev = lambda c: (n - 1 - c, 0)
    return pl.pallas_call(
        body, name=name, grid=(n,),
        in_specs=[pl.BlockSpec((t, 2 * LRU_W), rev), pl.BlockSpec((t, LRU_W), rev),
                  pl.BlockSpec((t, LRU_W), lambda c: (n - 1 - c, C_G // LRU_W)), pl.BlockSpec((8, LRU_W), lambda c: (0, 0)),
                  pl.BlockSpec((t, LRU_W), rev),
                  pl.BlockSpec((8, LRU_W), lambda c: (jnp.maximum((n - 1 - c) * t8 - 1, 0), 0)),
                  pl.BlockSpec((t, LRU_W), lambda c: (n - 1 - c, dout.shape[1] // LRU_W - 1))],
        out_specs=[pl.BlockSpec((t, 2 * LRU_W), rev), pl.BlockSpec((t, LRU_W), rev), pl.BlockSpec((t, LRU_W), rev),
                   pl.BlockSpec((8, LRU_W), lambda c: (0, 0))],
        out_shape=[jax.ShapeDtypeStruct((s, 2 * LRU_W), F32), jax.ShapeDtypeStruct((s, LRU_W), F32),
                   jax.ShapeDtypeStruct((s, LRU_W), MXU), jax.ShapeDtypeStruct((8, LRU_W), F32)],
        scratch_shapes=[pltpu.VMEM((8, LRU_W), F32)],
        compiler_params=_cp("arbitrary"),
    )(pre, xc, proj, par, h, h, dout)


def _swiglu_act(gu, name):
    def fn(rv, hv, cv):
        return [_silu(rv[0].astype(F32)) * rv[1].astype(F32)], []
    return _rows(fn, [(gu, D_FF, 0), (gu, D_FF, 1)], [], [(D_FF, MXU)], tile=256, name=name)[0]


def _epi_swiglu_bwd(gu):
    def fn(r, rows, consts):
        gt, up = rows[0][:, :D_FF].astype(F32), rows[0][:, D_FF:].astype(F32)
        sg = _sigmoid(gt)
        dgate = r * up * (sg * (1.0 + gt * (1.0 - sg)))
        dup = r * (gt * sg)
        return r, [jnp.concatenate([dgate, dup], axis=1)], []
    return (fn, [gu], [], [(2 * D_FF, MXU)], [])


def _loss_head(x, g, target, name):
    d = x.shape[1]

    def fn(rv, hv, cv):
        xb, tb = rv
        y, vjp = jax.vjp(_rms, xb, cv[0])
        err = y - tb
        dy = err * (1.0 / d)
        dx, _ = vjp(dy)
        rstd = lax.rsqrt(jnp.mean(xb * xb, axis=-1, keepdims=True) + NORM_EPS)
        e2 = err * err * (0.5 / d)
        e2 = functools.reduce(lambda a, b: a + b, [e2[:, k * BLK:(k + 1) * BLK] for k in range(d // BLK)])
        return [dx], [_colsum8(dy * xb * rstd), _colsum8(e2)]
    return _rows(fn, [x, target], [g.reshape(1, -1)], [(d, F32)], [(8, d), (8, BLK)], tile=512, name=name)


ANY = pl.BlockSpec(memory_space=pl.ANY)


def _coords():
    return lax.axis_index("x"), lax.axis_index("y"), lax.axis_index("c")


class _Comm:
    def __init__(self, gathers=(), scatters=()):
        self.gathers = list(gathers)
        self.scatters = list(scatters)
        self.n = len(self.gathers) + len(self.scatters)

    def args(self):
        return [g[0] for g in self.gathers] + self.scatters

    def out_shape(self):
        out = [jax.ShapeDtypeStruct((4,) + (a.shape if l is None else a.shape[1:]), a.dtype) for a, l, _ in self.gathers]
        return out + [jax.ShapeDtypeStruct((3,) + a.shape[1:], a.dtype) for a in self.scatters]

    def scratch(self):
        if not self.n:
            return []
        return [pltpu.SemaphoreType.DMA((3 * self.n,)), pltpu.SemaphoreType.DMA((3 * self.n,)),
                pltpu.SemaphoreType.DMA((max(len(self.gathers), 1),)),
                pltpu.SemaphoreType.DMA((3 * self.n,)), pltpu.SemaphoreType.DMA((3 * self.n,))]

    def split(self, refs, n_in, n_out, n_scratch):
        refs = list(refs)
        n = self.n
        own = refs[:n_in] + refs[n_in + n:n_in + n + n_out] + refs[n_in + 2 * n + n_out:n_in + 2 * n + n_out + n_scratch]
        cm = (refs[n_in:n_in + n], refs[n_in + n + n_out:n_in + 2 * n + n_out], refs[n_in + 2 * n + n_out + n_scratch:])
        return own, cm

    def _copies(self, cm, arriving):
        ins, outs, (send, recv, local, _, _) = cm
        x, y, c = _coords()
        me = 2 * x + y
        chips = [(1 - x, y), (x, 1 - y), (1 - x, 1 - y)]
        remote, locals_ = [], []
        ng = len(self.gathers)
        for i in range(self.n):
            if i < ng:
                _, l, halved = self.gathers[i]
                slab = ins[i] if l is None else ins[i].at[l]
                if not arriving:
                    locals_.append(pltpu.make_async_copy(slab, outs[i].at[me], local.at[i]))
            for j, (px, py) in enumerate(chips):
                if i < ng:
                    slot = 2 * px + py if arriving else me
                    src, dst = (slab.at[c], outs[i].at[slot, c]) if halved else (slab, outs[i].at[slot])
                else:
                    src, dst = ins[i].at[2 * px + py], outs[i].at[j]
                remote.append(pltpu.make_async_remote_copy(src, dst, send.at[3 * i + j], recv.at[3 * i + j],
                                                           device_id=(px, py, c), device_id_type=MESH))
        return remote, locals_

    def _handovers(self, cm, arriving):
        _, outs, (_, _, _, send, recv) = cm
        x, y, c = _coords()
        chips = [(1 - x, y), (x, 1 - y), (1 - x, 1 - y)]
        cps = []
        for i, (_, _, halved) in enumerate(self.gathers):
            if halved:
                for j, (px, py) in enumerate(chips):
                    src = outs[i].at[2 * px + py, c]
                    dst = outs[i].at[2 * px + py, 1 - c if arriving else c]
                    cps.append(pltpu.make_async_remote_copy(src, dst, send.at[3 * i + j], recv.at[3 * i + j],
                                                            device_id=(x, y, 1 - c), device_id_type=MESH))
        return cps

    def start_at(self, cond, cm):
        def go():
            remote, locals_ = self._copies(cm, False)
            for cp in locals_ + remote:
                cp.start()

        if self.n:
            go() if cond is True else pl.when(cond)(go)

    def wait_at(self, cond, cm):
        def go():
            for cp in self._copies(cm, True)[0]:
                cp.wait_recv()
            handed = self._handovers(cm, False)
            for cp in handed:
                cp.start()
            for cp in self._handovers(cm, True):
                cp.wait_recv()
            remote, locals_ = self._copies(cm, False)
            for cp in handed + remote:
                cp.wait_send()
            for cp in locals_:
                cp.wait()

        if self.n:
            go() if cond is True else pl.when(cond)(go)


def _swap_sibling(arrs):
    n = len(arrs)

    def body(*refs):
        ins, outs, send, recv = refs[:n], refs[n:2 * n], refs[2 * n], refs[2 * n + 1]
        x, y, c = _coords()
        cps = [pltpu.make_async_remote_copy(ins[i], outs[i], send.at[i], recv.at[i], device_id=(x, y, 1 - c), device_id_type=MESH)
               for i in range(n)]
        for cp in cps:
            cp.start()
        for cp in cps:
            cp.wait_recv()
        for cp in cps:
            cp.wait_send()

    return list(pl.pallas_call(
        body, name="swap_sibling", in_specs=[ANY] * n, out_specs=[ANY] * n,
        out_shape=[jax.ShapeDtypeStruct(a.shape, a.dtype) for a in arrs],
        scratch_shapes=[pltpu.SemaphoreType.DMA((n,)), pltpu.SemaphoreType.DMA((n,))],
        compiler_params=pltpu.CompilerParams(has_side_effects=True),
    )(*arrs))


def _gather_small(gs):
    def body(g_ref, o_ref, send_sems, recv_sems, local_sem):
        x, y, c = _coords()
        me = 4 * x + 2 * y + c
        mine = pltpu.make_async_copy(g_ref, o_ref.at[me], local_sem)
        mine.start()
        sends = []
        for k in range(1, 8):
            px, py, pc = x ^ (k >> 2), y ^ ((k >> 1) & 1), c ^ (k & 1)
            sends.append((pltpu.make_async_remote_copy(g_ref, o_ref.at[me], send_sems.at[k - 1], recv_sems.at[k - 1],
                                                       device_id=(px, py, pc), device_id_type=MESH), 4 * px + 2 * py + pc, k))
        for cp, _, _ in sends:
            cp.start()
        for cp, src, k in sends:
            pltpu.make_async_remote_copy(g_ref, o_ref.at[src], send_sems.at[k - 1], recv_sems.at[k - 1],
                                         device_id=(x, y, c), device_id_type=MESH).wait_recv()
        for cp, _, _ in sends:
            cp.wait_send()
        mine.wait()

    return pl.pallas_call(
        body, name="gather_small", in_specs=[ANY], out_specs=ANY,
        out_shape=jax.ShapeDtypeStruct((8,) + gs.shape, gs.dtype),
        scratch_shapes=[pltpu.SemaphoreType.DMA((7,)), pltpu.SemaphoreType.DMA((7,)), pltpu.SemaphoreType.DMA],
        compiler_params=pltpu.CompilerParams(has_side_effects=True),
    )(gs)


def _sum_slots(own, others, name, tile):
    k, r, c = others.shape

    def body(*refs):
        if own is None:
            o_ref, out_ref = refs
            acc = o_ref[0].astype(F32)
            first = 1
        else:
            own_ref, o_ref, out_ref = refs
            acc = own_ref[...]
            first = 0
        for j in range(first, k):
            acc = acc + o_ref[j].astype(F32)
        out_ref[...] = acc

    row = pl.BlockSpec((tile, c), lambda i: (i, 0))
    specs = ([] if own is None else [row]) + [pl.BlockSpec((k, tile, c), lambda i: (0, i, 0))]
    args = ([] if own is None else [own]) + [others]
    return pl.pallas_call(body, name=name, grid=(r // tile,), in_specs=specs, out_specs=row,
                          out_shape=jax.ShapeDtypeStruct((r, c), F32), compiler_params=_cp("parallel"))(*args)


def _adamw(w, m, v, ga, gb, name, tile, rows_first=False):
    lead = 0 if rows_first else w.ndim - 2
    r, c = w.shape[-2:]

    def body(*refs):
        vals = [ref[0] if lead else ref[...] for ref in refs[:len(refs) - 4]]
        w_, m_, v_, g = vals[0], vals[1], vals[2], vals[3]
        if gb is not None:
            g = g + vals[4]
        nm = ADAM_B1 * m_ + (1.0 - ADAM_B1) * g
        nv = ADAM_B2 * v_ + (1.0 - ADAM_B2) * (g * g)
        d = -ADAM_LR * ((nm / BC1) / (jnp.sqrt(nv / BC2) + ADAM_EPS) + ADAM_WD * w_)
        for ref, val in zip(refs[len(refs) - 4:], (g, d, nm, nv)):
            if lead:
                ref[0] = val
            else:
                ref[...] = val

    if rows_first:
        row = pl.BlockSpec((tile,) + w.shape[1:], lambda i: (i, 0, 0))
        grid = (w.shape[0] // tile,)
    elif lead:
        row = pl.BlockSpec((1, tile, c), lambda l, i: (l, i, 0))
        grid = (w.shape[0], r // tile)
    else:
        row = pl.BlockSpec((tile, c), lambda i: (i, 0))
        grid = (r // tile,)
    args = [w, m, v, ga] + ([] if gb is None else [gb])
    return pl.pallas_call(body, name=name, grid=grid, in_specs=[row] * len(args), out_specs=[row] * 4,
                          out_shape=[jax.ShapeDtypeStruct(w.shape, F32)] * 4,
                          compiler_params=_cp(*(["parallel"] * len(grid))))(*args)


MATS = ("w_in", "w_out", "w_gate", "w_up", "w_down")
CONVS = ("ssd_conv_w", "lru_conv_w")
BIG = MATS + CONVS
TRANSPOSED = ("w_gate", "w_up")
COL_SHARDED = ("ssd_conv_w", "lru_conv_w")
W_IN_SHARD = IN_COLS // 4
W_IN_PAD = 1056
SMALL = ("norm_mix", "ssd_conv_b", "ssd_dt_bias", "ssd_a_log", "ssd_d", "ssd_norm", "lru_conv_b", "lru_wa", "lru_ba",
         "lru_wx", "lru_bx", "lru_lambda", "norm_ffn", "norm_final")
WEIGHTS = ("norm_mix", "w_in", "ssd_conv_w", "ssd_conv_b", "ssd_dt_bias", "ssd_a_log", "ssd_d", "ssd_norm", "lru_conv_w",
           "lru_conv_b", "lru_wa", "lru_ba", "lru_wx", "lru_bx", "lru_lambda", "w_out", "norm_ffn", "w_gate", "w_up",
           "w_down", "norm_final")
ROW_TILE = {"w_in": W_IN_SHARD, "w_out": 128, "w_gate": 352, "w_up": 352, "w_down": 352}
W_IN_ADAM_TILE = 54


def _pack(arrs, width, row_mult, dtype):
    flat = jnp.concatenate([a.reshape(-1).astype(dtype) for a in arrs])
    rows = -(-flat.shape[0] // width)
    rows = -(-rows // row_mult) * row_mult
    flat = jnp.pad(flat, (0, rows * width - flat.shape[0]))
    return flat.reshape(rows, width)


def _unpack(buf, shapes):
    flat = buf.reshape(-1)
    out, off = [], 0
    for shp in shapes:
        n = int(np.prod(shp))
        out.append(flat[off:off + n].reshape(shp))
        off += n
    return out


def _join(name, g4):
    if name in COL_SHARDED:
        return jnp.moveaxis(g4, 0, -2).reshape(g4.shape[1:-1] + (4 * g4.shape[-1],))
    return g4.reshape((4 * g4.shape[1],) + g4.shape[2:])


def _slabs(name, g):
    if name in COL_SHARDED:
        return jnp.moveaxis(g.reshape(g.shape[:-1] + (4, g.shape[-1] // 4)), -2, 0)
    return g.reshape((4, g.shape[0] // 4) + g.shape[1:])


def _w_in_rows(g4):
    def nat(lo, hi):
        out = []
        while lo < hi:
            j = lo // W_IN_SHARD
            stop = min(hi, (j + 1) * W_IN_SHARD)
            out.append((j, lo - j * W_IN_SHARD, stop - lo))
            lo = stop
        return out
    pieces = nat(0, 3072) + nat(3080, IN_COLS) + nat(3072, 3080)

    def body(g_ref, o_ref):
        row = 0
        for j, first, n in pieces:
            o_ref[row:row + n, :] = g_ref[j, first:first + n, :]
            row += n
        o_ref[row:, :] = jnp.zeros((NP - row, o_ref.shape[1]), o_ref.dtype)

    return pl.pallas_call(body, name="w_in_rows", out_shape=jax.ShapeDtypeStruct((NP, g4.shape[-1]), g4.dtype),
                          compiler_params=pltpu.CompilerParams(vmem_limit_bytes=VMEM_LIMIT))(g4)


def _w_in_slabs(gt):
    def kern(n):
        return n if n < 3072 else (C_DT + n - 3072 if n < 3080 else n - 8)
    slabs = []
    for j in range(4):
        lo, hi = j * W_IN_SHARD, (j + 1) * W_IN_SHARD
        cuts = sorted({lo, hi} | {c for c in (3072, 3080) if lo < c < hi})
        slabs.append(jnp.concatenate([gt[kern(a):kern(a) + b - a] for a, b in zip(cuts[:-1], cuts[1:])], axis=0))
    return jnp.stack(slabs, axis=0)


def _block_diag(w):
    eye = jnp.eye(LRU_BLOCKS, dtype=w.dtype)
    return jnp.einsum("ncd,nm->ncmd", w, eye).reshape(LRU_W, LRU_W)


def _block_diag_extract(g):
    g4 = g.reshape(LRU_BLOCKS, 64, LRU_BLOCKS, 64)
    return jnp.stack([g4[n, :, n, :] for n in range(LRU_BLOCKS)], axis=0)


def _lanes128(v):
    return jnp.pad(v, (0, BLK - v.shape[0])).reshape(1, BLK)


def _layer_mixers(x, p, comm=None, h=None):
    if h is None:
        h = _rms_fwd(x, p["norm_mix"], "rms_mix")
    proj = _mm(h, p["w_in_t"], tb=True, tm=1024, tn=1408, tk=1024, name="mm_in")
    att, lse, attb, got = _att_fwd_fused(proj, "att_fwd", comm)
    xconv, dt = _ssd_pre(proj, p["ssd_conv_w"], p["ssd_conv_b"], _lanes128(p["ssd_dt_bias"]), "ssd_pre")
    spar = jnp.concatenate([_lanes128(p["ssd_a_log"]), _lanes128(p["ssd_d"]), jnp.zeros((6, BLK), F32)], axis=0)
    y, states = _ssd_scan(xconv, dt, spar, "ssd_scan")
    ssd = _ssd_post(y, proj, p["ssd_norm"], "ssd_post")
    xc = _lru_conv(proj, p["lru_conv_w"], p["lru_conv_b"], "lru_conv")
    wab = jnp.concatenate([_block_diag(p["lru_wa"]), _block_diag(p["lru_wx"])], axis=1).astype(MXU)
    pre = _mm(xc, wab, tm=1024, tn=1024, tk=512, name="mm_lru")
    lpar = jnp.concatenate([p["lru_ba"].reshape(1, -1), p["lru_bx"].reshape(1, -1), p["lru_lambda"].reshape(1, -1),
                            jnp.zeros((5, LRU_W), F32)], axis=0)
    lru, hs = _lru_scan(pre, xc, proj, lpar, "lru_scan")
    mix = jnp.concatenate([attb, ssd, lru], axis=1)
    saved = dict(x=x, h=h, proj=proj, att=att, lse=lse, xconv=xconv, dt=dt, spar=spar, y=y, states=states, xc=xc, wab=wab,
                 pre=pre, lpar=lpar, hs=hs, mix=mix)
    return mix, saved, got


def _layer_ffn(x, mix, p, saved, comms=(None, None, None), next_norm=None):
    comm_out, comm, comm_down = comms
    x1 = _mm(mix, p["w_out"], add=x, tm=1024, tn=1024, tk=1536, name="mm_out", epi=_epi_rms(p["norm_ffn"]), comm=comm_out)
    (x1, h2), got_out = x1 if comm_out is not None else (x1, [])
    gu = _mm(h2, p["w_gu_t"], tb=True, out_dtype=MXU, tm=1024, tn=1408, tk=1024, name="mm_gu", comm=comm)
    gu, got = gu if comm is not None else (gu, [])
    x2 = _mm(gu, p["w_down"], add=x1, tm=512, tn=1024, tk=D_FF, name="mm_down", comm=comm_down,
             epi=None if next_norm is None else _epi_rms(next_norm),
             a_pro=lambda t: _silu(t[:, :D_FF].astype(F32)) * t[:, D_FF:].astype(F32))
    x2, got_down = x2 if comm_down is not None else (x2, [])
    x2, h_next, act = x2 if next_norm is not None else (x2[0], None, x2[1])
    saved.update(x1=x1, h2=h2, gu=gu, act=act)
    return x2, got_out + got + got_down, h_next


def _layer_bwd(dx2, p, sv, comm_ssd=None, comm_att=None, comm_tail=None):
    g = {}
    _, dgu = _mm(dx2, p["w_down"], tb=True, out_dtype=MXU, tm=512, tn=D_FF, tk=1024, name="mm_d_act", epi=_epi_swiglu_bwd(sv["gu"]))
    g["w_down"], g["w_down@wire"] = _mm(sv["act"], dx2, ta=True, tm=1408, tn=1024, tk=1024, name="mm_g_down", epi=_epi_wire(D_MODEL))
    dx1, gn = _mm(dgu, p["w_gu_t"], tm=1024, tn=1024, tk=1408, name="mm_d_h2", epi=_epi_rms_bwd(sv["x1"], p["norm_ffn"], dx2))
    g["w_gu_t"], g["w_gu_t@wire"] = _mm(dgu, sv["h2"], ta=True, tm=1408, tn=1024, tk=1024, name="mm_g_gu", epi=_epi_wire(D_MODEL))
    g["norm_ffn"] = jnp.sum(gn, axis=0)
    dmix, stats = _mm(dx1, p["w_out"], tb=True, tm=1024, tn=1536, tk=1024, name="mm_d_mix", epi=_epi_att_stats(sv["att"], sv["lse"]))
    g["w_out"], g["w_out@wire"] = _mm(sv["mix"], dx1, ta=True, tm=1536, tn=1024, tk=1024, name="mm_g_out", epi=_epi_wire(D_MODEL))
    proj = sv["proj"]
    dpre, dxc_u, dgl, dlpar = _lru_scan_bwd(sv["pre"], sv["xc"], proj, sv["lpar"], sv["hs"], dmix, "lru_scan_bwd")
    dxc = _mm(dpre, sv["wab"], tb=True, add=dxc_u, tm=1024, tn=512, tk=1024, name="mm_d_xc")
    gwab = _mm(sv["xc"], dpre, ta=True, tm=512, tn=1024, tk=1024, name="mm_g_lru")
    g["lru_wa"], g["lru_wx"] = _block_diag_extract(gwab[:, :LRU_W]), _block_diag_extract(gwab[:, LRU_W:])
    g["lru_ba"], g["lru_bx"], g["lru_lambda"] = dlpar[0], dlpar[1], dlpar[2]
    dxl, gcw, gcb = _lru_conv_bwd(proj, dxc, p["lru_conv_w"], "lru_conv_bwd")
    g["lru_conv_w"], g["lru_conv_b"] = gcw[:CONV_K], jnp.sum(gcb, axis=0)
    dy, dz, gsn = _ssd_post_bwd(sv["y"], proj, p["ssd_norm"], (dmix, SSD_W, 1), "ssd_post_bwd")
    g["ssd_norm"] = jnp.sum(gsn, axis=0)
    dxconv, ddt, dal, ddk, got_ssd = _ssd_scan_bwd(sv["xconv"], sv["dt"], sv["spar"], sv["states"], dy, "ssd_scan_bwd", comm_ssd)
    g["ssd_a_log"], g["ssd_d"] = dal[0, :8], ddk[0, :8]
    dxbc, ddtr, gsw, gsb, gdb = _ssd_pre_bwd(proj, dxconv, ddt, p["ssd_conv_w"], p["ssd_conv_b"],
                                             _lanes128(p["ssd_dt_bias"]), "ssd_pre_bwd")
    g["ssd_conv_w"], g["ssd_conv_b"], g["ssd_dt_bias"] = gsw[:CONV_K], jnp.sum(gsb, axis=0), jnp.sum(gdb, axis=0)[:8]
    dq, dk, dv, got_att = _att_bwd_rev(proj, dmix, stats, "att_bwd", None if comm_att is None else comm_att(g))
    dproj = jnp.concatenate([dq, dk, dv, dz, dxbc, dgl, dxl, ddtr], axis=1)
    g["w_in_t"], g["w_in_t@wire"] = _mm(dproj, sv["h"], ta=True, tm=1408, tn=1024, tk=1024, name="mm_g_in", epi=_epi_wire(D_MODEL))
    res = _mm(dproj, p["w_in_t"], tm=1024, tn=1024, tk=1408, name="mm_d_h", comm=None if comm_tail is None else comm_tail(g),
              epi=_epi_rms_bwd(sv["x"], p["norm_mix"], dx1))
    (dx, gm), got_tail = res if comm_tail is not None else (res, [])
    g["norm_mix"] = jnp.sum(gm, axis=0)
    return dx, g, got_ssd, got_att, got_tail


def _grad_slabs(g, names, suffix=""):
    out = {}
    for n in names:
        if n == "w_in":
            out[n] = _w_in_slabs(g["w_in_t" + suffix])
        elif n == "w_gate":
            out[n] = _slabs(n, g["w_gu_t" + suffix][:D_FF])
        elif n == "w_up":
            out[n] = _slabs(n, g["w_gu_t" + suffix][D_FF:])
        else:
            out[n] = _slabs(n, g[n + suffix])
    return out


def kernel(x, norm_mix, w_in, ssd_conv_w, ssd_conv_b, ssd_dt_bias, ssd_a_log, ssd_d, ssd_norm, lru_conv_w, lru_conv_b, lru_wa, lru_ba, lru_wx, lru_bx, lru_lambda, w_out, norm_ffn, w_gate, w_up, w_down, norm_final, loss_target, m_norm_mix, m_w_in, m_ssd_conv_w, m_ssd_conv_b, m_ssd_dt_bias, m_ssd_a_log, m_ssd_d, m_ssd_norm, m_lru_conv_w, m_lru_conv_b, m_lru_wa, m_lru_ba, m_lru_wx, m_lru_bx, m_lru_lambda, m_w_out, m_norm_ffn, m_w_gate, m_w_up, m_w_down, m_norm_final, v_norm_mix, v_w_in, v_ssd_conv_w, v_ssd_conv_b, v_ssd_dt_bias, v_ssd_a_log, v_ssd_d, v_ssd_norm, v_lru_conv_w, v_lru_conv_b, v_lru_wa, v_lru_ba, v_lru_wx, v_lru_bx, v_lru_lambda, v_w_out, v_norm_ffn, v_w_gate, v_w_up, v_w_down, v_norm_final):
    loc = dict(locals())
    w = {n: loc[n] for n in WEIGHTS}
    m = {n: loc["m_" + n] for n in WEIGHTS}
    v = {n: loc["v_" + n] for n in WEIGHTS}
    for n in TRANSPOSED:
        w[n], m[n], v[n] = [jnp.transpose(t, (0, 2, 1)) for t in (w[n], m[n], v[n])]
    wt_in, mt_in, vt_in = [jnp.transpose(t, (2, 0, 1)) for t in (w["w_in"], m["w_in"], v["w_in"])]

    def halves(a):
        return a.reshape(a.shape[0], 2, a.shape[1] // 2, a.shape[2])

    def unhalve(a):
        return a.reshape(4, 2 * a.shape[2], a.shape[3])

    def joined(name, a):
        return _w_in_rows(unhalve(a)) if name == "w_in" else _join(name, unhalve(a))

    wb = {n: halves(w[n].astype(MXU)) for n in MATS[1:]}
    wb["w_in"] = halves(jnp.pad(jnp.transpose(wt_in.astype(MXU), (1, 0, 2)), ((0, 0), (0, W_IN_PAD - W_IN_SHARD), (0, 0))))
    xs = x[0]
    h0, first = _rms_fwd(xs, norm_mix[0], "rms_mix", _Comm(gathers=[(wb["w_in"], 0, True), (w["ssd_conv_w"], None, False),
                                                                    (w["lru_conv_w"], None, False)]))
    convs = {"ssd_conv_w": _join("ssd_conv_w", first[1]), "lru_conv_w": _join("lru_conv_w", first[2])}
    behind_att = [(n, 0) for n in MATS[1:]]
    behind_ffn = [[], [("w_out", 1), ("w_gate", 1), ("w_up", 1)], [("w_down", 1), ("w_in", 1)]]
    whole = {("w_in", 0): joined("w_in", first[0])}
    params = {}

    def layer_params(l):
        if l not in params:
            p = {n: w[n][l] for n in SMALL if n != "norm_final"}
            p.update(w_in_t=whole["w_in", l], ssd_conv_w=convs["ssd_conv_w"][l], lru_conv_w=convs["lru_conv_w"][l])
            params[l] = p
        if "w_out" not in params[l] and ("w_out", l) in whole:
            params[l].update(w_out=whole["w_out", l], w_down=whole["w_down", l],
                             w_gu_t=jnp.concatenate([whole["w_gate", l], whole["w_up", l]], axis=0))
        return params[l]

    saved = []
    h_in = h0
    for l in range(DEPTH):
        first_layer = l == 0
        mix, sv, got = _layer_mixers(xs, layer_params(l), _Comm(gathers=[(wb[n], k, True) for n, k in behind_att]) if first_layer else None,
                                     h_in)
        whole.update({k: joined(k[0], a) for k, a in zip(behind_att, got)})
        comms = [_Comm(gathers=[(wb[n], k, True) for n, k in part]) if first_layer and part else None for part in behind_ffn]
        xs, got, h_in = _layer_ffn(xs, mix, layer_params(l), sv, comms, norm_mix[l + 1] if l + 1 < DEPTH else None)
        whole.update({k: joined(k[0], a) for k, a in zip([k for part in behind_ffn for k in part], got)})
        saved.append(sv)
    dx, gnf, lsum = _loss_head(xs, norm_final, loss_target[0], "loss_head")
    loss = lax.psum(jnp.sum(lsum), ("x", "y", "c"))

    dx, g1, _, _, _ = _layer_bwd(dx, layer_params(1), saved[1])
    def slabs_of(g, names):
        own = _grad_slabs(g, names)
        sent = _grad_slabs(g, [n for n in names if n in MATS], "@wire")
        sent.update({n: own[n] for n in names if n not in MATS})
        return own, sent

    s1, sent1 = slabs_of(g1, BIG)
    att0 = ("w_gate", "w_up", "w_down", "w_out")
    s0, sent0 = {}, {}

    def add0(g0, names):
        own, sent = slabs_of(g0, names)
        s0.update(own)
        sent0.update(sent)

    ssd1 = ("w_gate", "w_up")
    att1 = tuple(n for n in BIG if n not in ssd1)

    def comm_att(g0):
        add0(g0, att0)
        return _Comm(scatters=[sent1[n] for n in att1] + [sent0[n] for n in att0])

    tail0 = ("w_in",) + CONVS

    def comm_tail(g0):
        add0(g0, tail0)
        return _Comm(scatters=[sent0[n] for n in tail0])

    dx, g0, got_ssd, got_att, got_tail = _layer_bwd(dx, layer_params(0), saved[0], _Comm(scatters=[sent1[n] for n in ssd1]),
                                                    comm_att, comm_tail)
    recv = {(n, 1): a for n, a in zip(ssd1, got_ssd)}
    recv.update({(n, 1): a for n, a in zip(att1, got_att[:len(att1)])})
    recv.update({(n, 0): a for n, a in zip(att0, got_att[len(att1):])})
    recv.update({(n, 0): a for n, a in zip(tail0, got_tail)})

    me = 2 * lax.axis_index("x") + lax.axis_index("y")
    slabs = (s0, s1)
    part = {}
    for n in BIG:
        per_layer = []
        for l in range(DEPTH):
            own = lax.dynamic_index_in_dim(slabs[l][n], me, axis=0, keepdims=False)
            per_layer.append(_sum_slots(own, recv[n, l], "sum_chips_" + n, ROW_TILE.get(n, own.shape[0])))
        part[n] = jnp.stack(per_layer, axis=0)
    sib = dict(zip(BIG, _swap_sibling([part[n] for n in BIG])))
    out_g, out_d, out_m, out_v = {}, {}, {}, {}
    for n in BIG:
        if n == "w_in":
            res = _adamw(wt_in, mt_in, vt_in, jnp.transpose(part[n], (1, 0, 2)), jnp.transpose(sib[n], (1, 0, 2)), "adamw_" + n,
                         W_IN_ADAM_TILE, rows_first=True)
            out_g[n], out_d[n], out_m[n], out_v[n] = [jnp.transpose(t, (1, 2, 0)) for t in res]
            continue
        res = _adamw(w[n], m[n], v[n], part[n], sib[n], "adamw_" + n, ROW_TILE.get(n, w[n].shape[1]))
        out_g[n], out_d[n], out_m[n], out_v[n] = [jnp.transpose(t, (0, 2, 1)) for t in res] if n in TRANSPOSED else res

    gsm = {n: jnp.stack([g0[n], g1[n]], axis=0) for n in SMALL if n != "norm_final"}
    gsm["norm_final"] = jnp.sum(gnf, axis=0)
    small_shapes = [w[n].shape for n in SMALL]
    gs = _pack([gsm[n].reshape(w[n].shape) for n in SMALL], BLK, 8, F32)
    gall = _gather_small(gs)
    gsum = _sum_slots(None, gall, "sum_devices", gs.shape[0])
    ws = _pack([w[n] for n in SMALL], BLK, 8, F32)
    ms = _pack([m[n] for n in SMALL], BLK, 8, F32)
    vs = _pack([v[n] for n in SMALL], BLK, 8, F32)
    gsr, dsr, nms, nvs = _adamw(ws, ms, vs, gsum, None, "adamw_small", gs.shape[0])
    out_g.update(zip(SMALL, _unpack(gsr, small_shapes)))
    out_d.update(zip(SMALL, _unpack(dsr, small_shapes)))
    out_m.update(zip(SMALL, _unpack(nms, small_shapes)))
    out_v.update(zip(SMALL, _unpack(nvs, small_shapes)))

    return (loss, dx[None], *[out_g[n] for n in WEIGHTS], *[out_d[n] for n in WEIGHTS],
            *[out_m[n] for n in WEIGHTS], *[out_v[n] for n in WEIGHTS])
```
